```python
import math
import jax, jax.numpy as jnp
from jax import lax
import numpy as np

D_MODEL = 1024
BATCH = 4
SEQ = 4096
DEPTH = 2

HEAD_DIM = 64
SWA_HEADS = 6
SWA_KV_HEADS = 2
SWA_WINDOW = 128
SWA_BLOCK = 128
SSD_HEADS = 6
SSD_HEAD_DIM = 64
SSD_D_INNER = SSD_HEADS * SSD_HEAD_DIM
SSD_GROUPS = 2
SSD_STATE = 64
SSD_CONV = 5
SSD_CHUNK = 128
DIFF_HEADS = 4
DIFF_QK_DIM = 32
DIFF_V_DIM = 2 * DIFF_QK_DIM
DIFF_Q_BLOCK = 128
N_ALIBI_HEADS = SWA_HEADS + DIFF_HEADS
A_Q_DIM = SWA_HEADS * HEAD_DIM
A_KV_DIM = SWA_KV_HEADS * HEAD_DIM
SSD_BC_DIM = SSD_GROUPS * SSD_STATE
SSD_CONV_DIM = SSD_D_INNER + 2 * SSD_BC_DIM
SSD_DT_DIM = 2 * SSD_HEADS
DIFF_QK_WIDTH = DIFF_HEADS * 2 * DIFF_QK_DIM
DIFF_V_WIDTH = DIFF_HEADS * DIFF_V_DIM
D_IN = A_Q_DIM + 2 * A_KV_DIM + SSD_D_INNER + SSD_CONV_DIM + SSD_DT_DIM + 2 * DIFF_QK_WIDTH + DIFF_V_WIDTH
D_MIX = A_Q_DIM + SSD_D_INNER + DIFF_V_WIDTH
N_EXPERT_GROUPS = 4
EXPERTS_PER_GROUP = 8
N_EXPERTS = N_EXPERT_GROUPS * EXPERTS_PER_GROUP
EXPERT_TOP_K = 2
D_FF_EXPERT = 512
NORM_EPS = 1e-6

kernel_name = 'hymba_style_hybrid_encoder_hmoe'


def rms_norm(x, w):
    xf = x.astype(jnp.float32)
    y = xf * lax.rsqrt(jnp.mean(xf * xf, axis=-1, keepdims=True) + NORM_EPS)
    return (y * w.astype(jnp.float32)).astype(x.dtype)


def alibi_slopes():
    return jnp.exp2(-8.0 * jnp.arange(1, N_ALIBI_HEADS + 1, dtype=jnp.float32) / N_ALIBI_HEADS)


def windowed_gqa_sink(q, k, v, sink, slopes):
    b, s = q.shape[0], q.shape[1]
    nb = s // SWA_BLOCK
    rep = SWA_HEADS // SWA_KV_HEADS
    f32 = jnp.float32
    qb = q.reshape(b, nb, SWA_BLOCK, SWA_KV_HEADS, rep, HEAD_DIM)
    pad = ((0, 0), (SWA_BLOCK, SWA_BLOCK), (0, 0), (0, 0))
    kp = jnp.pad(k, pad).reshape(b, nb + 2, SWA_BLOCK, SWA_KV_HEADS, HEAD_DIM)
    vp = jnp.pad(v, pad).reshape(b, nb + 2, SWA_BLOCK, SWA_KV_HEADS, HEAD_DIM)
    def band(t):
        return jnp.concatenate([t[:, :-2], t[:, 1:-1], t[:, 2:]], axis=2)
    kb, vb = band(kp), band(vp)
    scores = jnp.einsum('bnqgrd,bnkgd->bngrqk', qb, kb).astype(f32) * (HEAD_DIM ** -0.5)
    qi = jnp.arange(SWA_BLOCK)[:, None]
    kj = jnp.arange(3 * SWA_BLOCK)[None, :]
    rel = qi - kj + SWA_BLOCK
    kpos = jnp.arange(nb)[:, None] * SWA_BLOCK - SWA_BLOCK + jnp.arange(3 * SWA_BLOCK)[None, :]
    valid = (jnp.abs(rel) <= SWA_WINDOW)[None] & ((kpos >= 0) & (kpos < s))[:, None, :]
    bias = -slopes.astype(f32).reshape(SWA_KV_HEADS, rep)[:, :, None, None] * jnp.abs(rel).astype(f32)
    scores = jnp.where(valid[None, :, None, None], scores + bias[None, None], -jnp.inf)
    sink_l = sink.astype(f32).reshape(1, 1, SWA_KV_HEADS, rep, 1, 1)
    m = jnp.maximum(jnp.max(scores, axis=-1, keepdims=True), sink_l)
    p = jnp.exp(scores - m)
    denom = jnp.sum(p, axis=-1, keepdims=True) + jnp.exp(sink_l - m)
    probs = (p / denom).astype(v.dtype)
    out = jnp.einsum('bngrqk,bnkgd->bnqgrd', probs, vb)
    return out.reshape(b, s, SWA_HEADS * HEAD_DIM)


def diff_attention(q, k, v, lam_params, subln_w, slopes, lambda_init):
    b, s = q.shape[0], q.shape[1]
    nb = s // DIFF_Q_BLOCK
    f32 = jnp.float32
    lp = lam_params.astype(f32)
    lam = jnp.exp(jnp.sum(lp[0] * lp[1])) - jnp.exp(jnp.sum(lp[2] * lp[3])) + lambda_init
    qb = jnp.moveaxis(q.reshape(b, nb, DIFF_Q_BLOCK, DIFF_HEADS, 2, DIFF_QK_DIM), 1, 0)
    kpos = jnp.arange(s)
    sl = slopes.astype(f32)[None, :, None, None, None]
    def block(args):
        n, qn = args
        sc = jnp.einsum('bqhmd,bkhmd->bhmqk', qn, k).astype(f32) * (DIFF_QK_DIM ** -0.5)
        qpos = n * DIFF_Q_BLOCK + jnp.arange(DIFF_Q_BLOCK)
        dist = jnp.abs(qpos[:, None] - kpos[None, :]).astype(f32)
        a = jax.nn.softmax(sc - sl * dist, axis=-1)
        attn = (a[:, :, 0] - lam * a[:, :, 1]).astype(v.dtype)
        return jnp.einsum('bhqk,bkhd->bqhd', attn, v)
    out = lax.map(block, (jnp.arange(nb), qb))
    out = jnp.moveaxis(out, 0, 1).reshape(b, s, DIFF_HEADS, DIFF_V_DIM)
    out = rms_norm(out, subln_w) * (1.0 - lambda_init)
    return out.reshape(b, s, DIFF_HEADS * DIFF_V_DIM)


def ssd_chunked(x, dt, a, bm, cm):
    b, s, h, p = x.shape
    nc = s // SSD_CHUNK
    rep = h // SSD_GROUPS
    f32 = jnp.float32
    xdt = (x.astype(f32) * dt[..., None]).reshape(b, nc, SSD_CHUNK, h, p)
    da = jnp.cumsum((dt * a).reshape(b, nc, SSD_CHUNK, h), axis=2)
    bh = jnp.repeat(bm.astype(f32), rep, axis=2).reshape(b, nc, SSD_CHUNK, h, SSD_STATE)
    ch = jnp.repeat(cm.astype(f32), rep, axis=2).reshape(b, nc, SSD_CHUNK, h, SSD_STATE)
    lower = jnp.tril(jnp.ones((SSD_CHUNK, SSD_CHUNK), dtype=bool))[None, None, :, :, None]
    seg = da[:, :, :, None, :] - da[:, :, None, :, :]
    decay = jnp.exp(jnp.where(lower, seg, -jnp.inf))
    y_diag = jnp.einsum('bclsh,bcshp->bclhp', jnp.einsum('bclhn,bcshn->bclsh', ch, bh) * decay, xdt)
    to_end = jnp.exp(da[:, :, -1:, :] - da)
    states = jnp.einsum('bclhn,bclhp->bchpn', bh * to_end[..., None], xdt)
    chunk_decay = jnp.exp(da[:, :, -1, :])
    def step(carry, inp):
        st, dec = inp
        return carry * dec[:, :, None, None] + st, carry
    init = jnp.zeros((b, h, p, SSD_STATE), f32)
    _, prev = lax.scan(step, init, (jnp.moveaxis(states, 1, 0), jnp.moveaxis(chunk_decay, 1, 0)))
    prev = jnp.moveaxis(prev, 0, 1)
    y_off = jnp.einsum('bclhn,bchpn->bclhp', ch * jnp.exp(da)[..., None], prev)
    return (y_diag + y_off).reshape(b, s, h, p).astype(x.dtype)


def ssd_mixer(z, xbc, dt_raw, conv_w, conv_b, dt_bias, a_log, d_skip, norm_w):
    b, s = z.shape[0], z.shape[1]
    f32 = jnp.float32
    half = SSD_CONV // 2
    xbc = lax.conv_general_dilated(xbc, conv_w[:, None, :], window_strides=(1,), padding=[(half, half)],
                                   dimension_numbers=('NWC', 'WIO', 'NWC'), feature_group_count=SSD_CONV_DIM)
    xbc = jax.nn.silu(xbc + conv_b)
    xs, bm, cm = jnp.split(xbc, [SSD_D_INNER, SSD_D_INNER + SSD_BC_DIM], axis=-1)
    xs = xs.reshape(b, s, SSD_HEADS, SSD_HEAD_DIM)
    bm = bm.reshape(b, s, SSD_GROUPS, SSD_STATE)
    cm = cm.reshape(b, s, SSD_GROUPS, SSD_STATE)
    dt = jax.nn.softplus(dt_raw.astype(f32).reshape(b, s, 2, SSD_HEADS) + dt_bias.astype(f32))
    a = -jnp.exp(a_log.astype(f32))
    y_fwd = ssd_chunked(xs, dt[:, :, 0], a[0], bm, cm)
    def flip(t):
        return jnp.flip(t, axis=1)
    y_bwd = flip(ssd_chunked(flip(xs), flip(dt[:, :, 1]), a[1], flip(bm), flip(cm)))
    y = y_fwd + y_bwd + d_skip[:, None] * xs
    y = y.reshape(b, s, SSD_D_INNER) * jax.nn.silu(z)
    return rms_norm(y, norm_w)


def hierarchical_moe(h, w_rg, b_rg, w_re, b_re, w_gate, w_up, w_down):
    b, s, d = h.shape
    f32 = jnp.float32
    t = h.reshape(b * s, d)
    g_logits = (t @ w_rg + b_rg).astype(f32)
    g_sel = jnp.argmax(g_logits, axis=-1)
    g_gate = jnp.take_along_axis(jax.nn.softmax(g_logits, axis=-1), g_sel[:, None], axis=-1)
    e_logits = (t @ w_re + b_re).astype(f32).reshape(-1, N_EXPERT_GROUPS, EXPERTS_PER_GROUP)
    e_in = jnp.take_along_axis(e_logits, g_sel[:, None, None], axis=1)[:, 0]
    top_p, top_i = lax.top_k(jax.nn.softmax(e_in, axis=-1), EXPERT_TOP_K)
    combine = g_gate * top_p / jnp.sum(top_p, axis=-1, keepdims=True)
    expert_id = g_sel[:, None] * EXPERTS_PER_GROUP + top_i
    weights = jnp.sum(jax.nn.one_hot(expert_id, N_EXPERTS, dtype=f32) * combine[..., None], axis=1)
    weights = weights.astype(t.dtype)
    out = jnp.zeros_like(t)
    for gi in range(N_EXPERT_GROUPS):
        sl = slice(gi * EXPERTS_PER_GROUP, (gi + 1) * EXPERTS_PER_GROUP)
        hg = jnp.einsum('td,edf->tef', t, w_gate[sl])
        hu = jnp.einsum('td,edf->tef', t, w_up[sl])
        act = jax.nn.silu(hg) * hu * weights[:, sl, None]
        out = out + jnp.einsum('tef,efd->td', act, w_down[sl])
    return out.reshape(b, s, d)


def setup_inputs(seed: int = 0) -> dict:
    key = jax.random.key(seed)
    ks = jax.random.split(key, 24)
    f32 = jnp.float32
    def nrm(k, shape, scale):
        return jax.random.normal(k, shape, f32) * scale
    def gain(k, shape):
        return 1.0 + 0.02 * jax.random.normal(k, shape, f32)
    dt0 = jnp.exp(jax.random.uniform(ks[7], (DEPTH, 2, SSD_HEADS), f32, math.log(1e-3), math.log(1e-1)))
    return {
        'x': nrm(ks[0], (BATCH, SEQ, D_MODEL), 1.0),
        'attn_norm_w': gain(ks[1], (DEPTH, D_MODEL)),
        'w_in': nrm(ks[2], (DEPTH, D_MODEL, D_IN), D_MODEL ** -0.5),
        'swa_sink': nrm(ks[3], (DEPTH, SWA_HEADS), 0.5),
        'ssd_conv_w': nrm(ks[4], (DEPTH, SSD_CONV, SSD_CONV_DIM), SSD_CONV ** -0.5),
        'ssd_conv_b': nrm(ks[5], (DEPTH, SSD_CONV_DIM), 0.02),
        'ssd_dt_bias': dt0 + jnp.log(-jnp.expm1(-dt0)),
        'ssd_a_log': jnp.log(jax.random.uniform(ks[8], (DEPTH, 2, SSD_HEADS), f32, 1.0, 16.0)),
        'ssd_d': gain(ks[9], (DEPTH, SSD_HEADS)),
        'ssd_norm_w': gain(ks[10], (DEPTH, SSD_D_INNER)),
        'diff_lambda': nrm(ks[11], (DEPTH, 4, DIFF_QK_DIM), 0.1),
        'diff_subln_w': gain(ks[12], (DEPTH, DIFF_V_DIM)),
        'w_out': nrm(ks[13], (DEPTH, D_MIX, D_MODEL), D_MIX ** -0.5),
        'ffn_norm_w': gain(ks[14], (DEPTH, D_MODEL)),
        'w_router_group': nrm(ks[15], (DEPTH, D_MODEL, N_EXPERT_GROUPS), D_MODEL ** -0.5),
        'b_router_group': nrm(ks[16], (DEPTH, N_EXPERT_GROUPS), 0.01),
        'w_router_expert': nrm(ks[17], (DEPTH, D_MODEL, N_EXPERTS), D_MODEL ** -0.5),
        'b_router_expert': nrm(ks[18], (DEPTH, N_EXPERTS), 0.01),
        'w_gate': nrm(ks[19], (DEPTH, N_EXPERTS, D_MODEL, D_FF_EXPERT), D_MODEL ** -0.5),
        'w_up': nrm(ks[20], (DEPTH, N_EXPERTS, D_MODEL, D_FF_EXPERT), D_MODEL ** -0.5),
        'w_down': nrm(ks[21], (DEPTH, N_EXPERTS, D_FF_EXPERT, D_MODEL), D_FF_EXPERT ** -0.5),
        'final_norm_w': gain(ks[22], (D_MODEL,)),
    }


def reference(x, attn_norm_w, w_in, swa_sink, ssd_conv_w, ssd_conv_b, ssd_dt_bias, ssd_a_log, ssd_d,
              ssd_norm_w, diff_lambda, diff_subln_w, w_out, ffn_norm_w, w_router_group, b_router_group,
              w_router_expert, b_router_expert, w_gate, w_up, w_down, final_norm_w):
    b, s = x.shape[0], x.shape[1]
    slopes = alibi_slopes()
    swa_slopes, diff_slopes = slopes[:SWA_HEADS], slopes[SWA_HEADS:]
    sizes = [A_Q_DIM, A_KV_DIM, A_KV_DIM, SSD_D_INNER, SSD_CONV_DIM, SSD_DT_DIM,
             DIFF_QK_WIDTH, DIFF_QK_WIDTH, DIFF_V_WIDTH]
    offsets = [int(o) for o in np.cumsum(sizes)[:-1]]
    for l in range(DEPTH):
        h = rms_norm(x, attn_norm_w[l])
        proj = h @ w_in[l]
        a_q, a_k, a_v, b_z, b_xbc, b_dt, c_q, c_k, c_v = jnp.split(proj, offsets, axis=-1)
        y_a = windowed_gqa_sink(a_q.reshape(b, s, SWA_HEADS, HEAD_DIM),
                                a_k.reshape(b, s, SWA_KV_HEADS, HEAD_DIM),
                                a_v.reshape(b, s, SWA_KV_HEADS, HEAD_DIM), swa_sink[l], swa_slopes)
        y_b = ssd_mixer(b_z, b_xbc, b_dt, ssd_conv_w[l], ssd_conv_b[l], ssd_dt_bias[l], ssd_a_log[l],
                        ssd_d[l], ssd_norm_w[l])
        lambda_init = 0.8 - 0.6 * math.exp(-0.3 * l)
        y_c = diff_attention(c_q.reshape(b, s, DIFF_HEADS, 2, DIFF_QK_DIM),
                             c_k.reshape(b, s, DIFF_HEADS, 2, DIFF_QK_DIM),
                             c_v.reshape(b, s, DIFF_HEADS, DIFF_V_DIM),
                             diff_lambda[l], diff_subln_w[l], diff_slopes, lambda_init)
        x = x + jnp.concatenate([y_a, y_b, y_c], axis=-1) @ w_out[l]
        h = rms_norm(x, ffn_norm_w[l])
        x = x + hierarchical_moe(h, w_router_group[l], b_router_group[l], w_router_expert[l],
                                 b_router_expert[l], w_gate[l], w_up[l], w_down[l])
    return rms_norm(x, final_norm_w)
```

```python
import functools
import math

import jax
import jax.numpy as jnp
from jax import lax
from jax.experimental import pallas as pl
from jax.experimental.pallas import tpu as pltpu

F32 = jnp.float32
BF16 = jnp.bfloat16
I32 = jnp.int32

HEAD_DIM = 64
SWA_HEADS = 6
SWA_KV_HEADS = 2
SWA_WINDOW = 128
SSD_HEADS = 6
SSD_HEAD_DIM = 64
SSD_GROUPS = 2
SSD_STATE = 64
SSD_CONV = 5
DIFF_HEADS = 4
DIFF_QK_DIM = 32
DIFF_V_DIM = 64
N_EXPERT_GROUPS = 4
EXPERTS_PER_GROUP = 8
N_EXPERTS = N_EXPERT_GROUPS * EXPERTS_PER_GROUP
NORM_EPS = 1e-6

A_Q_DIM = SWA_HEADS * HEAD_DIM
A_KV_DIM = SWA_KV_HEADS * HEAD_DIM
SSD_D_INNER = SSD_HEADS * SSD_HEAD_DIM
SSD_BC_DIM = SSD_GROUPS * SSD_STATE
SSD_CONV_DIM = SSD_D_INNER + 2 * SSD_BC_DIM
SSD_DT_DIM = 2 * SSD_HEADS
DIFF_QK_WIDTH = DIFF_HEADS * 2 * DIFF_QK_DIM
DIFF_V_WIDTH = DIFF_HEADS * DIFF_V_DIM
N_ALIBI_HEADS = SWA_HEADS + DIFF_HEADS

LANES = 128
SUBLANES = 8
VMEM_LIMIT = 56 * 1024 * 1024
NEG = -1e30

SSD_CHUNK = 128
MOE_ROW_TILE = 256


def _cparams(*sem):
    return pltpu.CompilerParams(dimension_semantics=sem, vmem_limit_bytes=VMEM_LIMIT)


def _rms(x, w):
    return x * lax.rsqrt(jnp.mean(x * x, axis=-1, keepdims=True) + NORM_EPS) * w


def _silu(x):
    return x / (1.0 + jnp.exp(-x))


def _softplus(x):
    return jnp.maximum(x, 0.0) + jnp.log(1.0 + jnp.exp(-jnp.abs(x)))


_C_AQ = 0
_C_AK = _C_AQ + A_Q_DIM
_C_AV = _C_AK + A_KV_DIM
_C_Z = _C_AV + A_KV_DIM
_C_XBC = _C_Z + SSD_D_INNER
_C_CQ = _C_XBC + SSD_CONV_DIM
_C_CV = _C_CQ + DIFF_QK_WIDTH
_C_DT = _C_CV + DIFF_V_WIDTH
_C_END = _C_DT + LANES


def _inproj_kernel(x_ref, nw_ref, w_ref, wkt_ref, aq_ref, ak_ref, av_ref, z_ref, xbc_ref, cq_ref,
                   ckt_ref, cv_ref, dt_ref):
    h = _rms(x_ref[...], nw_ref[...]).astype(BF16)

    def seg(lo, hi):
        return jnp.dot(h, w_ref[:, lo:hi], preferred_element_type=F32)

    aq_ref[...] = (seg(_C_AQ, _C_AK) * (HEAD_DIM ** -0.5)).astype(BF16)
    ak_ref[...] = seg(_C_AK, _C_AV).astype(BF16)
    av_ref[...] = seg(_C_AV, _C_Z).astype(BF16)
    z_ref[...] = seg(_C_Z, _C_XBC)
    xbc_ref[...] = seg(_C_XBC, _C_CQ)
    cq_ref[...] = (seg(_C_CQ, _C_CV) * (DIFF_QK_DIM ** -0.5)).astype(BF16)
    cv_ref[...] = seg(_C_CV, _C_DT).astype(BF16)
    dt_ref[...] = seg(_C_DT, _C_END)
    ckt_ref[...] = lax.dot_general(wkt_ref[...], h, (((1,), (1,)), ((), ())),
                                   preferred_element_type=F32).astype(BF16)


def _inproj(x2, norm_w, w_main, w_ckt, tm):
    t, d = x2.shape
    row = lambda i: (i, 0)
    full = lambda i: (0, 0)
    outs = [
        (A_Q_DIM, BF16), (A_KV_DIM, BF16), (A_KV_DIM, BF16), (SSD_D_INNER, F32), (SSD_CONV_DIM, F32),
        (DIFF_QK_WIDTH, BF16), None, (DIFF_V_WIDTH, BF16), (LANES, F32),
    ]
    out_shape, out_specs = [], []
    for o in outs:
        if o is None:
            out_shape.append(jax.ShapeDtypeStruct((DIFF_QK_WIDTH, t), BF16))
            out_specs.append(pl.BlockSpec((DIFF_QK_WIDTH, tm), lambda i: (0, i)))
        else:
            out_shape.append(jax.ShapeDtypeStruct((t, o[0]), o[1]))
            out_specs.append(pl.BlockSpec((tm, o[0]), row))
    return pl.pallas_call(
        _inproj_kernel,
        grid=(t // tm,),
        in_specs=[pl.BlockSpec((tm, d), row), pl.BlockSpec((1, d), full),
                  pl.BlockSpec(w_main.shape, full), pl.BlockSpec(w_ckt.shape, full)],
        out_specs=out_specs,
        out_shape=out_shape,
        compiler_params=_cparams("parallel"),
        name="inproj",
    )(x2, norm_w, w_main, w_ckt)


def _swa_kernel(sink_ref, slope_ref, q_ref, k_ref, v_ref, o_ref):
    n = pl.program_id(1)
    s_len = k_ref.shape[0]
    blk = q_ref.shape[0]
    band = 3 * blk
    start = pl.multiple_of(jnp.clip((n - 1) * blk, 0, s_len - band), blk)
    kb = k_ref[pl.ds(start, band), :]
    vb = v_ref[pl.ds(start, band), :]
    qpos = n * blk + lax.broadcasted_iota(I32, (blk, band), 0)
    kpos = start + lax.broadcasted_iota(I32, (blk, band), 1)
    dist_i = jnp.abs(qpos - kpos)
    valid = dist_i <= SWA_WINDOW
    dist = dist_i.astype(F32)
    q = q_ref[...]
    rep = SWA_HEADS // SWA_KV_HEADS
    outs = []
    for h in range(SWA_HEADS):
        g = h // rep
        qh = q[:, h * HEAD_DIM:(h + 1) * HEAD_DIM]
        kh = kb[:, g * HEAD_DIM:(g + 1) * HEAD_DIM]
        vh = vb[:, g * HEAD_DIM:(g + 1) * HEAD_DIM]
        s = lax.dot_general(qh, kh, (((1,), (1,)), ((), ())), preferred_element_type=F32)
        s = jnp.where(valid, s - slope_ref[h] * dist, NEG)
        sink = sink_ref[h]
        m = jnp.maximum(jnp.max(s, axis=-1, keepdims=True), sink)
        p = jnp.exp(s - m)
        denom = jnp.sum(p, axis=-1, keepdims=True) + jnp.exp(sink - m)
        probs = (p / denom).astype(BF16)
        outs.append(jnp.dot(probs, vh, preferred_element_type=F32))
    o_ref[...] = jnp.concatenate(outs, axis=-1).astype(o_ref.dtype)


def _swa(aq, ak, av, sink, slopes, batch, seq):
    blk = SWA_WINDOW
    nb = seq // blk
    smem = pl.BlockSpec(memory_space=pltpu.SMEM)
    return pl.pallas_call(
        _swa_kernel,
        grid=(batch, nb),
        in_specs=[smem, smem,
                  pl.BlockSpec((blk, A_Q_DIM), lambda b, n: (b * nb + n, 0)),
                  pl.BlockSpec((seq, A_KV_DIM), lambda b, n: (b, 0)),
                  pl.BlockSpec((seq, A_KV_DIM), lambda b, n: (b, 0))],
        out_specs=pl.BlockSpec((blk, A_Q_DIM), lambda b, n: (b * nb + n, 0)),
        out_shape=jax.ShapeDtypeStruct(aq.shape, BF16),
        compiler_params=_cparams("parallel", "parallel"),
        name="swa",
    )(sink, slopes, aq, ak, av)


def _diff_kernel(slope_ref, q_ref, kt_ref, v_ref, lam_ref, sw_ref, o_ref, qp_ref, m_ref, l_ref, acc_ref,
                 *, lambda_init):
    i = pl.program_id(1)
    j = pl.program_id(2)
    nk = pl.num_programs(2)
    tq = q_ref.shape[0]
    tk = v_ref.shape[0]
    hw = 2 * DIFF_QK_DIM

    @pl.when(j == 0)
    def _():
        m_ref[...] = jnp.full(m_ref.shape, NEG, F32)
        l_ref[...] = jnp.zeros(l_ref.shape, F32)
        acc_ref[...] = jnp.zeros(acc_ref.shape, F32)
        q = q_ref[...]
        lane = lax.broadcasted_iota(I32, (tq, hw), 1)
        for h in range(DIFF_HEADS):
            qh = q[:, h * hw:(h + 1) * hw]
            qp_ref[h, 0:tq, :] = jnp.where(lane < DIFF_QK_DIM, qh, jnp.zeros_like(qh))
            qp_ref[h, tq:2 * tq, :] = jnp.where(lane >= DIFF_QK_DIM, qh, jnp.zeros_like(qh))

    kt = kt_ref[...]
    v = v_ref[...]
    qpos = i * tq + lax.broadcasted_iota(I32, (tq, tk), 0)
    kpos = j * tk + lax.broadcasted_iota(I32, (tq, tk), 1)
    dist = jnp.abs(qpos - kpos).astype(F32)
    for h in range(DIFF_HEADS):
        s = jnp.dot(qp_ref[h], kt[h * hw:(h + 1) * hw, :], preferred_element_type=F32)
        bias = slope_ref[h] * dist
        s = s - jnp.concatenate([bias, bias], axis=0)
        m_old = m_ref[h]
        m_new = jnp.maximum(m_old, jnp.max(s, axis=-1, keepdims=True))
        alpha = jnp.exp(m_old - m_new)
        p = jnp.exp(s - m_new)
        l_ref[h] = alpha * l_ref[h] + jnp.sum(p, axis=-1, keepdims=True)
        pv = jnp.dot(p.astype(BF16), v[:, h * DIFF_V_DIM:(h + 1) * DIFF_V_DIM], preferred_element_type=F32)
        acc_ref[h] = alpha * acc_ref[h] + pv
        m_ref[h] = m_new

    @pl.when(j == nk - 1)
    def _():
        lp = lam_ref[...]
        lam = (jnp.exp(jnp.sum(lp[0:1] * lp[1:2], axis=-1, keepdims=True))
               - jnp.exp(jnp.sum(lp[2:3] * lp[3:4], axis=-1, keepdims=True)) + lambda_init)
        outs = []
        for h in range(DIFF_HEADS):
            o = acc_ref[h] / l_ref[h]
            o = o[0:tq] - lam * o[tq:2 * tq]
            outs.append(_rms(o, sw_ref[...]) * (1.0 - lambda_init))
        o_ref[...] = jnp.concatenate(outs, axis=-1).astype(o_ref.dtype)


def _diff(cq, ckt, cv, slopes, lam_params, subln_w, lambda_init, batch, seq, tq, tk):
    nq, nk = seq // tq, seq // tk
    smem = pl.BlockSpec(memory_space=pltpu.SMEM)
    return pl.pallas_call(
        functools.partial(_diff_kernel, lambda_init=lambda_init),
        grid=(batch, nq, nk),
        in_specs=[smem,
                  pl.BlockSpec((tq, DIFF_QK_WIDTH), lambda b, i, j: (b * nq + i, 0)),
                  pl.BlockSpec((DIFF_QK_WIDTH, tk), lambda b, i, j: (0, b * nk + j)),
                  pl.BlockSpec((tk, DIFF_V_WIDTH), lambda b, i, j: (b * nk + j, 0)),
                  pl.BlockSpec(lam_params.shape, lambda b, i, j: (0, 0)),
                  pl.BlockSpec(subln_w.shape, lambda b, i, j: (0, 0))],
        out_specs=pl.BlockSpec((tq, DIFF_V_WIDTH), lambda b, i, j: (b * nq + i, 0)),
        out_shape=jax.ShapeDtypeStruct(cv.shape, BF16),
        scratch_shapes=[pltpu.VMEM((DIFF_HEADS, 2 * tq, 2 * DIFF_QK_DIM), BF16),
                        pltpu.VMEM((DIFF_HEADS, 2 * tq, 1), F32),
                        pltpu.VMEM((DIFF_HEADS, 2 * tq, 1), F32),
                        pltpu.VMEM((DIFF_HEADS, 2 * tq, DIFF_V_DIM), F32)],
        compiler_params=_cparams("parallel", "parallel", "arbitrary"),
        name="diffattn",
    )(slopes, cq, ckt, cv, lam_params, subln_w)


def _ssd_kernel(xbc_ref, prev_ref, next_ref, dt_ref, cw_ref, cb_ref, dtb_ref, alog_ref, dsk_ref, y_ref,
                state_ref):
    d = pl.program_id(1)
    c = pl.program_id(2)
    nc = pl.num_programs(2)
    q = xbc_ref.shape[0]
    cidx = jnp.where(d == 0, c, nc - 1 - c)
    fwd = d == 0

    @pl.when(c == 0)
    def _():
        state_ref[...] = jnp.zeros(state_ref.shape, F32)

    x = xbc_ref[...]
    prev = jnp.where(cidx > 0, prev_ref[...], 0.0)
    nxt = jnp.where(cidx < nc - 1, next_ref[...], 0.0)
    ext = jnp.concatenate([prev, x, nxt], axis=0)
    half = SSD_CONV // 2
    conv = cb_ref[...]
    for k in range(SSD_CONV):
        off = SUBLANES - half + k
        conv = conv + cw_ref[k:k + 1, :] * ext[off:off + q, :]
    u = _silu(conv)
    xs = u[:, :SSD_D_INNER]
    bmt = jnp.transpose(u[:, SSD_D_INNER:SSD_D_INNER + SSD_BC_DIM])
    cm = u[:, SSD_D_INNER + SSD_BC_DIM:]

    dt_all = _softplus(dt_ref[...] + dtb_ref[...])
    dta_all = dt_all * -jnp.exp(alog_ref[...])
    row = lax.broadcasted_iota(I32, (q, q), 0)
    col = lax.broadcasted_iota(I32, (q, q), 1)
    keep = (row - col) * jnp.where(fwd, 1, -1) >= 0
    tri = keep.astype(F32)
    da_all = jnp.dot(tri, dta_all, precision=lax.Precision.HIGHEST,
                     preferred_element_type=F32)
    da_all_t = jnp.transpose(da_all)
    tot_all = jnp.sum(dta_all, axis=0, keepdims=True)

    def pick_col(a, h):
        return jnp.where(fwd, a[:, h:h + 1], a[:, SSD_HEADS + h:SSD_HEADS + h + 1])

    def pick_row(a, h):
        return jnp.where(fwd, a[h:h + 1, :], a[SSD_HEADS + h:SSD_HEADS + h + 1, :])

    rep = SSD_HEADS // SSD_GROUPS
    g_mats = []
    for g in range(SSD_GROUPS):
        g_mats.append(jnp.dot(cm[:, g * SSD_STATE:(g + 1) * SSD_STATE].astype(BF16),
                              bmt[g * SSD_STATE:(g + 1) * SSD_STATE, :].astype(BF16),
                              preferred_element_type=F32))
    ys = []
    for h in range(SSD_HEADS):
        g = h // rep
        dac = pick_col(da_all, h)
        dar = pick_row(da_all_t, h)
        tot = pick_col(tot_all, h)
        decay = jnp.exp(jnp.where(keep, dac - dar, NEG))
        xh = xs[:, h * SSD_HEAD_DIM:(h + 1) * SSD_HEAD_DIM]
        xdt = (xh * pick_col(dt_all, h)).astype(BF16)
        y = jnp.dot((g_mats[g] * decay).astype(BF16), xdt, preferred_element_type=F32)
        st = state_ref[h]
        c_in = (cm[:, g * SSD_STATE:(g + 1) * SSD_STATE] * jnp.exp(dac)).astype(BF16)
        y = y + jnp.dot(c_in, st.astype(BF16), preferred_element_type=F32)
        to_end = jnp.exp(tot - dar)
        b_out = (bmt[g * SSD_STATE:(g + 1) * SSD_STATE, :] * to_end).astype(BF16)
        state_ref[h] = st * jnp.exp(tot) + jnp.dot(b_out, xdt, preferred_element_type=F32)
        y = y + jnp.where(fwd, dsk_ref[:, h:h + 1], 0.0) * xh
        ys.append(y)
    y_ref[...] = jnp.concatenate(ys, axis=-1)


def _ssd(xbc, dt_raw, conv_w, conv_b, dt_bias, a_log, d_skip, batch, seq):
    q = SSD_CHUNK
    nc = seq // q
    t = batch * seq
    hb = q // SUBLANES
    n_hblk = t // SUBLANES

    def cidx(d, c):
        return jnp.where(d == 0, c, nc - 1 - c)

    full = lambda b, d, c: (0, 0)
    return pl.pallas_call(
        _ssd_kernel,
        grid=(batch, 2, nc),
        in_specs=[pl.BlockSpec((q, SSD_CONV_DIM), lambda b, d, c: (b * nc + cidx(d, c), 0)),
                  pl.BlockSpec((SUBLANES, SSD_CONV_DIM),
                               lambda b, d, c: (jnp.maximum((b * nc + cidx(d, c)) * hb - 1, 0), 0)),
                  pl.BlockSpec((SUBLANES, SSD_CONV_DIM),
                               lambda b, d, c: (jnp.minimum((b * nc + cidx(d, c) + 1) * hb, n_hblk - 1), 0)),
                  pl.BlockSpec((q, LANES), lambda b, d, c: (b * nc + cidx(d, c), 0)),
                  pl.BlockSpec(conv_w.shape, full), pl.BlockSpec(conv_b.shape, full),
                  pl.BlockSpec(dt_bias.shape, full), pl.BlockSpec(a_log.shape, full),
                  pl.BlockSpec(d_skip.shape, full)],
        out_specs=pl.BlockSpec((None, q, SSD_D_INNER), lambda b, d, c: (d, b * nc + cidx(d, c), 0)),
        out_shape=jax.ShapeDtypeStruct((2, t, SSD_D_INNER), F32),
        scratch_shapes=[pltpu.VMEM((SSD_HEADS, SSD_STATE, SSD_HEAD_DIM), F32)],
        compiler_params=_cparams("parallel", "arbitrary", "arbitrary"),
        name="ssd",
    )(xbc, xbc, xbc, dt_raw, conv_w, conv_b, dt_bias, a_log, d_skip)


def _outproj_kernel(x_ref, ya_ref, ys_ref, z_ref, snw_ref, yc_ref, wo_ref, fnw_ref, wrt_ref, br_ref,
                    xn_ref, ri_ref, rf_ref, cnt_ref, tri_ref, carry_ref):
    step = pl.program_id(0)
    tm = x_ref.shape[0]

    @pl.when(step == 0)
    def _():
        carry_ref[...] = jnp.zeros(carry_ref.shape, F32)
        r = lax.broadcasted_iota(I32, (tm, tm), 0)
        cc = lax.broadcasted_iota(I32, (tm, tm), 1)
        tri_ref[...] = (r <= cc).astype(BF16)

    y = (ys_ref[0] + ys_ref[1]) * _silu(z_ref[...])
    yb = _rms(y, snw_ref[...]).astype(BF16)
    acc = jnp.dot(ya_ref[...], wo_ref[0:A_Q_DIM, :], preferred_element_type=F32)
    acc = acc + jnp.dot(yb, wo_ref[A_Q_DIM:A_Q_DIM + SSD_D_INNER, :], preferred_element_type=F32)
    acc = acc + jnp.dot(yc_ref[...], wo_ref[A_Q_DIM + SSD_D_INNER:, :], preferred_element_type=F32)
    xn = x_ref[...] + acc
    xn_ref[...] = xn

    h = _rms(xn, fnw_ref[...])
    logits = lax.dot_general(wrt_ref[...], h, (((1,), (1,)), ((), ())), precision=lax.Precision.HIGHEST,
                             preferred_element_type=F32) + br_ref[...]
    ne, epg, ng = N_EXPERTS, EXPERTS_PER_GROUP, N_EXPERT_GROUPS
    gl = logits[ne:ne + ng, :]
    gmax = jnp.max(gl, axis=0, keepdims=True)
    g_sel = jnp.full((1, tm), float(ng - 1), F32)
    for g in range(ng - 2, -1, -1):
        g_sel = jnp.where(gl[g:g + 1, :] == gmax, float(g), g_sel)
    g_gate = 1.0 / jnp.sum(jnp.exp(gl - gmax), axis=0, keepdims=True)
    e_in = logits[0:epg, :]
    for g in range(1, ng):
        e_in = jnp.where(g_sel == float(g), logits[g * epg:(g + 1) * epg, :], e_in)
    sub = lax.broadcasted_iota(I32, (epg, tm), 0).astype(F32)
    m1 = jnp.max(e_in, axis=0, keepdims=True)
    i1 = jnp.min(jnp.where(e_in == m1, sub, float(epg)), axis=0, keepdims=True)
    rest = jnp.where(sub == i1, NEG, e_in)
    m2 = jnp.max(rest, axis=0, keepdims=True)
    i2 = jnp.min(jnp.where(rest == m2, sub, float(epg)), axis=0, keepdims=True)
    r = jnp.exp(m2 - m1)
    c1 = g_gate / (1.0 + r)
    c2 = g_gate * r / (1.0 + r)
    e1 = (g_sel * epg + i1).astype(I32)
    e2 = (g_sel * epg + i2).astype(I32)

    erow = lax.broadcasted_iota(I32, (ne, tm), 0)
    hit1 = erow == e1
    hit2 = erow == e2
    oh = jnp.where(hit1 | hit2, 1.0, 0.0)
    incl = jnp.dot(oh.astype(BF16), tri_ref[...], preferred_element_type=F32)
    before = incl - oh + carry_ref[:, 0:1]
    rank1 = jnp.sum(jnp.where(hit1, before, 0.0), axis=0, keepdims=True)
    rank2 = jnp.sum(jnp.where(hit2, before, 0.0), axis=0, keepdims=True)
    carry_ref[...] = carry_ref[...] + jnp.sum(oh, axis=1, keepdims=True)
    cnt_ref[...] = carry_ref[...]
    zi = jnp.zeros((1, tm), I32)
    ri_ref[...] = jnp.concatenate([e1, e2, rank1.astype(I32), rank2.astype(I32), zi, zi, zi, zi], axis=0)
    zf = jnp.zeros((1, tm), F32)
    rf_ref[...] = jnp.concatenate([c1, c2, zf, zf, zf, zf, zf, zf], axis=0)


def _outproj(x2, ya, yssd, z, ssd_norm_w, yc, w_out, ffn_norm_w, wrt, br, tm):
    t, d = x2.shape
    row = lambda i: (i, 0)
    full = lambda i: (0, 0)
    return pl.pallas_call(
        _outproj_kernel,
        grid=(t // tm,),
        in_specs=[pl.BlockSpec((tm, d), row),
                  pl.BlockSpec((tm, A_Q_DIM), row),
                  pl.BlockSpec((2, tm, SSD_D_INNER), lambda i: (0, i, 0)),
                  pl.BlockSpec((tm, SSD_D_INNER), row),
                  pl.BlockSpec(ssd_norm_w.shape, full),
                  pl.BlockSpec((tm, DIFF_V_WIDTH), row),
                  pl.BlockSpec(w_out.shape, full),
                  pl.BlockSpec(ffn_norm_w.shape, full),
                  pl.BlockSpec(wrt.shape, full),
                  pl.BlockSpec(br.shape, full)],
        out_specs=[pl.BlockSpec((tm, d), row),
                   pl.BlockSpec((SUBLANES, tm), lambda i: (0, i)),
                   pl.BlockSpec((SUBLANES, tm), lambda i: (0, i)),
                   pl.BlockSpec((N_EXPERTS, LANES), full)],
        out_shape=[jax.ShapeDtypeStruct((t, d), F32),
                   jax.ShapeDtypeStruct((SUBLANES, t), I32),
                   jax.ShapeDtypeStruct((SUBLANES, t), F32),
                   jax.ShapeDtypeStruct((N_EXPERTS, LANES), F32)],
        scratch_shapes=[pltpu.VMEM((tm, tm), BF16), pltpu.VMEM((N_EXPERTS, LANES), F32)],
        compiler_params=_cparams("arbitrary"),
        name="outproj_router",
    )(x2, ya, yssd, z, ssd_norm_w, yc, w_out, ffn_norm_w, wrt, br)


def _moe_kernel(texp_ref, nused_ref, slot1_ref, slot2_ref, x_hbm, fnw_ref, wg_ref, wu_ref, wd_ref, y_ref,
                src_ref, xbuf, sem):
    i = pl.program_id(0)
    tr = xbuf.shape[1]
    n_tok = slot1_ref.shape[0]
    n_rows = src_ref.shape[0]
    n_used = nused_ref[0]

    def start_gather(tile, slot):
        base = tile * tr

        def body(r, carry):
            tok = src_ref[base + r]
            pltpu.make_async_copy(x_hbm.at[pl.ds(tok, 1)], xbuf.at[slot, pl.ds(r, 1)], sem.at[slot]).start()
            return carry

        lax.fori_loop(0, tr, body, 0, unroll=8)

    @pl.when(i == 0)
    def _():
        def zero(r, carry):
            src_ref[r] = 0
            return carry

        lax.fori_loop(0, n_rows, zero, 0, unroll=8)

        def fill(tok, carry):
            src_ref[slot1_ref[tok]] = tok
            src_ref[slot2_ref[tok]] = tok
            return carry

        lax.fori_loop(0, n_tok, fill, 0, unroll=8)
        start_gather(0, 0)

    @pl.when(i < n_used)
    def _():
        slot = i % 2

        @pl.when(i + 1 < n_used)
        def _():
            start_gather(i + 1, 1 - slot)

        pltpu.make_async_copy(x_hbm.at[pl.ds(0, tr)], xbuf.at[slot], sem.at[slot]).wait()
        h = _rms(xbuf[slot], fnw_ref[...]).astype(BF16)
        hg = jnp.dot(h, wg_ref[...], preferred_element_type=F32)
        hu = jnp.dot(h, wu_ref[...], preferred_element_type=F32)
        act = (_silu(hg) * hu).astype(BF16)
        y_ref[...] = jnp.dot(act, wd_ref[...], preferred_element_type=F32)

    @pl.when(i >= n_used)
    def _():
        y_ref[...] = jnp.zeros(y_ref.shape, F32)


def _moe(xn, ffn_norm_w, w_gate, w_up, w_down, tile_expert, n_used, slot1, slot2, n_tiles):
    t, d = xn.shape
    f = w_gate.shape[-1]
    tr = MOE_ROW_TILE

    def clamp(i, nu):
        return jnp.minimum(i, nu[0] - 1)

    grid_spec = pltpu.PrefetchScalarGridSpec(
        num_scalar_prefetch=4,
        grid=(n_tiles,),
        in_specs=[pl.BlockSpec(memory_space=pl.ANY),
                  pl.BlockSpec(ffn_norm_w.shape, lambda i, te, nu, s1, s2: (0, 0)),
                  pl.BlockSpec((None, d, f), lambda i, te, nu, s1, s2: (te[clamp(i, nu)], 0, 0)),
                  pl.BlockSpec((None, d, f), lambda i, te, nu, s1, s2: (te[clamp(i, nu)], 0, 0)),
                  pl.BlockSpec((None, f, d), lambda i, te, nu, s1, s2: (te[clamp(i, nu)], 0, 0))],
        out_specs=pl.BlockSpec((tr, d), lambda i, te, nu, s1, s2: (i, 0)),
        scratch_shapes=[pltpu.SMEM((n_tiles * tr,), I32),
                        pltpu.VMEM((2, tr, d), F32),
                        pltpu.SemaphoreType.DMA((2,))],
    )
    return pl.pallas_call(
        _moe_kernel,
        grid_spec=grid_spec,
        out_shape=jax.ShapeDtypeStruct((n_tiles * tr, d), F32),
        compiler_params=_cparams("arbitrary"),
        name="moe_experts",
    )(tile_expert, n_used, slot1, slot2, xn, ffn_norm_w, w_gate, w_up, w_down)


def _combine_kernel(slot1_ref, slot2_ref, x_ref, cw_ref, nw_ref, y_hbm, o_ref, ybuf, sem, *, final_norm):
    i = pl.program_id(0)
    n = pl.num_programs(0)
    tm = x_ref.shape[0]

    def start_gather(tile, slot):
        base = tile * tm

        def body(r, carry):
            pltpu.make_async_copy(y_hbm.at[pl.ds(slot1_ref[base + r], 1)], ybuf.at[slot, 0, pl.ds(r, 1)],
                                  sem.at[slot]).start()
            pltpu.make_async_copy(y_hbm.at[pl.ds(slot2_ref[base + r], 1)], ybuf.at[slot, 1, pl.ds(r, 1)],
                                  sem.at[slot]).start()
            return carry

        lax.fori_loop(0, tm, body, 0, unroll=8)

    @pl.when(i == 0)
    def _():
        start_gather(0, 0)

    slot = i % 2

    @pl.when(i + 1 < n)
    def _():
        start_gather(i + 1, 1 - slot)

    pltpu.make_async_copy(y_hbm.at[pl.ds(0, tm)], ybuf.at[slot, 0], sem.at[slot]).wait()
    pltpu.make_async_copy(y_hbm.at[pl.ds(0, tm)], ybuf.at[slot, 1], sem.at[slot]).wait()
    cw = cw_ref[...]
    out = x_ref[...] + cw[:, 0:1] * ybuf[slot, 0] + cw[:, 1:2] * ybuf[slot, 1]
    if final_norm:
        out = _rms(out, nw_ref[...])
    o_ref[...] = out


def _combine(xn, cw, norm_w, y_sorted, slot1, slot2, tm, final_norm):
    t, d = xn.shape
    grid_spec = pltpu.PrefetchScalarGridSpec(
        num_scalar_prefetch=2,
        grid=(t // tm,),
        in_specs=[pl.BlockSpec((tm, d), lambda i, s1, s2: (i, 0)),
                  pl.BlockSpec((tm, cw.shape[1]), lambda i, s1, s2: (i, 0)),
                  pl.BlockSpec(norm_w.shape, lambda i, s1, s2: (0, 0)),
                  pl.BlockSpec(memory_space=pl.ANY)],
        out_specs=pl.BlockSpec((tm, d), lambda i, s1, s2: (i, 0)),
        scratch_shapes=[pltpu.VMEM((2, 2, tm, d), F32), pltpu.SemaphoreType.DMA((2,))],
    )
    return pl.pallas_call(
        functools.partial(_combine_kernel, final_norm=final_norm),
        grid_spec=grid_spec,
        out_shape=jax.ShapeDtypeStruct((t, d), F32),
        compiler_params=_cparams("arbitrary"),
        name="moe_combine",
    )(slot1, slot2, xn, cw, norm_w, y_sorted)


def _pad_lanes(v):
    v = v.reshape(1, -1).astype(F32)
    return jnp.pad(v, ((0, 0), (0, LANES - v.shape[1])))


def kernel(x, attn_norm_w, w_in, swa_sink, ssd_conv_w, ssd_conv_b, ssd_dt_bias, ssd_a_log, ssd_d, ssd_norm_w,
           diff_lambda, diff_subln_w, w_out, ffn_norm_w, w_router_group, b_router_group, w_router_expert,
           b_router_expert, w_gate, w_up, w_down, final_norm_w):
    return _forward(x, attn_norm_w, w_in, swa_sink, ssd_conv_w, ssd_conv_b, ssd_dt_bias, ssd_a_log, ssd_d,
                    ssd_norm_w, diff_lambda, diff_subln_w, w_out, ffn_norm_w, w_router_group, b_router_group,
                    w_router_expert, b_router_expert, w_gate, w_up, w_down, final_norm_w)


def _forward(x, attn_norm_w, w_in, swa_sink, ssd_conv_w, ssd_conv_b, ssd_dt_bias, ssd_a_log, ssd_d, ssd_norm_w,
             diff_lambda, diff_subln_w, w_out, ffn_norm_w, w_router_group, b_router_group, w_router_expert,
             b_router_expert, w_gate, w_up, w_down, final_norm_w, tm=512, tq=512, tk=512, tmc=256):
    batch, seq, d = x.shape
    depth = w_in.shape[0]
    t = batch * seq
    tr = MOE_ROW_TILE
    n_tiles = (2 * t) // tr + N_EXPERTS
    slopes = jnp.exp2(-8.0 * jnp.arange(1, N_ALIBI_HEADS + 1, dtype=F32) / N_ALIBI_HEADS)
    swa_slopes, diff_slopes = slopes[:SWA_HEADS], slopes[SWA_HEADS:]

    sizes = [A_Q_DIM, A_KV_DIM, A_KV_DIM, SSD_D_INNER, SSD_CONV_DIM, SSD_DT_DIM, DIFF_QK_WIDTH, DIFF_QK_WIDTH,
             DIFF_V_WIDTH]
    offs = [0]
    for s in sizes:
        offs.append(offs[-1] + s)
    o_dt, o_cq, o_ck, o_cv = offs[5], offs[6], offs[7], offs[8]

    x2 = x.reshape(t, d)
    for l in range(depth):
        w = w_in[l]
        w_main = jnp.concatenate(
            [w[:, :o_dt], w[:, o_cq:o_ck], w[:, o_cv:], w[:, o_dt:o_cq],
             jnp.zeros((d, LANES - SSD_DT_DIM), w.dtype)], axis=1).astype(BF16)
        w_ckt = w[:, o_ck:o_cv].T.astype(BF16)
        aq, ak, av, z, xbc, cq, ckt, cv, dt_raw = _inproj(x2, attn_norm_w[l].reshape(1, d), w_main, w_ckt, tm)

        ya = _swa(aq, ak, av, swa_sink[l].astype(F32), swa_slopes, batch, seq)
        lambda_init = 0.8 - 0.6 * math.exp(-0.3 * l)
        yc = _diff(cq, ckt, cv, diff_slopes, diff_lambda[l].astype(F32),
                   diff_subln_w[l].reshape(1, DIFF_V_DIM).astype(F32), lambda_init, batch, seq, tq, tk)
        yssd = _ssd(xbc, dt_raw, ssd_conv_w[l].astype(F32), ssd_conv_b[l].reshape(1, -1).astype(F32),
                    _pad_lanes(ssd_dt_bias[l]), _pad_lanes(ssd_a_log[l]), _pad_lanes(ssd_d[l]), batch, seq)

        wrt = jnp.concatenate([w_router_expert[l], w_router_group[l],
                               jnp.zeros((d, SUBLANES - N_EXPERT_GROUPS), F32)], axis=1).T.astype(F32)
        br = jnp.concatenate([b_router_expert[l], b_router_group[l],
                              jnp.zeros((SUBLANES - N_EXPERT_GROUPS,), F32)]).reshape(-1, 1).astype(F32)
        xn, ri, rf, cnt = _outproj(x2, ya, yssd, z, ssd_norm_w[l].reshape(1, -1), yc, w_out[l].astype(BF16),
                                   ffn_norm_w[l].reshape(1, d), wrt, br, tm)

        counts = cnt[:, 0].astype(I32)
        padded = ((counts + tr - 1) // tr) * tr
        ends = jnp.cumsum(padded)
        starts = ends - padded
        slot1 = starts[ri[0]] + ri[2]
        slot2 = starts[ri[1]] + ri[3]
        tile_start = jnp.arange(n_tiles, dtype=I32) * tr
        tile_expert = jnp.minimum(jnp.sum(ends[None, :] <= tile_start[:, None], axis=1), N_EXPERTS - 1).astype(I32)
        n_used = (ends[-1] // tr).astype(I32).reshape(1)

        y_sorted = _moe(xn, ffn_norm_w[l].reshape(1, d), w_gate[l].astype(BF16), w_up[l].astype(BF16),
                        w_down[l].astype(BF16), tile_expert, n_used, slot1, slot2, n_tiles)
        last = l == depth - 1
        x2 = _combine(xn, rf.T, final_norm_w.reshape(1, d), y_sorted, slot1, slot2, tmc, last)
    return x2.reshape(batch, seq, d)
```

```python
import functools
import math

import jax
import jax.numpy as jnp
from jax import lax
from jax.experimental import pallas as pl
from jax.experimental.pallas import tpu as pltpu

F32 = jnp.float32
BF16 = jnp.bfloat16
I32 = jnp.int32

HEAD_DIM = 64
SWA_HEADS = 6
SWA_KV_HEADS = 2
SWA_WINDOW = 128
SSD_HEADS = 6
SSD_HEAD_DIM = 64
SSD_GROUPS = 2
SSD_STATE = 64
SSD_CONV = 5
DIFF_HEADS = 4
DIFF_QK_DIM = 32
DIFF_V_DIM = 64
N_EXPERT_GROUPS = 4
EXPERTS_PER_GROUP = 8
N_EXPERTS = N_EXPERT_GROUPS * EXPERTS_PER_GROUP
NORM_EPS = 1e-6

A_Q_DIM = SWA_HEADS * HEAD_DIM
A_KV_DIM = SWA_KV_HEADS * HEAD_DIM
SSD_D_INNER = SSD_HEADS * SSD_HEAD_DIM
SSD_BC_DIM = SSD_GROUPS * SSD_STATE
SSD_CONV_DIM = SSD_D_INNER + 2 * SSD_BC_DIM
SSD_DT_DIM = 2 * SSD_HEADS
DIFF_QK_WIDTH = DIFF_HEADS * 2 * DIFF_QK_DIM
DIFF_V_WIDTH = DIFF_HEADS * DIFF_V_DIM
N_ALIBI_HEADS = SWA_HEADS + DIFF_HEADS

LANES = 128
SUBLANES = 8
VMEM_LIMIT = 56 * 1024 * 1024
NEG = -1e30

SSD_CHUNK = 128
MOE_ROW_TILE = 256


def _cparams(*sem):
    return pltpu.CompilerParams(dimension_semantics=sem, vmem_limit_bytes=VMEM_LIMIT)


def _rms(x, w):
    return x * lax.rsqrt(jnp.mean(x * x, axis=-1, keepdims=True) + NORM_EPS) * w


def _silu(x):
    return x / (1.0 + jnp.exp(-x))


def _softplus(x):
    return jnp.maximum(x, 0.0) + jnp.log(1.0 + jnp.exp(-jnp.abs(x)))


_C_AQ = 0
_C_AK = _C_AQ + A_Q_DIM
_C_AV = _C_AK + A_KV_DIM
_C_Z = _C_AV + A_KV_DIM
_C_XBC = _C_Z + SSD_D_INNER
_C_CK = _C_XBC + SSD_CONV_DIM
_C_DT = _C_CK + DIFF_QK_WIDTH
_C_END = _C_DT + LANES


def _inproj_kernel(x_ref, nw_ref, w_ref, wt_ref, aq_ref, ak_ref, av_ref, z_ref, xbc_ref, ck_ref, dt_ref,
                   cqt_ref, cvt_ref):
    h = _rms(x_ref[...], nw_ref[...]).astype(BF16)

    def seg(lo, hi):
        return jnp.dot(h, w_ref[:, lo:hi], preferred_element_type=F32)

    aq_ref[...] = (seg(_C_AQ, _C_AK) * (HEAD_DIM ** -0.5)).astype(BF16)
    ak_ref[...] = seg(_C_AK, _C_AV).astype(BF16)
    av_ref[...] = seg(_C_AV, _C_Z).astype(BF16)
    z_ref[...] = seg(_C_Z, _C_XBC)
    xbc_ref[...] = seg(_C_XBC, _C_CK)
    ck_ref[...] = seg(_C_CK, _C_DT).astype(BF16)
    dt_ref[...] = seg(_C_DT, _C_END)
    nt = (((1,), (1,)), ((), ()))
    qt = lax.dot_general(wt_ref[0:DIFF_QK_WIDTH, :], h, nt, preferred_element_type=F32)
    cqt_ref[...] = (qt * (DIFF_QK_DIM ** -0.5 * LOG2E)).astype(BF16)
    vt = lax.dot_general(wt_ref[DIFF_QK_WIDTH:, :], h, nt, preferred_element_type=F32)
    tm = vt.shape[1]
    pad = DIFF_VROWS - DIFF_V_DIM
    ones_row = (lax.broadcasted_iota(I32, (pad, tm), 0) == 0).astype(BF16)
    for hh in range(DIFF_HEADS):
        cvt_ref[hh * DIFF_VROWS:hh * DIFF_VROWS + DIFF_V_DIM, :] = (
            vt[hh * DIFF_V_DIM:(hh + 1) * DIFF_V_DIM].astype(BF16))
        cvt_ref[hh * DIFF_VROWS + DIFF_V_DIM:(hh + 1) * DIFF_VROWS, :] = ones_row


def _inproj(x2, norm_w, w_main, w_t, tm):
    t, d = x2.shape
    row = lambda i: (i, 0)
    col = lambda i: (0, i)
    full = lambda i: (0, 0)
    row_outs = [(A_Q_DIM, BF16), (A_KV_DIM, BF16), (A_KV_DIM, BF16), (SSD_D_INNER, F32), (SSD_CONV_DIM, F32),
                (DIFF_QK_WIDTH, BF16), (LANES, F32)]
    col_outs = [(DIFF_QK_WIDTH, BF16), (DIFF_HEADS * DIFF_VROWS, BF16)]
    out_shape = ([jax.ShapeDtypeStruct((t, w), dt) for w, dt in row_outs]
                 + [jax.ShapeDtypeStruct((w, t), dt) for w, dt in col_outs])
    out_specs = ([pl.BlockSpec((tm, w), row) for w, _ in row_outs]
                 + [pl.BlockSpec((w, tm), col) for w, _ in col_outs])
    return pl.pallas_call(
        _inproj_kernel,
        grid=(t // tm,),
        in_specs=[pl.BlockSpec((tm, d), row), pl.BlockSpec((1, d), full),
                  pl.BlockSpec(w_main.shape, full), pl.BlockSpec(w_t.shape, full)],
        out_specs=out_specs,
        out_shape=out_shape,
        compiler_params=_cparams("parallel"),
        name="inproj",
    )(x2, norm_w, w_main, w_t)


def _swa_kernel(sink_ref, slope_ref, q_ref, k_ref, v_ref, o_ref):
    n = pl.program_id(1)
    s_len = k_ref.shape[0]
    blk = q_ref.shape[0]
    band = 3 * blk
    start = pl.multiple_of(jnp.clip((n - 1) * blk, 0, s_len - band), blk)
    kb = k_ref[pl.ds(start, band), :]
    vb = v_ref[pl.ds(start, band), :]
    qpos = n * blk + lax.broadcasted_iota(I32, (blk, band), 0)
    kpos = start + lax.broadcasted_iota(I32, (blk, band), 1)
    dist_i = jnp.abs(qpos - kpos)
    valid = dist_i <= SWA_WINDOW
    dist = dist_i.astype(F32)
    q = q_ref[...]
    rep = SWA_HEADS // SWA_KV_HEADS
    outs = []
    for h in range(SWA_HEADS):
        g = h // rep
        qh = q[:, h * HEAD_DIM:(h + 1) * HEAD_DIM]
        kh = kb[:, g * HEAD_DIM:(g + 1) * HEAD_DIM]
        vh = vb[:, g * HEAD_DIM:(g + 1) * HEAD_DIM]
        s = lax.dot_general(qh, kh, (((1,), (1,)), ((), ())), preferred_element_type=F32)
        s = jnp.where(valid, s - slope_ref[h] * dist, NEG)
        sink = sink_ref[h]
        m = jnp.maximum(jnp.max(s, axis=-1, keepdims=True), sink)
        p = jnp.exp(s - m)
        denom = jnp.sum(p, axis=-1, keepdims=True) + jnp.exp(sink - m)
        probs = (p / denom).astype(BF16)
        outs.append(jnp.dot(probs, vh, preferred_element_type=F32))
    o_ref[...] = jnp.concatenate(outs, axis=-1).astype(o_ref.dtype)


def _swa(aq, ak, av, sink, slopes, batch, seq):
    blk = SWA_WINDOW
    nb = seq // blk
    smem = pl.BlockSpec(memory_space=pltpu.SMEM)
    return pl.pallas_call(
        _swa_kernel,
        grid=(batch, nb),
        in_specs=[smem, smem,
                  pl.BlockSpec((blk, A_Q_DIM), lambda b, n: (b * nb + n, 0)),
                  pl.BlockSpec((seq, A_KV_DIM), lambda b, n: (b, 0)),
                  pl.BlockSpec((seq, A_KV_DIM), lambda b, n: (b, 0))],
        out_specs=pl.BlockSpec((blk, A_Q_DIM), lambda b, n: (b * nb + n, 0)),
        out_shape=jax.ShapeDtypeStruct(aq.shape, BF16),
        compiler_params=_cparams("parallel", "parallel"),
        name="swa",
    )(sink, slopes, aq, ak, av)


DIFF_PAIR = 2 * 2 * DIFF_QK_DIM
DIFF_VROWS = 80
DIFF_NFEAT = 6
LOG2E = math.log2(math.e)


def _bf16_split(x):
    hi = x.astype(BF16).astype(F32)
    lo = (x - hi).astype(BF16).astype(F32)
    return hi, lo


def _diff_key_tile(i, j, tq, tk, nk):
    return ((i * tq) // tk + j) % nk


def _diff_kernel(slope_ref, qt_ref, k_ref, vt_ref, lam_ref, sw_ref, o_ref, qtb_ref, kx_ref, m_ref, acc_ref,
                 s_ref, p_ref, *, lambda_init):
    i = pl.program_id(1)
    j = pl.program_id(2)
    nk = pl.num_programs(2)
    tq = qt_ref.shape[1]
    tk = k_ref.shape[0]
    dq = DIFF_QK_DIM
    hw = 2 * dq
    nf = DIFF_NFEAT

    @pl.when(j == 0)
    def _():
        m_ref[...] = jnp.full(m_ref.shape, NEG, F32)
        acc_ref[...] = jnp.zeros(acc_ref.shape, F32)
        jj = lax.broadcasted_iota(I32, (tk, LANES), 0)
        lane = lax.broadcasted_iota(I32, (tk, LANES), 1) % hw
        coarse = ((jj // 16) * 16).astype(F32)
        fine = (jj % 16).astype(F32)
        f = lane % nf
        feat = jnp.where(f < 2, coarse, jnp.where(f < 4, fine, 1.0))
        kx_ref[0] = jnp.where(lane < nf, feat, 0.0).astype(BF16)
        kx_ref[1] = jnp.where((lane >= nf) & (lane < 2 * nf), feat, 0.0).astype(BF16)
        kx_ref[2] = jnp.zeros((tk, LANES), BF16)
        ii = lax.broadcasted_iota(I32, (1, 2 * tq), 1)
        ii = jnp.where(ii >= tq, ii - tq, ii).astype(F32)
        qt = qt_ref[...]
        col = lax.broadcasted_iota(I32, (hw, 2 * tq), 1)
        row = lax.broadcasted_iota(I32, (hw, 2 * tq), 0)
        own_map = row // dq == col // tq
        for h in range(DIFF_HEADS):
            odd = h % 2
            qh = qt[h * hw:(h + 1) * hw, :]
            qh2 = jnp.where(own_map, jnp.concatenate([qh, qh], axis=1), jnp.zeros((hw, 2 * tq), BF16))
            sl = jnp.full((1, 2 * tq), slope_ref[h], F32)
            s_hi, s_lo = _bf16_split(sl)
            v_hi, v_lo = _bf16_split(-sl * ii)
            zero = jnp.zeros((hw - 2 * nf, 2 * tq), F32)
            bias_rows = jnp.concatenate([s_hi, s_lo, s_hi, s_lo, v_hi, v_lo,
                                         -s_hi, -s_lo, -s_hi, -s_lo, -v_hi, -v_lo, zero], axis=0).astype(BF16)
            qtb_ref[h, odd * hw:(odd + 1) * hw, :] = qh2
            qtb_ref[h, (1 - odd) * hw:(2 - odd) * hw, :] = bias_rows

    q0 = i * tq
    k0 = _diff_key_tile(i, j, tq, tk, nk) * tk
    lane = lax.broadcasted_iota(I32, (tk, LANES), 1)

    def scores(h, kx):
        odd = h % 2
        pair = k_ref[:, (h // 2) * DIFF_PAIR:(h // 2 + 1) * DIFF_PAIR]
        k_aug = jnp.where((lane >= hw) if odd else (lane < hw), pair, kx)
        return jnp.dot(k_aug, qtb_ref[h], preferred_element_type=F32)

    def update(h, s, shift):
        m_old = m_ref[h:h + 1, :]
        m_new = jnp.maximum(m_old, jnp.max(s, axis=0, keepdims=True) + shift)
        alpha = jnp.exp2(m_old - m_new)
        p = jnp.exp2(s - (m_new - shift)).astype(BF16)
        pv = jnp.dot(vt_ref[h * DIFF_VROWS:(h + 1) * DIFF_VROWS, :], p, preferred_element_type=F32)
        acc_ref[h] = alpha * acc_ref[h] + pv
        m_ref[h:h + 1, :] = m_new

    @pl.when(j == 0)
    def _():
        kpos = k0 + lax.broadcasted_iota(I32, (tk, tq), 0)
        qpos = q0 + lax.broadcasted_iota(I32, (tk, tq), 1)
        dist = jnp.abs(qpos - kpos).astype(F32)
        for h in range(DIFF_HEADS):
            bias = slope_ref[h] * dist
            update(h, scores(h, kx_ref[2]) - jnp.concatenate([bias, bias], axis=1), 0.0)

    @pl.when(j > 0)
    def _():
        below = k0 < q0
        kx = kx_ref[jnp.where(below, 0, 1)]
        sign = jnp.where(below, 1.0, -1.0)
        for h in range(DIFF_HEADS):
            s_ref[h] = scores(h, kx)
        alphas = []
        for h in range(DIFF_HEADS):
            shift = sign * slope_ref[h] * (k0 - q0).astype(F32)
            s = s_ref[h]
            m_old = m_ref[h:h + 1, :]
            m_new = jnp.maximum(m_old, jnp.max(s, axis=0, keepdims=True) + shift)
            alphas.append(jnp.exp2(m_old - m_new))
            p_ref[h] = jnp.exp2(s - (m_new - shift)).astype(BF16)
            m_ref[h:h + 1, :] = m_new
        for h in range(DIFF_HEADS):
            pv = jnp.dot(vt_ref[h * DIFF_VROWS:(h + 1) * DIFF_VROWS, :], p_ref[h], preferred_element_type=F32)
            acc_ref[h] = alphas[h] * acc_ref[h] + pv

    @pl.when(j == nk - 1)
    def _():
        lp = lam_ref[...]
        lam = (jnp.exp(jnp.sum(lp[0:1] * lp[1:2], axis=-1, keepdims=True))
               - jnp.exp(jnp.sum(lp[2:3] * lp[3:4], axis=-1, keepdims=True)) + lambda_init)
        outs = []
        for h in range(DIFF_HEADS):
            a = acc_ref[h]
            o = a[0:DIFF_V_DIM] / a[DIFF_V_DIM:DIFF_V_DIM + 1]
            o = o[:, 0:tq] - lam * o[:, tq:2 * tq]
            ms = jnp.mean(o * o, axis=0, keepdims=True)
            outs.append(o * lax.rsqrt(ms + NORM_EPS) * sw_ref[...] * (1.0 - lambda_init))
        o_ref[...] = jnp.transpose(jnp.concatenate(outs, axis=0)).astype(o_ref.dtype)


def _diff(cqt, ck, cvt, slopes, lam_params, subln_w_col, lambda_init, batch, seq, tq, tk):
    assert tk % tq == 0 and seq % tk == 0
    nq, nk = seq // tq, seq // tk
    t = batch * seq
    smem = pl.BlockSpec(memory_space=pltpu.SMEM)
    key_tile = functools.partial(_diff_key_tile, tq=tq, tk=tk, nk=nk)
    return pl.pallas_call(
        functools.partial(_diff_kernel, lambda_init=lambda_init),
        grid=(batch, nq, nk),
        in_specs=[smem,
                  pl.BlockSpec((DIFF_QK_WIDTH, tq), lambda b, i, j: (0, b * nq + i)),
                  pl.BlockSpec((tk, DIFF_QK_WIDTH), lambda b, i, j: (b * nk + key_tile(i, j), 0)),
                  pl.BlockSpec((DIFF_HEADS * DIFF_VROWS, tk), lambda b, i, j: (0, b * nk + key_tile(i, j))),
                  pl.BlockSpec(lam_params.shape, lambda b, i, j: (0, 0)),
                  pl.BlockSpec(subln_w_col.shape, lambda b, i, j: (0, 0))],
        out_specs=pl.BlockSpec((tq, DIFF_V_WIDTH), lambda b, i, j: (b * nq + i, 0)),
        out_shape=jax.ShapeDtypeStruct((t, DIFF_V_WIDTH), BF16),
        scratch_shapes=[pltpu.VMEM((DIFF_HEADS, LANES, 2 * tq), BF16),
                        pltpu.VMEM((3, tk, LANES), BF16),
                        pltpu.VMEM((DIFF_HEADS, 2 * tq), F32),
                        pltpu.VMEM((DIFF_HEADS, DIFF_VROWS, 2 * tq), F32),
                        pltpu.VMEM((DIFF_HEADS, tk, 2 * tq), F32),
                        pltpu.VMEM((DIFF_HEADS, tk, 2 * tq), BF16)],
        compiler_params=_cparams("parallel", "parallel", "arbitrary"),
        name="diffattn",
    )(slopes, cqt, ck, cvt, lam_params, subln_w_col)


def _ssd_kernel(xbc_ref, prev_ref, next_ref, dt_ref, cw_ref, cb_ref, dtb_ref, alog_ref, dsk_ref, y_ref,
                state_ref):
    d = pl.program_id(1)
    c = pl.program_id(2)
    nc = pl.num_programs(2)
    q = xbc_ref.shape[0]
    cidx = jnp.where(d == 0, c, nc - 1 - c)
    fwd = d == 0

    @pl.when(c == 0)
    def _():
        state_ref[...] = jnp.zeros(state_ref.shape, F32)

    x = xbc_ref[...]
    prev = jnp.where(cidx > 0, prev_ref[...], 0.0)
    nxt = jnp.where(cidx < nc - 1, next_ref[...], 0.0)
    ext = jnp.concatenate([prev, x, nxt], axis=0)
    half = SSD_CONV // 2
    conv = cb_ref[...]
    for k in range(SSD_CONV):
        off = SUBLANES - half + k
        conv = conv + cw_ref[k:k + 1, :] * ext[off:off + q, :]
    u = _silu(conv)
    xs = u[:, :SSD_D_INNER]
    bmt = jnp.transpose(u[:, SSD_D_INNER:SSD_D_INNER + SSD_BC_DIM])
    cm = u[:, SSD_D_INNER + SSD_BC_DIM:]

    dt_all = _softplus(dt_ref[...] + dtb_ref[...])
    dta_all = dt_all * -jnp.exp(alog_ref[...])
    row = lax.broadcasted_iota(I32, (q, q), 0)
    col = lax.broadcasted_iota(I32, (q, q), 1)
    keep = (row - col) * jnp.where(fwd, 1, -1) >= 0
    tri = keep.astype(F32)
    da_all = jnp.dot(tri, dta_all, precision=lax.Precision.HIGHEST,
                     preferred_element_type=F32)
    da_all_t = jnp.transpose(da_all)
    tot_all = jnp.sum(dta_all, axis=0, keepdims=True)

    def pick_col(a, h):
        return jnp.where(fwd, a[:, h:h + 1], a[:, SSD_HEADS + h:SSD_HEADS + h + 1])

    def pick_row(a, h):
        return jnp.where(fwd, a[h:h + 1, :], a[SSD_HEADS + h:SSD_HEADS + h + 1, :])

    rep = SSD_HEADS // SSD_GROUPS
    g_mats = []
    for g in range(SSD_GROUPS):
        g_mats.append(jnp.dot(cm[:, g * SSD_STATE:(g + 1) * SSD_STATE].astype(BF16),
                              bmt[g * SSD_STATE:(g + 1) * SSD_STATE, :].astype(BF16),
                              preferred_element_type=F32))
    ys = []
    for h in range(SSD_HEADS):
        g = h // rep
        dac = pick_col(da_all, h)
        dar = pick_row(da_all_t, h)
        tot = pick_col(tot_all, h)
        decay = jnp.exp(jnp.where(keep, dac - dar, NEG))
        xh = xs[:, h * SSD_HEAD_DIM:(h + 1) * SSD_HEAD_DIM]
        xdt = (xh * pick_col(dt_all, h)).astype(BF16)
        y = jnp.dot((g_mats[g] * decay).astype(BF16), xdt, preferred_element_type=F32)
        st = state_ref[h]
        c_in = (cm[:, g * SSD_STATE:(g + 1) * SSD_STATE] * jnp.exp(dac)).astype(BF16)
        y = y + jnp.dot(c_in, st.astype(BF16), preferred_element_type=F32)
        to_end = jnp.exp(tot - dar)
        b_out = (bmt[g * SSD_STATE:(g + 1) * SSD_STATE, :] * to_end).astype(BF16)
        state_ref[h] = st * jnp.exp(tot) + jnp.dot(b_out, xdt, preferred_element_type=F32)
        y = y + jnp.where(fwd, dsk_ref[:, h:h + 1], 0.0) * xh
        ys.append(y)
    y_ref[...] = jnp.concatenate(ys, axis=-1)


def _ssd(xbc, dt_raw, conv_w, conv_b, dt_bias, a_log, d_skip, batch, seq):
    q = SSD_CHUNK
    nc = seq // q
    t = batch * seq
    hb = q // SUBLANES
    n_hblk = t // SUBLANES

    def cidx(d, c):
        return jnp.where(d == 0, c, nc - 1 - c)

    full = lambda b, d, c: (0, 0)
    return pl.pallas_call(
        _ssd_kernel,
        grid=(batch, 2, nc),
        in_specs=[pl.BlockSpec((q, SSD_CONV_DIM), lambda b, d, c: (b * nc + cidx(d, c), 0)),
                  pl.BlockSpec((SUBLANES, SSD_CONV_DIM),
                               lambda b, d, c: (jnp.maximum((b * nc + cidx(d, c)) * hb - 1, 0), 0)),
                  pl.BlockSpec((SUBLANES, SSD_CONV_DIM),
                               lambda b, d, c: (jnp.minimum((b * nc + cidx(d, c) + 1) * hb, n_hblk - 1), 0)),
                  pl.BlockSpec((q, LANES), lambda b, d, c: (b * nc + cidx(d, c), 0)),
                  pl.BlockSpec(conv_w.shape, full), pl.BlockSpec(conv_b.shape, full),
                  pl.BlockSpec(dt_bias.shape, full), pl.BlockSpec(a_log.shape, full),
                  pl.BlockSpec(d_skip.shape, full)],
        out_specs=pl.BlockSpec((None, q, SSD_D_INNER), lambda b, d, c: (d, b * nc + cidx(d, c), 0)),
        out_shape=jax.ShapeDtypeStruct((2, t, SSD_D_INNER), F32),
        scratch_shapes=[pltpu.VMEM((SSD_HEADS, SSD_STATE, SSD_HEAD_DIM), F32)],
        compiler_params=_cparams("parallel", "arbitrary", "arbitrary"),
        name="ssd",
    )(xbc, xbc, xbc, dt_raw, conv_w, conv_b, dt_bias, a_log, d_skip)


def _outproj_kernel(x_ref, ya_ref, ys_ref, z_ref, snw_ref, yc_ref, wo_ref, fnw_ref, wrt_ref, br_ref,
                    xn_ref, ri_ref, rf_ref, cnt_ref, tri_ref, carry_ref):
    step = pl.program_id(0)
    tm = x_ref.shape[0]

    @pl.when(step == 0)
    def _():
        carry_ref[...] = jnp.zeros(carry_ref.shape, F32)
        r = lax.broadcasted_iota(I32, (tm, tm), 0)
        cc = lax.broadcasted_iota(I32, (tm, tm), 1)
        tri_ref[...] = (r <= cc).astype(BF16)

    y = (ys_ref[0] + ys_ref[1]) * _silu(z_ref[...])
    yb = _rms(y, snw_ref[...]).astype(BF16)
    acc = jnp.dot(ya_ref[...], wo_ref[0:A_Q_DIM, :], preferred_element_type=F32)
    acc = acc + jnp.dot(yb, wo_ref[A_Q_DIM:A_Q_DIM + SSD_D_INNER, :], preferred_element_type=F32)
    acc = acc + jnp.dot(yc_ref[...], wo_ref[A_Q_DIM + SSD_D_INNER:, :], preferred_element_type=F32)
    xn = x_ref[...] + acc
    xn_ref[...] = xn

    h = _rms(xn, fnw_ref[...])
    logits = lax.dot_general(wrt_ref[...], h, (((1,), (1,)), ((), ())), precision=lax.Precision.HIGHEST,
                             preferred_element_type=F32) + br_ref[...]
    ne, epg, ng = N_EXPERTS, EXPERTS_PER_GROUP, N_EXPERT_GROUPS
    gl = logits[ne:ne + ng, :]
    gmax = jnp.max(gl, axis=0, keepdims=True)
    g_sel = jnp.full((1, tm), float(ng - 1), F32)
    for g in range(ng - 2, -1, -1):
        g_sel = jnp.where(gl[g:g + 1, :] == gmax, float(g), g_sel)
    g_gate = 1.0 / jnp.sum(jnp.exp(gl - gmax), axis=0, keepdims=True)
    e_in = logits[0:epg, :]
    for g in range(1, ng):
        e_in = jnp.where(g_sel == float(g), logits[g * epg:(g + 1) * epg, :], e_in)
    sub = lax.broadcasted_iota(I32, (epg, tm), 0).astype(F32)
    m1 = jnp.max(e_in, axis=0, keepdims=True)
    i1 = jnp.min(jnp.where(e_in == m1, sub, float(epg)), axis=0, keepdims=True)
    rest = jnp.where(sub == i1, NEG, e_in)
    m2 = jnp.max(rest, axis=0, keepdims=True)
    i2 = jnp.min(jnp.where(rest == m2, sub, float(epg)), axis=0, keepdims=True)
    r = jnp.exp(m2 - m1)
    c1 = g_gate / (1.0 + r)
    c2 = g_gate * r / (1.0 + r)
    e1 = (g_sel * epg + i1).astype(I32)
    e2 = (g_sel * epg + i2).astype(I32)

    erow = lax.broadcasted_iota(I32, (ne, tm), 0)
    hit1 = erow == e1
    hit2 = erow == e2
    oh = jnp.where(hit1 | hit2, 1.0, 0.0)
    incl = jnp.dot(oh.astype(BF16), tri_ref[...], preferred_element_type=F32)
    before = incl - oh + carry_ref[:, 0:1]
    rank1 = jnp.sum(jnp.where(hit1, before, 0.0), axis=0, keepdims=True)
    rank2 = jnp.sum(jnp.where(hit2, before, 0.0), axis=0, keepdims=True)
    carry_ref[...] = carry_ref[...] + jnp.sum(oh, axis=1, keepdims=True)
    cnt_ref[...] = carry_ref[...]
    zi = jnp.zeros((1, tm), I32)
    ri_ref[...] = jnp.concatenate([e1, e2, rank1.astype(I32), rank2.astype(I32), zi, zi, zi, zi], axis=0)
    zf = jnp.zeros((1, tm), F32)
    rf_ref[...] = jnp.concatenate([c1, c2, zf, zf, zf, zf, zf, zf], axis=0)


def _outproj(x2, ya, yssd, z, ssd_norm_w, yc, w_out, ffn_norm_w, wrt, br, tm):
    t, d = x2.shape
    row = lambda i: (i, 0)
    full = lambda i: (0, 0)
    return pl.pallas_call(
        _outproj_kernel,
        grid=(t // tm,),
        in_specs=[pl.BlockSpec((tm, d), row),
                  pl.BlockSpec((tm, A_Q_DIM), row),
                  pl.BlockSpec((2, tm, SSD_D_INNER), lambda i: (0, i, 0)),
                  pl.BlockSpec((tm, SSD_D_INNER), row),
                  pl.BlockSpec(ssd_norm_w.shape, full),
                  pl.BlockSpec((tm, DIFF_V_WIDTH), row),
                  pl.BlockSpec(w_out.shape, full),
                  pl.BlockSpec(ffn_norm_w.shape, full),
                  pl.BlockSpec(wrt.shape, full),
                  pl.BlockSpec(br.shape, full)],
        out_specs=[pl.BlockSpec((tm, d), row),
                   pl.BlockSpec((SUBLANES, tm), lambda i: (0, i)),
                   pl.BlockSpec((SUBLANES, tm), lambda i: (0, i)),
                   pl.BlockSpec((N_EXPERTS, LANES), full)],
        out_shape=[jax.ShapeDtypeStruct((t, d), F32),
                   jax.ShapeDtypeStruct((SUBLANES, t), I32),
                   jax.ShapeDtypeStruct((SUBLANES, t), F32),
                   jax.ShapeDtypeStruct((N_EXPERTS, LANES), F32)],
        scratch_shapes=[pltpu.VMEM((tm, tm), BF16), pltpu.VMEM((N_EXPERTS, LANES), F32)],
        compiler_params=_cparams("arbitrary"),
        name="outproj_router",
    )(x2, ya, yssd, z, ssd_norm_w, yc, w_out, ffn_norm_w, wrt, br)


def _moe_kernel(texp_ref, nused_ref, slot1_ref, slot2_ref, x_hbm, fnw_ref, wg_ref, wu_ref, wd_ref, y_ref,
                src_ref, xbuf, sem):
    i = pl.program_id(0)
    tr = xbuf.shape[1]
    n_tok = slot1_ref.shape[0]
    n_rows = src_ref.shape[0]
    n_used = nused_ref[0]

    def start_gather(tile, slot):
        base = tile * tr

        def body(r, carry):
            tok = src_ref[base + r]
            pltpu.make_async_copy(x_hbm.at[pl.ds(tok, 1)], xbuf.at[slot, pl.ds(r, 1)], sem.at[slot]).start()
            return carry

        lax.fori_loop(0, tr, body, 0, unroll=8)

    @pl.when(i == 0)
    def _():
        def zero(r, carry):
            src_ref[r] = 0
            return carry

        lax.fori_loop(0, n_rows, zero, 0, unroll=8)

        def fill(tok, carry):
            src_ref[slot1_ref[tok]] = tok
            src_ref[slot2_ref[tok]] = tok
            return carry

        lax.fori_loop(0, n_tok, fill, 0, unroll=8)
        start_gather(0, 0)

    @pl.when(i < n_used)
    def _():
        slot = i % 2

        @pl.when(i + 1 < n_used)
        def _():
            start_gather(i + 1, 1 - slot)

        pltpu.make_async_copy(x_hbm.at[pl.ds(0, tr)], xbuf.at[slot], sem.at[slot]).wait()
        h = _rms(xbuf[slot], fnw_ref[...]).astype(BF16)
        hg = jnp.dot(h, wg_ref[...], preferred_element_type=F32)
        hu = jnp.dot(h, wu_ref[...], preferred_element_type=F32)
        act = (_silu(hg) * hu).astype(BF16)
        y_ref[...] = jnp.dot(act, wd_ref[...], preferred_element_type=F32)

    @pl.when(i >= n_used)
    def _():
        y_ref[...] = jnp.zeros(y_ref.shape, F32)


def _moe(xn, ffn_norm_w, w_gate, w_up, w_down, tile_expert, n_used, slot1, slot2, n_tiles):
    t, d = xn.shape
    f = w_gate.shape[-1]
    tr = MOE_ROW_TILE

    def clamp(i, nu):
        return jnp.minimum(i, nu[0] - 1)

    grid_spec = pltpu.PrefetchScalarGridSpec(
        num_scalar_prefetch=4,
        grid=(n_tiles,),
        in_specs=[pl.BlockSpec(memory_space=pl.ANY),
                  pl.BlockSpec(ffn_norm_w.shape, lambda i, te, nu, s1, s2: (0, 0)),
                  pl.BlockSpec((None, d, f), lambda i, te, nu, s1, s2: (te[clamp(i, nu)], 0, 0)),
                  pl.BlockSpec((None, d, f), lambda i, te, nu, s1, s2: (te[clamp(i, nu)], 0, 0)),
                  pl.BlockSpec((None, f, d), lambda i, te, nu, s1, s2: (te[clamp(i, nu)], 0, 0))],
        out_specs=pl.BlockSpec((tr, d), lambda i, te, nu, s1, s2: (i, 0)),
        scratch_shapes=[pltpu.SMEM((n_tiles * tr,), I32),
                        pltpu.VMEM((2, tr, d), F32),
                        pltpu.SemaphoreType.DMA((2,))],
    )
    return pl.pallas_call(
        _moe_kernel,
        grid_spec=grid_spec,
        out_shape=jax.ShapeDtypeStruct((n_tiles * tr, d), F32),
        compiler_params=_cparams("arbitrary"),
        name="moe_experts",
    )(tile_expert, n_used, slot1, slot2, xn, ffn_norm_w, w_gate, w_up, w_down)


def _combine_kernel(slot1_ref, slot2_ref, x_ref, cw_ref, nw_ref, y_hbm, o_ref, ybuf, sem, *, final_norm):
    i = pl.program_id(0)
    n = pl.num_programs(0)
    tm = x_ref.shape[0]

    def start_gather(tile, slot):
        base = tile * tm

        def body(r, carry):
            pltpu.make_async_copy(y_hbm.at[pl.ds(slot1_ref[base + r], 1)], ybuf.at[slot, 0, pl.ds(r, 1)],
                                  sem.at[slot]).start()
            pltpu.make_async_copy(y_hbm.at[pl.ds(slot2_ref[base + r], 1)], ybuf.at[slot, 1, pl.ds(r, 1)],
                                  sem.at[slot]).start()
            return carry

        lax.fori_loop(0, tm, body, 0, unroll=8)

    @pl.when(i == 0)
    def _():
        start_gather(0, 0)

    slot = i % 2

    @pl.when(i + 1 < n)
    def _():
        start_gather(i + 1, 1 - slot)

    pltpu.make_async_copy(y_hbm.at[pl.ds(0, tm)], ybuf.at[slot, 0], sem.at[slot]).wait()
    pltpu.make_async_copy(y_hbm.at[pl.ds(0, tm)], ybuf.at[slot, 1], sem.at[slot]).wait()
    cw = cw_ref[...]
    out = x_ref[...] + cw[:, 0:1] * ybuf[slot, 0] + cw[:, 1:2] * ybuf[slot, 1]
    if final_norm:
        out = _rms(out, nw_ref[...])
    o_ref[...] = out


def _combine(xn, cw, norm_w, y_sorted, slot1, slot2, tm, final_norm):
    t, d = xn.shape
    grid_spec = pltpu.PrefetchScalarGridSpec(
        num_scalar_prefetch=2,
        grid=(t // tm,),
        in_specs=[pl.BlockSpec((tm, d), lambda i, s1, s2: (i, 0)),
                  pl.BlockSpec((tm, cw.shape[1]), lambda i, s1, s2: (i, 0)),
                  pl.BlockSpec(norm_w.shape, lambda i, s1, s2: (0, 0)),
                  pl.BlockSpec(memory_space=pl.ANY)],
        out_specs=pl.BlockSpec((tm, d), lambda i, s1, s2: (i, 0)),
        scratch_shapes=[pltpu.VMEM((2, 2, tm, d), F32), pltpu.SemaphoreType.DMA((2,))],
    )
    return pl.pallas_call(
        functools.partial(_combine_kernel, final_norm=final_norm),
        grid_spec=grid_spec,
        out_shape=jax.ShapeDtypeStruct((t, d), F32),
        compiler_params=_cparams("arbitrary"),
        name="moe_combine",
    )(slot1, slot2, xn, cw, norm_w, y_sorted)


def _pad_lanes(v):
    v = v.reshape(1, -1).astype(F32)
    return jnp.pad(v, ((0, 0), (0, LANES - v.shape[1])))


def kernel(x, attn_norm_w, w_in, swa_sink, ssd_conv_w, ssd_conv_b, ssd_dt_bias, ssd_a_log, ssd_d, ssd_norm_w,
           diff_lambda, diff_subln_w, w_out, ffn_norm_w, w_router_group, b_router_group, w_router_expert,
           b_router_expert, w_gate, w_up, w_down, final_norm_w):
    return _forward(x, attn_norm_w, w_in, swa_sink, ssd_conv_w, ssd_conv_b, ssd_dt_bias, ssd_a_log, ssd_d,
                    ssd_norm_w, diff_lambda, diff_subln_w, w_out, ffn_norm_w, w_router_group, b_router_group,
                    w_router_expert, b_router_expert, w_gate, w_up, w_down, final_norm_w)


def _forward(x, attn_norm_w, w_in, swa_sink, ssd_conv_w, ssd_conv_b, ssd_dt_bias, ssd_a_log, ssd_d, ssd_norm_w,
             diff_lambda, diff_subln_w, w_out, ffn_norm_w, w_router_group, b_router_group, w_router_expert,
             b_router_expert, w_gate, w_up, w_down, final_norm_w, tm=512, tq=512, tk=512, tmc=256):
    batch, seq, d = x.shape
    depth = w_in.shape[0]
    t = batch * seq
    tr = MOE_ROW_TILE
    n_tiles = (2 * t) // tr + N_EXPERTS
    slopes = jnp.exp2(-8.0 * jnp.arange(1, N_ALIBI_HEADS + 1, dtype=F32) / N_ALIBI_HEADS)
    swa_slopes, diff_slopes = slopes[:SWA_HEADS], slopes[SWA_HEADS:]

    sizes = [A_Q_DIM, A_KV_DIM, A_KV_DIM, SSD_D_INNER, SSD_CONV_DIM, SSD_DT_DIM, DIFF_QK_WIDTH, DIFF_QK_WIDTH,
             DIFF_V_WIDTH]
    offs = [0]
    for s in sizes:
        offs.append(offs[-1] + s)
    o_dt, o_cq, o_ck, o_cv = offs[5], offs[6], offs[7], offs[8]

    x2 = x.reshape(t, d)
    for l in range(depth):
        w = w_in[l]
        w_main = jnp.concatenate(
            [w[:, :o_dt], w[:, o_ck:o_cv], w[:, o_dt:o_cq],
             jnp.zeros((d, LANES - SSD_DT_DIM), w.dtype)], axis=1).astype(BF16)
        w_t = jnp.concatenate([w[:, o_cq:o_ck], w[:, o_cv:]], axis=1).T.astype(BF16)
        aq, ak, av, z, xbc, ck, dt_raw, cqt, cvt = _inproj(x2, attn_norm_w[l].reshape(1, d), w_main, w_t, tm)

        ya = _swa(aq, ak, av, swa_sink[l].astype(F32), swa_slopes, batch, seq)
        lambda_init = 0.8 - 0.6 * math.exp(-0.3 * l)
        yc = _diff(cqt, ck, cvt, diff_slopes * LOG2E, diff_lambda[l].astype(F32),
                   diff_subln_w[l].reshape(DIFF_V_DIM, 1).astype(F32), lambda_init, batch, seq, tq, tk)
        yssd = _ssd(xbc, dt_raw, ssd_conv_w[l].astype(F32), ssd_conv_b[l].reshape(1, -1).astype(F32),
                    _pad_lanes(ssd_dt_bias[l]), _pad_lanes(ssd_a_log[l]), _pad_lanes(ssd_d[l]), batch, seq)

        wrt = jnp.concatenate([w_router_expert[l], w_router_group[l],
                               jnp.zeros((d, SUBLANES - N_EXPERT_GROUPS), F32)], axis=1).T.astype(F32)
        br = jnp.concatenate([b_router_expert[l], b_router_group[l],
                              jnp.zeros((SUBLANES - N_EXPERT_GROUPS,), F32)]).reshape(-1, 1).astype(F32)
        xn, ri, rf, cnt = _outproj(x2, ya, yssd, z, ssd_norm_w[l].reshape(1, -1), yc, w_out[l].astype(BF16),
                                   ffn_norm_w[l].reshape(1, d), wrt, br, tm)

        counts = cnt[:, 0].astype(I32)
        padded = ((counts + tr - 1) // tr) * tr
        ends = jnp.cumsum(padded)
        starts = ends - padded
        slot1 = starts[ri[0]] + ri[2]
        slot2 = starts[ri[1]] + ri[3]
        tile_start = jnp.arange(n_tiles, dtype=I32) * tr
        tile_expert = jnp.minimum(jnp.sum(ends[None, :] <= tile_start[:, None], axis=1), N_EXPERTS - 1).astype(I32)
        n_used = (ends[-1] // tr).astype(I32).reshape(1)

        y_sorted = _moe(xn, ffn_norm_w[l].reshape(1, d), w_gate[l].astype(BF16), w_up[l].astype(BF16),
                        w_down[l].astype(BF16), tile_expert, n_used, slot1, slot2, n_tiles)
        last = l == depth - 1
        x2 = _combine(xn, rf.T, final_norm_w.reshape(1, d), y_sorted, slot1, slot2, tmc, last)
    return x2.reshape(batch, seq, d)
```

```python
import functools
import math

import jax
import jax.numpy as jnp
from jax import lax
from jax.experimental import pallas as pl
from jax.experimental.pallas import tpu as pltpu

F32 = jnp.float32
BF16 = jnp.bfloat16
I32 = jnp.int32

HEAD_DIM = 64
SWA_HEADS = 6
SWA_KV_HEADS = 2
SWA_WINDOW = 128
SSD_HEADS = 6
SSD_HEAD_DIM = 64
SSD_GROUPS = 2
SSD_STATE = 64
SSD_CONV = 5
DIFF_HEADS = 4
DIFF_QK_DIM = 32
DIFF_V_DIM = 64
N_EXPERT_GROUPS = 4
EXPERTS_PER_GROUP = 8
N_EXPERTS = N_EXPERT_GROUPS * EXPERTS_PER_GROUP
NORM_EPS = 1e-6

A_Q_DIM = SWA_HEADS * HEAD_DIM
A_KV_DIM = SWA_KV_HEADS * HEAD_DIM
SSD_D_INNER = SSD_HEADS * SSD_HEAD_DIM
SSD_BC_DIM = SSD_GROUPS * SSD_STATE
SSD_CONV_DIM = SSD_D_INNER + 2 * SSD_BC_DIM
SSD_DT_DIM = 2 * SSD_HEADS
DIFF_QK_WIDTH = DIFF_HEADS * 2 * DIFF_QK_DIM
DIFF_V_WIDTH = DIFF_HEADS * DIFF_V_DIM
N_ALIBI_HEADS = SWA_HEADS + DIFF_HEADS

LANES = 128
SUBLANES = 8
VMEM_LIMIT = 56 * 1024 * 1024
NEG = -1e30
LOG2E = math.log2(math.e)

SSD_CHUNK = 128
SWA_BLOCKS_PER_STEP = 4
MOE_ROW_TILE = 256
DIFF_PAIR = 2 * 2 * DIFF_QK_DIM
DIFF_VROWS = 80
DIFF_NFEAT = 6


def _cparams(*sem):
    return pltpu.CompilerParams(dimension_semantics=sem, vmem_limit_bytes=VMEM_LIMIT)


def _rms(x, w):
    return x * lax.rsqrt(jnp.mean(x * x, axis=-1, keepdims=True) + NORM_EPS) * w


def _silu(x):
    return x / (1.0 + jnp.exp(-x))


def _softplus(x):
    return jnp.maximum(x, 0.0) + jnp.log(1.0 + jnp.exp(-jnp.abs(x)))


def _bf16_split(x):
    hi = x.astype(BF16).astype(F32)
    lo = (x - hi).astype(BF16).astype(F32)
    return hi, lo


_C_AK = 0
_C_Z = _C_AK + A_KV_DIM
_C_XBC = _C_Z + SSD_D_INNER
_C_CK = _C_XBC + SSD_CONV_DIM
_C_DT = _C_CK + DIFF_QK_WIDTH
_C_END = _C_DT + LANES
_R_AQ = 0
_R_AV = _R_AQ + A_Q_DIM
_R_CQ = _R_AV + A_KV_DIM
_R_CV = _R_CQ + DIFF_QK_WIDTH
_R_END = _R_CV + DIFF_V_WIDTH


def _inproj_kernel(x_ref, xp_ref, xn_ref, nw_ref, w_ref, wt_ref, cw_ref, cb_ref,
                   ak_ref, z_ref, xs_ref, cm_ref, ck_ref, dt_ref, aqt_ref, avt_ref, cqt_ref, cvt_ref, bt_ref,
                   *, tiles_per_seq):
    i = pl.program_id(0)
    tm = x_ref.shape[0]
    nw = nw_ref[...]
    h = _rms(x_ref[...], nw).astype(BF16)

    def seg(lo, hi):
        return jnp.dot(h, w_ref[:, lo:hi], preferred_element_type=F32)

    ak_ref[...] = seg(_C_AK, _C_Z).astype(BF16)
    z_ref[...] = seg(_C_Z, _C_XBC)
    ck_ref[...] = seg(_C_CK, _C_DT).astype(BF16)
    dt_ref[...] = seg(_C_DT, _C_END)

    w_xbc = w_ref[:, _C_XBC:_C_CK]
    first = i % tiles_per_seq == 0
    last = i % tiles_per_seq == tiles_per_seq - 1
    prev = jnp.dot(_rms(xp_ref[...], nw).astype(BF16), w_xbc, preferred_element_type=F32)
    nxt = jnp.dot(_rms(xn_ref[...], nw).astype(BF16), w_xbc, preferred_element_type=F32)
    prev = jnp.where(first, 0.0, prev)
    nxt = jnp.where(last, 0.0, nxt)
    ext = jnp.concatenate([prev, seg(_C_XBC, _C_CK), nxt], axis=0)
    half = SSD_CONV // 2
    conv = cb_ref[...]
    for k in range(SSD_CONV):
        off = SUBLANES - half + k
        conv = conv + cw_ref[k:k + 1, :] * ext[off:off + tm, :]
    u = _silu(conv)
    xs_ref[...] = u[:, :SSD_D_INNER]
    bt_ref[...] = jnp.transpose(u[:, SSD_D_INNER:SSD_D_INNER + SSD_BC_DIM])
    cm_ref[...] = u[:, SSD_D_INNER + SSD_BC_DIM:]

    tr = lax.dot_general(wt_ref[...], h, (((1,), (1,)), ((), ())), preferred_element_type=F32)
    aqt_ref[...] = (tr[_R_AQ:_R_AV] * (HEAD_DIM ** -0.5)).astype(BF16)
    avt = tr[_R_AV:_R_CQ].astype(BF16)
    for c in range(tm // LANES):
        avt_ref[c] = avt[:, c * LANES:(c + 1) * LANES]
    cqt_ref[...] = (tr[_R_CQ:_R_CV] * (DIFF_QK_DIM ** -0.5 * LOG2E)).astype(BF16)
    pad = DIFF_VROWS - DIFF_V_DIM
    ones_row = (lax.broadcasted_iota(I32, (pad, tm), 0) == 0).astype(BF16)
    for hh in range(DIFF_HEADS):
        cvt_ref[hh * DIFF_VROWS:hh * DIFF_VROWS + DIFF_V_DIM, :] = (
            tr[_R_CV + hh * DIFF_V_DIM:_R_CV + (hh + 1) * DIFF_V_DIM].astype(BF16))
        cvt_ref[hh * DIFF_VROWS + DIFF_V_DIM:(hh + 1) * DIFF_VROWS, :] = ones_row


def _inproj(x2, norm_w, w_main, w_t, conv_w, conv_b, tm, seq):
    t, d = x2.shape
    hb = tm // SUBLANES
    n_hblk = t // SUBLANES
    row = lambda i: (i, 0)
    col = lambda i: (0, i)
    full = lambda i: (0, 0)
    row_outs = [(A_KV_DIM, BF16), (SSD_D_INNER, F32), (SSD_D_INNER, F32), (SSD_BC_DIM, F32),
                (DIFF_QK_WIDTH, BF16), (LANES, F32)]
    out_shape = [jax.ShapeDtypeStruct((t, w), dt) for w, dt in row_outs]
    out_specs = [pl.BlockSpec((tm, w), row) for w, _ in row_outs]
    out_shape += [jax.ShapeDtypeStruct((A_Q_DIM, t), BF16),
                  jax.ShapeDtypeStruct((t // LANES, A_KV_DIM, LANES), BF16),
                  jax.ShapeDtypeStruct((DIFF_QK_WIDTH, t), BF16),
                  jax.ShapeDtypeStruct((DIFF_HEADS * DIFF_VROWS, t), BF16),
                  jax.ShapeDtypeStruct((SSD_BC_DIM, t), F32)]
    out_specs += [pl.BlockSpec((A_Q_DIM, tm), col),
                  pl.BlockSpec((tm // LANES, A_KV_DIM, LANES), lambda i: (i, 0, 0)),
                  pl.BlockSpec((DIFF_QK_WIDTH, tm), col),
                  pl.BlockSpec((DIFF_HEADS * DIFF_VROWS, tm), col),
                  pl.BlockSpec((SSD_BC_DIM, tm), col)]
    return pl.pallas_call(
        functools.partial(_inproj_kernel, tiles_per_seq=seq // tm),
        grid=(t // tm,),
        in_specs=[pl.BlockSpec((tm, d), row),
                  pl.BlockSpec((SUBLANES, d), lambda i: (jnp.maximum(i * hb - 1, 0), 0)),
                  pl.BlockSpec((SUBLANES, d), lambda i: (jnp.minimum((i + 1) * hb, n_hblk - 1), 0)),
                  pl.BlockSpec((1, d), full),
                  pl.BlockSpec(w_main.shape, full), pl.BlockSpec(w_t.shape, full),
                  pl.BlockSpec(conv_w.shape, full), pl.BlockSpec(conv_b.shape, full)],
        out_specs=out_specs,
        out_shape=out_shape,
        compiler_params=_cparams("parallel"),
        name="inproj",
    )(x2, x2, x2, norm_w, w_main, w_t, conv_w, conv_b)


def _swa_kernel(sink_ref, slope_ref, qt_ref, k_ref, vt_ref, o_ref):
    step = pl.program_id(1)
    s_len = k_ref.shape[0]
    blk = SWA_WINDOW
    band = 3 * blk
    nb = s_len // blk
    rep = SWA_HEADS // SWA_KV_HEADS
    hd = HEAD_DIM
    for u in range(SWA_BLOCKS_PER_STEP):
        n = step * SWA_BLOCKS_PER_STEP + u
        start_blk = jnp.clip(n - 1, 0, nb - 3)
        start = pl.multiple_of(start_blk * blk, blk)
        kb = k_ref[pl.ds(start, band), :]
        v3 = vt_ref[pl.ds(start_blk, 3)]
        vtb = jnp.concatenate([v3[0], v3[1], v3[2]], axis=1)
        qt = qt_ref[:, u * blk:(u + 1) * blk]
        zero = jnp.zeros((hd, rep * blk), BF16)
        grp = [jnp.concatenate([qt[(g * rep + r) * hd:(g * rep + r + 1) * hd] for r in range(rep)], axis=1)
               for g in range(SWA_KV_HEADS)]
        qbd = jnp.concatenate([jnp.concatenate([grp[0], zero], axis=1),
                               jnp.concatenate([zero, grp[1]], axis=1)], axis=0)
        st = jnp.dot(kb, qbd, preferred_element_type=F32)
        kpos = start + lax.broadcasted_iota(I32, (band, blk), 0)
        qpos = n * blk + lax.broadcasted_iota(I32, (band, blk), 1)
        dist_i = jnp.abs(qpos - kpos)
        valid = dist_i <= SWA_WINDOW
        dist = dist_i.astype(F32)
        ps, inv = [], []
        for h in range(SWA_HEADS):
            s = jnp.where(valid, st[:, h * blk:(h + 1) * blk] - slope_ref[h] * dist, NEG)
            sink = sink_ref[h]
            m = jnp.maximum(jnp.max(s, axis=0, keepdims=True), sink)
            p = jnp.exp(s - m)
            inv.append(1.0 / (jnp.sum(p, axis=0, keepdims=True) + jnp.exp(sink - m)))
            ps.append(p.astype(BF16))
        outs = []
        for g in range(SWA_KV_HEADS):
            pg = jnp.concatenate(ps[g * rep:(g + 1) * rep], axis=1)
            og = jnp.dot(vtb[g * hd:(g + 1) * hd, :], pg, preferred_element_type=F32)
            for r in range(rep):
                outs.append(og[:, r * blk:(r + 1) * blk] * inv[g * rep + r])
        o_ref[u * blk:(u + 1) * blk, :] = jnp.transpose(jnp.concatenate(outs, axis=0)).astype(o_ref.dtype)


def _swa(aqt, ak, avt3, sink, slopes, batch, seq):
    blk = SWA_WINDOW
    rows = blk * SWA_BLOCKS_PER_STEP
    steps = seq // rows
    nb = seq // blk
    t = batch * seq
    smem = pl.BlockSpec(memory_space=pltpu.SMEM)
    return pl.pallas_call(
        _swa_kernel,
        grid=(batch, steps),
        in_specs=[smem, smem,
                  pl.BlockSpec((A_Q_DIM, rows), lambda b, s: (0, b * steps + s)),
                  pl.BlockSpec((seq, A_KV_DIM), lambda b, s: (b, 0)),
                  pl.BlockSpec((nb, A_KV_DIM, blk), lambda b, s: (b, 0, 0))],
        out_specs=pl.BlockSpec((rows, A_Q_DIM), lambda b, s: (b * steps + s, 0)),
        out_shape=jax.ShapeDtypeStruct((t, A_Q_DIM), BF16),
        compiler_params=_cparams("parallel", "parallel"),
        name="swa",
    )(sink, slopes, aqt, ak, avt3)


def _diff_key_tile(i, j, tq, tk, nk):
    return ((i * tq) // tk + j) % nk


def _diff_kernel(slope_ref, qt_ref, k_ref, vt_ref, lam_ref, sw_ref, o_ref, qtb_ref, kx_ref, m_ref, acc_ref,
                 s_ref, p_ref, *, lambda_init):
    i = pl.program_id(1)
    j = pl.program_id(2)
    nk = pl.num_programs(2)
    tq = qt_ref.shape[1]
    tk = k_ref.shape[0]
    dq = DIFF_QK_DIM
    hw = 2 * dq
    nf = DIFF_NFEAT

    @pl.when(j == 0)
    def _():
        m_ref[...] = jnp.full(m_ref.shape, NEG, F32)
        acc_ref[...] = jnp.zeros(acc_ref.shape, F32)
        jj = lax.broadcasted_iota(I32, (tk, LANES), 0)
        lane = lax.broadcasted_iota(I32, (tk, LANES), 1) % hw
        coarse = ((jj // 16) * 16).astype(F32)
        fine = (jj % 16).astype(F32)
        f = lane % nf
        feat = jnp.where(f < 2, coarse, jnp.where(f < 4, fine, 1.0))
        kx_ref[0] = jnp.where(lane < nf, feat, 0.0).astype(BF16)
        kx_ref[1] = jnp.where((lane >= nf) & (lane < 2 * nf), feat, 0.0).astype(BF16)
        kx_ref[2] = jnp.zeros((tk, LANES), BF16)
        ii = lax.broadcasted_iota(I32, (1, 2 * tq), 1)
        ii = jnp.where(ii >= tq, ii - tq, ii).astype(F32)
        qt = qt_ref[...]
        col = lax.broadcasted_iota(I32, (hw, 2 * tq), 1)
        row = lax.broadcasted_iota(I32, (hw, 2 * tq), 0)
        own_map = row // dq == col // tq
        for h in range(DIFF_HEADS):
            odd = h % 2
            qh = qt[h * hw:(h + 1) * hw, :]
            qh2 = jnp.where(own_map, jnp.concatenate([qh, qh], axis=1), jnp.zeros((hw, 2 * tq), BF16))
            sl = jnp.full((1, 2 * tq), slope_ref[h], F32)
            s_hi, s_lo = _bf16_split(sl)
            v_hi, v_lo = _bf16_split(-sl * ii)
            zero = jnp.zeros((hw - 2 * nf, 2 * tq), F32)
            bias_rows = jnp.concatenate([s_hi, s_lo, s_hi, s_lo, v_hi, v_lo,
                                         -s_hi, -s_lo, -s_hi, -s_lo, -v_hi, -v_lo, zero], axis=0).astype(BF16)
            qtb_ref[h, odd * hw:(odd + 1) * hw, :] = qh2
            qtb_ref[h, (1 - odd) * hw:(2 - odd) * hw, :] = bias_rows

    q0 = i * tq
    k0 = _diff_key_tile(i, j, tq, tk, nk) * tk
    lane = lax.broadcasted_iota(I32, (tk, LANES), 1)

    def scores(h, kx):
        odd = h % 2
        pair = k_ref[:, (h // 2) * DIFF_PAIR:(h // 2 + 1) * DIFF_PAIR]
        k_aug = jnp.where((lane >= hw) if odd else (lane < hw), pair, kx)
        return jnp.dot(k_aug, qtb_ref[h], preferred_element_type=F32)

    def softmax_step(h, shift):
        s = s_ref[h]
        m_old = m_ref[h:h + 1, :]
        m_new = jnp.maximum(m_old, jnp.max(s, axis=0, keepdims=True) + shift)
        p_ref[h] = jnp.exp2(s - (m_new - shift)).astype(BF16)
        m_ref[h:h + 1, :] = m_new
        return jnp.exp2(m_old - m_new)

    def accumulate(h, alpha):
        pv = jnp.dot(vt_ref[h * DIFF_VROWS:(h + 1) * DIFF_VROWS, :], p_ref[h], preferred_element_type=F32)
        acc_ref[h] = alpha * acc_ref[h] + pv

    @pl.when(j == 0)
    def _():
        kpos = k0 + lax.broadcasted_iota(I32, (tk, tq), 0)
        qpos = q0 + lax.broadcasted_iota(I32, (tk, tq), 1)
        dist = jnp.abs(qpos - kpos).astype(F32)
        for h in range(DIFF_HEADS):
            bias = slope_ref[h] * dist
            s_ref[h] = scores(h, kx_ref[2]) - jnp.concatenate([bias, bias], axis=1)
        alphas = [softmax_step(h, 0.0) for h in range(DIFF_HEADS)]
        for h in range(DIFF_HEADS):
            accumulate(h, alphas[h])

    @pl.when(j > 0)
    def _():
        below = k0 < q0
        kx = kx_ref[jnp.where(below, 0, 1)]
        sign = jnp.where(below, 1.0, -1.0)
        for h in range(DIFF_HEADS):
            s_ref[h] = scores(h, kx)
        alphas = [softmax_step(h, sign * slope_ref[h] * (k0 - q0).astype(F32)) for h in range(DIFF_HEADS)]
        for h in range(DIFF_HEADS):
            accumulate(h, alphas[h])

    @pl.when(j == nk - 1)
    def _():
        lp = lam_ref[...]
        lam = (jnp.exp(jnp.sum(lp[0:1] * lp[1:2], axis=-1, keepdims=True))
               - jnp.exp(jnp.sum(lp[2:3] * lp[3:4], axis=-1, keepdims=True)) + lambda_init)
        outs = []
        for h in range(DIFF_HEADS):
            a = acc_ref[h]
            o = a[0:DIFF_V_DIM] / a[DIFF_V_DIM:DIFF_V_DIM + 1]
            o = o[:, 0:tq] - lam * o[:, tq:2 * tq]
            ms = jnp.mean(o * o, axis=0, keepdims=True)
            outs.append(o * lax.rsqrt(ms + NORM_EPS) * sw_ref[...] * (1.0 - lambda_init))
        o_ref[...] = jnp.transpose(jnp.concatenate(outs, axis=0)).astype(o_ref.dtype)


def _diff(cqt, ck, cvt, slopes, lam_params, subln_w_col, lambda_init, batch, seq, tq, tk):
    assert tk % tq == 0 and seq % tk == 0
    nq, nk = seq // tq, seq // tk
    t = batch * seq
    smem = pl.BlockSpec(memory_space=pltpu.SMEM)
    key_tile = functools.partial(_diff_key_tile, tq=tq, tk=tk, nk=nk)
    return pl.pallas_call(
        functools.partial(_diff_kernel, lambda_init=lambda_init),
        grid=(batch, nq, nk),
        in_specs=[smem,
                  pl.BlockSpec((DIFF_QK_WIDTH, tq), lambda b, i, j: (0, b * nq + i)),
                  pl.BlockSpec((tk, DIFF_QK_WIDTH), lambda b, i, j: (b * nk + key_tile(i, j), 0)),
                  pl.BlockSpec((DIFF_HEADS * DIFF_VROWS, tk), lambda b, i, j: (0, b * nk + key_tile(i, j))),
                  pl.BlockSpec(lam_params.shape, lambda b, i, j: (0, 0)),
                  pl.BlockSpec(subln_w_col.shape, lambda b, i, j: (0, 0))],
        out_specs=pl.BlockSpec((tq, DIFF_V_WIDTH), lambda b, i, j: (b * nq + i, 0)),
        out_shape=jax.ShapeDtypeStruct((t, DIFF_V_WIDTH), BF16),
        scratch_shapes=[pltpu.VMEM((DIFF_HEADS, LANES, 2 * tq), BF16),
                        pltpu.VMEM((3, tk, LANES), BF16),
                        pltpu.VMEM((DIFF_HEADS, 2 * tq), F32),
                        pltpu.VMEM((DIFF_HEADS, DIFF_VROWS, 2 * tq), F32),
                        pltpu.VMEM((DIFF_HEADS, tk, 2 * tq), F32),
                        pltpu.VMEM((DIFF_HEADS, tk, 2 * tq), BF16)],
        compiler_params=_cparams("parallel", "parallel", "arbitrary"),
        name="diffattn",
    )(slopes, cqt, ck, cvt, lam_params, subln_w_col)


def _ssd_direction(fwd, xs_ref, bt_ref, cm_ref, dt_ref, dtb_ref, alog_ref, dsk_ref, y_ref, state_ref):
    q = xs_ref.shape[0]
    lane0 = 0 if fwd else SSD_HEADS
    dt_all = _softplus(dt_ref[...] + dtb_ref[...])
    dta_all = dt_all * -jnp.exp(alog_ref[...])
    row = lax.broadcasted_iota(I32, (q, q), 0)
    col = lax.broadcasted_iota(I32, (q, q), 1)
    keep = (row >= col) if fwd else (row <= col)
    da_all = jnp.dot(keep.astype(F32), dta_all, precision=lax.Precision.HIGHEST,
                     preferred_element_type=F32)
    da_all_t = jnp.transpose(da_all)
    tot_all = jnp.sum(dta_all, axis=0, keepdims=True)

    xs = xs_ref[...]
    bmt = bt_ref[...]
    cm = cm_ref[...]
    rep = SSD_HEADS // SSD_GROUPS
    ns = SSD_STATE
    g_mats = [jnp.dot(cm[:, g * ns:(g + 1) * ns].astype(BF16), bmt[g * ns:(g + 1) * ns, :].astype(BF16),
                      preferred_element_type=F32) for g in range(SSD_GROUPS)]
    ys = []
    for h in range(SSD_HEADS):
        g = h // rep
        ln = lane0 + h
        dac = da_all[:, ln:ln + 1]
        dar = da_all_t[ln:ln + 1, :]
        tot = tot_all[:, ln:ln + 1]
        decay = jnp.exp(jnp.where(keep, dac - dar, NEG))
        xh = xs[:, h * SSD_HEAD_DIM:(h + 1) * SSD_HEAD_DIM]
        xdt = (xh * dt_all[:, ln:ln + 1]).astype(BF16)
        y = jnp.dot((g_mats[g] * decay).astype(BF16), xdt, preferred_element_type=F32)
        st = state_ref[h]
        c_in = (cm[:, g * ns:(g + 1) * ns] * jnp.exp(dac)).astype(BF16)
        y = y + jnp.dot(c_in, st.astype(BF16), preferred_element_type=F32)
        to_end = jnp.exp(tot - dar)
        b_out = (bmt[g * ns:(g + 1) * ns, :] * to_end).astype(BF16)
        state_ref[h] = st * jnp.exp(tot) + jnp.dot(b_out, xdt, preferred_element_type=F32)
        if fwd:
            y = y + dsk_ref[:, h:h + 1] * xh
        ys.append(y)
    y_ref[...] = jnp.concatenate(ys, axis=-1)


def _ssd_kernel(xsf_ref, btf_ref, cmf_ref, dtf_ref, xsb_ref, btb_ref, cmb_ref, dtb_in_ref,
                dtbias_ref, alog_ref, dsk_ref, yf_ref, yb_ref, state_ref):
    @pl.when(pl.program_id(1) == 0)
    def _():
        state_ref[...] = jnp.zeros(state_ref.shape, F32)

    _ssd_direction(True, xsf_ref, btf_ref, cmf_ref, dtf_ref, dtbias_ref, alog_ref, dsk_ref, yf_ref,
                   state_ref.at[0])
    _ssd_direction(False, xsb_ref, btb_ref, cmb_ref, dtb_in_ref, dtbias_ref, alog_ref, dsk_ref, yb_ref,
                   state_ref.at[1])


def _ssd(xs, bt, cm, dt_raw, dt_bias, a_log, d_skip, batch, seq):
    q = SSD_CHUNK
    nc = seq // q
    t = batch * seq
    full = lambda b, c: (0, 0)
    fw = lambda b, c: b * nc + c
    bw = lambda b, c: b * nc + nc - 1 - c

    def specs(idx):
        return [pl.BlockSpec((q, SSD_D_INNER), lambda b, c: (idx(b, c), 0)),
                pl.BlockSpec((SSD_BC_DIM, q), lambda b, c: (0, idx(b, c))),
                pl.BlockSpec((q, SSD_BC_DIM), lambda b, c: (idx(b, c), 0)),
                pl.BlockSpec((q, LANES), lambda b, c: (idx(b, c), 0))]

    return pl.pallas_call(
        _ssd_kernel,
        grid=(batch, nc),
        in_specs=specs(fw) + specs(bw) + [pl.BlockSpec(dt_bias.shape, full), pl.BlockSpec(a_log.shape, full),
                                          pl.BlockSpec(d_skip.shape, full)],
        out_specs=[pl.BlockSpec((q, SSD_D_INNER), lambda b, c: (fw(b, c), 0)),
                   pl.BlockSpec((q, SSD_D_INNER), lambda b, c: (bw(b, c), 0))],
        out_shape=[jax.ShapeDtypeStruct((t, SSD_D_INNER), F32), jax.ShapeDtypeStruct((t, SSD_D_INNER), F32)],
        scratch_shapes=[pltpu.VMEM((2, SSD_HEADS, SSD_STATE, SSD_HEAD_DIM), F32)],
        compiler_params=_cparams("parallel", "arbitrary"),
        name="ssd",
    )(xs, bt, cm, dt_raw, xs, bt, cm, dt_raw, dt_bias, a_log, d_skip)


def _outproj_kernel(x_ref, ya_ref, yf_ref, yb_ref, z_ref, snw_ref, yc_ref, wo_ref, fnw_ref, wrt_ref, br_ref,
                    xn_ref, ri_ref, rf_ref, cnt_ref, tri_ref, carry_ref):
    step = pl.program_id(0)
    tm = x_ref.shape[0]

    @pl.when(step == 0)
    def _():
        carry_ref[...] = jnp.zeros(carry_ref.shape, F32)
        r = lax.broadcasted_iota(I32, (tm, tm), 0)
        cc = lax.broadcasted_iota(I32, (tm, tm), 1)
        tri_ref[...] = (r <= cc).astype(BF16)

    y = (yf_ref[...] + yb_ref[...]) * _silu(z_ref[...])
    yb = _rms(y, snw_ref[...]).astype(BF16)
    acc = jnp.dot(ya_ref[...], wo_ref[0:A_Q_DIM, :], preferred_element_type=F32)
    acc = acc + jnp.dot(yb, wo_ref[A_Q_DIM:A_Q_DIM + SSD_D_INNER, :], preferred_element_type=F32)
    acc = acc + jnp.dot(yc_ref[...], wo_ref[A_Q_DIM + SSD_D_INNER:, :], preferred_element_type=F32)
    xn = x_ref[...] + acc
    xn_ref[...] = xn

    h = _rms(xn, fnw_ref[...])
    logits = lax.dot_general(wrt_ref[...], h, (((1,), (1,)), ((), ())), precision=lax.Precision.HIGHEST,
                             preferred_element_type=F32) + br_ref[...]
    ne, epg, ng = N_EXPERTS, EXPERTS_PER_GROUP, N_EXPERT_GROUPS
    gl = logits[ne:ne + ng, :]
    gmax = jnp.max(gl, axis=0, keepdims=True)
    g_sel = jnp.full((1, tm), float(ng - 1), F32)
    for g in range(ng - 2, -1, -1):
        g_sel = jnp.where(gl[g:g + 1, :] == gmax, float(g), g_sel)
    g_gate = 1.0 / jnp.sum(jnp.exp(gl - gmax), axis=0, keepdims=True)
    e_in = logits[0:epg, :]
    for g in range(1, ng):
        e_in = jnp.where(g_sel == float(g), logits[g * epg:(g + 1) * epg, :], e_in)
    sub = lax.broadcasted_iota(I32, (epg, tm), 0).astype(F32)
    m1 = jnp.max(e_in, axis=0, keepdims=True)
    i1 = jnp.min(jnp.where(e_in == m1, sub, float(epg)), axis=0, keepdims=True)
    rest = jnp.where(sub == i1, NEG, e_in)
    m2 = jnp.max(rest, axis=0, keepdims=True)
    i2 = jnp.min(jnp.where(rest == m2, sub, float(epg)), axis=0, keepdims=True)
    r = jnp.exp(m2 - m1)
    c1 = g_gate / (1.0 + r)
    c2 = g_gate * r / (1.0 + r)
    e1 = (g_sel * epg + i1).astype(I32)
    e2 = (g_sel * epg + i2).astype(I32)

    erow = lax.broadcasted_iota(I32, (ne, tm), 0)
    hit1 = erow == e1
    hit2 = erow == e2
    oh = jnp.where(hit1 | hit2, 1.0, 0.0)
    incl = jnp.dot(oh.astype(BF16), tri_ref[...], preferred_element_type=F32)
    before = incl - oh + carry_ref[:, 0:1]
    rank1 = jnp.sum(jnp.where(hit1, before, 0.0), axis=0, keepdims=True)
    rank2 = jnp.sum(jnp.where(hit2, before, 0.0), axis=0, keepdims=True)
    carry_ref[...] = carry_ref[...] + jnp.sum(oh, axis=1, keepdims=True)
    cnt_ref[...] = carry_ref[...]
    zi = jnp.zeros((1, tm), I32)
    ri_ref[...] = jnp.concatenate([e1, e2, rank1.astype(I32), rank2.astype(I32), zi, zi, zi, zi], axis=0)
    zf = jnp.zeros((1, tm), F32)
    rf_ref[...] = jnp.concatenate([c1, c2, zf, zf, zf, zf, zf, zf], axis=0)


def _outproj(x2, ya, yf, yb, z, ssd_norm_w, yc, w_out, ffn_norm_w, wrt, br, tm):
    t, d = x2.shape
    row = lambda i: (i, 0)
    full = lambda i: (0, 0)
    return pl.pallas_call(
        _outproj_kernel,
        grid=(t // tm,),
        in_specs=[pl.BlockSpec((tm, d), row),
                  pl.BlockSpec((tm, A_Q_DIM), row),
                  pl.BlockSpec((tm, SSD_D_INNER), row),
                  pl.BlockSpec((tm, SSD_D_INNER), row),
                  pl.BlockSpec((tm, SSD_D_INNER), row),
                  pl.BlockSpec(ssd_norm_w.shape, full),
                  pl.BlockSpec((tm, DIFF_V_WIDTH), row),
                  pl.BlockSpec(w_out.shape, full),
                  pl.BlockSpec(ffn_norm_w.shape, full),
                  pl.BlockSpec(wrt.shape, full),
                  pl.BlockSpec(br.shape, full)],
        out_specs=[pl.BlockSpec((tm, d), row),
                   pl.BlockSpec((SUBLANES, tm), lambda i: (0, i)),
                   pl.BlockSpec((SUBLANES, tm), lambda i: (0, i)),
                   pl.BlockSpec((N_EXPERTS, LANES), full)],
        out_shape=[jax.ShapeDtypeStruct((t, d), F32),
                   jax.ShapeDtypeStruct((SUBLANES, t), I32),
                   jax.ShapeDtypeStruct((SUBLANES, t), F32),
                   jax.ShapeDtypeStruct((N_EXPERTS, LANES), F32)],
        scratch_shapes=[pltpu.VMEM((tm, tm), BF16), pltpu.VMEM((N_EXPERTS, LANES), F32)],
        compiler_params=_cparams("arbitrary"),
        name="outproj_router",
    )(x2, ya, yf, yb, z, ssd_norm_w, yc, w_out, ffn_norm_w, wrt, br)


_PAD_PIECES = tuple(1 << b for b in reversed(range(MOE_ROW_TILE.bit_length() - 1)))


def _dispatch_kernel(slot1_ref, slot2_ref, pstart_ref, plen_ref, nused_ref, x_ref, xs_hbm, zero_ref, sem):
    i = pl.program_id(0)
    tm = x_ref.shape[0]

    @pl.when(i == 0)
    def _():
        zero_ref[...] = jnp.zeros(zero_ref.shape, F32)

        def pieces(e, wait):
            n = plen_ref[e]
            first = pstart_ref[e]
            off = first + n
            for b in _PAD_PIECES:
                off = off - (n & b)
                dst = pl.ds(pl.multiple_of(off, b), b) if b >= SUBLANES else None
                if dst is not None:
                    @pl.when((n & b) != 0)
                    def _():
                        cp = pltpu.make_async_copy(zero_ref.at[pl.ds(0, b)], xs_hbm.at[dst], sem.at[1])
                        cp.wait() if wait else cp.start()

            for u in range(SUBLANES - 1):
                @pl.when(u < (n & (SUBLANES - 1)))
                def _():
                    cp = pltpu.make_async_copy(zero_ref.at[pl.ds(0, 1)], xs_hbm.at[pl.ds(first + u, 1)],
                                               sem.at[1])
                    cp.wait() if wait else cp.start()

        def tail(tile, wait):
            big = _PAD_PIECES[0]
            for part in range(MOE_ROW_TILE // big):
                dst = xs_hbm.at[pl.ds(pl.multiple_of(tile * MOE_ROW_TILE + part * big, big), big)]
                cp = pltpu.make_async_copy(zero_ref, dst, sem.at[1])
                cp.wait() if wait else cp.start()

        def loop(fn, lo, hi, wait):
            def body(k, carry):
                fn(k, wait)
                return carry

            lax.fori_loop(lo, hi, body, 0)

        n_tiles = xs_hbm.shape[0] // MOE_ROW_TILE
        for wait in (False, True):
            loop(pieces, 0, N_EXPERTS, wait)
            loop(tail, nused_ref[0], n_tiles, wait)

    base = i * tm
    for r in range(tm):
        src = x_ref.at[pl.ds(r, 1)]
        pltpu.make_async_copy(src, xs_hbm.at[pl.ds(slot1_ref[base + r], 1)], sem.at[0]).start()
        pltpu.make_async_copy(src, xs_hbm.at[pl.ds(slot2_ref[base + r], 1)], sem.at[0]).start()
    for _ in range(2):
        pltpu.make_async_copy(x_ref, xs_hbm.at[pl.ds(0, tm)], sem.at[0]).wait()


def _dispatch(xn, slot1, slot2, pad_start, pad_len, n_used, n_rows, tm):
    t, d = xn.shape
    grid_spec = pltpu.PrefetchScalarGridSpec(
        num_scalar_prefetch=5,
        grid=(t // tm,),
        in_specs=[pl.BlockSpec((tm, d), lambda i, s1, s2, ps, pn, nu: (i, 0))],
        out_specs=pl.BlockSpec(memory_space=pl.ANY),
        scratch_shapes=[pltpu.VMEM((_PAD_PIECES[0], d), F32), pltpu.SemaphoreType.DMA((2,))],
    )
    return pl.pallas_call(
        _dispatch_kernel,
        grid_spec=grid_spec,
        out_shape=jax.ShapeDtypeStruct((n_rows, d), F32),
        compiler_params=_cparams("arbitrary"),
        name="moe_dispatch",
    )(slot1, slot2, pad_start, pad_len, n_used, xn)


def _moe_kernel(texp_ref, nused_ref, x_ref, fnw_ref, wg_ref, wu_ref, wd_ref, y_ref):
    i = pl.program_id(0)

    @pl.when(i < nused_ref[0])
    def _():
        h = _rms(x_ref[...], fnw_ref[...]).astype(BF16)
        hg = jnp.dot(h, wg_ref[...].astype(BF16), preferred_element_type=F32)
        hu = jnp.dot(h, wu_ref[...].astype(BF16), preferred_element_type=F32)
        act = (_silu(hg) * hu).astype(BF16)
        y_ref[...] = jnp.dot(act, wd_ref[...].astype(BF16), preferred_element_type=F32)

    @pl.when(i >= nused_ref[0])
    def _():
        y_ref[...] = jnp.zeros(y_ref.shape, F32)


def _moe(xs, ffn_norm_w, w_gate, w_up, w_down, layer, tile_expert, n_used):
    n_rows, d = xs.shape
    f = w_gate.shape[-1]
    tr = MOE_ROW_TILE

    def used(i, nu):
        return jnp.maximum(jnp.minimum(i, nu[0] - 1), 0)

    grid_spec = pltpu.PrefetchScalarGridSpec(
        num_scalar_prefetch=2,
        grid=(n_rows // tr,),
        in_specs=[pl.BlockSpec((tr, d), lambda i, te, nu: (used(i, nu), 0)),
                  pl.BlockSpec(ffn_norm_w.shape, lambda i, te, nu: (0, 0)),
                  pl.BlockSpec((None, None, d, f), lambda i, te, nu: (layer, te[used(i, nu)], 0, 0)),
                  pl.BlockSpec((None, None, d, f), lambda i, te, nu: (layer, te[used(i, nu)], 0, 0)),
                  pl.BlockSpec((None, None, f, d), lambda i, te, nu: (layer, te[used(i, nu)], 0, 0))],
        out_specs=pl.BlockSpec((tr, d), lambda i, te, nu: (i, 0)),
    )
    return pl.pallas_call(
        _moe_kernel,
        grid_spec=grid_spec,
        out_shape=jax.ShapeDtypeStruct((n_rows, d), F32),
        compiler_params=_cparams("arbitrary"),
        name="moe_experts",
    )(tile_expert, n_used, xs, ffn_norm_w, w_gate, w_up, w_down)


def _combine_kernel(slot1_ref, slot2_ref, x_ref, cw_ref, nw_ref, y_hbm, o_ref, ybuf, sem, *, final_norm):
    i = pl.program_id(0)
    n = pl.num_programs(0)
    tm = x_ref.shape[0]

    def start_gather(tile, slot):
        base = tile * tm
        for r in range(tm):
            pltpu.make_async_copy(y_hbm.at[pl.ds(slot1_ref[base + r], 1)], ybuf.at[slot, 0, pl.ds(r, 1)],
                                  sem.at[slot]).start()
            pltpu.make_async_copy(y_hbm.at[pl.ds(slot2_ref[base + r], 1)], ybuf.at[slot, 1, pl.ds(r, 1)],
                                  sem.at[slot]).start()

    def compute(slot):
        for k in range(2):
            pltpu.make_async_copy(y_hbm.at[pl.ds(0, tm)], ybuf.at[slot, k], sem.at[slot]).wait()
        cw = cw_ref[...]
        out = x_ref[...] + cw[:, 0:1] * ybuf[slot, 0] + cw[:, 1:2] * ybuf[slot, 1]
        if final_norm:
            out = _rms(out, nw_ref[...])
        o_ref[...] = out

    @pl.when(i == 0)
    def _():
        start_gather(0, 0)

    for parity in range(2):
        @pl.when(i % 2 == parity)
        def _():
            @pl.when(i + 1 < n)
            def _():
                start_gather(i + 1, 1 - parity)

            compute(parity)


def _combine(xn, cw, norm_w, y_sorted, slot1, slot2, tm, final_norm):
    t, d = xn.shape
    grid_spec = pltpu.PrefetchScalarGridSpec(
        num_scalar_prefetch=2,
        grid=(t // tm,),
        in_specs=[pl.BlockSpec((tm, d), lambda i, s1, s2: (i, 0)),
                  pl.BlockSpec((tm, cw.shape[1]), lambda i, s1, s2: (i, 0)),
                  pl.BlockSpec(norm_w.shape, lambda i, s1, s2: (0, 0)),
                  pl.BlockSpec(memory_space=pl.ANY)],
        out_specs=pl.BlockSpec((tm, d), lambda i, s1, s2: (i, 0)),
        scratch_shapes=[pltpu.VMEM((2, 2, tm, d), F32), pltpu.SemaphoreType.DMA((2,))],
    )
    return pl.pallas_call(
        functools.partial(_combine_kernel, final_norm=final_norm),
        grid_spec=grid_spec,
        out_shape=jax.ShapeDtypeStruct((t, d), F32),
        compiler_params=_cparams("arbitrary"),
        name="moe_combine",
    )(slot1, slot2, xn, cw, norm_w, y_sorted)


def _pad_lanes(v):
    v = v.reshape(1, -1).astype(F32)
    return jnp.pad(v, ((0, 0), (0, LANES - v.shape[1])))


def kernel(x, attn_norm_w, w_in, swa_sink, ssd_conv_w, ssd_conv_b, ssd_dt_bias, ssd_a_log, ssd_d, ssd_norm_w,
           diff_lambda, diff_subln_w, w_out, ffn_norm_w, w_router_group, b_router_group, w_router_expert,
           b_router_expert, w_gate, w_up, w_down, final_norm_w):
    return _forward(x, attn_norm_w, w_in, swa_sink, ssd_conv_w, ssd_conv_b, ssd_dt_bias, ssd_a_log, ssd_d,
                    ssd_norm_w, diff_lambda, diff_subln_w, w_out, ffn_norm_w, w_router_group, b_router_group,
                    w_router_expert, b_router_expert, w_gate, w_up, w_down, final_norm_w)


def _forward(x, attn_norm_w, w_in, swa_sink, ssd_conv_w, ssd_conv_b, ssd_dt_bias, ssd_a_log, ssd_d, ssd_norm_w,
             diff_lambda, diff_subln_w, w_out, ffn_norm_w, w_router_group, b_router_group, w_router_expert,
             b_router_expert, w_gate, w_up, w_down, final_norm_w, tm=512, tq=512, tk=512, tmc=256):
    batch, seq, d = x.shape
    depth = w_in.shape[0]
    t = batch * seq
    tr = MOE_ROW_TILE
    n_tiles = (2 * t) // tr + N_EXPERTS
    slopes = jnp.exp2(-8.0 * jnp.arange(1, N_ALIBI_HEADS + 1, dtype=F32) / N_ALIBI_HEADS)
    swa_slopes, diff_slopes = slopes[:SWA_HEADS], slopes[SWA_HEADS:]

    sizes = [A_Q_DIM, A_KV_DIM, A_KV_DIM, SSD_D_INNER, SSD_CONV_DIM, SSD_DT_DIM, DIFF_QK_WIDTH, DIFF_QK_WIDTH,
             DIFF_V_WIDTH]
    offs = [0]
    for s in sizes:
        offs.append(offs[-1] + s)
    o_aq, o_ak, o_av, o_z, o_xbc, o_dt, o_cq, o_ck, o_cv, o_end = offs

    x2 = x.reshape(t, d)
    for l in range(depth):
        w = w_in[l]
        w_main = jnp.concatenate(
            [w[:, o_ak:o_av], w[:, o_z:o_dt], w[:, o_ck:o_cv], w[:, o_dt:o_cq],
             jnp.zeros((d, LANES - SSD_DT_DIM), w.dtype)], axis=1).astype(BF16)
        w_t = jnp.concatenate([w[:, o_aq:o_ak], w[:, o_av:o_z], w[:, o_cq:o_ck], w[:, o_cv:o_end]],
                              axis=1).T.astype(BF16)
        ak, z, xs, cm, ck, dt_raw, aqt, avt3, cqt, cvt, bt = _inproj(
            x2, attn_norm_w[l].reshape(1, d), w_main, w_t, ssd_conv_w[l].astype(F32),
            ssd_conv_b[l].reshape(1, -1).astype(F32), tm, seq)

        ya = _swa(aqt, ak, avt3, swa_sink[l].astype(F32), swa_slopes, batch, seq)
        lambda_init = 0.8 - 0.6 * math.exp(-0.3 * l)
        yc = _diff(cqt, ck, cvt, diff_slopes * LOG2E, diff_lambda[l].astype(F32),
                   diff_subln_w[l].reshape(DIFF_V_DIM, 1).astype(F32), lambda_init, batch, seq, tq, tk)
        yf, yb = _ssd(xs, bt, cm, dt_raw, _pad_lanes(ssd_dt_bias[l]), _pad_lanes(ssd_a_log[l]),
                      _pad_lanes(ssd_d[l]), batch, seq)

        wrt = jnp.concatenate([w_router_expert[l], w_router_group[l],
                               jnp.zeros((d, SUBLANES - N_EXPERT_GROUPS), F32)], axis=1).T.astype(F32)
        br = jnp.concatenate([b_router_expert[l], b_router_group[l],
                              jnp.zeros((SUBLANES - N_EXPERT_GROUPS,), F32)]).reshape(-1, 1).astype(F32)
        xn, ri, rf, cnt = _outproj(x2, ya, yf, yb, z, ssd_norm_w[l].reshape(1, -1), yc, w_out[l].astype(BF16),
                                   ffn_norm_w[l].reshape(1, d), wrt, br, tm)

        counts = cnt[:, 0].astype(I32)
        padded = ((counts + tr - 1) // tr) * tr
        ends = jnp.cumsum(padded)
        starts = ends - padded
        experts = jnp.arange(N_EXPERTS, dtype=I32)[:, None]

        def slot_of(e, rank):
            return jnp.sum(jnp.where(e[None, :] == experts, starts[:, None], 0), axis=0) + rank

        slot1 = slot_of(ri[0], ri[2])
        slot2 = slot_of(ri[1], ri[3])
        tile_start = jnp.arange(n_tiles, dtype=I32) * tr
        tile_expert = jnp.minimum(jnp.sum(ends[None, :] <= tile_start[:, None], axis=1), N_EXPERTS - 1).astype(I32)
        n_used = (ends[-1] // tr).astype(I32).reshape(1)

        xs_sorted = _dispatch(xn, slot1, slot2, starts + counts, padded - counts, n_used, n_tiles * tr, tmc)
        y_sorted = _moe(xs_sorted, ffn_norm_w[l].reshape(1, d), w_gate, w_up, w_down, l, tile_expert, n_used)
        last = l == depth - 1
        x2 = _combine(xn, rf.T, final_norm_w.reshape(1, d), y_sorted, slot1, slot2, tmc, last)
    return x2.reshape(batch, seq, d)
```

```python
import functools
import math

import jax
import jax.numpy as jnp
from jax import lax
from jax.experimental import pallas as pl
from jax.experimental.pallas import tpu as pltpu

F32 = jnp.float32
BF16 = jnp.bfloat16
I32 = jnp.int32

HEAD_DIM = 64
SWA_HEADS = 6
SWA_KV_HEADS = 2
SWA_WINDOW = 128
SSD_HEADS = 6
SSD_HEAD_DIM = 64
SSD_GROUPS = 2
SSD_STATE = 64
SSD_CONV = 5
DIFF_HEADS = 4
DIFF_QK_DIM = 32
DIFF_V_DIM = 64
N_EXPERT_GROUPS = 4
EXPERTS_PER_GROUP = 8
N_EXPERTS = N_EXPERT_GROUPS * EXPERTS_PER_GROUP
NORM_EPS = 1e-6

A_Q_DIM = SWA_HEADS * HEAD_DIM
A_KV_DIM = SWA_KV_HEADS * HEAD_DIM
SSD_D_INNER = SSD_HEADS * SSD_HEAD_DIM
SSD_BC_DIM = SSD_GROUPS * SSD_STATE
SSD_CONV_DIM = SSD_D_INNER + 2 * SSD_BC_DIM
SSD_DT_DIM = 2 * SSD_HEADS
DIFF_QK_WIDTH = DIFF_HEADS * 2 * DIFF_QK_DIM
DIFF_V_WIDTH = DIFF_HEADS * DIFF_V_DIM
N_ALIBI_HEADS = SWA_HEADS + DIFF_HEADS

LANES = 128
SUBLANES = 8
VMEM_LIMIT = 56 * 1024 * 1024
NEG = -1e30
LOG2E = math.log2(math.e)

SSD_CHUNK = 128
SWA_BLOCKS_PER_STEP = 4
MOE_ROW_TILE = 256
DIFF_PAIR = 2 * 2 * DIFF_QK_DIM
DIFF_VROWS = 80
DIFF_NFEAT = 6


def _cparams(*sem):
    return pltpu.CompilerParams(dimension_semantics=sem, vmem_limit_bytes=VMEM_LIMIT)


def _rms(x, w):
    return x * lax.rsqrt(jnp.mean(x * x, axis=-1, keepdims=True) + NORM_EPS) * w


def _silu(x):
    return x / (1.0 + jnp.exp(-x))


def _softplus(x):
    return jnp.maximum(x, 0.0) + jnp.log(1.0 + jnp.exp(-jnp.abs(x)))


def _bf16_split(x):
    hi = x.astype(BF16).astype(F32)
    lo = (x - hi).astype(BF16).astype(F32)
    return hi, lo


_C_AK = 0
_C_Z = _C_AK + A_KV_DIM
_C_XBC = _C_Z + SSD_D_INNER
_C_CK = _C_XBC + SSD_CONV_DIM
_C_DT = _C_CK + DIFF_HEADS * LANES
_C_END = _C_DT + LANES
_R_AQ = 0
_R_AV = _R_AQ + A_Q_DIM
_R_CQ = _R_AV + A_KV_DIM
_R_CV = _R_CQ + DIFF_QK_WIDTH
_R_END = _R_CV + DIFF_V_WIDTH


def _inproj_kernel(x_ref, xp_ref, xn_ref, nw_ref, w_ref, wt_ref, cw_ref, cb_ref,
                   ak_ref, z_ref, xs_ref, cm_ref, ck_ref, dt_ref, aqt_ref, avt_ref, cqt_ref, cvt_ref, bt_ref,
                   *, tiles_per_seq, diff_key_tile):
    i = pl.program_id(0)
    tm = x_ref.shape[0]
    nw = nw_ref[...]
    h = _rms(x_ref[...], nw).astype(BF16)

    def seg(lo, hi):
        return jnp.dot(h, w_ref[:, lo:hi], preferred_element_type=F32)

    ak_ref[...] = seg(_C_AK, _C_Z).astype(BF16)
    z_ref[...] = seg(_C_Z, _C_XBC)
    pos = (i * tm + lax.broadcasted_iota(I32, (tm, _C_DT - _C_CK), 0)) % diff_key_tile
    ck_ref[...] = (seg(_C_CK, _C_DT) + _diff_key_features(pos)).astype(BF16)
    dt_ref[...] = seg(_C_DT, _C_END)

    w_xbc = w_ref[:, _C_XBC:_C_CK]
    first = i % tiles_per_seq == 0
    last = i % tiles_per_seq == tiles_per_seq - 1
    prev = jnp.dot(_rms(xp_ref[...], nw).astype(BF16), w_xbc, preferred_element_type=F32)
    nxt = jnp.dot(_rms(xn_ref[...], nw).astype(BF16), w_xbc, preferred_element_type=F32)
    prev = jnp.where(first, 0.0, prev)
    nxt = jnp.where(last, 0.0, nxt)
    ext = jnp.concatenate([prev, seg(_C_XBC, _C_CK), nxt], axis=0)
    half = SSD_CONV // 2
    conv = cb_ref[...]
    for k in range(SSD_CONV):
        off = SUBLANES - half + k
        conv = conv + cw_ref[k:k + 1, :] * ext[off:off + tm, :]
    u = _silu(conv)
    xs_ref[...] = u[:, :SSD_D_INNER]
    bt_ref[...] = jnp.transpose(u[:, SSD_D_INNER:SSD_D_INNER + SSD_BC_DIM])
    cm_ref[...] = u[:, SSD_D_INNER + SSD_BC_DIM:]

    tr = lax.dot_general(wt_ref[...], h, (((1,), (1,)), ((), ())), preferred_element_type=F32)
    aqt_ref[...] = (tr[_R_AQ:_R_AV] * (HEAD_DIM ** -0.5)).astype(BF16)
    avt = tr[_R_AV:_R_CQ].astype(BF16)
    for c in range(tm // LANES):
        avt_ref[c] = avt[:, c * LANES:(c + 1) * LANES]
    cqt_ref[...] = (tr[_R_CQ:_R_CV] * (DIFF_QK_DIM ** -0.5 * LOG2E)).astype(BF16)
    pad = DIFF_VROWS - DIFF_V_DIM
    ones_row = (lax.broadcasted_iota(I32, (pad, tm), 0) == 0).astype(BF16)
    for hh in range(DIFF_HEADS):
        cvt_ref[hh * DIFF_VROWS:hh * DIFF_VROWS + DIFF_V_DIM, :] = (
            tr[_R_CV + hh * DIFF_V_DIM:_R_CV + (hh + 1) * DIFF_V_DIM].astype(BF16))
        cvt_ref[hh * DIFF_VROWS + DIFF_V_DIM:(hh + 1) * DIFF_VROWS, :] = ones_row


def _inproj(x2, norm_w, w_main, w_t, conv_w, conv_b, tm, seq, diff_key_tile):
    t, d = x2.shape
    hb = tm // SUBLANES
    n_hblk = t // SUBLANES
    row = lambda i: (i, 0)
    col = lambda i: (0, i)
    full = lambda i: (0, 0)
    row_outs = [(A_KV_DIM, BF16), (SSD_D_INNER, F32), (SSD_D_INNER, F32), (SSD_BC_DIM, F32),
                (DIFF_HEADS * LANES, BF16), (LANES, F32)]
    out_shape = [jax.ShapeDtypeStruct((t, w), dt) for w, dt in row_outs]
    out_specs = [pl.BlockSpec((tm, w), row) for w, _ in row_outs]
    out_shape += [jax.ShapeDtypeStruct((A_Q_DIM, t), BF16),
                  jax.ShapeDtypeStruct((t // LANES, A_KV_DIM, LANES), BF16),
                  jax.ShapeDtypeStruct((DIFF_QK_WIDTH, t), BF16),
                  jax.ShapeDtypeStruct((DIFF_HEADS * DIFF_VROWS, t), BF16),
                  jax.ShapeDtypeStruct((SSD_BC_DIM, t), F32)]
    out_specs += [pl.BlockSpec((A_Q_DIM, tm), col),
                  pl.BlockSpec((tm // LANES, A_KV_DIM, LANES), lambda i: (i, 0, 0)),
                  pl.BlockSpec((DIFF_QK_WIDTH, tm), col),
                  pl.BlockSpec((DIFF_HEADS * DIFF_VROWS, tm), col),
                  pl.BlockSpec((SSD_BC_DIM, tm), col)]
    return pl.pallas_call(
        functools.partial(_inproj_kernel, tiles_per_seq=seq // tm, diff_key_tile=diff_key_tile),
        grid=(t // tm,),
        in_specs=[pl.BlockSpec((tm, d), row),
                  pl.BlockSpec((SUBLANES, d), lambda i: (jnp.maximum(i * hb - 1, 0), 0)),
                  pl.BlockSpec((SUBLANES, d), lambda i: (jnp.minimum((i + 1) * hb, n_hblk - 1), 0)),
                  pl.BlockSpec((1, d), full),
                  pl.BlockSpec(w_main.shape, full), pl.BlockSpec(w_t.shape, full),
                  pl.BlockSpec(conv_w.shape, full), pl.BlockSpec(conv_b.shape, full)],
        out_specs=out_specs,
        out_shape=out_shape,
        compiler_params=_cparams("parallel"),
        name="inproj",
    )(x2, x2, x2, norm_w, w_main, w_t, conv_w, conv_b)


def _swa_kernel(sink_ref, slope_ref, qt_ref, k_ref, vt_ref, o_ref):
    step = pl.program_id(1)
    s_len = k_ref.shape[0]
    blk = SWA_WINDOW
    band = 3 * blk
    nb = s_len // blk
    rep = SWA_HEADS // SWA_KV_HEADS
    hd = HEAD_DIM
    for u in range(SWA_BLOCKS_PER_STEP):
        n = step * SWA_BLOCKS_PER_STEP + u
        start_blk = jnp.clip(n - 1, 0, nb - 3)
        start = pl.multiple_of(start_blk * blk, blk)
        kb = k_ref[pl.ds(start, band), :]
        v3 = vt_ref[pl.ds(start_blk, 3)]
        vtb = jnp.concatenate([v3[0], v3[1], v3[2]], axis=1)
        qt = qt_ref[:, u * blk:(u + 1) * blk]
        zero = jnp.zeros((hd, rep * blk), BF16)
        grp = [jnp.concatenate([qt[(g * rep + r) * hd:(g * rep + r + 1) * hd] for r in range(rep)], axis=1)
               for g in range(SWA_KV_HEADS)]
        qbd = jnp.concatenate([jnp.concatenate([grp[0], zero], axis=1),
                               jnp.concatenate([zero, grp[1]], axis=1)], axis=0)
        st = jnp.dot(kb, qbd, preferred_element_type=F32)
        kpos = start + lax.broadcasted_iota(I32, (band, blk), 0)
        qpos = n * blk + lax.broadcasted_iota(I32, (band, blk), 1)
        dist_i = jnp.abs(qpos - kpos)
        valid = dist_i <= SWA_WINDOW
        dist = dist_i.astype(F32)
        ps, inv = [], []
        for h in range(SWA_HEADS):
            s = jnp.where(valid, st[:, h * blk:(h + 1) * blk] - slope_ref[h] * dist, NEG)
            sink = sink_ref[h]
            m = jnp.maximum(jnp.max(s, axis=0, keepdims=True), sink)
            p = jnp.exp(s - m)
            inv.append(1.0 / (jnp.sum(p, axis=0, keepdims=True) + jnp.exp(sink - m)))
            ps.append(p.astype(BF16))
        outs = []
        for g in range(SWA_KV_HEADS):
            pg = jnp.concatenate(ps[g * rep:(g + 1) * rep], axis=1)
            og = jnp.dot(vtb[g * hd:(g + 1) * hd, :], pg, preferred_element_type=F32)
            for r in range(rep):
                outs.append(og[:, r * blk:(r + 1) * blk] * inv[g * rep + r])
        o_ref[u * blk:(u + 1) * blk, :] = jnp.transpose(jnp.concatenate(outs, axis=0)).astype(o_ref.dtype)


def _swa(aqt, ak, avt3, sink, slopes, batch, seq):
    blk = SWA_WINDOW
    rows = blk * SWA_BLOCKS_PER_STEP
    steps = seq // rows
    nb = seq // blk
    t = batch * seq
    smem = pl.BlockSpec(memory_space=pltpu.SMEM)
    return pl.pallas_call(
        _swa_kernel,
        grid=(batch, steps),
        in_specs=[smem, smem,
                  pl.BlockSpec((A_Q_DIM, rows), lambda b, s: (0, b * steps + s)),
                  pl.BlockSpec((seq, A_KV_DIM), lambda b, s: (b, 0)),
                  pl.BlockSpec((nb, A_KV_DIM, blk), lambda b, s: (b, 0, 0))],
        out_specs=pl.BlockSpec((rows, A_Q_DIM), lambda b, s: (b * steps + s, 0)),
        out_shape=jax.ShapeDtypeStruct((t, A_Q_DIM), BF16),
        compiler_params=_cparams("parallel", "parallel"),
        name="swa",
    )(sink, slopes, aqt, ak, avt3)


def _diff_key_tile(i, j, tq, tk, nk):
    return ((i * tq) // tk + j) % nk


def _diff_key_features(pos_in_tile):
    lane = lax.broadcasted_iota(I32, pos_in_tile.shape, 1) % LANES - 2 * DIFF_QK_DIM
    coarse = ((pos_in_tile // 16) * 16).astype(F32)
    fine = (pos_in_tile % 16).astype(F32)
    f = lane % DIFF_NFEAT
    feat = jnp.where(f < 2, coarse, jnp.where(f < 4, fine, 1.0))
    return jnp.where((lane >= 0) & (lane < 2 * DIFF_NFEAT), feat, 0.0)


def _diff_kernel(slope_ref, qt_ref, k_ref, vt_ref, lam_ref, sw_ref, o_ref, qtb_ref, m_ref, acc_ref,
                 s_ref, p_ref, *, lambda_init):
    i = pl.program_id(1)
    j = pl.program_id(2)
    nk = pl.num_programs(2)
    tq = qt_ref.shape[1]
    tk = k_ref.shape[0]
    dq = DIFF_QK_DIM
    hw = 2 * dq
    nf = DIFF_NFEAT

    @pl.when(j == 0)
    def _():
        m_ref[...] = jnp.full(m_ref.shape, NEG, F32)
        acc_ref[...] = jnp.zeros(acc_ref.shape, F32)
        ii = lax.broadcasted_iota(I32, (1, 2 * tq), 1)
        ii = jnp.where(ii >= tq, ii - tq, ii).astype(F32)
        qt = qt_ref[...]
        col = lax.broadcasted_iota(I32, (hw, 2 * tq), 1)
        row = lax.broadcasted_iota(I32, (hw, 2 * tq), 0)
        own_map = row // dq == col // tq
        for h in range(DIFF_HEADS):
            qh = qt[h * hw:(h + 1) * hw, :]
            qh2 = jnp.where(own_map, jnp.concatenate([qh, qh], axis=1), jnp.zeros((hw, 2 * tq), BF16))
            sl = jnp.full((1, 2 * tq), slope_ref[h], F32)
            s_hi, s_lo = _bf16_split(sl)
            v_hi, v_lo = _bf16_split(-sl * ii)
            rows = jnp.concatenate([s_hi, s_lo, s_hi, s_lo, v_hi, v_lo], axis=0)
            zrow = jnp.zeros((nf, 2 * tq), F32)
            zero = jnp.zeros((hw - 2 * nf, 2 * tq), F32)
            variants = ([rows, zrow], [zrow, -rows], [zrow, zrow])
            for v, pieces in enumerate(variants):
                qtb_ref[v, h, 0:hw, :] = qh2
                qtb_ref[v, h, hw:2 * hw, :] = jnp.concatenate(pieces + [zero], axis=0).astype(BF16)

    q0 = i * tq
    k0 = _diff_key_tile(i, j, tq, tk, nk) * tk

    def scores(h, variant):
        return jnp.dot(k_ref[:, h * LANES:(h + 1) * LANES], qtb_ref[variant, h],
                       preferred_element_type=F32)

    def stage_scores(h, s):
        s_ref[h] = s
        return jnp.max(s, axis=0, keepdims=True)

    def softmax_step(h, tile_max, shift):
        m_old = m_ref[h:h + 1, :]
        m_new = jnp.maximum(m_old, tile_max + shift)
        p_ref[h] = jnp.exp2((s_ref[h] - (m_new - shift)).astype(BF16))
        m_ref[h:h + 1, :] = m_new
        return jnp.exp2(m_old - m_new)

    def accumulate(h, alpha):
        pv = jnp.dot(vt_ref[h * DIFF_VROWS:(h + 1) * DIFF_VROWS, :], p_ref[h], preferred_element_type=F32)
        acc_ref[h] = alpha * acc_ref[h] + pv

    @pl.when(j == 0)
    def _():
        kpos = k0 + lax.broadcasted_iota(I32, (tk, tq), 0)
        qpos = q0 + lax.broadcasted_iota(I32, (tk, tq), 1)
        dist = jnp.abs(qpos - kpos).astype(F32)
        maxima = []
        for h in range(DIFF_HEADS):
            bias = slope_ref[h] * dist
            maxima.append(stage_scores(h, scores(h, 2) - jnp.concatenate([bias, bias], axis=1)))
        alphas = [softmax_step(h, maxima[h], 0.0) for h in range(DIFF_HEADS)]
        for h in range(DIFF_HEADS):
            accumulate(h, alphas[h])

    @pl.when(j > 0)
    def _():
        below = k0 < q0
        variant = jnp.where(below, 0, 1)
        sign = jnp.where(below, 1.0, -1.0)
        maxima = [stage_scores(h, scores(h, variant)) for h in range(DIFF_HEADS)]
        alphas = [softmax_step(h, maxima[h], sign * slope_ref[h] * (k0 - q0).astype(F32))
                  for h in range(DIFF_HEADS)]
        for h in range(DIFF_HEADS):
            accumulate(h, alphas[h])

    @pl.when(j == nk - 1)
    def _():
        lp = lam_ref[...]
        lam = (jnp.exp(jnp.sum(lp[0:1] * lp[1:2], axis=-1, keepdims=True))
               - jnp.exp(jnp.sum(lp[2:3] * lp[3:4], axis=-1, keepdims=True)) + lambda_init)
        outs = []
        for h in range(DIFF_HEADS):
            a = acc_ref[h]
            o = a[0:DIFF_V_DIM] / a[DIFF_V_DIM:DIFF_V_DIM + 1]
            o = o[:, 0:tq] - lam * o[:, tq:2 * tq]
            ms = jnp.mean(o * o, axis=0, keepdims=True)
            outs.append(o * lax.rsqrt(ms + NORM_EPS) * sw_ref[...] * (1.0 - lambda_init))
        o_ref[...] = jnp.transpose(jnp.concatenate(outs, axis=0)).astype(o_ref.dtype)


def _diff(cqt, ck, cvt, slopes, lam_params, subln_w_col, lambda_init, batch, seq, tq, tk):
    assert tk % tq == 0 and seq % tk == 0
    nq, nk = seq // tq, seq // tk
    t = batch * seq
    smem = pl.BlockSpec(memory_space=pltpu.SMEM)
    key_tile = functools.partial(_diff_key_tile, tq=tq, tk=tk, nk=nk)
    return pl.pallas_call(
        functools.partial(_diff_kernel, lambda_init=lambda_init),
        grid=(batch, nq, nk),
        in_specs=[smem,
                  pl.BlockSpec((DIFF_QK_WIDTH, tq), lambda b, i, j: (0, b * nq + i)),
                  pl.BlockSpec((tk, DIFF_HEADS * LANES), lambda b, i, j: (b * nk + key_tile(i, j), 0)),
                  pl.BlockSpec((DIFF_HEADS * DIFF_VROWS, tk), lambda b, i, j: (0, b * nk + key_tile(i, j))),
                  pl.BlockSpec(lam_params.shape, lambda b, i, j: (0, 0)),
                  pl.BlockSpec(subln_w_col.shape, lambda b, i, j: (0, 0))],
        out_specs=pl.BlockSpec((tq, DIFF_V_WIDTH), lambda b, i, j: (b * nq + i, 0)),
        out_shape=jax.ShapeDtypeStruct((t, DIFF_V_WIDTH), BF16),
        scratch_shapes=[pltpu.VMEM((3, DIFF_HEADS, LANES, 2 * tq), BF16),
                        pltpu.VMEM((DIFF_HEADS, 2 * tq), F32),
                        pltpu.VMEM((DIFF_HEADS, DIFF_VROWS, 2 * tq), F32),
                        pltpu.VMEM((DIFF_HEADS, tk, 2 * tq), F32),
                        pltpu.VMEM((DIFF_HEADS, tk, 2 * tq), BF16)],
        compiler_params=_cparams("parallel", "parallel", "arbitrary"),
        name="diffattn",
    )(slopes, cqt, ck, cvt, lam_params, subln_w_col)


def _ssd_direction(fwd, xs_ref, bt_ref, cm_ref, dt_ref, dtb_ref, alog_ref, dsk_ref, y_ref, state_ref):
    q = xs_ref.shape[0]
    lane0 = 0 if fwd else SSD_HEADS
    dt_all = _softplus(dt_ref[...] + dtb_ref[...])
    dta_all = dt_all * -jnp.exp(alog_ref[...])
    row = lax.broadcasted_iota(I32, (q, q), 0)
    col = lax.broadcasted_iota(I32, (q, q), 1)
    keep = (row >= col) if fwd else (row <= col)
    da_all = jnp.dot(keep.astype(F32), dta_all, precision=lax.Precision.HIGHEST,
                     preferred_element_type=F32)
    da_all_t = jnp.transpose(da_all)
    dt_all_t = jnp.transpose(dt_all)
    tot_all = jnp.sum(dta_all, axis=0, keepdims=True)

    xs = xs_ref[...]
    bmt = bt_ref[...]
    cm = cm_ref[...]
    rep = SSD_HEADS // SSD_GROUPS
    ns = SSD_STATE
    g_mats = [jnp.dot(cm[:, g * ns:(g + 1) * ns].astype(BF16), bmt[g * ns:(g + 1) * ns, :].astype(BF16),
                      preferred_element_type=F32) for g in range(SSD_GROUPS)]
    ys = []
    for h in range(SSD_HEADS):
        g = h // rep
        ln = lane0 + h
        dac = da_all[:, ln:ln + 1]
        dar = da_all_t[ln:ln + 1, :]
        dtr = dt_all_t[ln:ln + 1, :]
        tot = tot_all[:, ln:ln + 1]
        dac_b = jnp.broadcast_to(dac, (q, q))
        decay = jnp.exp(jnp.where(keep, dac_b - dar, NEG))
        xh = xs[:, h * SSD_HEAD_DIM:(h + 1) * SSD_HEAD_DIM]
        xhb = xh.astype(BF16)
        y = jnp.dot((g_mats[g] * decay * dtr).astype(BF16), xhb, preferred_element_type=F32)
        st = state_ref[h]
        c_in = (cm[:, g * ns:(g + 1) * ns] * jnp.exp(dac_b[:, 0:ns])).astype(BF16)
        y = y + jnp.dot(c_in, st.astype(BF16), preferred_element_type=F32)
        to_end = jnp.exp(tot - dar) * dtr
        b_out = (bmt[g * ns:(g + 1) * ns, :] * to_end).astype(BF16)
        state_ref[h] = st * jnp.exp(tot) + jnp.dot(b_out, xhb, preferred_element_type=F32)
        if fwd:
            y = y + dsk_ref[:, h:h + 1] * xh
        ys.append(y)
    y_ref[...] = jnp.concatenate(ys, axis=-1)


def _ssd_kernel(xsf_ref, btf_ref, cmf_ref, dtf_ref, xsb_ref, btb_ref, cmb_ref, dtb_in_ref,
                dtbias_ref, alog_ref, dsk_ref, yf_ref, yb_ref, state_ref):
    @pl.when(pl.program_id(1) == 0)
    def _():
        state_ref[...] = jnp.zeros(state_ref.shape, F32)

    _ssd_direction(True, xsf_ref, btf_ref, cmf_ref, dtf_ref, dtbias_ref, alog_ref, dsk_ref, yf_ref,
                   state_ref.at[0])
    _ssd_direction(False, xsb_ref, btb_ref, cmb_ref, dtb_in_ref, dtbias_ref, alog_ref, dsk_ref, yb_ref,
                   state_ref.at[1])


def _ssd(xs, bt, cm, dt_raw, dt_bias, a_log, d_skip, batch, seq):
    q = SSD_CHUNK
    nc = seq // q
    t = batch * seq
    full = lambda b, c: (0, 0)
    fw = lambda b, c: b * nc + c
    bw = lambda b, c: b * nc + nc - 1 - c

    def specs(idx):
        return [pl.BlockSpec((q, SSD_D_INNER), lambda b, c: (idx(b, c), 0)),
                pl.BlockSpec((SSD_BC_DIM, q), lambda b, c: (0, idx(b, c))),
                pl.BlockSpec((q, SSD_BC_DIM), lambda b, c: (idx(b, c), 0)),
                pl.BlockSpec((q, LANES), lambda b, c: (idx(b, c), 0))]

    return pl.pallas_call(
        _ssd_kernel,
        grid=(batch, nc),
        in_specs=specs(fw) + specs(bw) + [pl.BlockSpec(dt_bias.shape, full), pl.BlockSpec(a_log.shape, full),
                                          pl.BlockSpec(d_skip.shape, full)],
        out_specs=[pl.BlockSpec((q, SSD_D_INNER), lambda b, c: (fw(b, c), 0)),
                   pl.BlockSpec((q, SSD_D_INNER), lambda b, c: (bw(b, c), 0))],
        out_shape=[jax.ShapeDtypeStruct((t, SSD_D_INNER), F32), jax.ShapeDtypeStruct((t, SSD_D_INNER), F32)],
        scratch_shapes=[pltpu.VMEM((2, SSD_HEADS, SSD_STATE, SSD_HEAD_DIM), F32)],
        compiler_params=_cparams("parallel", "arbitrary"),
        name="ssd",
    )(xs, bt, cm, dt_raw, xs, bt, cm, dt_raw, dt_bias, a_log, d_skip)


def _outproj_kernel(x_ref, ya_ref, yf_ref, yb_ref, z_ref, snw_ref, yc_ref, wo_ref, fnw_ref, wrt_ref, br_ref,
                    xn_ref, ri_ref, rf_ref, cnt_ref, tri_ref, carry_ref):
    step = pl.program_id(0)
    tm = x_ref.shape[0]

    @pl.when(step == 0)
    def _():
        carry_ref[...] = jnp.zeros(carry_ref.shape, F32)
        r = lax.broadcasted_iota(I32, (tm, tm), 0)
        cc = lax.broadcasted_iota(I32, (tm, tm), 1)
        tri_ref[...] = (r <= cc).astype(BF16)

    y = (yf_ref[...] + yb_ref[...]) * _silu(z_ref[...])
    yb = _rms(y, snw_ref[...]).astype(BF16)
    acc = jnp.dot(ya_ref[...], wo_ref[0:A_Q_DIM, :], preferred_element_type=F32)
    acc = acc + jnp.dot(yb, wo_ref[A_Q_DIM:A_Q_DIM + SSD_D_INNER, :], preferred_element_type=F32)
    acc = acc + jnp.dot(yc_ref[...], wo_ref[A_Q_DIM + SSD_D_INNER:, :], preferred_element_type=F32)
    xn = x_ref[...] + acc
    xn_ref[...] = xn

    h = _rms(xn, fnw_ref[...])
    logits = lax.dot_general(wrt_ref[...], h, (((1,), (1,)), ((), ())), precision=lax.Precision.HIGHEST,
                             preferred_element_type=F32) + br_ref[...]
    ne, epg, ng = N_EXPERTS, EXPERTS_PER_GROUP, N_EXPERT_GROUPS
    gl = logits[ne:ne + ng, :]
    gmax = jnp.max(gl, axis=0, keepdims=True)
    g_sel = jnp.full((1, tm), float(ng - 1), F32)
    for g in range(ng - 2, -1, -1):
        g_sel = jnp.where(gl[g:g + 1, :] == gmax, float(g), g_sel)
    g_gate = 1.0 / jnp.sum(jnp.exp(gl - gmax), axis=0, keepdims=True)
    e_in = logits[0:epg, :]
    for g in range(1, ng):
        e_in = jnp.where(g_sel == float(g), logits[g * epg:(g + 1) * epg, :], e_in)
    sub = lax.broadcasted_iota(I32, (epg, tm), 0).astype(F32)
    m1 = jnp.max(e_in, axis=0, keepdims=True)
    i1 = jnp.min(jnp.where(e_in == m1, sub, float(epg)), axis=0, keepdims=True)
    rest = jnp.where(sub == i1, NEG, e_in)
    m2 = jnp.max(rest, axis=0, keepdims=True)
    i2 = jnp.min(jnp.where(rest == m2, sub, float(epg)), axis=0, keepdims=True)
    r = jnp.exp(m2 - m1)
    c1 = g_gate / (1.0 + r)
    c2 = g_gate * r / (1.0 + r)
    e1 = (g_sel * epg + i1).astype(I32)
    e2 = (g_sel * epg + i2).astype(I32)

    erow = lax.broadcasted_iota(I32, (ne, tm), 0)
    hit1 = erow == e1
    hit2 = erow == e2
    oh = jnp.where(hit1 | hit2, 1.0, 0.0)
    incl = jnp.dot(oh.astype(BF16), tri_ref[...], preferred_element_type=F32)
    before = incl - oh + carry_ref[:, 0:1]
    rank1 = jnp.sum(jnp.where(hit1, before, 0.0), axis=0, keepdims=True)
    rank2 = jnp.sum(jnp.where(hit2, before, 0.0), axis=0, keepdims=True)
    carry_ref[...] = carry_ref[...] + jnp.sum(oh, axis=1, keepdims=True)
    cnt_ref[...] = carry_ref[...]
    zi = jnp.zeros((1, tm), I32)
    ri_ref[...] = jnp.concatenate([e1, e2, rank1.astype(I32), rank2.astype(I32), zi, zi, zi, zi], axis=0)
    zf = jnp.zeros((1, tm), F32)
    rf_ref[...] = jnp.concatenate([c1, c2, zf, zf, zf, zf, zf, zf], axis=0)


def _outproj(x2, ya, yf, yb, z, ssd_norm_w, yc, w_out, ffn_norm_w, wrt, br, tm):
    t, d = x2.shape
    row = lambda i: (i, 0)
    full = lambda i: (0, 0)
    return pl.pallas_call(
        _outproj_kernel,
        grid=(t // tm,),
        in_specs=[pl.BlockSpec((tm, d), row),
                  pl.BlockSpec((tm, A_Q_DIM), row),
                  pl.BlockSpec((tm, SSD_D_INNER), row),
                  pl.BlockSpec((tm, SSD_D_INNER), row),
                  pl.BlockSpec((tm, SSD_D_INNER), row),
                  pl.BlockSpec(ssd_norm_w.shape, full),
                  pl.BlockSpec((tm, DIFF_V_WIDTH), row),
                  pl.BlockSpec(w_out.shape, full),
                  pl.BlockSpec(ffn_norm_w.shape, full),
                  pl.BlockSpec(wrt.shape, full),
                  pl.BlockSpec(br.shape, full)],
        out_specs=[pl.BlockSpec((tm, d), row),
                   pl.BlockSpec((SUBLANES, tm), lambda i: (0, i)),
                   pl.BlockSpec((SUBLANES, tm), lambda i: (0, i)),
                   pl.BlockSpec((N_EXPERTS, LANES), full)],
        out_shape=[jax.ShapeDtypeStruct((t, d), F32),
                   jax.ShapeDtypeStruct((SUBLANES, t), I32),
                   jax.ShapeDtypeStruct((SUBLANES, t), F32),
                   jax.ShapeDtypeStruct((N_EXPERTS, LANES), F32)],
        scratch_shapes=[pltpu.VMEM((tm, tm), BF16), pltpu.VMEM((N_EXPERTS, LANES), F32)],
        compiler_params=_cparams("arbitrary"),
        name="outproj_router",
    )(x2, ya, yf, yb, z, ssd_norm_w, yc, w_out, ffn_norm_w, wrt, br)


_PAD_PIECES = tuple(1 << b for b in reversed(range(MOE_ROW_TILE.bit_length() - 1)))


def _dispatch_kernel(slot1_ref, slot2_ref, pstart_ref, plen_ref, nused_ref, x_ref, xs_hbm, zero_ref, sem):
    i = pl.program_id(0)
    tm = x_ref.shape[0]

    @pl.when(i == 0)
    def _():
        zero_ref[...] = jnp.zeros(zero_ref.shape, F32)

        def pieces(e, wait):
            n = plen_ref[e]
            first = pstart_ref[e]
            off = first + n
            for b in _PAD_PIECES:
                off = off - (n & b)
                dst = pl.ds(pl.multiple_of(off, b), b) if b >= SUBLANES else None
                if dst is not None:
                    @pl.when((n & b) != 0)
                    def _():
                        cp = pltpu.make_async_copy(zero_ref.at[pl.ds(0, b)], xs_hbm.at[dst], sem.at[1])
                        cp.wait() if wait else cp.start()

            for u in range(SUBLANES - 1):
                @pl.when(u < (n & (SUBLANES - 1)))
                def _():
                    cp = pltpu.make_async_copy(zero_ref.at[pl.ds(0, 1)], xs_hbm.at[pl.ds(first + u, 1)],
                                               sem.at[1])
                    cp.wait() if wait else cp.start()

        def tail(tile, wait):
            big = _PAD_PIECES[0]
            for part in range(MOE_ROW_TILE // big):
                dst = xs_hbm.at[pl.ds(pl.multiple_of(tile * MOE_ROW_TILE + part * big, big), big)]
                cp = pltpu.make_async_copy(zero_ref, dst, sem.at[1])
                cp.wait() if wait else cp.start()

        def loop(fn, lo, hi, wait):
            def body(k, carry):
                fn(k, wait)
                return carry

            lax.fori_loop(lo, hi, body, 0)

        n_tiles = xs_hbm.shape[0] // MOE_ROW_TILE
        for wait in (False, True):
            loop(pieces, 0, N_EXPERTS, wait)
            loop(tail, nused_ref[0], n_tiles, wait)

    base = i * tm
    for r in range(tm):
        src = x_ref.at[pl.ds(r, 1)]
        pltpu.make_async_copy(src, xs_hbm.at[pl.ds(slot1_ref[base + r], 1)], sem.at[0]).start(priority=0)
        pltpu.make_async_copy(src, xs_hbm.at[pl.ds(slot2_ref[base + r], 1)], sem.at[0]).start(priority=1)
    for _ in range(2):
        pltpu.make_async_copy(x_ref, xs_hbm.at[pl.ds(0, tm)], sem.at[0]).wait()


def _dispatch(xn, slot1, slot2, pad_start, pad_len, n_used, n_rows, tm):
    t, d = xn.shape
    grid_spec = pltpu.PrefetchScalarGridSpec(
        num_scalar_prefetch=5,
        grid=(t // tm,),
        in_specs=[pl.BlockSpec((tm, d), lambda i, s1, s2, ps, pn, nu: (i, 0))],
        out_specs=pl.BlockSpec(memory_space=pl.ANY),
        scratch_shapes=[pltpu.VMEM((_PAD_PIECES[0], d), F32), pltpu.SemaphoreType.DMA((2,))],
    )
    return pl.pallas_call(
        _dispatch_kernel,
        grid_spec=grid_spec,
        out_shape=jax.ShapeDtypeStruct((n_rows, d), F32),
        compiler_params=_cparams("arbitrary"),
        name="moe_dispatch",
    )(slot1, slot2, pad_start, pad_len, n_used, xn)


def _moe_kernel(texp_ref, nused_ref, x_ref, fnw_ref, wg_ref, wu_ref, wd_ref, y_ref):
    i = pl.program_id(0)

    @pl.when(i < nused_ref[0])
    def _():
        h = _rms(x_ref[...], fnw_ref[...]).astype(BF16)
        hg = jnp.dot(h, wg_ref[...].astype(BF16), preferred_element_type=F32)
        hu = jnp.dot(h, wu_ref[...].astype(BF16), preferred_element_type=F32)
        act = (_silu(hg) * hu).astype(BF16)
        y_ref[...] = jnp.dot(act, wd_ref[...].astype(BF16), preferred_element_type=F32)

    @pl.when(i >= nused_ref[0])
    def _():
        y_ref[...] = jnp.zeros(y_ref.shape, F32)


def _moe(xs, ffn_norm_w, w_gate, w_up, w_down, layer, tile_expert, n_used):
    n_rows, d = xs.shape
    f = w_gate.shape[-1]
    tr = MOE_ROW_TILE

    def used(i, nu):
        return jnp.maximum(jnp.minimum(i, nu[0] - 1), 0)

    grid_spec = pltpu.PrefetchScalarGridSpec(
        num_scalar_prefetch=2,
        grid=(n_rows // tr,),
        in_specs=[pl.BlockSpec((tr, d), lambda i, te, nu: (used(i, nu), 0)),
                  pl.BlockSpec(ffn_norm_w.shape, lambda i, te, nu: (0, 0)),
                  pl.BlockSpec((None, None, d, f), lambda i, te, nu: (layer, te[used(i, nu)], 0, 0)),
                  pl.BlockSpec((None, None, d, f), lambda i, te, nu: (layer, te[used(i, nu)], 0, 0)),
                  pl.BlockSpec((None, None, f, d), lambda i, te, nu: (layer, te[used(i, nu)], 0, 0))],
        out_specs=pl.BlockSpec((tr, d), lambda i, te, nu: (i, 0)),
    )
    return pl.pallas_call(
        _moe_kernel,
        grid_spec=grid_spec,
        out_shape=jax.ShapeDtypeStruct((n_rows, d), F32),
        compiler_params=_cparams("arbitrary"),
        name="moe_experts",
    )(tile_expert, n_used, xs, ffn_norm_w, w_gate, w_up, w_down)


def _combine_kernel(slot1_ref, slot2_ref, x_ref, cw_ref, nw_ref, y_hbm, o_ref, ybuf, sem, *, final_norm):
    i = pl.program_id(0)
    n = pl.num_programs(0)
    tm = x_ref.shape[0]

    def start_gather(tile, slot):
        base = tile * tm
        for r in range(tm):
            pltpu.make_async_copy(y_hbm.at[pl.ds(slot1_ref[base + r], 1)], ybuf.at[slot, 0, pl.ds(r, 1)],
                                  sem.at[slot]).start(priority=0)
            pltpu.make_async_copy(y_hbm.at[pl.ds(slot2_ref[base + r], 1)], ybuf.at[slot, 1, pl.ds(r, 1)],
                                  sem.at[slot]).start(priority=1)

    def compute(slot):
        for k in range(2):
            pltpu.make_async_copy(y_hbm.at[pl.ds(0, tm)], ybuf.at[slot, k], sem.at[slot]).wait()
        cw = cw_ref[...]
        out = x_ref[...] + cw[:, 0:1] * ybuf[slot, 0] + cw[:, 1:2] * ybuf[slot, 1]
        if final_norm:
            out = _rms(out, nw_ref[...])
        o_ref[...] = out

    @pl.when(i == 0)
    def _():
        start_gather(0, 0)

    for parity in range(2):
        @pl.when(i % 2 == parity)
        def _():
            @pl.when(i + 1 < n)
            def _():
                start_gather(i + 1, 1 - parity)

            compute(parity)


def _combine(xn, cw, norm_w, y_sorted, slot1, slot2, tm, final_norm):
    t, d = xn.shape
    grid_spec = pltpu.PrefetchScalarGridSpec(
        num_scalar_prefetch=2,
        grid=(t // tm,),
        in_specs=[pl.BlockSpec((tm, d), lambda i, s1, s2: (i, 0)),
                  pl.BlockSpec((tm, cw.shape[1]), lambda i, s1, s2: (i, 0)),
                  pl.BlockSpec(norm_w.shape, lambda i, s1, s2: (0, 0)),
                  pl.BlockSpec(memory_space=pl.ANY)],
        out_specs=pl.BlockSpec((tm, d), lambda i, s1, s2: (i, 0)),
        scratch_shapes=[pltpu.VMEM((2, 2, tm, d), F32), pltpu.SemaphoreType.DMA((2,))],
    )
    return pl.pallas_call(
        functools.partial(_combine_kernel, final_norm=final_norm),
        grid_spec=grid_spec,
        out_shape=jax.ShapeDtypeStruct((t, d), F32),
        compiler_params=_cparams("arbitrary"),
        name="moe_combine",
    )(slot1, slot2, xn, cw, norm_w, y_sorted)


def _pad_lanes(v):
    v = v.reshape(1, -1).astype(F32)
    return jnp.pad(v, ((0, 0), (0, LANES - v.shape[1])))


def kernel(x, attn_norm_w, w_in, swa_sink, ssd_conv_w, ssd_conv_b, ssd_dt_bias, ssd_a_log, ssd_d, ssd_norm_w,
           diff_lambda, diff_subln_w, w_out, ffn_norm_w, w_router_group, b_router_group, w_router_expert,
           b_router_expert, w_gate, w_up, w_down, final_norm_w):
    return _forward(x, attn_norm_w, w_in, swa_sink, ssd_conv_w, ssd_conv_b, ssd_dt_bias, ssd_a_log, ssd_d,
                    ssd_norm_w, diff_lambda, diff_subln_w, w_out, ffn_norm_w, w_router_group, b_router_group,
                    w_router_expert, b_router_expert, w_gate, w_up, w_down, final_norm_w)


def _forward(x, attn_norm_w, w_in, swa_sink, ssd_conv_w, ssd_conv_b, ssd_dt_bias, ssd_a_log, ssd_d, ssd_norm_w,
             diff_lambda, diff_subln_w, w_out, ffn_norm_w, w_router_group, b_router_group, w_router_expert,
             b_router_expert, w_gate, w_up, w_down, final_norm_w, tm=512, tq=512, tk=512, tmc=256):
    batch, seq, d = x.shape
    depth = w_in.shape[0]
    t = batch * seq
    tr = MOE_ROW_TILE
    n_tiles = (2 * t) // tr + N_EXPERTS
    slopes = jnp.exp2(-8.0 * jnp.arange(1, N_ALIBI_HEADS + 1, dtype=F32) / N_ALIBI_HEADS)
    swa_slopes, diff_slopes = slopes[:SWA_HEADS], slopes[SWA_HEADS:]

    sizes = [A_Q_DIM, A_KV_DIM, A_KV_DIM, SSD_D_INNER, SSD_CONV_DIM, SSD_DT_DIM, DIFF_QK_WIDTH, DIFF_QK_WIDTH,
             DIFF_V_WIDTH]
    offs = [0]
    for s in sizes:
        offs.append(offs[-1] + s)
    o_aq, o_ak, o_av, o_z, o_xbc, o_dt, o_cq, o_ck, o_cv, o_end = offs

    x2 = x.reshape(t, d)
    for l in range(depth):
        w = w_in[l]
        hw = 2 * DIFF_QK_DIM
        w_ck = jnp.pad(w[:, o_ck:o_cv].reshape(d, DIFF_HEADS, hw), ((0, 0), (0, 0), (0, LANES - hw)))
        w_main = jnp.concatenate(
            [w[:, o_ak:o_av], w[:, o_z:o_dt], w_ck.reshape(d, DIFF_HEADS * LANES), w[:, o_dt:o_cq],
             jnp.zeros((d, LANES - SSD_DT_DIM), w.dtype)], axis=1).astype(BF16)
        w_t = jnp.concatenate([w[:, o_aq:o_ak], w[:, o_av:o_z], w[:, o_cq:o_ck], w[:, o_cv:o_end]],
                              axis=1).T.astype(BF16)
        ak, z, xs, cm, ck, dt_raw, aqt, avt3, cqt, cvt, bt = _inproj(
            x2, attn_norm_w[l].reshape(1, d), w_main, w_t, ssd_conv_w[l].astype(F32),
            ssd_conv_b[l].reshape(1, -1).astype(F32), tm, seq, tk)

        ya = _swa(aqt, ak, avt3, swa_sink[l].astype(F32), swa_slopes, batch, seq)
        lambda_init = 0.8 - 0.6 * math.exp(-0.3 * l)
        yc = _diff(cqt, ck, cvt, diff_slopes * LOG2E, diff_lambda[l].astype(F32),
                   diff_subln_w[l].reshape(DIFF_V_DIM, 1).astype(F32), lambda_init, batch, seq, tq, tk)
        yf, yb = _ssd(xs, bt, cm, dt_raw, _pad_lanes(ssd_dt_bias[l]), _pad_lanes(ssd_a_log[l]),
                      _pad_lanes(ssd_d[l]), batch, seq)

        wrt = jnp.concatenate([w_router_expert[l], w_router_group[l],
                               jnp.zeros((d, SUBLANES - N_EXPERT_GROUPS), F32)], axis=1).T.astype(F32)
        br = jnp.concatenate([b_router_expert[l], b_router_group[l],
                              jnp.zeros((SUBLANES - N_EXPERT_GROUPS,), F32)]).reshape(-1, 1).astype(F32)
        xn, ri, rf, cnt = _outproj(x2, ya, yf, yb, z, ssd_norm_w[l].reshape(1, -1), yc, w_out[l].astype(BF16),
                                   ffn_norm_w[l].reshape(1, d), wrt, br, tm)

        counts = cnt[:, 0].astype(I32)
        padded = ((counts + tr - 1) // tr) * tr
        ends = jnp.cumsum(padded)
        starts = ends - padded
        experts = jnp.arange(N_EXPERTS, dtype=I32)[:, None]

        def slot_of(e, rank):
            return jnp.sum(jnp.where(e[None, :] == experts, starts[:, None], 0), axis=0) + rank

        slot1 = slot_of(ri[0], ri[2])
        slot2 = slot_of(ri[1], ri[3])
        tile_start = jnp.arange(n_tiles, dtype=I32) * tr
        tile_expert = jnp.minimum(jnp.sum(ends[None, :] <= tile_start[:, None], axis=1), N_EXPERTS - 1).astype(I32)
        n_used = (ends[-1] // tr).astype(I32).reshape(1)

        xs_sorted = _dispatch(xn, slot1, slot2, starts + counts, padded - counts, n_used, n_tiles * tr, tmc)
        y_sorted = _moe(xs_sorted, ffn_norm_w[l].reshape(1, d), w_gate, w_up, w_down, l, tile_expert, n_used)
        last = l == depth - 1
        x2 = _combine(xn, rf.T, final_norm_w.reshape(1, d), y_sorted, slot1, slot2, tmc, last)
    return x2.reshape(batch, seq, d)
```

```python
import functools
import math

import jax
import jax.numpy as jnp
from jax import lax
from jax.experimental import pallas as pl
from jax.experimental.pallas import tpu as pltpu

F32 = jnp.float32
BF16 = jnp.bfloat16
I32 = jnp.int32

HEAD_DIM = 64
SWA_HEADS = 6
SWA_KV_HEADS = 2
SWA_WINDOW = 128
SSD_HEADS = 6
SSD_HEAD_DIM = 64
SSD_GROUPS = 2
SSD_STATE = 64
SSD_CONV = 5
DIFF_HEADS = 4
DIFF_QK_DIM = 32
DIFF_V_DIM = 64
N_EXPERT_GROUPS = 4
EXPERTS_PER_GROUP = 8
N_EXPERTS = N_EXPERT_GROUPS * EXPERTS_PER_GROUP
NORM_EPS = 1e-6

A_Q_DIM = SWA_HEADS * HEAD_DIM
A_KV_DIM = SWA_KV_HEADS * HEAD_DIM
SSD_D_INNER = SSD_HEADS * SSD_HEAD_DIM
SSD_BC_DIM = SSD_GROUPS * SSD_STATE
SSD_CONV_DIM = SSD_D_INNER + 2 * SSD_BC_DIM
SSD_DT_DIM = 2 * SSD_HEADS
DIFF_QK_WIDTH = DIFF_HEADS * 2 * DIFF_QK_DIM
DIFF_V_WIDTH = DIFF_HEADS * DIFF_V_DIM
N_ALIBI_HEADS = SWA_HEADS + DIFF_HEADS

LANES = 128
SUBLANES = 8
VMEM_LIMIT = 56 * 1024 * 1024
NEG = -1e30
LOG2E = math.log2(math.e)

SSD_CHUNK = 128
SWA_BLOCKS_PER_STEP = 4
MOE_ROW_TILE = 256
DIFF_PAIR = 2 * 2 * DIFF_QK_DIM
DIFF_VROWS = 80
DIFF_NFEAT = 6


def _cparams(*sem):
    return pltpu.CompilerParams(dimension_semantics=sem, vmem_limit_bytes=VMEM_LIMIT)


def _rms(x, w):
    return x * lax.rsqrt(jnp.mean(x * x, axis=-1, keepdims=True) + NORM_EPS) * w


def _silu(x):
    return x / (1.0 + jnp.exp(-x))


def _softplus(x):
    return jnp.maximum(x, 0.0) + jnp.log(1.0 + jnp.exp(-jnp.abs(x)))


def _bf16_split(x):
    hi = x.astype(BF16).astype(F32)
    lo = (x - hi).astype(BF16).astype(F32)
    return hi, lo


_C_AK = 0
_C_Z = _C_AK + A_KV_DIM
_C_XBC = _C_Z + SSD_D_INNER
_C_CK = _C_XBC + SSD_CONV_DIM
_C_DT = _C_CK + DIFF_HEADS * LANES
_C_END = _C_DT + LANES
_R_AQ = 0
_R_AV = _R_AQ + A_Q_DIM
_R_CQ = _R_AV + A_KV_DIM
_R_CV = _R_CQ + DIFF_QK_WIDTH
_R_END = _R_CV + DIFF_V_WIDTH


def _inproj_kernel(x_ref, xp_ref, xn_ref, nw_ref, w_ref, wt_ref, cw_ref, cb_ref,
                   ak_ref, z_ref, xs_ref, cm_ref, ck_ref, dt_ref, aqt_ref, avt_ref, cqt_ref, cvt_ref, bt_ref,
                   *, tiles_per_seq, diff_key_tile):
    i = pl.program_id(0)
    tm = x_ref.shape[0]
    nw = nw_ref[...]
    h = _rms(x_ref[...], nw).astype(BF16)

    def seg(lo, hi):
        return jnp.dot(h, w_ref[:, lo:hi], preferred_element_type=F32)

    ak_ref[...] = seg(_C_AK, _C_Z).astype(BF16)
    z_ref[...] = seg(_C_Z, _C_XBC)
    pos = (i * tm + lax.broadcasted_iota(I32, (tm, _C_DT - _C_CK), 0)) % diff_key_tile
    ck_ref[...] = (seg(_C_CK, _C_DT) + _diff_key_features(pos)).astype(BF16)
    dt_ref[...] = seg(_C_DT, _C_END)

    w_xbc = w_ref[:, _C_XBC:_C_CK]
    first = i % tiles_per_seq == 0
    last = i % tiles_per_seq == tiles_per_seq - 1
    prev = jnp.dot(_rms(xp_ref[...], nw).astype(BF16), w_xbc, preferred_element_type=F32)
    nxt = jnp.dot(_rms(xn_ref[...], nw).astype(BF16), w_xbc, preferred_element_type=F32)
    prev = jnp.where(first, 0.0, prev)
    nxt = jnp.where(last, 0.0, nxt)
    ext = jnp.concatenate([prev, seg(_C_XBC, _C_CK), nxt], axis=0)
    half = SSD_CONV // 2
    conv = cb_ref[...]
    for k in range(SSD_CONV):
        off = SUBLANES - half + k
        conv = conv + cw_ref[k:k + 1, :] * ext[off:off + tm, :]
    u = _silu(conv)
    xs_ref[...] = u[:, :SSD_D_INNER]
    bt_ref[...] = jnp.transpose(u[:, SSD_D_INNER:SSD_D_INNER + SSD_BC_DIM])
    cm_ref[...] = u[:, SSD_D_INNER + SSD_BC_DIM:]

    tr = lax.dot_general(wt_ref[...], h, (((1,), (1,)), ((), ())), preferred_element_type=F32)
    aqt_ref[...] = (tr[_R_AQ:_R_AV] * (HEAD_DIM ** -0.5)).astype(BF16)
    avt = tr[_R_AV:_R_CQ].astype(BF16)
    for c in range(tm // LANES):
        avt_ref[c] = avt[:, c * LANES:(c + 1) * LANES]
    cqt_ref[...] = (tr[_R_CQ:_R_CV] * (DIFF_QK_DIM ** -0.5 * LOG2E)).astype(BF16)
    pad = DIFF_VROWS - DIFF_V_DIM
    ones_row = (lax.broadcasted_iota(I32, (pad, tm), 0) == 0).astype(BF16)
    for hh in range(DIFF_HEADS):
        cvt_ref[hh * DIFF_VROWS:hh * DIFF_VROWS + DIFF_V_DIM, :] = (
            tr[_R_CV + hh * DIFF_V_DIM:_R_CV + (hh + 1) * DIFF_V_DIM].astype(BF16))
        cvt_ref[hh * DIFF_VROWS + DIFF_V_DIM:(hh + 1) * DIFF_VROWS, :] = ones_row


def _inproj(x2, norm_w, w_main, w_t, conv_w, conv_b, tm, seq, diff_key_tile):
    t, d = x2.shape
    hb = tm // SUBLANES
    n_hblk = t // SUBLANES
    row = lambda i: (i, 0)
    col = lambda i: (0, i)
    full = lambda i: (0, 0)
    row_outs = [(A_KV_DIM, BF16), (SSD_D_INNER, F32), (SSD_D_INNER, F32), (SSD_BC_DIM, F32),
                (DIFF_HEADS * LANES, BF16), (LANES, F32)]
    out_shape = [jax.ShapeDtypeStruct((t, w), dt) for w, dt in row_outs]
    out_specs = [pl.BlockSpec((tm, w), row) for w, _ in row_outs]
    out_shape += [jax.ShapeDtypeStruct((A_Q_DIM, t), BF16),
                  jax.ShapeDtypeStruct((t // LANES, A_KV_DIM, LANES), BF16),
                  jax.ShapeDtypeStruct((DIFF_QK_WIDTH, t), BF16),
                  jax.ShapeDtypeStruct((DIFF_HEADS * DIFF_VROWS, t), BF16),
                  jax.ShapeDtypeStruct((SSD_BC_DIM, t), F32)]
    out_specs += [pl.BlockSpec((A_Q_DIM, tm), col),
                  pl.BlockSpec((tm // LANES, A_KV_DIM, LANES), lambda i: (i, 0, 0)),
                  pl.BlockSpec((DIFF_QK_WIDTH, tm), col),
                  pl.BlockSpec((DIFF_HEADS * DIFF_VROWS, tm), col),
                  pl.BlockSpec((SSD_BC_DIM, tm), col)]
    return pl.pallas_call(
        functools.partial(_inproj_kernel, tiles_per_seq=seq // tm, diff_key_tile=diff_key_tile),
        grid=(t // tm,),
        in_specs=[pl.BlockSpec((tm, d), row),
                  pl.BlockSpec((SUBLANES, d), lambda i: (jnp.maximum(i * hb - 1, 0), 0)),
                  pl.BlockSpec((SUBLANES, d), lambda i: (jnp.minimum((i + 1) * hb, n_hblk - 1), 0)),
                  pl.BlockSpec((1, d), full),
                  pl.BlockSpec(w_main.shape, full), pl.BlockSpec(w_t.shape, full),
                  pl.BlockSpec(conv_w.shape, full), pl.BlockSpec(conv_b.shape, full)],
        out_specs=out_specs,
        out_shape=out_shape,
        compiler_params=_cparams("parallel"),
        name="inproj",
    )(x2, x2, x2, norm_w, w_main, w_t, conv_w, conv_b)


def _swa_kernel(sink_ref, slope_ref, qt_ref, k_ref, vt_ref, o_ref):
    step = pl.program_id(1)
    s_len = k_ref.shape[0]
    blk = SWA_WINDOW
    band = 3 * blk
    nb = s_len // blk
    rep = SWA_HEADS // SWA_KV_HEADS
    hd = HEAD_DIM
    for u in range(SWA_BLOCKS_PER_STEP):
        n = step * SWA_BLOCKS_PER_STEP + u
        start_blk = jnp.clip(n - 1, 0, nb - 3)
        start = pl.multiple_of(start_blk * blk, blk)
        kb = k_ref[pl.ds(start, band), :]
        v3 = vt_ref[pl.ds(start_blk, 3)]
        vtb = jnp.concatenate([v3[0], v3[1], v3[2]], axis=1)
        qt = qt_ref[:, u * blk:(u + 1) * blk]
        zero = jnp.zeros((hd, rep * blk), BF16)
        grp = [jnp.concatenate([qt[(g * rep + r) * hd:(g * rep + r + 1) * hd] for r in range(rep)], axis=1)
               for g in range(SWA_KV_HEADS)]
        qbd = jnp.concatenate([jnp.concatenate([grp[0], zero], axis=1),
                               jnp.concatenate([zero, grp[1]], axis=1)], axis=0)
        st = jnp.dot(kb, qbd, preferred_element_type=F32)
        kpos = start + lax.broadcasted_iota(I32, (band, blk), 0)
        qpos = n * blk + lax.broadcasted_iota(I32, (band, blk), 1)
        dist_i = jnp.abs(qpos - kpos)
        valid = dist_i <= SWA_WINDOW
        dist = dist_i.astype(F32)
        ps, inv = [], []
        for h in range(SWA_HEADS):
            s = jnp.where(valid, st[:, h * blk:(h + 1) * blk] - slope_ref[h] * dist, NEG)
            sink = sink_ref[h]
            m = jnp.maximum(jnp.max(s, axis=0, keepdims=True), sink)
            p = jnp.exp(s - m)
            inv.append(1.0 / (jnp.sum(p, axis=0, keepdims=True) + jnp.exp(sink - m)))
            ps.append(p.astype(BF16))
        outs = []
        for g in range(SWA_KV_HEADS):
            pg = jnp.concatenate(ps[g * rep:(g + 1) * rep], axis=1)
            og = jnp.dot(vtb[g * hd:(g + 1) * hd, :], pg, preferred_element_type=F32)
            for r in range(rep):
                outs.append(og[:, r * blk:(r + 1) * blk] * inv[g * rep + r])
        o_ref[u * blk:(u + 1) * blk, :] = jnp.transpose(jnp.concatenate(outs, axis=0)).astype(o_ref.dtype)


def _swa(aqt, ak, avt3, sink, slopes, batch, seq):
    blk = SWA_WINDOW
    rows = blk * SWA_BLOCKS_PER_STEP
    steps = seq // rows
    nb = seq // blk
    t = batch * seq
    smem = pl.BlockSpec(memory_space=pltpu.SMEM)
    return pl.pallas_call(
        _swa_kernel,
        grid=(batch, steps),
        in_specs=[smem, smem,
                  pl.BlockSpec((A_Q_DIM, rows), lambda b, s: (0, b * steps + s)),
                  pl.BlockSpec((seq, A_KV_DIM), lambda b, s: (b, 0)),
                  pl.BlockSpec((nb, A_KV_DIM, blk), lambda b, s: (b, 0, 0))],
        out_specs=pl.BlockSpec((rows, A_Q_DIM), lambda b, s: (b * steps + s, 0)),
        out_shape=jax.ShapeDtypeStruct((t, A_Q_DIM), BF16),
        compiler_params=_cparams("parallel", "parallel"),
        name="swa",
    )(sink, slopes, aqt, ak, avt3)


def _diff_key_tile(i, j, tq, tk, nk):
    return ((i * tq) // tk + j) % nk


def _diff_key_features(pos_in_tile):
    lane = lax.broadcasted_iota(I32, pos_in_tile.shape, 1) % LANES - 2 * DIFF_QK_DIM
    coarse = ((pos_in_tile // 16) * 16).astype(F32)
    fine = (pos_in_tile % 16).astype(F32)
    f = lane % DIFF_NFEAT
    feat = jnp.where(f < 2, coarse, jnp.where(f < 4, fine, 1.0))
    return jnp.where((lane >= 0) & (lane < 2 * DIFF_NFEAT), feat, 0.0)


def _diff_kernel(slope_ref, qt_ref, k_ref, vt_ref, lam_ref, sw_ref, o_ref, qtb_ref, m_ref, acc_ref,
                 s_ref, p_ref, *, lambda_init):
    i = pl.program_id(1)
    j = pl.program_id(2)
    nk = pl.num_programs(2)
    tq = qt_ref.shape[1]
    tk = k_ref.shape[0]
    dq = DIFF_QK_DIM
    hw = 2 * dq
    nf = DIFF_NFEAT

    @pl.when(j == 0)
    def _():
        m_ref[...] = jnp.full(m_ref.shape, NEG, F32)
        acc_ref[...] = jnp.zeros(acc_ref.shape, F32)
        ii = lax.broadcasted_iota(I32, (1, 2 * tq), 1)
        ii = jnp.where(ii >= tq, ii - tq, ii).astype(F32)
        qt = qt_ref[...]
        col = lax.broadcasted_iota(I32, (hw, 2 * tq), 1)
        row = lax.broadcasted_iota(I32, (hw, 2 * tq), 0)
        own_map = row // dq == col // tq
        for h in range(DIFF_HEADS):
            qh = qt[h * hw:(h + 1) * hw, :]
            qh2 = jnp.where(own_map, jnp.concatenate([qh, qh], axis=1), jnp.zeros((hw, 2 * tq), BF16))
            sl = jnp.full((1, 2 * tq), slope_ref[h], F32)
            s_hi, s_lo = _bf16_split(sl)
            v_hi, v_lo = _bf16_split(-sl * ii)
            rows = jnp.concatenate([s_hi, s_lo, s_hi, s_lo, v_hi, v_lo], axis=0)
            zrow = jnp.zeros((nf, 2 * tq), F32)
            zero = jnp.zeros((hw - 2 * nf, 2 * tq), F32)
            variants = ([rows, zrow], [zrow, -rows], [zrow, zrow])
            for v, pieces in enumerate(variants):
                qtb_ref[v, h, 0:hw, :] = qh2
                qtb_ref[v, h, hw:2 * hw, :] = jnp.concatenate(pieces + [zero], axis=0).astype(BF16)

    q0 = i * tq
    k0 = _diff_key_tile(i, j, tq, tk, nk) * tk

    def scores(h, variant):
        return jnp.dot(k_ref[:, h * LANES:(h + 1) * LANES], qtb_ref[variant, h],
                       preferred_element_type=F32)

    def stage_scores(h, s):
        s_ref[h] = s
        return jnp.max(s, axis=0, keepdims=True)

    def softmax_step(h, tile_max, shift):
        m_old = m_ref[h:h + 1, :]
        m_new = jnp.maximum(m_old, tile_max + shift)
        p_ref[h] = jnp.exp2(s_ref[h] - (m_new - shift)).astype(BF16)
        m_ref[h:h + 1, :] = m_new
        return jnp.exp2(m_old - m_new)

    def accumulate(h, alpha):
        pv = jnp.dot(vt_ref[h * DIFF_VROWS:(h + 1) * DIFF_VROWS, :], p_ref[h], preferred_element_type=F32)
        acc_ref[h] = alpha * acc_ref[h] + pv

    @pl.when(j == 0)
    def _():
        kpos = k0 + lax.broadcasted_iota(I32, (tk, tq), 0)
        qpos = q0 + lax.broadcasted_iota(I32, (tk, tq), 1)
        dist = jnp.abs(qpos - kpos).astype(F32)
        maxima = []
        for h in range(DIFF_HEADS):
            bias = slope_ref[h] * dist
            maxima.append(stage_scores(h, scores(h, 2) - jnp.concatenate([bias, bias], axis=1)))
        alphas = [softmax_step(h, maxima[h], 0.0) for h in range(DIFF_HEADS)]
        for h in range(DIFF_HEADS):
            accumulate(h, alphas[h])

    @pl.when(j > 0)
    def _():
        below = k0 < q0
        variant = jnp.where(below, 0, 1)
        sign = jnp.where(below, 1.0, -1.0)
        maxima = [stage_scores(h, scores(h, variant)) for h in range(DIFF_HEADS)]
        alphas = [softmax_step(h, maxima[h], sign * slope_ref[h] * (k0 - q0).astype(F32))
                  for h in range(DIFF_HEADS)]
        for h in range(DIFF_HEADS):
            accumulate(h, alphas[h])

    @pl.when(j == nk - 1)
    def _():
        lp = lam_ref[...]
        lam = (jnp.exp(jnp.sum(lp[0:1] * lp[1:2], axis=-1, keepdims=True))
               - jnp.exp(jnp.sum(lp[2:3] * lp[3:4], axis=-1, keepdims=True)) + lambda_init)
        outs = []
        for h in range(DIFF_HEADS):
            a = acc_ref[h]
            o = a[0:DIFF_V_DIM] / a[DIFF_V_DIM:DIFF_V_DIM + 1]
            o = o[:, 0:tq] - lam * o[:, tq:2 * tq]
            ms = jnp.mean(o * o, axis=0, keepdims=True)
            outs.append(o * lax.rsqrt(ms + NORM_EPS) * sw_ref[...] * (1.0 - lambda_init))
        o_ref[...] = jnp.transpose(jnp.concatenate(outs, axis=0)).astype(o_ref.dtype)


def _diff(cqt, ck, cvt, slopes, lam_params, subln_w_col, lambda_init, batch, seq, tq, tk):
    assert tk % tq == 0 and seq % tk == 0
    nq, nk = seq // tq, seq // tk
    t = batch * seq
    smem = pl.BlockSpec(memory_space=pltpu.SMEM)
    key_tile = functools.partial(_diff_key_tile, tq=tq, tk=tk, nk=nk)
    return pl.pallas_call(
        functools.partial(_diff_kernel, lambda_init=lambda_init),
        grid=(batch, nq, nk),
        in_specs=[smem,
                  pl.BlockSpec((DIFF_QK_WIDTH, tq), lambda b, i, j: (0, b * nq + i)),
                  pl.BlockSpec((tk, DIFF_HEADS * LANES), lambda b, i, j: (b * nk + key_tile(i, j), 0)),
                  pl.BlockSpec((DIFF_HEADS * DIFF_VROWS, tk), lambda b, i, j: (0, b * nk + key_tile(i, j))),
                  pl.BlockSpec(lam_params.shape, lambda b, i, j: (0, 0)),
                  pl.BlockSpec(subln_w_col.shape, lambda b, i, j: (0, 0))],
        out_specs=pl.BlockSpec((tq, DIFF_V_WIDTH), lambda b, i, j: (b * nq + i, 0)),
        out_shape=jax.ShapeDtypeStruct((t, DIFF_V_WIDTH), BF16),
        scratch_shapes=[pltpu.VMEM((3, DIFF_HEADS, LANES, 2 * tq), BF16),
                        pltpu.VMEM((DIFF_HEADS, 2 * tq), F32),
                        pltpu.VMEM((DIFF_HEADS, DIFF_VROWS, 2 * tq), F32),
                        pltpu.VMEM((DIFF_HEADS, tk, 2 * tq), F32),
                        pltpu.VMEM((DIFF_HEADS, tk, 2 * tq), BF16)],
        compiler_params=_cparams("parallel", "parallel", "arbitrary"),
        name="diffattn",
    )(slopes, cqt, ck, cvt, lam_params, subln_w_col)


def _ssd_direction(fwd, xs_ref, bt_ref, cm_ref, dt_ref, dtb_ref, alog_ref, dsk_ref, y_ref, state_ref):
    q = xs_ref.shape[0]
    lane0 = 0 if fwd else SSD_HEADS
    dt_all = _softplus(dt_ref[...] + dtb_ref[...])
    dta_all = dt_all * -jnp.exp(alog_ref[...])
    row = lax.broadcasted_iota(I32, (q, q), 0)
    col = lax.broadcasted_iota(I32, (q, q), 1)
    keep = (row >= col) if fwd else (row <= col)
    da_all = jnp.dot(keep.astype(F32), dta_all, precision=lax.Precision.HIGHEST,
                     preferred_element_type=F32)
    da_all_t = jnp.transpose(da_all)
    dt_all_t = jnp.transpose(dt_all)
    tot_all = jnp.sum(dta_all, axis=0, keepdims=True)

    xs = xs_ref[...]
    bmt = bt_ref[...]
    cm = cm_ref[...]
    rep = SSD_HEADS // SSD_GROUPS
    ns = SSD_STATE
    g_mats = [jnp.dot(cm[:, g * ns:(g + 1) * ns].astype(BF16), bmt[g * ns:(g + 1) * ns, :].astype(BF16),
                      preferred_element_type=F32) for g in range(SSD_GROUPS)]
    ys = []
    for h in range(SSD_HEADS):
        g = h // rep
        ln = lane0 + h
        dac = da_all[:, ln:ln + 1]
        dar = da_all_t[ln:ln + 1, :]
        dtr = dt_all_t[ln:ln + 1, :]
        tot = tot_all[:, ln:ln + 1]
        dac_b = jnp.broadcast_to(dac, (q, q))
        decay = jnp.exp(jnp.where(keep, dac_b - dar, NEG))
        xh = xs[:, h * SSD_HEAD_DIM:(h + 1) * SSD_HEAD_DIM]
        xhb = xh.astype(BF16)
        y = jnp.dot((g_mats[g] * decay * dtr).astype(BF16), xhb, preferred_element_type=F32)
        st = state_ref[h]
        c_in = (cm[:, g * ns:(g + 1) * ns] * jnp.exp(dac_b[:, 0:ns])).astype(BF16)
        y = y + jnp.dot(c_in, st.astype(BF16), preferred_element_type=F32)
        to_end = jnp.exp(tot - dar) * dtr
        b_out = (bmt[g * ns:(g + 1) * ns, :] * to_end).astype(BF16)
        state_ref[h] = st * jnp.exp(tot) + jnp.dot(b_out, xhb, preferred_element_type=F32)
        if fwd:
            y = y + dsk_ref[:, h:h + 1] * xh
        ys.append(y)
    y_ref[...] = jnp.concatenate(ys, axis=-1)


def _ssd_kernel(xsf_ref, btf_ref, cmf_ref, dtf_ref, xsb_ref, btb_ref, cmb_ref, dtb_in_ref,
                dtbias_ref, alog_ref, dsk_ref, yf_ref, yb_ref, state_ref):
    @pl.when(pl.program_id(1) == 0)
    def _():
        state_ref[...] = jnp.zeros(state_ref.shape, F32)

    _ssd_direction(True, xsf_ref, btf_ref, cmf_ref, dtf_ref, dtbias_ref, alog_ref, dsk_ref, yf_ref,
                   state_ref.at[0])
    _ssd_direction(False, xsb_ref, btb_ref, cmb_ref, dtb_in_ref, dtbias_ref, alog_ref, dsk_ref, yb_ref,
                   state_ref.at[1])


def _ssd(xs, bt, cm, dt_raw, dt_bias, a_log, d_skip, batch, seq):
    q = SSD_CHUNK
    nc = seq // q
    t = batch * seq
    full = lambda b, c: (0, 0)
    fw = lambda b, c: b * nc + c
    bw = lambda b, c: b * nc + nc - 1 - c

    def specs(idx):
        return [pl.BlockSpec((q, SSD_D_INNER), lambda b, c: (idx(b, c), 0)),
                pl.BlockSpec((SSD_BC_DIM, q), lambda b, c: (0, idx(b, c))),
                pl.BlockSpec((q, SSD_BC_DIM), lambda b, c: (idx(b, c), 0)),
                pl.BlockSpec((q, LANES), lambda b, c: (idx(b, c), 0))]

    return pl.pallas_call(
        _ssd_kernel,
        grid=(batch, nc),
        in_specs=specs(fw) + specs(bw) + [pl.BlockSpec(dt_bias.shape, full), pl.BlockSpec(a_log.shape, full),
                                          pl.BlockSpec(d_skip.shape, full)],
        out_specs=[pl.BlockSpec((q, SSD_D_INNER), lambda b, c: (fw(b, c), 0)),
                   pl.BlockSpec((q, SSD_D_INNER), lambda b, c: (bw(b, c), 0))],
        out_shape=[jax.ShapeDtypeStruct((t, SSD_D_INNER), F32), jax.ShapeDtypeStruct((t, SSD_D_INNER), F32)],
        scratch_shapes=[pltpu.VMEM((2, SSD_HEADS, SSD_STATE, SSD_HEAD_DIM), F32)],
        compiler_params=_cparams("parallel", "arbitrary"),
        name="ssd",
    )(xs, bt, cm, dt_raw, xs, bt, cm, dt_raw, dt_bias, a_log, d_skip)


def _outproj_kernel(x_ref, ya_ref, yf_ref, yb_ref, z_ref, snw_ref, yc_ref, wo_ref, fnw_ref, wrt_ref, br_ref,
                    xn_ref, ri_ref, rf_ref, cnt_ref, tri_ref, carry_ref):
    step = pl.program_id(0)
    tm = x_ref.shape[0]

    @pl.when(step == 0)
    def _():
        carry_ref[...] = jnp.zeros(carry_ref.shape, F32)
        r = lax.broadcasted_iota(I32, (tm, tm), 0)
        cc = lax.broadcasted_iota(I32, (tm, tm), 1)
        tri_ref[...] = (r <= cc).astype(BF16)

    y = (yf_ref[...] + yb_ref[...]) * _silu(z_ref[...])
    yb = _rms(y, snw_ref[...]).astype(BF16)
    acc = jnp.dot(ya_ref[...], wo_ref[0:A_Q_DIM, :], preferred_element_type=F32)
    acc = acc + jnp.dot(yb, wo_ref[A_Q_DIM:A_Q_DIM + SSD_D_INNER, :], preferred_element_type=F32)
    acc = acc + jnp.dot(yc_ref[...], wo_ref[A_Q_DIM + SSD_D_INNER:, :], preferred_element_type=F32)
    xn = x_ref[...] + acc
    xn_ref[...] = xn

    h = _rms(xn, fnw_ref[...])
    logits = lax.dot_general(wrt_ref[...], h, (((1,), (1,)), ((), ())), precision=lax.Precision.HIGHEST,
                             preferred_element_type=F32) + br_ref[...]
    ne, epg, ng = N_EXPERTS, EXPERTS_PER_GROUP, N_EXPERT_GROUPS
    gl = logits[ne:ne + ng, :]
    gmax = jnp.max(gl, axis=0, keepdims=True)
    g_sel = jnp.full((1, tm), float(ng - 1), F32)
    for g in range(ng - 2, -1, -1):
        g_sel = jnp.where(gl[g:g + 1, :] == gmax, float(g), g_sel)
    g_gate = 1.0 / jnp.sum(jnp.exp(gl - gmax), axis=0, keepdims=True)
    e_in = logits[0:epg, :]
    for g in range(1, ng):
        e_in = jnp.where(g_sel == float(g), logits[g * epg:(g + 1) * epg, :], e_in)
    sub = lax.broadcasted_iota(I32, (epg, tm), 0).astype(F32)
    m1 = jnp.max(e_in, axis=0, keepdims=True)
    i1 = jnp.min(jnp.where(e_in == m1, sub, float(epg)), axis=0, keepdims=True)
    rest = jnp.where(sub == i1, NEG, e_in)
    m2 = jnp.max(rest, axis=0, keepdims=True)
    i2 = jnp.min(jnp.where(rest == m2, sub, float(epg)), axis=0, keepdims=True)
    r = jnp.exp(m2 - m1)
    c1 = g_gate / (1.0 + r)
    c2 = g_gate * r / (1.0 + r)
    e1 = (g_sel * epg + i1).astype(I32)
    e2 = (g_sel * epg + i2).astype(I32)

    erow = lax.broadcasted_iota(I32, (ne, tm), 0)
    hit1 = erow == e1
    hit2 = erow == e2
    oh = jnp.where(hit1 | hit2, 1.0, 0.0)
    incl = jnp.dot(oh.astype(BF16), tri_ref[...], preferred_element_type=F32)
    before = incl - oh + carry_ref[:, 0:1]
    rank1 = jnp.sum(jnp.where(hit1, before, 0.0), axis=0, keepdims=True)
    rank2 = jnp.sum(jnp.where(hit2, before, 0.0), axis=0, keepdims=True)
    carry_ref[...] = carry_ref[...] + jnp.sum(oh, axis=1, keepdims=True)
    cnt_ref[...] = carry_ref[...]
    zi = jnp.zeros((1, tm), I32)
    ri_ref[...] = jnp.concatenate([e1, e2, rank1.astype(I32), rank2.astype(I32), zi, zi, zi, zi], axis=0)
    zf = jnp.zeros((1, tm), F32)
    rf_ref[...] = jnp.concatenate([c1, c2, zf, zf, zf, zf, zf, zf], axis=0)


def _outproj(x2, ya, yf, yb, z, ssd_norm_w, yc, w_out, ffn_norm_w, wrt, br, tm):
    t, d = x2.shape
    row = lambda i: (i, 0)
    full = lambda i: (0, 0)
    return pl.pallas_call(
        _outproj_kernel,
        grid=(t // tm,),
        in_specs=[pl.BlockSpec((tm, d), row),
                  pl.BlockSpec((tm, A_Q_DIM), row),
                  pl.BlockSpec((tm, SSD_D_INNER), row),
                  pl.BlockSpec((tm, SSD_D_INNER), row),
                  pl.BlockSpec((tm, SSD_D_INNER), row),
                  pl.BlockSpec(ssd_norm_w.shape, full),
                  pl.BlockSpec((tm, DIFF_V_WIDTH), row),
                  pl.BlockSpec(w_out.shape, full),
                  pl.BlockSpec(ffn_norm_w.shape, full),
                  pl.BlockSpec(wrt.shape, full),
                  pl.BlockSpec(br.shape, full)],
        out_specs=[pl.BlockSpec((tm, d), row),
                   pl.BlockSpec((SUBLANES, tm), lambda i: (0, i)),
                   pl.BlockSpec((SUBLANES, tm), lambda i: (0, i)),
                   pl.BlockSpec((N_EXPERTS, LANES), full)],
        out_shape=[jax.ShapeDtypeStruct((t, d), F32),
                   jax.ShapeDtypeStruct((SUBLANES, t), I32),
                   jax.ShapeDtypeStruct((SUBLANES, t), F32),
                   jax.ShapeDtypeStruct((N_EXPERTS, LANES), F32)],
        scratch_shapes=[pltpu.VMEM((tm, tm), BF16), pltpu.VMEM((N_EXPERTS, LANES), F32)],
        compiler_params=_cparams("arbitrary"),
        name="outproj_router",
    )(x2, ya, yf, yb, z, ssd_norm_w, yc, w_out, ffn_norm_w, wrt, br)


_PAD_PIECES = tuple(1 << b for b in reversed(range(MOE_ROW_TILE.bit_length() - 1)))


def _dispatch_kernel(slot1_ref, slot2_ref, pstart_ref, plen_ref, nused_ref, x_ref, xs_hbm, zero_ref, sem):
    i = pl.program_id(0)
    tm = x_ref.shape[0]

    @pl.when(i == 0)
    def _():
        zero_ref[...] = jnp.zeros(zero_ref.shape, F32)

        def pieces(e, wait):
            n = plen_ref[e]
            first = pstart_ref[e]
            off = first + n
            for b in _PAD_PIECES:
                off = off - (n & b)
                dst = pl.ds(pl.multiple_of(off, b), b) if b >= SUBLANES else None
                if dst is not None:
                    @pl.when((n & b) != 0)
                    def _():
                        cp = pltpu.make_async_copy(zero_ref.at[pl.ds(0, b)], xs_hbm.at[dst], sem.at[1])
                        cp.wait() if wait else cp.start()

            for u in range(SUBLANES - 1):
                @pl.when(u < (n & (SUBLANES - 1)))
                def _():
                    cp = pltpu.make_async_copy(zero_ref.at[pl.ds(0, 1)], xs_hbm.at[pl.ds(first + u, 1)],
                                               sem.at[1])
                    cp.wait() if wait else cp.start()

        def tail(tile, wait):
            big = _PAD_PIECES[0]
            for part in range(MOE_ROW_TILE // big):
                dst = xs_hbm.at[pl.ds(pl.multiple_of(tile * MOE_ROW_TILE + part * big, big), big)]
                cp = pltpu.make_async_copy(zero_ref, dst, sem.at[1])
                cp.wait() if wait else cp.start()

        def loop(fn, lo, hi, wait):
            def body(k, carry):
                fn(k, wait)
                return carry

            lax.fori_loop(lo, hi, body, 0)

        n_tiles = xs_hbm.shape[0] // MOE_ROW_TILE
        for wait in (False, True):
            loop(pieces, 0, N_EXPERTS, wait)
            loop(tail, nused_ref[0], n_tiles, wait)

    base = i * tm
    for r in range(tm):
        src = x_ref.at[pl.ds(r, 1)]
        pltpu.make_async_copy(src, xs_hbm.at[pl.ds(slot1_ref[base + r], 1)], sem.at[0]).start(priority=0)
        pltpu.make_async_copy(src, xs_hbm.at[pl.ds(slot2_ref[base + r], 1)], sem.at[0]).start(priority=1)
    for _ in range(2):
        pltpu.make_async_copy(x_ref, xs_hbm.at[pl.ds(0, tm)], sem.at[0]).wait()


def _dispatch(xn, slot1, slot2, pad_start, pad_len, n_used, n_rows, tm):
    t, d = xn.shape
    grid_spec = pltpu.PrefetchScalarGridSpec(
        num_scalar_prefetch=5,
        grid=(t // tm,),
        in_specs=[pl.BlockSpec((tm, d), lambda i, s1, s2, ps, pn, nu: (i, 0))],
        out_specs=pl.BlockSpec(memory_space=pl.ANY),
        scratch_shapes=[pltpu.VMEM((_PAD_PIECES[0], d), F32), pltpu.SemaphoreType.DMA((2,))],
    )
    return pl.pallas_call(
        _dispatch_kernel,
        grid_spec=grid_spec,
        out_shape=jax.ShapeDtypeStruct((n_rows, d), F32),
        compiler_params=_cparams("arbitrary"),
        name="moe_dispatch",
    )(slot1, slot2, pad_start, pad_len, n_used, xn)


def _moe_kernel(texp_ref, nused_ref, x_ref, fnw_ref, wg_ref, wu_ref, wd_ref, y_ref):
    i = pl.program_id(0)

    @pl.when(i < nused_ref[0])
    def _():
        h = _rms(x_ref[...], fnw_ref[...]).astype(BF16)
        hg = jnp.dot(h, wg_ref[...].astype(BF16), preferred_element_type=F32)
        hu = jnp.dot(h, wu_ref[...].astype(BF16), preferred_element_type=F32)
        act = (_silu(hg) * hu).astype(BF16)
        y_ref[...] = jnp.dot(act, wd_ref[...].astype(BF16), preferred_element_type=F32)

    @pl.when(i >= nused_ref[0])
    def _():
        y_ref[...] = jnp.zeros(y_ref.shape, F32)


def _moe(xs, ffn_norm_w, w_gate, w_up, w_down, layer, tile_expert, n_used):
    n_rows, d = xs.shape
    f = w_gate.shape[-1]
    tr = MOE_ROW_TILE

    def used(i, nu):
        return jnp.maximum(jnp.minimum(i, nu[0] - 1), 0)

    grid_spec = pltpu.PrefetchScalarGridSpec(
        num_scalar_prefetch=2,
        grid=(n_rows // tr,),
        in_specs=[pl.BlockSpec((tr, d), lambda i, te, nu: (used(i, nu), 0)),
                  pl.BlockSpec(ffn_norm_w.shape, lambda i, te, nu: (0, 0)),
                  pl.BlockSpec((None, None, d, f), lambda i, te, nu: (layer, te[used(i, nu)], 0, 0)),
                  pl.BlockSpec((None, None, d, f), lambda i, te, nu: (layer, te[used(i, nu)], 0, 0)),
                  pl.BlockSpec((None, None, f, d), lambda i, te, nu: (layer, te[used(i, nu)], 0, 0))],
        out_specs=pl.BlockSpec((tr, d), lambda i, te, nu: (i, 0)),
    )
    return pl.pallas_call(
        _moe_kernel,
        grid_spec=grid_spec,
        out_shape=jax.ShapeDtypeStruct((n_rows, d), F32),
        compiler_params=_cparams("arbitrary"),
        name="moe_experts",
    )(tile_expert, n_used, xs, ffn_norm_w, w_gate, w_up, w_down)


def _combine_kernel(slot1_ref, slot2_ref, x_ref, cw_ref, nw_ref, y_hbm, o_ref, ybuf, sem, *, final_norm):
    i = pl.program_id(0)
    n = pl.num_programs(0)
    tm = x_ref.shape[0]

    def start_gather(tile, slot):
        base = tile * tm
        for r in range(tm):
            pltpu.make_async_copy(y_hbm.at[pl.ds(slot1_ref[base + r], 1)], ybuf.at[slot, 0, pl.ds(r, 1)],
                                  sem.at[slot]).start(priority=0)
            pltpu.make_async_copy(y_hbm.at[pl.ds(slot2_ref[base + r], 1)], ybuf.at[slot, 1, pl.ds(r, 1)],
                                  sem.at[slot]).start(priority=1)

    def compute(slot):
        for k in range(2):
            pltpu.make_async_copy(y_hbm.at[pl.ds(0, tm)], ybuf.at[slot, k], sem.at[slot]).wait()
        cw = cw_ref[...]
        out = x_ref[...] + cw[:, 0:1] * ybuf[slot, 0] + cw[:, 1:2] * ybuf[slot, 1]
        if final_norm:
            out = _rms(out, nw_ref[...])
        o_ref[...] = out

    @pl.when(i == 0)
    def _():
        start_gather(0, 0)

    for parity in range(2):
        @pl.when(i % 2 == parity)
        def _():
            @pl.when(i + 1 < n)
            def _():
                start_gather(i + 1, 1 - parity)

            compute(parity)


def _combine(xn, cw, norm_w, y_sorted, slot1, slot2, tm, final_norm):
    t, d = xn.shape
    grid_spec = pltpu.PrefetchScalarGridSpec(
        num_scalar_prefetch=2,
        grid=(t // tm,),
        in_specs=[pl.BlockSpec((tm, d), lambda i, s1, s2: (i, 0)),
                  pl.BlockSpec((tm, cw.shape[1]), lambda i, s1, s2: (i, 0)),
                  pl.BlockSpec(norm_w.shape, lambda i, s1, s2: (0, 0)),
                  pl.BlockSpec(memory_space=pl.ANY)],
        out_specs=pl.BlockSpec((tm, d), lambda i, s1, s2: (i, 0)),
        scratch_shapes=[pltpu.VMEM((2, 2, tm, d), F32), pltpu.SemaphoreType.DMA((2,))],
    )
    return pl.pallas_call(
        functools.partial(_combine_kernel, final_norm=final_norm),
        grid_spec=grid_spec,
        out_shape=jax.ShapeDtypeStruct((t, d), F32),
        compiler_params=_cparams("arbitrary"),
        name="moe_combine",
    )(slot1, slot2, xn, cw, norm_w, y_sorted)


def _pad_lanes(v):
    v = v.reshape(1, -1).astype(F32)
    return jnp.pad(v, ((0, 0), (0, LANES - v.shape[1])))


def kernel(x, attn_norm_w, w_in, swa_sink, ssd_conv_w, ssd_conv_b, ssd_dt_bias, ssd_a_log, ssd_d, ssd_norm_w,
           diff_lambda, diff_subln_w, w_out, ffn_norm_w, w_router_group, b_router_group, w_router_expert,
           b_router_expert, w_gate, w_up, w_down, final_norm_w):
    return _forward(x, attn_norm_w, w_in, swa_sink, ssd_conv_w, ssd_conv_b, ssd_dt_bias, ssd_a_log, ssd_d,
                    ssd_norm_w, diff_lambda, diff_subln_w, w_out, ffn_norm_w, w_router_group, b_router_group,
                    w_router_expert, b_router_expert, w_gate, w_up, w_down, final_norm_w)


def _forward(x, attn_norm_w, w_in, swa_sink, ssd_conv_w, ssd_conv_b, ssd_dt_bias, ssd_a_log, ssd_d, ssd_norm_w,
             diff_lambda, diff_subln_w, w_out, ffn_norm_w, w_router_group, b_router_group, w_router_expert,
             b_router_expert, w_gate, w_up, w_down, final_norm_w, tm=512, tq=512, tk=512, tmc=256):
    batch, seq, d = x.shape
    depth = w_in.shape[0]
    t = batch * seq
    tr = MOE_ROW_TILE
    n_tiles = (2 * t) // tr + N_EXPERTS
    slopes = jnp.exp2(-8.0 * jnp.arange(1, N_ALIBI_HEADS + 1, dtype=F32) / N_ALIBI_HEADS)
    swa_slopes, diff_slopes = slopes[:SWA_HEADS], slopes[SWA_HEADS:]

    sizes = [A_Q_DIM, A_KV_DIM, A_KV_DIM, SSD_D_INNER, SSD_CONV_DIM, SSD_DT_DIM, DIFF_QK_WIDTH, DIFF_QK_WIDTH,
             DIFF_V_WIDTH]
    offs = [0]
    for s in sizes:
        offs.append(offs[-1] + s)
    o_aq, o_ak, o_av, o_z, o_xbc, o_dt, o_cq, o_ck, o_cv, o_end = offs

    x2 = x.reshape(t, d)
    for l in range(depth):
        w = w_in[l]
        hw = 2 * DIFF_QK_DIM
        w_ck = jnp.pad(w[:, o_ck:o_cv].reshape(d, DIFF_HEADS, hw), ((0, 0), (0, 0), (0, LANES - hw)))
        w_main = jnp.concatenate(
            [w[:, o_ak:o_av], w[:, o_z:o_dt], w_ck.reshape(d, DIFF_HEADS * LANES), w[:, o_dt:o_cq],
             jnp.zeros((d, LANES - SSD_DT_DIM), w.dtype)], axis=1).astype(BF16)
        w_t = jnp.concatenate([w[:, o_aq:o_ak], w[:, o_av:o_z], w[:, o_cq:o_ck], w[:, o_cv:o_end]],
                              axis=1).T.astype(BF16)
        ak, z, xs, cm, ck, dt_raw, aqt, avt3, cqt, cvt, bt = _inproj(
            x2, attn_norm_w[l].reshape(1, d), w_main, w_t, ssd_conv_w[l].astype(F32),
            ssd_conv_b[l].reshape(1, -1).astype(F32), tm, seq, tk)

        ya = _swa(aqt, ak, avt3, swa_sink[l].astype(F32), swa_slopes, batch, seq)
        lambda_init = 0.8 - 0.6 * math.exp(-0.3 * l)
        yc = _diff(cqt, ck, cvt, diff_slopes * LOG2E, diff_lambda[l].astype(F32),
                   diff_subln_w[l].reshape(DIFF_V_DIM, 1).astype(F32), lambda_init, batch, seq, tq, tk)
        yf, yb = _ssd(xs, bt, cm, dt_raw, _pad_lanes(ssd_dt_bias[l]), _pad_lanes(ssd_a_log[l]),
                      _pad_lanes(ssd_d[l]), batch, seq)

        wrt = jnp.concatenate([w_router_expert[l], w_router_group[l],
                               jnp.zeros((d, SUBLANES - N_EXPERT_GROUPS), F32)], axis=1).T.astype(F32)
        br = jnp.concatenate([b_router_expert[l], b_router_group[l],
                              jnp.zeros((SUBLANES - N_EXPERT_GROUPS,), F32)]).reshape(-1, 1).astype(F32)
        xn, ri, rf, cnt = _outproj(x2, ya, yf, yb, z, ssd_norm_w[l].reshape(1, -1), yc, w_out[l].astype(BF16),
                                   ffn_norm_w[l].reshape(1, d), wrt, br, tm)

        counts = cnt[:, 0].astype(I32)
        padded = ((counts + tr - 1) // tr) * tr
        ends = jnp.cumsum(padded)
        starts = ends - padded
        experts = jnp.arange(N_EXPERTS, dtype=I32)[:, None]

        def slot_of(e, rank):
            return jnp.sum(jnp.where(e[None, :] == experts, starts[:, None], 0), axis=0) + rank

        slot1 = slot_of(ri[0], ri[2])
        slot2 = slot_of(ri[1], ri[3])
        tile_start = jnp.arange(n_tiles, dtype=I32) * tr
        tile_expert = jnp.minimum(jnp.sum(ends[None, :] <= tile_start[:, None], axis=1), N_EXPERTS - 1).astype(I32)
        n_used = (ends[-1] // tr).astype(I32).reshape(1)

        xs_sorted = _dispatch(xn, slot1, slot2, starts + counts, padded - counts, n_used, n_tiles * tr, tmc)
        y_sorted = _moe(xs_sorted, ffn_norm_w[l].reshape(1, d), w_gate, w_up, w_down, l, tile_expert, n_used)
        last = l == depth - 1
        x2 = _combine(xn, rf.T, final_norm_w.reshape(1, d), y_sorted, slot1, slot2, tmc, last)
    return x2.reshape(batch, seq, d)
```

```python
import functools
import math

import jax
import jax.numpy as jnp
from jax import lax
from jax.experimental import pallas as pl
from jax.experimental.pallas import tpu as pltpu

F32 = jnp.float32
BF16 = jnp.bfloat16
I32 = jnp.int32

HEAD_DIM = 64
SWA_HEADS = 6
SWA_KV_HEADS = 2
SWA_WINDOW = 128
SSD_HEADS = 6
SSD_HEAD_DIM = 64
SSD_GROUPS = 2
SSD_STATE = 64
SSD_CONV = 5
DIFF_HEADS = 4
DIFF_QK_DIM = 32
DIFF_V_DIM = 64
N_EXPERT_GROUPS = 4
EXPERTS_PER_GROUP = 8
N_EXPERTS = N_EXPERT_GROUPS * EXPERTS_PER_GROUP
NORM_EPS = 1e-6

A_Q_DIM = SWA_HEADS * HEAD_DIM
A_KV_DIM = SWA_KV_HEADS * HEAD_DIM
SSD_D_INNER = SSD_HEADS * SSD_HEAD_DIM
SSD_BC_DIM = SSD_GROUPS * SSD_STATE
SSD_CONV_DIM = SSD_D_INNER + 2 * SSD_BC_DIM
SSD_DT_DIM = 2 * SSD_HEADS
DIFF_QK_WIDTH = DIFF_HEADS * 2 * DIFF_QK_DIM
DIFF_V_WIDTH = DIFF_HEADS * DIFF_V_DIM
N_ALIBI_HEADS = SWA_HEADS + DIFF_HEADS

LANES = 128
SUBLANES = 8
VMEM_LIMIT = 56 * 1024 * 1024
NEG = -1e30
LOG2E = math.log2(math.e)

SSD_CHUNK = 128
SWA_BLOCKS_PER_STEP = 4
MOE_ROW_TILE = 256
DIFF_PAIR = 2 * 2 * DIFF_QK_DIM
DIFF_VROWS = 80
DIFF_NFEAT = 6


def _cparams(*sem):
    return pltpu.CompilerParams(dimension_semantics=sem, vmem_limit_bytes=VMEM_LIMIT)


def _rms(x, w):
    return x * lax.rsqrt(jnp.mean(x * x, axis=-1, keepdims=True) + NORM_EPS) * w


def _silu(x):
    return x / (1.0 + jnp.exp(-x))


def _softplus(x):
    return jnp.maximum(x, 0.0) + jnp.log(1.0 + jnp.exp(-jnp.abs(x)))


def _bf16_split(x):
    hi = x.astype(BF16).astype(F32)
    lo = (x - hi).astype(BF16).astype(F32)
    return hi, lo


_C_AK = 0
_C_Z = _C_AK + A_KV_DIM
_C_XBC = _C_Z + SSD_D_INNER
_C_CK = _C_XBC + SSD_CONV_DIM
_C_DT = _C_CK + DIFF_HEADS * LANES
_C_END = _C_DT + LANES
_R_AQ = 0
_R_AV = _R_AQ + A_Q_DIM
_R_CQ = _R_AV + A_KV_DIM
_R_CV = _R_CQ + DIFF_QK_WIDTH
_R_END = _R_CV + DIFF_V_WIDTH


def _inproj_kernel(x_ref, xp_ref, xn_ref, nw_ref, w_ref, wt_ref, cw_ref, cb_ref,
                   ak_ref, z_ref, xs_ref, cm_ref, ck_ref, dt_ref, aqt_ref, avt_ref, cqt_ref, cvt_ref, bt_ref,
                   *, tiles_per_seq, diff_key_tile):
    i = pl.program_id(0)
    tm = x_ref.shape[0]
    nw = nw_ref[...]
    h = _rms(x_ref[...], nw).astype(BF16)

    def seg(lo, hi):
        return jnp.dot(h, w_ref[:, lo:hi], preferred_element_type=F32)

    ak_ref[...] = seg(_C_AK, _C_Z).astype(BF16)
    z_ref[...] = seg(_C_Z, _C_XBC)
    pos = (i * tm + lax.broadcasted_iota(I32, (tm, _C_DT - _C_CK), 0)) % diff_key_tile
    ck_ref[...] = (seg(_C_CK, _C_DT) + _diff_key_features(pos)).astype(BF16)
    dt_ref[...] = seg(_C_DT, _C_END)

    w_xbc = w_ref[:, _C_XBC:_C_CK]
    first = i % tiles_per_seq == 0
    last = i % tiles_per_seq == tiles_per_seq - 1
    prev = jnp.dot(_rms(xp_ref[...], nw).astype(BF16), w_xbc, preferred_element_type=F32)
    nxt = jnp.dot(_rms(xn_ref[...], nw).astype(BF16), w_xbc, preferred_element_type=F32)
    prev = jnp.where(first, 0.0, prev)
    nxt = jnp.where(last, 0.0, nxt)
    ext = jnp.concatenate([prev, seg(_C_XBC, _C_CK), nxt], axis=0)
    half = SSD_CONV // 2
    conv = cb_ref[...]
    for k in range(SSD_CONV):
        off = SUBLANES - half + k
        conv = conv + cw_ref[k:k + 1, :] * ext[off:off + tm, :]
    u = _silu(conv)
    xs_ref[...] = u[:, :SSD_D_INNER]
    bt_ref[...] = jnp.transpose(u[:, SSD_D_INNER:SSD_D_INNER + SSD_BC_DIM])
    cm_ref[...] = u[:, SSD_D_INNER + SSD_BC_DIM:]

    tr = lax.dot_general(wt_ref[...], h, (((1,), (1,)), ((), ())), preferred_element_type=F32)
    aqt_ref[...] = (tr[_R_AQ:_R_AV] * (HEAD_DIM ** -0.5)).astype(BF16)
    avt = tr[_R_AV:_R_CQ].astype(BF16)
    for c in range(tm // LANES):
        avt_ref[c] = avt[:, c * LANES:(c + 1) * LANES]
    cqt_ref[...] = (tr[_R_CQ:_R_CV] * (DIFF_QK_DIM ** -0.5 * LOG2E)).astype(BF16)
    pad = DIFF_VROWS - DIFF_V_DIM
    ones_row = (lax.broadcasted_iota(I32, (pad, tm), 0) == 0).astype(BF16)
    for hh in range(DIFF_HEADS):
        cvt_ref[hh * DIFF_VROWS:hh * DIFF_VROWS + DIFF_V_DIM, :] = (
            tr[_R_CV + hh * DIFF_V_DIM:_R_CV + (hh + 1) * DIFF_V_DIM].astype(BF16))
        cvt_ref[hh * DIFF_VROWS + DIFF_V_DIM:(hh + 1) * DIFF_VROWS, :] = ones_row


def _inproj(x2, norm_w, w_main, w_t, conv_w, conv_b, tm, seq, diff_key_tile):
    t, d = x2.shape
    hb = tm // SUBLANES
    n_hblk = t // SUBLANES
    row = lambda i: (i, 0)
    col = lambda i: (0, i)
    full = lambda i: (0, 0)
    row_outs = [(A_KV_DIM, BF16), (SSD_D_INNER, F32), (SSD_D_INNER, F32), (SSD_BC_DIM, F32),
                (DIFF_HEADS * LANES, BF16), (LANES, F32)]
    out_shape = [jax.ShapeDtypeStruct((t, w), dt) for w, dt in row_outs]
    out_specs = [pl.BlockSpec((tm, w), row) for w, _ in row_outs]
    out_shape += [jax.ShapeDtypeStruct((A_Q_DIM, t), BF16),
                  jax.ShapeDtypeStruct((t // LANES, A_KV_DIM, LANES), BF16),
                  jax.ShapeDtypeStruct((DIFF_QK_WIDTH, t), BF16),
                  jax.ShapeDtypeStruct((DIFF_HEADS * DIFF_VROWS, t), BF16),
                  jax.ShapeDtypeStruct((SSD_BC_DIM, t), F32)]
    out_specs += [pl.BlockSpec((A_Q_DIM, tm), col),
                  pl.BlockSpec((tm // LANES, A_KV_DIM, LANES), lambda i: (i, 0, 0)),
                  pl.BlockSpec((DIFF_QK_WIDTH, tm), col),
                  pl.BlockSpec((DIFF_HEADS * DIFF_VROWS, tm), col),
                  pl.BlockSpec((SSD_BC_DIM, tm), col)]
    return pl.pallas_call(
        functools.partial(_inproj_kernel, tiles_per_seq=seq // tm, diff_key_tile=diff_key_tile),
        grid=(t // tm,),
        in_specs=[pl.BlockSpec((tm, d), row),
                  pl.BlockSpec((SUBLANES, d), lambda i: (jnp.maximum(i * hb - 1, 0), 0)),
                  pl.BlockSpec((SUBLANES, d), lambda i: (jnp.minimum((i + 1) * hb, n_hblk - 1), 0)),
                  pl.BlockSpec((1, d), full),
                  pl.BlockSpec(w_main.shape, full), pl.BlockSpec(w_t.shape, full),
                  pl.BlockSpec(conv_w.shape, full), pl.BlockSpec(conv_b.shape, full)],
        out_specs=out_specs,
        out_shape=out_shape,
        compiler_params=_cparams("parallel"),
        name="inproj",
    )(x2, x2, x2, norm_w, w_main, w_t, conv_w, conv_b)


def _swa_kernel(sink_ref, slope_ref, qt_ref, k_ref, vt_ref, o_ref):
    step = pl.program_id(1)
    s_len = k_ref.shape[0]
    blk = SWA_WINDOW
    band = 3 * blk
    nb = s_len // blk
    rep = SWA_HEADS // SWA_KV_HEADS
    hd = HEAD_DIM
    for u in range(SWA_BLOCKS_PER_STEP):
        n = step * SWA_BLOCKS_PER_STEP + u
        start_blk = jnp.clip(n - 1, 0, nb - 3)
        start = pl.multiple_of(start_blk * blk, blk)
        kb = k_ref[pl.ds(start, band), :]
        v3 = vt_ref[pl.ds(start_blk, 3)]
        vtb = jnp.concatenate([v3[0], v3[1], v3[2]], axis=1)
        qt = qt_ref[:, u * blk:(u + 1) * blk]
        zero = jnp.zeros((hd, rep * blk), BF16)
        grp = [jnp.concatenate([qt[(g * rep + r) * hd:(g * rep + r + 1) * hd] for r in range(rep)], axis=1)
               for g in range(SWA_KV_HEADS)]
        qbd = jnp.concatenate([jnp.concatenate([grp[0], zero], axis=1),
                               jnp.concatenate([zero, grp[1]], axis=1)], axis=0)
        st = jnp.dot(kb, qbd, preferred_element_type=F32)
        kpos = start + lax.broadcasted_iota(I32, (band, blk), 0)
        qpos = n * blk + lax.broadcasted_iota(I32, (band, blk), 1)
        dist_i = jnp.abs(qpos - kpos)
        valid = dist_i <= SWA_WINDOW
        dist = dist_i.astype(F32)
        ps, inv = [], []
        for h in range(SWA_HEADS):
            s = jnp.where(valid, st[:, h * blk:(h + 1) * blk] - slope_ref[h] * dist, NEG)
            sink = sink_ref[h]
            m = jnp.maximum(jnp.max(s, axis=0, keepdims=True), sink)
            p = jnp.exp(s - m)
            inv.append(1.0 / (jnp.sum(p, axis=0, keepdims=True) + jnp.exp(sink - m)))
            ps.append(p.astype(BF16))
        outs = []
        for g in range(SWA_KV_HEADS):
            pg = jnp.concatenate(ps[g * rep:(g + 1) * rep], axis=1)
            og = jnp.dot(vtb[g * hd:(g + 1) * hd, :], pg, preferred_element_type=F32)
            for r in range(rep):
                outs.append(og[:, r * blk:(r + 1) * blk] * inv[g * rep + r])
        o_ref[u * blk:(u + 1) * blk, :] = jnp.transpose(jnp.concatenate(outs, axis=0)).astype(o_ref.dtype)


def _swa(aqt, ak, avt3, sink, slopes, batch, seq):
    blk = SWA_WINDOW
    rows = blk * SWA_BLOCKS_PER_STEP
    steps = seq // rows
    nb = seq // blk
    t = batch * seq
    smem = pl.BlockSpec(memory_space=pltpu.SMEM)
    return pl.pallas_call(
        _swa_kernel,
        grid=(batch, steps),
        in_specs=[smem, smem,
                  pl.BlockSpec((A_Q_DIM, rows), lambda b, s: (0, b * steps + s)),
                  pl.BlockSpec((seq, A_KV_DIM), lambda b, s: (b, 0)),
                  pl.BlockSpec((nb, A_KV_DIM, blk), lambda b, s: (b, 0, 0))],
        out_specs=pl.BlockSpec((rows, A_Q_DIM), lambda b, s: (b * steps + s, 0)),
        out_shape=jax.ShapeDtypeStruct((t, A_Q_DIM), BF16),
        compiler_params=_cparams("parallel", "parallel"),
        name="swa",
    )(sink, slopes, aqt, ak, avt3)


def _diff_key_tile(i, j, tq, tk, nk):
    return ((i * tq) // tk + j) % nk


def _diff_key_features(pos_in_tile):
    lane = lax.broadcasted_iota(I32, pos_in_tile.shape, 1) % LANES - 2 * DIFF_QK_DIM
    coarse = ((pos_in_tile // 16) * 16).astype(F32)
    fine = (pos_in_tile % 16).astype(F32)
    f = lane % DIFF_NFEAT
    feat = jnp.where(f < 2, coarse, jnp.where(f < 4, fine, 1.0))
    return jnp.where((lane >= 0) & (lane < 2 * DIFF_NFEAT), feat, 0.0)


def _diff_kernel(slope_ref, qt_ref, k_ref, vt_ref, lam_ref, sw_ref, o_ref, qtb_ref, m_ref, acc_ref,
                 s_ref, p_ref, *, lambda_init):
    i = pl.program_id(1)
    j = pl.program_id(2)
    nk = pl.num_programs(2)
    tq = qt_ref.shape[1]
    tk = k_ref.shape[0]
    dq = DIFF_QK_DIM
    hw = 2 * dq
    nf = DIFF_NFEAT

    @pl.when(j == 0)
    def _():
        m_ref[...] = jnp.full(m_ref.shape, NEG, F32)
        acc_ref[...] = jnp.zeros(acc_ref.shape, F32)
        ii = lax.broadcasted_iota(I32, (1, 2 * tq), 1)
        ii = jnp.where(ii >= tq, ii - tq, ii).astype(F32)
        qt = qt_ref[...]
        col = lax.broadcasted_iota(I32, (hw, 2 * tq), 1)
        row = lax.broadcasted_iota(I32, (hw, 2 * tq), 0)
        own_map = row // dq == col // tq
        for h in range(DIFF_HEADS):
            qh = qt[h * hw:(h + 1) * hw, :]
            qh2 = jnp.where(own_map, jnp.concatenate([qh, qh], axis=1), jnp.zeros((hw, 2 * tq), BF16))
            sl = jnp.full((1, 2 * tq), slope_ref[h], F32)
            s_hi, s_lo = _bf16_split(sl)
            v_hi, v_lo = _bf16_split(-sl * ii)
            rows = jnp.concatenate([s_hi, s_lo, s_hi, s_lo, v_hi, v_lo], axis=0)
            zrow = jnp.zeros((nf, 2 * tq), F32)
            zero = jnp.zeros((hw - 2 * nf, 2 * tq), F32)
            variants = ([rows, zrow], [zrow, -rows], [zrow, zrow])
            for v, pieces in enumerate(variants):
                qtb_ref[v, h, 0:hw, :] = qh2
                qtb_ref[v, h, hw:2 * hw, :] = jnp.concatenate(pieces + [zero], axis=0).astype(BF16)

    q0 = i * tq
    k0 = _diff_key_tile(i, j, tq, tk, nk) * tk

    items = [(h, mp) for h in range(DIFF_HEADS) for mp in range(2)]

    def lanes(mp):
        return slice(mp * tq, (mp + 1) * tq)

    def scores(h, mp, variant):
        return jnp.dot(k_ref[:, h * LANES:(h + 1) * LANES], qtb_ref[variant, h, :, lanes(mp)],
                       preferred_element_type=F32)

    def stage_scores(h, mp, s):
        s_ref[h, :, lanes(mp)] = s
        return jnp.max(s, axis=0, keepdims=True)

    def softmax_step(h, mp, tile_max, shift):
        m_old = m_ref[h:h + 1, lanes(mp)]
        m_new = jnp.maximum(m_old, tile_max + shift)
        p_ref[h, :, lanes(mp)] = jnp.exp2(s_ref[h, :, lanes(mp)] - (m_new - shift)).astype(BF16)
        m_ref[h:h + 1, lanes(mp)] = m_new
        return jnp.exp2(m_old - m_new)

    def accumulate(h, mp, alpha):
        pv = jnp.dot(vt_ref[h * DIFF_VROWS:(h + 1) * DIFF_VROWS, :], p_ref[h, :, lanes(mp)],
                     preferred_element_type=F32)
        acc_ref[h, :, lanes(mp)] = alpha * acc_ref[h, :, lanes(mp)] + pv

    def tile_pass(score_fn, shift_fn):
        maxima, alphas = {}, {}
        for w in range(len(items) + 2):
            if w < len(items):
                h, mp = items[w]
                maxima[w] = stage_scores(h, mp, score_fn(h, mp))
            if 0 <= w - 1 < len(items):
                h, mp = items[w - 1]
                alphas[w - 1] = softmax_step(h, mp, maxima[w - 1], shift_fn(h))
            if 0 <= w - 2 < len(items):
                h, mp = items[w - 2]
                accumulate(h, mp, alphas[w - 2])

    @pl.when(j == 0)
    def _():
        kpos = k0 + lax.broadcasted_iota(I32, (tk, tq), 0)
        qpos = q0 + lax.broadcasted_iota(I32, (tk, tq), 1)
        dist = jnp.abs(qpos - kpos).astype(F32)
        tile_pass(lambda h, mp: scores(h, mp, 2) - slope_ref[h] * dist, lambda h: 0.0)

    @pl.when(j > 0)
    def _():
        below = k0 < q0
        variant = jnp.where(below, 0, 1)
        sign = jnp.where(below, 1.0, -1.0)
        tile_pass(lambda h, mp: scores(h, mp, variant),
                  lambda h: sign * slope_ref[h] * (k0 - q0).astype(F32))

    @pl.when(j == nk - 1)
    def _():
        lp = lam_ref[...]
        lam = (jnp.exp(jnp.sum(lp[0:1] * lp[1:2], axis=-1, keepdims=True))
               - jnp.exp(jnp.sum(lp[2:3] * lp[3:4], axis=-1, keepdims=True)) + lambda_init)
        outs = []
        for h in range(DIFF_HEADS):
            a = acc_ref[h]
            o = a[0:DIFF_V_DIM] / a[DIFF_V_DIM:DIFF_V_DIM + 1]
            o = o[:, 0:tq] - lam * o[:, tq:2 * tq]
            ms = jnp.mean(o * o, axis=0, keepdims=True)
            outs.append(o * lax.rsqrt(ms + NORM_EPS) * sw_ref[...] * (1.0 - lambda_init))
        o_ref[...] = jnp.transpose(jnp.concatenate(outs, axis=0)).astype(o_ref.dtype)


def _diff(cqt, ck, cvt, slopes, lam_params, subln_w_col, lambda_init, batch, seq, tq, tk):
    assert tk % tq == 0 and seq % tk == 0
    nq, nk = seq // tq, seq // tk
    t = batch * seq
    smem = pl.BlockSpec(memory_space=pltpu.SMEM)
    key_tile = functools.partial(_diff_key_tile, tq=tq, tk=tk, nk=nk)
    return pl.pallas_call(
        functools.partial(_diff_kernel, lambda_init=lambda_init),
        grid=(batch, nq, nk),
        in_specs=[smem,
                  pl.BlockSpec((DIFF_QK_WIDTH, tq), lambda b, i, j: (0, b * nq + i)),
                  pl.BlockSpec((tk, DIFF_HEADS * LANES), lambda b, i, j: (b * nk + key_tile(i, j), 0)),
                  pl.BlockSpec((DIFF_HEADS * DIFF_VROWS, tk), lambda b, i, j: (0, b * nk + key_tile(i, j))),
                  pl.BlockSpec(lam_params.shape, lambda b, i, j: (0, 0)),
                  pl.BlockSpec(subln_w_col.shape, lambda b, i, j: (0, 0))],
        out_specs=pl.BlockSpec((tq, DIFF_V_WIDTH), lambda b, i, j: (b * nq + i, 0)),
        out_shape=jax.ShapeDtypeStruct((t, DIFF_V_WIDTH), BF16),
        scratch_shapes=[pltpu.VMEM((3, DIFF_HEADS, LANES, 2 * tq), BF16),
                        pltpu.VMEM((DIFF_HEADS, 2 * tq), F32),
                        pltpu.VMEM((DIFF_HEADS, DIFF_VROWS, 2 * tq), F32),
                        pltpu.VMEM((DIFF_HEADS, tk, 2 * tq), F32),
                        pltpu.VMEM((DIFF_HEADS, tk, 2 * tq), BF16)],
        compiler_params=_cparams("parallel", "parallel", "arbitrary"),
        name="diffattn",
    )(slopes, cqt, ck, cvt, lam_params, subln_w_col)


def _ssd_direction(fwd, xs_ref, bt_ref, cm_ref, dt_ref, dtb_ref, alog_ref, dsk_ref, y_ref, state_ref):
    q = xs_ref.shape[0]
    lane0 = 0 if fwd else SSD_HEADS
    dt_all = _softplus(dt_ref[...] + dtb_ref[...])
    dta_all = dt_all * -jnp.exp(alog_ref[...])
    row = lax.broadcasted_iota(I32, (q, q), 0)
    col = lax.broadcasted_iota(I32, (q, q), 1)
    keep = (row >= col) if fwd else (row <= col)
    da_all = jnp.dot(keep.astype(F32), dta_all, precision=lax.Precision.HIGHEST,
                     preferred_element_type=F32)
    da_all_t = jnp.transpose(da_all)
    dt_all_t = jnp.transpose(dt_all)
    tot_all = jnp.sum(dta_all, axis=0, keepdims=True)

    xs = xs_ref[...]
    bmt = bt_ref[...]
    cm = cm_ref[...]
    rep = SSD_HEADS // SSD_GROUPS
    ns = SSD_STATE
    g_mats = [jnp.dot(cm[:, g * ns:(g + 1) * ns].astype(BF16), bmt[g * ns:(g + 1) * ns, :].astype(BF16),
                      preferred_element_type=F32) for g in range(SSD_GROUPS)]
    ys = []
    for h in range(SSD_HEADS):
        g = h // rep
        ln = lane0 + h
        dac = da_all[:, ln:ln + 1]
        dar = da_all_t[ln:ln + 1, :]
        dtr = dt_all_t[ln:ln + 1, :]
        tot = tot_all[:, ln:ln + 1]
        dac_b = jnp.broadcast_to(dac, (q, q))
        decay = jnp.exp(jnp.where(keep, dac_b - dar, NEG))
        xh = xs[:, h * SSD_HEAD_DIM:(h + 1) * SSD_HEAD_DIM]
        xhb = xh.astype(BF16)
        y = jnp.dot((g_mats[g] * decay * dtr).astype(BF16), xhb, preferred_element_type=F32)
        st = state_ref[h]
        c_in = (cm[:, g * ns:(g + 1) * ns] * jnp.exp(dac_b[:, 0:ns])).astype(BF16)
        y = y + jnp.dot(c_in, st.astype(BF16), preferred_element_type=F32)
        to_end = jnp.exp(tot - dar) * dtr
        b_out = (bmt[g * ns:(g + 1) * ns, :] * to_end).astype(BF16)
        state_ref[h] = st * jnp.exp(tot) + jnp.dot(b_out, xhb, preferred_element_type=F32)
        if fwd:
            y = y + dsk_ref[:, h:h + 1] * xh
        ys.append(y)
    y_ref[...] = jnp.concatenate(ys, axis=-1)


def _ssd_kernel(xsf_ref, btf_ref, cmf_ref, dtf_ref, xsb_ref, btb_ref, cmb_ref, dtb_in_ref,
                dtbias_ref, alog_ref, dsk_ref, yf_ref, yb_ref, state_ref):
    @pl.when(pl.program_id(1) == 0)
    def _():
        state_ref[...] = jnp.zeros(state_ref.shape, F32)

    _ssd_direction(True, xsf_ref, btf_ref, cmf_ref, dtf_ref, dtbias_ref, alog_ref, dsk_ref, yf_ref,
                   state_ref.at[0])
    _ssd_direction(False, xsb_ref, btb_ref, cmb_ref, dtb_in_ref, dtbias_ref, alog_ref, dsk_ref, yb_ref,
                   state_ref.at[1])


def _ssd(xs, bt, cm, dt_raw, dt_bias, a_log, d_skip, batch, seq):
    q = SSD_CHUNK
    nc = seq // q
    t = batch * seq
    full = lambda b, c: (0, 0)
    fw = lambda b, c: b * nc + c
    bw = lambda b, c: b * nc + nc - 1 - c

    def specs(idx):
        return [pl.BlockSpec((q, SSD_D_INNER), lambda b, c: (idx(b, c), 0)),
                pl.BlockSpec((SSD_BC_DIM, q), lambda b, c: (0, idx(b, c))),
                pl.BlockSpec((q, SSD_BC_DIM), lambda b, c: (idx(b, c), 0)),
                pl.BlockSpec((q, LANES), lambda b, c: (idx(b, c), 0))]

    return pl.pallas_call(
        _ssd_kernel,
        grid=(batch, nc),
        in_specs=specs(fw) + specs(bw) + [pl.BlockSpec(dt_bias.shape, full), pl.BlockSpec(a_log.shape, full),
                                          pl.BlockSpec(d_skip.shape, full)],
        out_specs=[pl.BlockSpec((q, SSD_D_INNER), lambda b, c: (fw(b, c), 0)),
                   pl.BlockSpec((q, SSD_D_INNER), lambda b, c: (bw(b, c), 0))],
        out_shape=[jax.ShapeDtypeStruct((t, SSD_D_INNER), F32), jax.ShapeDtypeStruct((t, SSD_D_INNER), F32)],
        scratch_shapes=[pltpu.VMEM((2, SSD_HEADS, SSD_STATE, SSD_HEAD_DIM), F32)],
        compiler_params=_cparams("parallel", "arbitrary"),
        name="ssd",
    )(xs, bt, cm, dt_raw, xs, bt, cm, dt_raw, dt_bias, a_log, d_skip)


def _outproj_kernel(x_ref, ya_ref, yf_ref, yb_ref, z_ref, snw_ref, yc_ref, wo_ref, fnw_ref, wrt_ref, br_ref,
                    xn_ref, ri_ref, rf_ref, cnt_ref, tri_ref, carry_ref):
    step = pl.program_id(0)
    tm = x_ref.shape[0]

    @pl.when(step == 0)
    def _():
        carry_ref[...] = jnp.zeros(carry_ref.shape, F32)
        r = lax.broadcasted_iota(I32, (tm, tm), 0)
        cc = lax.broadcasted_iota(I32, (tm, tm), 1)
        tri_ref[...] = (r <= cc).astype(BF16)

    y = (yf_ref[...] + yb_ref[...]) * _silu(z_ref[...])
    yb = _rms(y, snw_ref[...]).astype(BF16)
    acc = jnp.dot(ya_ref[...], wo_ref[0:A_Q_DIM, :], preferred_element_type=F32)
    acc = acc + jnp.dot(yb, wo_ref[A_Q_DIM:A_Q_DIM + SSD_D_INNER, :], preferred_element_type=F32)
    acc = acc + jnp.dot(yc_ref[...], wo_ref[A_Q_DIM + SSD_D_INNER:, :], preferred_element_type=F32)
    xn = x_ref[...] + acc
    xn_ref[...] = xn

    h = _rms(xn, fnw_ref[...])
    logits = lax.dot_general(wrt_ref[...], h, (((1,), (1,)), ((), ())), precision=lax.Precision.HIGHEST,
                             preferred_element_type=F32) + br_ref[...]
    ne, epg, ng = N_EXPERTS, EXPERTS_PER_GROUP, N_EXPERT_GROUPS
    gl = logits[ne:ne + ng, :]
    gmax = jnp.max(gl, axis=0, keepdims=True)
    g_sel = jnp.full((1, tm), float(ng - 1), F32)
    for g in range(ng - 2, -1, -1):
        g_sel = jnp.where(gl[g:g + 1, :] == gmax, float(g), g_sel)
    g_gate = 1.0 / jnp.sum(jnp.exp(gl - gmax), axis=0, keepdims=True)
    e_in = logits[0:epg, :]
    for g in range(1, ng):
        e_in = jnp.where(g_sel == float(g), logits[g * epg:(g + 1) * epg, :], e_in)
    sub = lax.broadcasted_iota(I32, (epg, tm), 0).astype(F32)
    m1 = jnp.max(e_in, axis=0, keepdims=True)
    i1 = jnp.min(jnp.where(e_in == m1, sub, float(epg)), axis=0, keepdims=True)
    rest = jnp.where(sub == i1, NEG, e_in)
    m2 = jnp.max(rest, axis=0, keepdims=True)
    i2 = jnp.min(jnp.where(rest == m2, sub, float(epg)), axis=0, keepdims=True)
    r = jnp.exp(m2 - m1)
    c1 = g_gate / (1.0 + r)
    c2 = g_gate * r / (1.0 + r)
    e1 = (g_sel * epg + i1).astype(I32)
    e2 = (g_sel * epg + i2).astype(I32)

    erow = lax.broadcasted_iota(I32, (ne, tm), 0)
    hit1 = erow == e1
    hit2 = erow == e2
    oh = jnp.where(hit1 | hit2, 1.0, 0.0)
    incl = jnp.dot(oh.astype(BF16), tri_ref[...], preferred_element_type=F32)
    before = incl - oh + carry_ref[:, 0:1]
    rank1 = jnp.sum(jnp.where(hit1, before, 0.0), axis=0, keepdims=True)
    rank2 = jnp.sum(jnp.where(hit2, before, 0.0), axis=0, keepdims=True)
    carry_ref[...] = carry_ref[...] + jnp.sum(oh, axis=1, keepdims=True)
    cnt_ref[...] = carry_ref[...]
    zi = jnp.zeros((1, tm), I32)
    ri_ref[...] = jnp.concatenate([e1, e2, rank1.astype(I32), rank2.astype(I32), zi, zi, zi, zi], axis=0)
    zf = jnp.zeros((1, tm), F32)
    rf_ref[...] = jnp.concatenate([c1, c2, zf, zf, zf, zf, zf, zf], axis=0)


def _outproj(x2, ya, yf, yb, z, ssd_norm_w, yc, w_out, ffn_norm_w, wrt, br, tm):
    t, d = x2.shape
    row = lambda i: (i, 0)
    full = lambda i: (0, 0)
    return pl.pallas_call(
        _outproj_kernel,
        grid=(t // tm,),
        in_specs=[pl.BlockSpec((tm, d), row),
                  pl.BlockSpec((tm, A_Q_DIM), row),
                  pl.BlockSpec((tm, SSD_D_INNER), row),
                  pl.BlockSpec((tm, SSD_D_INNER), row),
                  pl.BlockSpec((tm, SSD_D_INNER), row),
                  pl.BlockSpec(ssd_norm_w.shape, full),
                  pl.BlockSpec((tm, DIFF_V_WIDTH), row),
                  pl.BlockSpec(w_out.shape, full),
                  pl.BlockSpec(ffn_norm_w.shape, full),
                  pl.BlockSpec(wrt.shape, full),
                  pl.BlockSpec(br.shape, full)],
        out_specs=[pl.BlockSpec((tm, d), row),
                   pl.BlockSpec((SUBLANES, tm), lambda i: (0, i)),
                   pl.BlockSpec((SUBLANES, tm), lambda i: (0, i)),
                   pl.BlockSpec((N_EXPERTS, LANES), full)],
        out_shape=[jax.ShapeDtypeStruct((t, d), F32),
                   jax.ShapeDtypeStruct((SUBLANES, t), I32),
                   jax.ShapeDtypeStruct((SUBLANES, t), F32),
                   jax.ShapeDtypeStruct((N_EXPERTS, LANES), F32)],
        scratch_shapes=[pltpu.VMEM((tm, tm), BF16), pltpu.VMEM((N_EXPERTS, LANES), F32)],
        compiler_params=_cparams("arbitrary"),
        name="outproj_router",
    )(x2, ya, yf, yb, z, ssd_norm_w, yc, w_out, ffn_norm_w, wrt, br)


_PAD_PIECES = tuple(1 << b for b in reversed(range(MOE_ROW_TILE.bit_length() - 1)))


def _dispatch_kernel(slot1_ref, slot2_ref, pstart_ref, plen_ref, nused_ref, x_ref, xs_hbm, zero_ref, sem):
    i = pl.program_id(0)
    tm = x_ref.shape[0]

    @pl.when(i == 0)
    def _():
        zero_ref[...] = jnp.zeros(zero_ref.shape, F32)

        def pieces(e, wait):
            n = plen_ref[e]
            first = pstart_ref[e]
            off = first + n
            for b in _PAD_PIECES:
                off = off - (n & b)
                dst = pl.ds(pl.multiple_of(off, b), b) if b >= SUBLANES else None
                if dst is not None:
                    @pl.when((n & b) != 0)
                    def _():
                        cp = pltpu.make_async_copy(zero_ref.at[pl.ds(0, b)], xs_hbm.at[dst], sem.at[1])
                        cp.wait() if wait else cp.start()

            for u in range(SUBLANES - 1):
                @pl.when(u < (n & (SUBLANES - 1)))
                def _():
                    cp = pltpu.make_async_copy(zero_ref.at[pl.ds(0, 1)], xs_hbm.at[pl.ds(first + u, 1)],
                                               sem.at[1])
                    cp.wait() if wait else cp.start()

        def tail(tile, wait):
            big = _PAD_PIECES[0]
            for part in range(MOE_ROW_TILE // big):
                dst = xs_hbm.at[pl.ds(pl.multiple_of(tile * MOE_ROW_TILE + part * big, big), big)]
                cp = pltpu.make_async_copy(zero_ref, dst, sem.at[1])
                cp.wait() if wait else cp.start()

        def loop(fn, lo, hi, wait):
            def body(k, carry):
                fn(k, wait)
                return carry

            lax.fori_loop(lo, hi, body, 0)

        n_tiles = xs_hbm.shape[0] // MOE_ROW_TILE
        for wait in (False, True):
            loop(pieces, 0, N_EXPERTS, wait)
            loop(tail, nused_ref[0], n_tiles, wait)

    base = i * tm
    for r in range(tm):
        src = x_ref.at[pl.ds(r, 1)]
        pltpu.make_async_copy(src, xs_hbm.at[pl.ds(slot1_ref[base + r], 1)], sem.at[0]).start(priority=0)
        pltpu.make_async_copy(src, xs_hbm.at[pl.ds(slot2_ref[base + r], 1)], sem.at[0]).start(priority=1)
    for _ in range(2):
        pltpu.make_async_copy(x_ref, xs_hbm.at[pl.ds(0, tm)], sem.at[0]).wait()


def _dispatch(xn, slot1, slot2, pad_start, pad_len, n_used, n_rows, tm):
    t, d = xn.shape
    grid_spec = pltpu.PrefetchScalarGridSpec(
        num_scalar_prefetch=5,
        grid=(t // tm,),
        in_specs=[pl.BlockSpec((tm, d), lambda i, s1, s2, ps, pn, nu: (i, 0))],
        out_specs=pl.BlockSpec(memory_space=pl.ANY),
        scratch_shapes=[pltpu.VMEM((_PAD_PIECES[0], d), F32), pltpu.SemaphoreType.DMA((2,))],
    )
    return pl.pallas_call(
        _dispatch_kernel,
        grid_spec=grid_spec,
        out_shape=jax.ShapeDtypeStruct((n_rows, d), F32),
        compiler_params=_cparams("arbitrary"),
        name="moe_dispatch",
    )(slot1, slot2, pad_start, pad_len, n_used, xn)


def _moe_kernel(texp_ref, nused_ref, x_ref, fnw_ref, wg_ref, wu_ref, wd_ref, y_ref):
    i = pl.program_id(0)

    @pl.when(i < nused_ref[0])
    def _():
        h = _rms(x_ref[...], fnw_ref[...]).astype(BF16)
        hg = jnp.dot(h, wg_ref[...].astype(BF16), preferred_element_type=F32)
        hu = jnp.dot(h, wu_ref[...].astype(BF16), preferred_element_type=F32)
        act = (_silu(hg) * hu).astype(BF16)
        y_ref[...] = jnp.dot(act, wd_ref[...].astype(BF16), preferred_element_type=F32)

    @pl.when(i >= nused_ref[0])
    def _():
        y_ref[...] = jnp.zeros(y_ref.shape, F32)


def _moe(xs, ffn_norm_w, w_gate, w_up, w_down, layer, tile_expert, n_used):
    n_rows, d = xs.shape
    f = w_gate.shape[-1]
    tr = MOE_ROW_TILE

    def used(i, nu):
        return jnp.maximum(jnp.minimum(i, nu[0] - 1), 0)

    grid_spec = pltpu.PrefetchScalarGridSpec(
        num_scalar_prefetch=2,
        grid=(n_rows // tr,),
        in_specs=[pl.BlockSpec((tr, d), lambda i, te, nu: (used(i, nu), 0)),
                  pl.BlockSpec(ffn_norm_w.shape, lambda i, te, nu: (0, 0)),
                  pl.BlockSpec((None, None, d, f), lambda i, te, nu: (layer, te[used(i, nu)], 0, 0)),
                  pl.BlockSpec((None, None, d, f), lambda i, te, nu: (layer, te[used(i, nu)], 0, 0)),
                  pl.BlockSpec((None, None, f, d), lambda i, te, nu: (layer, te[used(i, nu)], 0, 0))],
        out_specs=pl.BlockSpec((tr, d), lambda i, te, nu: (i, 0)),
    )
    return pl.pallas_call(
        _moe_kernel,
        grid_spec=grid_spec,
        out_shape=jax.ShapeDtypeStruct((n_rows, d), F32),
        compiler_params=_cparams("arbitrary"),
        name="moe_experts",
    )(tile_expert, n_used, xs, ffn_norm_w, w_gate, w_up, w_down)


def _combine_kernel(slot1_ref, slot2_ref, x_ref, cw_ref, nw_ref, y_hbm, o_ref, ybuf, sem, *, final_norm):
    i = pl.program_id(0)
    n = pl.num_programs(0)
    tm = x_ref.shape[0]

    def start_gather(tile, slot):
        base = tile * tm
        for r in range(tm):
            pltpu.make_async_copy(y_hbm.at[pl.ds(slot1_ref[base + r], 1)], ybuf.at[slot, 0, pl.ds(r, 1)],
                                  sem.at[slot]).start(priority=0)
            pltpu.make_async_copy(y_hbm.at[pl.ds(slot2_ref[base + r], 1)], ybuf.at[slot, 1, pl.ds(r, 1)],
                                  sem.at[slot]).start(priority=1)

    def compute(slot):
        for k in range(2):
            pltpu.make_async_copy(y_hbm.at[pl.ds(0, tm)], ybuf.at[slot, k], sem.at[slot]).wait()
        cw = cw_ref[...]
        out = x_ref[...] + cw[:, 0:1] * ybuf[slot, 0] + cw[:, 1:2] * ybuf[slot, 1]
        if final_norm:
            out = _rms(out, nw_ref[...])
        o_ref[...] = out

    @pl.when(i == 0)
    def _():
        start_gather(0, 0)

    for parity in range(2):
        @pl.when(i % 2 == parity)
        def _():
            @pl.when(i + 1 < n)
            def _():
                start_gather(i + 1, 1 - parity)

            compute(parity)


def _combine(xn, cw, norm_w, y_sorted, slot1, slot2, tm, final_norm):
    t, d = xn.shape
    grid_spec = pltpu.PrefetchScalarGridSpec(
        num_scalar_prefetch=2,
        grid=(t // tm,),
        in_specs=[pl.BlockSpec((tm, d), lambda i, s1, s2: (i, 0)),
                  pl.BlockSpec((tm, cw.shape[1]), lambda i, s1, s2: (i, 0)),
                  pl.BlockSpec(norm_w.shape, lambda i, s1, s2: (0, 0)),
                  pl.BlockSpec(memory_space=pl.ANY)],
        out_specs=pl.BlockSpec((tm, d), lambda i, s1, s2: (i, 0)),
        scratch_shapes=[pltpu.VMEM((2, 2, tm, d), F32), pltpu.SemaphoreType.DMA((2,))],
    )
    return pl.pallas_call(
        functools.partial(_combine_kernel, final_norm=final_norm),
        grid_spec=grid_spec,
        out_shape=jax.ShapeDtypeStruct((t, d), F32),
        compiler_params=_cparams("arbitrary"),
        name="moe_combine",
    )(slot1, slot2, xn, cw, norm_w, y_sorted)


def _pad_lanes(v):
    v = v.reshape(1, -1).astype(F32)
    return jnp.pad(v, ((0, 0), (0, LANES - v.shape[1])))


def kernel(x, attn_norm_w, w_in, swa_sink, ssd_conv_w, ssd_conv_b, ssd_dt_bias, ssd_a_log, ssd_d, ssd_norm_w,
           diff_lambda, diff_subln_w, w_out, ffn_norm_w, w_router_group, b_router_group, w_router_expert,
           b_router_expert, w_gate, w_up, w_down, final_norm_w):
    return _forward(x, attn_norm_w, w_in, swa_sink, ssd_conv_w, ssd_conv_b, ssd_dt_bias, ssd_a_log, ssd_d,
                    ssd_norm_w, diff_lambda, diff_subln_w, w_out, ffn_norm_w, w_router_group, b_router_group,
                    w_router_expert, b_router_expert, w_gate, w_up, w_down, final_norm_w)


def _forward(x, attn_norm_w, w_in, swa_sink, ssd_conv_w, ssd_conv_b, ssd_dt_bias, ssd_a_log, ssd_d, ssd_norm_w,
             diff_lambda, diff_subln_w, w_out, ffn_norm_w, w_router_group, b_router_group, w_router_expert,
             b_router_expert, w_gate, w_up, w_down, final_norm_w, tm=512, tq=512, tk=512, tmc=256):
    batch, seq, d = x.shape
    depth = w_in.shape[0]
    t = batch * seq
    tr = MOE_ROW_TILE
    n_tiles = (2 * t) // tr + N_EXPERTS
    slopes = jnp.exp2(-8.0 * jnp.arange(1, N_ALIBI_HEADS + 1, dtype=F32) / N_ALIBI_HEADS)
    swa_slopes, diff_slopes = slopes[:SWA_HEADS], slopes[SWA_HEADS:]

    sizes = [A_Q_DIM, A_KV_DIM, A_KV_DIM, SSD_D_INNER, SSD_CONV_DIM, SSD_DT_DIM, DIFF_QK_WIDTH, DIFF_QK_WIDTH,
             DIFF_V_WIDTH]
    offs = [0]
    for s in sizes:
        offs.append(offs[-1] + s)
    o_aq, o_ak, o_av, o_z, o_xbc, o_dt, o_cq, o_ck, o_cv, o_end = offs

    x2 = x.reshape(t, d)
    for l in range(depth):
        w = w_in[l]
        hw = 2 * DIFF_QK_DIM
        w_ck = jnp.pad(w[:, o_ck:o_cv].reshape(d, DIFF_HEADS, hw), ((0, 0), (0, 0), (0, LANES - hw)))
        w_main = jnp.concatenate(
            [w[:, o_ak:o_av], w[:, o_z:o_dt], w_ck.reshape(d, DIFF_HEADS * LANES), w[:, o_dt:o_cq],
             jnp.zeros((d, LANES - SSD_DT_DIM), w.dtype)], axis=1).astype(BF16)
        w_t = jnp.concatenate([w[:, o_aq:o_ak], w[:, o_av:o_z], w[:, o_cq:o_ck], w[:, o_cv:o_end]],
                              axis=1).T.astype(BF16)
        ak, z, xs, cm, ck, dt_raw, aqt, avt3, cqt, cvt, bt = _inproj(
            x2, attn_norm_w[l].reshape(1, d), w_main, w_t, ssd_conv_w[l].astype(F32),
            ssd_conv_b[l].reshape(1, -1).astype(F32), tm, seq, tk)

        ya = _swa(aqt, ak, avt3, swa_sink[l].astype(F32), swa_slopes, batch, seq)
        lambda_init = 0.8 - 0.6 * math.exp(-0.3 * l)
        yc = _diff(cqt, ck, cvt, diff_slopes * LOG2E, diff_lambda[l].astype(F32),
                   diff_subln_w[l].reshape(DIFF_V_DIM, 1).astype(F32), lambda_init, batch, seq, tq, tk)
        yf, yb = _ssd(xs, bt, cm, dt_raw, _pad_lanes(ssd_dt_bias[l]), _pad_lanes(ssd_a_log[l]),
                      _pad_lanes(ssd_d[l]), batch, seq)

        wrt = jnp.concatenate([w_router_expert[l], w_router_group[l],
                               jnp.zeros((d, SUBLANES - N_EXPERT_GROUPS), F32)], axis=1).T.astype(F32)
        br = jnp.concatenate([b_router_expert[l], b_router_group[l],
                              jnp.zeros((SUBLANES - N_EXPERT_GROUPS,), F32)]).reshape(-1, 1).astype(F32)
        xn, ri, rf, cnt = _outproj(x2, ya, yf, yb, z, ssd_norm_w[l].reshape(1, -1), yc, w_out[l].astype(BF16),
                                   ffn_norm_w[l].reshape(1, d), wrt, br, tm)

        counts = cnt[:, 0].astype(I32)
        padded = ((counts + tr - 1) // tr) * tr
        ends = jnp.cumsum(padded)
        starts = ends - padded
        experts = jnp.arange(N_EXPERTS, dtype=I32)[:, None]

        def slot_of(e, rank):
            return jnp.sum(jnp.where(e[None, :] == experts, starts[:, None], 0), axis=0) + rank

        slot1 = slot_of(ri[0], ri[2])
        slot2 = slot_of(ri[1], ri[3])
        tile_start = jnp.arange(n_tiles, dtype=I32) * tr
        tile_expert = jnp.minimum(jnp.sum(ends[None, :] <= tile_start[:, None], axis=1), N_EXPERTS - 1).astype(I32)
        n_used = (ends[-1] // tr).astype(I32).reshape(1)

        xs_sorted = _dispatch(xn, slot1, slot2, starts + counts, padded - counts, n_used, n_tiles * tr, tmc)
        y_sorted = _moe(xs_sorted, ffn_norm_w[l].reshape(1, d), w_gate, w_up, w_down, l, tile_expert, n_used)
        last = l == depth - 1
        x2 = _combine(xn, rf.T, final_norm_w.reshape(1, d), y_sorted, slot1, slot2, tmc, last)
    return x2.reshape(batch, seq, d)
```

```python
import functools
import math

import jax
import jax.numpy as jnp
from jax import lax
from jax.experimental import pallas as pl
from jax.experimental.pallas import tpu as pltpu

F32 = jnp.float32
BF16 = jnp.bfloat16
I32 = jnp.int32

HEAD_DIM = 64
SWA_HEADS = 6
SWA_KV_HEADS = 2
SWA_WINDOW = 128
SSD_HEADS = 6
SSD_HEAD_DIM = 64
SSD_GROUPS = 2
SSD_STATE = 64
SSD_CONV = 5
DIFF_HEADS = 4
DIFF_QK_DIM = 32
DIFF_V_DIM = 64
N_EXPERT_GROUPS = 4
EXPERTS_PER_GROUP = 8
N_EXPERTS = N_EXPERT_GROUPS * EXPERTS_PER_GROUP
NORM_EPS = 1e-6

A_Q_DIM = SWA_HEADS * HEAD_DIM
A_KV_DIM = SWA_KV_HEADS * HEAD_DIM
SSD_D_INNER = SSD_HEADS * SSD_HEAD_DIM
SSD_BC_DIM = SSD_GROUPS * SSD_STATE
SSD_CONV_DIM = SSD_D_INNER + 2 * SSD_BC_DIM
SSD_DT_DIM = 2 * SSD_HEADS
DIFF_QK_WIDTH = DIFF_HEADS * 2 * DIFF_QK_DIM
DIFF_V_WIDTH = DIFF_HEADS * DIFF_V_DIM
N_ALIBI_HEADS = SWA_HEADS + DIFF_HEADS

LANES = 128
SUBLANES = 8
VMEM_LIMIT = 56 * 1024 * 1024
NEG = -1e30
LOG2E = math.log2(math.e)

SSD_CHUNK = 128
SWA_BLOCKS_PER_STEP = 4
MOE_ROW_TILE = 256
DIFF_PAIR = 2 * 2 * DIFF_QK_DIM
DIFF_VROWS = 80
DIFF_NFEAT = 6
DIFF_ITEM_SPLIT = 1


def _cparams(*sem):
    return pltpu.CompilerParams(dimension_semantics=sem, vmem_limit_bytes=VMEM_LIMIT)


def _rms(x, w):
    return x * lax.rsqrt(jnp.mean(x * x, axis=-1, keepdims=True) + NORM_EPS) * w


def _silu(x):
    return x / (1.0 + jnp.exp(-x))


def _softplus(x):
    return jnp.maximum(x, 0.0) + jnp.log(1.0 + jnp.exp(-jnp.abs(x)))


def _bf16_split(x):
    hi = x.astype(BF16).astype(F32)
    lo = (x - hi).astype(BF16).astype(F32)
    return hi, lo


_C_AK = 0
_C_Z = _C_AK + A_KV_DIM
_C_XBC = _C_Z + SSD_D_INNER
_C_CK = _C_XBC + SSD_CONV_DIM
_C_DT = _C_CK + DIFF_HEADS * LANES
_C_END = _C_DT + LANES
_R_AQ = 0
_R_AV = _R_AQ + A_Q_DIM
_R_CQ = _R_AV + A_KV_DIM
_R_CV = _R_CQ + DIFF_QK_WIDTH
_R_END = _R_CV + DIFF_V_WIDTH


def _inproj_kernel(x_ref, xp_ref, xn_ref, nw_ref, w_ref, wt_ref, cw_ref, cb_ref,
                   ak_ref, z_ref, xs_ref, cm_ref, ck_ref, dt_ref, aqt_ref, avt_ref, cqt_ref, cvt_ref, bt_ref,
                   *, tiles_per_seq, diff_key_tile):
    i = pl.program_id(0)
    tm = x_ref.shape[0]
    nw = nw_ref[...]
    h = _rms(x_ref[...], nw).astype(BF16)

    def seg(lo, hi):
        return jnp.dot(h, w_ref[:, lo:hi], preferred_element_type=F32)

    ak_ref[...] = seg(_C_AK, _C_Z).astype(BF16)
    z_ref[...] = seg(_C_Z, _C_XBC)
    pos = (i * tm + lax.broadcasted_iota(I32, (tm, _C_DT - _C_CK), 0)) % diff_key_tile
    ck_ref[...] = (seg(_C_CK, _C_DT) + _diff_key_features(pos)).astype(BF16)
    dt_ref[...] = seg(_C_DT, _C_END)

    w_xbc = w_ref[:, _C_XBC:_C_CK]
    first = i % tiles_per_seq == 0
    last = i % tiles_per_seq == tiles_per_seq - 1
    prev = jnp.dot(_rms(xp_ref[...], nw).astype(BF16), w_xbc, preferred_element_type=F32)
    nxt = jnp.dot(_rms(xn_ref[...], nw).astype(BF16), w_xbc, preferred_element_type=F32)
    prev = jnp.where(first, 0.0, prev)
    nxt = jnp.where(last, 0.0, nxt)
    ext = jnp.concatenate([prev, seg(_C_XBC, _C_CK), nxt], axis=0)
    half = SSD_CONV // 2
    conv = cb_ref[...]
    for k in range(SSD_CONV):
        off = SUBLANES - half + k
        conv = conv + cw_ref[k:k + 1, :] * ext[off:off + tm, :]
    u = _silu(conv)
    xs_ref[...] = u[:, :SSD_D_INNER]
    bt_ref[...] = jnp.transpose(u[:, SSD_D_INNER:SSD_D_INNER + SSD_BC_DIM])
    cm_ref[...] = u[:, SSD_D_INNER + SSD_BC_DIM:]

    tr = lax.dot_general(wt_ref[...], h, (((1,), (1,)), ((), ())), preferred_element_type=F32)
    aqt_ref[...] = (tr[_R_AQ:_R_AV] * (HEAD_DIM ** -0.5)).astype(BF16)
    avt = tr[_R_AV:_R_CQ].astype(BF16)
    for c in range(tm // LANES):
        avt_ref[c] = avt[:, c * LANES:(c + 1) * LANES]
    cqt_ref[...] = (tr[_R_CQ:_R_CV] * (DIFF_QK_DIM ** -0.5 * LOG2E)).astype(BF16)
    pad = DIFF_VROWS - DIFF_V_DIM
    ones_row = (lax.broadcasted_iota(I32, (pad, tm), 0) == 0).astype(BF16)
    for hh in range(DIFF_HEADS):
        cvt_ref[hh * DIFF_VROWS:hh * DIFF_VROWS + DIFF_V_DIM, :] = (
            tr[_R_CV + hh * DIFF_V_DIM:_R_CV + (hh + 1) * DIFF_V_DIM].astype(BF16))
        cvt_ref[hh * DIFF_VROWS + DIFF_V_DIM:(hh + 1) * DIFF_VROWS, :] = ones_row


def _inproj(x2, norm_w, w_main, w_t, conv_w, conv_b, tm, seq, diff_key_tile):
    t, d = x2.shape
    hb = tm // SUBLANES
    n_hblk = t // SUBLANES
    row = lambda i: (i, 0)
    col = lambda i: (0, i)
    full = lambda i: (0, 0)
    row_outs = [(A_KV_DIM, BF16), (SSD_D_INNER, F32), (SSD_D_INNER, F32), (SSD_BC_DIM, F32),
                (DIFF_HEADS * LANES, BF16), (LANES, F32)]
    out_shape = [jax.ShapeDtypeStruct((t, w), dt) for w, dt in row_outs]
    out_specs = [pl.BlockSpec((tm, w), row) for w, _ in row_outs]
    out_shape += [jax.ShapeDtypeStruct((A_Q_DIM, t), BF16),
                  jax.ShapeDtypeStruct((t // LANES, A_KV_DIM, LANES), BF16),
                  jax.ShapeDtypeStruct((DIFF_QK_WIDTH, t), BF16),
                  jax.ShapeDtypeStruct((DIFF_HEADS * DIFF_VROWS, t), BF16),
                  jax.ShapeDtypeStruct((SSD_BC_DIM, t), F32)]
    out_specs += [pl.BlockSpec((A_Q_DIM, tm), col),
                  pl.BlockSpec((tm // LANES, A_KV_DIM, LANES), lambda i: (i, 0, 0)),
                  pl.BlockSpec((DIFF_QK_WIDTH, tm), col),
                  pl.BlockSpec((DIFF_HEADS * DIFF_VROWS, tm), col),
                  pl.BlockSpec((SSD_BC_DIM, tm), col)]
    return pl.pallas_call(
        functools.partial(_inproj_kernel, tiles_per_seq=seq // tm, diff_key_tile=diff_key_tile),
        grid=(t // tm,),
        in_specs=[pl.BlockSpec((tm, d), row),
                  pl.BlockSpec((SUBLANES, d), lambda i: (jnp.maximum(i * hb - 1, 0), 0)),
                  pl.BlockSpec((SUBLANES, d), lambda i: (jnp.minimum((i + 1) * hb, n_hblk - 1), 0)),
                  pl.BlockSpec((1, d), full),
                  pl.BlockSpec(w_main.shape, full), pl.BlockSpec(w_t.shape, full),
                  pl.BlockSpec(conv_w.shape, full), pl.BlockSpec(conv_b.shape, full)],
        out_specs=out_specs,
        out_shape=out_shape,
        compiler_params=_cparams("parallel"),
        name="inproj",
    )(x2, x2, x2, norm_w, w_main, w_t, conv_w, conv_b)


def _swa_kernel(sink_ref, slope_ref, qt_ref, k_ref, vt_ref, o_ref):
    step = pl.program_id(1)
    s_len = k_ref.shape[0]
    blk = SWA_WINDOW
    band = 3 * blk
    nb = s_len // blk
    rep = SWA_HEADS // SWA_KV_HEADS
    hd = HEAD_DIM
    for u in range(SWA_BLOCKS_PER_STEP):
        n = step * SWA_BLOCKS_PER_STEP + u
        start_blk = jnp.clip(n - 1, 0, nb - 3)
        start = pl.multiple_of(start_blk * blk, blk)
        kb = k_ref[pl.ds(start, band), :]
        v3 = vt_ref[pl.ds(start_blk, 3)]
        vtb = jnp.concatenate([v3[0], v3[1], v3[2]], axis=1)
        qt = qt_ref[:, u * blk:(u + 1) * blk]
        zero = jnp.zeros((hd, rep * blk), BF16)
        grp = [jnp.concatenate([qt[(g * rep + r) * hd:(g * rep + r + 1) * hd] for r in range(rep)], axis=1)
               for g in range(SWA_KV_HEADS)]
        qbd = jnp.concatenate([jnp.concatenate([grp[0], zero], axis=1),
                               jnp.concatenate([zero, grp[1]], axis=1)], axis=0)
        st = jnp.dot(kb, qbd, preferred_element_type=F32)
        kpos = start + lax.broadcasted_iota(I32, (band, blk), 0)
        qpos = n * blk + lax.broadcasted_iota(I32, (band, blk), 1)
        dist_i = jnp.abs(qpos - kpos)
        valid = dist_i <= SWA_WINDOW
        dist = dist_i.astype(F32)
        ps, inv = [], []
        for h in range(SWA_HEADS):
            s = jnp.where(valid, st[:, h * blk:(h + 1) * blk] - slope_ref[h] * dist, NEG)
            sink = sink_ref[h]
            m = jnp.maximum(jnp.max(s, axis=0, keepdims=True), sink)
            p = jnp.exp(s - m)
            inv.append(1.0 / (jnp.sum(p, axis=0, keepdims=True) + jnp.exp(sink - m)))
            ps.append(p.astype(BF16))
        outs = []
        for g in range(SWA_KV_HEADS):
            pg = jnp.concatenate(ps[g * rep:(g + 1) * rep], axis=1)
            og = jnp.dot(vtb[g * hd:(g + 1) * hd, :], pg, preferred_element_type=F32)
            for r in range(rep):
                outs.append(og[:, r * blk:(r + 1) * blk] * inv[g * rep + r])
        o_ref[u * blk:(u + 1) * blk, :] = jnp.transpose(jnp.concatenate(outs, axis=0)).astype(o_ref.dtype)


def _swa(aqt, ak, avt3, sink, slopes, batch, seq):
    blk = SWA_WINDOW
    rows = blk * SWA_BLOCKS_PER_STEP
    steps = seq // rows
    nb = seq // blk
    t = batch * seq
    smem = pl.BlockSpec(memory_space=pltpu.SMEM)
    return pl.pallas_call(
        _swa_kernel,
        grid=(batch, steps),
        in_specs=[smem, smem,
                  pl.BlockSpec((A_Q_DIM, rows), lambda b, s: (0, b * steps + s)),
                  pl.BlockSpec((seq, A_KV_DIM), lambda b, s: (b, 0)),
                  pl.BlockSpec((nb, A_KV_DIM, blk), lambda b, s: (b, 0, 0))],
        out_specs=pl.BlockSpec((rows, A_Q_DIM), lambda b, s: (b * steps + s, 0)),
        out_shape=jax.ShapeDtypeStruct((t, A_Q_DIM), BF16),
        compiler_params=_cparams("parallel", "parallel"),
        name="swa",
    )(sink, slopes, aqt, ak, avt3)


def _diff_key_tile(i, j, tq, tk, nk):
    return ((i * tq) // tk + j) % nk


def _diff_key_features(pos_in_tile):
    lane = lax.broadcasted_iota(I32, pos_in_tile.shape, 1) % LANES - 2 * DIFF_QK_DIM
    coarse = ((pos_in_tile // 16) * 16).astype(F32)
    fine = (pos_in_tile % 16).astype(F32)
    f = lane % DIFF_NFEAT
    feat = jnp.where(f < 2, coarse, jnp.where(f < 4, fine, 1.0))
    return jnp.where((lane >= 0) & (lane < 2 * DIFF_NFEAT), feat, 0.0)


def _diff_kernel(slope_ref, qt_ref, k_ref, vt_ref, lam_ref, sw_ref, o_ref, qtb_ref, m_ref, acc_ref,
                 s_ref, p_ref, *, lambda_init, n_split):
    i = pl.program_id(1)
    j = pl.program_id(2)
    nk = pl.num_programs(2)
    tq = qt_ref.shape[1]
    tk = k_ref.shape[0]
    dq = DIFF_QK_DIM
    hw = 2 * dq
    nf = DIFF_NFEAT

    @pl.when(j == 0)
    def _():
        m_ref[...] = jnp.full(m_ref.shape, NEG, F32)
        acc_ref[...] = jnp.zeros(acc_ref.shape, F32)
        ii = lax.broadcasted_iota(I32, (1, 2 * tq), 1)
        ii = jnp.where(ii >= tq, ii - tq, ii).astype(F32)
        qt = qt_ref[...]
        col = lax.broadcasted_iota(I32, (hw, 2 * tq), 1)
        row = lax.broadcasted_iota(I32, (hw, 2 * tq), 0)
        own_map = row // dq == col // tq
        for h in range(DIFF_HEADS):
            qh = qt[h * hw:(h + 1) * hw, :]
            qh2 = jnp.where(own_map, jnp.concatenate([qh, qh], axis=1), jnp.zeros((hw, 2 * tq), BF16))
            sl = jnp.full((1, 2 * tq), slope_ref[h], F32)
            s_hi, s_lo = _bf16_split(sl)
            v_hi, v_lo = _bf16_split(-sl * ii)
            rows = jnp.concatenate([s_hi, s_lo, s_hi, s_lo, v_hi, v_lo], axis=0)
            zrow = jnp.zeros((nf, 2 * tq), F32)
            zero = jnp.zeros((hw - 2 * nf, 2 * tq), F32)
            variants = ([rows, zrow], [zrow, -rows], [zrow, zrow])
            for v, pieces in enumerate(variants):
                qtb_ref[v, h, 0:hw, :] = qh2
                qtb_ref[v, h, hw:2 * hw, :] = jnp.concatenate(pieces + [zero], axis=0).astype(BF16)

    q0 = i * tq
    k0 = _diff_key_tile(i, j, tq, tk, nk) * tk

    width = 2 * tq // n_split
    items = [(h, mp) for h in range(DIFF_HEADS) for mp in range(n_split)]

    def lanes(mp):
        return slice(mp * width, (mp + 1) * width)

    def scores(h, mp, variant):
        return jnp.dot(k_ref[:, h * LANES:(h + 1) * LANES], qtb_ref[variant, h, :, lanes(mp)],
                       preferred_element_type=F32)

    def stage_scores(h, mp, s):
        s_ref[h, :, lanes(mp)] = s
        return jnp.max(s, axis=0, keepdims=True)

    def softmax_step(h, mp, tile_max, shift):
        m_old = m_ref[h:h + 1, lanes(mp)]
        m_new = jnp.maximum(m_old, tile_max + shift)
        p_ref[h, :, lanes(mp)] = jnp.exp2(s_ref[h, :, lanes(mp)] - (m_new - shift)).astype(BF16)
        m_ref[h:h + 1, lanes(mp)] = m_new
        return jnp.exp2(m_old - m_new)

    def accumulate(h, mp, alpha):
        pv = jnp.dot(vt_ref[h * DIFF_VROWS:(h + 1) * DIFF_VROWS, :], p_ref[h, :, lanes(mp)],
                     preferred_element_type=F32)
        acc_ref[h, :, lanes(mp)] = alpha * acc_ref[h, :, lanes(mp)] + pv

    def tile_pass(score_fn, shift_fn):
        maxima, alphas = {}, {}
        for w in range(len(items) + 2):
            if w < len(items):
                h, mp = items[w]
                maxima[w] = stage_scores(h, mp, score_fn(h, mp))
            if 0 <= w - 1 < len(items):
                h, mp = items[w - 1]
                alphas[w - 1] = softmax_step(h, mp, maxima[w - 1], shift_fn(h))
            if 0 <= w - 2 < len(items):
                h, mp = items[w - 2]
                accumulate(h, mp, alphas[w - 2])

    @pl.when(j == 0)
    def _():
        kpos = k0 + lax.broadcasted_iota(I32, (tk, width), 0)
        qpos = q0 + lax.broadcasted_iota(I32, (tk, width), 1) % tq
        dist = jnp.abs(qpos - kpos).astype(F32)
        tile_pass(lambda h, mp: scores(h, mp, 2) - slope_ref[h] * dist, lambda h: 0.0)

    @pl.when(j > 0)
    def _():
        below = k0 < q0
        variant = jnp.where(below, 0, 1)
        sign = jnp.where(below, 1.0, -1.0)
        tile_pass(lambda h, mp: scores(h, mp, variant),
                  lambda h: sign * slope_ref[h] * (k0 - q0).astype(F32))

    @pl.when(j == nk - 1)
    def _():
        lp = lam_ref[...]
        lam = (jnp.exp(jnp.sum(lp[0:1] * lp[1:2], axis=-1, keepdims=True))
               - jnp.exp(jnp.sum(lp[2:3] * lp[3:4], axis=-1, keepdims=True)) + lambda_init)
        outs = []
        for h in range(DIFF_HEADS):
            a = acc_ref[h]
            o = a[0:DIFF_V_DIM] / a[DIFF_V_DIM:DIFF_V_DIM + 1]
            o = o[:, 0:tq] - lam * o[:, tq:2 * tq]
            ms = jnp.mean(o * o, axis=0, keepdims=True)
            outs.append(o * lax.rsqrt(ms + NORM_EPS) * sw_ref[...] * (1.0 - lambda_init))
        o_ref[...] = jnp.transpose(jnp.concatenate(outs, axis=0)).astype(o_ref.dtype)


def _diff(cqt, ck, cvt, slopes, lam_params, subln_w_col, lambda_init, batch, seq, tq, tk):
    assert tk % tq == 0 and seq % tk == 0
    nq, nk = seq // tq, seq // tk
    t = batch * seq
    smem = pl.BlockSpec(memory_space=pltpu.SMEM)
    key_tile = functools.partial(_diff_key_tile, tq=tq, tk=tk, nk=nk)
    return pl.pallas_call(
        functools.partial(_diff_kernel, lambda_init=lambda_init, n_split=DIFF_ITEM_SPLIT),
        grid=(batch, nq, nk),
        in_specs=[smem,
                  pl.BlockSpec((DIFF_QK_WIDTH, tq), lambda b, i, j: (0, b * nq + i)),
                  pl.BlockSpec((tk, DIFF_HEADS * LANES), lambda b, i, j: (b * nk + key_tile(i, j), 0)),
                  pl.BlockSpec((DIFF_HEADS * DIFF_VROWS, tk), lambda b, i, j: (0, b * nk + key_tile(i, j))),
                  pl.BlockSpec(lam_params.shape, lambda b, i, j: (0, 0)),
                  pl.BlockSpec(subln_w_col.shape, lambda b, i, j: (0, 0))],
        out_specs=pl.BlockSpec((tq, DIFF_V_WIDTH), lambda b, i, j: (b * nq + i, 0)),
        out_shape=jax.ShapeDtypeStruct((t, DIFF_V_WIDTH), BF16),
        scratch_shapes=[pltpu.VMEM((3, DIFF_HEADS, LANES, 2 * tq), BF16),
                        pltpu.VMEM((DIFF_HEADS, 2 * tq), F32),
                        pltpu.VMEM((DIFF_HEADS, DIFF_VROWS, 2 * tq), F32),
                        pltpu.VMEM((DIFF_HEADS, tk, 2 * tq), F32),
                        pltpu.VMEM((DIFF_HEADS, tk, 2 * tq), BF16)],
        compiler_params=_cparams("parallel", "parallel", "arbitrary"),
        name="diffattn",
    )(slopes, cqt, ck, cvt, lam_params, subln_w_col)


def _ssd_direction(fwd, xs_ref, bt_ref, cm_ref, dt_ref, dtb_ref, alog_ref, dsk_ref, y_ref, state_ref):
    q = xs_ref.shape[0]
    lane0 = 0 if fwd else SSD_HEADS
    dt_all = _softplus(dt_ref[...] + dtb_ref[...])
    dta_all = dt_all * -jnp.exp(alog_ref[...])
    row = lax.broadcasted_iota(I32, (q, q), 0)
    col = lax.broadcasted_iota(I32, (q, q), 1)
    keep = (row >= col) if fwd else (row <= col)
    da_all = jnp.dot(keep.astype(F32), dta_all, precision=lax.Precision.HIGHEST,
                     preferred_element_type=F32)
    da_all_t = jnp.transpose(da_all)
    dt_all_t = jnp.transpose(dt_all)
    tot_all = jnp.sum(dta_all, axis=0, keepdims=True)

    xs = xs_ref[...]
    bmt = bt_ref[...]
    cm = cm_ref[...]
    rep = SSD_HEADS // SSD_GROUPS
    ns = SSD_STATE
    g_mats = [jnp.dot(cm[:, g * ns:(g + 1) * ns].astype(BF16), bmt[g * ns:(g + 1) * ns, :].astype(BF16),
                      preferred_element_type=F32) for g in range(SSD_GROUPS)]
    ys = []
    for h in range(SSD_HEADS):
        g = h // rep
        ln = lane0 + h
        dac = da_all[:, ln:ln + 1]
        dar = da_all_t[ln:ln + 1, :]
        dtr = dt_all_t[ln:ln + 1, :]
        tot = tot_all[:, ln:ln + 1]
        dac_b = jnp.broadcast_to(dac, (q, q))
        decay = jnp.exp(jnp.where(keep, dac_b - dar, NEG))
        xh = xs[:, h * SSD_HEAD_DIM:(h + 1) * SSD_HEAD_DIM]
        xhb = xh.astype(BF16)
        y = jnp.dot((g_mats[g] * decay * dtr).astype(BF16), xhb, preferred_element_type=F32)
        st = state_ref[h]
        c_in = (cm[:, g * ns:(g + 1) * ns] * jnp.exp(dac_b[:, 0:ns])).astype(BF16)
        y = y + jnp.dot(c_in, st.astype(BF16), preferred_element_type=F32)
        to_end = jnp.exp(tot - dar) * dtr
        b_out = (bmt[g * ns:(g + 1) * ns, :] * to_end).astype(BF16)
        state_ref[h] = st * jnp.exp(tot) + jnp.dot(b_out, xhb, preferred_element_type=F32)
        if fwd:
            y = y + dsk_ref[:, h:h + 1] * xh
        ys.append(y)
    y_ref[...] = jnp.concatenate(ys, axis=-1)


def _ssd_kernel(xsf_ref, btf_ref, cmf_ref, dtf_ref, xsb_ref, btb_ref, cmb_ref, dtb_in_ref,
                dtbias_ref, alog_ref, dsk_ref, yf_ref, yb_ref, state_ref):
    @pl.when(pl.program_id(1) == 0)
    def _():
        state_ref[...] = jnp.zeros(state_ref.shape, F32)

    _ssd_direction(True, xsf_ref, btf_ref, cmf_ref, dtf_ref, dtbias_ref, alog_ref, dsk_ref, yf_ref,
                   state_ref.at[0])
    _ssd_direction(False, xsb_ref, btb_ref, cmb_ref, dtb_in_ref, dtbias_ref, alog_ref, dsk_ref, yb_ref,
                   state_ref.at[1])


def _ssd(xs, bt, cm, dt_raw, dt_bias, a_log, d_skip, batch, seq):
    q = SSD_CHUNK
    nc = seq // q
    t = batch * seq
    full = lambda b, c: (0, 0)
    fw = lambda b, c: b * nc + c
    bw = lambda b, c: b * nc + nc - 1 - c

    def specs(idx):
        return [pl.BlockSpec((q, SSD_D_INNER), lambda b, c: (idx(b, c), 0)),
                pl.BlockSpec((SSD_BC_DIM, q), lambda b, c: (0, idx(b, c))),
                pl.BlockSpec((q, SSD_BC_DIM), lambda b, c: (idx(b, c), 0)),
                pl.BlockSpec((q, LANES), lambda b, c: (idx(b, c), 0))]

    return pl.pallas_call(
        _ssd_kernel,
        grid=(batch, nc),
        in_specs=specs(fw) + specs(bw) + [pl.BlockSpec(dt_bias.shape, full), pl.BlockSpec(a_log.shape, full),
                                          pl.BlockSpec(d_skip.shape, full)],
        out_specs=[pl.BlockSpec((q, SSD_D_INNER), lambda b, c: (fw(b, c), 0)),
                   pl.BlockSpec((q, SSD_D_INNER), lambda b, c: (bw(b, c), 0))],
        out_shape=[jax.ShapeDtypeStruct((t, SSD_D_INNER), F32), jax.ShapeDtypeStruct((t, SSD_D_INNER), F32)],
        scratch_shapes=[pltpu.VMEM((2, SSD_HEADS, SSD_STATE, SSD_HEAD_DIM), F32)],
        compiler_params=_cparams("parallel", "arbitrary"),
        name="ssd",
    )(xs, bt, cm, dt_raw, xs, bt, cm, dt_raw, dt_bias, a_log, d_skip)


def _outproj_kernel(x_ref, ya_ref, yf_ref, yb_ref, z_ref, snw_ref, yc_ref, wo_ref, fnw_ref, wrt_ref, br_ref,
                    xn_ref, ri_ref, rf_ref, cnt_ref, tri_ref, carry_ref):
    step = pl.program_id(0)
    tm = x_ref.shape[0]

    @pl.when(step == 0)
    def _():
        carry_ref[...] = jnp.zeros(carry_ref.shape, F32)
        r = lax.broadcasted_iota(I32, (tm, tm), 0)
        cc = lax.broadcasted_iota(I32, (tm, tm), 1)
        tri_ref[...] = (r <= cc).astype(BF16)

    y = (yf_ref[...] + yb_ref[...]) * _silu(z_ref[...])
    yb = _rms(y, snw_ref[...]).astype(BF16)
    acc = jnp.dot(ya_ref[...], wo_ref[0:A_Q_DIM, :], preferred_element_type=F32)
    acc = acc + jnp.dot(yb, wo_ref[A_Q_DIM:A_Q_DIM + SSD_D_INNER, :], preferred_element_type=F32)
    acc = acc + jnp.dot(yc_ref[...], wo_ref[A_Q_DIM + SSD_D_INNER:, :], preferred_element_type=F32)
    xn = x_ref[...] + acc
    xn_ref[...] = xn

    h = _rms(xn, fnw_ref[...])
    logits = lax.dot_general(wrt_ref[...], h, (((1,), (1,)), ((), ())), precision=lax.Precision.HIGHEST,
                             preferred_element_type=F32) + br_ref[...]
    ne, epg, ng = N_EXPERTS, EXPERTS_PER_GROUP, N_EXPERT_GROUPS
    gl = logits[ne:ne + ng, :]
    gmax = jnp.max(gl, axis=0, keepdims=True)
    g_sel = jnp.full((1, tm), float(ng - 1), F32)
    for g in range(ng - 2, -1, -1):
        g_sel = jnp.where(gl[g:g + 1, :] == gmax, float(g), g_sel)
    g_gate = 1.0 / jnp.sum(jnp.exp(gl - gmax), axis=0, keepdims=True)
    e_in = logits[0:epg, :]
    for g in range(1, ng):
        e_in = jnp.where(g_sel == float(g), logits[g * epg:(g + 1) * epg, :], e_in)
    sub = lax.broadcasted_iota(I32, (epg, tm), 0).astype(F32)
    m1 = jnp.max(e_in, axis=0, keepdims=True)
    i1 = jnp.min(jnp.where(e_in == m1, sub, float(epg)), axis=0, keepdims=True)
    rest = jnp.where(sub == i1, NEG, e_in)
    m2 = jnp.max(rest, axis=0, keepdims=True)
    i2 = jnp.min(jnp.where(rest == m2, sub, float(epg)), axis=0, keepdims=True)
    r = jnp.exp(m2 - m1)
    c1 = g_gate / (1.0 + r)
    c2 = g_gate * r / (1.0 + r)
    e1 = (g_sel * epg + i1).astype(I32)
    e2 = (g_sel * epg + i2).astype(I32)

    erow = lax.broadcasted_iota(I32, (ne, tm), 0)
    hit1 = erow == e1
    hit2 = erow == e2
    oh = jnp.where(hit1 | hit2, 1.0, 0.0)
    incl = jnp.dot(oh.astype(BF16), tri_ref[...], preferred_element_type=F32)
    before = incl - oh + carry_ref[:, 0:1]
    rank1 = jnp.sum(jnp.where(hit1, before, 0.0), axis=0, keepdims=True)
    rank2 = jnp.sum(jnp.where(hit2, before, 0.0), axis=0, keepdims=True)
    carry_ref[...] = carry_ref[...] + jnp.sum(oh, axis=1, keepdims=True)
    cnt_ref[...] = carry_ref[...]
    zi = jnp.zeros((1, tm), I32)
    ri_ref[...] = jnp.concatenate([e1, e2, rank1.astype(I32), rank2.astype(I32), zi, zi, zi, zi], axis=0)
    zf = jnp.zeros((1, tm), F32)
    rf_ref[...] = jnp.concatenate([c1, c2, zf, zf, zf, zf, zf, zf], axis=0)


def _outproj(x2, ya, yf, yb, z, ssd_norm_w, yc, w_out, ffn_norm_w, wrt, br, tm):
    t, d = x2.shape
    row = lambda i: (i, 0)
    full = lambda i: (0, 0)
    return pl.pallas_call(
        _outproj_kernel,
        grid=(t // tm,),
        in_specs=[pl.BlockSpec((tm, d), row),
                  pl.BlockSpec((tm, A_Q_DIM), row),
                  pl.BlockSpec((tm, SSD_D_INNER), row),
                  pl.BlockSpec((tm, SSD_D_INNER), row),
                  pl.BlockSpec((tm, SSD_D_INNER), row),
                  pl.BlockSpec(ssd_norm_w.shape, full),
                  pl.BlockSpec((tm, DIFF_V_WIDTH), row),
                  pl.BlockSpec(w_out.shape, full),
                  pl.BlockSpec(ffn_norm_w.shape, full),
                  pl.BlockSpec(wrt.shape, full),
                  pl.BlockSpec(br.shape, full)],
        out_specs=[pl.BlockSpec((tm, d), row),
                   pl.BlockSpec((SUBLANES, tm), lambda i: (0, i)),
                   pl.BlockSpec((SUBLANES, tm), lambda i: (0, i)),
                   pl.BlockSpec((N_EXPERTS, LANES), full)],
        out_shape=[jax.ShapeDtypeStruct((t, d), F32),
                   jax.ShapeDtypeStruct((SUBLANES, t), I32),
                   jax.ShapeDtypeStruct((SUBLANES, t), F32),
                   jax.ShapeDtypeStruct((N_EXPERTS, LANES), F32)],
        scratch_shapes=[pltpu.VMEM((tm, tm), BF16), pltpu.VMEM((N_EXPERTS, LANES), F32)],
        compiler_params=_cparams("arbitrary"),
        name="outproj_router",
    )(x2, ya, yf, yb, z, ssd_norm_w, yc, w_out, ffn_norm_w, wrt, br)


_PAD_PIECES = tuple(1 << b for b in reversed(range(MOE_ROW_TILE.bit_length() - 1)))


def _dispatch_kernel(slot1_ref, slot2_ref, pstart_ref, plen_ref, nused_ref, x_ref, xs_hbm, zero_ref, sem):
    i = pl.program_id(0)
    tm = x_ref.shape[0]

    @pl.when(i == 0)
    def _():
        zero_ref[...] = jnp.zeros(zero_ref.shape, F32)

        def pieces(e, wait):
            n = plen_ref[e]
            first = pstart_ref[e]
            off = first + n
            for b in _PAD_PIECES:
                off = off - (n & b)
                dst = pl.ds(pl.multiple_of(off, b), b) if b >= SUBLANES else None
                if dst is not None:
                    @pl.when((n & b) != 0)
                    def _():
                        cp = pltpu.make_async_copy(zero_ref.at[pl.ds(0, b)], xs_hbm.at[dst], sem.at[1])
                        cp.wait() if wait else cp.start()

            for u in range(SUBLANES - 1):
                @pl.when(u < (n & (SUBLANES - 1)))
                def _():
                    cp = pltpu.make_async_copy(zero_ref.at[pl.ds(0, 1)], xs_hbm.at[pl.ds(first + u, 1)],
                                               sem.at[1])
                    cp.wait() if wait else cp.start()

        def tail(tile, wait):
            big = _PAD_PIECES[0]
            for part in range(MOE_ROW_TILE // big):
                dst = xs_hbm.at[pl.ds(pl.multiple_of(tile * MOE_ROW_TILE + part * big, big), big)]
                cp = pltpu.make_async_copy(zero_ref, dst, sem.at[1])
                cp.wait() if wait else cp.start()

        def loop(fn, lo, hi, wait):
            def body(k, carry):
                fn(k, wait)
                return carry

            lax.fori_loop(lo, hi, body, 0)

        n_tiles = xs_hbm.shape[0] // MOE_ROW_TILE
        for wait in (False, True):
            loop(pieces, 0, N_EXPERTS, wait)
            loop(tail, nused_ref[0], n_tiles, wait)

    base = i * tm
    for r in range(tm):
        src = x_ref.at[pl.ds(r, 1)]
        pltpu.make_async_copy(src, xs_hbm.at[pl.ds(slot1_ref[base + r], 1)], sem.at[0]).start(priority=0)
        pltpu.make_async_copy(src, xs_hbm.at[pl.ds(slot2_ref[base + r], 1)], sem.at[0]).start(priority=1)
    for _ in range(2):
        pltpu.make_async_copy(x_ref, xs_hbm.at[pl.ds(0, tm)], sem.at[0]).wait()


def _dispatch(xn, slot1, slot2, pad_start, pad_len, n_used, n_rows, tm):
    t, d = xn.shape
    grid_spec = pltpu.PrefetchScalarGridSpec(
        num_scalar_prefetch=5,
        grid=(t // tm,),
        in_specs=[pl.BlockSpec((tm, d), lambda i, s1, s2, ps, pn, nu: (i, 0))],
        out_specs=pl.BlockSpec(memory_space=pl.ANY),
        scratch_shapes=[pltpu.VMEM((_PAD_PIECES[0], d), F32), pltpu.SemaphoreType.DMA((2,))],
    )
    return pl.pallas_call(
        _dispatch_kernel,
        grid_spec=grid_spec,
        out_shape=jax.ShapeDtypeStruct((n_rows, d), F32),
        compiler_params=_cparams("arbitrary"),
        name="moe_dispatch",
    )(slot1, slot2, pad_start, pad_len, n_used, xn)


def _moe_kernel(texp_ref, nused_ref, x_ref, fnw_ref, wg_ref, wu_ref, wd_ref, y_ref):
    i = pl.program_id(0)

    @pl.when(i < nused_ref[0])
    def _():
        h = _rms(x_ref[...], fnw_ref[...]).astype(BF16)
        hg = jnp.dot(h, wg_ref[...].astype(BF16), preferred_element_type=F32)
        hu = jnp.dot(h, wu_ref[...].astype(BF16), preferred_element_type=F32)
        act = (_silu(hg) * hu).astype(BF16)
        y_ref[...] = jnp.dot(act, wd_ref[...].astype(BF16), preferred_element_type=F32)

    @pl.when(i >= nused_ref[0])
    def _():
        y_ref[...] = jnp.zeros(y_ref.shape, F32)


def _moe(xs, ffn_norm_w, w_gate, w_up, w_down, layer, tile_expert, n_used):
    n_rows, d = xs.shape
    f = w_gate.shape[-1]
    tr = MOE_ROW_TILE

    def used(i, nu):
        return jnp.maximum(jnp.minimum(i, nu[0] - 1), 0)

    grid_spec = pltpu.PrefetchScalarGridSpec(
        num_scalar_prefetch=2,
        grid=(n_rows // tr,),
        in_specs=[pl.BlockSpec((tr, d), lambda i, te, nu: (used(i, nu), 0)),
                  pl.BlockSpec(ffn_norm_w.shape, lambda i, te, nu: (0, 0)),
                  pl.BlockSpec((None, None, d, f), lambda i, te, nu: (layer, te[used(i, nu)], 0, 0)),
                  pl.BlockSpec((None, None, d, f), lambda i, te, nu: (layer, te[used(i, nu)], 0, 0)),
                  pl.BlockSpec((None, None, f, d), lambda i, te, nu: (layer, te[used(i, nu)], 0, 0))],
        out_specs=pl.BlockSpec((tr, d), lambda i, te, nu: (i, 0)),
    )
    return pl.pallas_call(
        _moe_kernel,
        grid_spec=grid_spec,
        out_shape=jax.ShapeDtypeStruct((n_rows, d), F32),
        compiler_params=_cparams("arbitrary"),
        name="moe_experts",
    )(tile_expert, n_used, xs, ffn_norm_w, w_gate, w_up, w_down)


def _combine_kernel(slot1_ref, slot2_ref, x_ref, cw_ref, nw_ref, y_hbm, o_ref, ybuf, sem, *, final_norm):
    i = pl.program_id(0)
    n = pl.num_programs(0)
    tm = x_ref.shape[0]

    def start_gather(tile, slot):
        base = tile * tm
        for r in range(tm):
            pltpu.make_async_copy(y_hbm.at[pl.ds(slot1_ref[base + r], 1)], ybuf.at[slot, 0, pl.ds(r, 1)],
                                  sem.at[slot]).start(priority=0)
            pltpu.make_async_copy(y_hbm.at[pl.ds(slot2_ref[base + r], 1)], ybuf.at[slot, 1, pl.ds(r, 1)],
                                  sem.at[slot]).start(priority=1)

    def compute(slot):
        for k in range(2):
            pltpu.make_async_copy(y_hbm.at[pl.ds(0, tm)], ybuf.at[slot, k], sem.at[slot]).wait()
        cw = cw_ref[...]
        out = x_ref[...] + cw[:, 0:1] * ybuf[slot, 0] + cw[:, 1:2] * ybuf[slot, 1]
        if final_norm:
            out = _rms(out, nw_ref[...])
        o_ref[...] = out

    @pl.when(i == 0)
    def _():
        start_gather(0, 0)

    for parity in range(2):
        @pl.when(i % 2 == parity)
        def _():
            @pl.when(i + 1 < n)
            def _():
                start_gather(i + 1, 1 - parity)

            compute(parity)


def _combine(xn, cw, norm_w, y_sorted, slot1, slot2, tm, final_norm):
    t, d = xn.shape
    grid_spec = pltpu.PrefetchScalarGridSpec(
        num_scalar_prefetch=2,
        grid=(t // tm,),
        in_specs=[pl.BlockSpec((tm, d), lambda i, s1, s2: (i, 0)),
                  pl.BlockSpec((tm, cw.shape[1]), lambda i, s1, s2: (i, 0)),
                  pl.BlockSpec(norm_w.shape, lambda i, s1, s2: (0, 0)),
                  pl.BlockSpec(memory_space=pl.ANY)],
        out_specs=pl.BlockSpec((tm, d), lambda i, s1, s2: (i, 0)),
        scratch_shapes=[pltpu.VMEM((2, 2, tm, d), F32), pltpu.SemaphoreType.DMA((2,))],
    )
    return pl.pallas_call(
        functools.partial(_combine_kernel, final_norm=final_norm),
        grid_spec=grid_spec,
        out_shape=jax.ShapeDtypeStruct((t, d), F32),
        compiler_params=_cparams("arbitrary"),
        name="moe_combine",
    )(slot1, slot2, xn, cw, norm_w, y_sorted)


def _pad_lanes(v):
    v = v.reshape(1, -1).astype(F32)
    return jnp.pad(v, ((0, 0), (0, LANES - v.shape[1])))


def kernel(x, attn_norm_w, w_in, swa_sink, ssd_conv_w, ssd_conv_b, ssd_dt_bias, ssd_a_log, ssd_d, ssd_norm_w,
           diff_lambda, diff_subln_w, w_out, ffn_norm_w, w_router_group, b_router_group, w_router_expert,
           b_router_expert, w_gate, w_up, w_down, final_norm_w):
    return _forward(x, attn_norm_w, w_in, swa_sink, ssd_conv_w, ssd_conv_b, ssd_dt_bias, ssd_a_log, ssd_d,
                    ssd_norm_w, diff_lambda, diff_subln_w, w_out, ffn_norm_w, w_router_group, b_router_group,
                    w_router_expert, b_router_expert, w_gate, w_up, w_down, final_norm_w)


def _forward(x, attn_norm_w, w_in, swa_sink, ssd_conv_w, ssd_conv_b, ssd_dt_bias, ssd_a_log, ssd_d, ssd_norm_w,
             diff_lambda, diff_subln_w, w_out, ffn_norm_w, w_router_group, b_router_group, w_router_expert,
             b_router_expert, w_gate, w_up, w_down, final_norm_w, tm=512, tq=256, tk=512, tmc=256):
    batch, seq, d = x.shape
    depth = w_in.shape[0]
    t = batch * seq
    tr = MOE_ROW_TILE
    n_tiles = (2 * t) // tr + N_EXPERTS
    slopes = jnp.exp2(-8.0 * jnp.arange(1, N_ALIBI_HEADS + 1, dtype=F32) / N_ALIBI_HEADS)
    swa_slopes, diff_slopes = slopes[:SWA_HEADS], slopes[SWA_HEADS:]

    sizes = [A_Q_DIM, A_KV_DIM, A_KV_DIM, SSD_D_INNER, SSD_CONV_DIM, SSD_DT_DIM, DIFF_QK_WIDTH, DIFF_QK_WIDTH,
             DIFF_V_WIDTH]
    offs = [0]
    for s in sizes:
        offs.append(offs[-1] + s)
    o_aq, o_ak, o_av, o_z, o_xbc, o_dt, o_cq, o_ck, o_cv, o_end = offs

    x2 = x.reshape(t, d)
    for l in range(depth):
        w = w_in[l]
        hw = 2 * DIFF_QK_DIM
        w_ck = jnp.pad(w[:, o_ck:o_cv].reshape(d, DIFF_HEADS, hw), ((0, 0), (0, 0), (0, LANES - hw)))
        w_main = jnp.concatenate(
            [w[:, o_ak:o_av], w[:, o_z:o_dt], w_ck.reshape(d, DIFF_HEADS * LANES), w[:, o_dt:o_cq],
             jnp.zeros((d, LANES - SSD_DT_DIM), w.dtype)], axis=1).astype(BF16)
        w_t = jnp.concatenate([w[:, o_aq:o_ak], w[:, o_av:o_z], w[:, o_cq:o_ck], w[:, o_cv:o_end]],
                              axis=1).T.astype(BF16)
        ak, z, xs, cm, ck, dt_raw, aqt, avt3, cqt, cvt, bt = _inproj(
            x2, attn_norm_w[l].reshape(1, d), w_main, w_t, ssd_conv_w[l].astype(F32),
            ssd_conv_b[l].reshape(1, -1).astype(F32), tm, seq, tk)

        ya = _swa(aqt, ak, avt3, swa_sink[l].astype(F32), swa_slopes, batch, seq)
        lambda_init = 0.8 - 0.6 * math.exp(-0.3 * l)
        yc = _diff(cqt, ck, cvt, diff_slopes * LOG2E, diff_lambda[l].astype(F32),
                   diff_subln_w[l].reshape(DIFF_V_DIM, 1).astype(F32), lambda_init, batch, seq, tq, tk)
        yf, yb = _ssd(xs, bt, cm, dt_raw, _pad_lanes(ssd_dt_bias[l]), _pad_lanes(ssd_a_log[l]),
                      _pad_lanes(ssd_d[l]), batch, seq)

        wrt = jnp.concatenate([w_router_expert[l], w_router_group[l],
                               jnp.zeros((d, SUBLANES - N_EXPERT_GROUPS), F32)], axis=1).T.astype(F32)
        br = jnp.concatenate([b_router_expert[l], b_router_group[l],
                              jnp.zeros((SUBLANES - N_EXPERT_GROUPS,), F32)]).reshape(-1, 1).astype(F32)
        xn, ri, rf, cnt = _outproj(x2, ya, yf, yb, z, ssd_norm_w[l].reshape(1, -1), yc, w_out[l].astype(BF16),
                                   ffn_norm_w[l].reshape(1, d), wrt, br, tm)

        counts = cnt[:, 0].astype(I32)
        padded = ((counts + tr - 1) // tr) * tr
        ends = jnp.cumsum(padded)
        starts = ends - padded
        experts = jnp.arange(N_EXPERTS, dtype=I32)[:, None]

        def slot_of(e, rank):
            return jnp.sum(jnp.where(e[None, :] == experts, starts[:, None], 0), axis=0) + rank

        slot1 = slot_of(ri[0], ri[2])
        slot2 = slot_of(ri[1], ri[3])
        tile_start = jnp.arange(n_tiles, dtype=I32) * tr
        tile_expert = jnp.minimum(jnp.sum(ends[None, :] <= tile_start[:, None], axis=1), N_EXPERTS - 1).astype(I32)
        n_used = (ends[-1] // tr).astype(I32).reshape(1)

        xs_sorted = _dispatch(xn, slot1, slot2, starts + counts, padded - counts, n_used, n_tiles * tr, tmc)
        y_sorted = _moe(xs_sorted, ffn_norm_w[l].reshape(1, d), w_gate, w_up, w_down, l, tile_expert, n_used)
        last = l == depth - 1
        x2 = _combine(xn, rf.T, final_norm_w.reshape(1, d), y_sorted, slot1, slot2, tmc, last)
    return x2.reshape(batch, seq, d)
```

```python
import functools
import math

import jax
import jax.numpy as jnp
from jax import lax
from jax.experimental import pallas as pl
from jax.experimental.pallas import tpu as pltpu

F32 = jnp.float32
BF16 = jnp.bfloat16
I32 = jnp.int32

HEAD_DIM = 64
SWA_HEADS = 6
SWA_KV_HEADS = 2
SWA_WINDOW = 128
SSD_HEADS = 6
SSD_HEAD_DIM = 64
SSD_GROUPS = 2
SSD_STATE = 64
SSD_CONV = 5
DIFF_HEADS = 4
DIFF_QK_DIM = 32
DIFF_V_DIM = 64
N_EXPERT_GROUPS = 4
EXPERTS_PER_GROUP = 8
N_EXPERTS = N_EXPERT_GROUPS * EXPERTS_PER_GROUP
NORM_EPS = 1e-6

A_Q_DIM = SWA_HEADS * HEAD_DIM
A_KV_DIM = SWA_KV_HEADS * HEAD_DIM
SSD_D_INNER = SSD_HEADS * SSD_HEAD_DIM
SSD_BC_DIM = SSD_GROUPS * SSD_STATE
SSD_CONV_DIM = SSD_D_INNER + 2 * SSD_BC_DIM
SSD_DT_DIM = 2 * SSD_HEADS
DIFF_QK_WIDTH = DIFF_HEADS * 2 * DIFF_QK_DIM
DIFF_V_WIDTH = DIFF_HEADS * DIFF_V_DIM
N_ALIBI_HEADS = SWA_HEADS + DIFF_HEADS

LANES = 128
SUBLANES = 8
VMEM_LIMIT = 56 * 1024 * 1024
NEG = -1e30
LOG2E = math.log2(math.e)

SSD_CHUNK = 128
SWA_BLOCKS_PER_STEP = 4
MOE_ROW_TILE = 256
DIFF_PAIR = 2 * 2 * DIFF_QK_DIM
DIFF_VROWS = 80
DIFF_NFEAT = 6
DIFF_ITEM_SPLIT = 1


def _cparams(*sem):
    return pltpu.CompilerParams(dimension_semantics=sem, vmem_limit_bytes=VMEM_LIMIT)


def _rms(x, w):
    return x * lax.rsqrt(jnp.mean(x * x, axis=-1, keepdims=True) + NORM_EPS) * w


def _silu(x):
    return x / (1.0 + jnp.exp(-x))


def _softplus(x):
    return jnp.maximum(x, 0.0) + jnp.log(1.0 + jnp.exp(-jnp.abs(x)))


def _bf16_split(x):
    hi = x.astype(BF16).astype(F32)
    lo = (x - hi).astype(BF16).astype(F32)
    return hi, lo


_C_AK = 0
_C_Z = _C_AK + A_KV_DIM
_C_XBC = _C_Z + SSD_D_INNER
_C_CK = _C_XBC + SSD_CONV_DIM
_C_DT = _C_CK + DIFF_HEADS * LANES
_C_END = _C_DT + LANES
_R_AQ = 0
_R_AV = _R_AQ + A_Q_DIM
_R_CQ = _R_AV + A_KV_DIM
_R_CV = _R_CQ + DIFF_QK_WIDTH
_R_END = _R_CV + DIFF_V_WIDTH


def _inproj_kernel(x_ref, xp_ref, xn_ref, nw_ref, w_ref, wt_ref, cw_ref, cb_ref,
                   ak_ref, z_ref, xs_ref, cm_ref, ck_ref, dt_ref, aqt_ref, avt_ref, cqt_ref, cvt_ref, bt_ref,
                   *, tiles_per_seq, diff_key_tile):
    i = pl.program_id(0)
    tm = x_ref.shape[0]
    nw = nw_ref[...]
    h = _rms(x_ref[...], nw).astype(BF16)

    def seg(lo, hi):
        return jnp.dot(h, w_ref[:, lo:hi], preferred_element_type=F32)

    ak_ref[...] = seg(_C_AK, _C_Z).astype(BF16)
    z_ref[...] = seg(_C_Z, _C_XBC)
    pos = (i * tm + lax.broadcasted_iota(I32, (tm, _C_DT - _C_CK), 0)) % diff_key_tile
    ck_ref[...] = (seg(_C_CK, _C_DT) + _diff_key_features(pos)).astype(BF16)
    dt_ref[...] = seg(_C_DT, _C_END)

    w_xbc = w_ref[:, _C_XBC:_C_CK]
    first = i % tiles_per_seq == 0
    last = i % tiles_per_seq == tiles_per_seq - 1
    prev = jnp.dot(_rms(xp_ref[...], nw).astype(BF16), w_xbc, preferred_element_type=F32)
    nxt = jnp.dot(_rms(xn_ref[...], nw).astype(BF16), w_xbc, preferred_element_type=F32)
    prev = jnp.where(first, 0.0, prev)
    nxt = jnp.where(last, 0.0, nxt)
    ext = jnp.concatenate([prev, seg(_C_XBC, _C_CK), nxt], axis=0)
    half = SSD_CONV // 2
    conv = cb_ref[...]
    for k in range(SSD_CONV):
        off = SUBLANES - half + k
        conv = conv + cw_ref[k:k + 1, :] * ext[off:off + tm, :]
    u = _silu(conv)
    xs_ref[...] = u[:, :SSD_D_INNER]
    bt_ref[...] = jnp.transpose(u[:, SSD_D_INNER:SSD_D_INNER + SSD_BC_DIM])
    cm_ref[...] = u[:, SSD_D_INNER + SSD_BC_DIM:]

    tr = lax.dot_general(wt_ref[...], h, (((1,), (1,)), ((), ())), preferred_element_type=F32)
    aqt_ref[...] = (tr[_R_AQ:_R_AV] * (HEAD_DIM ** -0.5)).astype(BF16)
    avt = tr[_R_AV:_R_CQ].astype(BF16)
    for c in range(tm // LANES):
        avt_ref[c] = avt[:, c * LANES:(c + 1) * LANES]
    cqt_ref[...] = (tr[_R_CQ:_R_CV] * (DIFF_QK_DIM ** -0.5 * LOG2E)).astype(BF16)
    pad = DIFF_VROWS - DIFF_V_DIM
    ones_row = (lax.broadcasted_iota(I32, (pad, tm), 0) == 0).astype(BF16)
    for hh in range(DIFF_HEADS):
        cvt_ref[hh * DIFF_VROWS:hh * DIFF_VROWS + DIFF_V_DIM, :] = (
            tr[_R_CV + hh * DIFF_V_DIM:_R_CV + (hh + 1) * DIFF_V_DIM].astype(BF16))
        cvt_ref[hh * DIFF_VROWS + DIFF_V_DIM:(hh + 1) * DIFF_VROWS, :] = ones_row


def _inproj(x2, norm_w, w_main, w_t, conv_w, conv_b, tm, seq, diff_key_tile):
    t, d = x2.shape
    hb = tm // SUBLANES
    n_hblk = t // SUBLANES
    row = lambda i: (i, 0)
    col = lambda i: (0, i)
    full = lambda i: (0, 0)
    row_outs = [(A_KV_DIM, BF16), (SSD_D_INNER, F32), (SSD_D_INNER, F32), (SSD_BC_DIM, F32),
                (DIFF_HEADS * LANES, BF16), (LANES, F32)]
    out_shape = [jax.ShapeDtypeStruct((t, w), dt) for w, dt in row_outs]
    out_specs = [pl.BlockSpec((tm, w), row) for w, _ in row_outs]
    out_shape += [jax.ShapeDtypeStruct((A_Q_DIM, t), BF16),
                  jax.ShapeDtypeStruct((t // LANES, A_KV_DIM, LANES), BF16),
                  jax.ShapeDtypeStruct((DIFF_QK_WIDTH, t), BF16),
                  jax.ShapeDtypeStruct((DIFF_HEADS * DIFF_VROWS, t), BF16),
                  jax.ShapeDtypeStruct((SSD_BC_DIM, t), F32)]
    out_specs += [pl.BlockSpec((A_Q_DIM, tm), col),
                  pl.BlockSpec((tm // LANES, A_KV_DIM, LANES), lambda i: (i, 0, 0)),
                  pl.BlockSpec((DIFF_QK_WIDTH, tm), col),
                  pl.BlockSpec((DIFF_HEADS * DIFF_VROWS, tm), col),
                  pl.BlockSpec((SSD_BC_DIM, tm), col)]
    return pl.pallas_call(
        functools.partial(_inproj_kernel, tiles_per_seq=seq // tm, diff_key_tile=diff_key_tile),
        grid=(t // tm,),
        in_specs=[pl.BlockSpec((tm, d), row),
                  pl.BlockSpec((SUBLANES, d), lambda i: (jnp.maximum(i * hb - 1, 0), 0)),
                  pl.BlockSpec((SUBLANES, d), lambda i: (jnp.minimum((i + 1) * hb, n_hblk - 1), 0)),
                  pl.BlockSpec((1, d), full),
                  pl.BlockSpec(w_main.shape, full), pl.BlockSpec(w_t.shape, full),
                  pl.BlockSpec(conv_w.shape, full), pl.BlockSpec(conv_b.shape, full)],
        out_specs=out_specs,
        out_shape=out_shape,
        compiler_params=_cparams("parallel"),
        name="inproj",
    )(x2, x2, x2, norm_w, w_main, w_t, conv_w, conv_b)


def _swa_kernel(sink_ref, slope_ref, qt_ref, k_ref, vt_ref, o_ref):
    step = pl.program_id(1)
    s_len = k_ref.shape[0]
    blk = SWA_WINDOW
    band = 3 * blk
    nb = s_len // blk
    rep = SWA_HEADS // SWA_KV_HEADS
    hd = HEAD_DIM
    for u in range(SWA_BLOCKS_PER_STEP):
        n = step * SWA_BLOCKS_PER_STEP + u
        start_blk = jnp.clip(n - 1, 0, nb - 3)
        start = pl.multiple_of(start_blk * blk, blk)
        kb = k_ref[pl.ds(start, band), :]
        v3 = vt_ref[pl.ds(start_blk, 3)]
        vtb = jnp.concatenate([v3[0], v3[1], v3[2]], axis=1)
        qt = qt_ref[:, u * blk:(u + 1) * blk]
        zero = jnp.zeros((hd, rep * blk), BF16)
        grp = [jnp.concatenate([qt[(g * rep + r) * hd:(g * rep + r + 1) * hd] for r in range(rep)], axis=1)
               for g in range(SWA_KV_HEADS)]
        qbd = jnp.concatenate([jnp.concatenate([grp[0], zero], axis=1),
                               jnp.concatenate([zero, grp[1]], axis=1)], axis=0)
        st = jnp.dot(kb, qbd, preferred_element_type=F32)
        kpos = start + lax.broadcasted_iota(I32, (band, blk), 0)
        qpos = n * blk + lax.broadcasted_iota(I32, (band, blk), 1)
        dist_i = jnp.abs(qpos - kpos)
        valid = dist_i <= SWA_WINDOW
        dist = dist_i.astype(F32)
        ps, inv = [], []
        for h in range(SWA_HEADS):
            s = jnp.where(valid, st[:, h * blk:(h + 1) * blk] - slope_ref[h] * dist, NEG)
            sink = sink_ref[h]
            m = jnp.maximum(jnp.max(s, axis=0, keepdims=True), sink)
            p = jnp.exp(s - m)
            inv.append(1.0 / (jnp.sum(p, axis=0, keepdims=True) + jnp.exp(sink - m)))
            ps.append(p.astype(BF16))
        outs = []
        for g in range(SWA_KV_HEADS):
            pg = jnp.concatenate(ps[g * rep:(g + 1) * rep], axis=1)
            og = jnp.dot(vtb[g * hd:(g + 1) * hd, :], pg, preferred_element_type=F32)
            for r in range(rep):
                outs.append(og[:, r * blk:(r + 1) * blk] * inv[g * rep + r])
        o_ref[u * blk:(u + 1) * blk, :] = jnp.transpose(jnp.concatenate(outs, axis=0)).astype(o_ref.dtype)


def _swa(aqt, ak, avt3, sink, slopes, batch, seq):
    blk = SWA_WINDOW
    rows = blk * SWA_BLOCKS_PER_STEP
    steps = seq // rows
    nb = seq // blk
    t = batch * seq
    smem = pl.BlockSpec(memory_space=pltpu.SMEM)
    return pl.pallas_call(
        _swa_kernel,
        grid=(batch, steps),
        in_specs=[smem, smem,
                  pl.BlockSpec((A_Q_DIM, rows), lambda b, s: (0, b * steps + s)),
                  pl.BlockSpec((seq, A_KV_DIM), lambda b, s: (b, 0)),
                  pl.BlockSpec((nb, A_KV_DIM, blk), lambda b, s: (b, 0, 0))],
        out_specs=pl.BlockSpec((rows, A_Q_DIM), lambda b, s: (b * steps + s, 0)),
        out_shape=jax.ShapeDtypeStruct((t, A_Q_DIM), BF16),
        compiler_params=_cparams("parallel", "parallel"),
        name="swa",
    )(sink, slopes, aqt, ak, avt3)


def _diff_key_tile(i, j, tq, tk, nk):
    return ((i * tq) // tk + j) % nk


def _diff_key_features(pos_in_tile):
    lane = lax.broadcasted_iota(I32, pos_in_tile.shape, 1) % LANES - 2 * DIFF_QK_DIM
    coarse = ((pos_in_tile // 16) * 16).astype(F32)
    fine = (pos_in_tile % 16).astype(F32)
    f = lane % DIFF_NFEAT
    feat = jnp.where(f < 2, coarse, jnp.where(f < 4, fine, 1.0))
    return jnp.where((lane >= 0) & (lane < 2 * DIFF_NFEAT), feat, 0.0)


def _diff_kernel(slope_ref, qt_ref, k_ref, vt_ref, lam_ref, sw_ref, o_ref, qtb_ref, m_ref, acc_ref,
                 s_ref, p_ref, *, lambda_init, n_split):
    i = pl.program_id(1)
    j = pl.program_id(2)
    nk = pl.num_programs(2)
    tq = qt_ref.shape[1]
    tk = k_ref.shape[0]
    dq = DIFF_QK_DIM
    hw = 2 * dq
    nf = DIFF_NFEAT

    @pl.when(j == 0)
    def _():
        m_ref[...] = jnp.full(m_ref.shape, NEG, F32)
        acc_ref[...] = jnp.zeros(acc_ref.shape, F32)
        ii = lax.broadcasted_iota(I32, (1, 2 * tq), 1)
        ii = jnp.where(ii >= tq, ii - tq, ii).astype(F32)
        qt = qt_ref[...]
        col = lax.broadcasted_iota(I32, (hw, 2 * tq), 1)
        row = lax.broadcasted_iota(I32, (hw, 2 * tq), 0)
        own_map = row // dq == col // tq
        for h in range(DIFF_HEADS):
            qh = qt[h * hw:(h + 1) * hw, :]
            qh2 = jnp.where(own_map, jnp.concatenate([qh, qh], axis=1), jnp.zeros((hw, 2 * tq), BF16))
            sl = jnp.full((1, 2 * tq), slope_ref[h], F32)
            s_hi, s_lo = _bf16_split(sl)
            v_hi, v_lo = _bf16_split(-sl * ii)
            rows = jnp.concatenate([s_hi, s_lo, s_hi, s_lo, v_hi, v_lo], axis=0)
            zrow = jnp.zeros((nf, 2 * tq), F32)
            zero = jnp.zeros((hw - 2 * nf, 2 * tq), F32)
            variants = ([rows, zrow], [zrow, -rows], [zrow, zrow])
            for v, pieces in enumerate(variants):
                qtb_ref[v, h, 0:hw, :] = qh2
                qtb_ref[v, h, hw:2 * hw, :] = jnp.concatenate(pieces + [zero], axis=0).astype(BF16)

    q0 = i * tq
    k0 = _diff_key_tile(i, j, tq, tk, nk) * tk

    width = 2 * tq // n_split
    items = [(h, mp) for h in range(DIFF_HEADS) for mp in range(n_split)]

    def lanes(mp):
        return slice(mp * width, (mp + 1) * width)

    def scores(h, mp, variant):
        return jnp.dot(k_ref[:, h * LANES:(h + 1) * LANES], qtb_ref[variant, h, :, lanes(mp)],
                       preferred_element_type=F32)

    def stage_scores(h, mp, s):
        s_ref[h, :, lanes(mp)] = s
        return jnp.max(s, axis=0, keepdims=True)

    def softmax_step(h, mp, tile_max, shift):
        m_old = m_ref[h:h + 1, lanes(mp)]
        m_new = jnp.maximum(m_old, tile_max + shift)
        p_ref[h, :, lanes(mp)] = jnp.exp2(s_ref[h, :, lanes(mp)] - (m_new - shift)).astype(BF16)
        m_ref[h:h + 1, lanes(mp)] = m_new
        return jnp.exp2(m_old - m_new)

    def accumulate(h, mp, alpha):
        pv = jnp.dot(vt_ref[h * DIFF_VROWS:(h + 1) * DIFF_VROWS, :], p_ref[h, :, lanes(mp)],
                     preferred_element_type=F32)
        acc_ref[h, :, lanes(mp)] = alpha * acc_ref[h, :, lanes(mp)] + pv

    def tile_pass(score_fn, shift_fn):
        maxima, alphas = {}, {}
        for w in range(len(items) + 2):
            if w < len(items):
                h, mp = items[w]
                maxima[w] = stage_scores(h, mp, score_fn(h, mp))
            if 0 <= w - 1 < len(items):
                h, mp = items[w - 1]
                alphas[w - 1] = softmax_step(h, mp, maxima[w - 1], shift_fn(h))
            if 0 <= w - 2 < len(items):
                h, mp = items[w - 2]
                accumulate(h, mp, alphas[w - 2])

    @pl.when(j == 0)
    def _():
        kpos = k0 + lax.broadcasted_iota(I32, (tk, width), 0)
        qpos = q0 + lax.broadcasted_iota(I32, (tk, width), 1) % tq
        dist = jnp.abs(qpos - kpos).astype(F32)
        tile_pass(lambda h, mp: scores(h, mp, 2) - slope_ref[h] * dist, lambda h: 0.0)

    @pl.when(j > 0)
    def _():
        below = k0 < q0
        variant = jnp.where(below, 0, 1)
        sign = jnp.where(below, 1.0, -1.0)
        tile_pass(lambda h, mp: scores(h, mp, variant),
                  lambda h: sign * slope_ref[h] * (k0 - q0).astype(F32))

    @pl.when(j == nk - 1)
    def _():
        lp = lam_ref[...]
        lam = (jnp.exp(jnp.sum(lp[0:1] * lp[1:2], axis=-1, keepdims=True))
               - jnp.exp(jnp.sum(lp[2:3] * lp[3:4], axis=-1, keepdims=True)) + lambda_init)
        outs = []
        for h in range(DIFF_HEADS):
            a = acc_ref[h]
            o = a[0:DIFF_V_DIM] / a[DIFF_V_DIM:DIFF_V_DIM + 1]
            o = o[:, 0:tq] - lam * o[:, tq:2 * tq]
            ms = jnp.mean(o * o, axis=0, keepdims=True)
            outs.append(o * lax.rsqrt(ms + NORM_EPS) * sw_ref[...] * (1.0 - lambda_init))
        o_ref[...] = jnp.transpose(jnp.concatenate(outs, axis=0)).astype(o_ref.dtype)


def _diff(cqt, ck, cvt, slopes, lam_params, subln_w_col, lambda_init, batch, seq, tq, tk):
    assert tk % tq == 0 and seq % tk == 0
    nq, nk = seq // tq, seq // tk
    t = batch * seq
    smem = pl.BlockSpec(memory_space=pltpu.SMEM)
    key_tile = functools.partial(_diff_key_tile, tq=tq, tk=tk, nk=nk)
    return pl.pallas_call(
        functools.partial(_diff_kernel, lambda_init=lambda_init, n_split=DIFF_ITEM_SPLIT),
        grid=(batch, nq, nk),
        in_specs=[smem,
                  pl.BlockSpec((DIFF_QK_WIDTH, tq), lambda b, i, j: (0, b * nq + i)),
                  pl.BlockSpec((tk, DIFF_HEADS * LANES), lambda b, i, j: (b * nk + key_tile(i, j), 0)),
                  pl.BlockSpec((DIFF_HEADS * DIFF_VROWS, tk), lambda b, i, j: (0, b * nk + key_tile(i, j))),
                  pl.BlockSpec(lam_params.shape, lambda b, i, j: (0, 0)),
                  pl.BlockSpec(subln_w_col.shape, lambda b, i, j: (0, 0))],
        out_specs=pl.BlockSpec((tq, DIFF_V_WIDTH), lambda b, i, j: (b * nq + i, 0)),
        out_shape=jax.ShapeDtypeStruct((t, DIFF_V_WIDTH), BF16),
        scratch_shapes=[pltpu.VMEM((3, DIFF_HEADS, LANES, 2 * tq), BF16),
                        pltpu.VMEM((DIFF_HEADS, 2 * tq), F32),
                        pltpu.VMEM((DIFF_HEADS, DIFF_VROWS, 2 * tq), F32),
                        pltpu.VMEM((DIFF_HEADS, tk, 2 * tq), F32),
                        pltpu.VMEM((DIFF_HEADS, tk, 2 * tq), BF16)],
        compiler_params=_cparams("parallel", "parallel", "arbitrary"),
        name="diffattn",
    )(slopes, cqt, ck, cvt, lam_params, subln_w_col)


def _ssd_direction(fwd, xs_ref, bt_ref, cm_ref, dt_ref, dtb_ref, alog_ref, dsk_ref, y_ref, state_ref):
    q = xs_ref.shape[0]
    lane0 = 0 if fwd else SSD_HEADS
    dt_all = _softplus(dt_ref[...] + dtb_ref[...])
    dta_all = dt_all * -jnp.exp(alog_ref[...])
    row = lax.broadcasted_iota(I32, (q, q), 0)
    col = lax.broadcasted_iota(I32, (q, q), 1)
    keep = (row >= col) if fwd else (row <= col)
    da_all = jnp.dot(keep.astype(F32), dta_all, precision=lax.Precision.HIGHEST,
                     preferred_element_type=F32)
    da_all_t = jnp.transpose(da_all)
    dt_all_t = jnp.transpose(dt_all)
    tot_all = jnp.sum(dta_all, axis=0, keepdims=True)

    xs = xs_ref[...]
    bmt = bt_ref[...]
    cm = cm_ref[...]
    rep = SSD_HEADS // SSD_GROUPS
    ns = SSD_STATE
    g_mats = [jnp.dot(cm[:, g * ns:(g + 1) * ns].astype(BF16), bmt[g * ns:(g + 1) * ns, :].astype(BF16),
                      preferred_element_type=F32) for g in range(SSD_GROUPS)]
    ys = []
    for h in range(SSD_HEADS):
        g = h // rep
        ln = lane0 + h
        dac = da_all[:, ln:ln + 1]
        dar = da_all_t[ln:ln + 1, :]
        dtr = dt_all_t[ln:ln + 1, :]
        tot = tot_all[:, ln:ln + 1]
        dac_b = jnp.broadcast_to(dac, (q, q))
        decay = jnp.exp(jnp.where(keep, dac_b - dar, NEG))
        xh = xs[:, h * SSD_HEAD_DIM:(h + 1) * SSD_HEAD_DIM]
        xhb = xh.astype(BF16)
        y = jnp.dot((g_mats[g] * decay * dtr).astype(BF16), xhb, preferred_element_type=F32)
        st = state_ref[h]
        c_in = (cm[:, g * ns:(g + 1) * ns] * jnp.exp(dac_b[:, 0:ns])).astype(BF16)
        y = y + jnp.dot(c_in, st.astype(BF16), preferred_element_type=F32)
        to_end = jnp.exp(tot - dar) * dtr
        b_out = (bmt[g * ns:(g + 1) * ns, :] * to_end).astype(BF16)
        state_ref[h] = st * jnp.exp(tot) + jnp.dot(b_out, xhb, preferred_element_type=F32)
        if fwd:
            y = y + dsk_ref[:, h:h + 1] * xh
        ys.append(y)
    y_ref[...] = jnp.concatenate(ys, axis=-1)


def _ssd_kernel(xsf_ref, btf_ref, cmf_ref, dtf_ref, xsb_ref, btb_ref, cmb_ref, dtb_in_ref,
                dtbias_ref, alog_ref, dsk_ref, yf_ref, yb_ref, state_ref):
    @pl.when(pl.program_id(1) == 0)
    def _():
        state_ref[...] = jnp.zeros(state_ref.shape, F32)

    _ssd_direction(True, xsf_ref, btf_ref, cmf_ref, dtf_ref, dtbias_ref, alog_ref, dsk_ref, yf_ref,
                   state_ref.at[0])
    _ssd_direction(False, xsb_ref, btb_ref, cmb_ref, dtb_in_ref, dtbias_ref, alog_ref, dsk_ref, yb_ref,
                   state_ref.at[1])


def _ssd(xs, bt, cm, dt_raw, dt_bias, a_log, d_skip, batch, seq):
    q = SSD_CHUNK
    nc = seq // q
    t = batch * seq
    full = lambda b, c: (0, 0)
    fw = lambda b, c: b * nc + c
    bw = lambda b, c: b * nc + nc - 1 - c

    def specs(idx):
        return [pl.BlockSpec((q, SSD_D_INNER), lambda b, c: (idx(b, c), 0)),
                pl.BlockSpec((SSD_BC_DIM, q), lambda b, c: (0, idx(b, c))),
                pl.BlockSpec((q, SSD_BC_DIM), lambda b, c: (idx(b, c), 0)),
                pl.BlockSpec((q, LANES), lambda b, c: (idx(b, c), 0))]

    return pl.pallas_call(
        _ssd_kernel,
        grid=(batch, nc),
        in_specs=specs(fw) + specs(bw) + [pl.BlockSpec(dt_bias.shape, full), pl.BlockSpec(a_log.shape, full),
                                          pl.BlockSpec(d_skip.shape, full)],
        out_specs=[pl.BlockSpec((q, SSD_D_INNER), lambda b, c: (fw(b, c), 0)),
                   pl.BlockSpec((q, SSD_D_INNER), lambda b, c: (bw(b, c), 0))],
        out_shape=[jax.ShapeDtypeStruct((t, SSD_D_INNER), F32), jax.ShapeDtypeStruct((t, SSD_D_INNER), F32)],
        scratch_shapes=[pltpu.VMEM((2, SSD_HEADS, SSD_STATE, SSD_HEAD_DIM), F32)],
        compiler_params=_cparams("parallel", "arbitrary"),
        name="ssd",
    )(xs, bt, cm, dt_raw, xs, bt, cm, dt_raw, dt_bias, a_log, d_skip)


def _outproj_kernel(x_ref, ya_ref, yf_ref, yb_ref, z_ref, snw_ref, yc_ref, wo_ref, fnw_ref, wrt_ref, br_ref,
                    xn_ref, ri_ref, rf_ref, cnt_ref, tri_ref, carry_ref):
    step = pl.program_id(0)
    tm = x_ref.shape[0]

    @pl.when(step == 0)
    def _():
        carry_ref[...] = jnp.zeros(carry_ref.shape, F32)
        r = lax.broadcasted_iota(I32, (tm, tm), 0)
        cc = lax.broadcasted_iota(I32, (tm, tm), 1)
        tri_ref[...] = (r <= cc).astype(BF16)

    y = (yf_ref[...] + yb_ref[...]) * _silu(z_ref[...])
    yb = _rms(y, snw_ref[...]).astype(BF16)
    acc = jnp.dot(ya_ref[...], wo_ref[0:A_Q_DIM, :], preferred_element_type=F32)
    acc = acc + jnp.dot(yb, wo_ref[A_Q_DIM:A_Q_DIM + SSD_D_INNER, :], preferred_element_type=F32)
    acc = acc + jnp.dot(yc_ref[...], wo_ref[A_Q_DIM + SSD_D_INNER:, :], preferred_element_type=F32)
    xn = x_ref[...] + acc
    xn_ref[...] = xn

    h = _rms(xn, fnw_ref[...])
    logits = lax.dot_general(wrt_ref[...], h, (((1,), (1,)), ((), ())), precision=lax.Precision.HIGHEST,
                             preferred_element_type=F32) + br_ref[...]
    ne, epg, ng = N_EXPERTS, EXPERTS_PER_GROUP, N_EXPERT_GROUPS
    gl = logits[ne:ne + ng, :]
    gmax = jnp.max(gl, axis=0, keepdims=True)
    g_sel = jnp.full((1, tm), float(ng - 1), F32)
    for g in range(ng - 2, -1, -1):
        g_sel = jnp.where(gl[g:g + 1, :] == gmax, float(g), g_sel)
    g_gate = 1.0 / jnp.sum(jnp.exp(gl - gmax), axis=0, keepdims=True)
    e_in = logits[0:epg, :]
    for g in range(1, ng):
        e_in = jnp.where(g_sel == float(g), logits[g * epg:(g + 1) * epg, :], e_in)
    sub = lax.broadcasted_iota(I32, (epg, tm), 0).astype(F32)
    m1 = jnp.max(e_in, axis=0, keepdims=True)
    i1 = jnp.min(jnp.where(e_in == m1, sub, float(epg)), axis=0, keepdims=True)
    rest = jnp.where(sub == i1, NEG, e_in)
    m2 = jnp.max(rest, axis=0, keepdims=True)
    i2 = jnp.min(jnp.where(rest == m2, sub, float(epg)), axis=0, keepdims=True)
    r = jnp.exp(m2 - m1)
    c1 = g_gate / (1.0 + r)
    c2 = g_gate * r / (1.0 + r)
    e1 = (g_sel * epg + i1).astype(I32)
    e2 = (g_sel * epg + i2).astype(I32)

    erow = lax.broadcasted_iota(I32, (ne, tm), 0)
    hit1 = erow == e1
    hit2 = erow == e2
    oh = jnp.where(hit1 | hit2, 1.0, 0.0)
    incl = jnp.dot(oh.astype(BF16), tri_ref[...], preferred_element_type=F32)
    before = incl - oh + carry_ref[:, 0:1]
    rank1 = jnp.sum(jnp.where(hit1, before, 0.0), axis=0, keepdims=True)
    rank2 = jnp.sum(jnp.where(hit2, before, 0.0), axis=0, keepdims=True)
    carry_ref[...] = carry_ref[...] + jnp.sum(oh, axis=1, keepdims=True)
    cnt_ref[...] = carry_ref[...]
    zi = jnp.zeros((1, tm), I32)
    ri_ref[...] = jnp.concatenate([e1, e2, rank1.astype(I32), rank2.astype(I32), zi, zi, zi, zi], axis=0)
    zf = jnp.zeros((1, tm), F32)
    rf_ref[...] = jnp.concatenate([c1, c2, zf, zf, zf, zf, zf, zf], axis=0)


def _outproj(x2, ya, yf, yb, z, ssd_norm_w, yc, w_out, ffn_norm_w, wrt, br, tm):
    t, d = x2.shape
    row = lambda i: (i, 0)
    full = lambda i: (0, 0)
    return pl.pallas_call(
        _outproj_kernel,
        grid=(t // tm,),
        in_specs=[pl.BlockSpec((tm, d), row),
                  pl.BlockSpec((tm, A_Q_DIM), row),
                  pl.BlockSpec((tm, SSD_D_INNER), row),
                  pl.BlockSpec((tm, SSD_D_INNER), row),
                  pl.BlockSpec((tm, SSD_D_INNER), row),
                  pl.BlockSpec(ssd_norm_w.shape, full),
                  pl.BlockSpec((tm, DIFF_V_WIDTH), row),
                  pl.BlockSpec(w_out.shape, full),
                  pl.BlockSpec(ffn_norm_w.shape, full),
                  pl.BlockSpec(wrt.shape, full),
                  pl.BlockSpec(br.shape, full)],
        out_specs=[pl.BlockSpec((tm, d), row),
                   pl.BlockSpec((SUBLANES, tm), lambda i: (0, i)),
                   pl.BlockSpec((SUBLANES, tm), lambda i: (0, i)),
                   pl.BlockSpec((N_EXPERTS, LANES), full)],
        out_shape=[jax.ShapeDtypeStruct((t, d), F32),
                   jax.ShapeDtypeStruct((SUBLANES, t), I32),
                   jax.ShapeDtypeStruct((SUBLANES, t), F32),
                   jax.ShapeDtypeStruct((N_EXPERTS, LANES), F32)],
        scratch_shapes=[pltpu.VMEM((tm, tm), BF16), pltpu.VMEM((N_EXPERTS, LANES), F32)],
        compiler_params=_cparams("arbitrary"),
        name="outproj_router",
    )(x2, ya, yf, yb, z, ssd_norm_w, yc, w_out, ffn_norm_w, wrt, br)


_PAD_PIECES = tuple(1 << b for b in reversed(range(MOE_ROW_TILE.bit_length() - 1)))


def _dispatch_kernel(slot1_ref, slot2_ref, pstart_ref, plen_ref, nused_ref, x_ref, xs_hbm, zero_ref, sem):
    i = pl.program_id(0)
    tm = x_ref.shape[0]

    @pl.when(i == 0)
    def _():
        zero_ref[...] = jnp.zeros(zero_ref.shape, F32)

        def pieces(e, wait):
            n = plen_ref[e]
            first = pstart_ref[e]
            off = first + n
            for b in _PAD_PIECES:
                off = off - (n & b)
                dst = pl.ds(pl.multiple_of(off, b), b) if b >= SUBLANES else None
                if dst is not None:
                    @pl.when((n & b) != 0)
                    def _():
                        cp = pltpu.make_async_copy(zero_ref.at[pl.ds(0, b)], xs_hbm.at[dst], sem.at[1])
                        cp.wait() if wait else cp.start()

            for u in range(SUBLANES - 1):
                @pl.when(u < (n & (SUBLANES - 1)))
                def _():
                    cp = pltpu.make_async_copy(zero_ref.at[pl.ds(0, 1)], xs_hbm.at[pl.ds(first + u, 1)],
                                               sem.at[1])
                    cp.wait() if wait else cp.start()

        def tail(tile, wait):
            big = _PAD_PIECES[0]
            for part in range(MOE_ROW_TILE // big):
                dst = xs_hbm.at[pl.ds(pl.multiple_of(tile * MOE_ROW_TILE + part * big, big), big)]
                cp = pltpu.make_async_copy(zero_ref, dst, sem.at[1])
                cp.wait() if wait else cp.start()

        def loop(fn, lo, hi, wait):
            def body(k, carry):
                fn(k, wait)
                return carry

            lax.fori_loop(lo, hi, body, 0)

        n_tiles = xs_hbm.shape[0] // MOE_ROW_TILE
        for wait in (False, True):
            loop(pieces, 0, N_EXPERTS, wait)
            loop(tail, nused_ref[0], n_tiles, wait)

    base = i * tm
    for r in range(tm):
        src = x_ref.at[pl.ds(r, 1)]
        pltpu.make_async_copy(src, xs_hbm.at[pl.ds(slot1_ref[base + r], 1)], sem.at[0]).start(priority=0)
        pltpu.make_async_copy(src, xs_hbm.at[pl.ds(slot2_ref[base + r], 1)], sem.at[0]).start(priority=1)
    for _ in range(2):
        pltpu.make_async_copy(x_ref, xs_hbm.at[pl.ds(0, tm)], sem.at[0]).wait()


def _dispatch(xn, slot1, slot2, pad_start, pad_len, n_used, n_rows, tm):
    t, d = xn.shape
    grid_spec = pltpu.PrefetchScalarGridSpec(
        num_scalar_prefetch=5,
        grid=(t // tm,),
        in_specs=[pl.BlockSpec((tm, d), lambda i, s1, s2, ps, pn, nu: (i, 0))],
        out_specs=pl.BlockSpec(memory_space=pl.ANY),
        scratch_shapes=[pltpu.VMEM((_PAD_PIECES[0], d), F32), pltpu.SemaphoreType.DMA((2,))],
    )
    return pl.pallas_call(
        _dispatch_kernel,
        grid_spec=grid_spec,
        out_shape=jax.ShapeDtypeStruct((n_rows, d), F32),
        compiler_params=_cparams("arbitrary"),
        name="moe_dispatch",
    )(slot1, slot2, pad_start, pad_len, n_used, xn)


def _moe_kernel(texp_ref, nused_ref, x_ref, fnw_ref, wg_ref, wu_ref, wd_ref, y_ref):
    i = pl.program_id(0)

    @pl.when(i < nused_ref[0])
    def _():
        h = _rms(x_ref[...], fnw_ref[...]).astype(BF16)
        hg = jnp.dot(h, wg_ref[...].astype(BF16), preferred_element_type=F32)
        hu = jnp.dot(h, wu_ref[...].astype(BF16), preferred_element_type=F32)
        act = (_silu(hg) * hu).astype(BF16)
        y_ref[...] = jnp.dot(act, wd_ref[...].astype(BF16), preferred_element_type=F32)

    @pl.when(i >= nused_ref[0])
    def _():
        y_ref[...] = jnp.zeros(y_ref.shape, F32)


def _moe(xs, ffn_norm_w, w_gate, w_up, w_down, layer, tile_expert, n_used):
    n_rows, d = xs.shape
    f = w_gate.shape[-1]
    tr = MOE_ROW_TILE

    def used(i, nu):
        return jnp.maximum(jnp.minimum(i, nu[0] - 1), 0)

    grid_spec = pltpu.PrefetchScalarGridSpec(
        num_scalar_prefetch=2,
        grid=(n_rows // tr,),
        in_specs=[pl.BlockSpec((tr, d), lambda i, te, nu: (used(i, nu), 0)),
                  pl.BlockSpec(ffn_norm_w.shape, lambda i, te, nu: (0, 0)),
                  pl.BlockSpec((None, None, d, f), lambda i, te, nu: (layer, te[used(i, nu)], 0, 0)),
                  pl.BlockSpec((None, None, d, f), lambda i, te, nu: (layer, te[used(i, nu)], 0, 0)),
                  pl.BlockSpec((None, None, f, d), lambda i, te, nu: (layer, te[used(i, nu)], 0, 0))],
        out_specs=pl.BlockSpec((tr, d), lambda i, te, nu: (i, 0)),
    )
    return pl.pallas_call(
        _moe_kernel,
        grid_spec=grid_spec,
        out_shape=jax.ShapeDtypeStruct((n_rows, d), F32),
        compiler_params=_cparams("arbitrary"),
        name="moe_experts",
    )(tile_expert, n_used, xs, ffn_norm_w, w_gate, w_up, w_down)


def _combine_kernel(slot1_ref, slot2_ref, x_ref, cw_ref, nw_ref, y_hbm, o_ref, ybuf, sem, *, final_norm):
    i = pl.program_id(0)
    n = pl.num_programs(0)
    tm = x_ref.shape[0]

    def start_gather(tile, slot):
        base = tile * tm
        for r in range(tm):
            pltpu.make_async_copy(y_hbm.at[pl.ds(slot1_ref[base + r], 1)], ybuf.at[slot, 0, pl.ds(r, 1)],
                                  sem.at[slot]).start(priority=0)
            pltpu.make_async_copy(y_hbm.at[pl.ds(slot2_ref[base + r], 1)], ybuf.at[slot, 1, pl.ds(r, 1)],
                                  sem.at[slot]).start(priority=1)

    def compute(slot):
        for k in range(2):
            pltpu.make_async_copy(y_hbm.at[pl.ds(0, tm)], ybuf.at[slot, k], sem.at[slot]).wait()
        cw = cw_ref[...]
        out = x_ref[...] + cw[:, 0:1] * ybuf[slot, 0] + cw[:, 1:2] * ybuf[slot, 1]
        if final_norm:
            out = _rms(out, nw_ref[...])
        o_ref[...] = out

    @pl.when(i == 0)
    def _():
        start_gather(0, 0)

    for parity in range(2):
        @pl.when(i % 2 == parity)
        def _():
            @pl.when(i + 1 < n)
            def _():
                start_gather(i + 1, 1 - parity)

            compute(parity)


def _combine(xn, cw, norm_w, y_sorted, slot1, slot2, tm, final_norm):
    t, d = xn.shape
    grid_spec = pltpu.PrefetchScalarGridSpec(
        num_scalar_prefetch=2,
        grid=(t // tm,),
        in_specs=[pl.BlockSpec((tm, d), lambda i, s1, s2: (i, 0)),
                  pl.BlockSpec((tm, cw.shape[1]), lambda i, s1, s2: (i, 0)),
                  pl.BlockSpec(norm_w.shape, lambda i, s1, s2: (0, 0)),
                  pl.BlockSpec(memory_space=pl.ANY)],
        out_specs=pl.BlockSpec((tm, d), lambda i, s1, s2: (i, 0)),
        scratch_shapes=[pltpu.VMEM((2, 2, tm, d), F32), pltpu.SemaphoreType.DMA((2,))],
    )
    return pl.pallas_call(
        functools.partial(_combine_kernel, final_norm=final_norm),
        grid_spec=grid_spec,
        out_shape=jax.ShapeDtypeStruct((t, d), F32),
        compiler_params=_cparams("arbitrary"),
        name="moe_combine",
    )(slot1, slot2, xn, cw, norm_w, y_sorted)


def _pad_lanes(v):
    v = v.reshape(1, -1).astype(F32)
    return jnp.pad(v, ((0, 0), (0, LANES - v.shape[1])))


def kernel(x, attn_norm_w, w_in, swa_sink, ssd_conv_w, ssd_conv_b, ssd_dt_bias, ssd_a_log, ssd_d, ssd_norm_w,
           diff_lambda, diff_subln_w, w_out, ffn_norm_w, w_router_group, b_router_group, w_router_expert,
           b_router_expert, w_gate, w_up, w_down, final_norm_w):
    return _forward(x, attn_norm_w, w_in, swa_sink, ssd_conv_w, ssd_conv_b, ssd_dt_bias, ssd_a_log, ssd_d,
                    ssd_norm_w, diff_lambda, diff_subln_w, w_out, ffn_norm_w, w_router_group, b_router_group,
                    w_router_expert, b_router_expert, w_gate, w_up, w_down, final_norm_w)


def _forward(x, attn_norm_w, w_in, swa_sink, ssd_conv_w, ssd_conv_b, ssd_dt_bias, ssd_a_log, ssd_d, ssd_norm_w,
             diff_lambda, diff_subln_w, w_out, ffn_norm_w, w_router_group, b_router_group, w_router_expert,
             b_router_expert, w_gate, w_up, w_down, final_norm_w, tm=512, tq=512, tk=1024, tmc=256):
    batch, seq, d = x.shape
    depth = w_in.shape[0]
    t = batch * seq
    tr = MOE_ROW_TILE
    n_tiles = (2 * t) // tr + N_EXPERTS
    slopes = jnp.exp2(-8.0 * jnp.arange(1, N_ALIBI_HEADS + 1, dtype=F32) / N_ALIBI_HEADS)
    swa_slopes, diff_slopes = slopes[:SWA_HEADS], slopes[SWA_HEADS:]

    sizes = [A_Q_DIM, A_KV_DIM, A_KV_DIM, SSD_D_INNER, SSD_CONV_DIM, SSD_DT_DIM, DIFF_QK_WIDTH, DIFF_QK_WIDTH,
             DIFF_V_WIDTH]
    offs = [0]
    for s in sizes:
        offs.append(offs[-1] + s)
    o_aq, o_ak, o_av, o_z, o_xbc, o_dt, o_cq, o_ck, o_cv, o_end = offs

    x2 = x.reshape(t, d)
    for l in range(depth):
        w = w_in[l]
        hw = 2 * DIFF_QK_DIM
        w_ck = jnp.pad(w[:, o_ck:o_cv].reshape(d, DIFF_HEADS, hw), ((0, 0), (0, 0), (0, LANES - hw)))
        w_main = jnp.concatenate(
            [w[:, o_ak:o_av], w[:, o_z:o_dt], w_ck.reshape(d, DIFF_HEADS * LANES), w[:, o_dt:o_cq],
             jnp.zeros((d, LANES - SSD_DT_DIM), w.dtype)], axis=1).astype(BF16)
        w_t = jnp.concatenate([w[:, o_aq:o_ak], w[:, o_av:o_z], w[:, o_cq:o_ck], w[:, o_cv:o_end]],
                              axis=1).T.astype(BF16)
        ak, z, xs, cm, ck, dt_raw, aqt, avt3, cqt, cvt, bt = _inproj(
            x2, attn_norm_w[l].reshape(1, d), w_main, w_t, ssd_conv_w[l].astype(F32),
            ssd_conv_b[l].reshape(1, -1).astype(F32), tm, seq, tk)

        ya = _swa(aqt, ak, avt3, swa_sink[l].astype(F32), swa_slopes, batch, seq)
        lambda_init = 0.8 - 0.6 * math.exp(-0.3 * l)
        yc = _diff(cqt, ck, cvt, diff_slopes * LOG2E, diff_lambda[l].astype(F32),
                   diff_subln_w[l].reshape(DIFF_V_DIM, 1).astype(F32), lambda_init, batch, seq, tq, tk)
        yf, yb = _ssd(xs, bt, cm, dt_raw, _pad_lanes(ssd_dt_bias[l]), _pad_lanes(ssd_a_log[l]),
                      _pad_lanes(ssd_d[l]), batch, seq)

        wrt = jnp.concatenate([w_router_expert[l], w_router_group[l],
                               jnp.zeros((d, SUBLANES - N_EXPERT_GROUPS), F32)], axis=1).T.astype(F32)
        br = jnp.concatenate([b_router_expert[l], b_router_group[l],
                              jnp.zeros((SUBLANES - N_EXPERT_GROUPS,), F32)]).reshape(-1, 1).astype(F32)
        xn, ri, rf, cnt = _outproj(x2, ya, yf, yb, z, ssd_norm_w[l].reshape(1, -1), yc, w_out[l].astype(BF16),
                                   ffn_norm_w[l].reshape(1, d), wrt, br, tm)

        counts = cnt[:, 0].astype(I32)
        padded = ((counts + tr - 1) // tr) * tr
        ends = jnp.cumsum(padded)
        starts = ends - padded
        experts = jnp.arange(N_EXPERTS, dtype=I32)[:, None]

        def slot_of(e, rank):
            return jnp.sum(jnp.where(e[None, :] == experts, starts[:, None], 0), axis=0) + rank

        slot1 = slot_of(ri[0], ri[2])
        slot2 = slot_of(ri[1], ri[3])
        tile_start = jnp.arange(n_tiles, dtype=I32) * tr
        tile_expert = jnp.minimum(jnp.sum(ends[None, :] <= tile_start[:, None], axis=1), N_EXPERTS - 1).astype(I32)
        n_used = (ends[-1] // tr).astype(I32).reshape(1)

        xs_sorted = _dispatch(xn, slot1, slot2, starts + counts, padded - counts, n_used, n_tiles * tr, tmc)
        y_sorted = _moe(xs_sorted, ffn_norm_w[l].reshape(1, d), w_gate, w_up, w_down, l, tile_expert, n_used)
        last = l == depth - 1
        x2 = _combine(xn, rf.T, final_norm_w.reshape(1, d), y_sorted, slot1, slot2, tmc, last)
    return x2.reshape(batch, seq, d)
```

```python
import functools
import math

import jax
import jax.numpy as jnp
from jax import lax
from jax.experimental import pallas as pl
from jax.experimental.pallas import tpu as pltpu

F32 = jnp.float32
BF16 = jnp.bfloat16
I32 = jnp.int32

HEAD_DIM = 64
SWA_HEADS = 6
SWA_KV_HEADS = 2
SWA_WINDOW = 128
SSD_HEADS = 6
SSD_HEAD_DIM = 64
SSD_GROUPS = 2
SSD_STATE = 64
SSD_CONV = 5
DIFF_HEADS = 4
DIFF_QK_DIM = 32
DIFF_V_DIM = 64
N_EXPERT_GROUPS = 4
EXPERTS_PER_GROUP = 8
N_EXPERTS = N_EXPERT_GROUPS * EXPERTS_PER_GROUP
NORM_EPS = 1e-6

A_Q_DIM = SWA_HEADS * HEAD_DIM
A_KV_DIM = SWA_KV_HEADS * HEAD_DIM
SSD_D_INNER = SSD_HEADS * SSD_HEAD_DIM
SSD_BC_DIM = SSD_GROUPS * SSD_STATE
SSD_CONV_DIM = SSD_D_INNER + 2 * SSD_BC_DIM
SSD_DT_DIM = 2 * SSD_HEADS
DIFF_QK_WIDTH = DIFF_HEADS * 2 * DIFF_QK_DIM
DIFF_V_WIDTH = DIFF_HEADS * DIFF_V_DIM
N_ALIBI_HEADS = SWA_HEADS + DIFF_HEADS

LANES = 128
SUBLANES = 8
VMEM_LIMIT = 56 * 1024 * 1024
NEG = -1e30
LOG2E = math.log2(math.e)

SSD_CHUNK = 128
SSD_CHUNKS_PER_STEP = 2
SWA_BLOCKS_PER_STEP = 4
MOE_ROW_TILE = 256
DIFF_PAIR = 2 * 2 * DIFF_QK_DIM
DIFF_VROWS = 80
DIFF_NFEAT = 6
DIFF_ITEM_SPLIT = 1


def _cparams(*sem):
    return pltpu.CompilerParams(dimension_semantics=sem, vmem_limit_bytes=VMEM_LIMIT)


def _rms(x, w):
    return x * lax.rsqrt(jnp.mean(x * x, axis=-1, keepdims=True) + NORM_EPS) * w


def _silu(x):
    return x / (1.0 + jnp.exp(-x))


def _softplus(x):
    return jnp.maximum(x, 0.0) + jnp.log(1.0 + jnp.exp(-jnp.abs(x)))


def _bf16_split(x):
    hi = x.astype(BF16).astype(F32)
    lo = (x - hi).astype(BF16).astype(F32)
    return hi, lo


_C_AK = 0
_C_Z = _C_AK + A_KV_DIM
_C_XBC = _C_Z + SSD_D_INNER
_C_CK = _C_XBC + SSD_CONV_DIM
_C_DT = _C_CK + DIFF_HEADS * LANES
_C_END = _C_DT + LANES
_R_AQ = 0
_R_AV = _R_AQ + A_Q_DIM
_R_CQ = _R_AV + A_KV_DIM
_R_CV = _R_CQ + DIFF_QK_WIDTH
_R_END = _R_CV + DIFF_V_WIDTH


def _inproj_kernel(x_ref, xp_ref, xn_ref, nw_ref, w_ref, wt_ref, cw_ref, cb_ref,
                   ak_ref, z_ref, xs_ref, cm_ref, ck_ref, dt_ref, aqt_ref, avt_ref, cqt_ref, cvt_ref, bt_ref,
                   *, tiles_per_seq, diff_key_tile):
    i = pl.program_id(0)
    tm = x_ref.shape[0]
    nw = nw_ref[...]
    h = _rms(x_ref[...], nw).astype(BF16)

    def seg(lo, hi):
        return jnp.dot(h, w_ref[:, lo:hi], preferred_element_type=F32)

    ak_ref[...] = seg(_C_AK, _C_Z).astype(BF16)
    z_ref[...] = seg(_C_Z, _C_XBC)
    pos = (i * tm + lax.broadcasted_iota(I32, (tm, _C_DT - _C_CK), 0)) % diff_key_tile
    ck_ref[...] = (seg(_C_CK, _C_DT) + _diff_key_features(pos)).astype(BF16)
    dt_ref[...] = seg(_C_DT, _C_END)

    w_xbc = w_ref[:, _C_XBC:_C_CK]
    first = i % tiles_per_seq == 0
    last = i % tiles_per_seq == tiles_per_seq - 1
    prev = jnp.dot(_rms(xp_ref[...], nw).astype(BF16), w_xbc, preferred_element_type=F32)
    nxt = jnp.dot(_rms(xn_ref[...], nw).astype(BF16), w_xbc, preferred_element_type=F32)
    prev = jnp.where(first, 0.0, prev)
    nxt = jnp.where(last, 0.0, nxt)
    ext = jnp.concatenate([prev, seg(_C_XBC, _C_CK), nxt], axis=0)
    half = SSD_CONV // 2
    conv = cb_ref[...]
    for k in range(SSD_CONV):
        off = SUBLANES - half + k
        conv = conv + cw_ref[k:k + 1, :] * ext[off:off + tm, :]
    u = _silu(conv)
    xs_ref[...] = u[:, :SSD_D_INNER]
    bt_ref[...] = jnp.transpose(u[:, SSD_D_INNER:SSD_D_INNER + SSD_BC_DIM])
    cm_ref[...] = u[:, SSD_D_INNER + SSD_BC_DIM:]

    tr = lax.dot_general(wt_ref[...], h, (((1,), (1,)), ((), ())), preferred_element_type=F32)
    aqt_ref[...] = (tr[_R_AQ:_R_AV] * (HEAD_DIM ** -0.5)).astype(BF16)
    avt = tr[_R_AV:_R_CQ].astype(BF16)
    for c in range(tm // LANES):
        avt_ref[c] = avt[:, c * LANES:(c + 1) * LANES]
    cqt_ref[...] = (tr[_R_CQ:_R_CV] * (DIFF_QK_DIM ** -0.5 * LOG2E)).astype(BF16)
    pad = DIFF_VROWS - DIFF_V_DIM
    ones_row = (lax.broadcasted_iota(I32, (pad, tm), 0) == 0).astype(BF16)
    for hh in range(DIFF_HEADS):
        cvt_ref[hh * DIFF_VROWS:hh * DIFF_VROWS + DIFF_V_DIM, :] = (
            tr[_R_CV + hh * DIFF_V_DIM:_R_CV + (hh + 1) * DIFF_V_DIM].astype(BF16))
        cvt_ref[hh * DIFF_VROWS + DIFF_V_DIM:(hh + 1) * DIFF_VROWS, :] = ones_row


def _inproj(x2, norm_w, w_main, w_t, conv_w, conv_b, tm, seq, diff_key_tile):
    t, d = x2.shape
    hb = tm // SUBLANES
    n_hblk = t // SUBLANES
    row = lambda i: (i, 0)
    col = lambda i: (0, i)
    full = lambda i: (0, 0)
    row_outs = [(A_KV_DIM, BF16), (SSD_D_INNER, F32), (SSD_D_INNER, F32), (SSD_BC_DIM, F32),
                (DIFF_HEADS * LANES, BF16), (LANES, F32)]
    out_shape = [jax.ShapeDtypeStruct((t, w), dt) for w, dt in row_outs]
    out_specs = [pl.BlockSpec((tm, w), row) for w, _ in row_outs]
    out_shape += [jax.ShapeDtypeStruct((A_Q_DIM, t), BF16),
                  jax.ShapeDtypeStruct((t // LANES, A_KV_DIM, LANES), BF16),
                  jax.ShapeDtypeStruct((DIFF_QK_WIDTH, t), BF16),
                  jax.ShapeDtypeStruct((DIFF_HEADS * DIFF_VROWS, t), BF16),
                  jax.ShapeDtypeStruct((SSD_BC_DIM, t), F32)]
    out_specs += [pl.BlockSpec((A_Q_DIM, tm), col),
                  pl.BlockSpec((tm // LANES, A_KV_DIM, LANES), lambda i: (i, 0, 0)),
                  pl.BlockSpec((DIFF_QK_WIDTH, tm), col),
                  pl.BlockSpec((DIFF_HEADS * DIFF_VROWS, tm), col),
                  pl.BlockSpec((SSD_BC_DIM, tm), col)]
    return pl.pallas_call(
        functools.partial(_inproj_kernel, tiles_per_seq=seq // tm, diff_key_tile=diff_key_tile),
        grid=(t // tm,),
        in_specs=[pl.BlockSpec((tm, d), row),
                  pl.BlockSpec((SUBLANES, d), lambda i: (jnp.maximum(i * hb - 1, 0), 0)),
                  pl.BlockSpec((SUBLANES, d), lambda i: (jnp.minimum((i + 1) * hb, n_hblk - 1), 0)),
                  pl.BlockSpec((1, d), full),
                  pl.BlockSpec(w_main.shape, full), pl.BlockSpec(w_t.shape, full),
                  pl.BlockSpec(conv_w.shape, full), pl.BlockSpec(conv_b.shape, full)],
        out_specs=out_specs,
        out_shape=out_shape,
        compiler_params=_cparams("parallel"),
        name="inproj",
    )(x2, x2, x2, norm_w, w_main, w_t, conv_w, conv_b)


def _swa_kernel(sink_ref, slope_ref, qt_ref, k_ref, vt_ref, o_ref):
    step = pl.program_id(1)
    s_len = k_ref.shape[0]
    blk = SWA_WINDOW
    band = 3 * blk
    nb = s_len // blk
    rep = SWA_HEADS // SWA_KV_HEADS
    hd = HEAD_DIM
    for u in range(SWA_BLOCKS_PER_STEP):
        n = step * SWA_BLOCKS_PER_STEP + u
        start_blk = jnp.clip(n - 1, 0, nb - 3)
        start = pl.multiple_of(start_blk * blk, blk)
        kb = k_ref[pl.ds(start, band), :]
        v3 = vt_ref[pl.ds(start_blk, 3)]
        vtb = jnp.concatenate([v3[0], v3[1], v3[2]], axis=1)
        qt = qt_ref[:, u * blk:(u + 1) * blk]
        zero = jnp.zeros((hd, rep * blk), BF16)
        grp = [jnp.concatenate([qt[(g * rep + r) * hd:(g * rep + r + 1) * hd] for r in range(rep)], axis=1)
               for g in range(SWA_KV_HEADS)]
        qbd = jnp.concatenate([jnp.concatenate([grp[0], zero], axis=1),
                               jnp.concatenate([zero, grp[1]], axis=1)], axis=0)
        st = jnp.dot(kb, qbd, preferred_element_type=F32)
        kpos = start + lax.broadcasted_iota(I32, (band, blk), 0)
        qpos = n * blk + lax.broadcasted_iota(I32, (band, blk), 1)
        dist_i = jnp.abs(qpos - kpos)
        valid = dist_i <= SWA_WINDOW
        dist = dist_i.astype(F32)
        ps, inv = [], []
        for h in range(SWA_HEADS):
            s = jnp.where(valid, st[:, h * blk:(h + 1) * blk] - slope_ref[h] * dist, NEG)
            sink = sink_ref[h]
            m = jnp.maximum(jnp.max(s, axis=0, keepdims=True), sink)
            p = jnp.exp(s - m)
            inv.append(1.0 / (jnp.sum(p, axis=0, keepdims=True) + jnp.exp(sink - m)))
            ps.append(p.astype(BF16))
        outs = []
        for g in range(SWA_KV_HEADS):
            pg = jnp.concatenate(ps[g * rep:(g + 1) * rep], axis=1)
            og = jnp.dot(vtb[g * hd:(g + 1) * hd, :], pg, preferred_element_type=F32)
            for r in range(rep):
                outs.append(og[:, r * blk:(r + 1) * blk] * inv[g * rep + r])
        o_ref[u * blk:(u + 1) * blk, :] = jnp.transpose(jnp.concatenate(outs, axis=0)).astype(o_ref.dtype)


def _swa(aqt, ak, avt3, sink, slopes, batch, seq):
    blk = SWA_WINDOW
    rows = blk * SWA_BLOCKS_PER_STEP
    steps = seq // rows
    nb = seq // blk
    t = batch * seq
    smem = pl.BlockSpec(memory_space=pltpu.SMEM)
    return pl.pallas_call(
        _swa_kernel,
        grid=(batch, steps),
        in_specs=[smem, smem,
                  pl.BlockSpec((A_Q_DIM, rows), lambda b, s: (0, b * steps + s)),
                  pl.BlockSpec((seq, A_KV_DIM), lambda b, s: (b, 0)),
                  pl.BlockSpec((nb, A_KV_DIM, blk), lambda b, s: (b, 0, 0))],
        out_specs=pl.BlockSpec((rows, A_Q_DIM), lambda b, s: (b * steps + s, 0)),
        out_shape=jax.ShapeDtypeStruct((t, A_Q_DIM), BF16),
        compiler_params=_cparams("parallel", "parallel"),
        name="swa",
    )(sink, slopes, aqt, ak, avt3)


def _diff_key_tile(i, j, tq, tk, nk):
    return ((i * tq) // tk + j) % nk


def _diff_key_features(pos_in_tile):
    lane = lax.broadcasted_iota(I32, pos_in_tile.shape, 1) % LANES - 2 * DIFF_QK_DIM
    coarse = ((pos_in_tile // 16) * 16).astype(F32)
    fine = (pos_in_tile % 16).astype(F32)
    f = lane % DIFF_NFEAT
    feat = jnp.where(f < 2, coarse, jnp.where(f < 4, fine, 1.0))
    return jnp.where((lane >= 0) & (lane < 2 * DIFF_NFEAT), feat, 0.0)


def _diff_kernel(slope_ref, qt_ref, k_ref, vt_ref, lam_ref, sw_ref, o_ref, qtb_ref, m_ref, acc_ref,
                 s_ref, p_ref, *, lambda_init, n_split):
    i = pl.program_id(1)
    j = pl.program_id(2)
    nk = pl.num_programs(2)
    tq = qt_ref.shape[1]
    tk = k_ref.shape[0]
    dq = DIFF_QK_DIM
    hw = 2 * dq
    nf = DIFF_NFEAT

    @pl.when(j == 0)
    def _():
        m_ref[...] = jnp.full(m_ref.shape, NEG, F32)
        acc_ref[...] = jnp.zeros(acc_ref.shape, F32)
        ii = lax.broadcasted_iota(I32, (1, 2 * tq), 1)
        ii = jnp.where(ii >= tq, ii - tq, ii).astype(F32)
        qt = qt_ref[...]
        col = lax.broadcasted_iota(I32, (hw, 2 * tq), 1)
        row = lax.broadcasted_iota(I32, (hw, 2 * tq), 0)
        own_map = row // dq == col // tq
        for h in range(DIFF_HEADS):
            qh = qt[h * hw:(h + 1) * hw, :]
            qh2 = jnp.where(own_map, jnp.concatenate([qh, qh], axis=1), jnp.zeros((hw, 2 * tq), BF16))
            sl = jnp.full((1, 2 * tq), slope_ref[h], F32)
            s_hi, s_lo = _bf16_split(sl)
            v_hi, v_lo = _bf16_split(-sl * ii)
            rows = jnp.concatenate([s_hi, s_lo, s_hi, s_lo, v_hi, v_lo], axis=0)
            zrow = jnp.zeros((nf, 2 * tq), F32)
            zero = jnp.zeros((hw - 2 * nf, 2 * tq), F32)
            variants = ([rows, zrow], [zrow, -rows], [zrow, zrow])
            for v, pieces in enumerate(variants):
                qtb_ref[v, h, 0:hw, :] = qh2
                qtb_ref[v, h, hw:2 * hw, :] = jnp.concatenate(pieces + [zero], axis=0).astype(BF16)

    q0 = i * tq
    k0 = _diff_key_tile(i, j, tq, tk, nk) * tk

    width = 2 * tq // n_split
    items = [(h, mp) for h in range(DIFF_HEADS) for mp in range(n_split)]

    def lanes(mp):
        return slice(mp * width, (mp + 1) * width)

    def scores(h, mp, variant):
        return jnp.dot(k_ref[:, h * LANES:(h + 1) * LANES], qtb_ref[variant, h, :, lanes(mp)],
                       preferred_element_type=F32)

    def stage_scores(h, mp, s):
        s_ref[h, :, lanes(mp)] = s
        return jnp.max(s, axis=0, keepdims=True)

    def softmax_step(h, mp, tile_max, shift):
        m_old = m_ref[h:h + 1, lanes(mp)]
        m_new = jnp.maximum(m_old, tile_max + shift)
        p_ref[h, :, lanes(mp)] = jnp.exp2(s_ref[h, :, lanes(mp)] - (m_new - shift)).astype(BF16)
        m_ref[h:h + 1, lanes(mp)] = m_new
        return jnp.exp2(m_old - m_new)

    def accumulate(h, mp, alpha):
        pv = jnp.dot(vt_ref[h * DIFF_VROWS:(h + 1) * DIFF_VROWS, :], p_ref[h, :, lanes(mp)],
                     preferred_element_type=F32)
        acc_ref[h, :, lanes(mp)] = alpha * acc_ref[h, :, lanes(mp)] + pv

    def tile_pass(score_fn, shift_fn):
        maxima, alphas = {}, {}
        for w in range(len(items) + 2):
            if w < len(items):
                h, mp = items[w]
                maxima[w] = stage_scores(h, mp, score_fn(h, mp))
            if 0 <= w - 1 < len(items):
                h, mp = items[w - 1]
                alphas[w - 1] = softmax_step(h, mp, maxima[w - 1], shift_fn(h))
            if 0 <= w - 2 < len(items):
                h, mp = items[w - 2]
                accumulate(h, mp, alphas[w - 2])

    @pl.when(j == 0)
    def _():
        kpos = k0 + lax.broadcasted_iota(I32, (tk, width), 0)
        qpos = q0 + lax.broadcasted_iota(I32, (tk, width), 1) % tq
        dist = jnp.abs(qpos - kpos).astype(F32)
        tile_pass(lambda h, mp: scores(h, mp, 2) - slope_ref[h] * dist, lambda h: 0.0)

    @pl.when(j > 0)
    def _():
        below = k0 < q0
        variant = jnp.where(below, 0, 1)
        sign = jnp.where(below, 1.0, -1.0)
        tile_pass(lambda h, mp: scores(h, mp, variant),
                  lambda h: sign * slope_ref[h] * (k0 - q0).astype(F32))

    @pl.when(j == nk - 1)
    def _():
        lp = lam_ref[...]
        lam = (jnp.exp(jnp.sum(lp[0:1] * lp[1:2], axis=-1, keepdims=True))
               - jnp.exp(jnp.sum(lp[2:3] * lp[3:4], axis=-1, keepdims=True)) + lambda_init)
        outs = []
        for h in range(DIFF_HEADS):
            a = acc_ref[h]
            o = a[0:DIFF_V_DIM] / a[DIFF_V_DIM:DIFF_V_DIM + 1]
            o = o[:, 0:tq] - lam * o[:, tq:2 * tq]
            ms = jnp.mean(o * o, axis=0, keepdims=True)
            outs.append(o * lax.rsqrt(ms + NORM_EPS) * sw_ref[...] * (1.0 - lambda_init))
        o_ref[...] = jnp.transpose(jnp.concatenate(outs, axis=0)).astype(o_ref.dtype)


def _diff(cqt, ck, cvt, slopes, lam_params, subln_w_col, lambda_init, batch, seq, tq, tk):
    assert tk % tq == 0 and seq % tk == 0
    nq, nk = seq // tq, seq // tk
    t = batch * seq
    smem = pl.BlockSpec(memory_space=pltpu.SMEM)
    key_tile = functools.partial(_diff_key_tile, tq=tq, tk=tk, nk=nk)
    return pl.pallas_call(
        functools.partial(_diff_kernel, lambda_init=lambda_init, n_split=DIFF_ITEM_SPLIT),
        grid=(batch, nq, nk),
        in_specs=[smem,
                  pl.BlockSpec((DIFF_QK_WIDTH, tq), lambda b, i, j: (0, b * nq + i)),
                  pl.BlockSpec((tk, DIFF_HEADS * LANES), lambda b, i, j: (b * nk + key_tile(i, j), 0)),
                  pl.BlockSpec((DIFF_HEADS * DIFF_VROWS, tk), lambda b, i, j: (0, b * nk + key_tile(i, j))),
                  pl.BlockSpec(lam_params.shape, lambda b, i, j: (0, 0)),
                  pl.BlockSpec(subln_w_col.shape, lambda b, i, j: (0, 0))],
        out_specs=pl.BlockSpec((tq, DIFF_V_WIDTH), lambda b, i, j: (b * nq + i, 0)),
        out_shape=jax.ShapeDtypeStruct((t, DIFF_V_WIDTH), BF16),
        scratch_shapes=[pltpu.VMEM((3, DIFF_HEADS, LANES, 2 * tq), BF16),
                        pltpu.VMEM((DIFF_HEADS, 2 * tq), F32),
                        pltpu.VMEM((DIFF_HEADS, DIFF_VROWS, 2 * tq), F32),
                        pltpu.VMEM((DIFF_HEADS, tk, 2 * tq), F32),
                        pltpu.VMEM((DIFF_HEADS, tk, 2 * tq), BF16)],
        compiler_params=_cparams("parallel", "parallel", "arbitrary"),
        name="diffattn",
    )(slopes, cqt, ck, cvt, lam_params, subln_w_col)


def _ssd_direction(fwd, off, xs_ref, bt_ref, cm_ref, dt_ref, dtb_ref, alog_ref, dsk_ref, y_ref, state_ref):
    q = SSD_CHUNK
    rows = slice(off, off + q)
    lane0 = 0 if fwd else SSD_HEADS
    dt_all = _softplus(dt_ref[rows, :] + dtb_ref[...])
    dta_all = dt_all * -jnp.exp(alog_ref[...])
    row = lax.broadcasted_iota(I32, (q, q), 0)
    col = lax.broadcasted_iota(I32, (q, q), 1)
    keep = (row >= col) if fwd else (row <= col)
    tri = keep.astype(BF16)
    part_hi = dta_all.astype(BF16)
    rest = dta_all - part_hi.astype(F32)
    part_mid = rest.astype(BF16)
    part_lo = (rest - part_mid.astype(F32)).astype(BF16)
    da_all = (jnp.dot(tri, part_hi, preferred_element_type=F32)
              + jnp.dot(tri, part_mid, preferred_element_type=F32)
              + jnp.dot(tri, part_lo, preferred_element_type=F32))
    da_all_t = jnp.transpose(da_all)
    dt_all_t = jnp.transpose(dt_all)
    tot_all = jnp.sum(dta_all, axis=0, keepdims=True)

    xs = xs_ref[rows, :]
    bmt = bt_ref[:, rows]
    cm = cm_ref[rows, :]
    rep = SSD_HEADS // SSD_GROUPS
    ns = SSD_STATE
    g_mats = [jnp.dot(cm[:, g * ns:(g + 1) * ns].astype(BF16), bmt[g * ns:(g + 1) * ns, :].astype(BF16),
                      preferred_element_type=F32) for g in range(SSD_GROUPS)]
    ys = []
    for h in range(SSD_HEADS):
        g = h // rep
        ln = lane0 + h
        dac = da_all[:, ln:ln + 1]
        dar = da_all_t[ln:ln + 1, :]
        dtr = dt_all_t[ln:ln + 1, :]
        tot = tot_all[:, ln:ln + 1]
        dac_b = jnp.broadcast_to(dac, (q, q))
        decay = jnp.exp(jnp.where(keep, dac_b - dar, NEG))
        xh = xs[:, h * SSD_HEAD_DIM:(h + 1) * SSD_HEAD_DIM]
        xhb = xh.astype(BF16)
        y = jnp.dot((g_mats[g] * decay * dtr).astype(BF16), xhb, preferred_element_type=F32)
        st = state_ref[h]
        c_in = (cm[:, g * ns:(g + 1) * ns] * jnp.exp(dac_b[:, 0:ns])).astype(BF16)
        y = y + jnp.dot(c_in, st.astype(BF16), preferred_element_type=F32)
        to_end = jnp.exp(tot - dar) * dtr
        b_out = (bmt[g * ns:(g + 1) * ns, :] * to_end).astype(BF16)
        state_ref[h] = st * jnp.exp(tot) + jnp.dot(b_out, xhb, preferred_element_type=F32)
        if fwd:
            y = y + dsk_ref[:, h:h + 1] * xh
        ys.append(y)
    y_ref[rows, :] = jnp.concatenate(ys, axis=-1)


def _ssd_kernel(xsf_ref, btf_ref, cmf_ref, dtf_ref, xsb_ref, btb_ref, cmb_ref, dtb_in_ref,
                dtbias_ref, alog_ref, dsk_ref, yf_ref, yb_ref, state_ref):
    @pl.when(pl.program_id(1) == 0)
    def _():
        state_ref[...] = jnp.zeros(state_ref.shape, F32)

    n = SSD_CHUNKS_PER_STEP
    for u in range(n):
        _ssd_direction(True, u * SSD_CHUNK, xsf_ref, btf_ref, cmf_ref, dtf_ref, dtbias_ref, alog_ref, dsk_ref,
                       yf_ref, state_ref.at[0])
        _ssd_direction(False, (n - 1 - u) * SSD_CHUNK, xsb_ref, btb_ref, cmb_ref, dtb_in_ref, dtbias_ref,
                       alog_ref, dsk_ref, yb_ref, state_ref.at[1])


def _ssd(xs, bt, cm, dt_raw, dt_bias, a_log, d_skip, batch, seq):
    q = SSD_CHUNK * SSD_CHUNKS_PER_STEP
    nc = seq // q
    t = batch * seq
    full = lambda b, c: (0, 0)
    fw = lambda b, c: b * nc + c
    bw = lambda b, c: b * nc + nc - 1 - c

    def specs(idx):
        return [pl.BlockSpec((q, SSD_D_INNER), lambda b, c: (idx(b, c), 0)),
                pl.BlockSpec((SSD_BC_DIM, q), lambda b, c: (0, idx(b, c))),
                pl.BlockSpec((q, SSD_BC_DIM), lambda b, c: (idx(b, c), 0)),
                pl.BlockSpec((q, LANES), lambda b, c: (idx(b, c), 0))]

    return pl.pallas_call(
        _ssd_kernel,
        grid=(batch, nc),
        in_specs=specs(fw) + specs(bw) + [pl.BlockSpec(dt_bias.shape, full), pl.BlockSpec(a_log.shape, full),
                                          pl.BlockSpec(d_skip.shape, full)],
        out_specs=[pl.BlockSpec((q, SSD_D_INNER), lambda b, c: (fw(b, c), 0)),
                   pl.BlockSpec((q, SSD_D_INNER), lambda b, c: (bw(b, c), 0))],
        out_shape=[jax.ShapeDtypeStruct((t, SSD_D_INNER), F32), jax.ShapeDtypeStruct((t, SSD_D_INNER), F32)],
        scratch_shapes=[pltpu.VMEM((2, SSD_HEADS, SSD_STATE, SSD_HEAD_DIM), F32)],
        compiler_params=_cparams("parallel", "arbitrary"),
        name="ssd",
    )(xs, bt, cm, dt_raw, xs, bt, cm, dt_raw, dt_bias, a_log, d_skip)


def _outproj_kernel(x_ref, ya_ref, yf_ref, yb_ref, z_ref, snw_ref, yc_ref, wo_ref, fnw_ref, wrt_ref, br_ref,
                    xn_ref, ri_ref, rf_ref, cnt_ref, tri_ref, carry_ref):
    step = pl.program_id(0)
    tm = x_ref.shape[0]

    @pl.when(step == 0)
    def _():
        carry_ref[...] = jnp.zeros(carry_ref.shape, F32)
        r = lax.broadcasted_iota(I32, (tm, tm), 0)
        cc = lax.broadcasted_iota(I32, (tm, tm), 1)
        tri_ref[...] = (r <= cc).astype(BF16)

    y = (yf_ref[...] + yb_ref[...]) * _silu(z_ref[...])
    yb = _rms(y, snw_ref[...]).astype(BF16)
    acc = jnp.dot(ya_ref[...], wo_ref[0:A_Q_DIM, :], preferred_element_type=F32)
    acc = acc + jnp.dot(yb, wo_ref[A_Q_DIM:A_Q_DIM + SSD_D_INNER, :], preferred_element_type=F32)
    acc = acc + jnp.dot(yc_ref[...], wo_ref[A_Q_DIM + SSD_D_INNER:, :], preferred_element_type=F32)
    xn = x_ref[...] + acc
    xn_ref[...] = xn

    h = _rms(xn, fnw_ref[...])
    h_hi = h.astype(BF16)
    h_lo = (h - h_hi.astype(F32)).astype(BF16)
    nt = (((1,), (1,)), ((), ()))
    logits = (lax.dot_general(wrt_ref[0], h_hi, nt, preferred_element_type=F32)
              + lax.dot_general(wrt_ref[0], h_lo, nt, preferred_element_type=F32)
              + lax.dot_general(wrt_ref[1], h_hi, nt, preferred_element_type=F32)) + br_ref[...]
    ne, epg, ng = N_EXPERTS, EXPERTS_PER_GROUP, N_EXPERT_GROUPS
    gl = logits[ne:ne + ng, :]
    gmax = jnp.max(gl, axis=0, keepdims=True)
    g_sel = jnp.full((1, tm), float(ng - 1), F32)
    for g in range(ng - 2, -1, -1):
        g_sel = jnp.where(gl[g:g + 1, :] == gmax, float(g), g_sel)
    g_gate = 1.0 / jnp.sum(jnp.exp(gl - gmax), axis=0, keepdims=True)
    e_in = logits[0:epg, :]
    for g in range(1, ng):
        e_in = jnp.where(g_sel == float(g), logits[g * epg:(g + 1) * epg, :], e_in)
    sub = lax.broadcasted_iota(I32, (epg, tm), 0).astype(F32)
    m1 = jnp.max(e_in, axis=0, keepdims=True)
    i1 = jnp.min(jnp.where(e_in == m1, sub, float(epg)), axis=0, keepdims=True)
    rest = jnp.where(sub == i1, NEG, e_in)
    m2 = jnp.max(rest, axis=0, keepdims=True)
    i2 = jnp.min(jnp.where(rest == m2, sub, float(epg)), axis=0, keepdims=True)
    r = jnp.exp(m2 - m1)
    c1 = g_gate / (1.0 + r)
    c2 = g_gate * r / (1.0 + r)
    e1 = (g_sel * epg + i1).astype(I32)
    e2 = (g_sel * epg + i2).astype(I32)

    erow = lax.broadcasted_iota(I32, (ne, tm), 0)
    hit1 = erow == e1
    hit2 = erow == e2
    oh = jnp.where(hit1 | hit2, 1.0, 0.0)
    incl = jnp.dot(oh.astype(BF16), tri_ref[...], preferred_element_type=F32)
    before = incl - oh + carry_ref[:, 0:1]
    rank1 = jnp.sum(jnp.where(hit1, before, 0.0), axis=0, keepdims=True)
    rank2 = jnp.sum(jnp.where(hit2, before, 0.0), axis=0, keepdims=True)
    carry_ref[...] = carry_ref[...] + jnp.sum(oh, axis=1, keepdims=True)
    cnt_ref[...] = carry_ref[...]
    zi = jnp.zeros((1, tm), I32)
    ri_ref[...] = jnp.concatenate([e1, e2, rank1.astype(I32), rank2.astype(I32), zi, zi, zi, zi], axis=0)
    zf = jnp.zeros((1, tm), F32)
    rf_ref[...] = jnp.concatenate([c1, c2, zf, zf, zf, zf, zf, zf], axis=0)


def _outproj(x2, ya, yf, yb, z, ssd_norm_w, yc, w_out, ffn_norm_w, wrt, br, tm):
    t, d = x2.shape
    row = lambda i: (i, 0)
    full = lambda i: (0, 0)
    return pl.pallas_call(
        _outproj_kernel,
        grid=(t // tm,),
        in_specs=[pl.BlockSpec((tm, d), row),
                  pl.BlockSpec((tm, A_Q_DIM), row),
                  pl.BlockSpec((tm, SSD_D_INNER), row),
                  pl.BlockSpec((tm, SSD_D_INNER), row),
                  pl.BlockSpec((tm, SSD_D_INNER), row),
                  pl.BlockSpec(ssd_norm_w.shape, full),
                  pl.BlockSpec((tm, DIFF_V_WIDTH), row),
                  pl.BlockSpec(w_out.shape, full),
                  pl.BlockSpec(ffn_norm_w.shape, full),
                  pl.BlockSpec(wrt.shape, lambda i: (0, 0, 0)),
                  pl.BlockSpec(br.shape, full)],
        out_specs=[pl.BlockSpec((tm, d), row),
                   pl.BlockSpec((SUBLANES, tm), lambda i: (0, i)),
                   pl.BlockSpec((SUBLANES, tm), lambda i: (0, i)),
                   pl.BlockSpec((N_EXPERTS, LANES), full)],
        out_shape=[jax.ShapeDtypeStruct((t, d), F32),
                   jax.ShapeDtypeStruct((SUBLANES, t), I32),
                   jax.ShapeDtypeStruct((SUBLANES, t), F32),
                   jax.ShapeDtypeStruct((N_EXPERTS, LANES), F32)],
        scratch_shapes=[pltpu.VMEM((tm, tm), BF16), pltpu.VMEM((N_EXPERTS, LANES), F32)],
        compiler_params=_cparams("arbitrary"),
        name="outproj_router",
    )(x2, ya, yf, yb, z, ssd_norm_w, yc, w_out, ffn_norm_w, wrt, br)


_PAD_PIECES = tuple(1 << b for b in reversed(range(MOE_ROW_TILE.bit_length() - 1)))


def _dispatch_kernel(slot1_ref, slot2_ref, pstart_ref, plen_ref, nused_ref, x_ref, xs_hbm, zero_ref, sem):
    i = pl.program_id(0)
    tm = x_ref.shape[0]

    @pl.when(i == 0)
    def _():
        zero_ref[...] = jnp.zeros(zero_ref.shape, F32)

        def pieces(e, wait):
            n = plen_ref[e]
            first = pstart_ref[e]
            off = first + n
            for b in _PAD_PIECES:
                off = off - (n & b)
                dst = pl.ds(pl.multiple_of(off, b), b) if b >= SUBLANES else None
                if dst is not None:
                    @pl.when((n & b) != 0)
                    def _():
                        cp = pltpu.make_async_copy(zero_ref.at[pl.ds(0, b)], xs_hbm.at[dst], sem.at[1])
                        cp.wait() if wait else cp.start()

            for u in range(SUBLANES - 1):
                @pl.when(u < (n & (SUBLANES - 1)))
                def _():
                    cp = pltpu.make_async_copy(zero_ref.at[pl.ds(0, 1)], xs_hbm.at[pl.ds(first + u, 1)],
                                               sem.at[1])
                    cp.wait() if wait else cp.start()

        def tail(tile, wait):
            big = _PAD_PIECES[0]
            for part in range(MOE_ROW_TILE // big):
                dst = xs_hbm.at[pl.ds(pl.multiple_of(tile * MOE_ROW_TILE + part * big, big), big)]
                cp = pltpu.make_async_copy(zero_ref, dst, sem.at[1])
                cp.wait() if wait else cp.start()

        def loop(fn, lo, hi, wait):
            def body(k, carry):
                fn(k, wait)
                return carry

            lax.fori_loop(lo, hi, body, 0)

        n_tiles = xs_hbm.shape[0] // MOE_ROW_TILE
        for wait in (False, True):
            loop(pieces, 0, N_EXPERTS, wait)
            loop(tail, nused_ref[0], n_tiles, wait)

    base = i * tm
    for r in range(tm):
        src = x_ref.at[pl.ds(r, 1)]
        pltpu.make_async_copy(src, xs_hbm.at[pl.ds(slot1_ref[base + r], 1)], sem.at[0]).start(priority=0)
        pltpu.make_async_copy(src, xs_hbm.at[pl.ds(slot2_ref[base + r], 1)], sem.at[0]).start(priority=1)
    for _ in range(2):
        pltpu.make_async_copy(x_ref, xs_hbm.at[pl.ds(0, tm)], sem.at[0]).wait()


def _dispatch(xn, slot1, slot2, pad_start, pad_len, n_used, n_rows, tm):
    t, d = xn.shape
    grid_spec = pltpu.PrefetchScalarGridSpec(
        num_scalar_prefetch=5,
        grid=(t // tm,),
        in_specs=[pl.BlockSpec((tm, d), lambda i, s1, s2, ps, pn, nu: (i, 0))],
        out_specs=pl.BlockSpec(memory_space=pl.ANY),
        scratch_shapes=[pltpu.VMEM((_PAD_PIECES[0], d), F32), pltpu.SemaphoreType.DMA((2,))],
    )
    return pl.pallas_call(
        _dispatch_kernel,
        grid_spec=grid_spec,
        out_shape=jax.ShapeDtypeStruct((n_rows, d), F32),
        compiler_params=_cparams("arbitrary"),
        name="moe_dispatch",
    )(slot1, slot2, pad_start, pad_len, n_used, xn)


def _moe_kernel(texp_ref, nused_ref, x_ref, fnw_ref, wg_ref, wu_ref, wd_ref, y_ref):
    i = pl.program_id(0)

    @pl.when(i < nused_ref[0])
    def _():
        h = _rms(x_ref[...], fnw_ref[...]).astype(BF16)
        hg = jnp.dot(h, wg_ref[...].astype(BF16), preferred_element_type=F32)
        hu = jnp.dot(h, wu_ref[...].astype(BF16), preferred_element_type=F32)
        act = (_silu(hg) * hu).astype(BF16)
        y_ref[...] = jnp.dot(act, wd_ref[...].astype(BF16), preferred_element_type=F32)

    @pl.when(i >= nused_ref[0])
    def _():
        y_ref[...] = jnp.zeros(y_ref.shape, F32)


def _moe(xs, ffn_norm_w, w_gate, w_up, w_down, layer, tile_expert, n_used):
    n_rows, d = xs.shape
    f = w_gate.shape[-1]
    tr = MOE_ROW_TILE

    def used(i, nu):
        return jnp.maximum(jnp.minimum(i, nu[0] - 1), 0)

    grid_spec = pltpu.PrefetchScalarGridSpec(
        num_scalar_prefetch=2,
        grid=(n_rows // tr,),
        in_specs=[pl.BlockSpec((tr, d), lambda i, te, nu: (used(i, nu), 0)),
                  pl.BlockSpec(ffn_norm_w.shape, lambda i, te, nu: (0, 0)),
                  pl.BlockSpec((None, None, d, f), lambda i, te, nu: (layer, te[used(i, nu)], 0, 0)),
                  pl.BlockSpec((None, None, d, f), lambda i, te, nu: (layer, te[used(i, nu)], 0, 0)),
                  pl.BlockSpec((None, None, f, d), lambda i, te, nu: (layer, te[used(i, nu)], 0, 0))],
        out_specs=pl.BlockSpec((tr, d), lambda i, te, nu: (i, 0)),
    )
    return pl.pallas_call(
        _moe_kernel,
        grid_spec=grid_spec,
        out_shape=jax.ShapeDtypeStruct((n_rows, d), F32),
        compiler_params=_cparams("arbitrary"),
        name="moe_experts",
    )(tile_expert, n_used, xs, ffn_norm_w, w_gate, w_up, w_down)


def _combine_kernel(slot1_ref, slot2_ref, x_ref, cw_ref, nw_ref, y_hbm, o_ref, ybuf, sem, *, final_norm):
    i = pl.program_id(0)
    n = pl.num_programs(0)
    tm = x_ref.shape[0]

    def start_gather(tile, slot):
        base = tile * tm
        for r in range(tm):
            pltpu.make_async_copy(y_hbm.at[pl.ds(slot1_ref[base + r], 1)], ybuf.at[slot, 0, pl.ds(r, 1)],
                                  sem.at[slot]).start(priority=0)
            pltpu.make_async_copy(y_hbm.at[pl.ds(slot2_ref[base + r], 1)], ybuf.at[slot, 1, pl.ds(r, 1)],
                                  sem.at[slot]).start(priority=1)

    def compute(slot):
        for k in range(2):
            pltpu.make_async_copy(y_hbm.at[pl.ds(0, tm)], ybuf.at[slot, k], sem.at[slot]).wait()
        cw = cw_ref[...]
        out = x_ref[...] + cw[:, 0:1] * ybuf[slot, 0] + cw[:, 1:2] * ybuf[slot, 1]
        if final_norm:
            out = _rms(out, nw_ref[...])
        o_ref[...] = out

    @pl.when(i == 0)
    def _():
        start_gather(0, 0)

    for parity in range(2):
        @pl.when(i % 2 == parity)
        def _():
            @pl.when(i + 1 < n)
            def _():
                start_gather(i + 1, 1 - parity)

            compute(parity)


def _combine(xn, cw, norm_w, y_sorted, slot1, slot2, tm, final_norm):
    t, d = xn.shape
    grid_spec = pltpu.PrefetchScalarGridSpec(
        num_scalar_prefetch=2,
        grid=(t // tm,),
        in_specs=[pl.BlockSpec((tm, d), lambda i, s1, s2: (i, 0)),
                  pl.BlockSpec((tm, cw.shape[1]), lambda i, s1, s2: (i, 0)),
                  pl.BlockSpec(norm_w.shape, lambda i, s1, s2: (0, 0)),
                  pl.BlockSpec(memory_space=pl.ANY)],
        out_specs=pl.BlockSpec((tm, d), lambda i, s1, s2: (i, 0)),
        scratch_shapes=[pltpu.VMEM((2, 2, tm, d), F32), pltpu.SemaphoreType.DMA((2,))],
    )
    return pl.pallas_call(
        functools.partial(_combine_kernel, final_norm=final_norm),
        grid_spec=grid_spec,
        out_shape=jax.ShapeDtypeStruct((t, d), F32),
        compiler_params=_cparams("arbitrary"),
        name="moe_combine",
    )(slot1, slot2, xn, cw, norm_w, y_sorted)


def _pad_lanes(v):
    v = v.reshape(1, -1).astype(F32)
    return jnp.pad(v, ((0, 0), (0, LANES - v.shape[1])))


def kernel(x, attn_norm_w, w_in, swa_sink, ssd_conv_w, ssd_conv_b, ssd_dt_bias, ssd_a_log, ssd_d, ssd_norm_w,
           diff_lambda, diff_subln_w, w_out, ffn_norm_w, w_router_group, b_router_group, w_router_expert,
           b_router_expert, w_gate, w_up, w_down, final_norm_w):
    return _forward(x, attn_norm_w, w_in, swa_sink, ssd_conv_w, ssd_conv_b, ssd_dt_bias, ssd_a_log, ssd_d,
                    ssd_norm_w, diff_lambda, diff_subln_w, w_out, ffn_norm_w, w_router_group, b_router_group,
                    w_router_expert, b_router_expert, w_gate, w_up, w_down, final_norm_w)


def _forward(x, attn_norm_w, w_in, swa_sink, ssd_conv_w, ssd_conv_b, ssd_dt_bias, ssd_a_log, ssd_d, ssd_norm_w,
             diff_lambda, diff_subln_w, w_out, ffn_norm_w, w_router_group, b_router_group, w_router_expert,
             b_router_expert, w_gate, w_up, w_down, final_norm_w, tm=512, tq=512, tk=1024, tmc=256):
    batch, seq, d = x.shape
    depth = w_in.shape[0]
    t = batch * seq
    tr = MOE_ROW_TILE
    n_tiles = (2 * t) // tr + N_EXPERTS
    slopes = jnp.exp2(-8.0 * jnp.arange(1, N_ALIBI_HEADS + 1, dtype=F32) / N_ALIBI_HEADS)
    swa_slopes, diff_slopes = slopes[:SWA_HEADS], slopes[SWA_HEADS:]

    sizes = [A_Q_DIM, A_KV_DIM, A_KV_DIM, SSD_D_INNER, SSD_CONV_DIM, SSD_DT_DIM, DIFF_QK_WIDTH, DIFF_QK_WIDTH,
             DIFF_V_WIDTH]
    offs = [0]
    for s in sizes:
        offs.append(offs[-1] + s)
    o_aq, o_ak, o_av, o_z, o_xbc, o_dt, o_cq, o_ck, o_cv, o_end = offs

    x2 = x.reshape(t, d)
    for l in range(depth):
        w = w_in[l]
        hw = 2 * DIFF_QK_DIM
        w_ck = jnp.pad(w[:, o_ck:o_cv].reshape(d, DIFF_HEADS, hw), ((0, 0), (0, 0), (0, LANES - hw)))
        w_main = jnp.concatenate(
            [w[:, o_ak:o_av], w[:, o_z:o_dt], w_ck.reshape(d, DIFF_HEADS * LANES), w[:, o_dt:o_cq],
             jnp.zeros((d, LANES - SSD_DT_DIM), w.dtype)], axis=1).astype(BF16)
        w_t = jnp.concatenate([w[:, o_aq:o_ak], w[:, o_av:o_z], w[:, o_cq:o_ck], w[:, o_cv:o_end]],
                              axis=1).T.astype(BF16)
        ak, z, xs, cm, ck, dt_raw, aqt, avt3, cqt, cvt, bt = _inproj(
            x2, attn_norm_w[l].reshape(1, d), w_main, w_t, ssd_conv_w[l].astype(F32),
            ssd_conv_b[l].reshape(1, -1).astype(F32), tm, seq, tk)

        ya = _swa(aqt, ak, avt3, swa_sink[l].astype(F32), swa_slopes, batch, seq)
        lambda_init = 0.8 - 0.6 * math.exp(-0.3 * l)
        yc = _diff(cqt, ck, cvt, diff_slopes * LOG2E, diff_lambda[l].astype(F32),
                   diff_subln_w[l].reshape(DIFF_V_DIM, 1).astype(F32), lambda_init, batch, seq, tq, tk)
        yf, yb = _ssd(xs, bt, cm, dt_raw, _pad_lanes(ssd_dt_bias[l]), _pad_lanes(ssd_a_log[l]),
                      _pad_lanes(ssd_d[l]), batch, seq)

        wr32 = jnp.concatenate([w_router_expert[l], w_router_group[l],
                                jnp.zeros((d, SUBLANES - N_EXPERT_GROUPS), F32)], axis=1).T.astype(F32)
        wr_hi = wr32.astype(BF16)
        wrt = jnp.stack([wr_hi, (wr32 - wr_hi.astype(F32)).astype(BF16)])
        br = jnp.concatenate([b_router_expert[l], b_router_group[l],
                              jnp.zeros((SUBLANES - N_EXPERT_GROUPS,), F32)]).reshape(-1, 1).astype(F32)
        xn, ri, rf, cnt = _outproj(x2, ya, yf, yb, z, ssd_norm_w[l].reshape(1, -1), yc, w_out[l].astype(BF16),
                                   ffn_norm_w[l].reshape(1, d), wrt, br, tm)

        counts = cnt[:, 0].astype(I32)
        padded = ((counts + tr - 1) // tr) * tr
        ends = jnp.cumsum(padded)
        starts = ends - padded
        experts = jnp.arange(N_EXPERTS, dtype=I32)[:, None]

        def slot_of(e, rank):
            return jnp.sum(jnp.where(e[None, :] == experts, starts[:, None], 0), axis=0) + rank

        slot1 = slot_of(ri[0], ri[2])
        slot2 = slot_of(ri[1], ri[3])
        tile_start = jnp.arange(n_tiles, dtype=I32) * tr
        tile_expert = jnp.minimum(jnp.sum(ends[None, :] <= tile_start[:, None], axis=1), N_EXPERTS - 1).astype(I32)
        n_used = (ends[-1] // tr).astype(I32).reshape(1)

        xs_sorted = _dispatch(xn, slot1, slot2, starts + counts, padded - counts, n_used, n_tiles * tr, tmc)
        y_sorted = _moe(xs_sorted, ffn_norm_w[l].reshape(1, d), w_gate, w_up, w_down, l, tile_expert, n_used)
        last = l == depth - 1
        x2 = _combine(xn, rf.T, final_norm_w.reshape(1, d), y_sorted, slot1, slot2, tmc, last)
    return x2.reshape(batch, seq, d)
```

```python
import functools
import math

import jax
import jax.numpy as jnp
from jax import lax
from jax.experimental import pallas as pl
from jax.experimental.pallas import tpu as pltpu

F32 = jnp.float32
BF16 = jnp.bfloat16
I32 = jnp.int32

HEAD_DIM = 64
SWA_HEADS = 6
SWA_KV_HEADS = 2
SWA_WINDOW = 128
SSD_HEADS = 6
SSD_HEAD_DIM = 64
SSD_GROUPS = 2
SSD_STATE = 64
SSD_CONV = 5
DIFF_HEADS = 4
DIFF_QK_DIM = 32
DIFF_V_DIM = 64
N_EXPERT_GROUPS = 4
EXPERTS_PER_GROUP = 8
N_EXPERTS = N_EXPERT_GROUPS * EXPERTS_PER_GROUP
NORM_EPS = 1e-6

A_Q_DIM = SWA_HEADS * HEAD_DIM
A_KV_DIM = SWA_KV_HEADS * HEAD_DIM
SSD_D_INNER = SSD_HEADS * SSD_HEAD_DIM
SSD_BC_DIM = SSD_GROUPS * SSD_STATE
SSD_CONV_DIM = SSD_D_INNER + 2 * SSD_BC_DIM
SSD_DT_DIM = 2 * SSD_HEADS
DIFF_QK_WIDTH = DIFF_HEADS * 2 * DIFF_QK_DIM
DIFF_V_WIDTH = DIFF_HEADS * DIFF_V_DIM
N_ALIBI_HEADS = SWA_HEADS + DIFF_HEADS

LANES = 128
SUBLANES = 8
VMEM_LIMIT = 56 * 1024 * 1024
NEG = -1e30
LOG2E = math.log2(math.e)

SSD_CHUNK = 128
SSD_CHUNKS_PER_STEP = 4
SWA_BLOCKS_PER_STEP = 4
MOE_ROW_TILE = 256
DIFF_PAIR = 2 * 2 * DIFF_QK_DIM
DIFF_VROWS = 80
DIFF_NFEAT = 6
DIFF_ITEM_SPLIT = 1


def _cparams(*sem):
    return pltpu.CompilerParams(dimension_semantics=sem, vmem_limit_bytes=VMEM_LIMIT)


def _rms(x, w):
    return x * lax.rsqrt(jnp.mean(x * x, axis=-1, keepdims=True) + NORM_EPS) * w


def _silu(x):
    return x / (1.0 + jnp.exp(-x))


def _softplus(x):
    return jnp.maximum(x, 0.0) + jnp.log(1.0 + jnp.exp(-jnp.abs(x)))


def _bf16_split(x):
    hi = x.astype(BF16).astype(F32)
    lo = (x - hi).astype(BF16).astype(F32)
    return hi, lo


_C_AK = 0
_C_Z = _C_AK + A_KV_DIM
_C_XBC = _C_Z + SSD_D_INNER
_C_CK = _C_XBC + SSD_CONV_DIM
_C_DT = _C_CK + DIFF_HEADS * LANES
_C_END = _C_DT + LANES
_R_AQ = 0
_R_AV = _R_AQ + A_Q_DIM
_R_CQ = _R_AV + A_KV_DIM
_R_CV = _R_CQ + DIFF_QK_WIDTH
_R_END = _R_CV + DIFF_V_WIDTH


def _inproj_kernel(x_ref, xp_ref, xn_ref, nw_ref, w_ref, wt_ref, cw_ref, cb_ref,
                   ak_ref, z_ref, xs_ref, cm_ref, ck_ref, dt_ref, aqt_ref, avt_ref, cqt_ref, cvt_ref, bt_ref,
                   *, tiles_per_seq, diff_key_tile):
    i = pl.program_id(0)
    tm = x_ref.shape[0]
    nw = nw_ref[...]
    h = _rms(x_ref[...], nw).astype(BF16)

    def seg(lo, hi):
        return jnp.dot(h, w_ref[:, lo:hi], preferred_element_type=F32)

    ak_ref[...] = seg(_C_AK, _C_Z).astype(BF16)
    z_ref[...] = seg(_C_Z, _C_XBC)
    pos = (i * tm + lax.broadcasted_iota(I32, (tm, _C_DT - _C_CK), 0)) % diff_key_tile
    ck_ref[...] = (seg(_C_CK, _C_DT) + _diff_key_features(pos)).astype(BF16)
    dt_ref[...] = seg(_C_DT, _C_END)

    w_xbc = w_ref[:, _C_XBC:_C_CK]
    first = i % tiles_per_seq == 0
    last = i % tiles_per_seq == tiles_per_seq - 1
    prev = jnp.dot(_rms(xp_ref[...], nw).astype(BF16), w_xbc, preferred_element_type=F32)
    nxt = jnp.dot(_rms(xn_ref[...], nw).astype(BF16), w_xbc, preferred_element_type=F32)
    prev = jnp.where(first, 0.0, prev)
    nxt = jnp.where(last, 0.0, nxt)
    ext = jnp.concatenate([prev, seg(_C_XBC, _C_CK), nxt], axis=0)
    half = SSD_CONV // 2
    conv = cb_ref[...]
    for k in range(SSD_CONV):
        off = SUBLANES - half + k
        conv = conv + cw_ref[k:k + 1, :] * ext[off:off + tm, :]
    u = _silu(conv)
    xs_ref[...] = u[:, :SSD_D_INNER]
    bt_ref[...] = jnp.transpose(u[:, SSD_D_INNER:SSD_D_INNER + SSD_BC_DIM])
    cm_ref[...] = u[:, SSD_D_INNER + SSD_BC_DIM:]

    tr = lax.dot_general(wt_ref[...], h, (((1,), (1,)), ((), ())), preferred_element_type=F32)
    aqt_ref[...] = (tr[_R_AQ:_R_AV] * (HEAD_DIM ** -0.5)).astype(BF16)
    avt = tr[_R_AV:_R_CQ].astype(BF16)
    for c in range(tm // LANES):
        avt_ref[c] = avt[:, c * LANES:(c + 1) * LANES]
    cqt_ref[...] = (tr[_R_CQ:_R_CV] * (DIFF_QK_DIM ** -0.5 * LOG2E)).astype(BF16)
    pad = DIFF_VROWS - DIFF_V_DIM
    ones_row = (lax.broadcasted_iota(I32, (pad, tm), 0) == 0).astype(BF16)
    for hh in range(DIFF_HEADS):
        cvt_ref[hh * DIFF_VROWS:hh * DIFF_VROWS + DIFF_V_DIM, :] = (
            tr[_R_CV + hh * DIFF_V_DIM:_R_CV + (hh + 1) * DIFF_V_DIM].astype(BF16))
        cvt_ref[hh * DIFF_VROWS + DIFF_V_DIM:(hh + 1) * DIFF_VROWS, :] = ones_row


def _inproj(x2, norm_w, w_main, w_t, conv_w, conv_b, tm, seq, diff_key_tile):
    t, d = x2.shape
    hb = tm // SUBLANES
    n_hblk = t // SUBLANES
    row = lambda i: (i, 0)
    col = lambda i: (0, i)
    full = lambda i: (0, 0)
    row_outs = [(A_KV_DIM, BF16), (SSD_D_INNER, F32), (SSD_D_INNER, F32), (SSD_BC_DIM, F32),
                (DIFF_HEADS * LANES, BF16), (LANES, F32)]
    out_shape = [jax.ShapeDtypeStruct((t, w), dt) for w, dt in row_outs]
    out_specs = [pl.BlockSpec((tm, w), row) for w, _ in row_outs]
    out_shape += [jax.ShapeDtypeStruct((A_Q_DIM, t), BF16),
                  jax.ShapeDtypeStruct((t // LANES, A_KV_DIM, LANES), BF16),
                  jax.ShapeDtypeStruct((DIFF_QK_WIDTH, t), BF16),
                  jax.ShapeDtypeStruct((DIFF_HEADS * DIFF_VROWS, t), BF16),
                  jax.ShapeDtypeStruct((SSD_BC_DIM, t), F32)]
    out_specs += [pl.BlockSpec((A_Q_DIM, tm), col),
                  pl.BlockSpec((tm // LANES, A_KV_DIM, LANES), lambda i: (i, 0, 0)),
                  pl.BlockSpec((DIFF_QK_WIDTH, tm), col),
                  pl.BlockSpec((DIFF_HEADS * DIFF_VROWS, tm), col),
                  pl.BlockSpec((SSD_BC_DIM, tm), col)]
    return pl.pallas_call(
        functools.partial(_inproj_kernel, tiles_per_seq=seq // tm, diff_key_tile=diff_key_tile),
        grid=(t // tm,),
        in_specs=[pl.BlockSpec((tm, d), row),
                  pl.BlockSpec((SUBLANES, d), lambda i: (jnp.maximum(i * hb - 1, 0), 0)),
                  pl.BlockSpec((SUBLANES, d), lambda i: (jnp.minimum((i + 1) * hb, n_hblk - 1), 0)),
                  pl.BlockSpec((1, d), full),
                  pl.BlockSpec(w_main.shape, full), pl.BlockSpec(w_t.shape, full),
                  pl.BlockSpec(conv_w.shape, full), pl.BlockSpec(conv_b.shape, full)],
        out_specs=out_specs,
        out_shape=out_shape,
        compiler_params=_cparams("parallel"),
        name="inproj",
    )(x2, x2, x2, norm_w, w_main, w_t, conv_w, conv_b)


def _swa_kernel(sink_ref, slope_ref, qt_ref, k_ref, vt_ref, o_ref):
    step = pl.program_id(1)
    s_len = k_ref.shape[0]
    blk = SWA_WINDOW
    band = 3 * blk
    nb = s_len // blk
    rep = SWA_HEADS // SWA_KV_HEADS
    hd = HEAD_DIM
    for u in range(SWA_BLOCKS_PER_STEP):
        n = step * SWA_BLOCKS_PER_STEP + u
        start_blk = jnp.clip(n - 1, 0, nb - 3)
        start = pl.multiple_of(start_blk * blk, blk)
        kb = k_ref[pl.ds(start, band), :]
        v3 = vt_ref[pl.ds(start_blk, 3)]
        vtb = jnp.concatenate([v3[0], v3[1], v3[2]], axis=1)
        qt = qt_ref[:, u * blk:(u + 1) * blk]
        zero = jnp.zeros((hd, rep * blk), BF16)
        grp = [jnp.concatenate([qt[(g * rep + r) * hd:(g * rep + r + 1) * hd] for r in range(rep)], axis=1)
               for g in range(SWA_KV_HEADS)]
        qbd = jnp.concatenate([jnp.concatenate([grp[0], zero], axis=1),
                               jnp.concatenate([zero, grp[1]], axis=1)], axis=0)
        st = jnp.dot(kb, qbd, preferred_element_type=F32)
        kpos = start + lax.broadcasted_iota(I32, (band, blk), 0)
        qpos = n * blk + lax.broadcasted_iota(I32, (band, blk), 1)
        dist_i = jnp.abs(qpos - kpos)
        valid = dist_i <= SWA_WINDOW
        dist = dist_i.astype(F32)
        ps, inv = [], []
        for h in range(SWA_HEADS):
            s = jnp.where(valid, st[:, h * blk:(h + 1) * blk] - slope_ref[h] * dist, NEG)
            sink = sink_ref[h]
            m = jnp.maximum(jnp.max(s, axis=0, keepdims=True), sink)
            p = jnp.exp(s - m)
            inv.append(1.0 / (jnp.sum(p, axis=0, keepdims=True) + jnp.exp(sink - m)))
            ps.append(p.astype(BF16))
        outs = []
        for g in range(SWA_KV_HEADS):
            pg = jnp.concatenate(ps[g * rep:(g + 1) * rep], axis=1)
            og = jnp.dot(vtb[g * hd:(g + 1) * hd, :], pg, preferred_element_type=F32)
            for r in range(rep):
                outs.append(og[:, r * blk:(r + 1) * blk] * inv[g * rep + r])
        o_ref[u * blk:(u + 1) * blk, :] = jnp.transpose(jnp.concatenate(outs, axis=0)).astype(o_ref.dtype)


def _swa(aqt, ak, avt3, sink, slopes, batch, seq):
    blk = SWA_WINDOW
    rows = blk * SWA_BLOCKS_PER_STEP
    steps = seq // rows
    nb = seq // blk
    t = batch * seq
    smem = pl.BlockSpec(memory_space=pltpu.SMEM)
    return pl.pallas_call(
        _swa_kernel,
        grid=(batch, steps),
        in_specs=[smem, smem,
                  pl.BlockSpec((A_Q_DIM, rows), lambda b, s: (0, b * steps + s)),
                  pl.BlockSpec((seq, A_KV_DIM), lambda b, s: (b, 0)),
                  pl.BlockSpec((nb, A_KV_DIM, blk), lambda b, s: (b, 0, 0))],
        out_specs=pl.BlockSpec((rows, A_Q_DIM), lambda b, s: (b * steps + s, 0)),
        out_shape=jax.ShapeDtypeStruct((t, A_Q_DIM), BF16),
        compiler_params=_cparams("parallel", "parallel"),
        name="swa",
    )(sink, slopes, aqt, ak, avt3)


def _diff_key_tile(i, j, tq, tk, nk):
    return ((i * tq) // tk + j) % nk


def _diff_key_features(pos_in_tile):
    lane = lax.broadcasted_iota(I32, pos_in_tile.shape, 1) % LANES - 2 * DIFF_QK_DIM
    coarse = ((pos_in_tile // 16) * 16).astype(F32)
    fine = (pos_in_tile % 16).astype(F32)
    f = lane % DIFF_NFEAT
    feat = jnp.where(f < 2, coarse, jnp.where(f < 4, fine, 1.0))
    return jnp.where((lane >= 0) & (lane < 2 * DIFF_NFEAT), feat, 0.0)


def _diff_kernel(slope_ref, qt_ref, k_ref, vt_ref, lam_ref, sw_ref, o_ref, qtb_ref, m_ref, acc_ref,
                 s_ref, p_ref, *, lambda_init, n_split):
    i = pl.program_id(1)
    j = pl.program_id(2)
    nk = pl.num_programs(2)
    tq = qt_ref.shape[1]
    tk = k_ref.shape[0]
    dq = DIFF_QK_DIM
    hw = 2 * dq
    nf = DIFF_NFEAT

    @pl.when(j == 0)
    def _():
        m_ref[...] = jnp.full(m_ref.shape, NEG, F32)
        acc_ref[...] = jnp.zeros(acc_ref.shape, F32)
        ii = lax.broadcasted_iota(I32, (1, 2 * tq), 1)
        ii = jnp.where(ii >= tq, ii - tq, ii).astype(F32)
        qt = qt_ref[...]
        col = lax.broadcasted_iota(I32, (hw, 2 * tq), 1)
        row = lax.broadcasted_iota(I32, (hw, 2 * tq), 0)
        own_map = row // dq == col // tq
        for h in range(DIFF_HEADS):
            qh = qt[h * hw:(h + 1) * hw, :]
            qh2 = jnp.where(own_map, jnp.concatenate([qh, qh], axis=1), jnp.zeros((hw, 2 * tq), BF16))
            sl = jnp.full((1, 2 * tq), slope_ref[h], F32)
            s_hi, s_lo = _bf16_split(sl)
            v_hi, v_lo = _bf16_split(-sl * ii)
            rows = jnp.concatenate([s_hi, s_lo, s_hi, s_lo, v_hi, v_lo], axis=0)
            zrow = jnp.zeros((nf, 2 * tq), F32)
            zero = jnp.zeros((hw - 2 * nf, 2 * tq), F32)
            variants = ([rows, zrow], [zrow, -rows], [zrow, zrow])
            for v, pieces in enumerate(variants):
                qtb_ref[v, h, 0:hw, :] = qh2
                qtb_ref[v, h, hw:2 * hw, :] = jnp.concatenate(pieces + [zero], axis=0).astype(BF16)

    q0 = i * tq
    k0 = _diff_key_tile(i, j, tq, tk, nk) * tk

    width = 2 * tq // n_split
    items = [(h, mp) for h in range(DIFF_HEADS) for mp in range(n_split)]

    def lanes(mp):
        return slice(mp * width, (mp + 1) * width)

    def scores(h, mp, variant):
        return jnp.dot(k_ref[:, h * LANES:(h + 1) * LANES], qtb_ref[variant, h, :, lanes(mp)],
                       preferred_element_type=F32)

    def stage_scores(h, mp, s):
        s_ref[h, :, lanes(mp)] = s
        return jnp.max(s, axis=0, keepdims=True)

    def softmax_step(h, mp, tile_max, shift):
        m_old = m_ref[h:h + 1, lanes(mp)]
        m_new = jnp.maximum(m_old, tile_max + shift)
        p_ref[h, :, lanes(mp)] = jnp.exp2(s_ref[h, :, lanes(mp)] - (m_new - shift)).astype(BF16)
        m_ref[h:h + 1, lanes(mp)] = m_new
        return jnp.exp2(m_old - m_new)

    def accumulate(h, mp, alpha):
        pv = jnp.dot(vt_ref[h * DIFF_VROWS:(h + 1) * DIFF_VROWS, :], p_ref[h, :, lanes(mp)],
                     preferred_element_type=F32)
        acc_ref[h, :, lanes(mp)] = alpha * acc_ref[h, :, lanes(mp)] + pv

    def tile_pass(score_fn, shift_fn):
        maxima, alphas = {}, {}
        for w in range(len(items) + 2):
            if w < len(items):
                h, mp = items[w]
                maxima[w] = stage_scores(h, mp, score_fn(h, mp))
            if 0 <= w - 1 < len(items):
                h, mp = items[w - 1]
                alphas[w - 1] = softmax_step(h, mp, maxima[w - 1], shift_fn(h))
            if 0 <= w - 2 < len(items):
                h, mp = items[w - 2]
                accumulate(h, mp, alphas[w - 2])

    @pl.when(j == 0)
    def _():
        kpos = k0 + lax.broadcasted_iota(I32, (tk, width), 0)
        qpos = q0 + lax.broadcasted_iota(I32, (tk, width), 1) % tq
        dist = jnp.abs(qpos - kpos).astype(F32)
        tile_pass(lambda h, mp: scores(h, mp, 2) - slope_ref[h] * dist, lambda h: 0.0)

    @pl.when(j > 0)
    def _():
        below = k0 < q0
        variant = jnp.where(below, 0, 1)
        sign = jnp.where(below, 1.0, -1.0)
        tile_pass(lambda h, mp: scores(h, mp, variant),
                  lambda h: sign * slope_ref[h] * (k0 - q0).astype(F32))

    @pl.when(j == nk - 1)
    def _():
        lp = lam_ref[...]
        lam = (jnp.exp(jnp.sum(lp[0:1] * lp[1:2], axis=-1, keepdims=True))
               - jnp.exp(jnp.sum(lp[2:3] * lp[3:4], axis=-1, keepdims=True)) + lambda_init)
        outs = []
        for h in range(DIFF_HEADS):
            a = acc_ref[h]
            o = a[0:DIFF_V_DIM] / a[DIFF_V_DIM:DIFF_V_DIM + 1]
            o = o[:, 0:tq] - lam * o[:, tq:2 * tq]
            ms = jnp.mean(o * o, axis=0, keepdims=True)
            outs.append(o * lax.rsqrt(ms + NORM_EPS) * sw_ref[...] * (1.0 - lambda_init))
        o_ref[...] = jnp.transpose(jnp.concatenate(outs, axis=0)).astype(o_ref.dtype)


def _diff(cqt, ck, cvt, slopes, lam_params, subln_w_col, lambda_init, batch, seq, tq, tk):
    assert tk % tq == 0 and seq % tk == 0
    nq, nk = seq // tq, seq // tk
    t = batch * seq
    smem = pl.BlockSpec(memory_space=pltpu.SMEM)
    key_tile = functools.partial(_diff_key_tile, tq=tq, tk=tk, nk=nk)
    return pl.pallas_call(
        functools.partial(_diff_kernel, lambda_init=lambda_init, n_split=DIFF_ITEM_SPLIT),
        grid=(batch, nq, nk),
        in_specs=[smem,
                  pl.BlockSpec((DIFF_QK_WIDTH, tq), lambda b, i, j: (0, b * nq + i)),
                  pl.BlockSpec((tk, DIFF_HEADS * LANES), lambda b, i, j: (b * nk + key_tile(i, j), 0)),
                  pl.BlockSpec((DIFF_HEADS * DIFF_VROWS, tk), lambda b, i, j: (0, b * nk + key_tile(i, j))),
                  pl.BlockSpec(lam_params.shape, lambda b, i, j: (0, 0)),
                  pl.BlockSpec(subln_w_col.shape, lambda b, i, j: (0, 0))],
        out_specs=pl.BlockSpec((tq, DIFF_V_WIDTH), lambda b, i, j: (b * nq + i, 0)),
        out_shape=jax.ShapeDtypeStruct((t, DIFF_V_WIDTH), BF16),
        scratch_shapes=[pltpu.VMEM((3, DIFF_HEADS, LANES, 2 * tq), BF16),
                        pltpu.VMEM((DIFF_HEADS, 2 * tq), F32),
                        pltpu.VMEM((DIFF_HEADS, DIFF_VROWS, 2 * tq), F32),
                        pltpu.VMEM((DIFF_HEADS, tk, 2 * tq), F32),
                        pltpu.VMEM((DIFF_HEADS, tk, 2 * tq), BF16)],
        compiler_params=_cparams("parallel", "parallel", "arbitrary"),
        name="diffattn",
    )(slopes, cqt, ck, cvt, lam_params, subln_w_col)


def _ssd_direction(fwd, off, xs_ref, bt_ref, cm_ref, dt_ref, dtb_ref, alog_ref, dsk_ref, y_ref, state_ref):
    q = SSD_CHUNK
    rows = slice(off, off + q)
    lane0 = 0 if fwd else SSD_HEADS
    dt_all = _softplus(dt_ref[rows, :] + dtb_ref[...])
    dta_all = dt_all * -jnp.exp(alog_ref[...])
    row = lax.broadcasted_iota(I32, (q, q), 0)
    col = lax.broadcasted_iota(I32, (q, q), 1)
    keep = (row >= col) if fwd else (row <= col)
    tri = keep.astype(BF16)
    part_hi = dta_all.astype(BF16)
    rest = dta_all - part_hi.astype(F32)
    part_mid = rest.astype(BF16)
    part_lo = (rest - part_mid.astype(F32)).astype(BF16)
    da_all = (jnp.dot(tri, part_hi, preferred_element_type=F32)
              + jnp.dot(tri, part_mid, preferred_element_type=F32)
              + jnp.dot(tri, part_lo, preferred_element_type=F32))
    da_all_t = jnp.transpose(da_all)
    dt_all_t = jnp.transpose(dt_all)
    tot_all = jnp.sum(dta_all, axis=0, keepdims=True)

    xs = xs_ref[rows, :]
    bmt = bt_ref[:, rows]
    cm = cm_ref[rows, :]
    rep = SSD_HEADS // SSD_GROUPS
    ns = SSD_STATE
    g_mats = [jnp.dot(cm[:, g * ns:(g + 1) * ns].astype(BF16), bmt[g * ns:(g + 1) * ns, :].astype(BF16),
                      preferred_element_type=F32) for g in range(SSD_GROUPS)]
    ys = []
    for h in range(SSD_HEADS):
        g = h // rep
        ln = lane0 + h
        dac = da_all[:, ln:ln + 1]
        dar = da_all_t[ln:ln + 1, :]
        dtr = dt_all_t[ln:ln + 1, :]
        tot = tot_all[:, ln:ln + 1]
        dac_b = jnp.broadcast_to(dac, (q, q))
        decay = jnp.exp(jnp.where(keep, dac_b - dar, NEG))
        xh = xs[:, h * SSD_HEAD_DIM:(h + 1) * SSD_HEAD_DIM]
        xhb = xh.astype(BF16)
        y = jnp.dot((g_mats[g] * decay * dtr).astype(BF16), xhb, preferred_element_type=F32)
        st = state_ref[h]
        c_in = (cm[:, g * ns:(g + 1) * ns] * jnp.exp(dac_b[:, 0:ns])).astype(BF16)
        y = y + jnp.dot(c_in, st.astype(BF16), preferred_element_type=F32)
        to_end = jnp.exp(tot - dar) * dtr
        b_out = (bmt[g * ns:(g + 1) * ns, :] * to_end).astype(BF16)
        state_ref[h] = st * jnp.exp(tot) + jnp.dot(b_out, xhb, preferred_element_type=F32)
        if fwd:
            y = y + dsk_ref[:, h:h + 1] * xh
        ys.append(y)
    y_ref[rows, :] = jnp.concatenate(ys, axis=-1)


def _ssd_kernel(xsf_ref, btf_ref, cmf_ref, dtf_ref, xsb_ref, btb_ref, cmb_ref, dtb_in_ref,
                dtbias_ref, alog_ref, dsk_ref, yf_ref, yb_ref, state_ref):
    @pl.when(pl.program_id(1) == 0)
    def _():
        state_ref[...] = jnp.zeros(state_ref.shape, F32)

    n = SSD_CHUNKS_PER_STEP
    for u in range(n):
        _ssd_direction(True, u * SSD_CHUNK, xsf_ref, btf_ref, cmf_ref, dtf_ref, dtbias_ref, alog_ref, dsk_ref,
                       yf_ref, state_ref.at[0])
        _ssd_direction(False, (n - 1 - u) * SSD_CHUNK, xsb_ref, btb_ref, cmb_ref, dtb_in_ref, dtbias_ref,
                       alog_ref, dsk_ref, yb_ref, state_ref.at[1])


def _ssd(xs, bt, cm, dt_raw, dt_bias, a_log, d_skip, batch, seq):
    q = SSD_CHUNK * SSD_CHUNKS_PER_STEP
    nc = seq // q
    t = batch * seq
    full = lambda b, c: (0, 0)
    fw = lambda b, c: b * nc + c
    bw = lambda b, c: b * nc + nc - 1 - c

    def specs(idx):
        return [pl.BlockSpec((q, SSD_D_INNER), lambda b, c: (idx(b, c), 0)),
                pl.BlockSpec((SSD_BC_DIM, q), lambda b, c: (0, idx(b, c))),
                pl.BlockSpec((q, SSD_BC_DIM), lambda b, c: (idx(b, c), 0)),
                pl.BlockSpec((q, LANES), lambda b, c: (idx(b, c), 0))]

    return pl.pallas_call(
        _ssd_kernel,
        grid=(batch, nc),
        in_specs=specs(fw) + specs(bw) + [pl.BlockSpec(dt_bias.shape, full), pl.BlockSpec(a_log.shape, full),
                                          pl.BlockSpec(d_skip.shape, full)],
        out_specs=[pl.BlockSpec((q, SSD_D_INNER), lambda b, c: (fw(b, c), 0)),
                   pl.BlockSpec((q, SSD_D_INNER), lambda b, c: (bw(b, c), 0))],
        out_shape=[jax.ShapeDtypeStruct((t, SSD_D_INNER), F32), jax.ShapeDtypeStruct((t, SSD_D_INNER), F32)],
        scratch_shapes=[pltpu.VMEM((2, SSD_HEADS, SSD_STATE, SSD_HEAD_DIM), F32)],
        compiler_params=_cparams("parallel", "arbitrary"),
        name="ssd",
    )(xs, bt, cm, dt_raw, xs, bt, cm, dt_raw, dt_bias, a_log, d_skip)


def _outproj_kernel(x_ref, ya_ref, yf_ref, yb_ref, z_ref, snw_ref, yc_ref, wo_ref, fnw_ref, wrt_ref, br_ref,
                    xn_ref, ri_ref, rf_ref, cnt_ref, tri_ref, carry_ref):
    step = pl.program_id(0)
    tm = x_ref.shape[0]

    @pl.when(step == 0)
    def _():
        carry_ref[...] = jnp.zeros(carry_ref.shape, F32)
        r = lax.broadcasted_iota(I32, (tm, tm), 0)
        cc = lax.broadcasted_iota(I32, (tm, tm), 1)
        tri_ref[...] = (r <= cc).astype(BF16)

    y = (yf_ref[...] + yb_ref[...]) * _silu(z_ref[...])
    yb = _rms(y, snw_ref[...]).astype(BF16)
    acc = jnp.dot(ya_ref[...], wo_ref[0:A_Q_DIM, :], preferred_element_type=F32)
    acc = acc + jnp.dot(yb, wo_ref[A_Q_DIM:A_Q_DIM + SSD_D_INNER, :], preferred_element_type=F32)
    acc = acc + jnp.dot(yc_ref[...], wo_ref[A_Q_DIM + SSD_D_INNER:, :], preferred_element_type=F32)
    xn = x_ref[...] + acc
    xn_ref[...] = xn

    h = _rms(xn, fnw_ref[...])
    h_hi = h.astype(BF16)
    h_lo = (h - h_hi.astype(F32)).astype(BF16)
    nt = (((1,), (1,)), ((), ()))
    logits = (lax.dot_general(wrt_ref[0], h_hi, nt, preferred_element_type=F32)
              + lax.dot_general(wrt_ref[0], h_lo, nt, preferred_element_type=F32)
              + lax.dot_general(wrt_ref[1], h_hi, nt, preferred_element_type=F32)) + br_ref[...]
    ne, epg, ng = N_EXPERTS, EXPERTS_PER_GROUP, N_EXPERT_GROUPS
    gl = logits[ne:ne + ng, :]
    gmax = jnp.max(gl, axis=0, keepdims=True)
    g_sel = jnp.full((1, tm), float(ng - 1), F32)
    for g in range(ng - 2, -1, -1):
        g_sel = jnp.where(gl[g:g + 1, :] == gmax, float(g), g_sel)
    g_gate = 1.0 / jnp.sum(jnp.exp(gl - gmax), axis=0, keepdims=True)
    e_in = logits[0:epg, :]
    for g in range(1, ng):
        e_in = jnp.where(g_sel == float(g), logits[g * epg:(g + 1) * epg, :], e_in)
    sub = lax.broadcasted_iota(I32, (epg, tm), 0).astype(F32)
    m1 = jnp.max(e_in, axis=0, keepdims=True)
    i1 = jnp.min(jnp.where(e_in == m1, sub, float(epg)), axis=0, keepdims=True)
    rest = jnp.where(sub == i1, NEG, e_in)
    m2 = jnp.max(rest, axis=0, keepdims=True)
    i2 = jnp.min(jnp.where(rest == m2, sub, float(epg)), axis=0, keepdims=True)
    r = jnp.exp(m2 - m1)
    c1 = g_gate / (1.0 + r)
    c2 = g_gate * r / (1.0 + r)
    e1 = (g_sel * epg + i1).astype(I32)
    e2 = (g_sel * epg + i2).astype(I32)

    erow = lax.broadcasted_iota(I32, (ne, tm), 0)
    hit1 = erow == e1
    hit2 = erow == e2
    oh = jnp.where(hit1 | hit2, 1.0, 0.0)
    incl = jnp.dot(oh.astype(BF16), tri_ref[...], preferred_element_type=F32)
    before = incl - oh + carry_ref[:, 0:1]
    rank1 = jnp.sum(jnp.where(hit1, before, 0.0), axis=0, keepdims=True)
    rank2 = jnp.sum(jnp.where(hit2, before, 0.0), axis=0, keepdims=True)
    carry_ref[...] = carry_ref[...] + jnp.sum(oh, axis=1, keepdims=True)
    cnt_ref[...] = carry_ref[...]
    zi = jnp.zeros((1, tm), I32)
    ri_ref[...] = jnp.concatenate([e1, e2, rank1.astype(I32), rank2.astype(I32), zi, zi, zi, zi], axis=0)
    zf = jnp.zeros((1, tm), F32)
    rf_ref[...] = jnp.concatenate([c1, c2, zf, zf, zf, zf, zf, zf], axis=0)


def _outproj(x2, ya, yf, yb, z, ssd_norm_w, yc, w_out, ffn_norm_w, wrt, br, tm):
    t, d = x2.shape
    row = lambda i: (i, 0)
    full = lambda i: (0, 0)
    return pl.pallas_call(
        _outproj_kernel,
        grid=(t // tm,),
        in_specs=[pl.BlockSpec((tm, d), row),
                  pl.BlockSpec((tm, A_Q_DIM), row),
                  pl.BlockSpec((tm, SSD_D_INNER), row),
                  pl.BlockSpec((tm, SSD_D_INNER), row),
                  pl.BlockSpec((tm, SSD_D_INNER), row),
                  pl.BlockSpec(ssd_norm_w.shape, full),
                  pl.BlockSpec((tm, DIFF_V_WIDTH), row),
                  pl.BlockSpec(w_out.shape, full),
                  pl.BlockSpec(ffn_norm_w.shape, full),
                  pl.BlockSpec(wrt.shape, lambda i: (0, 0, 0)),
                  pl.BlockSpec(br.shape, full)],
        out_specs=[pl.BlockSpec((tm, d), row),
                   pl.BlockSpec((SUBLANES, tm), lambda i: (0, i)),
                   pl.BlockSpec((SUBLANES, tm), lambda i: (0, i)),
                   pl.BlockSpec((N_EXPERTS, LANES), full)],
        out_shape=[jax.ShapeDtypeStruct((t, d), F32),
                   jax.ShapeDtypeStruct((SUBLANES, t), I32),
                   jax.ShapeDtypeStruct((SUBLANES, t), F32),
                   jax.ShapeDtypeStruct((N_EXPERTS, LANES), F32)],
        scratch_shapes=[pltpu.VMEM((tm, tm), BF16), pltpu.VMEM((N_EXPERTS, LANES), F32)],
        compiler_params=_cparams("arbitrary"),
        name="outproj_router",
    )(x2, ya, yf, yb, z, ssd_norm_w, yc, w_out, ffn_norm_w, wrt, br)


_PAD_PIECES = tuple(1 << b for b in reversed(range(MOE_ROW_TILE.bit_length() - 1)))


def _dispatch_kernel(slot1_ref, slot2_ref, pstart_ref, plen_ref, nused_ref, x_ref, xs_hbm, zero_ref, xbuf, sem):
    i = pl.program_id(0)
    tm = x_ref.shape[0]

    @pl.when(i == 0)
    def _():
        zero_ref[...] = jnp.zeros(zero_ref.shape, F32)

        def pieces(e, wait):
            n = plen_ref[e]
            first = pstart_ref[e]
            off = first + n
            for b in _PAD_PIECES:
                off = off - (n & b)
                dst = pl.ds(pl.multiple_of(off, b), b) if b >= SUBLANES else None
                if dst is not None:
                    @pl.when((n & b) != 0)
                    def _():
                        cp = pltpu.make_async_copy(zero_ref.at[pl.ds(0, b)], xs_hbm.at[dst], sem.at[2])
                        cp.wait() if wait else cp.start()

            for u in range(SUBLANES - 1):
                @pl.when(u < (n & (SUBLANES - 1)))
                def _():
                    cp = pltpu.make_async_copy(zero_ref.at[pl.ds(0, 1)], xs_hbm.at[pl.ds(first + u, 1)],
                                               sem.at[2])
                    cp.wait() if wait else cp.start()

        def tail(tile, wait):
            big = _PAD_PIECES[0]
            for part in range(MOE_ROW_TILE // big):
                dst = xs_hbm.at[pl.ds(pl.multiple_of(tile * MOE_ROW_TILE + part * big, big), big)]
                cp = pltpu.make_async_copy(zero_ref, dst, sem.at[2])
                cp.wait() if wait else cp.start()

        def loop(fn, lo, hi, wait):
            def body(k, carry):
                fn(k, wait)
                return carry

            lax.fori_loop(lo, hi, body, 0)

        n_tiles = xs_hbm.shape[0] // MOE_ROW_TILE
        for wait in (False, True):
            loop(pieces, 0, N_EXPERTS, wait)
            loop(tail, nused_ref[0], n_tiles, wait)

    base = i * tm
    last = pl.num_programs(0) - 1

    def drain(slot):
        for _ in range(2):
            pltpu.make_async_copy(xbuf.at[slot], xs_hbm.at[pl.ds(0, tm)], sem.at[slot]).wait()

    for parity in range(2):
        @pl.when(i % 2 == parity)
        def _():
            buf = xbuf.at[parity]
            buf[...] = x_ref[...]
            for r in range(tm):
                src = buf.at[pl.ds(r, 1)]
                pltpu.make_async_copy(src, xs_hbm.at[pl.ds(slot1_ref[base + r], 1)],
                                      sem.at[parity]).start(priority=0)
                pltpu.make_async_copy(src, xs_hbm.at[pl.ds(slot2_ref[base + r], 1)],
                                      sem.at[parity]).start(priority=1)

            @pl.when(i > 0)
            def _():
                drain(1 - parity)

            @pl.when(i == last)
            def _():
                drain(parity)


def _dispatch(xn, slot1, slot2, pad_start, pad_len, n_used, n_rows, tm):
    t, d = xn.shape
    grid_spec = pltpu.PrefetchScalarGridSpec(
        num_scalar_prefetch=5,
        grid=(t // tm,),
        in_specs=[pl.BlockSpec((tm, d), lambda i, s1, s2, ps, pn, nu: (i, 0))],
        out_specs=pl.BlockSpec(memory_space=pl.ANY),
        scratch_shapes=[pltpu.VMEM((_PAD_PIECES[0], d), F32), pltpu.VMEM((2, tm, d), F32),
                        pltpu.SemaphoreType.DMA((3,))],
    )
    return pl.pallas_call(
        _dispatch_kernel,
        grid_spec=grid_spec,
        out_shape=jax.ShapeDtypeStruct((n_rows, d), F32),
        compiler_params=_cparams("arbitrary"),
        name="moe_dispatch",
    )(slot1, slot2, pad_start, pad_len, n_used, xn)


def _moe_kernel(texp_ref, nused_ref, x_ref, fnw_ref, wg_ref, wu_ref, wd_ref, y_ref):
    i = pl.program_id(0)

    @pl.when(i < nused_ref[0])
    def _():
        h = _rms(x_ref[...], fnw_ref[...]).astype(BF16)
        hg = jnp.dot(h, wg_ref[...].astype(BF16), preferred_element_type=F32)
        hu = jnp.dot(h, wu_ref[...].astype(BF16), preferred_element_type=F32)
        act = (_silu(hg) * hu).astype(BF16)
        y_ref[...] = jnp.dot(act, wd_ref[...].astype(BF16), preferred_element_type=F32)

    @pl.when(i >= nused_ref[0])
    def _():
        y_ref[...] = jnp.zeros(y_ref.shape, F32)


def _moe(xs, ffn_norm_w, w_gate, w_up, w_down, layer, tile_expert, n_used):
    n_rows, d = xs.shape
    f = w_gate.shape[-1]
    tr = MOE_ROW_TILE

    def used(i, nu):
        return jnp.maximum(jnp.minimum(i, nu[0] - 1), 0)

    grid_spec = pltpu.PrefetchScalarGridSpec(
        num_scalar_prefetch=2,
        grid=(n_rows // tr,),
        in_specs=[pl.BlockSpec((tr, d), lambda i, te, nu: (used(i, nu), 0)),
                  pl.BlockSpec(ffn_norm_w.shape, lambda i, te, nu: (0, 0)),
                  pl.BlockSpec((None, None, d, f), lambda i, te, nu: (layer, te[used(i, nu)], 0, 0)),
                  pl.BlockSpec((None, None, d, f), lambda i, te, nu: (layer, te[used(i, nu)], 0, 0)),
                  pl.BlockSpec((None, None, f, d), lambda i, te, nu: (layer, te[used(i, nu)], 0, 0))],
        out_specs=pl.BlockSpec((tr, d), lambda i, te, nu: (i, 0)),
    )
    return pl.pallas_call(
        _moe_kernel,
        grid_spec=grid_spec,
        out_shape=jax.ShapeDtypeStruct((n_rows, d), F32),
        compiler_params=_cparams("arbitrary"),
        name="moe_experts",
    )(tile_expert, n_used, xs, ffn_norm_w, w_gate, w_up, w_down)


def _combine_kernel(slot1_ref, slot2_ref, x_ref, cw_ref, nw_ref, y_hbm, o_ref, ybuf, sem, *, final_norm):
    i = pl.program_id(0)
    n = pl.num_programs(0)
    tm = x_ref.shape[0]

    def start_gather(tile, slot):
        base = tile * tm
        for r in range(tm):
            pltpu.make_async_copy(y_hbm.at[pl.ds(slot1_ref[base + r], 1)], ybuf.at[slot, 0, pl.ds(r, 1)],
                                  sem.at[slot]).start(priority=0)
            pltpu.make_async_copy(y_hbm.at[pl.ds(slot2_ref[base + r], 1)], ybuf.at[slot, 1, pl.ds(r, 1)],
                                  sem.at[slot]).start(priority=1)

    def compute(slot):
        for k in range(2):
            pltpu.make_async_copy(y_hbm.at[pl.ds(0, tm)], ybuf.at[slot, k], sem.at[slot]).wait()
        cw = cw_ref[...]
        out = x_ref[...] + cw[:, 0:1] * ybuf[slot, 0] + cw[:, 1:2] * ybuf[slot, 1]
        if final_norm:
            out = _rms(out, nw_ref[...])
        o_ref[...] = out

    @pl.when(i == 0)
    def _():
        start_gather(0, 0)

    for parity in range(2):
        @pl.when(i % 2 == parity)
        def _():
            @pl.when(i + 1 < n)
            def _():
                start_gather(i + 1, 1 - parity)

            compute(parity)


def _combine(xn, cw, norm_w, y_sorted, slot1, slot2, tm, final_norm):
    t, d = xn.shape
    grid_spec = pltpu.PrefetchScalarGridSpec(
        num_scalar_prefetch=2,
        grid=(t // tm,),
        in_specs=[pl.BlockSpec((tm, d), lambda i, s1, s2: (i, 0)),
                  pl.BlockSpec((tm, cw.shape[1]), lambda i, s1, s2: (i, 0)),
                  pl.BlockSpec(norm_w.shape, lambda i, s1, s2: (0, 0)),
                  pl.BlockSpec(memory_space=pl.ANY)],
        out_specs=pl.BlockSpec((tm, d), lambda i, s1, s2: (i, 0)),
        scratch_shapes=[pltpu.VMEM((2, 2, tm, d), F32), pltpu.SemaphoreType.DMA((2,))],
    )
    return pl.pallas_call(
        functools.partial(_combine_kernel, final_norm=final_norm),
        grid_spec=grid_spec,
        out_shape=jax.ShapeDtypeStruct((t, d), F32),
        compiler_params=_cparams("arbitrary"),
        name="moe_combine",
    )(slot1, slot2, xn, cw, norm_w, y_sorted)


def _pad_lanes(v):
    v = v.reshape(1, -1).astype(F32)
    return jnp.pad(v, ((0, 0), (0, LANES - v.shape[1])))


def kernel(x, attn_norm_w, w_in, swa_sink, ssd_conv_w, ssd_conv_b, ssd_dt_bias, ssd_a_log, ssd_d, ssd_norm_w,
           diff_lambda, diff_subln_w, w_out, ffn_norm_w, w_router_group, b_router_group, w_router_expert,
           b_router_expert, w_gate, w_up, w_down, final_norm_w):
    return _forward(x, attn_norm_w, w_in, swa_sink, ssd_conv_w, ssd_conv_b, ssd_dt_bias, ssd_a_log, ssd_d,
                    ssd_norm_w, diff_lambda, diff_subln_w, w_out, ffn_norm_w, w_router_group, b_router_group,
                    w_router_expert, b_router_expert, w_gate, w_up, w_down, final_norm_w)


def _forward(x, attn_norm_w, w_in, swa_sink, ssd_conv_w, ssd_conv_b, ssd_dt_bias, ssd_a_log, ssd_d, ssd_norm_w,
             diff_lambda, diff_subln_w, w_out, ffn_norm_w, w_router_group, b_router_group, w_router_expert,
             b_router_expert, w_gate, w_up, w_down, final_norm_w, tm=512, tq=512, tk=1024, tmc=256):
    batch, seq, d = x.shape
    depth = w_in.shape[0]
    t = batch * seq
    tr = MOE_ROW_TILE
    n_tiles = (2 * t) // tr + N_EXPERTS
    slopes = jnp.exp2(-8.0 * jnp.arange(1, N_ALIBI_HEADS + 1, dtype=F32) / N_ALIBI_HEADS)
    swa_slopes, diff_slopes = slopes[:SWA_HEADS], slopes[SWA_HEADS:]

    sizes = [A_Q_DIM, A_KV_DIM, A_KV_DIM, SSD_D_INNER, SSD_CONV_DIM, SSD_DT_DIM, DIFF_QK_WIDTH, DIFF_QK_WIDTH,
             DIFF_V_WIDTH]
    offs = [0]
    for s in sizes:
        offs.append(offs[-1] + s)
    o_aq, o_ak, o_av, o_z, o_xbc, o_dt, o_cq, o_ck, o_cv, o_end = offs

    x2 = x.reshape(t, d)
    for l in range(depth):
        w = w_in[l]
        hw = 2 * DIFF_QK_DIM
        w_ck = jnp.pad(w[:, o_ck:o_cv].reshape(d, DIFF_HEADS, hw), ((0, 0), (0, 0), (0, LANES - hw)))
        w_main = jnp.concatenate(
            [w[:, o_ak:o_av], w[:, o_z:o_dt], w_ck.reshape(d, DIFF_HEADS * LANES), w[:, o_dt:o_cq],
             jnp.zeros((d, LANES - SSD_DT_DIM), w.dtype)], axis=1).astype(BF16)
        w_t = jnp.concatenate([w[:, o_aq:o_ak], w[:, o_av:o_z], w[:, o_cq:o_ck], w[:, o_cv:o_end]],
                              axis=1).T.astype(BF16)
        ak, z, xs, cm, ck, dt_raw, aqt, avt3, cqt, cvt, bt = _inproj(
            x2, attn_norm_w[l].reshape(1, d), w_main, w_t, ssd_conv_w[l].astype(F32),
            ssd_conv_b[l].reshape(1, -1).astype(F32), tm, seq, tk)

        ya = _swa(aqt, ak, avt3, swa_sink[l].astype(F32), swa_slopes, batch, seq)
        lambda_init = 0.8 - 0.6 * math.exp(-0.3 * l)
        yc = _diff(cqt, ck, cvt, diff_slopes * LOG2E, diff_lambda[l].astype(F32),
                   diff_subln_w[l].reshape(DIFF_V_DIM, 1).astype(F32), lambda_init, batch, seq, tq, tk)
        yf, yb = _ssd(xs, bt, cm, dt_raw, _pad_lanes(ssd_dt_bias[l]), _pad_lanes(ssd_a_log[l]),
                      _pad_lanes(ssd_d[l]), batch, seq)

        wr32 = jnp.concatenate([w_router_expert[l], w_router_group[l],
                                jnp.zeros((d, SUBLANES - N_EXPERT_GROUPS), F32)], axis=1).T.astype(F32)
        wr_hi = wr32.astype(BF16)
        wrt = jnp.stack([wr_hi, (wr32 - wr_hi.astype(F32)).astype(BF16)])
        br = jnp.concatenate([b_router_expert[l], b_router_group[l],
                              jnp.zeros((SUBLANES - N_EXPERT_GROUPS,), F32)]).reshape(-1, 1).astype(F32)
        xn, ri, rf, cnt = _outproj(x2, ya, yf, yb, z, ssd_norm_w[l].reshape(1, -1), yc, w_out[l].astype(BF16),
                                   ffn_norm_w[l].reshape(1, d), wrt, br, tm)

        counts = cnt[:, 0].astype(I32)
        padded = ((counts + tr - 1) // tr) * tr
        ends = jnp.cumsum(padded)
        starts = ends - padded
        experts = jnp.arange(N_EXPERTS, dtype=I32)[:, None]

        def slot_of(e, rank):
            return jnp.sum(jnp.where(e[None, :] == experts, starts[:, None], 0), axis=0) + rank

        slot1 = slot_of(ri[0], ri[2])
        slot2 = slot_of(ri[1], ri[3])
        tile_start = jnp.arange(n_tiles, dtype=I32) * tr
        tile_expert = jnp.minimum(jnp.sum(ends[None, :] <= tile_start[:, None], axis=1), N_EXPERTS - 1).astype(I32)
        n_used = (ends[-1] // tr).astype(I32).reshape(1)

        xs_sorted = _dispatch(xn, slot1, slot2, starts + counts, padded - counts, n_used, n_tiles * tr, tmc)
        y_sorted = _moe(xs_sorted, ffn_norm_w[l].reshape(1, d), w_gate, w_up, w_down, l, tile_expert, n_used)
        last = l == depth - 1
        x2 = _combine(xn, rf.T, final_norm_w.reshape(1, d), y_sorted, slot1, slot2, tmc, last)
    return x2.reshape(batch, seq, d)
```

```python
import functools
import math

import jax
import jax.numpy as jnp
from jax import lax
from jax.experimental import pallas as pl
from jax.experimental.pallas import tpu as pltpu

F32 = jnp.float32
BF16 = jnp.bfloat16
I32 = jnp.int32

HEAD_DIM = 64
SWA_HEADS = 6
SWA_KV_HEADS = 2
SWA_WINDOW = 128
SSD_HEADS = 6
SSD_HEAD_DIM = 64
SSD_GROUPS = 2
SSD_STATE = 64
SSD_CONV = 5
DIFF_HEADS = 4
DIFF_QK_DIM = 32
DIFF_V_DIM = 64
N_EXPERT_GROUPS = 4
EXPERTS_PER_GROUP = 8
N_EXPERTS = N_EXPERT_GROUPS * EXPERTS_PER_GROUP
NORM_EPS = 1e-6

A_Q_DIM = SWA_HEADS * HEAD_DIM
A_KV_DIM = SWA_KV_HEADS * HEAD_DIM
SSD_D_INNER = SSD_HEADS * SSD_HEAD_DIM
SSD_BC_DIM = SSD_GROUPS * SSD_STATE
SSD_CONV_DIM = SSD_D_INNER + 2 * SSD_BC_DIM
SSD_DT_DIM = 2 * SSD_HEADS
DIFF_QK_WIDTH = DIFF_HEADS * 2 * DIFF_QK_DIM
DIFF_V_WIDTH = DIFF_HEADS * DIFF_V_DIM
N_ALIBI_HEADS = SWA_HEADS + DIFF_HEADS

LANES = 128
SUBLANES = 8
VMEM_LIMIT = 56 * 1024 * 1024
NEG = -1e30
LOG2E = math.log2(math.e)

SSD_CHUNK = 128
SSD_CHUNKS_PER_STEP = 4
SWA_BLOCKS_PER_STEP = 4
MOE_ROW_TILE = 256
DIFF_PAIR = 2 * 2 * DIFF_QK_DIM
DIFF_VROWS = 80
DIFF_NFEAT = 6


def _cparams(*sem):
    return pltpu.CompilerParams(dimension_semantics=sem, vmem_limit_bytes=VMEM_LIMIT)


def _rms(x, w):
    return x * lax.rsqrt(jnp.mean(x * x, axis=-1, keepdims=True) + NORM_EPS) * w


def _silu(x):
    return x / (1.0 + jnp.exp(-x))


def _softplus(x):
    return jnp.maximum(x, 0.0) + jnp.log(1.0 + jnp.exp(-jnp.abs(x)))


def _bf16_split(x):
    hi = x.astype(BF16).astype(F32)
    lo = (x - hi).astype(BF16).astype(F32)
    return hi, lo


_C_AK = 0
_C_Z = _C_AK + A_KV_DIM
_C_XBC = _C_Z + SSD_D_INNER
_C_CK = _C_XBC + SSD_CONV_DIM
_C_DT = _C_CK + DIFF_HEADS * LANES
_C_END = _C_DT + LANES
_R_AQ = 0
_R_AV = _R_AQ + A_Q_DIM
_R_CQ = _R_AV + A_KV_DIM
_R_CV = _R_CQ + DIFF_QK_WIDTH
_R_END = _R_CV + DIFF_V_WIDTH


def _inproj_kernel(x_ref, xp_ref, xn_ref, nw_ref, w_ref, wt_ref, cw_ref, cb_ref,
                   ak_ref, z_ref, xs_ref, cm_ref, ck_ref, dt_ref, aqt_ref, avt_ref, cqt_ref, cvt_ref, bt_ref,
                   *, tiles_per_seq, diff_key_tile):
    i = pl.program_id(0)
    tm = x_ref.shape[0]
    nw = nw_ref[...]
    h = _rms(x_ref[...], nw).astype(BF16)

    def seg(lo, hi):
        return jnp.dot(h, w_ref[:, lo:hi], preferred_element_type=F32)

    ak_ref[...] = seg(_C_AK, _C_Z).astype(BF16)
    z_ref[...] = seg(_C_Z, _C_XBC)
    pos = (i * tm + lax.broadcasted_iota(I32, (tm, _C_DT - _C_CK), 0)) % diff_key_tile
    ck_ref[...] = (seg(_C_CK, _C_DT) + _diff_key_features(pos)).astype(BF16)
    dt_ref[...] = seg(_C_DT, _C_END)

    w_xbc = w_ref[:, _C_XBC:_C_CK]
    first = i % tiles_per_seq == 0
    last = i % tiles_per_seq == tiles_per_seq - 1
    prev = jnp.dot(_rms(xp_ref[...], nw).astype(BF16), w_xbc, preferred_element_type=F32)
    nxt = jnp.dot(_rms(xn_ref[...], nw).astype(BF16), w_xbc, preferred_element_type=F32)
    prev = jnp.where(first, 0.0, prev)
    nxt = jnp.where(last, 0.0, nxt)
    ext = jnp.concatenate([prev, seg(_C_XBC, _C_CK), nxt], axis=0)
    half = SSD_CONV // 2
    conv = cb_ref[...]
    for k in range(SSD_CONV):
        off = SUBLANES - half + k
        conv = conv + cw_ref[k:k + 1, :] * ext[off:off + tm, :]
    u = _silu(conv)
    xs_ref[...] = u[:, :SSD_D_INNER]
    bt_ref[...] = jnp.transpose(u[:, SSD_D_INNER:SSD_D_INNER + SSD_BC_DIM])
    cm_ref[...] = u[:, SSD_D_INNER + SSD_BC_DIM:]

    tr = lax.dot_general(wt_ref[...], h, (((1,), (1,)), ((), ())), preferred_element_type=F32)
    aqt_ref[...] = (tr[_R_AQ:_R_AV] * (HEAD_DIM ** -0.5)).astype(BF16)
    avt = tr[_R_AV:_R_CQ].astype(BF16)
    for c in range(tm // LANES):
        avt_ref[c] = avt[:, c * LANES:(c + 1) * LANES]
    cqt_ref[...] = (tr[_R_CQ:_R_CV] * (DIFF_QK_DIM ** -0.5 * LOG2E)).astype(BF16)
    pad = DIFF_VROWS - DIFF_V_DIM
    ones_row = (lax.broadcasted_iota(I32, (pad, tm), 0) == 0).astype(BF16)
    for hh in range(DIFF_HEADS):
        cvt_ref[hh * DIFF_VROWS:hh * DIFF_VROWS + DIFF_V_DIM, :] = (
            tr[_R_CV + hh * DIFF_V_DIM:_R_CV + (hh + 1) * DIFF_V_DIM].astype(BF16))
        cvt_ref[hh * DIFF_VROWS + DIFF_V_DIM:(hh + 1) * DIFF_VROWS, :] = ones_row


def _inproj(x2, norm_w, w_main, w_t, conv_w, conv_b, tm, seq, diff_key_tile):
    t, d = x2.shape
    hb = tm // SUBLANES
    n_hblk = t // SUBLANES
    row = lambda i: (i, 0)
    col = lambda i: (0, i)
    full = lambda i: (0, 0)
    row_outs = [(A_KV_DIM, BF16), (SSD_D_INNER, F32), (SSD_D_INNER, F32), (SSD_BC_DIM, F32),
                (DIFF_HEADS * LANES, BF16), (LANES, F32)]
    out_shape = [jax.ShapeDtypeStruct((t, w), dt) for w, dt in row_outs]
    out_specs = [pl.BlockSpec((tm, w), row) for w, _ in row_outs]
    out_shape += [jax.ShapeDtypeStruct((A_Q_DIM, t), BF16),
                  jax.ShapeDtypeStruct((t // LANES, A_KV_DIM, LANES), BF16),
                  jax.ShapeDtypeStruct((DIFF_QK_WIDTH, t), BF16),
                  jax.ShapeDtypeStruct((DIFF_HEADS * DIFF_VROWS, t), BF16),
                  jax.ShapeDtypeStruct((SSD_BC_DIM, t), F32)]
    out_specs += [pl.BlockSpec((A_Q_DIM, tm), col),
                  pl.BlockSpec((tm // LANES, A_KV_DIM, LANES), lambda i: (i, 0, 0)),
                  pl.BlockSpec((DIFF_QK_WIDTH, tm), col),
                  pl.BlockSpec((DIFF_HEADS * DIFF_VROWS, tm), col),
                  pl.BlockSpec((SSD_BC_DIM, tm), col)]
    return pl.pallas_call(
        functools.partial(_inproj_kernel, tiles_per_seq=seq // tm, diff_key_tile=diff_key_tile),
        grid=(t // tm,),
        in_specs=[pl.BlockSpec((tm, d), row),
                  pl.BlockSpec((SUBLANES, d), lambda i: (jnp.maximum(i * hb - 1, 0), 0)),
                  pl.BlockSpec((SUBLANES, d), lambda i: (jnp.minimum((i + 1) * hb, n_hblk - 1), 0)),
                  pl.BlockSpec((1, d), full),
                  pl.BlockSpec(w_main.shape, full), pl.BlockSpec(w_t.shape, full),
                  pl.BlockSpec(conv_w.shape, full), pl.BlockSpec(conv_b.shape, full)],
        out_specs=out_specs,
        out_shape=out_shape,
        compiler_params=_cparams("parallel"),
        name="inproj",
    )(x2, x2, x2, norm_w, w_main, w_t, conv_w, conv_b)


def _swa_kernel(sink_ref, slope_ref, qt_ref, k_ref, vt_ref, o_ref):
    step = pl.program_id(1)
    s_len = k_ref.shape[0]
    blk = SWA_WINDOW
    band = 3 * blk
    nb = s_len // blk
    rep = SWA_HEADS // SWA_KV_HEADS
    hd = HEAD_DIM
    for u in range(SWA_BLOCKS_PER_STEP):
        n = step * SWA_BLOCKS_PER_STEP + u
        start_blk = jnp.clip(n - 1, 0, nb - 3)
        start = pl.multiple_of(start_blk * blk, blk)
        kb = k_ref[pl.ds(start, band), :]
        v3 = vt_ref[pl.ds(start_blk, 3)]
        vtb = jnp.concatenate([v3[0], v3[1], v3[2]], axis=1)
        qt = qt_ref[:, u * blk:(u + 1) * blk]
        zero = jnp.zeros((hd, rep * blk), BF16)
        grp = [jnp.concatenate([qt[(g * rep + r) * hd:(g * rep + r + 1) * hd] for r in range(rep)], axis=1)
               for g in range(SWA_KV_HEADS)]
        qbd = jnp.concatenate([jnp.concatenate([grp[0], zero], axis=1),
                               jnp.concatenate([zero, grp[1]], axis=1)], axis=0)
        st = jnp.dot(kb, qbd, preferred_element_type=F32)
        kpos = start + lax.broadcasted_iota(I32, (band, blk), 0)
        qpos = n * blk + lax.broadcasted_iota(I32, (band, blk), 1)
        dist_i = jnp.abs(qpos - kpos)
        valid = dist_i <= SWA_WINDOW
        dist = dist_i.astype(F32)
        ps, inv = [], []
        for h in range(SWA_HEADS):
            s = jnp.where(valid, st[:, h * blk:(h + 1) * blk] - slope_ref[h] * dist, NEG)
            sink = sink_ref[h]
            m = jnp.maximum(jnp.max(s, axis=0, keepdims=True), sink)
            p = jnp.exp(s - m)
            inv.append(1.0 / (jnp.sum(p, axis=0, keepdims=True) + jnp.exp(sink - m)))
            ps.append(p.astype(BF16))
        outs = []
        for g in range(SWA_KV_HEADS):
            pg = jnp.concatenate(ps[g * rep:(g + 1) * rep], axis=1)
            og = jnp.dot(vtb[g * hd:(g + 1) * hd, :], pg, preferred_element_type=F32)
            for r in range(rep):
                outs.append(og[:, r * blk:(r + 1) * blk] * inv[g * rep + r])
        o_ref[u * blk:(u + 1) * blk, :] = jnp.transpose(jnp.concatenate(outs, axis=0)).astype(o_ref.dtype)


def _swa(aqt, ak, avt3, sink, slopes, batch, seq):
    blk = SWA_WINDOW
    rows = blk * SWA_BLOCKS_PER_STEP
    steps = seq // rows
    nb = seq // blk
    t = batch * seq
    smem = pl.BlockSpec(memory_space=pltpu.SMEM)
    return pl.pallas_call(
        _swa_kernel,
        grid=(batch, steps),
        in_specs=[smem, smem,
                  pl.BlockSpec((A_Q_DIM, rows), lambda b, s: (0, b * steps + s)),
                  pl.BlockSpec((seq, A_KV_DIM), lambda b, s: (b, 0)),
                  pl.BlockSpec((nb, A_KV_DIM, blk), lambda b, s: (b, 0, 0))],
        out_specs=pl.BlockSpec((rows, A_Q_DIM), lambda b, s: (b * steps + s, 0)),
        out_shape=jax.ShapeDtypeStruct((t, A_Q_DIM), BF16),
        compiler_params=_cparams("parallel", "parallel"),
        name="swa",
    )(sink, slopes, aqt, ak, avt3)


def _diff_key_tile(i, j, tq, tk, nk):
    return ((i * tq) // tk + j) % nk


def _diff_key_features(pos_in_tile):
    lane = lax.broadcasted_iota(I32, pos_in_tile.shape, 1) % LANES - 2 * DIFF_QK_DIM
    coarse = ((pos_in_tile // 16) * 16).astype(F32)
    fine = (pos_in_tile % 16).astype(F32)
    f = lane % DIFF_NFEAT
    feat = jnp.where(f < 2, coarse, jnp.where(f < 4, fine, 1.0))
    return jnp.where((lane >= 0) & (lane < 2 * DIFF_NFEAT), feat, 0.0)


def _diff_kernel(slope_ref, qt_ref, k_ref, vt_ref, lam_ref, sw_ref, o_ref, qtb_ref, m_ref, acc_ref,
                 s_ref, p_ref, mx_ref, *, lambda_init, n_key_tiles):
    i = pl.program_id(1)
    j = pl.program_id(2)
    tq = qt_ref.shape[1]
    tk = k_ref.shape[0]
    dq = DIFF_QK_DIM
    hw = 2 * dq
    nf = DIFF_NFEAT

    @pl.when(j == 0)
    def _():
        m_ref[...] = jnp.full(m_ref.shape, NEG, F32)
        acc_ref[...] = jnp.zeros(acc_ref.shape, F32)
        ii = lax.broadcasted_iota(I32, (1, 2 * tq), 1)
        ii = jnp.where(ii >= tq, ii - tq, ii).astype(F32)
        qt = qt_ref[...]
        col = lax.broadcasted_iota(I32, (hw, 2 * tq), 1)
        row = lax.broadcasted_iota(I32, (hw, 2 * tq), 0)
        own_map = row // dq == col // tq
        for h in range(DIFF_HEADS):
            qh = qt[h * hw:(h + 1) * hw, :]
            qh2 = jnp.where(own_map, jnp.concatenate([qh, qh], axis=1), jnp.zeros((hw, 2 * tq), BF16))
            sl = jnp.full((1, 2 * tq), slope_ref[h], F32)
            s_hi, s_lo = _bf16_split(sl)
            v_hi, v_lo = _bf16_split(-sl * ii)
            rows = jnp.concatenate([s_hi, s_lo, s_hi, s_lo, v_hi, v_lo], axis=0)
            zrow = jnp.zeros((nf, 2 * tq), F32)
            zero = jnp.zeros((hw - 2 * nf, 2 * tq), F32)
            variants = ([rows, zrow], [zrow, -rows], [zrow, zrow])
            for v, pieces in enumerate(variants):
                qtb_ref[v, h, 0:hw, :] = qh2
                qtb_ref[v, h, hw:2 * hw, :] = jnp.concatenate(pieces + [zero], axis=0).astype(BF16)

    q0 = i * tq

    def key_start(step):
        return _diff_key_tile(i, step, tq, tk, n_key_tiles) * tk

    def score_head(slot, h, variant, dist):
        s = jnp.dot(k_ref[:, h * LANES:(h + 1) * LANES], qtb_ref[variant, h], preferred_element_type=F32)
        if dist is not None:
            s = s - slope_ref[h] * dist
        s_ref[slot, h] = s
        mx_ref[slot, h:h + 1, :] = jnp.max(s, axis=0, keepdims=True)

    def softmax_head(slot, h, shift):
        m_old = m_ref[h:h + 1, :]
        m_new = jnp.maximum(m_old, mx_ref[slot, h:h + 1, :] + shift)
        p_ref[h] = jnp.exp2(s_ref[slot, h] - (m_new - shift)).astype(BF16)
        m_ref[h:h + 1, :] = m_new
        return jnp.exp2(m_old - m_new)

    def value_head(h, alpha):
        pv = jnp.dot(vt_ref[h * DIFF_VROWS:(h + 1) * DIFF_VROWS, :], p_ref[h], preferred_element_type=F32)
        acc_ref[h] = alpha * acc_ref[h] + pv

    def tile_shift(step):
        k0 = key_start(step)
        sign = jnp.where(step == 0, 0.0, jnp.where(k0 < q0, 1.0, -1.0))
        return sign * (k0 - q0).astype(F32)

    @pl.when(j == 0)
    def _():
        kpos = key_start(0) + lax.broadcasted_iota(I32, (tk, 2 * tq), 0)
        qpos = q0 + lax.broadcasted_iota(I32, (tk, 2 * tq), 1) % tq
        dist = jnp.abs(qpos - kpos).astype(F32)
        for h in range(DIFF_HEADS):
            score_head(0, h, 2, dist)

    for parity in range(2):
        @pl.when((j > 0) & (j < n_key_tiles) & (j % 2 == parity))
        def _():
            variant = jnp.where(key_start(j) < q0, 0, 1)
            shift = tile_shift(j - 1)
            alphas = []
            for h in range(DIFF_HEADS):
                score_head(parity, h, variant, None)
                alphas.append(softmax_head(1 - parity, h, slope_ref[h] * shift))
            for h in range(DIFF_HEADS):
                value_head(h, alphas[h])

    @pl.when(j == n_key_tiles)
    def _():
        shift = tile_shift(n_key_tiles - 1)
        slot = (n_key_tiles - 1) % 2
        alphas = [softmax_head(slot, h, slope_ref[h] * shift) for h in range(DIFF_HEADS)]
        for h in range(DIFF_HEADS):
            value_head(h, alphas[h])
        lp = lam_ref[...]
        lam = (jnp.exp(jnp.sum(lp[0:1] * lp[1:2], axis=-1, keepdims=True))
               - jnp.exp(jnp.sum(lp[2:3] * lp[3:4], axis=-1, keepdims=True)) + lambda_init)
        outs = []
        for h in range(DIFF_HEADS):
            a = acc_ref[h]
            o = a[0:DIFF_V_DIM] / a[DIFF_V_DIM:DIFF_V_DIM + 1]
            o = o[:, 0:tq] - lam * o[:, tq:2 * tq]
            ms = jnp.mean(o * o, axis=0, keepdims=True)
            outs.append(o * lax.rsqrt(ms + NORM_EPS) * sw_ref[...] * (1.0 - lambda_init))
        o_ref[...] = jnp.transpose(jnp.concatenate(outs, axis=0)).astype(o_ref.dtype)


def _diff(cqt, ck, cvt, slopes, lam_params, subln_w_col, lambda_init, batch, seq, tq, tk):
    assert tk % tq == 0 and seq % tk == 0
    nq, nk = seq // tq, seq // tk
    t = batch * seq
    smem = pl.BlockSpec(memory_space=pltpu.SMEM)
    key_tile = functools.partial(_diff_key_tile, tq=tq, tk=tk, nk=nk)
    return pl.pallas_call(
        functools.partial(_diff_kernel, lambda_init=lambda_init, n_key_tiles=nk),
        grid=(batch, nq, nk + 1),
        in_specs=[smem,
                  pl.BlockSpec((DIFF_QK_WIDTH, tq), lambda b, i, j: (0, b * nq + i)),
                  pl.BlockSpec((tk, DIFF_HEADS * LANES),
                               lambda b, i, j: (b * nk + key_tile(i, jnp.minimum(j, nk - 1)), 0)),
                  pl.BlockSpec((DIFF_HEADS * DIFF_VROWS, tk),
                               lambda b, i, j: (0, b * nk + key_tile(i, jnp.maximum(j - 1, 0)))),
                  pl.BlockSpec(lam_params.shape, lambda b, i, j: (0, 0)),
                  pl.BlockSpec(subln_w_col.shape, lambda b, i, j: (0, 0))],
        out_specs=pl.BlockSpec((tq, DIFF_V_WIDTH), lambda b, i, j: (b * nq + i, 0)),
        out_shape=jax.ShapeDtypeStruct((t, DIFF_V_WIDTH), BF16),
        scratch_shapes=[pltpu.VMEM((3, DIFF_HEADS, LANES, 2 * tq), BF16),
                        pltpu.VMEM((DIFF_HEADS, 2 * tq), F32),
                        pltpu.VMEM((DIFF_HEADS, DIFF_VROWS, 2 * tq), F32),
                        pltpu.VMEM((2, DIFF_HEADS, tk, 2 * tq), F32),
                        pltpu.VMEM((DIFF_HEADS, tk, 2 * tq), BF16),
                        pltpu.VMEM((2, DIFF_HEADS, 2 * tq), F32)],
        compiler_params=_cparams("parallel", "parallel", "arbitrary"),
        name="diffattn",
    )(slopes, cqt, ck, cvt, lam_params, subln_w_col)


def _ssd_direction(fwd, off, xs_ref, bt_ref, cm_ref, dt_ref, dtb_ref, alog_ref, dsk_ref, y_ref, state_ref):
    q = SSD_CHUNK
    rows = slice(off, off + q)
    lane0 = 0 if fwd else SSD_HEADS
    dt_all = _softplus(dt_ref[rows, :] + dtb_ref[...])
    dta_all = dt_all * -jnp.exp(alog_ref[...])
    row = lax.broadcasted_iota(I32, (q, q), 0)
    col = lax.broadcasted_iota(I32, (q, q), 1)
    keep = (row >= col) if fwd else (row <= col)
    tri = keep.astype(BF16)
    part_hi = dta_all.astype(BF16)
    rest = dta_all - part_hi.astype(F32)
    part_mid = rest.astype(BF16)
    part_lo = (rest - part_mid.astype(F32)).astype(BF16)
    da_all = (jnp.dot(tri, part_hi, preferred_element_type=F32)
              + jnp.dot(tri, part_mid, preferred_element_type=F32)
              + jnp.dot(tri, part_lo, preferred_element_type=F32))
    da_all_t = jnp.transpose(da_all)
    dt_all_t = jnp.transpose(dt_all)
    tot_all = jnp.sum(dta_all, axis=0, keepdims=True)

    xs = xs_ref[rows, :]
    bmt = bt_ref[:, rows]
    cm = cm_ref[rows, :]
    rep = SSD_HEADS // SSD_GROUPS
    ns = SSD_STATE
    g_mats = [jnp.dot(cm[:, g * ns:(g + 1) * ns].astype(BF16), bmt[g * ns:(g + 1) * ns, :].astype(BF16),
                      preferred_element_type=F32) for g in range(SSD_GROUPS)]
    ys = []
    for h in range(SSD_HEADS):
        g = h // rep
        ln = lane0 + h
        dac = da_all[:, ln:ln + 1]
        dar = da_all_t[ln:ln + 1, :]
        dtr = dt_all_t[ln:ln + 1, :]
        tot = tot_all[:, ln:ln + 1]
        dac_b = jnp.broadcast_to(dac, (q, q))
        decay = jnp.exp(jnp.where(keep, dac_b - dar, NEG))
        xh = xs[:, h * SSD_HEAD_DIM:(h + 1) * SSD_HEAD_DIM]
        xhb = xh.astype(BF16)
        y = jnp.dot((g_mats[g] * decay * dtr).astype(BF16), xhb, preferred_element_type=F32)
        st = state_ref[h]
        c_in = (cm[:, g * ns:(g + 1) * ns] * jnp.exp(dac_b[:, 0:ns])).astype(BF16)
        y = y + jnp.dot(c_in, st.astype(BF16), preferred_element_type=F32)
        to_end = jnp.exp(tot - dar) * dtr
        b_out = (bmt[g * ns:(g + 1) * ns, :] * to_end).astype(BF16)
        state_ref[h] = st * jnp.exp(tot) + jnp.dot(b_out, xhb, preferred_element_type=F32)
        if fwd:
            y = y + dsk_ref[:, h:h + 1] * xh
        ys.append(y)
    y_ref[rows, :] = jnp.concatenate(ys, axis=-1)


def _ssd_kernel(xsf_ref, btf_ref, cmf_ref, dtf_ref, xsb_ref, btb_ref, cmb_ref, dtb_in_ref,
                dtbias_ref, alog_ref, dsk_ref, yf_ref, yb_ref, state_ref):
    @pl.when(pl.program_id(1) == 0)
    def _():
        state_ref[...] = jnp.zeros(state_ref.shape, F32)

    n = SSD_CHUNKS_PER_STEP
    for u in range(n):
        _ssd_direction(True, u * SSD_CHUNK, xsf_ref, btf_ref, cmf_ref, dtf_ref, dtbias_ref, alog_ref, dsk_ref,
                       yf_ref, state_ref.at[0])
        _ssd_direction(False, (n - 1 - u) * SSD_CHUNK, xsb_ref, btb_ref, cmb_ref, dtb_in_ref, dtbias_ref,
                       alog_ref, dsk_ref, yb_ref, state_ref.at[1])


def _ssd(xs, bt, cm, dt_raw, dt_bias, a_log, d_skip, batch, seq):
    q = SSD_CHUNK * SSD_CHUNKS_PER_STEP
    nc = seq // q
    t = batch * seq
    full = lambda b, c: (0, 0)
    fw = lambda b, c: b * nc + c
    bw = lambda b, c: b * nc + nc - 1 - c

    def specs(idx):
        return [pl.BlockSpec((q, SSD_D_INNER), lambda b, c: (idx(b, c), 0)),
                pl.BlockSpec((SSD_BC_DIM, q), lambda b, c: (0, idx(b, c))),
                pl.BlockSpec((q, SSD_BC_DIM), lambda b, c: (idx(b, c), 0)),
                pl.BlockSpec((q, LANES), lambda b, c: (idx(b, c), 0))]

    return pl.pallas_call(
        _ssd_kernel,
        grid=(batch, nc),
        in_specs=specs(fw) + specs(bw) + [pl.BlockSpec(dt_bias.shape, full), pl.BlockSpec(a_log.shape, full),
                                          pl.BlockSpec(d_skip.shape, full)],
        out_specs=[pl.BlockSpec((q, SSD_D_INNER), lambda b, c: (fw(b, c), 0)),
                   pl.BlockSpec((q, SSD_D_INNER), lambda b, c: (bw(b, c), 0))],
        out_shape=[jax.ShapeDtypeStruct((t, SSD_D_INNER), F32), jax.ShapeDtypeStruct((t, SSD_D_INNER), F32)],
        scratch_shapes=[pltpu.VMEM((2, SSD_HEADS, SSD_STATE, SSD_HEAD_DIM), F32)],
        compiler_params=_cparams("parallel", "arbitrary"),
        name="ssd",
    )(xs, bt, cm, dt_raw, xs, bt, cm, dt_raw, dt_bias, a_log, d_skip)


def _outproj_kernel(x_ref, ya_ref, yf_ref, yb_ref, z_ref, snw_ref, yc_ref, wo_ref, fnw_ref, wrt_ref, br_ref,
                    xn_ref, ri_ref, rf_ref, cnt_ref, tri_ref, carry_ref):
    step = pl.program_id(0)
    tm = x_ref.shape[0]

    @pl.when(step == 0)
    def _():
        carry_ref[...] = jnp.zeros(carry_ref.shape, F32)
        r = lax.broadcasted_iota(I32, (tm, tm), 0)
        cc = lax.broadcasted_iota(I32, (tm, tm), 1)
        tri_ref[...] = (r <= cc).astype(BF16)

    y = (yf_ref[...] + yb_ref[...]) * _silu(z_ref[...])
    yb = _rms(y, snw_ref[...]).astype(BF16)
    acc = jnp.dot(ya_ref[...], wo_ref[0:A_Q_DIM, :], preferred_element_type=F32)
    acc = acc + jnp.dot(yb, wo_ref[A_Q_DIM:A_Q_DIM + SSD_D_INNER, :], preferred_element_type=F32)
    acc = acc + jnp.dot(yc_ref[...], wo_ref[A_Q_DIM + SSD_D_INNER:, :], preferred_element_type=F32)
    xn = x_ref[...] + acc
    xn_ref[...] = xn

    h = _rms(xn, fnw_ref[...])
    h_hi = h.astype(BF16)
    h_lo = (h - h_hi.astype(F32)).astype(BF16)
    nt = (((1,), (1,)), ((), ()))
    logits = (lax.dot_general(wrt_ref[0], h_hi, nt, preferred_element_type=F32)
              + lax.dot_general(wrt_ref[0], h_lo, nt, preferred_element_type=F32)
              + lax.dot_general(wrt_ref[1], h_hi, nt, preferred_element_type=F32)) + br_ref[...]
    ne, epg, ng = N_EXPERTS, EXPERTS_PER_GROUP, N_EXPERT_GROUPS
    gl = logits[ne:ne + ng, :]
    gmax = jnp.max(gl, axis=0, keepdims=True)
    g_sel = jnp.full((1, tm), float(ng - 1), F32)
    for g in range(ng - 2, -1, -1):
        g_sel = jnp.where(gl[g:g + 1, :] == gmax, float(g), g_sel)
    g_gate = 1.0 / jnp.sum(jnp.exp(gl - gmax), axis=0, keepdims=True)
    e_in = logits[0:epg, :]
    for g in range(1, ng):
        e_in = jnp.where(g_sel == float(g), logits[g * epg:(g + 1) * epg, :], e_in)
    sub = lax.broadcasted_iota(I32, (epg, tm), 0).astype(F32)
    m1 = jnp.max(e_in, axis=0, keepdims=True)
    i1 = jnp.min(jnp.where(e_in == m1, sub, float(epg)), axis=0, keepdims=True)
    rest = jnp.where(sub == i1, NEG, e_in)
    m2 = jnp.max(rest, axis=0, keepdims=True)
    i2 = jnp.min(jnp.where(rest == m2, sub, float(epg)), axis=0, keepdims=True)
    r = jnp.exp(m2 - m1)
    c1 = g_gate / (1.0 + r)
    c2 = g_gate * r / (1.0 + r)
    e1 = (g_sel * epg + i1).astype(I32)
    e2 = (g_sel * epg + i2).astype(I32)

    erow = lax.broadcasted_iota(I32, (ne, tm), 0)
    hit1 = erow == e1
    hit2 = erow == e2
    oh = jnp.where(hit1 | hit2, 1.0, 0.0)
    incl = jnp.dot(oh.astype(BF16), tri_ref[...], preferred_element_type=F32)
    before = incl - oh + carry_ref[:, 0:1]
    rank1 = jnp.sum(jnp.where(hit1, before, 0.0), axis=0, keepdims=True)
    rank2 = jnp.sum(jnp.where(hit2, before, 0.0), axis=0, keepdims=True)
    carry_ref[...] = carry_ref[...] + jnp.sum(oh, axis=1, keepdims=True)
    cnt_ref[...] = carry_ref[...]
    zi = jnp.zeros((1, tm), I32)
    ri_ref[...] = jnp.concatenate([e1, e2, rank1.astype(I32), rank2.astype(I32), zi, zi, zi, zi], axis=0)
    zf = jnp.zeros((1, tm), F32)
    rf_ref[...] = jnp.concatenate([c1, c2, zf, zf, zf, zf, zf, zf], axis=0)


def _outproj(x2, ya, yf, yb, z, ssd_norm_w, yc, w_out, ffn_norm_w, wrt, br, tm):
    t, d = x2.shape
    row = lambda i: (i, 0)
    full = lambda i: (0, 0)
    return pl.pallas_call(
        _outproj_kernel,
        grid=(t // tm,),
        in_specs=[pl.BlockSpec((tm, d), row),
                  pl.BlockSpec((tm, A_Q_DIM), row),
                  pl.BlockSpec((tm, SSD_D_INNER), row),
                  pl.BlockSpec((tm, SSD_D_INNER), row),
                  pl.BlockSpec((tm, SSD_D_INNER), row),
                  pl.BlockSpec(ssd_norm_w.shape, full),
                  pl.BlockSpec((tm, DIFF_V_WIDTH), row),
                  pl.BlockSpec(w_out.shape, full),
                  pl.BlockSpec(ffn_norm_w.shape, full),
                  pl.BlockSpec(wrt.shape, lambda i: (0, 0, 0)),
                  pl.BlockSpec(br.shape, full)],
        out_specs=[pl.BlockSpec((tm, d), row),
                   pl.BlockSpec((SUBLANES, tm), lambda i: (0, i)),
                   pl.BlockSpec((SUBLANES, tm), lambda i: (0, i)),
                   pl.BlockSpec((N_EXPERTS, LANES), full)],
        out_shape=[jax.ShapeDtypeStruct((t, d), F32),
                   jax.ShapeDtypeStruct((SUBLANES, t), I32),
                   jax.ShapeDtypeStruct((SUBLANES, t), F32),
                   jax.ShapeDtypeStruct((N_EXPERTS, LANES), F32)],
        scratch_shapes=[pltpu.VMEM((tm, tm), BF16), pltpu.VMEM((N_EXPERTS, LANES), F32)],
        compiler_params=_cparams("arbitrary"),
        name="outproj_router",
    )(x2, ya, yf, yb, z, ssd_norm_w, yc, w_out, ffn_norm_w, wrt, br)


_PAD_PIECES = tuple(1 << b for b in reversed(range(MOE_ROW_TILE.bit_length() - 1)))


def _dispatch_kernel(slot1_ref, slot2_ref, pstart_ref, plen_ref, nused_ref, x_ref, xs_hbm, zero_ref, xbuf, sem):
    i = pl.program_id(0)
    tm = x_ref.shape[0]

    @pl.when(i == 0)
    def _():
        zero_ref[...] = jnp.zeros(zero_ref.shape, F32)

        def pieces(e, wait):
            n = plen_ref[e]
            first = pstart_ref[e]
            off = first + n
            for b in _PAD_PIECES:
                off = off - (n & b)
                dst = pl.ds(pl.multiple_of(off, b), b) if b >= SUBLANES else None
                if dst is not None:
                    @pl.when((n & b) != 0)
                    def _():
                        cp = pltpu.make_async_copy(zero_ref.at[pl.ds(0, b)], xs_hbm.at[dst], sem.at[2])
                        cp.wait() if wait else cp.start()

            for u in range(SUBLANES - 1):
                @pl.when(u < (n & (SUBLANES - 1)))
                def _():
                    cp = pltpu.make_async_copy(zero_ref.at[pl.ds(0, 1)], xs_hbm.at[pl.ds(first + u, 1)],
                                               sem.at[2])
                    cp.wait() if wait else cp.start()

        def tail(tile, wait):
            big = _PAD_PIECES[0]
            for part in range(MOE_ROW_TILE // big):
                dst = xs_hbm.at[pl.ds(pl.multiple_of(tile * MOE_ROW_TILE + part * big, big), big)]
                cp = pltpu.make_async_copy(zero_ref, dst, sem.at[2])
                cp.wait() if wait else cp.start()

        def loop(fn, lo, hi, wait):
            def body(k, carry):
                fn(k, wait)
                return carry

            lax.fori_loop(lo, hi, body, 0)

        n_tiles = xs_hbm.shape[0] // MOE_ROW_TILE
        for wait in (False, True):
            loop(pieces, 0, N_EXPERTS, wait)
            loop(tail, nused_ref[0], n_tiles, wait)

    base = i * tm
    last = pl.num_programs(0) - 1

    def drain(slot):
        for _ in range(2):
            pltpu.make_async_copy(xbuf.at[slot], xs_hbm.at[pl.ds(0, tm)], sem.at[slot]).wait()

    for parity in range(2):
        @pl.when(i % 2 == parity)
        def _():
            buf = xbuf.at[parity]
            buf[...] = x_ref[...]
            for r in range(tm):
                src = buf.at[pl.ds(r, 1)]
                pltpu.make_async_copy(src, xs_hbm.at[pl.ds(slot1_ref[base + r], 1)],
                                      sem.at[parity]).start(priority=0)
                pltpu.make_async_copy(src, xs_hbm.at[pl.ds(slot2_ref[base + r], 1)],
                                      sem.at[parity]).start(priority=1)

            @pl.when(i > 0)
            def _():
                drain(1 - parity)

            @pl.when(i == last)
            def _():
                drain(parity)


def _dispatch(xn, slot1, slot2, pad_start, pad_len, n_used, n_rows, tm):
    t, d = xn.shape
    grid_spec = pltpu.PrefetchScalarGridSpec(
        num_scalar_prefetch=5,
        grid=(t // tm,),
        in_specs=[pl.BlockSpec((tm, d), lambda i, s1, s2, ps, pn, nu: (i, 0))],
        out_specs=pl.BlockSpec(memory_space=pl.ANY),
        scratch_shapes=[pltpu.VMEM((_PAD_PIECES[0], d), F32), pltpu.VMEM((2, tm, d), F32),
                        pltpu.SemaphoreType.DMA((3,))],
    )
    return pl.pallas_call(
        _dispatch_kernel,
        grid_spec=grid_spec,
        out_shape=jax.ShapeDtypeStruct((n_rows, d), F32),
        compiler_params=_cparams("arbitrary"),
        name="moe_dispatch",
    )(slot1, slot2, pad_start, pad_len, n_used, xn)


def _moe_kernel(texp_ref, nused_ref, x_ref, fnw_ref, wg_ref, wu_ref, wd_ref, y_ref):
    i = pl.program_id(0)

    @pl.when(i < nused_ref[0])
    def _():
        h = _rms(x_ref[...], fnw_ref[...]).astype(BF16)
        hg = jnp.dot(h, wg_ref[...].astype(BF16), preferred_element_type=F32)
        hu = jnp.dot(h, wu_ref[...].astype(BF16), preferred_element_type=F32)
        act = (_silu(hg) * hu).astype(BF16)
        y_ref[...] = jnp.dot(act, wd_ref[...].astype(BF16), preferred_element_type=F32)

    @pl.when(i >= nused_ref[0])
    def _():
        y_ref[...] = jnp.zeros(y_ref.shape, F32)


def _moe(xs, ffn_norm_w, w_gate, w_up, w_down, layer, tile_expert, n_used):
    n_rows, d = xs.shape
    f = w_gate.shape[-1]
    tr = MOE_ROW_TILE

    def used(i, nu):
        return jnp.maximum(jnp.minimum(i, nu[0] - 1), 0)

    grid_spec = pltpu.PrefetchScalarGridSpec(
        num_scalar_prefetch=2,
        grid=(n_rows // tr,),
        in_specs=[pl.BlockSpec((tr, d), lambda i, te, nu: (used(i, nu), 0)),
                  pl.BlockSpec(ffn_norm_w.shape, lambda i, te, nu: (0, 0)),
                  pl.BlockSpec((None, None, d, f), lambda i, te, nu: (layer, te[used(i, nu)], 0, 0)),
                  pl.BlockSpec((None, None, d, f), lambda i, te, nu: (layer, te[used(i, nu)], 0, 0)),
                  pl.BlockSpec((None, None, f, d), lambda i, te, nu: (layer, te[used(i, nu)], 0, 0))],
        out_specs=pl.BlockSpec((tr, d), lambda i, te, nu: (i, 0)),
    )
    return pl.pallas_call(
        _moe_kernel,
        grid_spec=grid_spec,
        out_shape=jax.ShapeDtypeStruct((n_rows, d), F32),
        compiler_params=_cparams("arbitrary"),
        name="moe_experts",
    )(tile_expert, n_used, xs, ffn_norm_w, w_gate, w_up, w_down)


def _combine_kernel(slot1_ref, slot2_ref, x_ref, cw_ref, nw_ref, y_hbm, o_ref, ybuf, sem, *, final_norm):
    i = pl.program_id(0)
    n = pl.num_programs(0)
    tm = x_ref.shape[0]

    def start_gather(tile, slot):
        base = tile * tm
        for r in range(tm):
            pltpu.make_async_copy(y_hbm.at[pl.ds(slot1_ref[base + r], 1)], ybuf.at[slot, 0, pl.ds(r, 1)],
                                  sem.at[slot]).start(priority=0)
            pltpu.make_async_copy(y_hbm.at[pl.ds(slot2_ref[base + r], 1)], ybuf.at[slot, 1, pl.ds(r, 1)],
                                  sem.at[slot]).start(priority=1)

    def compute(slot):
        for k in range(2):
            pltpu.make_async_copy(y_hbm.at[pl.ds(0, tm)], ybuf.at[slot, k], sem.at[slot]).wait()
        cw = cw_ref[...]
        out = x_ref[...] + cw[:, 0:1] * ybuf[slot, 0] + cw[:, 1:2] * ybuf[slot, 1]
        if final_norm:
            out = _rms(out, nw_ref[...])
        o_ref[...] = out

    @pl.when(i == 0)
    def _():
        start_gather(0, 0)

    for parity in range(2):
        @pl.when(i % 2 == parity)
        def _():
            @pl.when(i + 1 < n)
            def _():
                start_gather(i + 1, 1 - parity)

            compute(parity)


def _combine(xn, cw, norm_w, y_sorted, slot1, slot2, tm, final_norm):
    t, d = xn.shape
    grid_spec = pltpu.PrefetchScalarGridSpec(
        num_scalar_prefetch=2,
        grid=(t // tm,),
        in_specs=[pl.BlockSpec((tm, d), lambda i, s1, s2: (i, 0)),
                  pl.BlockSpec((tm, cw.shape[1]), lambda i, s1, s2: (i, 0)),
                  pl.BlockSpec(norm_w.shape, lambda i, s1, s2: (0, 0)),
                  pl.BlockSpec(memory_space=pl.ANY)],
        out_specs=pl.BlockSpec((tm, d), lambda i, s1, s2: (i, 0)),
        scratch_shapes=[pltpu.VMEM((2, 2, tm, d), F32), pltpu.SemaphoreType.DMA((2,))],
    )
    return pl.pallas_call(
        functools.partial(_combine_kernel, final_norm=final_norm),
        grid_spec=grid_spec,
        out_shape=jax.ShapeDtypeStruct((t, d), F32),
        compiler_params=_cparams("arbitrary"),
        name="moe_combine",
    )(slot1, slot2, xn, cw, norm_w, y_sorted)


def _pad_lanes(v):
    v = v.reshape(1, -1).astype(F32)
    return jnp.pad(v, ((0, 0), (0, LANES - v.shape[1])))


def kernel(x, attn_norm_w, w_in, swa_sink, ssd_conv_w, ssd_conv_b, ssd_dt_bias, ssd_a_log, ssd_d, ssd_norm_w,
           diff_lambda, diff_subln_w, w_out, ffn_norm_w, w_router_group, b_router_group, w_router_expert,
           b_router_expert, w_gate, w_up, w_down, final_norm_w):
    return _forward(x, attn_norm_w, w_in, swa_sink, ssd_conv_w, ssd_conv_b, ssd_dt_bias, ssd_a_log, ssd_d,
                    ssd_norm_w, diff_lambda, diff_subln_w, w_out, ffn_norm_w, w_router_group, b_router_group,
                    w_router_expert, b_router_expert, w_gate, w_up, w_down, final_norm_w)


def _forward(x, attn_norm_w, w_in, swa_sink, ssd_conv_w, ssd_conv_b, ssd_dt_bias, ssd_a_log, ssd_d, ssd_norm_w,
             diff_lambda, diff_subln_w, w_out, ffn_norm_w, w_router_group, b_router_group, w_router_expert,
             b_router_expert, w_gate, w_up, w_down, final_norm_w, tm=512, tq=512, tk=1024, tmc=256):
    batch, seq, d = x.shape
    depth = w_in.shape[0]
    t = batch * seq
    tr = MOE_ROW_TILE
    n_tiles = (2 * t) // tr + N_EXPERTS
    slopes = jnp.exp2(-8.0 * jnp.arange(1, N_ALIBI_HEADS + 1, dtype=F32) / N_ALIBI_HEADS)
    swa_slopes, diff_slopes = slopes[:SWA_HEADS], slopes[SWA_HEADS:]

    sizes = [A_Q_DIM, A_KV_DIM, A_KV_DIM, SSD_D_INNER, SSD_CONV_DIM, SSD_DT_DIM, DIFF_QK_WIDTH, DIFF_QK_WIDTH,
             DIFF_V_WIDTH]
    offs = [0]
    for s in sizes:
        offs.append(offs[-1] + s)
    o_aq, o_ak, o_av, o_z, o_xbc, o_dt, o_cq, o_ck, o_cv, o_end = offs

    x2 = x.reshape(t, d)
    for l in range(depth):
        w = w_in[l]
        hw = 2 * DIFF_QK_DIM
        w_ck = jnp.pad(w[:, o_ck:o_cv].reshape(d, DIFF_HEADS, hw), ((0, 0), (0, 0), (0, LANES - hw)))
        w_main = jnp.concatenate(
            [w[:, o_ak:o_av], w[:, o_z:o_dt], w_ck.reshape(d, DIFF_HEADS * LANES), w[:, o_dt:o_cq],
             jnp.zeros((d, LANES - SSD_DT_DIM), w.dtype)], axis=1).astype(BF16)
        w_t = jnp.concatenate([w[:, o_aq:o_ak], w[:, o_av:o_z], w[:, o_cq:o_ck], w[:, o_cv:o_end]],
                              axis=1).T.astype(BF16)
        ak, z, xs, cm, ck, dt_raw, aqt, avt3, cqt, cvt, bt = _inproj(
            x2, attn_norm_w[l].reshape(1, d), w_main, w_t, ssd_conv_w[l].astype(F32),
            ssd_conv_b[l].reshape(1, -1).astype(F32), tm, seq, tk)

        ya = _swa(aqt, ak, avt3, swa_sink[l].astype(F32), swa_slopes, batch, seq)
        lambda_init = 0.8 - 0.6 * math.exp(-0.3 * l)
        yc = _diff(cqt, ck, cvt, diff_slopes * LOG2E, diff_lambda[l].astype(F32),
                   diff_subln_w[l].reshape(DIFF_V_DIM, 1).astype(F32), lambda_init, batch, seq, tq, tk)
        yf, yb = _ssd(xs, bt, cm, dt_raw, _pad_lanes(ssd_dt_bias[l]), _pad_lanes(ssd_a_log[l]),
                      _pad_lanes(ssd_d[l]), batch, seq)

        wr32 = jnp.concatenate([w_router_expert[l], w_router_group[l],
                                jnp.zeros((d, SUBLANES - N_EXPERT_GROUPS), F32)], axis=1).T.astype(F32)
        wr_hi = wr32.astype(BF16)
        wrt = jnp.stack([wr_hi, (wr32 - wr_hi.astype(F32)).astype(BF16)])
        br = jnp.concatenate([b_router_expert[l], b_router_group[l],
                              jnp.zeros((SUBLANES - N_EXPERT_GROUPS,), F32)]).reshape(-1, 1).astype(F32)
        xn, ri, rf, cnt = _outproj(x2, ya, yf, yb, z, ssd_norm_w[l].reshape(1, -1), yc, w_out[l].astype(BF16),
                                   ffn_norm_w[l].reshape(1, d), wrt, br, tm)

        counts = cnt[:, 0].astype(I32)
        padded = ((counts + tr - 1) // tr) * tr
        ends = jnp.cumsum(padded)
        starts = ends - padded
        experts = jnp.arange(N_EXPERTS, dtype=I32)[:, None]

        def slot_of(e, rank):
            return jnp.sum(jnp.where(e[None, :] == experts, starts[:, None], 0), axis=0) + rank

        slot1 = slot_of(ri[0], ri[2])
        slot2 = slot_of(ri[1], ri[3])
        tile_start = jnp.arange(n_tiles, dtype=I32) * tr
        tile_expert = jnp.minimum(jnp.sum(ends[None, :] <= tile_start[:, None], axis=1), N_EXPERTS - 1).astype(I32)
        n_used = (ends[-1] // tr).astype(I32).reshape(1)

        xs_sorted = _dispatch(xn, slot1, slot2, starts + counts, padded - counts, n_used, n_tiles * tr, tmc)
        y_sorted = _moe(xs_sorted, ffn_norm_w[l].reshape(1, d), w_gate, w_up, w_down, l, tile_expert, n_used)
        last = l == depth - 1
        x2 = _combine(xn, rf.T, final_norm_w.reshape(1, d), y_sorted, slot1, slot2, tmc, last)
    return x2.reshape(batch, seq, d)
```

```python
import functools
import math

import jax
import jax.numpy as jnp
from jax import lax
from jax.experimental import pallas as pl
from jax.experimental.pallas import tpu as pltpu

F32 = jnp.float32
BF16 = jnp.bfloat16
I32 = jnp.int32

HEAD_DIM = 64
SWA_HEADS = 6
SWA_KV_HEADS = 2
SWA_WINDOW = 128
SSD_HEADS = 6
SSD_HEAD_DIM = 64
SSD_GROUPS = 2
SSD_STATE = 64
SSD_CONV = 5
DIFF_HEADS = 4
DIFF_QK_DIM = 32
DIFF_V_DIM = 64
N_EXPERT_GROUPS = 4
EXPERTS_PER_GROUP = 8
N_EXPERTS = N_EXPERT_GROUPS * EXPERTS_PER_GROUP
NORM_EPS = 1e-6

A_Q_DIM = SWA_HEADS * HEAD_DIM
A_KV_DIM = SWA_KV_HEADS * HEAD_DIM
SSD_D_INNER = SSD_HEADS * SSD_HEAD_DIM
SSD_BC_DIM = SSD_GROUPS * SSD_STATE
SSD_CONV_DIM = SSD_D_INNER + 2 * SSD_BC_DIM
SSD_DT_DIM = 2 * SSD_HEADS
DIFF_QK_WIDTH = DIFF_HEADS * 2 * DIFF_QK_DIM
DIFF_V_WIDTH = DIFF_HEADS * DIFF_V_DIM
N_ALIBI_HEADS = SWA_HEADS + DIFF_HEADS

LANES = 128
SUBLANES = 8
VMEM_LIMIT = 56 * 1024 * 1024
NEG = -1e30
LOG2E = math.log2(math.e)

SSD_CHUNK = 128
SSD_CHUNKS_PER_STEP = 4
SWA_BLOCKS_PER_STEP = 4
MOE_ROW_TILE = 256
DIFF_PAIR = 2 * 2 * DIFF_QK_DIM
DIFF_VROWS = 80
DIFF_NFEAT = 6


def _cparams(*sem):
    return pltpu.CompilerParams(dimension_semantics=sem, vmem_limit_bytes=VMEM_LIMIT)


def _rms(x, w):
    return x * lax.rsqrt(jnp.mean(x * x, axis=-1, keepdims=True) + NORM_EPS) * w


def _silu(x):
    return x / (1.0 + jnp.exp(-x))


def _softplus(x):
    return jnp.maximum(x, 0.0) + jnp.log(1.0 + jnp.exp(-jnp.abs(x)))


def _bf16_split(x):
    hi = x.astype(BF16).astype(F32)
    lo = (x - hi).astype(BF16).astype(F32)
    return hi, lo


_C_AK = 0
_C_Z = _C_AK + A_KV_DIM
_C_XBC = _C_Z + SSD_D_INNER
_C_CK = _C_XBC + SSD_CONV_DIM
_C_DT = _C_CK + DIFF_HEADS * LANES
_C_END = _C_DT + LANES
_R_AQ = 0
_R_AV = _R_AQ + A_Q_DIM
_R_CQ = _R_AV + A_KV_DIM
_R_CV = _R_CQ + DIFF_QK_WIDTH
_R_END = _R_CV + DIFF_V_WIDTH


def _inproj_kernel(x_ref, xp_ref, xn_ref, nw_ref, w_ref, wt_ref, cw_ref, cb_ref,
                   ak_ref, z_ref, xs_ref, cm_ref, ck_ref, dt_ref, aqt_ref, avt_ref, cqt_ref, cvt_ref, bt_ref,
                   *, tiles_per_seq, diff_key_tile):
    i = pl.program_id(0)
    tm = x_ref.shape[0]
    nw = nw_ref[...]
    h = _rms(x_ref[...], nw).astype(BF16)

    def seg(lo, hi):
        return jnp.dot(h, w_ref[:, lo:hi], preferred_element_type=F32)

    ak_ref[...] = seg(_C_AK, _C_Z).astype(BF16)
    z_ref[...] = seg(_C_Z, _C_XBC)
    pos = (i * tm + lax.broadcasted_iota(I32, (tm, _C_DT - _C_CK), 0)) % diff_key_tile
    ck_ref[...] = (seg(_C_CK, _C_DT) + _diff_key_features(pos)).astype(BF16)
    dt_ref[...] = seg(_C_DT, _C_END)

    w_xbc = w_ref[:, _C_XBC:_C_CK]
    first = i % tiles_per_seq == 0
    last = i % tiles_per_seq == tiles_per_seq - 1
    prev = jnp.dot(_rms(xp_ref[...], nw).astype(BF16), w_xbc, preferred_element_type=F32)
    nxt = jnp.dot(_rms(xn_ref[...], nw).astype(BF16), w_xbc, preferred_element_type=F32)
    prev = jnp.where(first, 0.0, prev)
    nxt = jnp.where(last, 0.0, nxt)
    ext = jnp.concatenate([prev, seg(_C_XBC, _C_CK), nxt], axis=0)
    half = SSD_CONV // 2
    conv = cb_ref[...]
    for k in range(SSD_CONV):
        off = SUBLANES - half + k
        conv = conv + cw_ref[k:k + 1, :] * ext[off:off + tm, :]
    u = _silu(conv)
    xs_ref[...] = u[:, :SSD_D_INNER]
    bt_ref[...] = jnp.transpose(u[:, SSD_D_INNER:SSD_D_INNER + SSD_BC_DIM])
    cm_ref[...] = u[:, SSD_D_INNER + SSD_BC_DIM:]

    tr = lax.dot_general(wt_ref[...], h, (((1,), (1,)), ((), ())), preferred_element_type=F32)
    aqt_ref[...] = (tr[_R_AQ:_R_AV] * (HEAD_DIM ** -0.5)).astype(BF16)
    avt = tr[_R_AV:_R_CQ].astype(BF16)
    for c in range(tm // LANES):
        avt_ref[c] = avt[:, c * LANES:(c + 1) * LANES]
    cqt_ref[...] = (tr[_R_CQ:_R_CV] * (DIFF_QK_DIM ** -0.5 * LOG2E)).astype(BF16)
    pad = DIFF_VROWS - DIFF_V_DIM
    ones_row = (lax.broadcasted_iota(I32, (pad, tm), 0) == 0).astype(BF16)
    for hh in range(DIFF_HEADS):
        cvt_ref[hh * DIFF_VROWS:hh * DIFF_VROWS + DIFF_V_DIM, :] = (
            tr[_R_CV + hh * DIFF_V_DIM:_R_CV + (hh + 1) * DIFF_V_DIM].astype(BF16))
        cvt_ref[hh * DIFF_VROWS + DIFF_V_DIM:(hh + 1) * DIFF_VROWS, :] = ones_row


def _inproj(x2, norm_w, w_main, w_t, conv_w, conv_b, tm, seq, diff_key_tile):
    t, d = x2.shape
    hb = tm // SUBLANES
    n_hblk = t // SUBLANES
    row = lambda i: (i, 0)
    col = lambda i: (0, i)
    full = lambda i: (0, 0)
    row_outs = [(A_KV_DIM, BF16), (SSD_D_INNER, F32), (SSD_D_INNER, F32), (SSD_BC_DIM, F32),
                (DIFF_HEADS * LANES, BF16), (LANES, F32)]
    out_shape = [jax.ShapeDtypeStruct((t, w), dt) for w, dt in row_outs]
    out_specs = [pl.BlockSpec((tm, w), row) for w, _ in row_outs]
    out_shape += [jax.ShapeDtypeStruct((A_Q_DIM, t), BF16),
                  jax.ShapeDtypeStruct((t // LANES, A_KV_DIM, LANES), BF16),
                  jax.ShapeDtypeStruct((DIFF_QK_WIDTH, t), BF16),
                  jax.ShapeDtypeStruct((DIFF_HEADS * DIFF_VROWS, t), BF16),
                  jax.ShapeDtypeStruct((SSD_BC_DIM, t), F32)]
    out_specs += [pl.BlockSpec((A_Q_DIM, tm), col),
                  pl.BlockSpec((tm // LANES, A_KV_DIM, LANES), lambda i: (i, 0, 0)),
                  pl.BlockSpec((DIFF_QK_WIDTH, tm), col),
                  pl.BlockSpec((DIFF_HEADS * DIFF_VROWS, tm), col),
                  pl.BlockSpec((SSD_BC_DIM, tm), col)]
    return pl.pallas_call(
        functools.partial(_inproj_kernel, tiles_per_seq=seq // tm, diff_key_tile=diff_key_tile),
        grid=(t // tm,),
        in_specs=[pl.BlockSpec((tm, d), row),
                  pl.BlockSpec((SUBLANES, d), lambda i: (jnp.maximum(i * hb - 1, 0), 0)),
                  pl.BlockSpec((SUBLANES, d), lambda i: (jnp.minimum((i + 1) * hb, n_hblk - 1), 0)),
                  pl.BlockSpec((1, d), full),
                  pl.BlockSpec(w_main.shape, full), pl.BlockSpec(w_t.shape, full),
                  pl.BlockSpec(conv_w.shape, full), pl.BlockSpec(conv_b.shape, full)],
        out_specs=out_specs,
        out_shape=out_shape,
        compiler_params=_cparams("parallel"),
        name="inproj",
    )(x2, x2, x2, norm_w, w_main, w_t, conv_w, conv_b)


def _swa_kernel(sink_ref, slope_ref, qt_ref, k_ref, vt_ref, o_ref):
    step = pl.program_id(1)
    s_len = k_ref.shape[0]
    blk = SWA_WINDOW
    band = 3 * blk
    nb = s_len // blk
    rep = SWA_HEADS // SWA_KV_HEADS
    hd = HEAD_DIM
    for u in range(SWA_BLOCKS_PER_STEP):
        n = step * SWA_BLOCKS_PER_STEP + u
        start_blk = jnp.clip(n - 1, 0, nb - 3)
        start = pl.multiple_of(start_blk * blk, blk)
        kb = k_ref[pl.ds(start, band), :]
        v3 = vt_ref[pl.ds(start_blk, 3)]
        vtb = jnp.concatenate([v3[0], v3[1], v3[2]], axis=1)
        qt = qt_ref[:, u * blk:(u + 1) * blk]
        zero = jnp.zeros((hd, rep * blk), BF16)
        grp = [jnp.concatenate([qt[(g * rep + r) * hd:(g * rep + r + 1) * hd] for r in range(rep)], axis=1)
               for g in range(SWA_KV_HEADS)]
        qbd = jnp.concatenate([jnp.concatenate([grp[0], zero], axis=1),
                               jnp.concatenate([zero, grp[1]], axis=1)], axis=0)
        st = jnp.dot(kb, qbd, preferred_element_type=F32)
        kpos = start + lax.broadcasted_iota(I32, (band, blk), 0)
        qpos = n * blk + lax.broadcasted_iota(I32, (band, blk), 1)
        dist_i = jnp.abs(qpos - kpos)
        valid = dist_i <= SWA_WINDOW
        dist = dist_i.astype(F32)
        ps, inv = [], []
        for h in range(SWA_HEADS):
            s = jnp.where(valid, st[:, h * blk:(h + 1) * blk] - slope_ref[h] * dist, NEG)
            sink = sink_ref[h]
            m = jnp.maximum(jnp.max(s, axis=0, keepdims=True), sink)
            p = jnp.exp(s - m)
            inv.append(1.0 / (jnp.sum(p, axis=0, keepdims=True) + jnp.exp(sink - m)))
            ps.append(p.astype(BF16))
        outs = []
        for g in range(SWA_KV_HEADS):
            pg = jnp.concatenate(ps[g * rep:(g + 1) * rep], axis=1)
            og = jnp.dot(vtb[g * hd:(g + 1) * hd, :], pg, preferred_element_type=F32)
            for r in range(rep):
                outs.append(og[:, r * blk:(r + 1) * blk] * inv[g * rep + r])
        o_ref[u * blk:(u + 1) * blk, :] = jnp.transpose(jnp.concatenate(outs, axis=0)).astype(o_ref.dtype)


def _swa(aqt, ak, avt3, sink, slopes, batch, seq):
    blk = SWA_WINDOW
    rows = blk * SWA_BLOCKS_PER_STEP
    steps = seq // rows
    nb = seq // blk
    t = batch * seq
    smem = pl.BlockSpec(memory_space=pltpu.SMEM)
    return pl.pallas_call(
        _swa_kernel,
        grid=(batch, steps),
        in_specs=[smem, smem,
                  pl.BlockSpec((A_Q_DIM, rows), lambda b, s: (0, b * steps + s)),
                  pl.BlockSpec((seq, A_KV_DIM), lambda b, s: (b, 0)),
                  pl.BlockSpec((nb, A_KV_DIM, blk), lambda b, s: (b, 0, 0))],
        out_specs=pl.BlockSpec((rows, A_Q_DIM), lambda b, s: (b * steps + s, 0)),
        out_shape=jax.ShapeDtypeStruct((t, A_Q_DIM), BF16),
        compiler_params=_cparams("parallel", "parallel"),
        name="swa",
    )(sink, slopes, aqt, ak, avt3)


def _diff_key_tile(i, j, tq, tk, nk):
    return ((i * tq) // tk + j) % nk


def _diff_key_features(pos_in_tile):
    lane = lax.broadcasted_iota(I32, pos_in_tile.shape, 1) % LANES - 2 * DIFF_QK_DIM
    coarse = ((pos_in_tile // 16) * 16).astype(F32)
    fine = (pos_in_tile % 16).astype(F32)
    f = lane % DIFF_NFEAT
    feat = jnp.where(f < 2, coarse, jnp.where(f < 4, fine, 1.0))
    return jnp.where((lane >= 0) & (lane < 2 * DIFF_NFEAT), feat, 0.0)


def _diff_kernel(slope_ref, qt_ref, k_ref, vt_ref, lam_ref, sw_ref, o_ref, qtb_ref, m_ref, acc_ref,
                 s_ref, p_ref, mx_ref, *, lambda_init, n_key_tiles):
    i = pl.program_id(1)
    j = pl.program_id(2)
    tq = qt_ref.shape[1]
    tk = k_ref.shape[0]
    dq = DIFF_QK_DIM
    hw = 2 * dq
    nf = DIFF_NFEAT

    @pl.when(j == 0)
    def _():
        m_ref[...] = jnp.full(m_ref.shape, NEG, F32)
        acc_ref[...] = jnp.zeros(acc_ref.shape, F32)
        ii = lax.broadcasted_iota(I32, (1, 2 * tq), 1)
        ii = jnp.where(ii >= tq, ii - tq, ii).astype(F32)
        qt = qt_ref[...]
        col = lax.broadcasted_iota(I32, (hw, 2 * tq), 1)
        row = lax.broadcasted_iota(I32, (hw, 2 * tq), 0)
        own_map = row // dq == col // tq
        for h in range(DIFF_HEADS):
            qh = qt[h * hw:(h + 1) * hw, :]
            qh2 = jnp.where(own_map, jnp.concatenate([qh, qh], axis=1), jnp.zeros((hw, 2 * tq), BF16))
            sl = jnp.full((1, 2 * tq), slope_ref[h], F32)
            s_hi, s_lo = _bf16_split(sl)
            v_hi, v_lo = _bf16_split(-sl * ii)
            rows = jnp.concatenate([s_hi, s_lo, s_hi, s_lo, v_hi, v_lo], axis=0)
            zrow = jnp.zeros((nf, 2 * tq), F32)
            zero = jnp.zeros((hw - 2 * nf, 2 * tq), F32)
            variants = ([rows, zrow], [zrow, -rows], [zrow, zrow])
            for v, pieces in enumerate(variants):
                qtb_ref[v, h, 0:hw, :] = qh2
                qtb_ref[v, h, hw:2 * hw, :] = jnp.concatenate(pieces + [zero], axis=0).astype(BF16)

    q0 = i * tq

    def key_start(step):
        return _diff_key_tile(i, step, tq, tk, n_key_tiles) * tk

    def score_head(slot, h, variant, dist):
        s = jnp.dot(k_ref[:, h * LANES:(h + 1) * LANES], qtb_ref[variant, h], preferred_element_type=F32)
        if dist is not None:
            s = s - slope_ref[h] * dist
        s_ref[slot, h] = s
        mx_ref[slot, h:h + 1, :] = jnp.max(s, axis=0, keepdims=True)

    def softmax_head(slot, h, shift):
        m_old = m_ref[h:h + 1, :]
        m_new = jnp.maximum(m_old, mx_ref[slot, h:h + 1, :] + shift)
        p_ref[h] = jnp.exp2(s_ref[slot, h] - (m_new - shift)).astype(BF16)
        m_ref[h:h + 1, :] = m_new
        return jnp.exp2(m_old - m_new)

    def value_head(h, alpha):
        pv = jnp.dot(vt_ref[h * DIFF_VROWS:(h + 1) * DIFF_VROWS, :], p_ref[h], preferred_element_type=F32)
        acc_ref[h] = alpha * acc_ref[h] + pv

    def tile_shift(step):
        k0 = key_start(step)
        sign = jnp.where(step == 0, 0.0, jnp.where(k0 < q0, 1.0, -1.0))
        return sign * (k0 - q0).astype(F32)

    @pl.when(j == 0)
    def _():
        kpos = key_start(0) + lax.broadcasted_iota(I32, (tk, 2 * tq), 0)
        qpos = q0 + lax.broadcasted_iota(I32, (tk, 2 * tq), 1) % tq
        dist = jnp.abs(qpos - kpos).astype(F32)
        for h in range(DIFF_HEADS):
            score_head(0, h, 2, dist)

    for parity in range(2):
        @pl.when((j > 0) & (j < n_key_tiles) & (j % 2 == parity))
        def _():
            variant = jnp.where(key_start(j) < q0, 0, 1)
            shift = tile_shift(j - 1)
            alphas = []
            for h in range(DIFF_HEADS):
                score_head(parity, h, variant, None)
                alphas.append(softmax_head(1 - parity, h, slope_ref[h] * shift))
            for h in range(DIFF_HEADS):
                value_head(h, alphas[h])

    @pl.when(j == n_key_tiles)
    def _():
        shift = tile_shift(n_key_tiles - 1)
        slot = (n_key_tiles - 1) % 2
        alphas = [softmax_head(slot, h, slope_ref[h] * shift) for h in range(DIFF_HEADS)]
        for h in range(DIFF_HEADS):
            value_head(h, alphas[h])
        lp = lam_ref[...]
        lam = (jnp.exp(jnp.sum(lp[0:1] * lp[1:2], axis=-1, keepdims=True))
               - jnp.exp(jnp.sum(lp[2:3] * lp[3:4], axis=-1, keepdims=True)) + lambda_init)
        outs = []
        for h in range(DIFF_HEADS):
            a = acc_ref[h]
            o = a[0:DIFF_V_DIM] / a[DIFF_V_DIM:DIFF_V_DIM + 1]
            o = o[:, 0:tq] - lam * o[:, tq:2 * tq]
            ms = jnp.mean(o * o, axis=0, keepdims=True)
            outs.append(o * lax.rsqrt(ms + NORM_EPS) * sw_ref[...] * (1.0 - lambda_init))
        o_ref[...] = jnp.transpose(jnp.concatenate(outs, axis=0)).astype(o_ref.dtype)


def _diff(cqt, ck, cvt, slopes, lam_params, subln_w_col, lambda_init, batch, seq, tq, tk):
    assert tk % tq == 0 and seq % tk == 0
    nq, nk = seq // tq, seq // tk
    t = batch * seq
    smem = pl.BlockSpec(memory_space=pltpu.SMEM)
    key_tile = functools.partial(_diff_key_tile, tq=tq, tk=tk, nk=nk)
    return pl.pallas_call(
        functools.partial(_diff_kernel, lambda_init=lambda_init, n_key_tiles=nk),
        grid=(batch, nq, nk + 1),
        in_specs=[smem,
                  pl.BlockSpec((DIFF_QK_WIDTH, tq), lambda b, i, j: (0, b * nq + i)),
                  pl.BlockSpec((tk, DIFF_HEADS * LANES),
                               lambda b, i, j: (b * nk + key_tile(i, jnp.minimum(j, nk - 1)), 0)),
                  pl.BlockSpec((DIFF_HEADS * DIFF_VROWS, tk),
                               lambda b, i, j: (0, b * nk + key_tile(i, jnp.maximum(j - 1, 0)))),
                  pl.BlockSpec(lam_params.shape, lambda b, i, j: (0, 0)),
                  pl.BlockSpec(subln_w_col.shape, lambda b, i, j: (0, 0))],
        out_specs=pl.BlockSpec((tq, DIFF_V_WIDTH), lambda b, i, j: (b * nq + i, 0)),
        out_shape=jax.ShapeDtypeStruct((t, DIFF_V_WIDTH), BF16),
        scratch_shapes=[pltpu.VMEM((3, DIFF_HEADS, LANES, 2 * tq), BF16),
                        pltpu.VMEM((DIFF_HEADS, 2 * tq), F32),
                        pltpu.VMEM((DIFF_HEADS, DIFF_VROWS, 2 * tq), F32),
                        pltpu.VMEM((2, DIFF_HEADS, tk, 2 * tq), F32),
                        pltpu.VMEM((DIFF_HEADS, tk, 2 * tq), BF16),
                        pltpu.VMEM((2, DIFF_HEADS, 2 * tq), F32)],
        compiler_params=_cparams("parallel", "parallel", "arbitrary"),
        name="diffattn",
    )(slopes, cqt, ck, cvt, lam_params, subln_w_col)


def _ssd_direction(fwd, off, xs_ref, bt_ref, cm_ref, dt_ref, dtb_ref, alog_ref, dsk_ref, y_ref, state_ref):
    q = SSD_CHUNK
    rows = slice(off, off + q)
    lane0 = 0 if fwd else SSD_HEADS
    dt_all = _softplus(dt_ref[rows, :] + dtb_ref[...])
    dta_all = dt_all * -jnp.exp(alog_ref[...])
    row = lax.broadcasted_iota(I32, (q, q), 0)
    col = lax.broadcasted_iota(I32, (q, q), 1)
    keep = (row >= col) if fwd else (row <= col)
    tri = keep.astype(BF16)
    part_hi = dta_all.astype(BF16)
    rest = dta_all - part_hi.astype(F32)
    part_mid = rest.astype(BF16)
    part_lo = (rest - part_mid.astype(F32)).astype(BF16)
    da_all = (jnp.dot(tri, part_hi, preferred_element_type=F32)
              + jnp.dot(tri, part_mid, preferred_element_type=F32)
              + jnp.dot(tri, part_lo, preferred_element_type=F32))
    da_all_t = jnp.transpose(da_all)
    dt_all_t = jnp.transpose(dt_all)
    tot_all = jnp.sum(dta_all, axis=0, keepdims=True)

    xs = xs_ref[rows, :]
    bmt = bt_ref[:, rows]
    cm = cm_ref[rows, :]
    rep = SSD_HEADS // SSD_GROUPS
    ns = SSD_STATE
    g_mats = [jnp.dot(cm[:, g * ns:(g + 1) * ns].astype(BF16), bmt[g * ns:(g + 1) * ns, :].astype(BF16),
                      preferred_element_type=F32) for g in range(SSD_GROUPS)]
    ys = []
    for h in range(SSD_HEADS):
        g = h // rep
        ln = lane0 + h
        dac = da_all[:, ln:ln + 1]
        dar = da_all_t[ln:ln + 1, :]
        dtr = dt_all_t[ln:ln + 1, :]
        tot = tot_all[:, ln:ln + 1]
        dac_b = jnp.broadcast_to(dac, (q, q))
        decay = jnp.exp(jnp.where(keep, dac_b - dar, NEG))
        xh = xs[:, h * SSD_HEAD_DIM:(h + 1) * SSD_HEAD_DIM]
        xhb = xh.astype(BF16)
        y = jnp.dot((g_mats[g] * decay * dtr).astype(BF16), xhb, preferred_element_type=F32)
        st = state_ref[h]
        c_in = (cm[:, g * ns:(g + 1) * ns] * jnp.exp(dac_b[:, 0:ns])).astype(BF16)
        y = y + jnp.dot(c_in, st.astype(BF16), preferred_element_type=F32)
        to_end = jnp.exp(tot - dar) * dtr
        b_out = (bmt[g * ns:(g + 1) * ns, :] * to_end).astype(BF16)
        state_ref[h] = st * jnp.exp(tot) + jnp.dot(b_out, xhb, preferred_element_type=F32)
        if fwd:
            y = y + dsk_ref[:, h:h + 1] * xh
        ys.append(y)
    y_ref[rows, :] = jnp.concatenate(ys, axis=-1)


def _ssd_kernel(xsf_ref, btf_ref, cmf_ref, dtf_ref, xsb_ref, btb_ref, cmb_ref, dtb_in_ref,
                dtbias_ref, alog_ref, dsk_ref, yf_ref, yb_ref, state_ref):
    @pl.when(pl.program_id(1) == 0)
    def _():
        state_ref[...] = jnp.zeros(state_ref.shape, F32)

    n = SSD_CHUNKS_PER_STEP
    for u in range(n):
        _ssd_direction(True, u * SSD_CHUNK, xsf_ref, btf_ref, cmf_ref, dtf_ref, dtbias_ref, alog_ref, dsk_ref,
                       yf_ref, state_ref.at[0])
        _ssd_direction(False, (n - 1 - u) * SSD_CHUNK, xsb_ref, btb_ref, cmb_ref, dtb_in_ref, dtbias_ref,
                       alog_ref, dsk_ref, yb_ref, state_ref.at[1])


def _ssd(xs, bt, cm, dt_raw, dt_bias, a_log, d_skip, batch, seq):
    q = SSD_CHUNK * SSD_CHUNKS_PER_STEP
    nc = seq // q
    t = batch * seq
    full = lambda b, c: (0, 0)
    fw = lambda b, c: b * nc + c
    bw = lambda b, c: b * nc + nc - 1 - c

    def specs(idx):
        return [pl.BlockSpec((q, SSD_D_INNER), lambda b, c: (idx(b, c), 0)),
                pl.BlockSpec((SSD_BC_DIM, q), lambda b, c: (0, idx(b, c))),
                pl.BlockSpec((q, SSD_BC_DIM), lambda b, c: (idx(b, c), 0)),
                pl.BlockSpec((q, LANES), lambda b, c: (idx(b, c), 0))]

    return pl.pallas_call(
        _ssd_kernel,
        grid=(batch, nc),
        in_specs=specs(fw) + specs(bw) + [pl.BlockSpec(dt_bias.shape, full), pl.BlockSpec(a_log.shape, full),
                                          pl.BlockSpec(d_skip.shape, full)],
        out_specs=[pl.BlockSpec((q, SSD_D_INNER), lambda b, c: (fw(b, c), 0)),
                   pl.BlockSpec((q, SSD_D_INNER), lambda b, c: (bw(b, c), 0))],
        out_shape=[jax.ShapeDtypeStruct((t, SSD_D_INNER), F32), jax.ShapeDtypeStruct((t, SSD_D_INNER), F32)],
        scratch_shapes=[pltpu.VMEM((2, SSD_HEADS, SSD_STATE, SSD_HEAD_DIM), F32)],
        compiler_params=_cparams("parallel", "arbitrary"),
        name="ssd",
    )(xs, bt, cm, dt_raw, xs, bt, cm, dt_raw, dt_bias, a_log, d_skip)


def _outproj_kernel(x_ref, ya_ref, yf_ref, yb_ref, z_ref, snw_ref, yc_ref, wo_ref, fnw_ref, wrt_ref, br_ref,
                    xn_ref, ri_ref, rf_ref, cnt_ref, tri_ref, carry_ref):
    step = pl.program_id(0)
    tm = x_ref.shape[0]

    @pl.when(step == 0)
    def _():
        carry_ref[...] = jnp.zeros(carry_ref.shape, F32)
        r = lax.broadcasted_iota(I32, (tm, tm), 0)
        cc = lax.broadcasted_iota(I32, (tm, tm), 1)
        tri_ref[...] = (r <= cc).astype(BF16)

    y = (yf_ref[...] + yb_ref[...]) * _silu(z_ref[...])
    yb = _rms(y, snw_ref[...]).astype(BF16)
    acc = jnp.dot(ya_ref[...], wo_ref[0:A_Q_DIM, :], preferred_element_type=F32)
    acc = acc + jnp.dot(yb, wo_ref[A_Q_DIM:A_Q_DIM + SSD_D_INNER, :], preferred_element_type=F32)
    acc = acc + jnp.dot(yc_ref[...], wo_ref[A_Q_DIM + SSD_D_INNER:, :], preferred_element_type=F32)
    xn = x_ref[...] + acc
    xn_ref[...] = xn

    h = _rms(xn, fnw_ref[...])
    h_hi = h.astype(BF16)
    h_lo = (h - h_hi.astype(F32)).astype(BF16)
    nt = (((1,), (1,)), ((), ()))
    logits = (lax.dot_general(wrt_ref[0], h_hi, nt, preferred_element_type=F32)
              + lax.dot_general(wrt_ref[0], h_lo, nt, preferred_element_type=F32)
              + lax.dot_general(wrt_ref[1], h_hi, nt, preferred_element_type=F32)) + br_ref[...]
    ne, epg, ng = N_EXPERTS, EXPERTS_PER_GROUP, N_EXPERT_GROUPS
    gl = logits[ne:ne + ng, :]
    gmax = jnp.max(gl, axis=0, keepdims=True)
    g_sel = jnp.full((1, tm), float(ng - 1), F32)
    for g in range(ng - 2, -1, -1):
        g_sel = jnp.where(gl[g:g + 1, :] == gmax, float(g), g_sel)
    g_gate = 1.0 / jnp.sum(jnp.exp(gl - gmax), axis=0, keepdims=True)
    e_in = logits[0:epg, :]
    for g in range(1, ng):
        e_in = jnp.where(g_sel == float(g), logits[g * epg:(g + 1) * epg, :], e_in)
    sub = lax.broadcasted_iota(I32, (epg, tm), 0).astype(F32)
    m1 = jnp.max(e_in, axis=0, keepdims=True)
    i1 = jnp.min(jnp.where(e_in == m1, sub, float(epg)), axis=0, keepdims=True)
    rest = jnp.where(sub == i1, NEG, e_in)
    m2 = jnp.max(rest, axis=0, keepdims=True)
    i2 = jnp.min(jnp.where(rest == m2, sub, float(epg)), axis=0, keepdims=True)
    r = jnp.exp(m2 - m1)
    c1 = g_gate / (1.0 + r)
    c2 = g_gate * r / (1.0 + r)
    e1 = (g_sel * epg + i1).astype(I32)
    e2 = (g_sel * epg + i2).astype(I32)

    erow = lax.broadcasted_iota(I32, (ne, tm), 0)
    hit1 = erow == e1
    hit2 = erow == e2
    oh = jnp.where(hit1 | hit2, 1.0, 0.0)
    incl = jnp.dot(oh.astype(BF16), tri_ref[...], preferred_element_type=F32)
    before = incl - oh + carry_ref[:, 0:1]
    rank1 = jnp.sum(jnp.where(hit1, before, 0.0), axis=0, keepdims=True)
    rank2 = jnp.sum(jnp.where(hit2, before, 0.0), axis=0, keepdims=True)
    carry_ref[...] = carry_ref[...] + jnp.sum(oh, axis=1, keepdims=True)
    cnt_ref[...] = carry_ref[...]
    zi = jnp.zeros((1, tm), I32)
    ri_ref[...] = jnp.concatenate([e1, e2, rank1.astype(I32), rank2.astype(I32), zi, zi, zi, zi], axis=0)
    zf = jnp.zeros((1, tm), F32)
    rf_ref[...] = jnp.concatenate([c1, c2, zf, zf, zf, zf, zf, zf], axis=0)


def _outproj(x2, ya, yf, yb, z, ssd_norm_w, yc, w_out, ffn_norm_w, wrt, br, tm):
    t, d = x2.shape
    row = lambda i: (i, 0)
    full = lambda i: (0, 0)
    return pl.pallas_call(
        _outproj_kernel,
        grid=(t // tm,),
        in_specs=[pl.BlockSpec((tm, d), row),
                  pl.BlockSpec((tm, A_Q_DIM), row),
                  pl.BlockSpec((tm, SSD_D_INNER), row),
                  pl.BlockSpec((tm, SSD_D_INNER), row),
                  pl.BlockSpec((tm, SSD_D_INNER), row),
                  pl.BlockSpec(ssd_norm_w.shape, full),
                  pl.BlockSpec((tm, DIFF_V_WIDTH), row),
                  pl.BlockSpec(w_out.shape, full),
                  pl.BlockSpec(ffn_norm_w.shape, full),
                  pl.BlockSpec(wrt.shape, lambda i: (0, 0, 0)),
                  pl.BlockSpec(br.shape, full)],
        out_specs=[pl.BlockSpec((tm, d), row),
                   pl.BlockSpec((SUBLANES, tm), lambda i: (0, i)),
                   pl.BlockSpec((SUBLANES, tm), lambda i: (0, i)),
                   pl.BlockSpec((N_EXPERTS, LANES), full)],
        out_shape=[jax.ShapeDtypeStruct((t, d), F32),
                   jax.ShapeDtypeStruct((SUBLANES, t), I32),
                   jax.ShapeDtypeStruct((SUBLANES, t), F32),
                   jax.ShapeDtypeStruct((N_EXPERTS, LANES), F32)],
        scratch_shapes=[pltpu.VMEM((tm, tm), BF16), pltpu.VMEM((N_EXPERTS, LANES), F32)],
        compiler_params=_cparams("arbitrary"),
        name="outproj_router",
    )(x2, ya, yf, yb, z, ssd_norm_w, yc, w_out, ffn_norm_w, wrt, br)


_PAD_PIECES = tuple(1 << b for b in reversed(range(MOE_ROW_TILE.bit_length() - 1)))


def _dispatch_kernel(slot1_ref, slot2_ref, pstart_ref, plen_ref, nused_ref, x_ref, xs_hbm, zero_ref, xbuf, sem):
    i = pl.program_id(0)
    tm = x_ref.shape[0]

    @pl.when(i == 0)
    def _():
        zero_ref[...] = jnp.zeros(zero_ref.shape, F32)

        def pieces(e, wait):
            n = plen_ref[e]
            first = pstart_ref[e]
            off = first + n
            for b in _PAD_PIECES:
                off = off - (n & b)
                dst = pl.ds(pl.multiple_of(off, b), b) if b >= SUBLANES else None
                if dst is not None:
                    @pl.when((n & b) != 0)
                    def _():
                        cp = pltpu.make_async_copy(zero_ref.at[pl.ds(0, b)], xs_hbm.at[dst], sem.at[2])
                        cp.wait() if wait else cp.start()

            for u in range(SUBLANES - 1):
                @pl.when(u < (n & (SUBLANES - 1)))
                def _():
                    cp = pltpu.make_async_copy(zero_ref.at[pl.ds(0, 1)], xs_hbm.at[pl.ds(first + u, 1)],
                                               sem.at[2])
                    cp.wait() if wait else cp.start()

        def tail(tile, wait):
            big = _PAD_PIECES[0]
            for part in range(MOE_ROW_TILE // big):
                dst = xs_hbm.at[pl.ds(pl.multiple_of(tile * MOE_ROW_TILE + part * big, big), big)]
                cp = pltpu.make_async_copy(zero_ref, dst, sem.at[2])
                cp.wait() if wait else cp.start()

        def loop(fn, lo, hi, wait):
            def body(k, carry):
                fn(k, wait)
                return carry

            lax.fori_loop(lo, hi, body, 0)

        n_tiles = xs_hbm.shape[0] // MOE_ROW_TILE
        for wait in (False, True):
            loop(pieces, 0, N_EXPERTS, wait)
            loop(tail, nused_ref[0], n_tiles, wait)

    base = i * tm
    last = pl.num_programs(0) - 1

    def drain(slot):
        for _ in range(2):
            pltpu.make_async_copy(xbuf.at[slot], xs_hbm.at[pl.ds(0, tm)], sem.at[slot]).wait()

    for parity in range(2):
        @pl.when(i % 2 == parity)
        def _():
            buf = xbuf.at[parity]
            buf[...] = x_ref[...]
            for r in range(tm):
                src = buf.at[pl.ds(r, 1)]
                pltpu.make_async_copy(src, xs_hbm.at[pl.ds(slot1_ref[base + r], 1)],
                                      sem.at[parity]).start(priority=0)
                pltpu.make_async_copy(src, xs_hbm.at[pl.ds(slot2_ref[base + r], 1)],
                                      sem.at[parity]).start(priority=1)

            @pl.when(i > 0)
            def _():
                drain(1 - parity)

            @pl.when(i == last)
            def _():
                drain(parity)


def _dispatch(xn, slot1, slot2, pad_start, pad_len, n_used, n_rows, tm):
    t, d = xn.shape
    grid_spec = pltpu.PrefetchScalarGridSpec(
        num_scalar_prefetch=5,
        grid=(t // tm,),
        in_specs=[pl.BlockSpec((tm, d), lambda i, s1, s2, ps, pn, nu: (i, 0))],
        out_specs=pl.BlockSpec(memory_space=pl.ANY),
        scratch_shapes=[pltpu.VMEM((_PAD_PIECES[0], d), F32), pltpu.VMEM((2, tm, d), F32),
                        pltpu.SemaphoreType.DMA((3,))],
    )
    return pl.pallas_call(
        _dispatch_kernel,
        grid_spec=grid_spec,
        out_shape=jax.ShapeDtypeStruct((n_rows, d), F32),
        compiler_params=_cparams("arbitrary"),
        name="moe_dispatch",
    )(slot1, slot2, pad_start, pad_len, n_used, xn)


def _moe_kernel(texp_ref, nused_ref, first_ref, wslot_ref, next_ref, x_ref, fnw_ref, wg_hbm, wu_hbm, wd_hbm,
                y_ref, wg_buf, wu_buf, wd_buf, sem, *, layer):
    i = pl.program_id(0)

    def weight_copies(expert, slot):
        return [pltpu.make_async_copy(hbm.at[layer, expert], buf.at[slot], sem.at[slot])
                for hbm, buf in ((wg_hbm, wg_buf), (wu_hbm, wu_buf), (wd_hbm, wd_buf))]

    @pl.when(i < nused_ref[0])
    def _():
        slot = wslot_ref[i]

        @pl.when(i == 0)
        def _():
            for cp in weight_copies(texp_ref[0], 0):
                cp.start()

        @pl.when(first_ref[i] == 1)
        def _():
            for cp in weight_copies(texp_ref[i], slot):
                cp.wait()

            @pl.when(next_ref[i] >= 0)
            def _():
                for cp in weight_copies(next_ref[i], 1 - slot):
                    cp.start()

        h = _rms(x_ref[...], fnw_ref[...]).astype(BF16)
        hg = jnp.dot(h, wg_buf[slot].astype(BF16), preferred_element_type=F32)
        hu = jnp.dot(h, wu_buf[slot].astype(BF16), preferred_element_type=F32)
        act = (_silu(hg) * hu).astype(BF16)
        y_ref[...] = jnp.dot(act, wd_buf[slot].astype(BF16), preferred_element_type=F32)

    @pl.when(i >= nused_ref[0])
    def _():
        y_ref[...] = jnp.zeros(y_ref.shape, F32)


def _moe(xs, ffn_norm_w, w_gate, w_up, w_down, layer, tile_expert, n_used):
    n_rows, d = xs.shape
    f = w_gate.shape[-1]
    tr = MOE_ROW_TILE

    n_tiles = n_rows // tr

    idx = jnp.arange(n_tiles, dtype=I32)
    prev_expert = jnp.concatenate([jnp.full((1,), -1, I32), tile_expert[:-1]])
    first = ((idx < n_used[0]) & (tile_expert != prev_expert)).astype(I32)
    wslot = ((jnp.cumsum(first) - 1) % 2).astype(I32)
    first_pos = jnp.where(first == 1, idx, n_tiles)
    next_first = jnp.concatenate([lax.cummin(first_pos, reverse=True)[1:], jnp.full((1,), n_tiles, I32)])
    next_expert = jnp.where(next_first < n_tiles, tile_expert[jnp.minimum(next_first, n_tiles - 1)], -1).astype(I32)

    def used(i, nu):
        return jnp.maximum(jnp.minimum(i, nu[0] - 1), 0)

    anyspace = pl.BlockSpec(memory_space=pl.ANY)
    grid_spec = pltpu.PrefetchScalarGridSpec(
        num_scalar_prefetch=5,
        grid=(n_tiles,),
        in_specs=[pl.BlockSpec((tr, d), lambda i, te, nu, fi, ws, nx: (used(i, nu), 0)),
                  pl.BlockSpec(ffn_norm_w.shape, lambda i, te, nu, fi, ws, nx: (0, 0)),
                  anyspace, anyspace, anyspace],
        out_specs=pl.BlockSpec((tr, d), lambda i, te, nu, fi, ws, nx: (i, 0)),
        scratch_shapes=[pltpu.VMEM((2, d, f), F32), pltpu.VMEM((2, d, f), F32), pltpu.VMEM((2, f, d), F32),
                        pltpu.SemaphoreType.DMA((2,))],
    )
    return pl.pallas_call(
        functools.partial(_moe_kernel, layer=layer),
        grid_spec=grid_spec,
        out_shape=jax.ShapeDtypeStruct((n_rows, d), F32),
        compiler_params=_cparams("arbitrary"),
        name="moe_experts",
    )(tile_expert, n_used, first, wslot, next_expert, xs, ffn_norm_w, w_gate, w_up, w_down)


def _combine_kernel(slot1_ref, slot2_ref, x_ref, cw_ref, nw_ref, y_hbm, o_ref, ybuf, sem, *, final_norm):
    i = pl.program_id(0)
    n = pl.num_programs(0)
    tm = x_ref.shape[0]

    def start_gather(tile, slot):
        base = tile * tm
        for r in range(tm):
            pltpu.make_async_copy(y_hbm.at[pl.ds(slot1_ref[base + r], 1)], ybuf.at[slot, 0, pl.ds(r, 1)],
                                  sem.at[slot]).start(priority=0)
            pltpu.make_async_copy(y_hbm.at[pl.ds(slot2_ref[base + r], 1)], ybuf.at[slot, 1, pl.ds(r, 1)],
                                  sem.at[slot]).start(priority=1)

    def compute(slot):
        for k in range(2):
            pltpu.make_async_copy(y_hbm.at[pl.ds(0, tm)], ybuf.at[slot, k], sem.at[slot]).wait()
        cw = cw_ref[...]
        out = x_ref[...] + cw[:, 0:1] * ybuf[slot, 0] + cw[:, 1:2] * ybuf[slot, 1]
        if final_norm:
            out = _rms(out, nw_ref[...])
        o_ref[...] = out

    @pl.when(i == 0)
    def _():
        start_gather(0, 0)

    for parity in range(2):
        @pl.when(i % 2 == parity)
        def _():
            @pl.when(i + 1 < n)
            def _():
                start_gather(i + 1, 1 - parity)

            compute(parity)


def _combine(xn, cw, norm_w, y_sorted, slot1, slot2, tm, final_norm):
    t, d = xn.shape
    grid_spec = pltpu.PrefetchScalarGridSpec(
        num_scalar_prefetch=2,
        grid=(t // tm,),
        in_specs=[pl.BlockSpec((tm, d), lambda i, s1, s2: (i, 0)),
                  pl.BlockSpec((tm, cw.shape[1]), lambda i, s1, s2: (i, 0)),
                  pl.BlockSpec(norm_w.shape, lambda i, s1, s2: (0, 0)),
                  pl.BlockSpec(memory_space=pl.ANY)],
        out_specs=pl.BlockSpec((tm, d), lambda i, s1, s2: (i, 0)),
        scratch_shapes=[pltpu.VMEM((2, 2, tm, d), F32), pltpu.SemaphoreType.DMA((2,))],
    )
    return pl.pallas_call(
        functools.partial(_combine_kernel, final_norm=final_norm),
        grid_spec=grid_spec,
        out_shape=jax.ShapeDtypeStruct((t, d), F32),
        compiler_params=_cparams("arbitrary"),
        name="moe_combine",
    )(slot1, slot2, xn, cw, norm_w, y_sorted)


def _pad_lanes(v):
    v = v.reshape(1, -1).astype(F32)
    return jnp.pad(v, ((0, 0), (0, LANES - v.shape[1])))


def kernel(x, attn_norm_w, w_in, swa_sink, ssd_conv_w, ssd_conv_b, ssd_dt_bias, ssd_a_log, ssd_d, ssd_norm_w,
           diff_lambda, diff_subln_w, w_out, ffn_norm_w, w_router_group, b_router_group, w_router_expert,
           b_router_expert, w_gate, w_up, w_down, final_norm_w):
    return _forward(x, attn_norm_w, w_in, swa_sink, ssd_conv_w, ssd_conv_b, ssd_dt_bias, ssd_a_log, ssd_d,
                    ssd_norm_w, diff_lambda, diff_subln_w, w_out, ffn_norm_w, w_router_group, b_router_group,
                    w_router_expert, b_router_expert, w_gate, w_up, w_down, final_norm_w)


def _forward(x, attn_norm_w, w_in, swa_sink, ssd_conv_w, ssd_conv_b, ssd_dt_bias, ssd_a_log, ssd_d, ssd_norm_w,
             diff_lambda, diff_subln_w, w_out, ffn_norm_w, w_router_group, b_router_group, w_router_expert,
             b_router_expert, w_gate, w_up, w_down, final_norm_w, tm=512, tq=512, tk=1024, tmc=256):
    batch, seq, d = x.shape
    depth = w_in.shape[0]
    t = batch * seq
    tr = MOE_ROW_TILE
    n_tiles = (2 * t) // tr + N_EXPERTS
    slopes = jnp.exp2(-8.0 * jnp.arange(1, N_ALIBI_HEADS + 1, dtype=F32) / N_ALIBI_HEADS)
    swa_slopes, diff_slopes = slopes[:SWA_HEADS], slopes[SWA_HEADS:]

    sizes = [A_Q_DIM, A_KV_DIM, A_KV_DIM, SSD_D_INNER, SSD_CONV_DIM, SSD_DT_DIM, DIFF_QK_WIDTH, DIFF_QK_WIDTH,
             DIFF_V_WIDTH]
    offs = [0]
    for s in sizes:
        offs.append(offs[-1] + s)
    o_aq, o_ak, o_av, o_z, o_xbc, o_dt, o_cq, o_ck, o_cv, o_end = offs

    x2 = x.reshape(t, d)
    for l in range(depth):
        w = w_in[l]
        hw = 2 * DIFF_QK_DIM
        w_ck = jnp.pad(w[:, o_ck:o_cv].reshape(d, DIFF_HEADS, hw), ((0, 0), (0, 0), (0, LANES - hw)))
        w_main = jnp.concatenate(
            [w[:, o_ak:o_av], w[:, o_z:o_dt], w_ck.reshape(d, DIFF_HEADS * LANES), w[:, o_dt:o_cq],
             jnp.zeros((d, LANES - SSD_DT_DIM), w.dtype)], axis=1).astype(BF16)
        w_t = jnp.concatenate([w[:, o_aq:o_ak], w[:, o_av:o_z], w[:, o_cq:o_ck], w[:, o_cv:o_end]],
                              axis=1).T.astype(BF16)
        ak, z, xs, cm, ck, dt_raw, aqt, avt3, cqt, cvt, bt = _inproj(
            x2, attn_norm_w[l].reshape(1, d), w_main, w_t, ssd_conv_w[l].astype(F32),
            ssd_conv_b[l].reshape(1, -1).astype(F32), tm, seq, tk)

        ya = _swa(aqt, ak, avt3, swa_sink[l].astype(F32), swa_slopes, batch, seq)
        lambda_init = 0.8 - 0.6 * math.exp(-0.3 * l)
        yc = _diff(cqt, ck, cvt, diff_slopes * LOG2E, diff_lambda[l].astype(F32),
                   diff_subln_w[l].reshape(DIFF_V_DIM, 1).astype(F32), lambda_init, batch, seq, tq, tk)
        yf, yb = _ssd(xs, bt, cm, dt_raw, _pad_lanes(ssd_dt_bias[l]), _pad_lanes(ssd_a_log[l]),
                      _pad_lanes(ssd_d[l]), batch, seq)

        wr32 = jnp.concatenate([w_router_expert[l], w_router_group[l],
                                jnp.zeros((d, SUBLANES - N_EXPERT_GROUPS), F32)], axis=1).T.astype(F32)
        wr_hi = wr32.astype(BF16)
        wrt = jnp.stack([wr_hi, (wr32 - wr_hi.astype(F32)).astype(BF16)])
        br = jnp.concatenate([b_router_expert[l], b_router_group[l],
                              jnp.zeros((SUBLANES - N_EXPERT_GROUPS,), F32)]).reshape(-1, 1).astype(F32)
        xn, ri, rf, cnt = _outproj(x2, ya, yf, yb, z, ssd_norm_w[l].reshape(1, -1), yc, w_out[l].astype(BF16),
                                   ffn_norm_w[l].reshape(1, d), wrt, br, tm)

        counts = cnt[:, 0].astype(I32)
        padded = ((counts + tr - 1) // tr) * tr
        ends = jnp.cumsum(padded)
        starts = ends - padded
        experts = jnp.arange(N_EXPERTS, dtype=I32)[:, None]

        def slot_of(e, rank):
            return jnp.sum(jnp.where(e[None, :] == experts, starts[:, None], 0), axis=0) + rank

        slot1 = slot_of(ri[0], ri[2])
        slot2 = slot_of(ri[1], ri[3])
        tile_start = jnp.arange(n_tiles, dtype=I32) * tr
        tile_expert = jnp.minimum(jnp.sum(ends[None, :] <= tile_start[:, None], axis=1), N_EXPERTS - 1).astype(I32)
        n_used = (ends[-1] // tr).astype(I32).reshape(1)

        xs_sorted = _dispatch(xn, slot1, slot2, starts + counts, padded - counts, n_used, n_tiles * tr, tmc)
        y_sorted = _moe(xs_sorted, ffn_norm_w[l].reshape(1, d), w_gate, w_up, w_down, l, tile_expert, n_used)
        last = l == depth - 1
        x2 = _combine(xn, rf.T, final_norm_w.reshape(1, d), y_sorted, slot1, slot2, tmc, last)
    return x2.reshape(batch, seq, d)
```

```python
import functools
import math

import jax
import jax.numpy as jnp
from jax import lax
from jax.experimental import pallas as pl
from jax.experimental.pallas import tpu as pltpu

F32 = jnp.float32
BF16 = jnp.bfloat16
I32 = jnp.int32

HEAD_DIM = 64
SWA_HEADS = 6
SWA_KV_HEADS = 2
SWA_WINDOW = 128
SSD_HEADS = 6
SSD_HEAD_DIM = 64
SSD_GROUPS = 2
SSD_STATE = 64
SSD_CONV = 5
DIFF_HEADS = 4
DIFF_QK_DIM = 32
DIFF_V_DIM = 64
N_EXPERT_GROUPS = 4
EXPERTS_PER_GROUP = 8
N_EXPERTS = N_EXPERT_GROUPS * EXPERTS_PER_GROUP
NORM_EPS = 1e-6

A_Q_DIM = SWA_HEADS * HEAD_DIM
A_KV_DIM = SWA_KV_HEADS * HEAD_DIM
SSD_D_INNER = SSD_HEADS * SSD_HEAD_DIM
SSD_BC_DIM = SSD_GROUPS * SSD_STATE
SSD_CONV_DIM = SSD_D_INNER + 2 * SSD_BC_DIM
SSD_DT_DIM = 2 * SSD_HEADS
DIFF_QK_WIDTH = DIFF_HEADS * 2 * DIFF_QK_DIM
DIFF_V_WIDTH = DIFF_HEADS * DIFF_V_DIM
N_ALIBI_HEADS = SWA_HEADS + DIFF_HEADS

LANES = 128
SUBLANES = 8
VMEM_LIMIT = 56 * 1024 * 1024
NEG = -1e30
LOG2E = math.log2(math.e)

SSD_CHUNK = 128
SSD_CHUNKS_PER_STEP = 4
SWA_BLOCKS_PER_STEP = 4
MOE_ROW_TILE = 256
DIFF_PAIR = 2 * 2 * DIFF_QK_DIM
DIFF_VROWS = 80
DIFF_NFEAT = 6


def _cparams(*sem):
    return pltpu.CompilerParams(dimension_semantics=sem, vmem_limit_bytes=VMEM_LIMIT)


def _rms(x, w):
    return x * lax.rsqrt(jnp.mean(x * x, axis=-1, keepdims=True) + NORM_EPS) * w


def _silu(x):
    return x / (1.0 + jnp.exp(-x))


def _softplus(x):
    return jnp.maximum(x, 0.0) + jnp.log(1.0 + jnp.exp(-jnp.abs(x)))


def _bf16_split(x):
    hi = x.astype(BF16).astype(F32)
    lo = (x - hi).astype(BF16).astype(F32)
    return hi, lo


_C_AK = 0
_C_Z = _C_AK + A_KV_DIM
_C_XBC = _C_Z + SSD_D_INNER
_C_CK = _C_XBC + SSD_CONV_DIM
_C_DT = _C_CK + DIFF_HEADS * LANES
_C_END = _C_DT + LANES
_R_AQ = 0
_R_AV = _R_AQ + A_Q_DIM
_R_CQ = _R_AV + A_KV_DIM
_R_CV = _R_CQ + DIFF_QK_WIDTH
_R_END = _R_CV + DIFF_V_WIDTH


def _inproj_kernel(x_ref, xp_ref, xn_ref, nw_ref, w_ref, wt_ref, cw_ref, cb_ref,
                   ak_ref, z_ref, xs_ref, cm_ref, ck_ref, dt_ref, aqt_ref, avt_ref, cqt_ref, cvt_ref, bt_ref,
                   *, tiles_per_seq, diff_key_tile):
    i = pl.program_id(0)
    tm = x_ref.shape[0]
    nw = nw_ref[...]
    h = _rms(x_ref[...], nw).astype(BF16)

    def seg(lo, hi):
        return jnp.dot(h, w_ref[:, lo:hi], preferred_element_type=F32)

    ak_ref[...] = seg(_C_AK, _C_Z).astype(BF16)
    z_ref[...] = seg(_C_Z, _C_XBC)
    pos = (i * tm + lax.broadcasted_iota(I32, (tm, _C_DT - _C_CK), 0)) % diff_key_tile
    ck_ref[...] = (seg(_C_CK, _C_DT) + _diff_key_features(pos)).astype(BF16)
    dt_ref[...] = seg(_C_DT, _C_END)

    w_xbc = w_ref[:, _C_XBC:_C_CK]
    first = i % tiles_per_seq == 0
    last = i % tiles_per_seq == tiles_per_seq - 1
    prev = jnp.dot(_rms(xp_ref[...], nw).astype(BF16), w_xbc, preferred_element_type=F32)
    nxt = jnp.dot(_rms(xn_ref[...], nw).astype(BF16), w_xbc, preferred_element_type=F32)
    prev = jnp.where(first, 0.0, prev)
    nxt = jnp.where(last, 0.0, nxt)
    ext = jnp.concatenate([prev, seg(_C_XBC, _C_CK), nxt], axis=0)
    half = SSD_CONV // 2
    conv = cb_ref[...]
    for k in range(SSD_CONV):
        off = SUBLANES - half + k
        conv = conv + cw_ref[k:k + 1, :] * ext[off:off + tm, :]
    u = _silu(conv)
    xs_ref[...] = u[:, :SSD_D_INNER]
    bt_ref[...] = jnp.transpose(u[:, SSD_D_INNER:SSD_D_INNER + SSD_BC_DIM])
    cm_ref[...] = u[:, SSD_D_INNER + SSD_BC_DIM:]

    tr = lax.dot_general(wt_ref[...], h, (((1,), (1,)), ((), ())), preferred_element_type=F32)
    aqt_ref[...] = (tr[_R_AQ:_R_AV] * (HEAD_DIM ** -0.5)).astype(BF16)
    avt = tr[_R_AV:_R_CQ].astype(BF16)
    for c in range(tm // LANES):
        avt_ref[c] = avt[:, c * LANES:(c + 1) * LANES]
    cqt_ref[...] = (tr[_R_CQ:_R_CV] * (DIFF_QK_DIM ** -0.5 * LOG2E)).astype(BF16)
    pad = DIFF_VROWS - DIFF_V_DIM
    ones_row = (lax.broadcasted_iota(I32, (pad, tm), 0) == 0).astype(BF16)
    for hh in range(DIFF_HEADS):
        cvt_ref[hh * DIFF_VROWS:hh * DIFF_VROWS + DIFF_V_DIM, :] = (
            tr[_R_CV + hh * DIFF_V_DIM:_R_CV + (hh + 1) * DIFF_V_DIM].astype(BF16))
        cvt_ref[hh * DIFF_VROWS + DIFF_V_DIM:(hh + 1) * DIFF_VROWS, :] = ones_row


def _inproj(x2, norm_w, w_main, w_t, conv_w, conv_b, tm, seq, diff_key_tile):
    t, d = x2.shape
    hb = tm // SUBLANES
    n_hblk = t // SUBLANES
    row = lambda i: (i, 0)
    col = lambda i: (0, i)
    full = lambda i: (0, 0)
    row_outs = [(A_KV_DIM, BF16), (SSD_D_INNER, F32), (SSD_D_INNER, F32), (SSD_BC_DIM, F32),
                (DIFF_HEADS * LANES, BF16), (LANES, F32)]
    out_shape = [jax.ShapeDtypeStruct((t, w), dt) for w, dt in row_outs]
    out_specs = [pl.BlockSpec((tm, w), row) for w, _ in row_outs]
    out_shape += [jax.ShapeDtypeStruct((A_Q_DIM, t), BF16),
                  jax.ShapeDtypeStruct((t // LANES, A_KV_DIM, LANES), BF16),
                  jax.ShapeDtypeStruct((DIFF_QK_WIDTH, t), BF16),
                  jax.ShapeDtypeStruct((DIFF_HEADS * DIFF_VROWS, t), BF16),
                  jax.ShapeDtypeStruct((SSD_BC_DIM, t), F32)]
    out_specs += [pl.BlockSpec((A_Q_DIM, tm), col),
                  pl.BlockSpec((tm // LANES, A_KV_DIM, LANES), lambda i: (i, 0, 0)),
                  pl.BlockSpec((DIFF_QK_WIDTH, tm), col),
                  pl.BlockSpec((DIFF_HEADS * DIFF_VROWS, tm), col),
                  pl.BlockSpec((SSD_BC_DIM, tm), col)]
    return pl.pallas_call(
        functools.partial(_inproj_kernel, tiles_per_seq=seq // tm, diff_key_tile=diff_key_tile),
        grid=(t // tm,),
        in_specs=[pl.BlockSpec((tm, d), row),
                  pl.BlockSpec((SUBLANES, d), lambda i: (jnp.maximum(i * hb - 1, 0), 0)),
                  pl.BlockSpec((SUBLANES, d), lambda i: (jnp.minimum((i + 1) * hb, n_hblk - 1), 0)),
                  pl.BlockSpec((1, d), full),
                  pl.BlockSpec(w_main.shape, full), pl.BlockSpec(w_t.shape, full),
                  pl.BlockSpec(conv_w.shape, full), pl.BlockSpec(conv_b.shape, full)],
        out_specs=out_specs,
        out_shape=out_shape,
        compiler_params=_cparams("parallel"),
        name="inproj",
    )(x2, x2, x2, norm_w, w_main, w_t, conv_w, conv_b)


def _swa_kernel(sink_ref, slope_ref, qt_ref, k_ref, vt_ref, o_ref):
    step = pl.program_id(1)
    s_len = k_ref.shape[0]
    blk = SWA_WINDOW
    band = 3 * blk
    nb = s_len // blk
    rep = SWA_HEADS // SWA_KV_HEADS
    hd = HEAD_DIM
    for u in range(SWA_BLOCKS_PER_STEP):
        n = step * SWA_BLOCKS_PER_STEP + u
        start_blk = jnp.clip(n - 1, 0, nb - 3)
        start = pl.multiple_of(start_blk * blk, blk)
        kb = k_ref[pl.ds(start, band), :]
        v3 = vt_ref[pl.ds(start_blk, 3)]
        vtb = jnp.concatenate([v3[0], v3[1], v3[2]], axis=1)
        qt = qt_ref[:, u * blk:(u + 1) * blk]
        zero = jnp.zeros((hd, rep * blk), BF16)
        grp = [jnp.concatenate([qt[(g * rep + r) * hd:(g * rep + r + 1) * hd] for r in range(rep)], axis=1)
               for g in range(SWA_KV_HEADS)]
        qbd = jnp.concatenate([jnp.concatenate([grp[0], zero], axis=1),
                               jnp.concatenate([zero, grp[1]], axis=1)], axis=0)
        st = jnp.dot(kb, qbd, preferred_element_type=F32)
        kpos = start + lax.broadcasted_iota(I32, (band, blk), 0)
        qpos = n * blk + lax.broadcasted_iota(I32, (band, blk), 1)
        dist_i = jnp.abs(qpos - kpos)
        valid = dist_i <= SWA_WINDOW
        dist = dist_i.astype(F32)
        ps, inv = [], []
        for h in range(SWA_HEADS):
            s = jnp.where(valid, st[:, h * blk:(h + 1) * blk] - slope_ref[h] * dist, NEG)
            sink = sink_ref[h]
            m = jnp.maximum(jnp.max(s, axis=0, keepdims=True), sink)
            p = jnp.exp(s - m)
            inv.append(1.0 / (jnp.sum(p, axis=0, keepdims=True) + jnp.exp(sink - m)))
            ps.append(p.astype(BF16))
        outs = []
        for g in range(SWA_KV_HEADS):
            pg = jnp.concatenate(ps[g * rep:(g + 1) * rep], axis=1)
            og = jnp.dot(vtb[g * hd:(g + 1) * hd, :], pg, preferred_element_type=F32)
            for r in range(rep):
                outs.append(og[:, r * blk:(r + 1) * blk] * inv[g * rep + r])
        o_ref[u * blk:(u + 1) * blk, :] = jnp.transpose(jnp.concatenate(outs, axis=0)).astype(o_ref.dtype)


def _swa(aqt, ak, avt3, sink, slopes, batch, seq):
    blk = SWA_WINDOW
    rows = blk * SWA_BLOCKS_PER_STEP
    steps = seq // rows
    nb = seq // blk
    t = batch * seq
    smem = pl.BlockSpec(memory_space=pltpu.SMEM)
    return pl.pallas_call(
        _swa_kernel,
        grid=(batch, steps),
        in_specs=[smem, smem,
                  pl.BlockSpec((A_Q_DIM, rows), lambda b, s: (0, b * steps + s)),
                  pl.BlockSpec((seq, A_KV_DIM), lambda b, s: (b, 0)),
                  pl.BlockSpec((nb, A_KV_DIM, blk), lambda b, s: (b, 0, 0))],
        out_specs=pl.BlockSpec((rows, A_Q_DIM), lambda b, s: (b * steps + s, 0)),
        out_shape=jax.ShapeDtypeStruct((t, A_Q_DIM), BF16),
        compiler_params=_cparams("parallel", "parallel"),
        name="swa",
    )(sink, slopes, aqt, ak, avt3)


def _diff_key_tile(i, j, tq, tk, nk):
    return ((i * tq) // tk + j) % nk


def _diff_key_features(pos_in_tile):
    lane = lax.broadcasted_iota(I32, pos_in_tile.shape, 1) % LANES - 2 * DIFF_QK_DIM
    coarse = ((pos_in_tile // 16) * 16).astype(F32)
    fine = (pos_in_tile % 16).astype(F32)
    f = lane % DIFF_NFEAT
    feat = jnp.where(f < 2, coarse, jnp.where(f < 4, fine, 1.0))
    return jnp.where((lane >= 0) & (lane < 2 * DIFF_NFEAT), feat, 0.0)


def _diff_kernel(slope_ref, qt_ref, k_ref, vt_ref, lam_ref, sw_ref, o_ref, qtb_ref, m_ref, acc_ref,
                 s_ref, p_ref, mx_ref, *, lambda_init, n_key_tiles):
    i = pl.program_id(1)
    j = pl.program_id(2)
    tq = qt_ref.shape[1]
    tk = k_ref.shape[0]
    dq = DIFF_QK_DIM
    hw = 2 * dq
    nf = DIFF_NFEAT

    @pl.when(j == 0)
    def _():
        m_ref[...] = jnp.full(m_ref.shape, NEG, F32)
        acc_ref[...] = jnp.zeros(acc_ref.shape, F32)
        ii = lax.broadcasted_iota(I32, (1, 2 * tq), 1)
        ii = jnp.where(ii >= tq, ii - tq, ii).astype(F32)
        qt = qt_ref[...]
        col = lax.broadcasted_iota(I32, (hw, 2 * tq), 1)
        row = lax.broadcasted_iota(I32, (hw, 2 * tq), 0)
        own_map = row // dq == col // tq
        for h in range(DIFF_HEADS):
            qh = qt[h * hw:(h + 1) * hw, :]
            qh2 = jnp.where(own_map, jnp.concatenate([qh, qh], axis=1), jnp.zeros((hw, 2 * tq), BF16))
            sl = jnp.full((1, 2 * tq), slope_ref[h], F32)
            s_hi, s_lo = _bf16_split(sl)
            v_hi, v_lo = _bf16_split(-sl * ii)
            rows = jnp.concatenate([s_hi, s_lo, s_hi, s_lo, v_hi, v_lo], axis=0)
            zrow = jnp.zeros((nf, 2 * tq), F32)
            zero = jnp.zeros((hw - 2 * nf, 2 * tq), F32)
            variants = ([rows, zrow], [zrow, -rows], [zrow, zrow])
            for v, pieces in enumerate(variants):
                qtb_ref[v, h, 0:hw, :] = qh2
                qtb_ref[v, h, hw:2 * hw, :] = jnp.concatenate(pieces + [zero], axis=0).astype(BF16)

    q0 = i * tq

    def key_start(step):
        return _diff_key_tile(i, step, tq, tk, n_key_tiles) * tk

    def score_head(slot, h, variant, dist):
        s = jnp.dot(k_ref[:, h * LANES:(h + 1) * LANES], qtb_ref[variant, h], preferred_element_type=F32)
        if dist is not None:
            s = s - slope_ref[h] * dist
        s_ref[slot, h] = s
        mx_ref[slot, h:h + 1, :] = jnp.max(s, axis=0, keepdims=True)

    def softmax_head(slot, h, shift):
        m_old = m_ref[h:h + 1, :]
        m_new = jnp.maximum(m_old, mx_ref[slot, h:h + 1, :] + shift)
        p_ref[h] = jnp.exp2(s_ref[slot, h] - (m_new - shift)).astype(BF16)
        m_ref[h:h + 1, :] = m_new
        return jnp.exp2(m_old - m_new)

    def value_head(h, alpha):
        pv = jnp.dot(vt_ref[h * DIFF_VROWS:(h + 1) * DIFF_VROWS, :], p_ref[h], preferred_element_type=F32)
        acc_ref[h] = alpha * acc_ref[h] + pv

    def tile_shift(step):
        k0 = key_start(step)
        sign = jnp.where(step == 0, 0.0, jnp.where(k0 < q0, 1.0, -1.0))
        return sign * (k0 - q0).astype(F32)

    @pl.when(j == 0)
    def _():
        kpos = key_start(0) + lax.broadcasted_iota(I32, (tk, 2 * tq), 0)
        qpos = q0 + lax.broadcasted_iota(I32, (tk, 2 * tq), 1) % tq
        dist = jnp.abs(qpos - kpos).astype(F32)
        for h in range(DIFF_HEADS):
            score_head(0, h, 2, dist)

    for parity in range(2):
        @pl.when((j > 0) & (j < n_key_tiles) & (j % 2 == parity))
        def _():
            variant = jnp.where(key_start(j) < q0, 0, 1)
            shift = tile_shift(j - 1)
            alphas = []
            for h in range(DIFF_HEADS):
                alphas.append(softmax_head(1 - parity, h, slope_ref[h] * shift))
                score_head(parity, h, variant, None)
                value_head(h, alphas[h])

    @pl.when(j == n_key_tiles)
    def _():
        shift = tile_shift(n_key_tiles - 1)
        slot = (n_key_tiles - 1) % 2
        alphas = [softmax_head(slot, h, slope_ref[h] * shift) for h in range(DIFF_HEADS)]
        for h in range(DIFF_HEADS):
            value_head(h, alphas[h])
        lp = lam_ref[...]
        lam = (jnp.exp(jnp.sum(lp[0:1] * lp[1:2], axis=-1, keepdims=True))
               - jnp.exp(jnp.sum(lp[2:3] * lp[3:4], axis=-1, keepdims=True)) + lambda_init)
        outs = []
        for h in range(DIFF_HEADS):
            a = acc_ref[h]
            o = a[0:DIFF_V_DIM] / a[DIFF_V_DIM:DIFF_V_DIM + 1]
            o = o[:, 0:tq] - lam * o[:, tq:2 * tq]
            ms = jnp.mean(o * o, axis=0, keepdims=True)
            outs.append(o * lax.rsqrt(ms + NORM_EPS) * sw_ref[...] * (1.0 - lambda_init))
        o_ref[...] = jnp.transpose(jnp.concatenate(outs, axis=0)).astype(o_ref.dtype)


def _diff(cqt, ck, cvt, slopes, lam_params, subln_w_col, lambda_init, batch, seq, tq, tk):
    assert tk % tq == 0 and seq % tk == 0
    nq, nk = seq // tq, seq // tk
    t = batch * seq
    smem = pl.BlockSpec(memory_space=pltpu.SMEM)
    key_tile = functools.partial(_diff_key_tile, tq=tq, tk=tk, nk=nk)
    return pl.pallas_call(
        functools.partial(_diff_kernel, lambda_init=lambda_init, n_key_tiles=nk),
        grid=(batch, nq, nk + 1),
        in_specs=[smem,
                  pl.BlockSpec((DIFF_QK_WIDTH, tq), lambda b, i, j: (0, b * nq + i)),
                  pl.BlockSpec((tk, DIFF_HEADS * LANES),
                               lambda b, i, j: (b * nk + key_tile(i, jnp.minimum(j, nk - 1)), 0)),
                  pl.BlockSpec((DIFF_HEADS * DIFF_VROWS, tk),
                               lambda b, i, j: (0, b * nk + key_tile(i, jnp.maximum(j - 1, 0)))),
                  pl.BlockSpec(lam_params.shape, lambda b, i, j: (0, 0)),
                  pl.BlockSpec(subln_w_col.shape, lambda b, i, j: (0, 0))],
        out_specs=pl.BlockSpec((tq, DIFF_V_WIDTH), lambda b, i, j: (b * nq + i, 0)),
        out_shape=jax.ShapeDtypeStruct((t, DIFF_V_WIDTH), BF16),
        scratch_shapes=[pltpu.VMEM((3, DIFF_HEADS, LANES, 2 * tq), BF16),
                        pltpu.VMEM((DIFF_HEADS, 2 * tq), F32),
                        pltpu.VMEM((DIFF_HEADS, DIFF_VROWS, 2 * tq), F32),
                        pltpu.VMEM((2, DIFF_HEADS, tk, 2 * tq), F32),
                        pltpu.VMEM((DIFF_HEADS, tk, 2 * tq), BF16),
                        pltpu.VMEM((2, DIFF_HEADS, 2 * tq), F32)],
        compiler_params=_cparams("parallel", "parallel", "arbitrary"),
        name="diffattn",
    )(slopes, cqt, ck, cvt, lam_params, subln_w_col)


def _ssd_direction(fwd, off, xs_ref, bt_ref, cm_ref, dt_ref, dtb_ref, alog_ref, dsk_ref, y_ref, state_ref):
    q = SSD_CHUNK
    rows = slice(off, off + q)
    lane0 = 0 if fwd else SSD_HEADS
    dt_all = _softplus(dt_ref[rows, :] + dtb_ref[...])
    dta_all = dt_all * -jnp.exp(alog_ref[...])
    row = lax.broadcasted_iota(I32, (q, q), 0)
    col = lax.broadcasted_iota(I32, (q, q), 1)
    keep = (row >= col) if fwd else (row <= col)
    tri = keep.astype(BF16)
    part_hi = dta_all.astype(BF16)
    rest = dta_all - part_hi.astype(F32)
    part_mid = rest.astype(BF16)
    part_lo = (rest - part_mid.astype(F32)).astype(BF16)
    da_all = (jnp.dot(tri, part_hi, preferred_element_type=F32)
              + jnp.dot(tri, part_mid, preferred_element_type=F32)
              + jnp.dot(tri, part_lo, preferred_element_type=F32))
    da_all_t = jnp.transpose(da_all)
    dt_all_t = jnp.transpose(dt_all)
    tot_all = jnp.sum(dta_all, axis=0, keepdims=True)

    xs = xs_ref[rows, :]
    bmt = bt_ref[:, rows]
    cm = cm_ref[rows, :]
    rep = SSD_HEADS // SSD_GROUPS
    ns = SSD_STATE
    lane = lax.broadcasted_iota(I32, (q, LANES), 1)
    low = lane < SSD_HEAD_DIM
    low_n = lax.broadcasted_iota(I32, (ns, LANES), 1) < SSD_HEAD_DIM
    cm_g = [jnp.where((lane // ns) == g, cm, 0.0) for g in range(SSD_GROUPS)]
    bmt_b = bmt.astype(BF16)
    g_mats = [jnp.dot(cm_g[g].astype(BF16), bmt_b, preferred_element_type=F32)
              for g in range(SSD_GROUPS)]
    zeros_n = jnp.zeros((ns, LANES), BF16)
    ys = []
    for pair in range(SSD_HEADS // 2):
        x_pair = xs[:, pair * LANES:(pair + 1) * LANES]
        x_pair_b = x_pair.astype(BF16)
        st = state_ref[pair]
        st_b = st.astype(BF16)
        y_heads, s_heads, keep_heads = [], [], []
        for h in (2 * pair, 2 * pair + 1):
            g = h // rep
            ln = lane0 + h
            dac = da_all[:, ln:ln + 1]
            dar = da_all_t[ln:ln + 1, :]
            dtr = dt_all_t[ln:ln + 1, :]
            tot = tot_all[:, ln:ln + 1]
            dac_b = jnp.broadcast_to(dac, (q, q))
            decay = jnp.exp(jnp.where(keep, dac_b - dar, NEG))
            y = jnp.dot((g_mats[g] * decay * dtr).astype(BF16), x_pair_b, preferred_element_type=F32)
            st_ext = jnp.concatenate([st_b, zeros_n] if g == 0 else [zeros_n, st_b], axis=0)
            c_in = (cm_g[g] * jnp.exp(dac_b)).astype(BF16)
            y_heads.append(y + jnp.dot(c_in, st_ext, preferred_element_type=F32))
            to_end = jnp.exp(tot - dar) * dtr
            b_out = (bmt[g * ns:(g + 1) * ns, :] * to_end).astype(BF16)
            s_heads.append(jnp.dot(b_out, x_pair_b, preferred_element_type=F32))
            keep_heads.append(jnp.exp(tot))
        state_keep = jnp.where(low_n, jnp.broadcast_to(keep_heads[0], (ns, LANES)),
                               jnp.broadcast_to(keep_heads[1], (ns, LANES)))
        state_ref[pair] = st * state_keep + jnp.where(low_n, s_heads[0], s_heads[1])
        y = jnp.where(low, y_heads[0], y_heads[1])
        if fwd:
            y = y + dsk_ref[:, pair * LANES:(pair + 1) * LANES] * x_pair
        ys.append(y)
    y_ref[rows, :] = jnp.concatenate(ys, axis=-1)


def _ssd_kernel(xsf_ref, btf_ref, cmf_ref, dtf_ref, xsb_ref, btb_ref, cmb_ref, dtb_in_ref,
                dtbias_ref, alog_ref, dsk_ref, yf_ref, yb_ref, state_ref):
    @pl.when(pl.program_id(1) == 0)
    def _():
        state_ref[...] = jnp.zeros(state_ref.shape, F32)

    n = SSD_CHUNKS_PER_STEP
    for u in range(n):
        _ssd_direction(True, u * SSD_CHUNK, xsf_ref, btf_ref, cmf_ref, dtf_ref, dtbias_ref, alog_ref, dsk_ref,
                       yf_ref, state_ref.at[0])
        _ssd_direction(False, (n - 1 - u) * SSD_CHUNK, xsb_ref, btb_ref, cmb_ref, dtb_in_ref, dtbias_ref,
                       alog_ref, dsk_ref, yb_ref, state_ref.at[1])


def _ssd(xs, bt, cm, dt_raw, dt_bias, a_log, d_skip, batch, seq):
    q = SSD_CHUNK * SSD_CHUNKS_PER_STEP
    nc = seq // q
    t = batch * seq
    full = lambda b, c: (0, 0)
    fw = lambda b, c: b * nc + c
    bw = lambda b, c: b * nc + nc - 1 - c

    def specs(idx):
        return [pl.BlockSpec((q, SSD_D_INNER), lambda b, c: (idx(b, c), 0)),
                pl.BlockSpec((SSD_BC_DIM, q), lambda b, c: (0, idx(b, c))),
                pl.BlockSpec((q, SSD_BC_DIM), lambda b, c: (idx(b, c), 0)),
                pl.BlockSpec((q, LANES), lambda b, c: (idx(b, c), 0))]

    return pl.pallas_call(
        _ssd_kernel,
        grid=(batch, nc),
        in_specs=specs(fw) + specs(bw) + [pl.BlockSpec(dt_bias.shape, full), pl.BlockSpec(a_log.shape, full),
                                          pl.BlockSpec(d_skip.shape, full)],
        out_specs=[pl.BlockSpec((q, SSD_D_INNER), lambda b, c: (fw(b, c), 0)),
                   pl.BlockSpec((q, SSD_D_INNER), lambda b, c: (bw(b, c), 0))],
        out_shape=[jax.ShapeDtypeStruct((t, SSD_D_INNER), F32), jax.ShapeDtypeStruct((t, SSD_D_INNER), F32)],
        scratch_shapes=[pltpu.VMEM((2, SSD_HEADS // 2, SSD_STATE, 2 * SSD_HEAD_DIM), F32)],
        compiler_params=_cparams("parallel", "arbitrary"),
        name="ssd",
    )(xs, bt, cm, dt_raw, xs, bt, cm, dt_raw, dt_bias, a_log, d_skip)


def _outproj_kernel(x_ref, ya_ref, yf_ref, yb_ref, z_ref, snw_ref, yc_ref, wo_ref, fnw_ref, wrt_ref, br_ref,
                    xn_ref, ri_ref, rf_ref, cnt_ref, tri_ref, carry_ref):
    step = pl.program_id(0)
    tm = x_ref.shape[0]

    @pl.when(step == 0)
    def _():
        carry_ref[...] = jnp.zeros(carry_ref.shape, F32)
        r = lax.broadcasted_iota(I32, (tm, tm), 0)
        cc = lax.broadcasted_iota(I32, (tm, tm), 1)
        tri_ref[...] = (r <= cc).astype(BF16)

    y = (yf_ref[...] + yb_ref[...]) * _silu(z_ref[...])
    yb = _rms(y, snw_ref[...]).astype(BF16)
    acc = jnp.dot(ya_ref[...], wo_ref[0:A_Q_DIM, :], preferred_element_type=F32)
    acc = acc + jnp.dot(yb, wo_ref[A_Q_DIM:A_Q_DIM + SSD_D_INNER, :], preferred_element_type=F32)
    acc = acc + jnp.dot(yc_ref[...], wo_ref[A_Q_DIM + SSD_D_INNER:, :], preferred_element_type=F32)
    xn = x_ref[...] + acc
    xn_ref[...] = xn

    h = _rms(xn, fnw_ref[...])
    h_hi = h.astype(BF16)
    h_lo = (h - h_hi.astype(F32)).astype(BF16)
    nt = (((1,), (1,)), ((), ()))
    logits = (lax.dot_general(wrt_ref[0], h_hi, nt, preferred_element_type=F32)
              + lax.dot_general(wrt_ref[0], h_lo, nt, preferred_element_type=F32)
              + lax.dot_general(wrt_ref[1], h_hi, nt, preferred_element_type=F32)) + br_ref[...]
    ne, epg, ng = N_EXPERTS, EXPERTS_PER_GROUP, N_EXPERT_GROUPS
    gl = logits[ne:ne + ng, :]
    gmax = jnp.max(gl, axis=0, keepdims=True)
    g_sel = jnp.full((1, tm), float(ng - 1), F32)
    for g in range(ng - 2, -1, -1):
        g_sel = jnp.where(gl[g:g + 1, :] == gmax, float(g), g_sel)
    g_gate = 1.0 / jnp.sum(jnp.exp(gl - gmax), axis=0, keepdims=True)
    e_in = logits[0:epg, :]
    for g in range(1, ng):
        e_in = jnp.where(g_sel == float(g), logits[g * epg:(g + 1) * epg, :], e_in)
    sub = lax.broadcasted_iota(I32, (epg, tm), 0).astype(F32)
    m1 = jnp.max(e_in, axis=0, keepdims=True)
    i1 = jnp.min(jnp.where(e_in == m1, sub, float(epg)), axis=0, keepdims=True)
    rest = jnp.where(sub == i1, NEG, e_in)
    m2 = jnp.max(rest, axis=0, keepdims=True)
    i2 = jnp.min(jnp.where(rest == m2, sub, float(epg)), axis=0, keepdims=True)
    r = jnp.exp(m2 - m1)
    c1 = g_gate / (1.0 + r)
    c2 = g_gate * r / (1.0 + r)
    e1 = (g_sel * epg + i1).astype(I32)
    e2 = (g_sel * epg + i2).astype(I32)

    erow = lax.broadcasted_iota(I32, (ne, tm), 0)
    hit1 = erow == e1
    hit2 = erow == e2
    oh = jnp.where(hit1 | hit2, 1.0, 0.0)
    incl = jnp.dot(oh.astype(BF16), tri_ref[...], preferred_element_type=F32)
    before = incl - oh + carry_ref[:, 0:1]
    rank1 = jnp.sum(jnp.where(hit1, before, 0.0), axis=0, keepdims=True)
    rank2 = jnp.sum(jnp.where(hit2, before, 0.0), axis=0, keepdims=True)
    carry_ref[...] = carry_ref[...] + jnp.sum(oh, axis=1, keepdims=True)
    cnt_ref[...] = carry_ref[...]
    zi = jnp.zeros((1, tm), I32)
    ri_ref[...] = jnp.concatenate([e1, e2, rank1.astype(I32), rank2.astype(I32), zi, zi, zi, zi], axis=0)
    zf = jnp.zeros((1, tm), F32)
    rf_ref[...] = jnp.concatenate([c1, c2, zf, zf, zf, zf, zf, zf], axis=0)


def _outproj(x2, ya, yf, yb, z, ssd_norm_w, yc, w_out, ffn_norm_w, wrt, br, tm):
    t, d = x2.shape
    row = lambda i: (i, 0)
    full = lambda i: (0, 0)
    return pl.pallas_call(
        _outproj_kernel,
        grid=(t // tm,),
        in_specs=[pl.BlockSpec((tm, d), row),
                  pl.BlockSpec((tm, A_Q_DIM), row),
                  pl.BlockSpec((tm, SSD_D_INNER), row),
                  pl.BlockSpec((tm, SSD_D_INNER), row),
                  pl.BlockSpec((tm, SSD_D_INNER), row),
                  pl.BlockSpec(ssd_norm_w.shape, full),
                  pl.BlockSpec((tm, DIFF_V_WIDTH), row),
                  pl.BlockSpec(w_out.shape, full),
                  pl.BlockSpec(ffn_norm_w.shape, full),
                  pl.BlockSpec(wrt.shape, lambda i: (0, 0, 0)),
                  pl.BlockSpec(br.shape, full)],
        out_specs=[pl.BlockSpec((tm, d), row),
                   pl.BlockSpec((SUBLANES, tm), lambda i: (0, i)),
                   pl.BlockSpec((SUBLANES, tm), lambda i: (0, i)),
                   pl.BlockSpec((N_EXPERTS, LANES), full)],
        out_shape=[jax.ShapeDtypeStruct((t, d), F32),
                   jax.ShapeDtypeStruct((SUBLANES, t), I32),
                   jax.ShapeDtypeStruct((SUBLANES, t), F32),
                   jax.ShapeDtypeStruct((N_EXPERTS, LANES), F32)],
        scratch_shapes=[pltpu.VMEM((tm, tm), BF16), pltpu.VMEM((N_EXPERTS, LANES), F32)],
        compiler_params=_cparams("arbitrary"),
        name="outproj_router",
    )(x2, ya, yf, yb, z, ssd_norm_w, yc, w_out, ffn_norm_w, wrt, br)


_PAD_PIECES = tuple(1 << b for b in reversed(range(MOE_ROW_TILE.bit_length() - 1)))


def _dispatch_kernel(slot1_ref, slot2_ref, pstart_ref, plen_ref, nused_ref, x_ref, xs_hbm, zero_ref, xbuf, sem):
    i = pl.program_id(0)
    tm = x_ref.shape[0]

    @pl.when(i == 0)
    def _():
        zero_ref[...] = jnp.zeros(zero_ref.shape, F32)

        def pieces(e, wait):
            n = plen_ref[e]
            first = pstart_ref[e]
            off = first + n
            for b in _PAD_PIECES:
                off = off - (n & b)
                dst = pl.ds(pl.multiple_of(off, b), b) if b >= SUBLANES else None
                if dst is not None:
                    @pl.when((n & b) != 0)
                    def _():
                        cp = pltpu.make_async_copy(zero_ref.at[pl.ds(0, b)], xs_hbm.at[dst], sem.at[2])
                        cp.wait() if wait else cp.start()

            for u in range(SUBLANES - 1):
                @pl.when(u < (n & (SUBLANES - 1)))
                def _():
                    cp = pltpu.make_async_copy(zero_ref.at[pl.ds(0, 1)], xs_hbm.at[pl.ds(first + u, 1)],
                                               sem.at[2])
                    cp.wait() if wait else cp.start()

        def tail(tile, wait):
            big = _PAD_PIECES[0]
            for part in range(MOE_ROW_TILE // big):
                dst = xs_hbm.at[pl.ds(pl.multiple_of(tile * MOE_ROW_TILE + part * big, big), big)]
                cp = pltpu.make_async_copy(zero_ref, dst, sem.at[2])
                cp.wait() if wait else cp.start()

        def loop(fn, lo, hi, wait):
            def body(k, carry):
                fn(k, wait)
                return carry

            lax.fori_loop(lo, hi, body, 0)

        n_tiles = xs_hbm.shape[0] // MOE_ROW_TILE
        for wait in (False, True):
            loop(pieces, 0, N_EXPERTS, wait)
            loop(tail, nused_ref[0], n_tiles, wait)

    base = i * tm
    last = pl.num_programs(0) - 1

    def drain(slot):
        for _ in range(2):
            pltpu.make_async_copy(xbuf.at[slot], xs_hbm.at[pl.ds(0, tm)], sem.at[slot]).wait()

    for parity in range(2):
        @pl.when(i % 2 == parity)
        def _():
            buf = xbuf.at[parity]
            buf[...] = x_ref[...]
            for r in range(tm):
                src = buf.at[pl.ds(r, 1)]
                pltpu.make_async_copy(src, xs_hbm.at[pl.ds(slot1_ref[base + r], 1)],
                                      sem.at[parity]).start(priority=0)
                pltpu.make_async_copy(src, xs_hbm.at[pl.ds(slot2_ref[base + r], 1)],
                                      sem.at[parity]).start(priority=1)

            @pl.when(i > 0)
            def _():
                drain(1 - parity)

            @pl.when(i == last)
            def _():
                drain(parity)


def _dispatch(xn, slot1, slot2, pad_start, pad_len, n_used, n_rows, tm):
    t, d = xn.shape
    grid_spec = pltpu.PrefetchScalarGridSpec(
        num_scalar_prefetch=5,
        grid=(t // tm,),
        in_specs=[pl.BlockSpec((tm, d), lambda i, s1, s2, ps, pn, nu: (i, 0))],
        out_specs=pl.BlockSpec(memory_space=pl.ANY),
        scratch_shapes=[pltpu.VMEM((_PAD_PIECES[0], d), F32), pltpu.VMEM((2, tm, d), F32),
                        pltpu.SemaphoreType.DMA((3,))],
    )
    return pl.pallas_call(
        _dispatch_kernel,
        grid_spec=grid_spec,
        out_shape=jax.ShapeDtypeStruct((n_rows, d), F32),
        compiler_params=_cparams("arbitrary"),
        name="moe_dispatch",
    )(slot1, slot2, pad_start, pad_len, n_used, xn)


def _moe_kernel(texp_ref, nused_ref, first_ref, wslot_ref, next_ref, x_ref, fnw_ref, wg_hbm, wu_hbm, wd_hbm,
                y_ref, wg_buf, wu_buf, wd_buf, sem, *, layer):
    i = pl.program_id(0)

    def weight_copies(expert, slot):
        return [pltpu.make_async_copy(hbm.at[layer, expert], buf.at[slot], sem.at[slot])
                for hbm, buf in ((wg_hbm, wg_buf), (wu_hbm, wu_buf), (wd_hbm, wd_buf))]

    @pl.when(i < nused_ref[0])
    def _():
        slot = wslot_ref[i]

        @pl.when(i == 0)
        def _():
            for cp in weight_copies(texp_ref[0], 0):
                cp.start()

        @pl.when(first_ref[i] == 1)
        def _():
            for cp in weight_copies(texp_ref[i], slot):
                cp.wait()

            @pl.when(next_ref[i] >= 0)
            def _():
                for cp in weight_copies(next_ref[i], 1 - slot):
                    cp.start()

        h = _rms(x_ref[...], fnw_ref[...]).astype(BF16)
        hg = jnp.dot(h, wg_buf[slot].astype(BF16), preferred_element_type=F32)
        hu = jnp.dot(h, wu_buf[slot].astype(BF16), preferred_element_type=F32)
        act = (_silu(hg) * hu).astype(BF16)
        y_ref[...] = jnp.dot(act, wd_buf[slot].astype(BF16), preferred_element_type=F32)

    @pl.when(i >= nused_ref[0])
    def _():
        y_ref[...] = jnp.zeros(y_ref.shape, F32)


def _moe(xs, ffn_norm_w, w_gate, w_up, w_down, layer, tile_expert, n_used):
    n_rows, d = xs.shape
    f = w_gate.shape[-1]
    tr = MOE_ROW_TILE

    n_tiles = n_rows // tr

    idx = jnp.arange(n_tiles, dtype=I32)
    prev_expert = jnp.concatenate([jnp.full((1,), -1, I32), tile_expert[:-1]])
    first = ((idx < n_used[0]) & (tile_expert != prev_expert)).astype(I32)
    wslot = ((jnp.cumsum(first) - 1) % 2).astype(I32)
    first_pos = jnp.where(first == 1, idx, n_tiles)
    next_first = jnp.concatenate([lax.cummin(first_pos, reverse=True)[1:], jnp.full((1,), n_tiles, I32)])
    next_expert = jnp.where(next_first < n_tiles, tile_expert[jnp.minimum(next_first, n_tiles - 1)], -1).astype(I32)

    def used(i, nu):
        return jnp.maximum(jnp.minimum(i, nu[0] - 1), 0)

    anyspace = pl.BlockSpec(memory_space=pl.ANY)
    grid_spec = pltpu.PrefetchScalarGridSpec(
        num_scalar_prefetch=5,
        grid=(n_tiles,),
        in_specs=[pl.BlockSpec((tr, d), lambda i, te, nu, fi, ws, nx: (used(i, nu), 0)),
                  pl.BlockSpec(ffn_norm_w.shape, lambda i, te, nu, fi, ws, nx: (0, 0)),
                  anyspace, anyspace, anyspace],
        out_specs=pl.BlockSpec((tr, d), lambda i, te, nu, fi, ws, nx: (i, 0)),
        scratch_shapes=[pltpu.VMEM((2, d, f), F32), pltpu.VMEM((2, d, f), F32), pltpu.VMEM((2, f, d), F32),
                        pltpu.SemaphoreType.DMA((2,))],
    )
    return pl.pallas_call(
        functools.partial(_moe_kernel, layer=layer),
        grid_spec=grid_spec,
        out_shape=jax.ShapeDtypeStruct((n_rows, d), F32),
        compiler_params=_cparams("arbitrary"),
        name="moe_experts",
    )(tile_expert, n_used, first, wslot, next_expert, xs, ffn_norm_w, w_gate, w_up, w_down)


def _combine_kernel(slot1_ref, slot2_ref, x_ref, cw_ref, nw_ref, y_hbm, o_ref, ybuf, sem, *, final_norm):
    i = pl.program_id(0)
    n = pl.num_programs(0)
    tm = x_ref.shape[0]

    def start_gather(tile, slot):
        base = tile * tm
        for r in range(tm):
            pltpu.make_async_copy(y_hbm.at[pl.ds(slot1_ref[base + r], 1)], ybuf.at[slot, 0, pl.ds(r, 1)],
                                  sem.at[slot]).start(priority=0)
            pltpu.make_async_copy(y_hbm.at[pl.ds(slot2_ref[base + r], 1)], ybuf.at[slot, 1, pl.ds(r, 1)],
                                  sem.at[slot]).start(priority=1)

    def compute(slot):
        for k in range(2):
            pltpu.make_async_copy(y_hbm.at[pl.ds(0, tm)], ybuf.at[slot, k], sem.at[slot]).wait()
        cw = cw_ref[...]
        out = x_ref[...] + cw[:, 0:1] * ybuf[slot, 0] + cw[:, 1:2] * ybuf[slot, 1]
        if final_norm:
            out = _rms(out, nw_ref[...])
        o_ref[...] = out

    @pl.when(i == 0)
    def _():
        start_gather(0, 0)

    for parity in range(2):
        @pl.when(i % 2 == parity)
        def _():
            @pl.when(i + 1 < n)
            def _():
                start_gather(i + 1, 1 - parity)

            compute(parity)


def _combine(xn, cw, norm_w, y_sorted, slot1, slot2, tm, final_norm):
    t, d = xn.shape
    grid_spec = pltpu.PrefetchScalarGridSpec(
        num_scalar_prefetch=2,
        grid=(t // tm,),
        in_specs=[pl.BlockSpec((tm, d), lambda i, s1, s2: (i, 0)),
                  pl.BlockSpec((tm, cw.shape[1]), lambda i, s1, s2: (i, 0)),
                  pl.BlockSpec(norm_w.shape, lambda i, s1, s2: (0, 0)),
                  pl.BlockSpec(memory_space=pl.ANY)],
        out_specs=pl.BlockSpec((tm, d), lambda i, s1, s2: (i, 0)),
        scratch_shapes=[pltpu.VMEM((2, 2, tm, d), F32), pltpu.SemaphoreType.DMA((2,))],
    )
    return pl.pallas_call(
        functools.partial(_combine_kernel, final_norm=final_norm),
        grid_spec=grid_spec,
        out_shape=jax.ShapeDtypeStruct((t, d), F32),
        compiler_params=_cparams("arbitrary"),
        name="moe_combine",
    )(slot1, slot2, xn, cw, norm_w, y_sorted)


def _pad_lanes(v):
    v = v.reshape(1, -1).astype(F32)
    return jnp.pad(v, ((0, 0), (0, LANES - v.shape[1])))


def kernel(x, attn_norm_w, w_in, swa_sink, ssd_conv_w, ssd_conv_b, ssd_dt_bias, ssd_a_log, ssd_d, ssd_norm_w,
           diff_lambda, diff_subln_w, w_out, ffn_norm_w, w_router_group, b_router_group, w_router_expert,
           b_router_expert, w_gate, w_up, w_down, final_norm_w):
    return _forward(x, attn_norm_w, w_in, swa_sink, ssd_conv_w, ssd_conv_b, ssd_dt_bias, ssd_a_log, ssd_d,
                    ssd_norm_w, diff_lambda, diff_subln_w, w_out, ffn_norm_w, w_router_group, b_router_group,
                    w_router_expert, b_router_expert, w_gate, w_up, w_down, final_norm_w)


def _forward(x, attn_norm_w, w_in, swa_sink, ssd_conv_w, ssd_conv_b, ssd_dt_bias, ssd_a_log, ssd_d, ssd_norm_w,
             diff_lambda, diff_subln_w, w_out, ffn_norm_w, w_router_group, b_router_group, w_router_expert,
             b_router_expert, w_gate, w_up, w_down, final_norm_w, tm=512, tq=512, tk=1024, tmc=256):
    batch, seq, d = x.shape
    depth = w_in.shape[0]
    t = batch * seq
    tr = MOE_ROW_TILE
    n_tiles = (2 * t) // tr + N_EXPERTS
    slopes = jnp.exp2(-8.0 * jnp.arange(1, N_ALIBI_HEADS + 1, dtype=F32) / N_ALIBI_HEADS)
    swa_slopes, diff_slopes = slopes[:SWA_HEADS], slopes[SWA_HEADS:]

    sizes = [A_Q_DIM, A_KV_DIM, A_KV_DIM, SSD_D_INNER, SSD_CONV_DIM, SSD_DT_DIM, DIFF_QK_WIDTH, DIFF_QK_WIDTH,
             DIFF_V_WIDTH]
    offs = [0]
    for s in sizes:
        offs.append(offs[-1] + s)
    o_aq, o_ak, o_av, o_z, o_xbc, o_dt, o_cq, o_ck, o_cv, o_end = offs

    x2 = x.reshape(t, d)
    for l in range(depth):
        w = w_in[l]
        hw = 2 * DIFF_QK_DIM
        w_ck = jnp.pad(w[:, o_ck:o_cv].reshape(d, DIFF_HEADS, hw), ((0, 0), (0, 0), (0, LANES - hw)))
        w_main = jnp.concatenate(
            [w[:, o_ak:o_av], w[:, o_z:o_dt], w_ck.reshape(d, DIFF_HEADS * LANES), w[:, o_dt:o_cq],
             jnp.zeros((d, LANES - SSD_DT_DIM), w.dtype)], axis=1).astype(BF16)
        w_t = jnp.concatenate([w[:, o_aq:o_ak], w[:, o_av:o_z], w[:, o_cq:o_ck], w[:, o_cv:o_end]],
                              axis=1).T.astype(BF16)
        ak, z, xs, cm, ck, dt_raw, aqt, avt3, cqt, cvt, bt = _inproj(
            x2, attn_norm_w[l].reshape(1, d), w_main, w_t, ssd_conv_w[l].astype(F32),
            ssd_conv_b[l].reshape(1, -1).astype(F32), tm, seq, tk)

        ya = _swa(aqt, ak, avt3, swa_sink[l].astype(F32), swa_slopes, batch, seq)
        lambda_init = 0.8 - 0.6 * math.exp(-0.3 * l)
        yc = _diff(cqt, ck, cvt, diff_slopes * LOG2E, diff_lambda[l].astype(F32),
                   diff_subln_w[l].reshape(DIFF_V_DIM, 1).astype(F32), lambda_init, batch, seq, tq, tk)
        yf, yb = _ssd(xs, bt, cm, dt_raw, _pad_lanes(ssd_dt_bias[l]), _pad_lanes(ssd_a_log[l]),
                      jnp.repeat(ssd_d[l].astype(F32), SSD_HEAD_DIM).reshape(1, SSD_D_INNER), batch, seq)

        wr32 = jnp.concatenate([w_router_expert[l], w_router_group[l],
                                jnp.zeros((d, SUBLANES - N_EXPERT_GROUPS), F32)], axis=1).T.astype(F32)
        wr_hi = wr32.astype(BF16)
        wrt = jnp.stack([wr_hi, (wr32 - wr_hi.astype(F32)).astype(BF16)])
        br = jnp.concatenate([b_router_expert[l], b_router_group[l],
                              jnp.zeros((SUBLANES - N_EXPERT_GROUPS,), F32)]).reshape(-1, 1).astype(F32)
        xn, ri, rf, cnt = _outproj(x2, ya, yf, yb, z, ssd_norm_w[l].reshape(1, -1), yc, w_out[l].astype(BF16),
                                   ffn_norm_w[l].reshape(1, d), wrt, br, tm)

        counts = cnt[:, 0].astype(I32)
        padded = ((counts + tr - 1) // tr) * tr
        ends = jnp.cumsum(padded)
        starts = ends - padded
        experts = jnp.arange(N_EXPERTS, dtype=I32)[:, None]

        def slot_of(e, rank):
            return jnp.sum(jnp.where(e[None, :] == experts, starts[:, None], 0), axis=0) + rank

        slot1 = slot_of(ri[0], ri[2])
        slot2 = slot_of(ri[1], ri[3])
        tile_start = jnp.arange(n_tiles, dtype=I32) * tr
        tile_expert = jnp.minimum(jnp.sum(ends[None, :] <= tile_start[:, None], axis=1), N_EXPERTS - 1).astype(I32)
        n_used = (ends[-1] // tr).astype(I32).reshape(1)

        xs_sorted = _dispatch(xn, slot1, slot2, starts + counts, padded - counts, n_used, n_tiles * tr, tmc)
        y_sorted = _moe(xs_sorted, ffn_norm_w[l].reshape(1, d), w_gate, w_up, w_down, l, tile_expert, n_used)
        last = l == depth - 1
        x2 = _combine(xn, rf.T, final_norm_w.reshape(1, d), y_sorted, slot1, slot2, tmc, last)
    return x2.reshape(batch, seq, d)
```

```python
import functools
import math

import jax
import jax.numpy as jnp
from jax import lax
from jax.experimental import pallas as pl
from jax.experimental.pallas import tpu as pltpu

F32 = jnp.float32
BF16 = jnp.bfloat16
I32 = jnp.int32

HEAD_DIM = 64
SWA_HEADS = 6
SWA_KV_HEADS = 2
SWA_WINDOW = 128
SSD_HEADS = 6
SSD_HEAD_DIM = 64
SSD_GROUPS = 2
SSD_STATE = 64
SSD_CONV = 5
DIFF_HEADS = 4
DIFF_QK_DIM = 32
DIFF_V_DIM = 64
N_EXPERT_GROUPS = 4
EXPERTS_PER_GROUP = 8
N_EXPERTS = N_EXPERT_GROUPS * EXPERTS_PER_GROUP
NORM_EPS = 1e-6

A_Q_DIM = SWA_HEADS * HEAD_DIM
A_KV_DIM = SWA_KV_HEADS * HEAD_DIM
SSD_D_INNER = SSD_HEADS * SSD_HEAD_DIM
SSD_BC_DIM = SSD_GROUPS * SSD_STATE
SSD_CONV_DIM = SSD_D_INNER + 2 * SSD_BC_DIM
SSD_DT_DIM = 2 * SSD_HEADS
DIFF_QK_WIDTH = DIFF_HEADS * 2 * DIFF_QK_DIM
DIFF_V_WIDTH = DIFF_HEADS * DIFF_V_DIM
N_ALIBI_HEADS = SWA_HEADS + DIFF_HEADS

LANES = 128
SUBLANES = 8
VMEM_LIMIT = 56 * 1024 * 1024
NEG = -1e30
LOG2E = math.log2(math.e)

SSD_CHUNK = 128
SSD_CHUNKS_PER_STEP = 4
SWA_BLOCKS_PER_STEP = 4
MOE_ROW_TILE = 256
DIFF_PAIR = 2 * 2 * DIFF_QK_DIM
DIFF_VROWS = 80
DIFF_NFEAT = 6


def _cparams(*sem):
    return pltpu.CompilerParams(dimension_semantics=sem, vmem_limit_bytes=VMEM_LIMIT)


def _rms(x, w):
    return x * lax.rsqrt(jnp.mean(x * x, axis=-1, keepdims=True) + NORM_EPS) * w


def _silu(x):
    return x / (1.0 + jnp.exp(-x))


def _softplus(x):
    return jnp.maximum(x, 0.0) + jnp.log(1.0 + jnp.exp(-jnp.abs(x)))


def _bf16_split(x):
    hi = x.astype(BF16).astype(F32)
    lo = (x - hi).astype(BF16).astype(F32)
    return hi, lo


_C_AK = 0
_C_Z = _C_AK + A_KV_DIM
_C_XBC = _C_Z + SSD_D_INNER
_C_CK = _C_XBC + SSD_CONV_DIM
_C_DT = _C_CK + DIFF_HEADS * LANES
_C_END = _C_DT + LANES
_R_AQ = 0
_R_AV = _R_AQ + A_Q_DIM
_R_CQ = _R_AV + A_KV_DIM
_R_CV = _R_CQ + DIFF_QK_WIDTH
_R_END = _R_CV + DIFF_V_WIDTH


def _inproj_kernel(x_ref, xp_ref, xn_ref, nw_ref, w_ref, wt_ref, cw_ref, cb_ref,
                   ak_ref, z_ref, xs_ref, cm_ref, ck_ref, dt_ref, aqt_ref, avt_ref, cqt_ref, cvt_ref, bt_ref,
                   *, tiles_per_seq, diff_key_tile):
    i = pl.program_id(0)
    tm = x_ref.shape[0]
    nw = nw_ref[...]
    h = _rms(x_ref[...], nw).astype(BF16)

    def seg(lo, hi):
        return jnp.dot(h, w_ref[:, lo:hi], preferred_element_type=F32)

    ak_ref[...] = seg(_C_AK, _C_Z).astype(BF16)
    z_ref[...] = seg(_C_Z, _C_XBC)
    pos = (i * tm + lax.broadcasted_iota(I32, (tm, _C_DT - _C_CK), 0)) % diff_key_tile
    ck_ref[...] = (seg(_C_CK, _C_DT) + _diff_key_features(pos)).astype(BF16)
    dt_ref[...] = seg(_C_DT, _C_END)

    w_xbc = w_ref[:, _C_XBC:_C_CK]
    first = i % tiles_per_seq == 0
    last = i % tiles_per_seq == tiles_per_seq - 1
    prev = jnp.dot(_rms(xp_ref[...], nw).astype(BF16), w_xbc, preferred_element_type=F32)
    nxt = jnp.dot(_rms(xn_ref[...], nw).astype(BF16), w_xbc, preferred_element_type=F32)
    prev = jnp.where(first, 0.0, prev)
    nxt = jnp.where(last, 0.0, nxt)
    ext = jnp.concatenate([prev, seg(_C_XBC, _C_CK), nxt], axis=0)
    half = SSD_CONV // 2
    conv = cb_ref[...]
    for k in range(SSD_CONV):
        off = SUBLANES - half + k
        conv = conv + cw_ref[k:k + 1, :] * ext[off:off + tm, :]
    u = _silu(conv)
    xs_ref[...] = u[:, :SSD_D_INNER]
    bt_ref[...] = jnp.transpose(u[:, SSD_D_INNER:SSD_D_INNER + SSD_BC_DIM])
    cm_ref[...] = u[:, SSD_D_INNER + SSD_BC_DIM:]

    tr = lax.dot_general(wt_ref[...], h, (((1,), (1,)), ((), ())), preferred_element_type=F32)
    aqt_ref[...] = (tr[_R_AQ:_R_AV] * (HEAD_DIM ** -0.5)).astype(BF16)
    avt = tr[_R_AV:_R_CQ].astype(BF16)
    for c in range(tm // LANES):
        avt_ref[c] = avt[:, c * LANES:(c + 1) * LANES]
    cqt_ref[...] = (tr[_R_CQ:_R_CV] * (DIFF_QK_DIM ** -0.5 * LOG2E)).astype(BF16)
    pad = DIFF_VROWS - DIFF_V_DIM
    ones_row = (lax.broadcasted_iota(I32, (pad, tm), 0) == 0).astype(BF16)
    for hh in range(DIFF_HEADS):
        cvt_ref[hh * DIFF_VROWS:hh * DIFF_VROWS + DIFF_V_DIM, :] = (
            tr[_R_CV + hh * DIFF_V_DIM:_R_CV + (hh + 1) * DIFF_V_DIM].astype(BF16))
        cvt_ref[hh * DIFF_VROWS + DIFF_V_DIM:(hh + 1) * DIFF_VROWS, :] = ones_row


def _inproj(x2, norm_w, w_main, w_t, conv_w, conv_b, tm, seq, diff_key_tile):
    t, d = x2.shape
    hb = tm // SUBLANES
    n_hblk = t // SUBLANES
    row = lambda i: (i, 0)
    col = lambda i: (0, i)
    full = lambda i: (0, 0)
    row_outs = [(A_KV_DIM, BF16), (SSD_D_INNER, F32), (SSD_D_INNER, F32), (SSD_BC_DIM, F32),
                (DIFF_HEADS * LANES, BF16), (LANES, F32)]
    out_shape = [jax.ShapeDtypeStruct((t, w), dt) for w, dt in row_outs]
    out_specs = [pl.BlockSpec((tm, w), row) for w, _ in row_outs]
    out_shape += [jax.ShapeDtypeStruct((A_Q_DIM, t), BF16),
                  jax.ShapeDtypeStruct((t // LANES, A_KV_DIM, LANES), BF16),
                  jax.ShapeDtypeStruct((DIFF_QK_WIDTH, t), BF16),
                  jax.ShapeDtypeStruct((DIFF_HEADS * DIFF_VROWS, t), BF16),
                  jax.ShapeDtypeStruct((SSD_BC_DIM, t), F32)]
    out_specs += [pl.BlockSpec((A_Q_DIM, tm), col),
                  pl.BlockSpec((tm // LANES, A_KV_DIM, LANES), lambda i: (i, 0, 0)),
                  pl.BlockSpec((DIFF_QK_WIDTH, tm), col),
                  pl.BlockSpec((DIFF_HEADS * DIFF_VROWS, tm), col),
                  pl.BlockSpec((SSD_BC_DIM, tm), col)]
    return pl.pallas_call(
        functools.partial(_inproj_kernel, tiles_per_seq=seq // tm, diff_key_tile=diff_key_tile),
        grid=(t // tm,),
        in_specs=[pl.BlockSpec((tm, d), row),
                  pl.BlockSpec((SUBLANES, d), lambda i: (jnp.maximum(i * hb - 1, 0), 0)),
                  pl.BlockSpec((SUBLANES, d), lambda i: (jnp.minimum((i + 1) * hb, n_hblk - 1), 0)),
                  pl.BlockSpec((1, d), full),
                  pl.BlockSpec(w_main.shape, full), pl.BlockSpec(w_t.shape, full),
                  pl.BlockSpec(conv_w.shape, full), pl.BlockSpec(conv_b.shape, full)],
        out_specs=out_specs,
        out_shape=out_shape,
        compiler_params=_cparams("parallel"),
        name="inproj",
    )(x2, x2, x2, norm_w, w_main, w_t, conv_w, conv_b)


def _swa_kernel(sink_ref, slope_ref, qt_ref, k_ref, vt_ref, o_ref):
    step = pl.program_id(1)
    s_len = k_ref.shape[0]
    blk = SWA_WINDOW
    band = 3 * blk
    nb = s_len // blk
    rep = SWA_HEADS // SWA_KV_HEADS
    hd = HEAD_DIM
    for u in range(SWA_BLOCKS_PER_STEP):
        n = step * SWA_BLOCKS_PER_STEP + u
        start_blk = jnp.clip(n - 1, 0, nb - 3)
        start = pl.multiple_of(start_blk * blk, blk)
        kb = k_ref[pl.ds(start, band), :]
        v3 = vt_ref[pl.ds(start_blk, 3)]
        vtb = jnp.concatenate([v3[0], v3[1], v3[2]], axis=1)
        qt = qt_ref[:, u * blk:(u + 1) * blk]
        zero = jnp.zeros((hd, rep * blk), BF16)
        grp = [jnp.concatenate([qt[(g * rep + r) * hd:(g * rep + r + 1) * hd] for r in range(rep)], axis=1)
               for g in range(SWA_KV_HEADS)]
        qbd = jnp.concatenate([jnp.concatenate([grp[0], zero], axis=1),
                               jnp.concatenate([zero, grp[1]], axis=1)], axis=0)
        st = jnp.dot(kb, qbd, preferred_element_type=F32)
        kpos = start + lax.broadcasted_iota(I32, (band, blk), 0)
        qpos = n * blk + lax.broadcasted_iota(I32, (band, blk), 1)
        dist_i = jnp.abs(qpos - kpos)
        valid = dist_i <= SWA_WINDOW
        dist = dist_i.astype(F32)
        ps, inv = [], []
        for h in range(SWA_HEADS):
            s = jnp.where(valid, st[:, h * blk:(h + 1) * blk] - slope_ref[h] * dist, NEG)
            sink = sink_ref[h]
            m = jnp.maximum(jnp.max(s, axis=0, keepdims=True), sink)
            p = jnp.exp(s - m)
            inv.append(1.0 / (jnp.sum(p, axis=0, keepdims=True) + jnp.exp(sink - m)))
            ps.append(p.astype(BF16))
        outs = []
        for g in range(SWA_KV_HEADS):
            pg = jnp.concatenate(ps[g * rep:(g + 1) * rep], axis=1)
            og = jnp.dot(vtb[g * hd:(g + 1) * hd, :], pg, preferred_element_type=F32)
            for r in range(rep):
                outs.append(og[:, r * blk:(r + 1) * blk] * inv[g * rep + r])
        o_ref[u * blk:(u + 1) * blk, :] = jnp.transpose(jnp.concatenate(outs, axis=0)).astype(o_ref.dtype)


def _swa(aqt, ak, avt3, sink, slopes, batch, seq):
    blk = SWA_WINDOW
    rows = blk * SWA_BLOCKS_PER_STEP
    steps = seq // rows
    nb = seq // blk
    t = batch * seq
    smem = pl.BlockSpec(memory_space=pltpu.SMEM)
    return pl.pallas_call(
        _swa_kernel,
        grid=(batch, steps),
        in_specs=[smem, smem,
                  pl.BlockSpec((A_Q_DIM, rows), lambda b, s: (0, b * steps + s)),
                  pl.BlockSpec((seq, A_KV_DIM), lambda b, s: (b, 0)),
                  pl.BlockSpec((nb, A_KV_DIM, blk), lambda b, s: (b, 0, 0))],
        out_specs=pl.BlockSpec((rows, A_Q_DIM), lambda b, s: (b * steps + s, 0)),
        out_shape=jax.ShapeDtypeStruct((t, A_Q_DIM), BF16),
        compiler_params=_cparams("parallel", "parallel"),
        name="swa",
    )(sink, slopes, aqt, ak, avt3)


def _diff_key_tile(i, j, tq, tk, nk):
    return ((i * tq) // tk + j) % nk


def _diff_key_features(pos_in_tile):
    lane = lax.broadcasted_iota(I32, pos_in_tile.shape, 1) % LANES - 2 * DIFF_QK_DIM
    coarse = ((pos_in_tile // 16) * 16).astype(F32)
    fine = (pos_in_tile % 16).astype(F32)
    f = lane % DIFF_NFEAT
    feat = jnp.where(f < 2, coarse, jnp.where(f < 4, fine, 1.0))
    return jnp.where((lane >= 0) & (lane < 2 * DIFF_NFEAT), feat, 0.0)


def _diff_kernel(slope_ref, qt_ref, k_ref, vt_ref, lam_ref, sw_ref, o_ref, qtb_ref, m_ref, acc_ref,
                 s_ref, p_ref, mx_ref, *, lambda_init, n_query_tiles, n_key_tiles):
    t = pl.program_id(1)
    n_pairs = n_query_tiles * n_key_tiles
    tq = qt_ref.shape[1]
    tk = k_ref.shape[0]
    dq = DIFF_QK_DIM
    hw = 2 * dq
    nf = DIFF_NFEAT
    nkt = n_key_tiles

    def reset_stats():
        m_ref[...] = jnp.full(m_ref.shape, NEG, F32)
        acc_ref[...] = jnp.zeros(acc_ref.shape, F32)

    def build_queries():
        ii = lax.broadcasted_iota(I32, (1, 2 * tq), 1)
        ii = jnp.where(ii >= tq, ii - tq, ii).astype(F32)
        qt = qt_ref[...]
        col = lax.broadcasted_iota(I32, (hw, 2 * tq), 1)
        row = lax.broadcasted_iota(I32, (hw, 2 * tq), 0)
        own_map = row // dq == col // tq
        for h in range(DIFF_HEADS):
            qh = qt[h * hw:(h + 1) * hw, :]
            qh2 = jnp.where(own_map, jnp.concatenate([qh, qh], axis=1), jnp.zeros((hw, 2 * tq), BF16))
            sl = jnp.full((1, 2 * tq), slope_ref[h], F32)
            s_hi, s_lo = _bf16_split(sl)
            v_hi, v_lo = _bf16_split(-sl * ii)
            rows = jnp.concatenate([s_hi, s_lo, s_hi, s_lo, v_hi, v_lo], axis=0)
            zrow = jnp.zeros((nf, 2 * tq), F32)
            zero = jnp.zeros((hw - 2 * nf, 2 * tq), F32)
            variants = ([rows, zrow], [zrow, -rows], [zrow, zrow])
            for v, pieces in enumerate(variants):
                qtb_ref[v, h, 0:hw, :] = qh2
                qtb_ref[v, h, hw:2 * hw, :] = jnp.concatenate(pieces + [zero], axis=0).astype(BF16)

    def pair_of(step):
        c = jnp.clip(step, 0, n_pairs - 1)
        return c // nkt, c % nkt

    def key_start(qi, step):
        return _diff_key_tile(qi, step, tq, tk, nkt) * tk

    def score_head(slot, h, variant, dist):
        s = jnp.dot(k_ref[:, h * LANES:(h + 1) * LANES], qtb_ref[variant, h], preferred_element_type=F32)
        if dist is not None:
            s = s - slope_ref[h] * dist
        s_ref[slot, h] = s
        mx_ref[slot, h:h + 1, :] = jnp.max(s, axis=0, keepdims=True)

    def softmax_head(slot, h, shift):
        m_old = m_ref[h:h + 1, :]
        m_new = jnp.maximum(m_old, mx_ref[slot, h:h + 1, :] + shift)
        p_ref[h] = jnp.exp2(s_ref[slot, h] - (m_new - shift)).astype(BF16)
        m_ref[h:h + 1, :] = m_new
        return jnp.exp2(m_old - m_new)

    def value_head(h, alpha):
        pv = jnp.dot(vt_ref[h * DIFF_VROWS:(h + 1) * DIFF_VROWS, :], p_ref[h], preferred_element_type=F32)
        acc_ref[h] = alpha * acc_ref[h] + pv

    def tile_shift(step):
        qi, js = pair_of(step)
        k0 = key_start(qi, js)
        q0 = qi * tq
        sign = jnp.where(js == 0, 0.0, jnp.where(k0 < q0, 1.0, -1.0))
        return sign * (k0 - q0).astype(F32)

    def diagonal_scores(slot):
        qi, _ = pair_of(t)
        kpos = key_start(qi, 0) + lax.broadcasted_iota(I32, (tk, 2 * tq), 0)
        qpos = qi * tq + lax.broadcasted_iota(I32, (tk, 2 * tq), 1) % tq
        dist = jnp.abs(qpos - kpos).astype(F32)
        for h in range(DIFF_HEADS):
            score_head(slot, h, 2, dist)

    def finish_query_tile():
        lp = lam_ref[...]
        lam = (jnp.exp(jnp.sum(lp[0:1] * lp[1:2], axis=-1, keepdims=True))
               - jnp.exp(jnp.sum(lp[2:3] * lp[3:4], axis=-1, keepdims=True)) + lambda_init)
        outs = []
        for h in range(DIFF_HEADS):
            a = acc_ref[h]
            o = a[0:DIFF_V_DIM] / a[DIFF_V_DIM:DIFF_V_DIM + 1]
            o = o[:, 0:tq] - lam * o[:, tq:2 * tq]
            ms = jnp.mean(o * o, axis=0, keepdims=True)
            outs.append(o * lax.rsqrt(ms + NORM_EPS) * sw_ref[...] * (1.0 - lambda_init))
        o_ref[...] = jnp.transpose(jnp.concatenate(outs, axis=0)).astype(o_ref.dtype)

    _, step_in_tile = pair_of(t)
    is_first = (t < n_pairs) & (step_in_tile == 0)

    @pl.when(t == 0)
    def _():
        reset_stats()
        build_queries()
        diagonal_scores(0)

    for parity in range(2):
        @pl.when((t > 0) & (t < n_pairs) & jnp.logical_not(is_first) & (t % 2 == parity))
        def _():
            qi, js = pair_of(t)
            variant = jnp.where(key_start(qi, js) < qi * tq, 0, 1)
            shift = tile_shift(t - 1)
            alphas = []
            for h in range(DIFF_HEADS):
                alphas.append(softmax_head(1 - parity, h, slope_ref[h] * shift))
                score_head(parity, h, variant, None)
                value_head(h, alphas[h])

        if nkt % 2 == 0 and parity == 1:
            continue

        @pl.when((t > 0) & is_first & (t % 2 == parity))
        def _():
            shift = tile_shift(t - 1)
            alphas = [softmax_head(1 - parity, h, slope_ref[h] * shift) for h in range(DIFF_HEADS)]
            build_queries()
            diagonal_scores(parity)
            for h in range(DIFF_HEADS):
                value_head(h, alphas[h])
            finish_query_tile()
            reset_stats()

    @pl.when(t == n_pairs)
    def _():
        shift = tile_shift(t - 1)
        alphas = [softmax_head((n_pairs - 1) % 2, h, slope_ref[h] * shift) for h in range(DIFF_HEADS)]
        for h in range(DIFF_HEADS):
            value_head(h, alphas[h])
        finish_query_tile()


def _diff(cqt, ck, cvt, slopes, lam_params, subln_w_col, lambda_init, batch, seq, tq, tk):
    assert tk % tq == 0 and seq % tk == 0
    nq, nk = seq // tq, seq // tk
    t = batch * seq
    smem = pl.BlockSpec(memory_space=pltpu.SMEM)
    n_pairs = nq * nk

    def query_tile(p):
        return jnp.clip(p, 0, n_pairs - 1) // nk

    def key_tile(p):
        c = jnp.clip(p, 0, n_pairs - 1)
        return _diff_key_tile(c // nk, c % nk, tq, tk, nk)

    return pl.pallas_call(
        functools.partial(_diff_kernel, lambda_init=lambda_init, n_query_tiles=nq, n_key_tiles=nk),
        grid=(batch, n_pairs + 1),
        in_specs=[smem,
                  pl.BlockSpec((DIFF_QK_WIDTH, tq), lambda b, p: (0, b * nq + query_tile(p))),
                  pl.BlockSpec((tk, DIFF_HEADS * LANES), lambda b, p: (b * nk + key_tile(p), 0)),
                  pl.BlockSpec((DIFF_HEADS * DIFF_VROWS, tk), lambda b, p: (0, b * nk + key_tile(p - 1))),
                  pl.BlockSpec(lam_params.shape, lambda b, p: (0, 0)),
                  pl.BlockSpec(subln_w_col.shape, lambda b, p: (0, 0))],
        out_specs=pl.BlockSpec((tq, DIFF_V_WIDTH), lambda b, p: (b * nq + query_tile(p - 1), 0)),
        out_shape=jax.ShapeDtypeStruct((t, DIFF_V_WIDTH), BF16),
        scratch_shapes=[pltpu.VMEM((3, DIFF_HEADS, LANES, 2 * tq), BF16),
                        pltpu.VMEM((DIFF_HEADS, 2 * tq), F32),
                        pltpu.VMEM((DIFF_HEADS, DIFF_VROWS, 2 * tq), F32),
                        pltpu.VMEM((2, DIFF_HEADS, tk, 2 * tq), F32),
                        pltpu.VMEM((DIFF_HEADS, tk, 2 * tq), BF16),
                        pltpu.VMEM((2, DIFF_HEADS, 2 * tq), F32)],
        compiler_params=_cparams("parallel", "arbitrary"),
        name="diffattn",
    )(slopes, cqt, ck, cvt, lam_params, subln_w_col)


def _ssd_direction(fwd, off, xs_ref, bt_ref, cm_ref, dt_ref, dtb_ref, alog_ref, dsk_ref, y_ref, state_ref):
    q = SSD_CHUNK
    rows = slice(off, off + q)
    lane0 = 0 if fwd else SSD_HEADS
    dt_all = _softplus(dt_ref[rows, :] + dtb_ref[...])
    dta_all = dt_all * -jnp.exp(alog_ref[...])
    row = lax.broadcasted_iota(I32, (q, q), 0)
    col = lax.broadcasted_iota(I32, (q, q), 1)
    keep = (row >= col) if fwd else (row <= col)
    tri = keep.astype(BF16)
    part_hi = dta_all.astype(BF16)
    rest = dta_all - part_hi.astype(F32)
    part_mid = rest.astype(BF16)
    part_lo = (rest - part_mid.astype(F32)).astype(BF16)
    da_all = (jnp.dot(tri, part_hi, preferred_element_type=F32)
              + jnp.dot(tri, part_mid, preferred_element_type=F32)
              + jnp.dot(tri, part_lo, preferred_element_type=F32))
    da_all_t = jnp.transpose(da_all)
    dt_all_t = jnp.transpose(dt_all)
    tot_all = jnp.sum(dta_all, axis=0, keepdims=True)

    xs = xs_ref[rows, :]
    bmt = bt_ref[:, rows]
    cm = cm_ref[rows, :]
    rep = SSD_HEADS // SSD_GROUPS
    ns = SSD_STATE
    lane = lax.broadcasted_iota(I32, (q, LANES), 1)
    low = lane < SSD_HEAD_DIM
    low_n = lax.broadcasted_iota(I32, (ns, LANES), 1) < SSD_HEAD_DIM
    cm_g = [jnp.where((lane // ns) == g, cm, 0.0) for g in range(SSD_GROUPS)]
    bmt_b = bmt.astype(BF16)
    g_mats = [jnp.dot(cm_g[g].astype(BF16), bmt_b, preferred_element_type=F32)
              for g in range(SSD_GROUPS)]
    zeros_n = jnp.zeros((ns, LANES), BF16)
    ys = []
    for pair in range(SSD_HEADS // 2):
        x_pair = xs[:, pair * LANES:(pair + 1) * LANES]
        x_pair_b = x_pair.astype(BF16)
        st = state_ref[pair]
        st_b = st.astype(BF16)
        y_heads, s_heads, keep_heads = [], [], []
        for h in (2 * pair, 2 * pair + 1):
            g = h // rep
            ln = lane0 + h
            dac = da_all[:, ln:ln + 1]
            dar = da_all_t[ln:ln + 1, :]
            dtr = dt_all_t[ln:ln + 1, :]
            tot = tot_all[:, ln:ln + 1]
            dac_b = jnp.broadcast_to(dac, (q, q))
            decay = jnp.exp(jnp.where(keep, dac_b - dar, NEG))
            y = jnp.dot((g_mats[g] * decay * dtr).astype(BF16), x_pair_b, preferred_element_type=F32)
            st_ext = jnp.concatenate([st_b, zeros_n] if g == 0 else [zeros_n, st_b], axis=0)
            c_in = (cm_g[g] * jnp.exp(dac_b)).astype(BF16)
            y_heads.append(y + jnp.dot(c_in, st_ext, preferred_element_type=F32))
            to_end = jnp.exp(tot - dar) * dtr
            b_out = (bmt[g * ns:(g + 1) * ns, :] * to_end).astype(BF16)
            s_heads.append(jnp.dot(b_out, x_pair_b, preferred_element_type=F32))
            keep_heads.append(jnp.exp(tot))
        state_keep = jnp.where(low_n, jnp.broadcast_to(keep_heads[0], (ns, LANES)),
                               jnp.broadcast_to(keep_heads[1], (ns, LANES)))
        state_ref[pair] = st * state_keep + jnp.where(low_n, s_heads[0], s_heads[1])
        y = jnp.where(low, y_heads[0], y_heads[1])
        if fwd:
            y = y + dsk_ref[:, pair * LANES:(pair + 1) * LANES] * x_pair
        ys.append(y)
    y_ref[rows, :] = jnp.concatenate(ys, axis=-1)


def _ssd_kernel(xsf_ref, btf_ref, cmf_ref, dtf_ref, xsb_ref, btb_ref, cmb_ref, dtb_in_ref,
                dtbias_ref, alog_ref, dsk_ref, yf_ref, yb_ref, state_ref):
    @pl.when(pl.program_id(1) == 0)
    def _():
        state_ref[...] = jnp.zeros(state_ref.shape, F32)

    n = SSD_CHUNKS_PER_STEP
    for u in range(n):
        _ssd_direction(True, u * SSD_CHUNK, xsf_ref, btf_ref, cmf_ref, dtf_ref, dtbias_ref, alog_ref, dsk_ref,
                       yf_ref, state_ref.at[0])
        _ssd_direction(False, (n - 1 - u) * SSD_CHUNK, xsb_ref, btb_ref, cmb_ref, dtb_in_ref, dtbias_ref,
                       alog_ref, dsk_ref, yb_ref, state_ref.at[1])


def _ssd(xs, bt, cm, dt_raw, dt_bias, a_log, d_skip, batch, seq):
    q = SSD_CHUNK * SSD_CHUNKS_PER_STEP
    nc = seq // q
    t = batch * seq
    full = lambda b, c: (0, 0)
    fw = lambda b, c: b * nc + c
    bw = lambda b, c: b * nc + nc - 1 - c

    def specs(idx):
        return [pl.BlockSpec((q, SSD_D_INNER), lambda b, c: (idx(b, c), 0)),
                pl.BlockSpec((SSD_BC_DIM, q), lambda b, c: (0, idx(b, c))),
                pl.BlockSpec((q, SSD_BC_DIM), lambda b, c: (idx(b, c), 0)),
                pl.BlockSpec((q, LANES), lambda b, c: (idx(b, c), 0))]

    return pl.pallas_call(
        _ssd_kernel,
        grid=(batch, nc),
        in_specs=specs(fw) + specs(bw) + [pl.BlockSpec(dt_bias.shape, full), pl.BlockSpec(a_log.shape, full),
                                          pl.BlockSpec(d_skip.shape, full)],
        out_specs=[pl.BlockSpec((q, SSD_D_INNER), lambda b, c: (fw(b, c), 0)),
                   pl.BlockSpec((q, SSD_D_INNER), lambda b, c: (bw(b, c), 0))],
        out_shape=[jax.ShapeDtypeStruct((t, SSD_D_INNER), F32), jax.ShapeDtypeStruct((t, SSD_D_INNER), F32)],
        scratch_shapes=[pltpu.VMEM((2, SSD_HEADS // 2, SSD_STATE, 2 * SSD_HEAD_DIM), F32)],
        compiler_params=_cparams("parallel", "arbitrary"),
        name="ssd",
    )(xs, bt, cm, dt_raw, xs, bt, cm, dt_raw, dt_bias, a_log, d_skip)


def _outproj_kernel(x_ref, ya_ref, yf_ref, yb_ref, z_ref, snw_ref, yc_ref, wo_ref, fnw_ref, wrt_ref, br_ref,
                    xn_ref, ri_ref, rf_ref, cnt_ref, tri_ref, carry_ref):
    step = pl.program_id(0)
    tm = x_ref.shape[0]

    @pl.when(step == 0)
    def _():
        carry_ref[...] = jnp.zeros(carry_ref.shape, F32)
        r = lax.broadcasted_iota(I32, (tm, tm), 0)
        cc = lax.broadcasted_iota(I32, (tm, tm), 1)
        tri_ref[...] = (r <= cc).astype(BF16)

    y = (yf_ref[...] + yb_ref[...]) * _silu(z_ref[...])
    yb = _rms(y, snw_ref[...]).astype(BF16)
    acc = jnp.dot(ya_ref[...], wo_ref[0:A_Q_DIM, :], preferred_element_type=F32)
    acc = acc + jnp.dot(yb, wo_ref[A_Q_DIM:A_Q_DIM + SSD_D_INNER, :], preferred_element_type=F32)
    acc = acc + jnp.dot(yc_ref[...], wo_ref[A_Q_DIM + SSD_D_INNER:, :], preferred_element_type=F32)
    xn = x_ref[...] + acc
    xn_ref[...] = xn

    h = _rms(xn, fnw_ref[...])
    h_hi = h.astype(BF16)
    h_lo = (h - h_hi.astype(F32)).astype(BF16)
    nt = (((1,), (1,)), ((), ()))
    logits = (lax.dot_general(wrt_ref[0], h_hi, nt, preferred_element_type=F32)
              + lax.dot_general(wrt_ref[0], h_lo, nt, preferred_element_type=F32)
              + lax.dot_general(wrt_ref[1], h_hi, nt, preferred_element_type=F32)) + br_ref[...]
    ne, epg, ng = N_EXPERTS, EXPERTS_PER_GROUP, N_EXPERT_GROUPS
    gl = logits[ne:ne + ng, :]
    gmax = jnp.max(gl, axis=0, keepdims=True)
    g_sel = jnp.full((1, tm), float(ng - 1), F32)
    for g in range(ng - 2, -1, -1):
        g_sel = jnp.where(gl[g:g + 1, :] == gmax, float(g), g_sel)
    g_gate = 1.0 / jnp.sum(jnp.exp(gl - gmax), axis=0, keepdims=True)
    e_in = logits[0:epg, :]
    for g in range(1, ng):
        e_in = jnp.where(g_sel == float(g), logits[g * epg:(g + 1) * epg, :], e_in)
    sub = lax.broadcasted_iota(I32, (epg, tm), 0).astype(F32)
    m1 = jnp.max(e_in, axis=0, keepdims=True)
    i1 = jnp.min(jnp.where(e_in == m1, sub, float(epg)), axis=0, keepdims=True)
    rest = jnp.where(sub == i1, NEG, e_in)
    m2 = jnp.max(rest, axis=0, keepdims=True)
    i2 = jnp.min(jnp.where(rest == m2, sub, float(epg)), axis=0, keepdims=True)
    r = jnp.exp(m2 - m1)
    c1 = g_gate / (1.0 + r)
    c2 = g_gate * r / (1.0 + r)
    e1 = (g_sel * epg + i1).astype(I32)
    e2 = (g_sel * epg + i2).astype(I32)

    erow = lax.broadcasted_iota(I32, (ne, tm), 0)
    hit1 = erow == e1
    hit2 = erow == e2
    oh = jnp.where(hit1 | hit2, 1.0, 0.0)
    incl = jnp.dot(oh.astype(BF16), tri_ref[...], preferred_element_type=F32)
    before = incl - oh + carry_ref[:, 0:1]
    rank1 = jnp.sum(jnp.where(hit1, before, 0.0), axis=0, keepdims=True)
    rank2 = jnp.sum(jnp.where(hit2, before, 0.0), axis=0, keepdims=True)
    carry_ref[...] = carry_ref[...] + jnp.sum(oh, axis=1, keepdims=True)
    cnt_ref[...] = carry_ref[...]
    zi = jnp.zeros((1, tm), I32)
    ri_ref[...] = jnp.concatenate([e1, e2, rank1.astype(I32), rank2.astype(I32), zi, zi, zi, zi], axis=0)
    zf = jnp.zeros((1, tm), F32)
    rf_ref[...] = jnp.concatenate([c1, c2, zf, zf, zf, zf, zf, zf], axis=0)


def _outproj(x2, ya, yf, yb, z, ssd_norm_w, yc, w_out, ffn_norm_w, wrt, br, tm):
    t, d = x2.shape
    row = lambda i: (i, 0)
    full = lambda i: (0, 0)
    return pl.pallas_call(
        _outproj_kernel,
        grid=(t // tm,),
        in_specs=[pl.BlockSpec((tm, d), row),
                  pl.BlockSpec((tm, A_Q_DIM), row),
                  pl.BlockSpec((tm, SSD_D_INNER), row),
                  pl.BlockSpec((tm, SSD_D_INNER), row),
                  pl.BlockSpec((tm, SSD_D_INNER), row),
                  pl.BlockSpec(ssd_norm_w.shape, full),
                  pl.BlockSpec((tm, DIFF_V_WIDTH), row),
                  pl.BlockSpec(w_out.shape, full),
                  pl.BlockSpec(ffn_norm_w.shape, full),
                  pl.BlockSpec(wrt.shape, lambda i: (0, 0, 0)),
                  pl.BlockSpec(br.shape, full)],
        out_specs=[pl.BlockSpec((tm, d), row),
                   pl.BlockSpec((SUBLANES, tm), lambda i: (0, i)),
                   pl.BlockSpec((SUBLANES, tm), lambda i: (0, i)),
                   pl.BlockSpec((N_EXPERTS, LANES), full)],
        out_shape=[jax.ShapeDtypeStruct((t, d), F32),
                   jax.ShapeDtypeStruct((SUBLANES, t), I32),
                   jax.ShapeDtypeStruct((SUBLANES, t), F32),
                   jax.ShapeDtypeStruct((N_EXPERTS, LANES), F32)],
        scratch_shapes=[pltpu.VMEM((tm, tm), BF16), pltpu.VMEM((N_EXPERTS, LANES), F32)],
        compiler_params=_cparams("arbitrary"),
        name="outproj_router",
    )(x2, ya, yf, yb, z, ssd_norm_w, yc, w_out, ffn_norm_w, wrt, br)


_PAD_PIECES = tuple(1 << b for b in reversed(range(MOE_ROW_TILE.bit_length() - 1)))


def _dispatch_kernel(slot1_ref, slot2_ref, pstart_ref, plen_ref, nused_ref, x_ref, xs_hbm, zero_ref, xbuf, sem):
    i = pl.program_id(0)
    tm = x_ref.shape[0]

    @pl.when(i == 0)
    def _():
        zero_ref[...] = jnp.zeros(zero_ref.shape, F32)

        def pieces(e, wait):
            n = plen_ref[e]
            first = pstart_ref[e]
            off = first + n
            for b in _PAD_PIECES:
                off = off - (n & b)
                dst = pl.ds(pl.multiple_of(off, b), b) if b >= SUBLANES else None
                if dst is not None:
                    @pl.when((n & b) != 0)
                    def _():
                        cp = pltpu.make_async_copy(zero_ref.at[pl.ds(0, b)], xs_hbm.at[dst], sem.at[2])
                        cp.wait() if wait else cp.start()

            for u in range(SUBLANES - 1):
                @pl.when(u < (n & (SUBLANES - 1)))
                def _():
                    cp = pltpu.make_async_copy(zero_ref.at[pl.ds(0, 1)], xs_hbm.at[pl.ds(first + u, 1)],
                                               sem.at[2])
                    cp.wait() if wait else cp.start()

        def tail(tile, wait):
            big = _PAD_PIECES[0]
            for part in range(MOE_ROW_TILE // big):
                dst = xs_hbm.at[pl.ds(pl.multiple_of(tile * MOE_ROW_TILE + part * big, big), big)]
                cp = pltpu.make_async_copy(zero_ref, dst, sem.at[2])
                cp.wait() if wait else cp.start()

        def loop(fn, lo, hi, wait):
            def body(k, carry):
                fn(k, wait)
                return carry

            lax.fori_loop(lo, hi, body, 0)

        n_tiles = xs_hbm.shape[0] // MOE_ROW_TILE
        for wait in (False, True):
            loop(pieces, 0, N_EXPERTS, wait)
            loop(tail, nused_ref[0], n_tiles, wait)

    base = i * tm
    last = pl.num_programs(0) - 1

    def drain(slot):
        for _ in range(2):
            pltpu.make_async_copy(xbuf.at[slot], xs_hbm.at[pl.ds(0, tm)], sem.at[slot]).wait()

    for parity in range(2):
        @pl.when(i % 2 == parity)
        def _():
            buf = xbuf.at[parity]
            buf[...] = x_ref[...]
            for r in range(tm):
                src = buf.at[pl.ds(r, 1)]
                pltpu.make_async_copy(src, xs_hbm.at[pl.ds(slot1_ref[base + r], 1)],
                                      sem.at[parity]).start(priority=0)
                pltpu.make_async_copy(src, xs_hbm.at[pl.ds(slot2_ref[base + r], 1)],
                                      sem.at[parity]).start(priority=1)

            @pl.when(i > 0)
            def _():
                drain(1 - parity)

            @pl.when(i == last)
            def _():
                drain(parity)


def _dispatch(xn, slot1, slot2, pad_start, pad_len, n_used, n_rows, tm):
    t, d = xn.shape
    grid_spec = pltpu.PrefetchScalarGridSpec(
        num_scalar_prefetch=5,
        grid=(t // tm,),
        in_specs=[pl.BlockSpec((tm, d), lambda i, s1, s2, ps, pn, nu: (i, 0))],
        out_specs=pl.BlockSpec(memory_space=pl.ANY),
        scratch_shapes=[pltpu.VMEM((_PAD_PIECES[0], d), F32), pltpu.VMEM((2, tm, d), F32),
                        pltpu.SemaphoreType.DMA((3,))],
    )
    return pl.pallas_call(
        _dispatch_kernel,
        grid_spec=grid_spec,
        out_shape=jax.ShapeDtypeStruct((n_rows, d), F32),
        compiler_params=_cparams("arbitrary"),
        name="moe_dispatch",
    )(slot1, slot2, pad_start, pad_len, n_used, xn)


def _moe_kernel(texp_ref, nused_ref, first_ref, wslot_ref, next_ref, x_ref, fnw_ref, wg_hbm, wu_hbm, wd_hbm,
                y_ref, wg_buf, wu_buf, wd_buf, sem, *, layer):
    i = pl.program_id(0)

    def weight_copies(expert, slot):
        return [pltpu.make_async_copy(hbm.at[layer, expert], buf.at[slot], sem.at[slot])
                for hbm, buf in ((wg_hbm, wg_buf), (wu_hbm, wu_buf), (wd_hbm, wd_buf))]

    @pl.when(i < nused_ref[0])
    def _():
        slot = wslot_ref[i]

        @pl.when(i == 0)
        def _():
            for cp in weight_copies(texp_ref[0], 0):
                cp.start()

        @pl.when(first_ref[i] == 1)
        def _():
            for cp in weight_copies(texp_ref[i], slot):
                cp.wait()

            @pl.when(next_ref[i] >= 0)
            def _():
                for cp in weight_copies(next_ref[i], 1 - slot):
                    cp.start()

        h = _rms(x_ref[...], fnw_ref[...]).astype(BF16)
        hg = jnp.dot(h, wg_buf[slot].astype(BF16), preferred_element_type=F32)
        hu = jnp.dot(h, wu_buf[slot].astype(BF16), preferred_element_type=F32)
        act = (_silu(hg) * hu).astype(BF16)
        y_ref[...] = jnp.dot(act, wd_buf[slot].astype(BF16), preferred_element_type=F32)

    @pl.when(i >= nused_ref[0])
    def _():
        y_ref[...] = jnp.zeros(y_ref.shape, F32)


def _moe(xs, ffn_norm_w, w_gate, w_up, w_down, layer, tile_expert, n_used):
    n_rows, d = xs.shape
    f = w_gate.shape[-1]
    tr = MOE_ROW_TILE

    n_tiles = n_rows // tr

    idx = jnp.arange(n_tiles, dtype=I32)
    prev_expert = jnp.concatenate([jnp.full((1,), -1, I32), tile_expert[:-1]])
    first = ((idx < n_used[0]) & (tile_expert != prev_expert)).astype(I32)
    wslot = ((jnp.cumsum(first) - 1) % 2).astype(I32)
    first_pos = jnp.where(first == 1, idx, n_tiles)
    next_first = jnp.concatenate([lax.cummin(first_pos, reverse=True)[1:], jnp.full((1,), n_tiles, I32)])
    next_expert = jnp.where(next_first < n_tiles, tile_expert[jnp.minimum(next_first, n_tiles - 1)], -1).astype(I32)

    def used(i, nu):
        return jnp.maximum(jnp.minimum(i, nu[0] - 1), 0)

    anyspace = pl.BlockSpec(memory_space=pl.ANY)
    grid_spec = pltpu.PrefetchScalarGridSpec(
        num_scalar_prefetch=5,
        grid=(n_tiles,),
        in_specs=[pl.BlockSpec((tr, d), lambda i, te, nu, fi, ws, nx: (used(i, nu), 0)),
                  pl.BlockSpec(ffn_norm_w.shape, lambda i, te, nu, fi, ws, nx: (0, 0)),
                  anyspace, anyspace, anyspace],
        out_specs=pl.BlockSpec((tr, d), lambda i, te, nu, fi, ws, nx: (i, 0)),
        scratch_shapes=[pltpu.VMEM((2, d, f), F32), pltpu.VMEM((2, d, f), F32), pltpu.VMEM((2, f, d), F32),
                        pltpu.SemaphoreType.DMA((2,))],
    )
    return pl.pallas_call(
        functools.partial(_moe_kernel, layer=layer),
        grid_spec=grid_spec,
        out_shape=jax.ShapeDtypeStruct((n_rows, d), F32),
        compiler_params=_cparams("arbitrary"),
        name="moe_experts",
    )(tile_expert, n_used, first, wslot, next_expert, xs, ffn_norm_w, w_gate, w_up, w_down)


def _combine_kernel(slot1_ref, slot2_ref, x_ref, cw_ref, nw_ref, y_hbm, o_ref, ybuf, sem, *, final_norm):
    i = pl.program_id(0)
    n = pl.num_programs(0)
    tm = x_ref.shape[0]

    def start_gather(tile, slot):
        base = tile * tm
        for r in range(tm):
            pltpu.make_async_copy(y_hbm.at[pl.ds(slot1_ref[base + r], 1)], ybuf.at[slot, 0, pl.ds(r, 1)],
                                  sem.at[slot]).start(priority=0)
            pltpu.make_async_copy(y_hbm.at[pl.ds(slot2_ref[base + r], 1)], ybuf.at[slot, 1, pl.ds(r, 1)],
                                  sem.at[slot]).start(priority=1)

    def compute(slot):
        for k in range(2):
            pltpu.make_async_copy(y_hbm.at[pl.ds(0, tm)], ybuf.at[slot, k], sem.at[slot]).wait()
        cw = cw_ref[...]
        out = x_ref[...] + cw[:, 0:1] * ybuf[slot, 0] + cw[:, 1:2] * ybuf[slot, 1]
        if final_norm:
            out = _rms(out, nw_ref[...])
        o_ref[...] = out

    @pl.when(i == 0)
    def _():
        start_gather(0, 0)

    for parity in range(2):
        @pl.when(i % 2 == parity)
        def _():
            @pl.when(i + 1 < n)
            def _():
                start_gather(i + 1, 1 - parity)

            compute(parity)


def _combine(xn, cw, norm_w, y_sorted, slot1, slot2, tm, final_norm):
    t, d = xn.shape
    grid_spec = pltpu.PrefetchScalarGridSpec(
        num_scalar_prefetch=2,
        grid=(t // tm,),
        in_specs=[pl.BlockSpec((tm, d), lambda i, s1, s2: (i, 0)),
                  pl.BlockSpec((tm, cw.shape[1]), lambda i, s1, s2: (i, 0)),
                  pl.BlockSpec(norm_w.shape, lambda i, s1, s2: (0, 0)),
                  pl.BlockSpec(memory_space=pl.ANY)],
        out_specs=pl.BlockSpec((tm, d), lambda i, s1, s2: (i, 0)),
        scratch_shapes=[pltpu.VMEM((2, 2, tm, d), F32), pltpu.SemaphoreType.DMA((2,))],
    )
    return pl.pallas_call(
        functools.partial(_combine_kernel, final_norm=final_norm),
        grid_spec=grid_spec,
        out_shape=jax.ShapeDtypeStruct((t, d), F32),
        compiler_params=_cparams("arbitrary"),
        name="moe_combine",
    )(slot1, slot2, xn, cw, norm_w, y_sorted)


def _pad_lanes(v):
    v = v.reshape(1, -1).astype(F32)
    return jnp.pad(v, ((0, 0), (0, LANES - v.shape[1])))


def kernel(x, attn_norm_w, w_in, swa_sink, ssd_conv_w, ssd_conv_b, ssd_dt_bias, ssd_a_log, ssd_d, ssd_norm_w,
           diff_lambda, diff_subln_w, w_out, ffn_norm_w, w_router_group, b_router_group, w_router_expert,
           b_router_expert, w_gate, w_up, w_down, final_norm_w):
    return _forward(x, attn_norm_w, w_in, swa_sink, ssd_conv_w, ssd_conv_b, ssd_dt_bias, ssd_a_log, ssd_d,
                    ssd_norm_w, diff_lambda, diff_subln_w, w_out, ffn_norm_w, w_router_group, b_router_group,
                    w_router_expert, b_router_expert, w_gate, w_up, w_down, final_norm_w)


def _forward(x, attn_norm_w, w_in, swa_sink, ssd_conv_w, ssd_conv_b, ssd_dt_bias, ssd_a_log, ssd_d, ssd_norm_w,
             diff_lambda, diff_subln_w, w_out, ffn_norm_w, w_router_group, b_router_group, w_router_expert,
             b_router_expert, w_gate, w_up, w_down, final_norm_w, tm=512, tq=512, tk=1024, tmc=256):
    batch, seq, d = x.shape
    depth = w_in.shape[0]
    t = batch * seq
    tr = MOE_ROW_TILE
    n_tiles = (2 * t) // tr + N_EXPERTS
    slopes = jnp.exp2(-8.0 * jnp.arange(1, N_ALIBI_HEADS + 1, dtype=F32) / N_ALIBI_HEADS)
    swa_slopes, diff_slopes = slopes[:SWA_HEADS], slopes[SWA_HEADS:]

    sizes = [A_Q_DIM, A_KV_DIM, A_KV_DIM, SSD_D_INNER, SSD_CONV_DIM, SSD_DT_DIM, DIFF_QK_WIDTH, DIFF_QK_WIDTH,
             DIFF_V_WIDTH]
    offs = [0]
    for s in sizes:
        offs.append(offs[-1] + s)
    o_aq, o_ak, o_av, o_z, o_xbc, o_dt, o_cq, o_ck, o_cv, o_end = offs

    x2 = x.reshape(t, d)
    for l in range(depth):
        w = w_in[l]
        hw = 2 * DIFF_QK_DIM
        w_ck = jnp.pad(w[:, o_ck:o_cv].reshape(d, DIFF_HEADS, hw), ((0, 0), (0, 0), (0, LANES - hw)))
        w_main = jnp.concatenate(
            [w[:, o_ak:o_av], w[:, o_z:o_dt], w_ck.reshape(d, DIFF_HEADS * LANES), w[:, o_dt:o_cq],
             jnp.zeros((d, LANES - SSD_DT_DIM), w.dtype)], axis=1).astype(BF16)
        w_t = jnp.concatenate([w[:, o_aq:o_ak], w[:, o_av:o_z], w[:, o_cq:o_ck], w[:, o_cv:o_end]],
                              axis=1).T.astype(BF16)
        ak, z, xs, cm, ck, dt_raw, aqt, avt3, cqt, cvt, bt = _inproj(
            x2, attn_norm_w[l].reshape(1, d), w_main, w_t, ssd_conv_w[l].astype(F32),
            ssd_conv_b[l].reshape(1, -1).astype(F32), tm, seq, tk)

        ya = _swa(aqt, ak, avt3, swa_sink[l].astype(F32), swa_slopes, batch, seq)
        lambda_init = 0.8 - 0.6 * math.exp(-0.3 * l)
        yc = _diff(cqt, ck, cvt, diff_slopes * LOG2E, diff_lambda[l].astype(F32),
                   diff_subln_w[l].reshape(DIFF_V_DIM, 1).astype(F32), lambda_init, batch, seq, tq, tk)
        yf, yb = _ssd(xs, bt, cm, dt_raw, _pad_lanes(ssd_dt_bias[l]), _pad_lanes(ssd_a_log[l]),
                      jnp.repeat(ssd_d[l].astype(F32), SSD_HEAD_DIM).reshape(1, SSD_D_INNER), batch, seq)

        wr32 = jnp.concatenate([w_router_expert[l], w_router_group[l],
                                jnp.zeros((d, SUBLANES - N_EXPERT_GROUPS), F32)], axis=1).T.astype(F32)
        wr_hi = wr32.astype(BF16)
        wrt = jnp.stack([wr_hi, (wr32 - wr_hi.astype(F32)).astype(BF16)])
        br = jnp.concatenate([b_router_expert[l], b_router_group[l],
                              jnp.zeros((SUBLANES - N_EXPERT_GROUPS,), F32)]).reshape(-1, 1).astype(F32)
        xn, ri, rf, cnt = _outproj(x2, ya, yf, yb, z, ssd_norm_w[l].reshape(1, -1), yc, w_out[l].astype(BF16),
                                   ffn_norm_w[l].reshape(1, d), wrt, br, tm)

        counts = cnt[:, 0].astype(I32)
        padded = ((counts + tr - 1) // tr) * tr
        ends = jnp.cumsum(padded)
        starts = ends - padded
        experts = jnp.arange(N_EXPERTS, dtype=I32)[:, None]

        def slot_of(e, rank):
            return jnp.sum(jnp.where(e[None, :] == experts, starts[:, None], 0), axis=0) + rank

        slot1 = slot_of(ri[0], ri[2])
        slot2 = slot_of(ri[1], ri[3])
        tile_start = jnp.arange(n_tiles, dtype=I32) * tr
        tile_expert = jnp.minimum(jnp.sum(ends[None, :] <= tile_start[:, None], axis=1), N_EXPERTS - 1).astype(I32)
        n_used = (ends[-1] // tr).astype(I32).reshape(1)

        xs_sorted = _dispatch(xn, slot1, slot2, starts + counts, padded - counts, n_used, n_tiles * tr, tmc)
        y_sorted = _moe(xs_sorted, ffn_norm_w[l].reshape(1, d), w_gate, w_up, w_down, l, tile_expert, n_used)
        last = l == depth - 1
        x2 = _combine(xn, rf.T, final_norm_w.reshape(1, d), y_sorted, slot1, slot2, tmc, last)
    return x2.reshape(batch, seq, d)
```

```python
import functools
import math

import jax
import jax.numpy as jnp
from jax import lax
from jax.experimental import pallas as pl
from jax.experimental.pallas import tpu as pltpu

F32 = jnp.float32
BF16 = jnp.bfloat16
I32 = jnp.int32

HEAD_DIM = 64
SWA_HEADS = 6
SWA_KV_HEADS = 2
SWA_WINDOW = 128
SSD_HEADS = 6
SSD_HEAD_DIM = 64
SSD_GROUPS = 2
SSD_STATE = 64
SSD_CONV = 5
DIFF_HEADS = 4
DIFF_QK_DIM = 32
DIFF_V_DIM = 64
N_EXPERT_GROUPS = 4
EXPERTS_PER_GROUP = 8
N_EXPERTS = N_EXPERT_GROUPS * EXPERTS_PER_GROUP
NORM_EPS = 1e-6

A_Q_DIM = SWA_HEADS * HEAD_DIM
A_KV_DIM = SWA_KV_HEADS * HEAD_DIM
SSD_D_INNER = SSD_HEADS * SSD_HEAD_DIM
SSD_BC_DIM = SSD_GROUPS * SSD_STATE
SSD_CONV_DIM = SSD_D_INNER + 2 * SSD_BC_DIM
SSD_DT_DIM = 2 * SSD_HEADS
DIFF_QK_WIDTH = DIFF_HEADS * 2 * DIFF_QK_DIM
DIFF_V_WIDTH = DIFF_HEADS * DIFF_V_DIM
N_ALIBI_HEADS = SWA_HEADS + DIFF_HEADS

LANES = 128
SUBLANES = 8
VMEM_LIMIT = 56 * 1024 * 1024
NEG = -1e30
LOG2E = math.log2(math.e)

SSD_CHUNK = 128
SSD_CHUNKS_PER_STEP = 4
SWA_BLOCKS_PER_STEP = 8
MOE_ROW_TILE = 256
DIFF_PAIR = 2 * 2 * DIFF_QK_DIM
DIFF_VROWS = 80
DIFF_NFEAT = 6


def _cparams(*sem):
    return pltpu.CompilerParams(dimension_semantics=sem, vmem_limit_bytes=VMEM_LIMIT)


def _rms(x, w):
    return x * lax.rsqrt(jnp.mean(x * x, axis=-1, keepdims=True) + NORM_EPS) * w


def _silu(x):
    return x / (1.0 + jnp.exp(-x))


def _softplus(x):
    return jnp.maximum(x, 0.0) + jnp.log(1.0 + jnp.exp(-jnp.abs(x)))


def _bf16_split(x):
    hi = x.astype(BF16).astype(F32)
    lo = (x - hi).astype(BF16).astype(F32)
    return hi, lo


_C_AK = 0
_C_Z = _C_AK + A_KV_DIM
_C_XBC = _C_Z + SSD_D_INNER
_C_CK = _C_XBC + SSD_CONV_DIM
_C_DT = _C_CK + DIFF_HEADS * LANES
_C_END = _C_DT + LANES
_R_AQ = 0
_R_AV = _R_AQ + A_Q_DIM
_R_CQ = _R_AV + A_KV_DIM
_R_CV = _R_CQ + DIFF_QK_WIDTH
_R_END = _R_CV + DIFF_V_WIDTH


def _inproj_kernel(x_ref, xp_ref, xn_ref, nw_ref, w_ref, wt_ref, cw_ref, cb_ref,
                   ak_ref, z_ref, xs_ref, cm_ref, ck_ref, dt_ref, aqt_ref, avt_ref, cqt_ref, cvt_ref, bt_ref,
                   *, tiles_per_seq, diff_key_tile):
    i = pl.program_id(0)
    tm = x_ref.shape[0]
    nw = nw_ref[...]
    h = _rms(x_ref[...], nw).astype(BF16)

    def seg(lo, hi):
        return jnp.dot(h, w_ref[:, lo:hi], preferred_element_type=F32)

    ak_ref[...] = seg(_C_AK, _C_Z).astype(BF16)
    z_ref[...] = seg(_C_Z, _C_XBC)
    pos = (i * tm + lax.broadcasted_iota(I32, (tm, _C_DT - _C_CK), 0)) % diff_key_tile
    ck_ref[...] = (seg(_C_CK, _C_DT) + _diff_key_features(pos)).astype(BF16)
    dt_ref[...] = seg(_C_DT, _C_END)

    w_xbc = w_ref[:, _C_XBC:_C_CK]
    first = i % tiles_per_seq == 0
    last = i % tiles_per_seq == tiles_per_seq - 1
    prev = jnp.dot(_rms(xp_ref[...], nw).astype(BF16), w_xbc, preferred_element_type=F32)
    nxt = jnp.dot(_rms(xn_ref[...], nw).astype(BF16), w_xbc, preferred_element_type=F32)
    prev = jnp.where(first, 0.0, prev)
    nxt = jnp.where(last, 0.0, nxt)
    ext = jnp.concatenate([prev, seg(_C_XBC, _C_CK), nxt], axis=0)
    half = SSD_CONV // 2
    conv = cb_ref[...]
    for k in range(SSD_CONV):
        off = SUBLANES - half + k
        conv = conv + cw_ref[k:k + 1, :] * ext[off:off + tm, :]
    u = _silu(conv)
    xs_ref[...] = u[:, :SSD_D_INNER]
    bt_ref[...] = jnp.transpose(u[:, SSD_D_INNER:SSD_D_INNER + SSD_BC_DIM])
    cm_ref[...] = u[:, SSD_D_INNER + SSD_BC_DIM:]

    tr = lax.dot_general(wt_ref[...], h, (((1,), (1,)), ((), ())), preferred_element_type=F32)
    aqt_ref[...] = (tr[_R_AQ:_R_AV] * (HEAD_DIM ** -0.5 * LOG2E)).astype(BF16)
    avt = tr[_R_AV:_R_CQ].astype(BF16)
    for c in range(tm // LANES):
        avt_ref[c] = avt[:, c * LANES:(c + 1) * LANES]
    cqt_ref[...] = (tr[_R_CQ:_R_CV] * (DIFF_QK_DIM ** -0.5 * LOG2E)).astype(BF16)
    pad = DIFF_VROWS - DIFF_V_DIM
    ones_row = (lax.broadcasted_iota(I32, (pad, tm), 0) == 0).astype(BF16)
    for hh in range(DIFF_HEADS):
        cvt_ref[hh * DIFF_VROWS:hh * DIFF_VROWS + DIFF_V_DIM, :] = (
            tr[_R_CV + hh * DIFF_V_DIM:_R_CV + (hh + 1) * DIFF_V_DIM].astype(BF16))
        cvt_ref[hh * DIFF_VROWS + DIFF_V_DIM:(hh + 1) * DIFF_VROWS, :] = ones_row


def _inproj(x2, norm_w, w_main, w_t, conv_w, conv_b, tm, seq, diff_key_tile):
    t, d = x2.shape
    hb = tm // SUBLANES
    n_hblk = t // SUBLANES
    row = lambda i: (i, 0)
    col = lambda i: (0, i)
    full = lambda i: (0, 0)
    row_outs = [(A_KV_DIM, BF16), (SSD_D_INNER, F32), (SSD_D_INNER, F32), (SSD_BC_DIM, F32),
                (DIFF_HEADS * LANES, BF16), (LANES, F32)]
    out_shape = [jax.ShapeDtypeStruct((t, w), dt) for w, dt in row_outs]
    out_specs = [pl.BlockSpec((tm, w), row) for w, _ in row_outs]
    out_shape += [jax.ShapeDtypeStruct((A_Q_DIM, t), BF16),
                  jax.ShapeDtypeStruct((t // LANES, A_KV_DIM, LANES), BF16),
                  jax.ShapeDtypeStruct((DIFF_QK_WIDTH, t), BF16),
                  jax.ShapeDtypeStruct((DIFF_HEADS * DIFF_VROWS, t), BF16),
                  jax.ShapeDtypeStruct((SSD_BC_DIM, t), F32)]
    out_specs += [pl.BlockSpec((A_Q_DIM, tm), col),
                  pl.BlockSpec((tm // LANES, A_KV_DIM, LANES), lambda i: (i, 0, 0)),
                  pl.BlockSpec((DIFF_QK_WIDTH, tm), col),
                  pl.BlockSpec((DIFF_HEADS * DIFF_VROWS, tm), col),
                  pl.BlockSpec((SSD_BC_DIM, tm), col)]
    return pl.pallas_call(
        functools.partial(_inproj_kernel, tiles_per_seq=seq // tm, diff_key_tile=diff_key_tile),
        grid=(t // tm,),
        in_specs=[pl.BlockSpec((tm, d), row),
                  pl.BlockSpec((SUBLANES, d), lambda i: (jnp.maximum(i * hb - 1, 0), 0)),
                  pl.BlockSpec((SUBLANES, d), lambda i: (jnp.minimum((i + 1) * hb, n_hblk - 1), 0)),
                  pl.BlockSpec((1, d), full),
                  pl.BlockSpec(w_main.shape, full), pl.BlockSpec(w_t.shape, full),
                  pl.BlockSpec(conv_w.shape, full), pl.BlockSpec(conv_b.shape, full)],
        out_specs=out_specs,
        out_shape=out_shape,
        compiler_params=_cparams("parallel"),
        name="inproj",
    )(x2, x2, x2, norm_w, w_main, w_t, conv_w, conv_b)


def _swa_kernel(sink_ref, slope_ref, qt_ref, k_ref, vt_ref, o_ref):
    step = pl.program_id(1)
    s_len = k_ref.shape[0]
    blk = SWA_WINDOW
    band = 3 * blk
    nb = s_len // blk
    rep = SWA_HEADS // SWA_KV_HEADS
    hd = HEAD_DIM
    blocks_per_step = qt_ref.shape[1] // blk
    for u in range(blocks_per_step):
        n = step * blocks_per_step + u
        start_blk = jnp.clip(n - 1, 0, nb - 3)
        start = pl.multiple_of(start_blk * blk, blk)
        kb = k_ref[pl.ds(start, band), :]
        v3 = vt_ref[pl.ds(start_blk, 3)]
        vtb = jnp.concatenate([v3[0], v3[1], v3[2]], axis=1)
        qt = qt_ref[:, u * blk:(u + 1) * blk]
        zero = jnp.zeros((hd, rep * blk), BF16)
        grp = [jnp.concatenate([qt[(g * rep + r) * hd:(g * rep + r + 1) * hd] for r in range(rep)], axis=1)
               for g in range(SWA_KV_HEADS)]
        qbd = jnp.concatenate([jnp.concatenate([grp[0], zero], axis=1),
                               jnp.concatenate([zero, grp[1]], axis=1)], axis=0)
        st = jnp.dot(kb, qbd, preferred_element_type=F32)
        kpos = start + lax.broadcasted_iota(I32, (band, blk), 0)
        qpos = n * blk + lax.broadcasted_iota(I32, (band, blk), 1)
        dist_i = jnp.abs(qpos - kpos)
        valid = dist_i <= SWA_WINDOW
        dist = dist_i.astype(F32)
        ps, inv = [], []
        for h in range(SWA_HEADS):
            s = jnp.where(valid, st[:, h * blk:(h + 1) * blk] - slope_ref[h] * dist, NEG)
            sink = sink_ref[h]
            m = jnp.maximum(jnp.max(s, axis=0, keepdims=True), sink)
            p = jnp.exp2(s - m)
            inv.append(1.0 / (jnp.sum(p, axis=0, keepdims=True) + jnp.exp2(sink - m)))
            ps.append(p.astype(BF16))
        outs = []
        for g in range(SWA_KV_HEADS):
            pg = jnp.concatenate(ps[g * rep:(g + 1) * rep], axis=1)
            og = jnp.dot(vtb[g * hd:(g + 1) * hd, :], pg, preferred_element_type=F32)
            for r in range(rep):
                outs.append(og[:, r * blk:(r + 1) * blk] * inv[g * rep + r])
        o_ref[u * blk:(u + 1) * blk, :] = jnp.transpose(jnp.concatenate(outs, axis=0)).astype(o_ref.dtype)


def _swa(aqt, ak, avt3, sink, slopes, batch, seq):
    blk = SWA_WINDOW
    rows = min(blk * SWA_BLOCKS_PER_STEP, seq)
    steps = seq // rows
    nb = seq // blk
    t = batch * seq
    smem = pl.BlockSpec(memory_space=pltpu.SMEM)
    return pl.pallas_call(
        _swa_kernel,
        grid=(batch, steps),
        in_specs=[smem, smem,
                  pl.BlockSpec((A_Q_DIM, rows), lambda b, s: (0, b * steps + s)),
                  pl.BlockSpec((seq, A_KV_DIM), lambda b, s: (b, 0)),
                  pl.BlockSpec((nb, A_KV_DIM, blk), lambda b, s: (b, 0, 0))],
        out_specs=pl.BlockSpec((rows, A_Q_DIM), lambda b, s: (b * steps + s, 0)),
        out_shape=jax.ShapeDtypeStruct((t, A_Q_DIM), BF16),
        compiler_params=_cparams("parallel", "parallel"),
        name="swa",
    )(sink, slopes, aqt, ak, avt3)


def _diff_key_tile(i, j, tq, tk, nk):
    return ((i * tq) // tk + j) % nk


def _diff_key_features(pos_in_tile):
    lane = lax.broadcasted_iota(I32, pos_in_tile.shape, 1) % LANES - 2 * DIFF_QK_DIM
    coarse = ((pos_in_tile // 16) * 16).astype(F32)
    fine = (pos_in_tile % 16).astype(F32)
    f = lane % DIFF_NFEAT
    feat = jnp.where(f < 2, coarse, jnp.where(f < 4, fine, 1.0))
    return jnp.where((lane >= 0) & (lane < 2 * DIFF_NFEAT), feat, 0.0)


def _diff_kernel(slope_ref, qt_ref, k_ref, vt_ref, lam_ref, sw_ref, o_ref, qtb_ref, m_ref, acc_ref,
                 s_ref, p_ref, mx_ref, *, lambda_init, n_query_tiles, n_key_tiles):
    t = pl.program_id(1)
    n_pairs = n_query_tiles * n_key_tiles
    tq = qt_ref.shape[1]
    tk = k_ref.shape[0]
    dq = DIFF_QK_DIM
    hw = 2 * dq
    nf = DIFF_NFEAT
    nkt = n_key_tiles

    def reset_stats():
        m_ref[...] = jnp.full(m_ref.shape, NEG, F32)
        acc_ref[...] = jnp.zeros(acc_ref.shape, F32)

    def build_queries():
        ii = lax.broadcasted_iota(I32, (1, 2 * tq), 1)
        ii = jnp.where(ii >= tq, ii - tq, ii).astype(F32)
        qt = qt_ref[...]
        col = lax.broadcasted_iota(I32, (hw, 2 * tq), 1)
        row = lax.broadcasted_iota(I32, (hw, 2 * tq), 0)
        own_map = row // dq == col // tq
        for h in range(DIFF_HEADS):
            qh = qt[h * hw:(h + 1) * hw, :]
            qh2 = jnp.where(own_map, jnp.concatenate([qh, qh], axis=1), jnp.zeros((hw, 2 * tq), BF16))
            sl = jnp.full((1, 2 * tq), slope_ref[h], F32)
            s_hi, s_lo = _bf16_split(sl)
            v_hi, v_lo = _bf16_split(-sl * ii)
            rows = jnp.concatenate([s_hi, s_lo, s_hi, s_lo, v_hi, v_lo], axis=0)
            zrow = jnp.zeros((nf, 2 * tq), F32)
            zero = jnp.zeros((hw - 2 * nf, 2 * tq), F32)
            variants = ([rows, zrow], [zrow, -rows], [zrow, zrow])
            for v, pieces in enumerate(variants):
                qtb_ref[v, h, 0:hw, :] = qh2
                qtb_ref[v, h, hw:2 * hw, :] = jnp.concatenate(pieces + [zero], axis=0).astype(BF16)

    def pair_of(step):
        c = jnp.clip(step, 0, n_pairs - 1)
        return c // nkt, c % nkt

    def key_start(qi, step):
        return _diff_key_tile(qi, step, tq, tk, nkt) * tk

    def score_head(slot, h, variant, dist):
        s = jnp.dot(k_ref[:, h * LANES:(h + 1) * LANES], qtb_ref[variant, h], preferred_element_type=F32)
        if dist is not None:
            s = s - slope_ref[h] * dist
        s_ref[slot, h] = s
        mx_ref[slot, h:h + 1, :] = jnp.max(s, axis=0, keepdims=True)

    def softmax_head(slot, h, shift):
        m_old = m_ref[h:h + 1, :]
        m_new = jnp.maximum(m_old, mx_ref[slot, h:h + 1, :] + shift)
        p_ref[h] = jnp.exp2(s_ref[slot, h] - (m_new - shift)).astype(BF16)
        m_ref[h:h + 1, :] = m_new
        return jnp.exp2(m_old - m_new)

    def value_head(h, alpha):
        pv = jnp.dot(vt_ref[h * DIFF_VROWS:(h + 1) * DIFF_VROWS, :], p_ref[h], preferred_element_type=F32)
        acc_ref[h] = alpha * acc_ref[h] + pv

    def tile_shift(step):
        qi, js = pair_of(step)
        k0 = key_start(qi, js)
        q0 = qi * tq
        sign = jnp.where(js == 0, 0.0, jnp.where(k0 < q0, 1.0, -1.0))
        return sign * (k0 - q0).astype(F32)

    def diagonal_scores(slot):
        qi, _ = pair_of(t)
        kpos = key_start(qi, 0) + lax.broadcasted_iota(I32, (tk, 2 * tq), 0)
        qpos = qi * tq + lax.broadcasted_iota(I32, (tk, 2 * tq), 1) % tq
        dist = jnp.abs(qpos - kpos).astype(F32)
        for h in range(DIFF_HEADS):
            score_head(slot, h, 2, dist)

    def finish_query_tile():
        lp = lam_ref[...]
        lam = (jnp.exp(jnp.sum(lp[0:1] * lp[1:2], axis=-1, keepdims=True))
               - jnp.exp(jnp.sum(lp[2:3] * lp[3:4], axis=-1, keepdims=True)) + lambda_init)
        outs = []
        for h in range(DIFF_HEADS):
            a = acc_ref[h]
            o = a[0:DIFF_V_DIM] / a[DIFF_V_DIM:DIFF_V_DIM + 1]
            o = o[:, 0:tq] - lam * o[:, tq:2 * tq]
            ms = jnp.mean(o * o, axis=0, keepdims=True)
            outs.append(o * lax.rsqrt(ms + NORM_EPS) * sw_ref[...] * (1.0 - lambda_init))
        o_ref[...] = jnp.transpose(jnp.concatenate(outs, axis=0)).astype(o_ref.dtype)

    _, step_in_tile = pair_of(t)
    is_first = (t < n_pairs) & (step_in_tile == 0)

    @pl.when(t == 0)
    def _():
        reset_stats()
        build_queries()
        diagonal_scores(0)

    for parity in range(2):
        @pl.when((t > 0) & (t < n_pairs) & jnp.logical_not(is_first) & (t % 2 == parity))
        def _():
            qi, js = pair_of(t)
            variant = jnp.where(key_start(qi, js) < qi * tq, 0, 1)
            shift = tile_shift(t - 1)
            alphas = []
            for h in range(DIFF_HEADS):
                alphas.append(softmax_head(1 - parity, h, slope_ref[h] * shift))
                score_head(parity, h, variant, None)
                value_head(h, alphas[h])

        if nkt % 2 == 0 and parity == 1:
            continue

        @pl.when((t > 0) & is_first & (t % 2 == parity))
        def _():
            shift = tile_shift(t - 1)
            alphas = [softmax_head(1 - parity, h, slope_ref[h] * shift) for h in range(DIFF_HEADS)]
            build_queries()
            diagonal_scores(parity)
            for h in range(DIFF_HEADS):
                value_head(h, alphas[h])
            finish_query_tile()
            reset_stats()

    @pl.when(t == n_pairs)
    def _():
        shift = tile_shift(t - 1)
        alphas = [softmax_head((n_pairs - 1) % 2, h, slope_ref[h] * shift) for h in range(DIFF_HEADS)]
        for h in range(DIFF_HEADS):
            value_head(h, alphas[h])
        finish_query_tile()


def _diff(cqt, ck, cvt, slopes, lam_params, subln_w_col, lambda_init, batch, seq, tq, tk):
    assert tk % tq == 0 and seq % tk == 0
    nq, nk = seq // tq, seq // tk
    t = batch * seq
    smem = pl.BlockSpec(memory_space=pltpu.SMEM)
    n_pairs = nq * nk

    def query_tile(p):
        return jnp.clip(p, 0, n_pairs - 1) // nk

    def key_tile(p):
        c = jnp.clip(p, 0, n_pairs - 1)
        return _diff_key_tile(c // nk, c % nk, tq, tk, nk)

    return pl.pallas_call(
        functools.partial(_diff_kernel, lambda_init=lambda_init, n_query_tiles=nq, n_key_tiles=nk),
        grid=(batch, n_pairs + 1),
        in_specs=[smem,
                  pl.BlockSpec((DIFF_QK_WIDTH, tq), lambda b, p: (0, b * nq + query_tile(p))),
                  pl.BlockSpec((tk, DIFF_HEADS * LANES), lambda b, p: (b * nk + key_tile(p), 0)),
                  pl.BlockSpec((DIFF_HEADS * DIFF_VROWS, tk), lambda b, p: (0, b * nk + key_tile(p - 1))),
                  pl.BlockSpec(lam_params.shape, lambda b, p: (0, 0)),
                  pl.BlockSpec(subln_w_col.shape, lambda b, p: (0, 0))],
        out_specs=pl.BlockSpec((tq, DIFF_V_WIDTH), lambda b, p: (b * nq + query_tile(p - 1), 0)),
        out_shape=jax.ShapeDtypeStruct((t, DIFF_V_WIDTH), BF16),
        scratch_shapes=[pltpu.VMEM((3, DIFF_HEADS, LANES, 2 * tq), BF16),
                        pltpu.VMEM((DIFF_HEADS, 2 * tq), F32),
                        pltpu.VMEM((DIFF_HEADS, DIFF_VROWS, 2 * tq), F32),
                        pltpu.VMEM((2, DIFF_HEADS, tk, 2 * tq), F32),
                        pltpu.VMEM((DIFF_HEADS, tk, 2 * tq), BF16),
                        pltpu.VMEM((2, DIFF_HEADS, 2 * tq), F32)],
        compiler_params=_cparams("parallel", "arbitrary"),
        name="diffattn",
    )(slopes, cqt, ck, cvt, lam_params, subln_w_col)


def _ssd_direction(fwd, off, xs_ref, bt_ref, cm_ref, dt_ref, dtb_ref, alog_ref, dsk_ref, y_ref, state_ref):
    q = SSD_CHUNK
    rows = slice(off, off + q)
    lane0 = 0 if fwd else SSD_HEADS
    dt_all = _softplus(dt_ref[rows, :] + dtb_ref[...])
    dta_all = dt_all * -jnp.exp(alog_ref[...])
    row = lax.broadcasted_iota(I32, (q, q), 0)
    col = lax.broadcasted_iota(I32, (q, q), 1)
    keep = (row >= col) if fwd else (row <= col)
    tri = keep.astype(BF16)
    part_hi = dta_all.astype(BF16)
    rest = dta_all - part_hi.astype(F32)
    part_mid = rest.astype(BF16)
    part_lo = (rest - part_mid.astype(F32)).astype(BF16)
    da_all = (jnp.dot(tri, part_hi, preferred_element_type=F32)
              + jnp.dot(tri, part_mid, preferred_element_type=F32)
              + jnp.dot(tri, part_lo, preferred_element_type=F32))
    da_all_t = jnp.transpose(da_all)
    dt_all_t = jnp.transpose(dt_all)
    tot_all = jnp.sum(dta_all, axis=0, keepdims=True)

    xs = xs_ref[rows, :]
    bmt = bt_ref[:, rows]
    cm = cm_ref[rows, :]
    rep = SSD_HEADS // SSD_GROUPS
    ns = SSD_STATE
    lane = lax.broadcasted_iota(I32, (q, LANES), 1)
    low = lane < SSD_HEAD_DIM
    low_n = lax.broadcasted_iota(I32, (ns, LANES), 1) < SSD_HEAD_DIM
    cm_g = [jnp.where((lane // ns) == g, cm, 0.0) for g in range(SSD_GROUPS)]
    bmt_b = bmt.astype(BF16)
    g_mats = [jnp.dot(cm_g[g].astype(BF16), bmt_b, preferred_element_type=F32)
              for g in range(SSD_GROUPS)]
    zeros_n = jnp.zeros((ns, LANES), BF16)
    ys = []
    for pair in range(SSD_HEADS // 2):
        x_pair = xs[:, pair * LANES:(pair + 1) * LANES]
        x_pair_b = x_pair.astype(BF16)
        st = state_ref[pair]
        st_b = st.astype(BF16)
        y_heads, s_heads, keep_heads = [], [], []
        for h in (2 * pair, 2 * pair + 1):
            g = h // rep
            ln = lane0 + h
            dac = da_all[:, ln:ln + 1]
            dar = da_all_t[ln:ln + 1, :]
            dtr = dt_all_t[ln:ln + 1, :]
            tot = tot_all[:, ln:ln + 1]
            dac_b = jnp.broadcast_to(dac, (q, q))
            decay = jnp.exp(jnp.where(keep, dac_b - dar, NEG))
            y = jnp.dot((g_mats[g] * decay * dtr).astype(BF16), x_pair_b, preferred_element_type=F32)
            st_ext = jnp.concatenate([st_b, zeros_n] if g == 0 else [zeros_n, st_b], axis=0)
            c_in = (cm_g[g] * jnp.exp(dac_b)).astype(BF16)
            y_heads.append(y + jnp.dot(c_in, st_ext, preferred_element_type=F32))
            to_end = jnp.exp(tot - dar) * dtr
            b_out = (bmt[g * ns:(g + 1) * ns, :] * to_end).astype(BF16)
            s_heads.append(jnp.dot(b_out, x_pair_b, preferred_element_type=F32))
            keep_heads.append(jnp.exp(tot))
        state_keep = jnp.where(low_n, jnp.broadcast_to(keep_heads[0], (ns, LANES)),
                               jnp.broadcast_to(keep_heads[1], (ns, LANES)))
        state_ref[pair] = st * state_keep + jnp.where(low_n, s_heads[0], s_heads[1])
        y = jnp.where(low, y_heads[0], y_heads[1])
        if fwd:
            y = y + dsk_ref[:, pair * LANES:(pair + 1) * LANES] * x_pair
        ys.append(y)
    y_ref[rows, :] = jnp.concatenate(ys, axis=-1)


def _ssd_kernel(xsf_ref, btf_ref, cmf_ref, dtf_ref, xsb_ref, btb_ref, cmb_ref, dtb_in_ref,
                dtbias_ref, alog_ref, dsk_ref, yf_ref, yb_ref, state_ref):
    @pl.when(pl.program_id(1) == 0)
    def _():
        state_ref[...] = jnp.zeros(state_ref.shape, F32)

    n = SSD_CHUNKS_PER_STEP
    for u in range(n):
        _ssd_direction(True, u * SSD_CHUNK, xsf_ref, btf_ref, cmf_ref, dtf_ref, dtbias_ref, alog_ref, dsk_ref,
                       yf_ref, state_ref.at[0])
        _ssd_direction(False, (n - 1 - u) * SSD_CHUNK, xsb_ref, btb_ref, cmb_ref, dtb_in_ref, dtbias_ref,
                       alog_ref, dsk_ref, yb_ref, state_ref.at[1])


def _ssd(xs, bt, cm, dt_raw, dt_bias, a_log, d_skip, batch, seq):
    q = SSD_CHUNK * SSD_CHUNKS_PER_STEP
    nc = seq // q
    t = batch * seq
    full = lambda b, c: (0, 0)
    fw = lambda b, c: b * nc + c
    bw = lambda b, c: b * nc + nc - 1 - c

    def specs(idx):
        return [pl.BlockSpec((q, SSD_D_INNER), lambda b, c: (idx(b, c), 0)),
                pl.BlockSpec((SSD_BC_DIM, q), lambda b, c: (0, idx(b, c))),
                pl.BlockSpec((q, SSD_BC_DIM), lambda b, c: (idx(b, c), 0)),
                pl.BlockSpec((q, LANES), lambda b, c: (idx(b, c), 0))]

    return pl.pallas_call(
        _ssd_kernel,
        grid=(batch, nc),
        in_specs=specs(fw) + specs(bw) + [pl.BlockSpec(dt_bias.shape, full), pl.BlockSpec(a_log.shape, full),
                                          pl.BlockSpec(d_skip.shape, full)],
        out_specs=[pl.BlockSpec((q, SSD_D_INNER), lambda b, c: (fw(b, c), 0)),
                   pl.BlockSpec((q, SSD_D_INNER), lambda b, c: (bw(b, c), 0))],
        out_shape=[jax.ShapeDtypeStruct((t, SSD_D_INNER), F32), jax.ShapeDtypeStruct((t, SSD_D_INNER), F32)],
        scratch_shapes=[pltpu.VMEM((2, SSD_HEADS // 2, SSD_STATE, 2 * SSD_HEAD_DIM), F32)],
        compiler_params=_cparams("parallel", "arbitrary"),
        name="ssd",
    )(xs, bt, cm, dt_raw, xs, bt, cm, dt_raw, dt_bias, a_log, d_skip)


def _outproj_kernel(x_ref, ya_ref, yf_ref, yb_ref, z_ref, snw_ref, yc_ref, wo_ref, fnw_ref, wrt_ref, br_ref,
                    xn_ref, ri_ref, rf_ref, cnt_ref, tri_ref, carry_ref):
    step = pl.program_id(0)
    tm = x_ref.shape[0]

    @pl.when(step == 0)
    def _():
        carry_ref[...] = jnp.zeros(carry_ref.shape, F32)
        r = lax.broadcasted_iota(I32, (tm, tm), 0)
        cc = lax.broadcasted_iota(I32, (tm, tm), 1)
        tri_ref[...] = (r <= cc).astype(BF16)

    y = (yf_ref[...] + yb_ref[...]) * _silu(z_ref[...])
    yb = _rms(y, snw_ref[...]).astype(BF16)
    acc = jnp.dot(ya_ref[...], wo_ref[0:A_Q_DIM, :], preferred_element_type=F32)
    acc = acc + jnp.dot(yb, wo_ref[A_Q_DIM:A_Q_DIM + SSD_D_INNER, :], preferred_element_type=F32)
    acc = acc + jnp.dot(yc_ref[...], wo_ref[A_Q_DIM + SSD_D_INNER:, :], preferred_element_type=F32)
    xn = x_ref[...] + acc
    xn_ref[...] = xn

    h = _rms(xn, fnw_ref[...])
    h_hi = h.astype(BF16)
    h_lo = (h - h_hi.astype(F32)).astype(BF16)
    nt = (((1,), (1,)), ((), ()))
    logits = (lax.dot_general(wrt_ref[0], h_hi, nt, preferred_element_type=F32)
              + lax.dot_general(wrt_ref[0], h_lo, nt, preferred_element_type=F32)
              + lax.dot_general(wrt_ref[1], h_hi, nt, preferred_element_type=F32)) + br_ref[...]
    ne, epg, ng = N_EXPERTS, EXPERTS_PER_GROUP, N_EXPERT_GROUPS
    gl = logits[ne:ne + ng, :]
    gmax = jnp.max(gl, axis=0, keepdims=True)
    g_sel = jnp.full((1, tm), float(ng - 1), F32)
    for g in range(ng - 2, -1, -1):
        g_sel = jnp.where(gl[g:g + 1, :] == gmax, float(g), g_sel)
    g_gate = 1.0 / jnp.sum(jnp.exp(gl - gmax), axis=0, keepdims=True)
    e_in = logits[0:epg, :]
    for g in range(1, ng):
        e_in = jnp.where(g_sel == float(g), logits[g * epg:(g + 1) * epg, :], e_in)
    sub = lax.broadcasted_iota(I32, (epg, tm), 0).astype(F32)
    m1 = jnp.max(e_in, axis=0, keepdims=True)
    i1 = jnp.min(jnp.where(e_in == m1, sub, float(epg)), axis=0, keepdims=True)
    rest = jnp.where(sub == i1, NEG, e_in)
    m2 = jnp.max(rest, axis=0, keepdims=True)
    i2 = jnp.min(jnp.where(rest == m2, sub, float(epg)), axis=0, keepdims=True)
    r = jnp.exp(m2 - m1)
    c1 = g_gate / (1.0 + r)
    c2 = g_gate * r / (1.0 + r)
    e1 = (g_sel * epg + i1).astype(I32)
    e2 = (g_sel * epg + i2).astype(I32)

    erow = lax.broadcasted_iota(I32, (ne, tm), 0)
    hit1 = erow == e1
    hit2 = erow == e2
    oh = jnp.where(hit1 | hit2, 1.0, 0.0)
    incl = jnp.dot(oh.astype(BF16), tri_ref[...], preferred_element_type=F32)
    before = incl - oh + carry_ref[:, 0:1]
    rank1 = jnp.sum(jnp.where(hit1, before, 0.0), axis=0, keepdims=True)
    rank2 = jnp.sum(jnp.where(hit2, before, 0.0), axis=0, keepdims=True)
    carry_ref[...] = carry_ref[...] + jnp.sum(oh, axis=1, keepdims=True)
    cnt_ref[...] = carry_ref[...]
    zi = jnp.zeros((1, tm), I32)
    ri_ref[...] = jnp.concatenate([e1, e2, rank1.astype(I32), rank2.astype(I32), zi, zi, zi, zi], axis=0)
    zf = jnp.zeros((1, tm), F32)
    rf_ref[...] = jnp.concatenate([c1, c2, zf, zf, zf, zf, zf, zf], axis=0)


def _outproj(x2, ya, yf, yb, z, ssd_norm_w, yc, w_out, ffn_norm_w, wrt, br, tm):
    t, d = x2.shape
    row = lambda i: (i, 0)
    full = lambda i: (0, 0)
    return pl.pallas_call(
        _outproj_kernel,
        grid=(t // tm,),
        in_specs=[pl.BlockSpec((tm, d), row),
                  pl.BlockSpec((tm, A_Q_DIM), row),
                  pl.BlockSpec((tm, SSD_D_INNER), row),
                  pl.BlockSpec((tm, SSD_D_INNER), row),
                  pl.BlockSpec((tm, SSD_D_INNER), row),
                  pl.BlockSpec(ssd_norm_w.shape, full),
                  pl.BlockSpec((tm, DIFF_V_WIDTH), row),
                  pl.BlockSpec(w_out.shape, full),
                  pl.BlockSpec(ffn_norm_w.shape, full),
                  pl.BlockSpec(wrt.shape, lambda i: (0, 0, 0)),
                  pl.BlockSpec(br.shape, full)],
        out_specs=[pl.BlockSpec((tm, d), row),
                   pl.BlockSpec((SUBLANES, tm), lambda i: (0, i)),
                   pl.BlockSpec((SUBLANES, tm), lambda i: (0, i)),
                   pl.BlockSpec((N_EXPERTS, LANES), full)],
        out_shape=[jax.ShapeDtypeStruct((t, d), F32),
                   jax.ShapeDtypeStruct((SUBLANES, t), I32),
                   jax.ShapeDtypeStruct((SUBLANES, t), F32),
                   jax.ShapeDtypeStruct((N_EXPERTS, LANES), F32)],
        scratch_shapes=[pltpu.VMEM((tm, tm), BF16), pltpu.VMEM((N_EXPERTS, LANES), F32)],
        compiler_params=_cparams("arbitrary"),
        name="outproj_router",
    )(x2, ya, yf, yb, z, ssd_norm_w, yc, w_out, ffn_norm_w, wrt, br)


_PAD_PIECES = tuple(1 << b for b in reversed(range(MOE_ROW_TILE.bit_length() - 1)))


def _dispatch_kernel(slot1_ref, slot2_ref, pstart_ref, plen_ref, nused_ref, x_ref, xs_hbm, zero_ref, xbuf, sem):
    i = pl.program_id(0)
    tm = x_ref.shape[0]

    @pl.when(i == 0)
    def _():
        zero_ref[...] = jnp.zeros(zero_ref.shape, F32)

        def pieces(e, wait):
            n = plen_ref[e]
            first = pstart_ref[e]
            off = first + n
            for b in _PAD_PIECES:
                off = off - (n & b)
                dst = pl.ds(pl.multiple_of(off, b), b) if b >= SUBLANES else None
                if dst is not None:
                    @pl.when((n & b) != 0)
                    def _():
                        cp = pltpu.make_async_copy(zero_ref.at[pl.ds(0, b)], xs_hbm.at[dst], sem.at[2])
                        cp.wait() if wait else cp.start()

            for u in range(SUBLANES - 1):
                @pl.when(u < (n & (SUBLANES - 1)))
                def _():
                    cp = pltpu.make_async_copy(zero_ref.at[pl.ds(0, 1)], xs_hbm.at[pl.ds(first + u, 1)],
                                               sem.at[2])
                    cp.wait() if wait else cp.start()

        def tail(tile, wait):
            big = _PAD_PIECES[0]
            for part in range(MOE_ROW_TILE // big):
                dst = xs_hbm.at[pl.ds(pl.multiple_of(tile * MOE_ROW_TILE + part * big, big), big)]
                cp = pltpu.make_async_copy(zero_ref, dst, sem.at[2])
                cp.wait() if wait else cp.start()

        def loop(fn, lo, hi, wait):
            def body(k, carry):
                fn(k, wait)
                return carry

            lax.fori_loop(lo, hi, body, 0)

        n_tiles = xs_hbm.shape[0] // MOE_ROW_TILE
        for wait in (False, True):
            loop(pieces, 0, N_EXPERTS, wait)
            loop(tail, nused_ref[0], n_tiles, wait)

    base = i * tm
    last = pl.num_programs(0) - 1

    def drain(slot):
        for _ in range(2):
            pltpu.make_async_copy(xbuf.at[slot], xs_hbm.at[pl.ds(0, tm)], sem.at[slot]).wait()

    for parity in range(2):
        @pl.when(i % 2 == parity)
        def _():
            buf = xbuf.at[parity]
            buf[...] = x_ref[...]
            for r in range(tm):
                src = buf.at[pl.ds(r, 1)]
                pltpu.make_async_copy(src, xs_hbm.at[pl.ds(slot1_ref[base + r], 1)],
                                      sem.at[parity]).start(priority=0)
                pltpu.make_async_copy(src, xs_hbm.at[pl.ds(slot2_ref[base + r], 1)],
                                      sem.at[parity]).start(priority=1)

            @pl.when(i > 0)
            def _():
                drain(1 - parity)

            @pl.when(i == last)
            def _():
                drain(parity)


def _dispatch(xn, slot1, slot2, pad_start, pad_len, n_used, n_rows, tm):
    t, d = xn.shape
    grid_spec = pltpu.PrefetchScalarGridSpec(
        num_scalar_prefetch=5,
        grid=(t // tm,),
        in_specs=[pl.BlockSpec((tm, d), lambda i, s1, s2, ps, pn, nu: (i, 0))],
        out_specs=pl.BlockSpec(memory_space=pl.ANY),
        scratch_shapes=[pltpu.VMEM((_PAD_PIECES[0], d), F32), pltpu.VMEM((2, tm, d), F32),
                        pltpu.SemaphoreType.DMA((3,))],
    )
    return pl.pallas_call(
        _dispatch_kernel,
        grid_spec=grid_spec,
        out_shape=jax.ShapeDtypeStruct((n_rows, d), F32),
        compiler_params=_cparams("arbitrary"),
        name="moe_dispatch",
    )(slot1, slot2, pad_start, pad_len, n_used, xn)


def _moe_kernel(texp_ref, nused_ref, first_ref, wslot_ref, next_ref, x_ref, fnw_ref, wg_hbm, wu_hbm, wd_hbm,
                y_ref, wg_buf, wu_buf, wd_buf, sem, *, layer):
    i = pl.program_id(0)

    def weight_copies(expert, slot):
        return [pltpu.make_async_copy(hbm.at[layer, expert], buf.at[slot], sem.at[slot])
                for hbm, buf in ((wg_hbm, wg_buf), (wu_hbm, wu_buf), (wd_hbm, wd_buf))]

    @pl.when(i < nused_ref[0])
    def _():
        slot = wslot_ref[i]

        @pl.when(i == 0)
        def _():
            for cp in weight_copies(texp_ref[0], 0):
                cp.start()

        @pl.when(first_ref[i] == 1)
        def _():
            for cp in weight_copies(texp_ref[i], slot):
                cp.wait()

            @pl.when(next_ref[i] >= 0)
            def _():
                for cp in weight_copies(next_ref[i], 1 - slot):
                    cp.start()

        h = _rms(x_ref[...], fnw_ref[...]).astype(BF16)
        hg = jnp.dot(h, wg_buf[slot].astype(BF16), preferred_element_type=F32)
        hu = jnp.dot(h, wu_buf[slot].astype(BF16), preferred_element_type=F32)
        act = (_silu(hg) * hu).astype(BF16)
        y_ref[...] = jnp.dot(act, wd_buf[slot].astype(BF16), preferred_element_type=F32)

    @pl.when(i >= nused_ref[0])
    def _():
        y_ref[...] = jnp.zeros(y_ref.shape, F32)


def _moe(xs, ffn_norm_w, w_gate, w_up, w_down, layer, tile_expert, n_used):
    n_rows, d = xs.shape
    f = w_gate.shape[-1]
    tr = MOE_ROW_TILE

    n_tiles = n_rows // tr

    idx = jnp.arange(n_tiles, dtype=I32)
    prev_expert = jnp.concatenate([jnp.full((1,), -1, I32), tile_expert[:-1]])
    first = ((idx < n_used[0]) & (tile_expert != prev_expert)).astype(I32)
    wslot = ((jnp.cumsum(first) - 1) % 2).astype(I32)
    first_pos = jnp.where(first == 1, idx, n_tiles)
    next_first = jnp.concatenate([lax.cummin(first_pos, reverse=True)[1:], jnp.full((1,), n_tiles, I32)])
    next_expert = jnp.where(next_first < n_tiles, tile_expert[jnp.minimum(next_first, n_tiles - 1)], -1).astype(I32)

    def used(i, nu):
        return jnp.maximum(jnp.minimum(i, nu[0] - 1), 0)

    anyspace = pl.BlockSpec(memory_space=pl.ANY)
    grid_spec = pltpu.PrefetchScalarGridSpec(
        num_scalar_prefetch=5,
        grid=(n_tiles,),
        in_specs=[pl.BlockSpec((tr, d), lambda i, te, nu, fi, ws, nx: (used(i, nu), 0)),
                  pl.BlockSpec(ffn_norm_w.shape, lambda i, te, nu, fi, ws, nx: (0, 0)),
                  anyspace, anyspace, anyspace],
        out_specs=pl.BlockSpec((tr, d), lambda i, te, nu, fi, ws, nx: (i, 0)),
        scratch_shapes=[pltpu.VMEM((2, d, f), F32), pltpu.VMEM((2, d, f), F32), pltpu.VMEM((2, f, d), F32),
                        pltpu.SemaphoreType.DMA((2,))],
    )
    return pl.pallas_call(
        functools.partial(_moe_kernel, layer=layer),
        grid_spec=grid_spec,
        out_shape=jax.ShapeDtypeStruct((n_rows, d), F32),
        compiler_params=_cparams("arbitrary"),
        name="moe_experts",
    )(tile_expert, n_used, first, wslot, next_expert, xs, ffn_norm_w, w_gate, w_up, w_down)


def _combine_kernel(slot1_ref, slot2_ref, x_ref, cw_ref, nw_ref, y_hbm, o_ref, ybuf, sem, *, final_norm):
    i = pl.program_id(0)
    n = pl.num_programs(0)
    tm = x_ref.shape[0]

    def start_gather(tile, slot):
        base = tile * tm
        for r in range(tm):
            pltpu.make_async_copy(y_hbm.at[pl.ds(slot1_ref[base + r], 1)], ybuf.at[slot, 0, pl.ds(r, 1)],
                                  sem.at[slot]).start(priority=0)
            pltpu.make_async_copy(y_hbm.at[pl.ds(slot2_ref[base + r], 1)], ybuf.at[slot, 1, pl.ds(r, 1)],
                                  sem.at[slot]).start(priority=1)

    def compute(slot):
        for k in range(2):
            pltpu.make_async_copy(y_hbm.at[pl.ds(0, tm)], ybuf.at[slot, k], sem.at[slot]).wait()
        cw = cw_ref[...]
        out = x_ref[...] + cw[:, 0:1] * ybuf[slot, 0] + cw[:, 1:2] * ybuf[slot, 1]
        if final_norm:
            out = _rms(out, nw_ref[...])
        o_ref[...] = out

    @pl.when(i == 0)
    def _():
        start_gather(0, 0)

    for parity in range(2):
        @pl.when(i % 2 == parity)
        def _():
            @pl.when(i + 1 < n)
            def _():
                start_gather(i + 1, 1 - parity)

            compute(parity)


def _combine(xn, cw, norm_w, y_sorted, slot1, slot2, tm, final_norm):
    t, d = xn.shape
    grid_spec = pltpu.PrefetchScalarGridSpec(
        num_scalar_prefetch=2,
        grid=(t // tm,),
        in_specs=[pl.BlockSpec((tm, d), lambda i, s1, s2: (i, 0)),
                  pl.BlockSpec((tm, cw.shape[1]), lambda i, s1, s2: (i, 0)),
                  pl.BlockSpec(norm_w.shape, lambda i, s1, s2: (0, 0)),
                  pl.BlockSpec(memory_space=pl.ANY)],
        out_specs=pl.BlockSpec((tm, d), lambda i, s1, s2: (i, 0)),
        scratch_shapes=[pltpu.VMEM((2, 2, tm, d), F32), pltpu.SemaphoreType.DMA((2,))],
    )
    return pl.pallas_call(
        functools.partial(_combine_kernel, final_norm=final_norm),
        grid_spec=grid_spec,
        out_shape=jax.ShapeDtypeStruct((t, d), F32),
        compiler_params=_cparams("arbitrary"),
        name="moe_combine",
    )(slot1, slot2, xn, cw, norm_w, y_sorted)


def _pad_lanes(v):
    v = v.reshape(1, -1).astype(F32)
    return jnp.pad(v, ((0, 0), (0, LANES - v.shape[1])))


def kernel(x, attn_norm_w, w_in, swa_sink, ssd_conv_w, ssd_conv_b, ssd_dt_bias, ssd_a_log, ssd_d, ssd_norm_w,
           diff_lambda, diff_subln_w, w_out, ffn_norm_w, w_router_group, b_router_group, w_router_expert,
           b_router_expert, w_gate, w_up, w_down, final_norm_w):
    return _forward(x, attn_norm_w, w_in, swa_sink, ssd_conv_w, ssd_conv_b, ssd_dt_bias, ssd_a_log, ssd_d,
                    ssd_norm_w, diff_lambda, diff_subln_w, w_out, ffn_norm_w, w_router_group, b_router_group,
                    w_router_expert, b_router_expert, w_gate, w_up, w_down, final_norm_w)


def _forward(x, attn_norm_w, w_in, swa_sink, ssd_conv_w, ssd_conv_b, ssd_dt_bias, ssd_a_log, ssd_d, ssd_norm_w,
             diff_lambda, diff_subln_w, w_out, ffn_norm_w, w_router_group, b_router_group, w_router_expert,
             b_router_expert, w_gate, w_up, w_down, final_norm_w, tm=512, tq=512, tk=1024, tmc=256):
    batch, seq, d = x.shape
    depth = w_in.shape[0]
    t = batch * seq
    tr = MOE_ROW_TILE
    n_tiles = (2 * t) // tr + N_EXPERTS
    slopes = jnp.exp2(-8.0 * jnp.arange(1, N_ALIBI_HEADS + 1, dtype=F32) / N_ALIBI_HEADS)
    swa_slopes, diff_slopes = slopes[:SWA_HEADS], slopes[SWA_HEADS:]

    sizes = [A_Q_DIM, A_KV_DIM, A_KV_DIM, SSD_D_INNER, SSD_CONV_DIM, SSD_DT_DIM, DIFF_QK_WIDTH, DIFF_QK_WIDTH,
             DIFF_V_WIDTH]
    offs = [0]
    for s in sizes:
        offs.append(offs[-1] + s)
    o_aq, o_ak, o_av, o_z, o_xbc, o_dt, o_cq, o_ck, o_cv, o_end = offs

    x2 = x.reshape(t, d)
    for l in range(depth):
        w = w_in[l]
        hw = 2 * DIFF_QK_DIM
        w_ck = jnp.pad(w[:, o_ck:o_cv].reshape(d, DIFF_HEADS, hw), ((0, 0), (0, 0), (0, LANES - hw)))
        w_main = jnp.concatenate(
            [w[:, o_ak:o_av], w[:, o_z:o_dt], w_ck.reshape(d, DIFF_HEADS * LANES), w[:, o_dt:o_cq],
             jnp.zeros((d, LANES - SSD_DT_DIM), w.dtype)], axis=1).astype(BF16)
        w_t = jnp.concatenate([w[:, o_aq:o_ak], w[:, o_av:o_z], w[:, o_cq:o_ck], w[:, o_cv:o_end]],
                              axis=1).T.astype(BF16)
        ak, z, xs, cm, ck, dt_raw, aqt, avt3, cqt, cvt, bt = _inproj(
            x2, attn_norm_w[l].reshape(1, d), w_main, w_t, ssd_conv_w[l].astype(F32),
            ssd_conv_b[l].reshape(1, -1).astype(F32), tm, seq, tk)

        ya = _swa(aqt, ak, avt3, swa_sink[l].astype(F32) * LOG2E, swa_slopes * LOG2E, batch, seq)
        lambda_init = 0.8 - 0.6 * math.exp(-0.3 * l)
        yc = _diff(cqt, ck, cvt, diff_slopes * LOG2E, diff_lambda[l].astype(F32),
                   diff_subln_w[l].reshape(DIFF_V_DIM, 1).astype(F32), lambda_init, batch, seq, tq, tk)
        yf, yb = _ssd(xs, bt, cm, dt_raw, _pad_lanes(ssd_dt_bias[l]), _pad_lanes(ssd_a_log[l]),
                      jnp.repeat(ssd_d[l].astype(F32), SSD_HEAD_DIM).reshape(1, SSD_D_INNER), batch, seq)

        wr32 = jnp.concatenate([w_router_expert[l], w_router_group[l],
                                jnp.zeros((d, SUBLANES - N_EXPERT_GROUPS), F32)], axis=1).T.astype(F32)
        wr_hi = wr32.astype(BF16)
        wrt = jnp.stack([wr_hi, (wr32 - wr_hi.astype(F32)).astype(BF16)])
        br = jnp.concatenate([b_router_expert[l], b_router_group[l],
                              jnp.zeros((SUBLANES - N_EXPERT_GROUPS,), F32)]).reshape(-1, 1).astype(F32)
        xn, ri, rf, cnt = _outproj(x2, ya, yf, yb, z, ssd_norm_w[l].reshape(1, -1), yc, w_out[l].astype(BF16),
                                   ffn_norm_w[l].reshape(1, d), wrt, br, tm)

        counts = cnt[:, 0].astype(I32)
        padded = ((counts + tr - 1) // tr) * tr
        ends = jnp.cumsum(padded)
        starts = ends - padded
        experts = jnp.arange(N_EXPERTS, dtype=I32)[:, None]

        def slot_of(e, rank):
            return jnp.sum(jnp.where(e[None, :] == experts, starts[:, None], 0), axis=0) + rank

        slot1 = slot_of(ri[0], ri[2])
        slot2 = slot_of(ri[1], ri[3])
        tile_start = jnp.arange(n_tiles, dtype=I32) * tr
        tile_expert = jnp.minimum(jnp.sum(ends[None, :] <= tile_start[:, None], axis=1), N_EXPERTS - 1).astype(I32)
        n_used = (ends[-1] // tr).astype(I32).reshape(1)

        xs_sorted = _dispatch(xn, slot1, slot2, starts + counts, padded - counts, n_used, n_tiles * tr, tmc)
        y_sorted = _moe(xs_sorted, ffn_norm_w[l].reshape(1, d), w_gate, w_up, w_down, l, tile_expert, n_used)
        last = l == depth - 1
        x2 = _combine(xn, rf.T, final_norm_w.reshape(1, d), y_sorted, slot1, slot2, tmc, last)
    return x2.reshape(batch, seq, d)
```

```python
import functools
import math

import jax
import jax.numpy as jnp
from jax import lax
from jax.experimental import pallas as pl
from jax.experimental.pallas import tpu as pltpu

F32 = jnp.float32
BF16 = jnp.bfloat16
I32 = jnp.int32

HEAD_DIM = 64
SWA_HEADS = 6
SWA_KV_HEADS = 2
SWA_WINDOW = 128
SSD_HEADS = 6
SSD_HEAD_DIM = 64
SSD_GROUPS = 2
SSD_STATE = 64
SSD_CONV = 5
DIFF_HEADS = 4
DIFF_QK_DIM = 32
DIFF_V_DIM = 64
N_EXPERT_GROUPS = 4
EXPERTS_PER_GROUP = 8
N_EXPERTS = N_EXPERT_GROUPS * EXPERTS_PER_GROUP
NORM_EPS = 1e-6

A_Q_DIM = SWA_HEADS * HEAD_DIM
A_KV_DIM = SWA_KV_HEADS * HEAD_DIM
SSD_D_INNER = SSD_HEADS * SSD_HEAD_DIM
SSD_BC_DIM = SSD_GROUPS * SSD_STATE
SSD_CONV_DIM = SSD_D_INNER + 2 * SSD_BC_DIM
SSD_DT_DIM = 2 * SSD_HEADS
DIFF_QK_WIDTH = DIFF_HEADS * 2 * DIFF_QK_DIM
DIFF_V_WIDTH = DIFF_HEADS * DIFF_V_DIM
N_ALIBI_HEADS = SWA_HEADS + DIFF_HEADS

LANES = 128
SUBLANES = 8
VMEM_LIMIT = 56 * 1024 * 1024
NEG = -1e30
LOG2E = math.log2(math.e)

SSD_CHUNK = 128
SSD_CHUNKS_PER_STEP = 4
SWA_BLOCKS_PER_STEP = 8
MOE_ROW_TILE = 256
DIFF_PAIR = 2 * 2 * DIFF_QK_DIM
DIFF_VROWS = 80
DIFF_NFEAT = 6


def _cparams(*sem):
    return pltpu.CompilerParams(dimension_semantics=sem, vmem_limit_bytes=VMEM_LIMIT)


def _rms(x, w):
    return x * lax.rsqrt(jnp.mean(x * x, axis=-1, keepdims=True) + NORM_EPS) * w


def _silu(x):
    return x / (1.0 + jnp.exp(-x))


def _softplus(x):
    return jnp.maximum(x, 0.0) + jnp.log(1.0 + jnp.exp(-jnp.abs(x)))


def _bf16_split(x):
    hi = x.astype(BF16).astype(F32)
    lo = (x - hi).astype(BF16).astype(F32)
    return hi, lo


_C_AK = 0
_C_Z = _C_AK + A_KV_DIM
_C_XBC = _C_Z + SSD_D_INNER
_C_CK = _C_XBC + SSD_CONV_DIM
_C_DT = _C_CK + DIFF_HEADS * LANES
_C_END = _C_DT + LANES
_R_AQ = 0
_R_AV = _R_AQ + A_Q_DIM
_R_CQ = _R_AV + A_KV_DIM
_R_CV = _R_CQ + DIFF_QK_WIDTH
_R_END = _R_CV + DIFF_V_WIDTH


def _inproj_kernel(x_ref, xp_ref, xn_ref, nw_ref, w_ref, wt_ref, cw_ref, cb_ref,
                   ak_ref, z_ref, xs_ref, cm_ref, ck_ref, dt_ref, aqt_ref, avt_ref, cqt_ref, cvt_ref, bt_ref,
                   *, tiles_per_seq, diff_key_tile):
    i = pl.program_id(0)
    tm = x_ref.shape[0]
    nw = nw_ref[...]
    h = _rms(x_ref[...], nw).astype(BF16)

    def seg(lo, hi):
        return jnp.dot(h, w_ref[:, lo:hi], preferred_element_type=F32)

    ak_ref[...] = seg(_C_AK, _C_Z).astype(BF16)
    z_ref[...] = seg(_C_Z, _C_XBC)
    pos = (i * tm + lax.broadcasted_iota(I32, (tm, _C_DT - _C_CK), 0)) % diff_key_tile
    ck_ref[...] = (seg(_C_CK, _C_DT) + _diff_key_features(pos)).astype(BF16)
    dt_ref[...] = seg(_C_DT, _C_END)

    w_xbc = w_ref[:, _C_XBC:_C_CK]
    first = i % tiles_per_seq == 0
    last = i % tiles_per_seq == tiles_per_seq - 1
    prev = jnp.dot(_rms(xp_ref[...], nw).astype(BF16), w_xbc, preferred_element_type=F32)
    nxt = jnp.dot(_rms(xn_ref[...], nw).astype(BF16), w_xbc, preferred_element_type=F32)
    prev = jnp.where(first, 0.0, prev)
    nxt = jnp.where(last, 0.0, nxt)
    ext = jnp.concatenate([prev, seg(_C_XBC, _C_CK), nxt], axis=0)
    half = SSD_CONV // 2
    conv = cb_ref[...]
    for k in range(SSD_CONV):
        off = SUBLANES - half + k
        conv = conv + cw_ref[k:k + 1, :] * ext[off:off + tm, :]
    u = _silu(conv)
    xs_ref[...] = u[:, :SSD_D_INNER]
    bt_ref[...] = jnp.transpose(u[:, SSD_D_INNER:SSD_D_INNER + SSD_BC_DIM])
    cm_ref[...] = u[:, SSD_D_INNER + SSD_BC_DIM:]

    tr = lax.dot_general(wt_ref[...], h, (((1,), (1,)), ((), ())), preferred_element_type=F32)
    aqt_ref[...] = (tr[_R_AQ:_R_AV] * (HEAD_DIM ** -0.5 * LOG2E)).astype(BF16)
    avt = tr[_R_AV:_R_CQ].astype(BF16)
    for c in range(tm // LANES):
        avt_ref[c] = avt[:, c * LANES:(c + 1) * LANES]
    cqt_ref[...] = (tr[_R_CQ:_R_CV] * (DIFF_QK_DIM ** -0.5 * LOG2E)).astype(BF16)
    pad = DIFF_VROWS - DIFF_V_DIM
    ones_row = (lax.broadcasted_iota(I32, (pad, tm), 0) == 0).astype(BF16)
    for hh in range(DIFF_HEADS):
        cvt_ref[hh * DIFF_VROWS:hh * DIFF_VROWS + DIFF_V_DIM, :] = (
            tr[_R_CV + hh * DIFF_V_DIM:_R_CV + (hh + 1) * DIFF_V_DIM].astype(BF16))
        cvt_ref[hh * DIFF_VROWS + DIFF_V_DIM:(hh + 1) * DIFF_VROWS, :] = ones_row


def _inproj(x2, norm_w, w_main, w_t, conv_w, conv_b, tm, seq, diff_key_tile):
    t, d = x2.shape
    hb = tm // SUBLANES
    n_hblk = t // SUBLANES
    row = lambda i: (i, 0)
    col = lambda i: (0, i)
    full = lambda i: (0, 0)
    row_outs = [(A_KV_DIM, BF16), (SSD_D_INNER, F32), (SSD_D_INNER, F32), (SSD_BC_DIM, F32),
                (DIFF_HEADS * LANES, BF16), (LANES, F32)]
    out_shape = [jax.ShapeDtypeStruct((t, w), dt) for w, dt in row_outs]
    out_specs = [pl.BlockSpec((tm, w), row) for w, _ in row_outs]
    out_shape += [jax.ShapeDtypeStruct((A_Q_DIM, t), BF16),
                  jax.ShapeDtypeStruct((t // LANES, A_KV_DIM, LANES), BF16),
                  jax.ShapeDtypeStruct((DIFF_QK_WIDTH, t), BF16),
                  jax.ShapeDtypeStruct((DIFF_HEADS * DIFF_VROWS, t), BF16),
                  jax.ShapeDtypeStruct((SSD_BC_DIM, t), F32)]
    out_specs += [pl.BlockSpec((A_Q_DIM, tm), col),
                  pl.BlockSpec((tm // LANES, A_KV_DIM, LANES), lambda i: (i, 0, 0)),
                  pl.BlockSpec((DIFF_QK_WIDTH, tm), col),
                  pl.BlockSpec((DIFF_HEADS * DIFF_VROWS, tm), col),
                  pl.BlockSpec((SSD_BC_DIM, tm), col)]
    return pl.pallas_call(
        functools.partial(_inproj_kernel, tiles_per_seq=seq // tm, diff_key_tile=diff_key_tile),
        grid=(t // tm,),
        in_specs=[pl.BlockSpec((tm, d), row),
                  pl.BlockSpec((SUBLANES, d), lambda i: (jnp.maximum(i * hb - 1, 0), 0)),
                  pl.BlockSpec((SUBLANES, d), lambda i: (jnp.minimum((i + 1) * hb, n_hblk - 1), 0)),
                  pl.BlockSpec((1, d), full),
                  pl.BlockSpec(w_main.shape, full), pl.BlockSpec(w_t.shape, full),
                  pl.BlockSpec(conv_w.shape, full), pl.BlockSpec(conv_b.shape, full)],
        out_specs=out_specs,
        out_shape=out_shape,
        compiler_params=_cparams("parallel"),
        name="inproj",
    )(x2, x2, x2, norm_w, w_main, w_t, conv_w, conv_b)


def _swa_kernel(sink_ref, slope_ref, qt_ref, k_ref, vt_ref, o_ref):
    step = pl.program_id(1)
    s_len = k_ref.shape[0]
    blk = SWA_WINDOW
    band = 3 * blk
    nb = s_len // blk
    rep = SWA_HEADS // SWA_KV_HEADS
    hd = HEAD_DIM
    blocks_per_step = qt_ref.shape[1] // blk
    for u in range(blocks_per_step):
        n = step * blocks_per_step + u
        start_blk = jnp.clip(n - 1, 0, nb - 3)
        start = pl.multiple_of(start_blk * blk, blk)
        kb = k_ref[pl.ds(start, band), :]
        v3 = vt_ref[pl.ds(start_blk, 3)]
        vtb = jnp.concatenate([v3[0], v3[1], v3[2]], axis=1)
        qt = qt_ref[:, u * blk:(u + 1) * blk]
        zero = jnp.zeros((hd, rep * blk), BF16)
        grp = [jnp.concatenate([qt[(g * rep + r) * hd:(g * rep + r + 1) * hd] for r in range(rep)], axis=1)
               for g in range(SWA_KV_HEADS)]
        qbd = jnp.concatenate([jnp.concatenate([grp[0], zero], axis=1),
                               jnp.concatenate([zero, grp[1]], axis=1)], axis=0)
        st = jnp.dot(kb, qbd, preferred_element_type=F32)
        kpos = start + lax.broadcasted_iota(I32, (band, blk), 0)
        qpos = n * blk + lax.broadcasted_iota(I32, (band, blk), 1)
        dist_i = jnp.abs(qpos - kpos)
        valid = dist_i <= SWA_WINDOW
        dist = dist_i.astype(F32)
        ps, inv = [], []
        for h in range(SWA_HEADS):
            s = jnp.where(valid, st[:, h * blk:(h + 1) * blk] - slope_ref[h] * dist, NEG)
            sink = sink_ref[h]
            m = jnp.maximum(jnp.max(s, axis=0, keepdims=True), sink)
            p = jnp.exp2(s - m)
            inv.append(1.0 / (jnp.sum(p, axis=0, keepdims=True) + jnp.exp2(sink - m)))
            ps.append(p.astype(BF16))
        outs = []
        for g in range(SWA_KV_HEADS):
            pg = jnp.concatenate(ps[g * rep:(g + 1) * rep], axis=1)
            og = jnp.dot(vtb[g * hd:(g + 1) * hd, :], pg, preferred_element_type=F32)
            for r in range(rep):
                outs.append(og[:, r * blk:(r + 1) * blk] * inv[g * rep + r])
        o_ref[u * blk:(u + 1) * blk, :] = jnp.transpose(jnp.concatenate(outs, axis=0)).astype(o_ref.dtype)


def _swa(aqt, ak, avt3, sink, slopes, batch, seq):
    blk = SWA_WINDOW
    rows = min(blk * SWA_BLOCKS_PER_STEP, seq)
    steps = seq // rows
    nb = seq // blk
    t = batch * seq
    smem = pl.BlockSpec(memory_space=pltpu.SMEM)
    return pl.pallas_call(
        _swa_kernel,
        grid=(batch, steps),
        in_specs=[smem, smem,
                  pl.BlockSpec((A_Q_DIM, rows), lambda b, s: (0, b * steps + s)),
                  pl.BlockSpec((seq, A_KV_DIM), lambda b, s: (b, 0)),
                  pl.BlockSpec((nb, A_KV_DIM, blk), lambda b, s: (b, 0, 0))],
        out_specs=pl.BlockSpec((rows, A_Q_DIM), lambda b, s: (b * steps + s, 0)),
        out_shape=jax.ShapeDtypeStruct((t, A_Q_DIM), BF16),
        compiler_params=_cparams("parallel", "parallel"),
        name="swa",
    )(sink, slopes, aqt, ak, avt3)


def _diff_key_tile(i, j, tq, tk, nk):
    return ((i * tq) // tk + j) % nk


def _diff_key_features(pos_in_tile):
    lane = lax.broadcasted_iota(I32, pos_in_tile.shape, 1) % LANES - 2 * DIFF_QK_DIM
    coarse = ((pos_in_tile // 16) * 16).astype(F32)
    fine = (pos_in_tile % 16).astype(F32)
    f = lane % DIFF_NFEAT
    feat = jnp.where(f < 2, coarse, jnp.where(f < 4, fine, 1.0))
    return jnp.where((lane >= 0) & (lane < 2 * DIFF_NFEAT), feat, 0.0)


def _diff_kernel(slope_ref, qt_ref, k_ref, vt_ref, lam_ref, sw_ref, o_ref, qtb_ref, m_ref, acc_ref,
                 s_ref, p_ref, mx_ref, *, lambda_init, n_query_tiles, n_key_tiles):
    t = pl.program_id(1)
    n_pairs = n_query_tiles * n_key_tiles
    tq = qt_ref.shape[1]
    tk = k_ref.shape[0]
    dq = DIFF_QK_DIM
    hw = 2 * dq
    nf = DIFF_NFEAT
    nkt = n_key_tiles

    def reset_stats():
        m_ref[...] = jnp.full(m_ref.shape, NEG, F32)
        acc_ref[...] = jnp.zeros(acc_ref.shape, F32)

    def build_queries():
        ii = lax.broadcasted_iota(I32, (1, 2 * tq), 1)
        ii = jnp.where(ii >= tq, ii - tq, ii).astype(F32)
        qt = qt_ref[...]
        col = lax.broadcasted_iota(I32, (hw, 2 * tq), 1)
        row = lax.broadcasted_iota(I32, (hw, 2 * tq), 0)
        own_map = row // dq == col // tq
        for h in range(DIFF_HEADS):
            qh = qt[h * hw:(h + 1) * hw, :]
            qh2 = jnp.where(own_map, jnp.concatenate([qh, qh], axis=1), jnp.zeros((hw, 2 * tq), BF16))
            sl = jnp.full((1, 2 * tq), slope_ref[h], F32)
            s_hi, s_lo = _bf16_split(sl)
            v_hi, v_lo = _bf16_split(-sl * ii)
            rows = jnp.concatenate([s_hi, s_lo, s_hi, s_lo, v_hi, v_lo], axis=0)
            zrow = jnp.zeros((nf, 2 * tq), F32)
            zero = jnp.zeros((hw - 2 * nf, 2 * tq), F32)
            variants = ([rows, zrow], [zrow, -rows], [zrow, zrow])
            for v, pieces in enumerate(variants):
                qtb_ref[v, h, 0:hw, :] = qh2
                qtb_ref[v, h, hw:2 * hw, :] = jnp.concatenate(pieces + [zero], axis=0).astype(BF16)

    def pair_of(step):
        c = jnp.clip(step, 0, n_pairs - 1)
        return c // nkt, c % nkt

    def key_start(qi, step):
        return _diff_key_tile(qi, step, tq, tk, nkt) * tk

    def score_head(slot, h, variant, dist):
        s = jnp.dot(k_ref[:, h * LANES:(h + 1) * LANES], qtb_ref[variant, h], preferred_element_type=F32)
        if dist is not None:
            s = s - slope_ref[h] * dist
        s_ref[slot, h] = s
        mx_ref[slot, h:h + 1, :] = jnp.max(s, axis=0, keepdims=True)

    def softmax_head(slot, h, shift):
        m_old = m_ref[h:h + 1, :]
        m_new = jnp.maximum(m_old, mx_ref[slot, h:h + 1, :] + shift)
        p_ref[h] = jnp.exp2(s_ref[slot, h] - (m_new - shift)).astype(BF16)
        m_ref[h:h + 1, :] = m_new
        return jnp.exp2(m_old - m_new)

    def value_head(h, alpha):
        pv = jnp.dot(vt_ref[h * DIFF_VROWS:(h + 1) * DIFF_VROWS, :], p_ref[h], preferred_element_type=F32)
        acc_ref[h] = alpha * acc_ref[h] + pv

    def tile_shift(step):
        qi, js = pair_of(step)
        k0 = key_start(qi, js)
        q0 = qi * tq
        sign = jnp.where(js == 0, 0.0, jnp.where(k0 < q0, 1.0, -1.0))
        return sign * (k0 - q0).astype(F32)

    def diagonal_scores(slot):
        qi, _ = pair_of(t)
        kpos = key_start(qi, 0) + lax.broadcasted_iota(I32, (tk, 2 * tq), 0)
        qpos = qi * tq + lax.broadcasted_iota(I32, (tk, 2 * tq), 1) % tq
        dist = jnp.abs(qpos - kpos).astype(F32)
        for h in range(DIFF_HEADS):
            score_head(slot, h, 2, dist)

    def finish_query_tile():
        lp = lam_ref[...]
        lam = (jnp.exp(jnp.sum(lp[0:1] * lp[1:2], axis=-1, keepdims=True))
               - jnp.exp(jnp.sum(lp[2:3] * lp[3:4], axis=-1, keepdims=True)) + lambda_init)
        outs = []
        for h in range(DIFF_HEADS):
            a = acc_ref[h]
            o = a[0:DIFF_V_DIM] / a[DIFF_V_DIM:DIFF_V_DIM + 1]
            o = o[:, 0:tq] - lam * o[:, tq:2 * tq]
            ms = jnp.mean(o * o, axis=0, keepdims=True)
            outs.append(o * lax.rsqrt(ms + NORM_EPS) * sw_ref[...] * (1.0 - lambda_init))
        o_ref[...] = jnp.transpose(jnp.concatenate(outs, axis=0)).astype(o_ref.dtype)

    _, step_in_tile = pair_of(t)
    is_first = (t < n_pairs) & (step_in_tile == 0)

    @pl.when(t == 0)
    def _():
        reset_stats()
        build_queries()
        diagonal_scores(0)

    for parity in range(2):
        @pl.when((t > 0) & (t < n_pairs) & jnp.logical_not(is_first) & (t % 2 == parity))
        def _():
            qi, js = pair_of(t)
            variant = jnp.where(key_start(qi, js) < qi * tq, 0, 1)
            shift = tile_shift(t - 1)
            alphas = []
            for h in range(DIFF_HEADS):
                alphas.append(softmax_head(1 - parity, h, slope_ref[h] * shift))
                score_head(parity, h, variant, None)
                value_head(h, alphas[h])

        if nkt % 2 == 0 and parity == 1:
            continue

        @pl.when((t > 0) & is_first & (t % 2 == parity))
        def _():
            shift = tile_shift(t - 1)
            alphas = [softmax_head(1 - parity, h, slope_ref[h] * shift) for h in range(DIFF_HEADS)]
            build_queries()
            diagonal_scores(parity)
            for h in range(DIFF_HEADS):
                value_head(h, alphas[h])
            finish_query_tile()
            reset_stats()

    @pl.when(t == n_pairs)
    def _():
        shift = tile_shift(t - 1)
        alphas = [softmax_head((n_pairs - 1) % 2, h, slope_ref[h] * shift) for h in range(DIFF_HEADS)]
        for h in range(DIFF_HEADS):
            value_head(h, alphas[h])
        finish_query_tile()


def _diff(cqt, ck, cvt, slopes, lam_params, subln_w_col, lambda_init, batch, seq, tq, tk):
    assert tk % tq == 0 and seq % tk == 0
    nq, nk = seq // tq, seq // tk
    t = batch * seq
    smem = pl.BlockSpec(memory_space=pltpu.SMEM)
    n_pairs = nq * nk

    def query_tile(p):
        return jnp.clip(p, 0, n_pairs - 1) // nk

    def key_tile(p):
        c = jnp.clip(p, 0, n_pairs - 1)
        return _diff_key_tile(c // nk, c % nk, tq, tk, nk)

    return pl.pallas_call(
        functools.partial(_diff_kernel, lambda_init=lambda_init, n_query_tiles=nq, n_key_tiles=nk),
        grid=(batch, n_pairs + 1),
        in_specs=[smem,
                  pl.BlockSpec((DIFF_QK_WIDTH, tq), lambda b, p: (0, b * nq + query_tile(p))),
                  pl.BlockSpec((tk, DIFF_HEADS * LANES), lambda b, p: (b * nk + key_tile(p), 0)),
                  pl.BlockSpec((DIFF_HEADS * DIFF_VROWS, tk), lambda b, p: (0, b * nk + key_tile(p - 1))),
                  pl.BlockSpec(lam_params.shape, lambda b, p: (0, 0)),
                  pl.BlockSpec(subln_w_col.shape, lambda b, p: (0, 0))],
        out_specs=pl.BlockSpec((tq, DIFF_V_WIDTH), lambda b, p: (b * nq + query_tile(p - 1), 0)),
        out_shape=jax.ShapeDtypeStruct((t, DIFF_V_WIDTH), BF16),
        scratch_shapes=[pltpu.VMEM((3, DIFF_HEADS, LANES, 2 * tq), BF16),
                        pltpu.VMEM((DIFF_HEADS, 2 * tq), F32),
                        pltpu.VMEM((DIFF_HEADS, DIFF_VROWS, 2 * tq), F32),
                        pltpu.VMEM((2, DIFF_HEADS, tk, 2 * tq), F32),
                        pltpu.VMEM((DIFF_HEADS, tk, 2 * tq), BF16),
                        pltpu.VMEM((2, DIFF_HEADS, 2 * tq), F32)],
        compiler_params=_cparams("parallel", "arbitrary"),
        name="diffattn",
    )(slopes, cqt, ck, cvt, lam_params, subln_w_col)


def _ssd_direction(fwd, off, xs_ref, bt_ref, cm_ref, dt_ref, dtb_ref, alog_ref, dsk_ref, y_ref, state_ref):
    q = SSD_CHUNK
    rows = slice(off, off + q)
    lane0 = 0 if fwd else SSD_HEADS
    dt_all = _softplus(dt_ref[rows, :] + dtb_ref[...])
    dta_all = dt_all * -jnp.exp(alog_ref[...])
    row = lax.broadcasted_iota(I32, (q, q), 0)
    col = lax.broadcasted_iota(I32, (q, q), 1)
    keep = (row >= col) if fwd else (row <= col)
    tri = keep.astype(BF16)
    part_hi = dta_all.astype(BF16)
    rest = dta_all - part_hi.astype(F32)
    part_mid = rest.astype(BF16)
    part_lo = (rest - part_mid.astype(F32)).astype(BF16)
    da_all = (jnp.dot(tri, part_hi, preferred_element_type=F32)
              + jnp.dot(tri, part_mid, preferred_element_type=F32)
              + jnp.dot(tri, part_lo, preferred_element_type=F32))
    da_all_t = jnp.transpose(da_all)
    dt_all_t = jnp.transpose(dt_all)
    tot_all = jnp.sum(dta_all, axis=0, keepdims=True)

    xs = xs_ref[rows, :]
    bmt = bt_ref[:, rows]
    cm = cm_ref[rows, :]
    rep = SSD_HEADS // SSD_GROUPS
    ns = SSD_STATE
    lane = lax.broadcasted_iota(I32, (q, LANES), 1)
    low = lane < SSD_HEAD_DIM
    low_n = lax.broadcasted_iota(I32, (ns, LANES), 1) < SSD_HEAD_DIM
    cm_g = [jnp.where((lane // ns) == g, cm, 0.0) for g in range(SSD_GROUPS)]
    bmt_b = bmt.astype(BF16)
    g_mats = [jnp.dot(cm_g[g].astype(BF16), bmt_b, preferred_element_type=F32)
              for g in range(SSD_GROUPS)]
    zeros_n = jnp.zeros((ns, LANES), BF16)
    ys = []
    for pair in range(SSD_HEADS // 2):
        x_pair = xs[:, pair * LANES:(pair + 1) * LANES]
        x_pair_b = x_pair.astype(BF16)
        st = state_ref[pair]
        st_b = st.astype(BF16)
        y_heads, s_heads, keep_heads = [], [], []
        for h in (2 * pair, 2 * pair + 1):
            g = h // rep
            ln = lane0 + h
            dac = da_all[:, ln:ln + 1]
            dar = da_all_t[ln:ln + 1, :]
            dtr = dt_all_t[ln:ln + 1, :]
            tot = tot_all[:, ln:ln + 1]
            dac_b = jnp.broadcast_to(dac, (q, q))
            decay = jnp.exp(jnp.where(keep, dac_b - dar, NEG))
            y = jnp.dot((g_mats[g] * decay * dtr).astype(BF16), x_pair_b, preferred_element_type=F32)
            st_ext = jnp.concatenate([st_b, zeros_n] if g == 0 else [zeros_n, st_b], axis=0)
            c_in = (cm_g[g] * jnp.exp(dac_b)).astype(BF16)
            y_heads.append(y + jnp.dot(c_in, st_ext, preferred_element_type=F32))
            to_end = jnp.exp(tot - dar) * dtr
            b_out = (bmt[g * ns:(g + 1) * ns, :] * to_end).astype(BF16)
            s_heads.append(jnp.dot(b_out, x_pair_b, preferred_element_type=F32))
            keep_heads.append(jnp.exp(tot))
        state_keep = jnp.where(low_n, jnp.broadcast_to(keep_heads[0], (ns, LANES)),
                               jnp.broadcast_to(keep_heads[1], (ns, LANES)))
        state_ref[pair] = st * state_keep + jnp.where(low_n, s_heads[0], s_heads[1])
        y = jnp.where(low, y_heads[0], y_heads[1])
        if fwd:
            y = y + dsk_ref[:, pair * LANES:(pair + 1) * LANES] * x_pair
        ys.append(y)
    y_ref[rows, :] = jnp.concatenate(ys, axis=-1)


def _ssd_kernel(xsf_ref, btf_ref, cmf_ref, dtf_ref, xsb_ref, btb_ref, cmb_ref, dtb_in_ref,
                dtbias_ref, alog_ref, dsk_ref, yf_ref, yb_ref, state_ref):
    @pl.when(pl.program_id(1) == 0)
    def _():
        state_ref[...] = jnp.zeros(state_ref.shape, F32)

    n = SSD_CHUNKS_PER_STEP
    for u in range(n):
        _ssd_direction(True, u * SSD_CHUNK, xsf_ref, btf_ref, cmf_ref, dtf_ref, dtbias_ref, alog_ref, dsk_ref,
                       yf_ref, state_ref.at[0])
        _ssd_direction(False, (n - 1 - u) * SSD_CHUNK, xsb_ref, btb_ref, cmb_ref, dtb_in_ref, dtbias_ref,
                       alog_ref, dsk_ref, yb_ref, state_ref.at[1])


def _ssd(xs, bt, cm, dt_raw, dt_bias, a_log, d_skip, batch, seq):
    q = SSD_CHUNK * SSD_CHUNKS_PER_STEP
    nc = seq // q
    t = batch * seq
    full = lambda b, c: (0, 0)
    fw = lambda b, c: b * nc + c
    bw = lambda b, c: b * nc + nc - 1 - c

    def specs(idx):
        return [pl.BlockSpec((q, SSD_D_INNER), lambda b, c: (idx(b, c), 0)),
                pl.BlockSpec((SSD_BC_DIM, q), lambda b, c: (0, idx(b, c))),
                pl.BlockSpec((q, SSD_BC_DIM), lambda b, c: (idx(b, c), 0)),
                pl.BlockSpec((q, LANES), lambda b, c: (idx(b, c), 0))]

    return pl.pallas_call(
        _ssd_kernel,
        grid=(batch, nc),
        in_specs=specs(fw) + specs(bw) + [pl.BlockSpec(dt_bias.shape, full), pl.BlockSpec(a_log.shape, full),
                                          pl.BlockSpec(d_skip.shape, full)],
        out_specs=[pl.BlockSpec((q, SSD_D_INNER), lambda b, c: (fw(b, c), 0)),
                   pl.BlockSpec((q, SSD_D_INNER), lambda b, c: (bw(b, c), 0))],
        out_shape=[jax.ShapeDtypeStruct((t, SSD_D_INNER), F32), jax.ShapeDtypeStruct((t, SSD_D_INNER), F32)],
        scratch_shapes=[pltpu.VMEM((2, SSD_HEADS // 2, SSD_STATE, 2 * SSD_HEAD_DIM), F32)],
        compiler_params=_cparams("parallel", "arbitrary"),
        name="ssd",
    )(xs, bt, cm, dt_raw, xs, bt, cm, dt_raw, dt_bias, a_log, d_skip)


def _outproj_kernel(x_ref, ya_ref, yf_ref, yb_ref, z_ref, snw_ref, yc_ref, wo_ref, fnw_ref, wrt_ref, br_ref,
                    xn_ref, ri_ref, rf_ref, cnt_ref, tri_ref, carry_ref):
    step = pl.program_id(0)
    tm = x_ref.shape[0]

    @pl.when(step == 0)
    def _():
        carry_ref[...] = jnp.zeros(carry_ref.shape, F32)
        r = lax.broadcasted_iota(I32, (tm, tm), 0)
        cc = lax.broadcasted_iota(I32, (tm, tm), 1)
        tri_ref[...] = (r <= cc).astype(BF16)

    y = (yf_ref[...] + yb_ref[...]) * _silu(z_ref[...])
    yb = _rms(y, snw_ref[...]).astype(BF16)
    acc = jnp.dot(ya_ref[...], wo_ref[0:A_Q_DIM, :], preferred_element_type=F32)
    acc = acc + jnp.dot(yb, wo_ref[A_Q_DIM:A_Q_DIM + SSD_D_INNER, :], preferred_element_type=F32)
    acc = acc + jnp.dot(yc_ref[...], wo_ref[A_Q_DIM + SSD_D_INNER:, :], preferred_element_type=F32)
    xn = x_ref[...] + acc
    xn_ref[...] = xn

    h = _rms(xn, fnw_ref[...])
    h_hi = h.astype(BF16)
    h_lo = (h - h_hi.astype(F32)).astype(BF16)
    nt = (((1,), (1,)), ((), ()))
    logits = (lax.dot_general(wrt_ref[0], h_hi, nt, preferred_element_type=F32)
              + lax.dot_general(wrt_ref[0], h_lo, nt, preferred_element_type=F32)
              + lax.dot_general(wrt_ref[1], h_hi, nt, preferred_element_type=F32)) + br_ref[...]
    ne, epg, ng = N_EXPERTS, EXPERTS_PER_GROUP, N_EXPERT_GROUPS
    gl = logits[ne:ne + ng, :]
    gmax = jnp.max(gl, axis=0, keepdims=True)
    g_sel = jnp.full((1, tm), float(ng - 1), F32)
    for g in range(ng - 2, -1, -1):
        g_sel = jnp.where(gl[g:g + 1, :] == gmax, float(g), g_sel)
    g_gate = 1.0 / jnp.sum(jnp.exp(gl - gmax), axis=0, keepdims=True)
    e_in = logits[0:epg, :]
    for g in range(1, ng):
        e_in = jnp.where(g_sel == float(g), logits[g * epg:(g + 1) * epg, :], e_in)
    sub = lax.broadcasted_iota(I32, (epg, tm), 0).astype(F32)
    m1 = jnp.max(e_in, axis=0, keepdims=True)
    i1 = jnp.min(jnp.where(e_in == m1, sub, float(epg)), axis=0, keepdims=True)
    rest = jnp.where(sub == i1, NEG, e_in)
    m2 = jnp.max(rest, axis=0, keepdims=True)
    i2 = jnp.min(jnp.where(rest == m2, sub, float(epg)), axis=0, keepdims=True)
    r = jnp.exp(m2 - m1)
    c1 = g_gate / (1.0 + r)
    c2 = g_gate * r / (1.0 + r)
    e1 = (g_sel * epg + i1).astype(I32)
    e2 = (g_sel * epg + i2).astype(I32)

    erow = lax.broadcasted_iota(I32, (ne, tm), 0)
    hit1 = erow == e1
    hit2 = erow == e2
    oh = jnp.where(hit1 | hit2, 1.0, 0.0)
    incl = jnp.dot(oh.astype(BF16), tri_ref[...], preferred_element_type=F32)
    before = incl - oh + carry_ref[:, 0:1]
    rank1 = jnp.sum(jnp.where(hit1, before, 0.0), axis=0, keepdims=True)
    rank2 = jnp.sum(jnp.where(hit2, before, 0.0), axis=0, keepdims=True)
    carry_ref[...] = carry_ref[...] + jnp.sum(oh, axis=1, keepdims=True)
    cnt_ref[...] = carry_ref[...]
    zi = jnp.zeros((1, tm), I32)
    ri_ref[...] = jnp.concatenate([e1, e2, rank1.astype(I32), rank2.astype(I32), zi, zi, zi, zi], axis=0)
    zf = jnp.zeros((1, tm), F32)
    rf_ref[...] = jnp.concatenate([c1, c2, zf, zf, zf, zf, zf, zf], axis=0)


def _outproj(x2, ya, yf, yb, z, ssd_norm_w, yc, w_out, ffn_norm_w, wrt, br, tm):
    t, d = x2.shape
    row = lambda i: (i, 0)
    full = lambda i: (0, 0)
    return pl.pallas_call(
        _outproj_kernel,
        grid=(t // tm,),
        in_specs=[pl.BlockSpec((tm, d), row),
                  pl.BlockSpec((tm, A_Q_DIM), row),
                  pl.BlockSpec((tm, SSD_D_INNER), row),
                  pl.BlockSpec((tm, SSD_D_INNER), row),
                  pl.BlockSpec((tm, SSD_D_INNER), row),
                  pl.BlockSpec(ssd_norm_w.shape, full),
                  pl.BlockSpec((tm, DIFF_V_WIDTH), row),
                  pl.BlockSpec(w_out.shape, full),
                  pl.BlockSpec(ffn_norm_w.shape, full),
                  pl.BlockSpec(wrt.shape, lambda i: (0, 0, 0)),
                  pl.BlockSpec(br.shape, full)],
        out_specs=[pl.BlockSpec((tm, d), row),
                   pl.BlockSpec((SUBLANES, tm), lambda i: (0, i)),
                   pl.BlockSpec((SUBLANES, tm), lambda i: (0, i)),
                   pl.BlockSpec((N_EXPERTS, LANES), full)],
        out_shape=[jax.ShapeDtypeStruct((t, d), F32),
                   jax.ShapeDtypeStruct((SUBLANES, t), I32),
                   jax.ShapeDtypeStruct((SUBLANES, t), F32),
                   jax.ShapeDtypeStruct((N_EXPERTS, LANES), F32)],
        scratch_shapes=[pltpu.VMEM((tm, tm), BF16), pltpu.VMEM((N_EXPERTS, LANES), F32)],
        compiler_params=_cparams("arbitrary"),
        name="outproj_router",
    )(x2, ya, yf, yb, z, ssd_norm_w, yc, w_out, ffn_norm_w, wrt, br)


_PAD_PIECES = tuple(1 << b for b in reversed(range(MOE_ROW_TILE.bit_length() - 1)))


def _dispatch_kernel(slot1_ref, slot2_ref, pstart_ref, plen_ref, nused_ref, x_ref, xs_hbm, zero_ref, xbuf, sem):
    i = pl.program_id(0)
    tm = x_ref.shape[0]

    @pl.when(i == 0)
    def _():
        zero_ref[...] = jnp.zeros(zero_ref.shape, F32)

        def pieces(e, wait):
            n = plen_ref[e]
            first = pstart_ref[e]
            off = first + n
            for b in _PAD_PIECES:
                off = off - (n & b)
                dst = pl.ds(pl.multiple_of(off, b), b) if b >= SUBLANES else None
                if dst is not None:
                    @pl.when((n & b) != 0)
                    def _():
                        cp = pltpu.make_async_copy(zero_ref.at[pl.ds(0, b)], xs_hbm.at[dst], sem.at[2])
                        cp.wait() if wait else cp.start()

            for u in range(SUBLANES - 1):
                @pl.when(u < (n & (SUBLANES - 1)))
                def _():
                    cp = pltpu.make_async_copy(zero_ref.at[pl.ds(0, 1)], xs_hbm.at[pl.ds(first + u, 1)],
                                               sem.at[2])
                    cp.wait() if wait else cp.start()

        def tail(tile, wait):
            big = _PAD_PIECES[0]
            for part in range(MOE_ROW_TILE // big):
                dst = xs_hbm.at[pl.ds(pl.multiple_of(tile * MOE_ROW_TILE + part * big, big), big)]
                cp = pltpu.make_async_copy(zero_ref, dst, sem.at[2])
                cp.wait() if wait else cp.start()

        def loop(fn, lo, hi, wait):
            def body(k, carry):
                fn(k, wait)
                return carry

            lax.fori_loop(lo, hi, body, 0)

        n_tiles = xs_hbm.shape[0] // MOE_ROW_TILE
        for wait in (False, True):
            loop(pieces, 0, N_EXPERTS, wait)
            loop(tail, nused_ref[0], n_tiles, wait)

    base = i * tm
    last = pl.num_programs(0) - 1

    def drain(slot):
        for _ in range(2):
            pltpu.make_async_copy(xbuf.at[slot], xs_hbm.at[pl.ds(0, tm)], sem.at[slot]).wait()

    for parity in range(2):
        @pl.when(i % 2 == parity)
        def _():
            buf = xbuf.at[parity]
            buf[...] = x_ref[...]
            for r in range(tm):
                src = buf.at[pl.ds(r, 1)]
                pltpu.make_async_copy(src, xs_hbm.at[pl.ds(slot1_ref[base + r], 1)],
                                      sem.at[parity]).start(priority=0)
                pltpu.make_async_copy(src, xs_hbm.at[pl.ds(slot2_ref[base + r], 1)],
                                      sem.at[parity]).start(priority=1)

            @pl.when(i > 0)
            def _():
                drain(1 - parity)

            @pl.when(i == last)
            def _():
                drain(parity)


def _dispatch(xn, slot1, slot2, pad_start, pad_len, n_used, n_rows, tm):
    t, d = xn.shape
    grid_spec = pltpu.PrefetchScalarGridSpec(
        num_scalar_prefetch=5,
        grid=(t // tm,),
        in_specs=[pl.BlockSpec((tm, d), lambda i, s1, s2, ps, pn, nu: (i, 0))],
        out_specs=pl.BlockSpec(memory_space=pl.ANY),
        scratch_shapes=[pltpu.VMEM((_PAD_PIECES[0], d), F32), pltpu.VMEM((2, tm, d), F32),
                        pltpu.SemaphoreType.DMA((3,))],
    )
    return pl.pallas_call(
        _dispatch_kernel,
        grid_spec=grid_spec,
        out_shape=jax.ShapeDtypeStruct((n_rows, d), F32),
        compiler_params=_cparams("arbitrary"),
        name="moe_dispatch",
    )(slot1, slot2, pad_start, pad_len, n_used, xn)


def _moe_kernel(texp_ref, nused_ref, first_ref, wslot_ref, next_ref, x_ref, fnw_ref, wg_hbm, wu_hbm, wd_hbm,
                y_ref, wg_buf, wu_buf, wd_buf, sem, *, layer):
    i = pl.program_id(0)

    def weight_copies(expert, slot):
        return [pltpu.make_async_copy(hbm.at[layer, expert], buf.at[slot], sem.at[slot])
                for hbm, buf in ((wg_hbm, wg_buf), (wu_hbm, wu_buf), (wd_hbm, wd_buf))]

    @pl.when(i < nused_ref[0])
    def _():
        slot = wslot_ref[i]

        @pl.when(i == 0)
        def _():
            for cp in weight_copies(texp_ref[0], 0):
                cp.start()

        @pl.when(first_ref[i] == 1)
        def _():
            for cp in weight_copies(texp_ref[i], slot):
                cp.wait()

            @pl.when(next_ref[i] >= 0)
            def _():
                for cp in weight_copies(next_ref[i], 1 - slot):
                    cp.start()

        h = _rms(x_ref[...], fnw_ref[...]).astype(BF16)
        hg = jnp.dot(h, wg_buf[slot].astype(BF16), preferred_element_type=F32)
        hu = jnp.dot(h, wu_buf[slot].astype(BF16), preferred_element_type=F32)
        act = (_silu(hg) * hu).astype(BF16)
        y_ref[...] = jnp.dot(act, wd_buf[slot].astype(BF16), preferred_element_type=F32)

    @pl.when(i >= nused_ref[0])
    def _():
        y_ref[...] = jnp.zeros(y_ref.shape, F32)


def _moe(xs, ffn_norm_w, w_gate, w_up, w_down, layer, tile_expert, n_used):
    n_rows, d = xs.shape
    f = w_gate.shape[-1]
    tr = MOE_ROW_TILE

    n_tiles = n_rows // tr

    idx = jnp.arange(n_tiles, dtype=I32)
    prev_expert = jnp.concatenate([jnp.full((1,), -1, I32), tile_expert[:-1]])
    first = ((idx < n_used[0]) & (tile_expert != prev_expert)).astype(I32)
    wslot = ((jnp.cumsum(first) - 1) % 2).astype(I32)
    first_pos = jnp.where(first == 1, idx, n_tiles)
    next_first = jnp.concatenate([lax.cummin(first_pos, reverse=True)[1:], jnp.full((1,), n_tiles, I32)])
    next_expert = jnp.where(next_first < n_tiles, tile_expert[jnp.minimum(next_first, n_tiles - 1)], -1).astype(I32)

    def used(i, nu):
        return jnp.maximum(jnp.minimum(i, nu[0] - 1), 0)

    anyspace = pl.BlockSpec(memory_space=pl.ANY)
    grid_spec = pltpu.PrefetchScalarGridSpec(
        num_scalar_prefetch=5,
        grid=(n_tiles,),
        in_specs=[pl.BlockSpec((tr, d), lambda i, te, nu, fi, ws, nx: (used(i, nu), 0)),
                  pl.BlockSpec(ffn_norm_w.shape, lambda i, te, nu, fi, ws, nx: (0, 0)),
                  anyspace, anyspace, anyspace],
        out_specs=pl.BlockSpec((tr, d), lambda i, te, nu, fi, ws, nx: (i, 0)),
        scratch_shapes=[pltpu.VMEM((2, d, f), F32), pltpu.VMEM((2, d, f), F32), pltpu.VMEM((2, f, d), F32),
                        pltpu.SemaphoreType.DMA((2,))],
    )
    return pl.pallas_call(
        functools.partial(_moe_kernel, layer=layer),
        grid_spec=grid_spec,
        out_shape=jax.ShapeDtypeStruct((n_rows, d), F32),
        compiler_params=_cparams("arbitrary"),
        name="moe_experts",
    )(tile_expert, n_used, first, wslot, next_expert, xs, ffn_norm_w, w_gate, w_up, w_down)


def _combine_kernel(slot1_ref, slot2_ref, x_ref, cw_ref, nw_ref, y_hbm, o_ref, ybuf, sem, *, final_norm):
    i = pl.program_id(0)
    n = pl.num_programs(0)
    tm = x_ref.shape[0]

    def start_gather(tile, slot):
        base = tile * tm
        for r in range(tm):
            pltpu.make_async_copy(y_hbm.at[pl.ds(slot1_ref[base + r], 1)], ybuf.at[slot, 0, pl.ds(r, 1)],
                                  sem.at[slot]).start(priority=0)
            pltpu.make_async_copy(y_hbm.at[pl.ds(slot2_ref[base + r], 1)], ybuf.at[slot, 1, pl.ds(r, 1)],
                                  sem.at[slot]).start(priority=1)

    def compute(slot):
        for k in range(2):
            pltpu.make_async_copy(y_hbm.at[pl.ds(0, tm)], ybuf.at[slot, k], sem.at[slot]).wait()
        cw = cw_ref[...]
        out = x_ref[...] + cw[:, 0:1] * ybuf[slot, 0] + cw[:, 1:2] * ybuf[slot, 1]
        if final_norm:
            out = _rms(out, nw_ref[...])
        o_ref[...] = out

    @pl.when(i == 0)
    def _():
        start_gather(0, 0)

    for parity in range(2):
        @pl.when(i % 2 == parity)
        def _():
            @pl.when(i + 1 < n)
            def _():
                start_gather(i + 1, 1 - parity)

            compute(parity)


def _combine(xn, cw, norm_w, y_sorted, slot1, slot2, tm, final_norm):
    t, d = xn.shape
    grid_spec = pltpu.PrefetchScalarGridSpec(
        num_scalar_prefetch=2,
        grid=(t // tm,),
        in_specs=[pl.BlockSpec((tm, d), lambda i, s1, s2: (i, 0)),
                  pl.BlockSpec((tm, cw.shape[1]), lambda i, s1, s2: (i, 0)),
                  pl.BlockSpec(norm_w.shape, lambda i, s1, s2: (0, 0)),
                  pl.BlockSpec(memory_space=pl.ANY)],
        out_specs=pl.BlockSpec((tm, d), lambda i, s1, s2: (i, 0)),
        scratch_shapes=[pltpu.VMEM((2, 2, tm, d), F32), pltpu.SemaphoreType.DMA((2,))],
    )
    return pl.pallas_call(
        functools.partial(_combine_kernel, final_norm=final_norm),
        grid_spec=grid_spec,
        out_shape=jax.ShapeDtypeStruct((t, d), F32),
        compiler_params=_cparams("arbitrary"),
        name="moe_combine",
    )(slot1, slot2, xn, cw, norm_w, y_sorted)


def _pad_lanes(v):
    v = v.reshape(1, -1).astype(F32)
    return jnp.pad(v, ((0, 0), (0, LANES - v.shape[1])))


def kernel(x, attn_norm_w, w_in, swa_sink, ssd_conv_w, ssd_conv_b, ssd_dt_bias, ssd_a_log, ssd_d, ssd_norm_w,
           diff_lambda, diff_subln_w, w_out, ffn_norm_w, w_router_group, b_router_group, w_router_expert,
           b_router_expert, w_gate, w_up, w_down, final_norm_w):
    return _forward(x, attn_norm_w, w_in, swa_sink, ssd_conv_w, ssd_conv_b, ssd_dt_bias, ssd_a_log, ssd_d,
                    ssd_norm_w, diff_lambda, diff_subln_w, w_out, ffn_norm_w, w_router_group, b_router_group,
                    w_router_expert, b_router_expert, w_gate, w_up, w_down, final_norm_w)


def _forward(x, attn_norm_w, w_in, swa_sink, ssd_conv_w, ssd_conv_b, ssd_dt_bias, ssd_a_log, ssd_d, ssd_norm_w,
             diff_lambda, diff_subln_w, w_out, ffn_norm_w, w_router_group, b_router_group, w_router_expert,
             b_router_expert, w_gate, w_up, w_down, final_norm_w, tm=512, tq=512, tk=1024, tmc=512):
    batch, seq, d = x.shape
    depth = w_in.shape[0]
    t = batch * seq
    tr = MOE_ROW_TILE
    n_tiles = (2 * t) // tr + N_EXPERTS
    slopes = jnp.exp2(-8.0 * jnp.arange(1, N_ALIBI_HEADS + 1, dtype=F32) / N_ALIBI_HEADS)
    swa_slopes, diff_slopes = slopes[:SWA_HEADS], slopes[SWA_HEADS:]

    sizes = [A_Q_DIM, A_KV_DIM, A_KV_DIM, SSD_D_INNER, SSD_CONV_DIM, SSD_DT_DIM, DIFF_QK_WIDTH, DIFF_QK_WIDTH,
             DIFF_V_WIDTH]
    offs = [0]
    for s in sizes:
        offs.append(offs[-1] + s)
    o_aq, o_ak, o_av, o_z, o_xbc, o_dt, o_cq, o_ck, o_cv, o_end = offs

    x2 = x.reshape(t, d)
    for l in range(depth):
        w = w_in[l]
        hw = 2 * DIFF_QK_DIM
        w_ck = jnp.pad(w[:, o_ck:o_cv].reshape(d, DIFF_HEADS, hw), ((0, 0), (0, 0), (0, LANES - hw)))
        w_main = jnp.concatenate(
            [w[:, o_ak:o_av], w[:, o_z:o_dt], w_ck.reshape(d, DIFF_HEADS * LANES), w[:, o_dt:o_cq],
             jnp.zeros((d, LANES - SSD_DT_DIM), w.dtype)], axis=1).astype(BF16)
        w_t = jnp.concatenate([w[:, o_aq:o_ak], w[:, o_av:o_z], w[:, o_cq:o_ck], w[:, o_cv:o_end]],
                              axis=1).T.astype(BF16)
        ak, z, xs, cm, ck, dt_raw, aqt, avt3, cqt, cvt, bt = _inproj(
            x2, attn_norm_w[l].reshape(1, d), w_main, w_t, ssd_conv_w[l].astype(F32),
            ssd_conv_b[l].reshape(1, -1).astype(F32), tm, seq, tk)

        ya = _swa(aqt, ak, avt3, swa_sink[l].astype(F32) * LOG2E, swa_slopes * LOG2E, batch, seq)
        lambda_init = 0.8 - 0.6 * math.exp(-0.3 * l)
        yc = _diff(cqt, ck, cvt, diff_slopes * LOG2E, diff_lambda[l].astype(F32),
                   diff_subln_w[l].reshape(DIFF_V_DIM, 1).astype(F32), lambda_init, batch, seq, tq, tk)
        yf, yb = _ssd(xs, bt, cm, dt_raw, _pad_lanes(ssd_dt_bias[l]), _pad_lanes(ssd_a_log[l]),
                      jnp.repeat(ssd_d[l].astype(F32), SSD_HEAD_DIM).reshape(1, SSD_D_INNER), batch, seq)

        wr32 = jnp.concatenate([w_router_expert[l], w_router_group[l],
                                jnp.zeros((d, SUBLANES - N_EXPERT_GROUPS), F32)], axis=1).T.astype(F32)
        wr_hi = wr32.astype(BF16)
        wrt = jnp.stack([wr_hi, (wr32 - wr_hi.astype(F32)).astype(BF16)])
        br = jnp.concatenate([b_router_expert[l], b_router_group[l],
                              jnp.zeros((SUBLANES - N_EXPERT_GROUPS,), F32)]).reshape(-1, 1).astype(F32)
        xn, ri, rf, cnt = _outproj(x2, ya, yf, yb, z, ssd_norm_w[l].reshape(1, -1), yc, w_out[l].astype(BF16),
                                   ffn_norm_w[l].reshape(1, d), wrt, br, tm)

        counts = cnt[:, 0].astype(I32)
        padded = ((counts + tr - 1) // tr) * tr
        ends = jnp.cumsum(padded)
        starts = ends - padded
        experts = jnp.arange(N_EXPERTS, dtype=I32)[:, None]

        def slot_of(e, rank):
            return jnp.sum(jnp.where(e[None, :] == experts, starts[:, None], 0), axis=0) + rank

        slot1 = slot_of(ri[0], ri[2])
        slot2 = slot_of(ri[1], ri[3])
        tile_start = jnp.arange(n_tiles, dtype=I32) * tr
        tile_expert = jnp.minimum(jnp.sum(ends[None, :] <= tile_start[:, None], axis=1), N_EXPERTS - 1).astype(I32)
        n_used = (ends[-1] // tr).astype(I32).reshape(1)

        xs_sorted = _dispatch(xn, slot1, slot2, starts + counts, padded - counts, n_used, n_tiles * tr, tmc)
        y_sorted = _moe(xs_sorted, ffn_norm_w[l].reshape(1, d), w_gate, w_up, w_down, l, tile_expert, n_used)
        last = l == depth - 1
        x2 = _combine(xn, rf.T, final_norm_w.reshape(1, d), y_sorted, slot1, slot2, tmc, last)
    return x2.reshape(batch, seq, d)
```

```python
import functools
import math

import jax
import jax.numpy as jnp
from jax import lax
from jax.experimental import pallas as pl
from jax.experimental.pallas import tpu as pltpu

F32 = jnp.float32
BF16 = jnp.bfloat16
I32 = jnp.int32

HEAD_DIM = 64
SWA_HEADS = 6
SWA_KV_HEADS = 2
SWA_WINDOW = 128
SSD_HEADS = 6
SSD_HEAD_DIM = 64
SSD_GROUPS = 2
SSD_STATE = 64
SSD_CONV = 5
DIFF_HEADS = 4
DIFF_QK_DIM = 32
DIFF_V_DIM = 64
N_EXPERT_GROUPS = 4
EXPERTS_PER_GROUP = 8
N_EXPERTS = N_EXPERT_GROUPS * EXPERTS_PER_GROUP
NORM_EPS = 1e-6

A_Q_DIM = SWA_HEADS * HEAD_DIM
A_KV_DIM = SWA_KV_HEADS * HEAD_DIM
SSD_D_INNER = SSD_HEADS * SSD_HEAD_DIM
SSD_BC_DIM = SSD_GROUPS * SSD_STATE
SSD_CONV_DIM = SSD_D_INNER + 2 * SSD_BC_DIM
SSD_DT_DIM = 2 * SSD_HEADS
DIFF_QK_WIDTH = DIFF_HEADS * 2 * DIFF_QK_DIM
DIFF_V_WIDTH = DIFF_HEADS * DIFF_V_DIM
N_ALIBI_HEADS = SWA_HEADS + DIFF_HEADS

LANES = 128
SUBLANES = 8
VMEM_LIMIT = 56 * 1024 * 1024
NEG = -1e30
LOG2E = math.log2(math.e)

SSD_CHUNK = 128
SSD_CHUNKS_PER_STEP = 4
SWA_BLOCKS_PER_STEP = 8
MOE_ROW_TILE = 256
DIFF_PAIR = 2 * 2 * DIFF_QK_DIM
DIFF_VROWS = 80
DIFF_NFEAT = 6


def _cparams(*sem):
    return pltpu.CompilerParams(dimension_semantics=sem, vmem_limit_bytes=VMEM_LIMIT)


def _rms(x, w):
    return x * lax.rsqrt(jnp.mean(x * x, axis=-1, keepdims=True) + NORM_EPS) * w


def _silu(x):
    return x / (1.0 + jnp.exp(-x))


def _softplus(x):
    return jnp.maximum(x, 0.0) + jnp.log(1.0 + jnp.exp(-jnp.abs(x)))


def _bf16_split(x):
    hi = x.astype(BF16).astype(F32)
    lo = (x - hi).astype(BF16).astype(F32)
    return hi, lo


_C_AK = 0
_C_Z = _C_AK + A_KV_DIM
_C_XBC = _C_Z + SSD_D_INNER
_C_CK = _C_XBC + SSD_CONV_DIM
_C_DT = _C_CK + DIFF_HEADS * LANES
_C_END = _C_DT + LANES
_R_AQ = 0
_R_AV = _R_AQ + A_Q_DIM
_R_CQ = _R_AV + A_KV_DIM
_R_CV = _R_CQ + DIFF_QK_WIDTH
_R_END = _R_CV + DIFF_V_WIDTH


def _inproj_kernel(x_ref, xp_ref, xn_ref, nw_ref, w_ref, wt_ref, cw_ref, cb_ref,
                   ak_ref, z_ref, xs_ref, cm_ref, ck_ref, dt_ref, aqt_ref, avt_ref, cqt_ref, cvt_ref, bt_ref,
                   *, tiles_per_seq, diff_key_tile):
    i = pl.program_id(0)
    tm = x_ref.shape[0]
    nw = nw_ref[...]
    h = _rms(x_ref[...], nw).astype(BF16)

    def seg(lo, hi):
        return jnp.dot(h, w_ref[:, lo:hi], preferred_element_type=F32)

    ak_ref[...] = seg(_C_AK, _C_Z).astype(BF16)
    z_ref[...] = seg(_C_Z, _C_XBC)
    pos = (i * tm + lax.broadcasted_iota(I32, (tm, _C_DT - _C_CK), 0)) % diff_key_tile
    ck_ref[...] = (seg(_C_CK, _C_DT) + _diff_key_features(pos)).astype(BF16)
    dt_ref[...] = seg(_C_DT, _C_END)

    w_xbc = w_ref[:, _C_XBC:_C_CK]
    first = i % tiles_per_seq == 0
    last = i % tiles_per_seq == tiles_per_seq - 1
    prev = jnp.dot(_rms(xp_ref[...], nw).astype(BF16), w_xbc, preferred_element_type=F32)
    nxt = jnp.dot(_rms(xn_ref[...], nw).astype(BF16), w_xbc, preferred_element_type=F32)
    prev = jnp.where(first, 0.0, prev)
    nxt = jnp.where(last, 0.0, nxt)
    ext = jnp.concatenate([prev, seg(_C_XBC, _C_CK), nxt], axis=0)
    half = SSD_CONV // 2
    conv = cb_ref[...]
    for k in range(SSD_CONV):
        off = SUBLANES - half + k
        conv = conv + cw_ref[k:k + 1, :] * ext[off:off + tm, :]
    u = _silu(conv)
    xs_ref[...] = u[:, :SSD_D_INNER]
    bt_ref[...] = jnp.transpose(u[:, SSD_D_INNER:SSD_D_INNER + SSD_BC_DIM])
    cm_ref[...] = u[:, SSD_D_INNER + SSD_BC_DIM:]

    tr = lax.dot_general(wt_ref[...], h, (((1,), (1,)), ((), ())), preferred_element_type=F32)
    aqt_ref[...] = (tr[_R_AQ:_R_AV] * (HEAD_DIM ** -0.5 * LOG2E)).astype(BF16)
    avt = tr[_R_AV:_R_CQ].astype(BF16)
    for c in range(tm // LANES):
        avt_ref[c] = avt[:, c * LANES:(c + 1) * LANES]
    cqt_ref[...] = (tr[_R_CQ:_R_CV] * (DIFF_QK_DIM ** -0.5 * LOG2E)).astype(BF16)
    pad = DIFF_VROWS - DIFF_V_DIM
    ones_row = (lax.broadcasted_iota(I32, (pad, tm), 0) == 0).astype(BF16)
    for hh in range(DIFF_HEADS):
        cvt_ref[hh * DIFF_VROWS:hh * DIFF_VROWS + DIFF_V_DIM, :] = (
            tr[_R_CV + hh * DIFF_V_DIM:_R_CV + (hh + 1) * DIFF_V_DIM].astype(BF16))
        cvt_ref[hh * DIFF_VROWS + DIFF_V_DIM:(hh + 1) * DIFF_VROWS, :] = ones_row


def _inproj(x2, norm_w, w_main, w_t, conv_w, conv_b, tm, seq, diff_key_tile):
    t, d = x2.shape
    hb = tm // SUBLANES
    n_hblk = t // SUBLANES
    row = lambda i: (i, 0)
    col = lambda i: (0, i)
    full = lambda i: (0, 0)
    row_outs = [(A_KV_DIM, BF16), (SSD_D_INNER, F32), (SSD_D_INNER, F32), (SSD_BC_DIM, F32),
                (DIFF_HEADS * LANES, BF16), (LANES, F32)]
    out_shape = [jax.ShapeDtypeStruct((t, w), dt) for w, dt in row_outs]
    out_specs = [pl.BlockSpec((tm, w), row) for w, _ in row_outs]
    out_shape += [jax.ShapeDtypeStruct((A_Q_DIM, t), BF16),
                  jax.ShapeDtypeStruct((t // LANES, A_KV_DIM, LANES), BF16),
                  jax.ShapeDtypeStruct((DIFF_QK_WIDTH, t), BF16),
                  jax.ShapeDtypeStruct((DIFF_HEADS * DIFF_VROWS, t), BF16),
                  jax.ShapeDtypeStruct((SSD_BC_DIM, t), F32)]
    out_specs += [pl.BlockSpec((A_Q_DIM, tm), col),
                  pl.BlockSpec((tm // LANES, A_KV_DIM, LANES), lambda i: (i, 0, 0)),
                  pl.BlockSpec((DIFF_QK_WIDTH, tm), col),
                  pl.BlockSpec((DIFF_HEADS * DIFF_VROWS, tm), col),
                  pl.BlockSpec((SSD_BC_DIM, tm), col)]
    return pl.pallas_call(
        functools.partial(_inproj_kernel, tiles_per_seq=seq // tm, diff_key_tile=diff_key_tile),
        grid=(t // tm,),
        in_specs=[pl.BlockSpec((tm, d), row),
                  pl.BlockSpec((SUBLANES, d), lambda i: (jnp.maximum(i * hb - 1, 0), 0)),
                  pl.BlockSpec((SUBLANES, d), lambda i: (jnp.minimum((i + 1) * hb, n_hblk - 1), 0)),
                  pl.BlockSpec((1, d), full),
                  pl.BlockSpec(w_main.shape, full), pl.BlockSpec(w_t.shape, full),
                  pl.BlockSpec(conv_w.shape, full), pl.BlockSpec(conv_b.shape, full)],
        out_specs=out_specs,
        out_shape=out_shape,
        compiler_params=_cparams("parallel"),
        name="inproj",
    )(x2, x2, x2, norm_w, w_main, w_t, conv_w, conv_b)


def _swa_kernel(sink_ref, slope_ref, qt_ref, k_ref, vt_ref, o_ref):
    step = pl.program_id(1)
    s_len = k_ref.shape[0]
    blk = SWA_WINDOW
    band = 3 * blk
    nb = s_len // blk
    rep = SWA_HEADS // SWA_KV_HEADS
    hd = HEAD_DIM
    blocks_per_step = qt_ref.shape[1] // blk
    for u in range(blocks_per_step):
        n = step * blocks_per_step + u
        start_blk = jnp.clip(n - 1, 0, nb - 3)
        start = pl.multiple_of(start_blk * blk, blk)
        kb = k_ref[pl.ds(start, band), :]
        v3 = vt_ref[pl.ds(start_blk, 3)]
        vtb = jnp.concatenate([v3[0], v3[1], v3[2]], axis=1)
        qt = qt_ref[:, u * blk:(u + 1) * blk]
        zero = jnp.zeros((hd, rep * blk), BF16)
        grp = [jnp.concatenate([qt[(g * rep + r) * hd:(g * rep + r + 1) * hd] for r in range(rep)], axis=1)
               for g in range(SWA_KV_HEADS)]
        qbd = jnp.concatenate([jnp.concatenate([grp[0], zero], axis=1),
                               jnp.concatenate([zero, grp[1]], axis=1)], axis=0)
        st = jnp.dot(kb, qbd, preferred_element_type=F32)
        kpos = start + lax.broadcasted_iota(I32, (band, blk), 0)
        qpos = n * blk + lax.broadcasted_iota(I32, (band, blk), 1)
        dist_i = jnp.abs(qpos - kpos)
        valid = dist_i <= SWA_WINDOW
        dist = dist_i.astype(F32)
        ps, inv = [], []
        for h in range(SWA_HEADS):
            s = jnp.where(valid, st[:, h * blk:(h + 1) * blk] - slope_ref[h] * dist, NEG)
            sink = sink_ref[h]
            m = jnp.maximum(jnp.max(s, axis=0, keepdims=True), sink)
            p = jnp.exp2(s - m)
            inv.append(1.0 / (jnp.sum(p, axis=0, keepdims=True) + jnp.exp2(sink - m)))
            ps.append(p.astype(BF16))
        outs = []
        for g in range(SWA_KV_HEADS):
            pg = jnp.concatenate(ps[g * rep:(g + 1) * rep], axis=1)
            og = jnp.dot(vtb[g * hd:(g + 1) * hd, :], pg, preferred_element_type=F32)
            for r in range(rep):
                outs.append(og[:, r * blk:(r + 1) * blk] * inv[g * rep + r])
        o_ref[u * blk:(u + 1) * blk, :] = jnp.transpose(jnp.concatenate(outs, axis=0)).astype(o_ref.dtype)


def _swa(aqt, ak, avt3, sink, slopes, batch, seq):
    blk = SWA_WINDOW
    rows = min(blk * SWA_BLOCKS_PER_STEP, seq)
    steps = seq // rows
    nb = seq // blk
    t = batch * seq
    smem = pl.BlockSpec(memory_space=pltpu.SMEM)
    return pl.pallas_call(
        _swa_kernel,
        grid=(batch, steps),
        in_specs=[smem, smem,
                  pl.BlockSpec((A_Q_DIM, rows), lambda b, s: (0, b * steps + s)),
                  pl.BlockSpec((seq, A_KV_DIM), lambda b, s: (b, 0)),
                  pl.BlockSpec((nb, A_KV_DIM, blk), lambda b, s: (b, 0, 0))],
        out_specs=pl.BlockSpec((rows, A_Q_DIM), lambda b, s: (b * steps + s, 0)),
        out_shape=jax.ShapeDtypeStruct((t, A_Q_DIM), BF16),
        compiler_params=_cparams("parallel", "parallel"),
        name="swa",
    )(sink, slopes, aqt, ak, avt3)


def _diff_key_tile(i, j, tq, tk, nk):
    return ((i * tq) // tk + j) % nk


def _diff_key_features(pos_in_tile):
    lane = lax.broadcasted_iota(I32, pos_in_tile.shape, 1) % LANES - 2 * DIFF_QK_DIM
    coarse = ((pos_in_tile // 16) * 16).astype(F32)
    fine = (pos_in_tile % 16).astype(F32)
    f = lane % DIFF_NFEAT
    feat = jnp.where(f < 2, coarse, jnp.where(f < 4, fine, 1.0))
    return jnp.where((lane >= 0) & (lane < 2 * DIFF_NFEAT), feat, 0.0)


def _diff_kernel(slope_ref, qt_ref, k_ref, vt_ref, lam_ref, sw_ref, o_ref, qtb_ref, m_ref, acc_ref,
                 s_ref, p_ref, mx_ref, *, lambda_init, n_query_tiles, n_key_tiles):
    t = pl.program_id(1)
    n_pairs = n_query_tiles * n_key_tiles
    tq = qt_ref.shape[1]
    tk = k_ref.shape[0]
    dq = DIFF_QK_DIM
    hw = 2 * dq
    nf = DIFF_NFEAT
    nkt = n_key_tiles

    def reset_stats():
        m_ref[...] = jnp.full(m_ref.shape, NEG, F32)
        acc_ref[...] = jnp.zeros(acc_ref.shape, F32)

    def build_queries():
        ii = lax.broadcasted_iota(I32, (1, 2 * tq), 1)
        ii = jnp.where(ii >= tq, ii - tq, ii).astype(F32)
        qt = qt_ref[...]
        col = lax.broadcasted_iota(I32, (hw, 2 * tq), 1)
        row = lax.broadcasted_iota(I32, (hw, 2 * tq), 0)
        own_map = row // dq == col // tq
        for h in range(DIFF_HEADS):
            qh = qt[h * hw:(h + 1) * hw, :]
            qh2 = jnp.where(own_map, jnp.concatenate([qh, qh], axis=1), jnp.zeros((hw, 2 * tq), BF16))
            sl = jnp.full((1, 2 * tq), slope_ref[h], F32)
            s_hi, s_lo = _bf16_split(sl)
            v_hi, v_lo = _bf16_split(-sl * ii)
            rows = jnp.concatenate([s_hi, s_lo, s_hi, s_lo, v_hi, v_lo], axis=0)
            zrow = jnp.zeros((nf, 2 * tq), F32)
            zero = jnp.zeros((hw - 2 * nf, 2 * tq), F32)
            variants = ([rows, zrow], [zrow, -rows], [zrow, zrow])
            for v, pieces in enumerate(variants):
                qtb_ref[v, h, 0:hw, :] = qh2
                qtb_ref[v, h, hw:2 * hw, :] = jnp.concatenate(pieces + [zero], axis=0).astype(BF16)

    def pair_of(step):
        c = jnp.clip(step, 0, n_pairs - 1)
        return c // nkt, c % nkt

    def key_start(qi, step):
        return _diff_key_tile(qi, step, tq, tk, nkt) * tk

    def score_head(slot, h, variant, dist):
        s = jnp.dot(k_ref[:, h * LANES:(h + 1) * LANES], qtb_ref[variant, h], preferred_element_type=F32)
        if dist is not None:
            s = s - slope_ref[h] * dist
        s_ref[slot, h] = s
        mx_ref[slot, h:h + 1, :] = jnp.max(s, axis=0, keepdims=True)

    def softmax_head(slot, h, shift):
        m_old = m_ref[h:h + 1, :]
        m_new = jnp.maximum(m_old, mx_ref[slot, h:h + 1, :] + shift)
        p_ref[h] = jnp.exp2(s_ref[slot, h] - (m_new - shift)).astype(BF16)
        m_ref[h:h + 1, :] = m_new
        return jnp.exp2(m_old - m_new)

    def value_head(h, alpha):
        pv = jnp.dot(vt_ref[h * DIFF_VROWS:(h + 1) * DIFF_VROWS, :], p_ref[h], preferred_element_type=F32)
        acc_ref[h] = alpha * acc_ref[h] + pv

    def tile_shift(step):
        qi, js = pair_of(step)
        k0 = key_start(qi, js)
        q0 = qi * tq
        sign = jnp.where(js == 0, 0.0, jnp.where(k0 < q0, 1.0, -1.0))
        return sign * (k0 - q0).astype(F32)

    def diagonal_scores(slot):
        qi, _ = pair_of(t)
        kpos = key_start(qi, 0) + lax.broadcasted_iota(I32, (tk, 2 * tq), 0)
        qpos = qi * tq + lax.broadcasted_iota(I32, (tk, 2 * tq), 1) % tq
        dist = jnp.abs(qpos - kpos).astype(F32)
        for h in range(DIFF_HEADS):
            score_head(slot, h, 2, dist)

    def finish_query_tile():
        lp = lam_ref[...]
        lam = (jnp.exp(jnp.sum(lp[0:1] * lp[1:2], axis=-1, keepdims=True))
               - jnp.exp(jnp.sum(lp[2:3] * lp[3:4], axis=-1, keepdims=True)) + lambda_init)
        outs = []
        for h in range(DIFF_HEADS):
            a = acc_ref[h]
            o = a[0:DIFF_V_DIM] / a[DIFF_V_DIM:DIFF_V_DIM + 1]
            o = o[:, 0:tq] - lam * o[:, tq:2 * tq]
            ms = jnp.mean(o * o, axis=0, keepdims=True)
            outs.append(o * lax.rsqrt(ms + NORM_EPS) * sw_ref[...] * (1.0 - lambda_init))
        o_ref[...] = jnp.transpose(jnp.concatenate(outs, axis=0)).astype(o_ref.dtype)

    _, step_in_tile = pair_of(t)
    is_first = (t < n_pairs) & (step_in_tile == 0)

    @pl.when(t == 0)
    def _():
        reset_stats()
        build_queries()
        diagonal_scores(0)

    for parity in range(2):
        @pl.when((t > 0) & (t < n_pairs) & jnp.logical_not(is_first) & (t % 2 == parity))
        def _():
            qi, js = pair_of(t)
            variant = jnp.where(key_start(qi, js) < qi * tq, 0, 1)
            shift = tile_shift(t - 1)
            alphas = []
            for h in range(DIFF_HEADS):
                alphas.append(softmax_head(1 - parity, h, slope_ref[h] * shift))
                score_head(parity, h, variant, None)
                value_head(h, alphas[h])

        if nkt % 2 == 0 and parity == 1:
            continue

        @pl.when((t > 0) & is_first & (t % 2 == parity))
        def _():
            shift = tile_shift(t - 1)
            alphas = [softmax_head(1 - parity, h, slope_ref[h] * shift) for h in range(DIFF_HEADS)]
            build_queries()
            diagonal_scores(parity)
            for h in range(DIFF_HEADS):
                value_head(h, alphas[h])
            finish_query_tile()
            reset_stats()

    @pl.when(t == n_pairs)
    def _():
        shift = tile_shift(t - 1)
        alphas = [softmax_head((n_pairs - 1) % 2, h, slope_ref[h] * shift) for h in range(DIFF_HEADS)]
        for h in range(DIFF_HEADS):
            value_head(h, alphas[h])
        finish_query_tile()


def _diff(cqt, ck, cvt, slopes, lam_params, subln_w_col, lambda_init, batch, seq, tq, tk):
    assert tk % tq == 0 and seq % tk == 0
    nq, nk = seq // tq, seq // tk
    t = batch * seq
    smem = pl.BlockSpec(memory_space=pltpu.SMEM)
    n_pairs = nq * nk

    def query_tile(p):
        return jnp.clip(p, 0, n_pairs - 1) // nk

    def key_tile(p):
        c = jnp.clip(p, 0, n_pairs - 1)
        return _diff_key_tile(c // nk, c % nk, tq, tk, nk)

    return pl.pallas_call(
        functools.partial(_diff_kernel, lambda_init=lambda_init, n_query_tiles=nq, n_key_tiles=nk),
        grid=(batch, n_pairs + 1),
        in_specs=[smem,
                  pl.BlockSpec((DIFF_QK_WIDTH, tq), lambda b, p: (0, b * nq + query_tile(p))),
                  pl.BlockSpec((tk, DIFF_HEADS * LANES), lambda b, p: (b * nk + key_tile(p), 0)),
                  pl.BlockSpec((DIFF_HEADS * DIFF_VROWS, tk), lambda b, p: (0, b * nk + key_tile(p - 1))),
                  pl.BlockSpec(lam_params.shape, lambda b, p: (0, 0)),
                  pl.BlockSpec(subln_w_col.shape, lambda b, p: (0, 0))],
        out_specs=pl.BlockSpec((tq, DIFF_V_WIDTH), lambda b, p: (b * nq + query_tile(p - 1), 0)),
        out_shape=jax.ShapeDtypeStruct((t, DIFF_V_WIDTH), BF16),
        scratch_shapes=[pltpu.VMEM((3, DIFF_HEADS, LANES, 2 * tq), BF16),
                        pltpu.VMEM((DIFF_HEADS, 2 * tq), F32),
                        pltpu.VMEM((DIFF_HEADS, DIFF_VROWS, 2 * tq), F32),
                        pltpu.VMEM((2, DIFF_HEADS, tk, 2 * tq), F32),
                        pltpu.VMEM((DIFF_HEADS, tk, 2 * tq), BF16),
                        pltpu.VMEM((2, DIFF_HEADS, 2 * tq), F32)],
        compiler_params=_cparams("parallel", "arbitrary"),
        name="diffattn",
    )(slopes, cqt, ck, cvt, lam_params, subln_w_col)


def _ssd_direction(fwd, off, xs_ref, bt_ref, cm_ref, dt_ref, dtb_ref, alog_ref, dsk_ref, y_ref, state_ref):
    q = SSD_CHUNK
    rows = slice(off, off + q)
    lane0 = 0 if fwd else SSD_HEADS
    dt_all = _softplus(dt_ref[rows, :] + dtb_ref[...])
    dta_all = dt_all * -jnp.exp(alog_ref[...])
    row = lax.broadcasted_iota(I32, (q, q), 0)
    col = lax.broadcasted_iota(I32, (q, q), 1)
    keep = (row >= col) if fwd else (row <= col)
    tri = keep.astype(BF16)
    part_hi = dta_all.astype(BF16)
    rest = dta_all - part_hi.astype(F32)
    part_mid = rest.astype(BF16)
    part_lo = (rest - part_mid.astype(F32)).astype(BF16)
    da_all = (jnp.dot(tri, part_hi, preferred_element_type=F32)
              + jnp.dot(tri, part_mid, preferred_element_type=F32)
              + jnp.dot(tri, part_lo, preferred_element_type=F32))
    da_all_t = jnp.transpose(da_all)
    dt_all_t = jnp.transpose(dt_all)
    tot_all = jnp.sum(dta_all, axis=0, keepdims=True)

    xs = xs_ref[rows, :]
    bmt = bt_ref[:, rows]
    cm = cm_ref[rows, :]
    rep = SSD_HEADS // SSD_GROUPS
    ns = SSD_STATE
    lane = lax.broadcasted_iota(I32, (q, LANES), 1)
    low = lane < SSD_HEAD_DIM
    low_n = lax.broadcasted_iota(I32, (ns, LANES), 1) < SSD_HEAD_DIM
    cm_g = [jnp.where((lane // ns) == g, cm, 0.0) for g in range(SSD_GROUPS)]
    bmt_b = bmt.astype(BF16)
    g_mats = [jnp.dot(cm_g[g].astype(BF16), bmt_b, preferred_element_type=F32)
              for g in range(SSD_GROUPS)]
    zeros_n = jnp.zeros((ns, LANES), BF16)
    ys = []
    for pair in range(SSD_HEADS // 2):
        x_pair = xs[:, pair * LANES:(pair + 1) * LANES]
        x_pair_b = x_pair.astype(BF16)
        st = state_ref[pair]
        st_b = st.astype(BF16)
        y_heads, s_heads, keep_heads = [], [], []
        for h in (2 * pair, 2 * pair + 1):
            g = h // rep
            ln = lane0 + h
            dac = da_all[:, ln:ln + 1]
            dar = da_all_t[ln:ln + 1, :]
            dtr = dt_all_t[ln:ln + 1, :]
            tot = tot_all[:, ln:ln + 1]
            dac_b = jnp.broadcast_to(dac, (q, q))
            decay = jnp.exp(jnp.where(keep, dac_b - dar, NEG))
            y = jnp.dot((g_mats[g] * decay * dtr).astype(BF16), x_pair_b, preferred_element_type=F32)
            st_ext = jnp.concatenate([st_b, zeros_n] if g == 0 else [zeros_n, st_b], axis=0)
            c_in = (cm_g[g] * jnp.exp(dac_b)).astype(BF16)
            y_heads.append(y + jnp.dot(c_in, st_ext, preferred_element_type=F32))
            to_end = jnp.exp(tot - dar) * dtr
            b_out = (bmt[g * ns:(g + 1) * ns, :] * to_end).astype(BF16)
            s_heads.append(jnp.dot(b_out, x_pair_b, preferred_element_type=F32))
            keep_heads.append(jnp.exp(tot))
        state_keep = jnp.where(low_n, jnp.broadcast_to(keep_heads[0], (ns, LANES)),
                               jnp.broadcast_to(keep_heads[1], (ns, LANES)))
        state_ref[pair] = st * state_keep + jnp.where(low_n, s_heads[0], s_heads[1])
        y = jnp.where(low, y_heads[0], y_heads[1])
        if fwd:
            y = y + dsk_ref[:, pair * LANES:(pair + 1) * LANES] * x_pair
        ys.append(y)
    y_ref[rows, :] = jnp.concatenate(ys, axis=-1)


def _ssd_kernel(xsf_ref, btf_ref, cmf_ref, dtf_ref, xsb_ref, btb_ref, cmb_ref, dtb_in_ref,
                dtbias_ref, alog_ref, dsk_ref, yf_ref, yb_ref, state_ref):
    @pl.when(pl.program_id(1) == 0)
    def _():
        state_ref[...] = jnp.zeros(state_ref.shape, F32)

    n = SSD_CHUNKS_PER_STEP
    for u in range(n):
        _ssd_direction(True, u * SSD_CHUNK, xsf_ref, btf_ref, cmf_ref, dtf_ref, dtbias_ref, alog_ref, dsk_ref,
                       yf_ref, state_ref.at[0])
        _ssd_direction(False, (n - 1 - u) * SSD_CHUNK, xsb_ref, btb_ref, cmb_ref, dtb_in_ref, dtbias_ref,
                       alog_ref, dsk_ref, yb_ref, state_ref.at[1])


def _ssd(xs, bt, cm, dt_raw, dt_bias, a_log, d_skip, batch, seq):
    q = SSD_CHUNK * SSD_CHUNKS_PER_STEP
    nc = seq // q
    t = batch * seq
    full = lambda b, c: (0, 0)
    fw = lambda b, c: b * nc + c
    bw = lambda b, c: b * nc + nc - 1 - c

    def specs(idx):
        return [pl.BlockSpec((q, SSD_D_INNER), lambda b, c: (idx(b, c), 0)),
                pl.BlockSpec((SSD_BC_DIM, q), lambda b, c: (0, idx(b, c))),
                pl.BlockSpec((q, SSD_BC_DIM), lambda b, c: (idx(b, c), 0)),
                pl.BlockSpec((q, LANES), lambda b, c: (idx(b, c), 0))]

    return pl.pallas_call(
        _ssd_kernel,
        grid=(batch, nc),
        in_specs=specs(fw) + specs(bw) + [pl.BlockSpec(dt_bias.shape, full), pl.BlockSpec(a_log.shape, full),
                                          pl.BlockSpec(d_skip.shape, full)],
        out_specs=[pl.BlockSpec((q, SSD_D_INNER), lambda b, c: (fw(b, c), 0)),
                   pl.BlockSpec((q, SSD_D_INNER), lambda b, c: (bw(b, c), 0))],
        out_shape=[jax.ShapeDtypeStruct((t, SSD_D_INNER), F32), jax.ShapeDtypeStruct((t, SSD_D_INNER), F32)],
        scratch_shapes=[pltpu.VMEM((2, SSD_HEADS // 2, SSD_STATE, 2 * SSD_HEAD_DIM), F32)],
        compiler_params=_cparams("parallel", "arbitrary"),
        name="ssd",
    )(xs, bt, cm, dt_raw, xs, bt, cm, dt_raw, dt_bias, a_log, d_skip)


def _outproj_kernel(x_ref, ya_ref, yf_ref, yb_ref, z_ref, snw_ref, yc_ref, wo_ref, fnw_ref, wrt_ref, br_ref,
                    xn_ref, ri_ref, rf_ref, cnt_ref, tri_ref, carry_ref):
    step = pl.program_id(0)
    tm = x_ref.shape[0]

    @pl.when(step == 0)
    def _():
        carry_ref[...] = jnp.zeros(carry_ref.shape, F32)
        r = lax.broadcasted_iota(I32, (tm, tm), 0)
        cc = lax.broadcasted_iota(I32, (tm, tm), 1)
        tri_ref[...] = (r <= cc).astype(BF16)

    y = (yf_ref[...] + yb_ref[...]) * _silu(z_ref[...])
    yb = _rms(y, snw_ref[...]).astype(BF16)
    acc = jnp.dot(ya_ref[...], wo_ref[0:A_Q_DIM, :], preferred_element_type=F32)
    acc = acc + jnp.dot(yb, wo_ref[A_Q_DIM:A_Q_DIM + SSD_D_INNER, :], preferred_element_type=F32)
    acc = acc + jnp.dot(yc_ref[...], wo_ref[A_Q_DIM + SSD_D_INNER:, :], preferred_element_type=F32)
    xn = x_ref[...] + acc
    xn_ref[...] = xn

    h = _rms(xn, fnw_ref[...])
    h_hi = h.astype(BF16)
    h_lo = (h - h_hi.astype(F32)).astype(BF16)
    nt = (((1,), (1,)), ((), ()))
    logits = (lax.dot_general(wrt_ref[0], h_hi, nt, preferred_element_type=F32)
              + lax.dot_general(wrt_ref[0], h_lo, nt, preferred_element_type=F32)
              + lax.dot_general(wrt_ref[1], h_hi, nt, preferred_element_type=F32)) + br_ref[...]
    ne, epg, ng = N_EXPERTS, EXPERTS_PER_GROUP, N_EXPERT_GROUPS
    gl = logits[ne:ne + ng, :]
    gmax = jnp.max(gl, axis=0, keepdims=True)
    g_sel = jnp.full((1, tm), float(ng - 1), F32)
    for g in range(ng - 2, -1, -1):
        g_sel = jnp.where(gl[g:g + 1, :] == gmax, float(g), g_sel)
    g_gate = 1.0 / jnp.sum(jnp.exp(gl - gmax), axis=0, keepdims=True)
    e_in = logits[0:epg, :]
    for g in range(1, ng):
        e_in = jnp.where(g_sel == float(g), logits[g * epg:(g + 1) * epg, :], e_in)
    sub = lax.broadcasted_iota(I32, (epg, tm), 0).astype(F32)
    m1 = jnp.max(e_in, axis=0, keepdims=True)
    i1 = jnp.min(jnp.where(e_in == m1, sub, float(epg)), axis=0, keepdims=True)
    rest = jnp.where(sub == i1, NEG, e_in)
    m2 = jnp.max(rest, axis=0, keepdims=True)
    i2 = jnp.min(jnp.where(rest == m2, sub, float(epg)), axis=0, keepdims=True)
    r = jnp.exp(m2 - m1)
    c1 = g_gate / (1.0 + r)
    c2 = g_gate * r / (1.0 + r)
    e1 = (g_sel * epg + i1).astype(I32)
    e2 = (g_sel * epg + i2).astype(I32)

    erow = lax.broadcasted_iota(I32, (ne, tm), 0)
    hit1 = erow == e1
    hit2 = erow == e2
    oh = jnp.where(hit1 | hit2, 1.0, 0.0)
    incl = jnp.dot(oh.astype(BF16), tri_ref[...], preferred_element_type=F32)
    before = incl - oh + carry_ref[:, 0:1]
    rank1 = jnp.sum(jnp.where(hit1, before, 0.0), axis=0, keepdims=True)
    rank2 = jnp.sum(jnp.where(hit2, before, 0.0), axis=0, keepdims=True)
    carry_ref[...] = carry_ref[...] + jnp.sum(oh, axis=1, keepdims=True)
    cnt_ref[...] = carry_ref[...]
    zi = jnp.zeros((1, tm), I32)
    ri_ref[...] = jnp.concatenate([e1, e2, rank1.astype(I32), rank2.astype(I32), zi, zi, zi, zi], axis=0)
    zf = jnp.zeros((1, tm), F32)
    rf_ref[...] = jnp.concatenate([c1, c2, zf, zf, zf, zf, zf, zf], axis=0)


def _outproj(x2, ya, yf, yb, z, ssd_norm_w, yc, w_out, ffn_norm_w, wrt, br, tm):
    t, d = x2.shape
    row = lambda i: (i, 0)
    full = lambda i: (0, 0)
    return pl.pallas_call(
        _outproj_kernel,
        grid=(t // tm,),
        in_specs=[pl.BlockSpec((tm, d), row),
                  pl.BlockSpec((tm, A_Q_DIM), row),
                  pl.BlockSpec((tm, SSD_D_INNER), row),
                  pl.BlockSpec((tm, SSD_D_INNER), row),
                  pl.BlockSpec((tm, SSD_D_INNER), row),
                  pl.BlockSpec(ssd_norm_w.shape, full),
                  pl.BlockSpec((tm, DIFF_V_WIDTH), row),
                  pl.BlockSpec(w_out.shape, full),
                  pl.BlockSpec(ffn_norm_w.shape, full),
                  pl.BlockSpec(wrt.shape, lambda i: (0, 0, 0)),
                  pl.BlockSpec(br.shape, full)],
        out_specs=[pl.BlockSpec((tm, d), row),
                   pl.BlockSpec((SUBLANES, tm), lambda i: (0, i)),
                   pl.BlockSpec((SUBLANES, tm), lambda i: (0, i)),
                   pl.BlockSpec((N_EXPERTS, LANES), full)],
        out_shape=[jax.ShapeDtypeStruct((t, d), F32),
                   jax.ShapeDtypeStruct((SUBLANES, t), I32),
                   jax.ShapeDtypeStruct((SUBLANES, t), F32),
                   jax.ShapeDtypeStruct((N_EXPERTS, LANES), F32)],
        scratch_shapes=[pltpu.VMEM((tm, tm), BF16), pltpu.VMEM((N_EXPERTS, LANES), F32)],
        compiler_params=_cparams("arbitrary"),
        name="outproj_router",
    )(x2, ya, yf, yb, z, ssd_norm_w, yc, w_out, ffn_norm_w, wrt, br)


_PAD_PIECES = tuple(1 << b for b in reversed(range(MOE_ROW_TILE.bit_length() - 1)))


def _dispatch_kernel(slot1_ref, slot2_ref, pstart_ref, plen_ref, nused_ref, x_ref, xs_hbm, zero_ref, xbuf, sem):
    i = pl.program_id(0)
    tm = x_ref.shape[0]

    @pl.when(i == 0)
    def _():
        zero_ref[...] = jnp.zeros(zero_ref.shape, F32)

        def pieces(e, wait):
            n = plen_ref[e]
            first = pstart_ref[e]
            off = first + n
            for b in _PAD_PIECES:
                off = off - (n & b)
                dst = pl.ds(pl.multiple_of(off, b), b) if b >= SUBLANES else None
                if dst is not None:
                    @pl.when((n & b) != 0)
                    def _():
                        cp = pltpu.make_async_copy(zero_ref.at[pl.ds(0, b)], xs_hbm.at[dst], sem.at[2])
                        cp.wait() if wait else cp.start()

            for u in range(SUBLANES - 1):
                @pl.when(u < (n & (SUBLANES - 1)))
                def _():
                    cp = pltpu.make_async_copy(zero_ref.at[pl.ds(0, 1)], xs_hbm.at[pl.ds(first + u, 1)],
                                               sem.at[2])
                    cp.wait() if wait else cp.start()

        def tail(tile, wait):
            big = _PAD_PIECES[0]
            for part in range(MOE_ROW_TILE // big):
                dst = xs_hbm.at[pl.ds(pl.multiple_of(tile * MOE_ROW_TILE + part * big, big), big)]
                cp = pltpu.make_async_copy(zero_ref, dst, sem.at[2])
                cp.wait() if wait else cp.start()

        def loop(fn, lo, hi, wait):
            def body(k, carry):
                fn(k, wait)
                return carry

            lax.fori_loop(lo, hi, body, 0)

        n_tiles = xs_hbm.shape[0] // MOE_ROW_TILE
        for wait in (False, True):
            loop(pieces, 0, N_EXPERTS, wait)
            loop(tail, nused_ref[0], n_tiles, wait)

    base = i * tm
    last = pl.num_programs(0) - 1

    def drain(slot):
        for _ in range(2):
            pltpu.make_async_copy(xbuf.at[slot], xs_hbm.at[pl.ds(0, tm)], sem.at[slot]).wait()

    for parity in range(2):
        @pl.when(i % 2 == parity)
        def _():
            buf = xbuf.at[parity]
            buf[...] = x_ref[...]
            for r in range(tm):
                src = buf.at[pl.ds(r, 1)]
                pltpu.make_async_copy(src, xs_hbm.at[pl.ds(slot1_ref[base + r], 1)],
                                      sem.at[parity]).start(priority=0)
                pltpu.make_async_copy(src, xs_hbm.at[pl.ds(slot2_ref[base + r], 1)],
                                      sem.at[parity]).start(priority=1)

            @pl.when(i > 0)
            def _():
                drain(1 - parity)

            @pl.when(i == last)
            def _():
                drain(parity)


def _dispatch(xn, slot1, slot2, pad_start, pad_len, n_used, n_rows, tm):
    t, d = xn.shape
    grid_spec = pltpu.PrefetchScalarGridSpec(
        num_scalar_prefetch=5,
        grid=(t // tm,),
        in_specs=[pl.BlockSpec((tm, d), lambda i, s1, s2, ps, pn, nu: (i, 0))],
        out_specs=pl.BlockSpec(memory_space=pl.ANY),
        scratch_shapes=[pltpu.VMEM((_PAD_PIECES[0], d), F32), pltpu.VMEM((2, tm, d), F32),
                        pltpu.SemaphoreType.DMA((3,))],
    )
    return pl.pallas_call(
        _dispatch_kernel,
        grid_spec=grid_spec,
        out_shape=jax.ShapeDtypeStruct((n_rows, d), F32),
        compiler_params=_cparams("arbitrary"),
        name="moe_dispatch",
    )(slot1, slot2, pad_start, pad_len, n_used, xn)


def _moe_kernel(texp_ref, nused_ref, first_ref, wslot_ref, next_ref, x_ref, fnw_ref, wg_hbm, wu_hbm, wd_hbm,
                y_ref, wg_buf, wu_buf, wd_buf, sem, *, layer):
    i = pl.program_id(0)

    def weight_copies(expert, slot):
        return [pltpu.make_async_copy(hbm.at[layer, expert], buf.at[slot], sem.at[slot])
                for hbm, buf in ((wg_hbm, wg_buf), (wu_hbm, wu_buf), (wd_hbm, wd_buf))]

    @pl.when(i < nused_ref[0])
    def _():
        slot = wslot_ref[i]

        @pl.when(i == 0)
        def _():
            for cp in weight_copies(texp_ref[0], 0):
                cp.start()

        @pl.when(first_ref[i] == 1)
        def _():
            for cp in weight_copies(texp_ref[i], slot):
                cp.wait()

            @pl.when(next_ref[i] >= 0)
            def _():
                for cp in weight_copies(next_ref[i], 1 - slot):
                    cp.start()

        h = _rms(x_ref[...], fnw_ref[...]).astype(BF16)
        hg = jnp.dot(h, wg_buf[slot].astype(BF16), preferred_element_type=F32)
        hu = jnp.dot(h, wu_buf[slot].astype(BF16), preferred_element_type=F32)
        act = (_silu(hg) * hu).astype(BF16)
        y_ref[...] = jnp.dot(act, wd_buf[slot].astype(BF16), preferred_element_type=F32)

    @pl.when(i >= nused_ref[0])
    def _():
        y_ref[...] = jnp.zeros(y_ref.shape, F32)


def _moe(xs, ffn_norm_w, w_gate, w_up, w_down, layer, tile_expert, n_used):
    n_rows, d = xs.shape
    f = w_gate.shape[-1]
    tr = MOE_ROW_TILE

    n_tiles = n_rows // tr

    idx = jnp.arange(n_tiles, dtype=I32)
    prev_expert = jnp.concatenate([jnp.full((1,), -1, I32), tile_expert[:-1]])
    first = ((idx < n_used[0]) & (tile_expert != prev_expert)).astype(I32)
    wslot = ((jnp.cumsum(first) - 1) % 2).astype(I32)
    first_pos = jnp.where(first == 1, idx, n_tiles)
    next_first = jnp.concatenate([lax.cummin(first_pos, reverse=True)[1:], jnp.full((1,), n_tiles, I32)])
    next_expert = jnp.where(next_first < n_tiles, tile_expert[jnp.minimum(next_first, n_tiles - 1)], -1).astype(I32)

    def used(i, nu):
        return jnp.maximum(jnp.minimum(i, nu[0] - 1), 0)

    anyspace = pl.BlockSpec(memory_space=pl.ANY)
    grid_spec = pltpu.PrefetchScalarGridSpec(
        num_scalar_prefetch=5,
        grid=(n_tiles,),
        in_specs=[pl.BlockSpec((tr, d), lambda i, te, nu, fi, ws, nx: (used(i, nu), 0)),
                  pl.BlockSpec(ffn_norm_w.shape, lambda i, te, nu, fi, ws, nx: (0, 0)),
                  anyspace, anyspace, anyspace],
        out_specs=pl.BlockSpec((tr, d), lambda i, te, nu, fi, ws, nx: (i, 0)),
        scratch_shapes=[pltpu.VMEM((2, d, f), F32), pltpu.VMEM((2, d, f), F32), pltpu.VMEM((2, f, d), F32),
                        pltpu.SemaphoreType.DMA((2,))],
    )
    return pl.pallas_call(
        functools.partial(_moe_kernel, layer=layer),
        grid_spec=grid_spec,
        out_shape=jax.ShapeDtypeStruct((n_rows, d), F32),
        compiler_params=_cparams("arbitrary"),
        name="moe_experts",
    )(tile_expert, n_used, first, wslot, next_expert, xs, ffn_norm_w, w_gate, w_up, w_down)


def _combine_kernel(slot1_ref, slot2_ref, x_ref, cw_ref, nw_ref, y_hbm, o_ref, ybuf, sem, *, final_norm):
    i = pl.program_id(0)
    n = pl.num_programs(0)
    tm = x_ref.shape[0]

    def start_gather(tile, slot):
        base = tile * tm
        for r in range(tm):
            pltpu.make_async_copy(y_hbm.at[pl.ds(slot1_ref[base + r], 1)], ybuf.at[slot, 0, pl.ds(r, 1)],
                                  sem.at[slot]).start(priority=0)
            pltpu.make_async_copy(y_hbm.at[pl.ds(slot2_ref[base + r], 1)], ybuf.at[slot, 1, pl.ds(r, 1)],
                                  sem.at[slot]).start(priority=1)

    def compute(slot):
        for k in range(2):
            pltpu.make_async_copy(y_hbm.at[pl.ds(0, tm)], ybuf.at[slot, k], sem.at[slot]).wait()
        cw = cw_ref[...]
        out = x_ref[...] + cw[:, 0:1] * ybuf[slot, 0] + cw[:, 1:2] * ybuf[slot, 1]
        if final_norm:
            out = _rms(out, nw_ref[...])
        o_ref[...] = out

    @pl.when(i == 0)
    def _():
        start_gather(0, 0)

    for parity in range(2):
        @pl.when(i % 2 == parity)
        def _():
            @pl.when(i + 1 < n)
            def _():
                start_gather(i + 1, 1 - parity)

            compute(parity)


def _combine(xn, cw, norm_w, y_sorted, slot1, slot2, tm, final_norm):
    t, d = xn.shape
    grid_spec = pltpu.PrefetchScalarGridSpec(
        num_scalar_prefetch=2,
        grid=(t // tm,),
        in_specs=[pl.BlockSpec((tm, d), lambda i, s1, s2: (i, 0)),
                  pl.BlockSpec((tm, cw.shape[1]), lambda i, s1, s2: (i, 0)),
                  pl.BlockSpec(norm_w.shape, lambda i, s1, s2: (0, 0)),
                  pl.BlockSpec(memory_space=pl.ANY)],
        out_specs=pl.BlockSpec((tm, d), lambda i, s1, s2: (i, 0)),
        scratch_shapes=[pltpu.VMEM((2, 2, tm, d), F32), pltpu.SemaphoreType.DMA((2,))],
    )
    return pl.pallas_call(
        functools.partial(_combine_kernel, final_norm=final_norm),
        grid_spec=grid_spec,
        out_shape=jax.ShapeDtypeStruct((t, d), F32),
        compiler_params=_cparams("arbitrary"),
        name="moe_combine",
    )(slot1, slot2, xn, cw, norm_w, y_sorted)


def _pad_lanes(v):
    v = v.reshape(1, -1).astype(F32)
    return jnp.pad(v, ((0, 0), (0, LANES - v.shape[1])))


def kernel(x, attn_norm_w, w_in, swa_sink, ssd_conv_w, ssd_conv_b, ssd_dt_bias, ssd_a_log, ssd_d, ssd_norm_w,
           diff_lambda, diff_subln_w, w_out, ffn_norm_w, w_router_group, b_router_group, w_router_expert,
           b_router_expert, w_gate, w_up, w_down, final_norm_w):
    return _forward(x, attn_norm_w, w_in, swa_sink, ssd_conv_w, ssd_conv_b, ssd_dt_bias, ssd_a_log, ssd_d,
                    ssd_norm_w, diff_lambda, diff_subln_w, w_out, ffn_norm_w, w_router_group, b_router_group,
                    w_router_expert, b_router_expert, w_gate, w_up, w_down, final_norm_w)


def _forward(x, attn_norm_w, w_in, swa_sink, ssd_conv_w, ssd_conv_b, ssd_dt_bias, ssd_a_log, ssd_d, ssd_norm_w,
             diff_lambda, diff_subln_w, w_out, ffn_norm_w, w_router_group, b_router_group, w_router_expert,
             b_router_expert, w_gate, w_up, w_down, final_norm_w, tm=1024, tq=512, tk=1024, tmc=512):
    batch, seq, d = x.shape
    depth = w_in.shape[0]
    t = batch * seq
    tr = MOE_ROW_TILE
    n_tiles = (2 * t) // tr + N_EXPERTS
    slopes = jnp.exp2(-8.0 * jnp.arange(1, N_ALIBI_HEADS + 1, dtype=F32) / N_ALIBI_HEADS)
    swa_slopes, diff_slopes = slopes[:SWA_HEADS], slopes[SWA_HEADS:]

    sizes = [A_Q_DIM, A_KV_DIM, A_KV_DIM, SSD_D_INNER, SSD_CONV_DIM, SSD_DT_DIM, DIFF_QK_WIDTH, DIFF_QK_WIDTH,
             DIFF_V_WIDTH]
    offs = [0]
    for s in sizes:
        offs.append(offs[-1] + s)
    o_aq, o_ak, o_av, o_z, o_xbc, o_dt, o_cq, o_ck, o_cv, o_end = offs

    x2 = x.reshape(t, d)
    for l in range(depth):
        w = w_in[l]
        hw = 2 * DIFF_QK_DIM
        w_ck = jnp.pad(w[:, o_ck:o_cv].reshape(d, DIFF_HEADS, hw), ((0, 0), (0, 0), (0, LANES - hw)))
        w_main = jnp.concatenate(
            [w[:, o_ak:o_av], w[:, o_z:o_dt], w_ck.reshape(d, DIFF_HEADS * LANES), w[:, o_dt:o_cq],
             jnp.zeros((d, LANES - SSD_DT_DIM), w.dtype)], axis=1).astype(BF16)
        w_t = jnp.concatenate([w[:, o_aq:o_ak], w[:, o_av:o_z], w[:, o_cq:o_ck], w[:, o_cv:o_end]],
                              axis=1).T.astype(BF16)
        ak, z, xs, cm, ck, dt_raw, aqt, avt3, cqt, cvt, bt = _inproj(
            x2, attn_norm_w[l].reshape(1, d), w_main, w_t, ssd_conv_w[l].astype(F32),
            ssd_conv_b[l].reshape(1, -1).astype(F32), tm, seq, tk)

        ya = _swa(aqt, ak, avt3, swa_sink[l].astype(F32) * LOG2E, swa_slopes * LOG2E, batch, seq)
        lambda_init = 0.8 - 0.6 * math.exp(-0.3 * l)
        yc = _diff(cqt, ck, cvt, diff_slopes * LOG2E, diff_lambda[l].astype(F32),
                   diff_subln_w[l].reshape(DIFF_V_DIM, 1).astype(F32), lambda_init, batch, seq, tq, tk)
        yf, yb = _ssd(xs, bt, cm, dt_raw, _pad_lanes(ssd_dt_bias[l]), _pad_lanes(ssd_a_log[l]),
                      jnp.repeat(ssd_d[l].astype(F32), SSD_HEAD_DIM).reshape(1, SSD_D_INNER), batch, seq)

        wr32 = jnp.concatenate([w_router_expert[l], w_router_group[l],
                                jnp.zeros((d, SUBLANES - N_EXPERT_GROUPS), F32)], axis=1).T.astype(F32)
        wr_hi = wr32.astype(BF16)
        wrt = jnp.stack([wr_hi, (wr32 - wr_hi.astype(F32)).astype(BF16)])
        br = jnp.concatenate([b_router_expert[l], b_router_group[l],
                              jnp.zeros((SUBLANES - N_EXPERT_GROUPS,), F32)]).reshape(-1, 1).astype(F32)
        xn, ri, rf, cnt = _outproj(x2, ya, yf, yb, z, ssd_norm_w[l].reshape(1, -1), yc, w_out[l].astype(BF16),
                                   ffn_norm_w[l].reshape(1, d), wrt, br, tm)

        counts = cnt[:, 0].astype(I32)
        padded = ((counts + tr - 1) // tr) * tr
        ends = jnp.cumsum(padded)
        starts = ends - padded
        experts = jnp.arange(N_EXPERTS, dtype=I32)[:, None]

        def slot_of(e, rank):
            return jnp.sum(jnp.where(e[None, :] == experts, starts[:, None], 0), axis=0) + rank

        slot1 = slot_of(ri[0], ri[2])
        slot2 = slot_of(ri[1], ri[3])
        tile_start = jnp.arange(n_tiles, dtype=I32) * tr
        tile_expert = jnp.minimum(jnp.sum(ends[None, :] <= tile_start[:, None], axis=1), N_EXPERTS - 1).astype(I32)
        n_used = (ends[-1] // tr).astype(I32).reshape(1)

        xs_sorted = _dispatch(xn, slot1, slot2, starts + counts, padded - counts, n_used, n_tiles * tr, tmc)
        y_sorted = _moe(xs_sorted, ffn_norm_w[l].reshape(1, d), w_gate, w_up, w_down, l, tile_expert, n_used)
        last = l == depth - 1
        x2 = _combine(xn, rf.T, final_norm_w.reshape(1, d), y_sorted, slot1, slot2, tmc, last)
    return x2.reshape(batch, seq, d)
```

```python
import functools
import math

import jax
import jax.numpy as jnp
from jax import lax
from jax.experimental import pallas as pl
from jax.experimental.pallas import tpu as pltpu

F32 = jnp.float32
BF16 = jnp.bfloat16
I32 = jnp.int32

HEAD_DIM = 64
SWA_HEADS = 6
SWA_KV_HEADS = 2
SWA_WINDOW = 128
SSD_HEADS = 6
SSD_HEAD_DIM = 64
SSD_GROUPS = 2
SSD_STATE = 64
SSD_CONV = 5
DIFF_HEADS = 4
DIFF_QK_DIM = 32
DIFF_V_DIM = 64
N_EXPERT_GROUPS = 4
EXPERTS_PER_GROUP = 8
N_EXPERTS = N_EXPERT_GROUPS * EXPERTS_PER_GROUP
NORM_EPS = 1e-6

A_Q_DIM = SWA_HEADS * HEAD_DIM
A_KV_DIM = SWA_KV_HEADS * HEAD_DIM
SSD_D_INNER = SSD_HEADS * SSD_HEAD_DIM
SSD_BC_DIM = SSD_GROUPS * SSD_STATE
SSD_CONV_DIM = SSD_D_INNER + 2 * SSD_BC_DIM
SSD_DT_DIM = 2 * SSD_HEADS
DIFF_QK_WIDTH = DIFF_HEADS * 2 * DIFF_QK_DIM
DIFF_V_WIDTH = DIFF_HEADS * DIFF_V_DIM
N_ALIBI_HEADS = SWA_HEADS + DIFF_HEADS

LANES = 128
SUBLANES = 8
VMEM_LIMIT = 56 * 1024 * 1024
NEG = -1e30
LOG2E = math.log2(math.e)

SSD_CHUNK = 128
SSD_CHUNKS_PER_STEP = 8
SWA_BLOCKS_PER_STEP = 16
MOE_ROW_TILE = 256
DIFF_PAIR = 2 * 2 * DIFF_QK_DIM
DIFF_VROWS = 80
DIFF_NFEAT = 6


def _cparams(*sem):
    return pltpu.CompilerParams(dimension_semantics=sem, vmem_limit_bytes=VMEM_LIMIT)


def _rms(x, w):
    return x * lax.rsqrt(jnp.mean(x * x, axis=-1, keepdims=True) + NORM_EPS) * w


def _silu(x):
    return x / (1.0 + jnp.exp(-x))


def _softplus(x):
    return jnp.maximum(x, 0.0) + jnp.log(1.0 + jnp.exp(-jnp.abs(x)))


def _bf16_split(x):
    hi = x.astype(BF16).astype(F32)
    lo = (x - hi).astype(BF16).astype(F32)
    return hi, lo


_C_AK = 0
_C_Z = _C_AK + A_KV_DIM
_C_XBC = _C_Z + SSD_D_INNER
_C_CK = _C_XBC + SSD_CONV_DIM
_C_DT = _C_CK + DIFF_HEADS * LANES
_C_END = _C_DT + LANES
_R_AQ = 0
_R_AV = _R_AQ + A_Q_DIM
_R_CQ = _R_AV + A_KV_DIM
_R_CV = _R_CQ + DIFF_QK_WIDTH
_R_END = _R_CV + DIFF_V_WIDTH


def _inproj_kernel(x_ref, xp_ref, xn_ref, nw_ref, w_ref, wt_ref, cw_ref, cb_ref,
                   ak_ref, z_ref, xs_ref, cm_ref, ck_ref, dt_ref, aqt_ref, avt_ref, cqt_ref, cvt_ref, bt_ref,
                   *, tiles_per_seq, diff_key_tile):
    i = pl.program_id(0)
    tm = x_ref.shape[0]
    nw = nw_ref[...]
    h = _rms(x_ref[...], nw).astype(BF16)

    def seg(lo, hi):
        return jnp.dot(h, w_ref[:, lo:hi], preferred_element_type=F32)

    ak_ref[...] = seg(_C_AK, _C_Z).astype(BF16)
    z_ref[...] = seg(_C_Z, _C_XBC)
    pos = (i * tm + lax.broadcasted_iota(I32, (tm, _C_DT - _C_CK), 0)) % diff_key_tile
    ck_ref[...] = (seg(_C_CK, _C_DT) + _diff_key_features(pos)).astype(BF16)
    dt_ref[...] = seg(_C_DT, _C_END)

    w_xbc = w_ref[:, _C_XBC:_C_CK]
    first = i % tiles_per_seq == 0
    last = i % tiles_per_seq == tiles_per_seq - 1
    prev = jnp.dot(_rms(xp_ref[...], nw).astype(BF16), w_xbc, preferred_element_type=F32)
    nxt = jnp.dot(_rms(xn_ref[...], nw).astype(BF16), w_xbc, preferred_element_type=F32)
    prev = jnp.where(first, 0.0, prev)
    nxt = jnp.where(last, 0.0, nxt)
    ext = jnp.concatenate([prev, seg(_C_XBC, _C_CK), nxt], axis=0)
    half = SSD_CONV // 2
    conv = cb_ref[...]
    for k in range(SSD_CONV):
        off = SUBLANES - half + k
        conv = conv + cw_ref[k:k + 1, :] * ext[off:off + tm, :]
    u = _silu(conv)
    xs_ref[...] = u[:, :SSD_D_INNER]
    bt_ref[...] = jnp.transpose(u[:, SSD_D_INNER:SSD_D_INNER + SSD_BC_DIM])
    cm_ref[...] = u[:, SSD_D_INNER + SSD_BC_DIM:]

    tr = lax.dot_general(wt_ref[...], h, (((1,), (1,)), ((), ())), preferred_element_type=F32)
    aqt_ref[...] = (tr[_R_AQ:_R_AV] * (HEAD_DIM ** -0.5 * LOG2E)).astype(BF16)
    avt = tr[_R_AV:_R_CQ].astype(BF16)
    for c in range(tm // LANES):
        avt_ref[c] = avt[:, c * LANES:(c + 1) * LANES]
    cqt_ref[...] = (tr[_R_CQ:_R_CV] * (DIFF_QK_DIM ** -0.5 * LOG2E)).astype(BF16)
    pad = DIFF_VROWS - DIFF_V_DIM
    ones_row = (lax.broadcasted_iota(I32, (pad, tm), 0) == 0).astype(BF16)
    for hh in range(DIFF_HEADS):
        cvt_ref[hh * DIFF_VROWS:hh * DIFF_VROWS + DIFF_V_DIM, :] = (
            tr[_R_CV + hh * DIFF_V_DIM:_R_CV + (hh + 1) * DIFF_V_DIM].astype(BF16))
        cvt_ref[hh * DIFF_VROWS + DIFF_V_DIM:(hh + 1) * DIFF_VROWS, :] = ones_row


def _inproj(x2, norm_w, w_main, w_t, conv_w, conv_b, tm, seq, diff_key_tile):
    t, d = x2.shape
    hb = tm // SUBLANES
    n_hblk = t // SUBLANES
    row = lambda i: (i, 0)
    col = lambda i: (0, i)
    full = lambda i: (0, 0)
    row_outs = [(A_KV_DIM, BF16), (SSD_D_INNER, F32), (SSD_D_INNER, F32), (SSD_BC_DIM, F32),
                (DIFF_HEADS * LANES, BF16), (LANES, F32)]
    out_shape = [jax.ShapeDtypeStruct((t, w), dt) for w, dt in row_outs]
    out_specs = [pl.BlockSpec((tm, w), row) for w, _ in row_outs]
    out_shape += [jax.ShapeDtypeStruct((A_Q_DIM, t), BF16),
                  jax.ShapeDtypeStruct((t // LANES, A_KV_DIM, LANES), BF16),
                  jax.ShapeDtypeStruct((DIFF_QK_WIDTH, t), BF16),
                  jax.ShapeDtypeStruct((DIFF_HEADS * DIFF_VROWS, t), BF16),
                  jax.ShapeDtypeStruct((SSD_BC_DIM, t), F32)]
    out_specs += [pl.BlockSpec((A_Q_DIM, tm), col),
                  pl.BlockSpec((tm // LANES, A_KV_DIM, LANES), lambda i: (i, 0, 0)),
                  pl.BlockSpec((DIFF_QK_WIDTH, tm), col),
                  pl.BlockSpec((DIFF_HEADS * DIFF_VROWS, tm), col),
                  pl.BlockSpec((SSD_BC_DIM, tm), col)]
    return pl.pallas_call(
        functools.partial(_inproj_kernel, tiles_per_seq=seq // tm, diff_key_tile=diff_key_tile),
        grid=(t // tm,),
        in_specs=[pl.BlockSpec((tm, d), row),
                  pl.BlockSpec((SUBLANES, d), lambda i: (jnp.maximum(i * hb - 1, 0), 0)),
                  pl.BlockSpec((SUBLANES, d), lambda i: (jnp.minimum((i + 1) * hb, n_hblk - 1), 0)),
                  pl.BlockSpec((1, d), full),
                  pl.BlockSpec(w_main.shape, full), pl.BlockSpec(w_t.shape, full),
                  pl.BlockSpec(conv_w.shape, full), pl.BlockSpec(conv_b.shape, full)],
        out_specs=out_specs,
        out_shape=out_shape,
        compiler_params=_cparams("parallel"),
        name="inproj",
    )(x2, x2, x2, norm_w, w_main, w_t, conv_w, conv_b)


def _swa_kernel(sink_ref, slope_ref, qt_ref, k_ref, vt_ref, o_ref):
    step = pl.program_id(1)
    s_len = k_ref.shape[0]
    blk = SWA_WINDOW
    band = 3 * blk
    nb = s_len // blk
    rep = SWA_HEADS // SWA_KV_HEADS
    hd = HEAD_DIM
    blocks_per_step = qt_ref.shape[1] // blk
    for u in range(blocks_per_step):
        n = step * blocks_per_step + u
        start_blk = jnp.clip(n - 1, 0, nb - 3)
        start = pl.multiple_of(start_blk * blk, blk)
        kb = k_ref[pl.ds(start, band), :]
        v3 = vt_ref[pl.ds(start_blk, 3)]
        vtb = jnp.concatenate([v3[0], v3[1], v3[2]], axis=1)
        qt = qt_ref[:, u * blk:(u + 1) * blk]
        zero = jnp.zeros((hd, rep * blk), BF16)
        grp = [jnp.concatenate([qt[(g * rep + r) * hd:(g * rep + r + 1) * hd] for r in range(rep)], axis=1)
               for g in range(SWA_KV_HEADS)]
        qbd = jnp.concatenate([jnp.concatenate([grp[0], zero], axis=1),
                               jnp.concatenate([zero, grp[1]], axis=1)], axis=0)
        st = jnp.dot(kb, qbd, preferred_element_type=F32)
        kpos = start + lax.broadcasted_iota(I32, (band, blk), 0)
        qpos = n * blk + lax.broadcasted_iota(I32, (band, blk), 1)
        dist_i = jnp.abs(qpos - kpos)
        valid = dist_i <= SWA_WINDOW
        dist = dist_i.astype(F32)
        ps, inv = [], []
        for h in range(SWA_HEADS):
            s = jnp.where(valid, st[:, h * blk:(h + 1) * blk] - slope_ref[h] * dist, NEG)
            sink = sink_ref[h]
            m = jnp.maximum(jnp.max(s, axis=0, keepdims=True), sink)
            p = jnp.exp2(s - m)
            inv.append(1.0 / (jnp.sum(p, axis=0, keepdims=True) + jnp.exp2(sink - m)))
            ps.append(p.astype(BF16))
        outs = []
        for g in range(SWA_KV_HEADS):
            pg = jnp.concatenate(ps[g * rep:(g + 1) * rep], axis=1)
            og = jnp.dot(vtb[g * hd:(g + 1) * hd, :], pg, preferred_element_type=F32)
            for r in range(rep):
                outs.append(og[:, r * blk:(r + 1) * blk] * inv[g * rep + r])
        o_ref[u * blk:(u + 1) * blk, :] = jnp.transpose(jnp.concatenate(outs, axis=0)).astype(o_ref.dtype)


def _swa(aqt, ak, avt3, sink, slopes, batch, seq):
    blk = SWA_WINDOW
    rows = min(blk * SWA_BLOCKS_PER_STEP, seq)
    steps = seq // rows
    nb = seq // blk
    t = batch * seq
    smem = pl.BlockSpec(memory_space=pltpu.SMEM)
    return pl.pallas_call(
        _swa_kernel,
        grid=(batch, steps),
        in_specs=[smem, smem,
                  pl.BlockSpec((A_Q_DIM, rows), lambda b, s: (0, b * steps + s)),
                  pl.BlockSpec((seq, A_KV_DIM), lambda b, s: (b, 0)),
                  pl.BlockSpec((nb, A_KV_DIM, blk), lambda b, s: (b, 0, 0))],
        out_specs=pl.BlockSpec((rows, A_Q_DIM), lambda b, s: (b * steps + s, 0)),
        out_shape=jax.ShapeDtypeStruct((t, A_Q_DIM), BF16),
        compiler_params=_cparams("parallel", "parallel"),
        name="swa",
    )(sink, slopes, aqt, ak, avt3)


def _diff_key_tile(i, j, tq, tk, nk):
    return ((i * tq) // tk + j) % nk


def _diff_key_features(pos_in_tile):
    lane = lax.broadcasted_iota(I32, pos_in_tile.shape, 1) % LANES - 2 * DIFF_QK_DIM
    coarse = ((pos_in_tile // 16) * 16).astype(F32)
    fine = (pos_in_tile % 16).astype(F32)
    f = lane % DIFF_NFEAT
    feat = jnp.where(f < 2, coarse, jnp.where(f < 4, fine, 1.0))
    return jnp.where((lane >= 0) & (lane < 2 * DIFF_NFEAT), feat, 0.0)


def _diff_kernel(slope_ref, qt_ref, k_ref, vt_ref, lam_ref, sw_ref, o_ref, qtb_ref, m_ref, acc_ref,
                 s_ref, p_ref, mx_ref, *, lambda_init, n_query_tiles, n_key_tiles):
    t = pl.program_id(1)
    n_pairs = n_query_tiles * n_key_tiles
    tq = qt_ref.shape[1]
    tk = k_ref.shape[0]
    dq = DIFF_QK_DIM
    hw = 2 * dq
    nf = DIFF_NFEAT
    nkt = n_key_tiles

    def reset_stats():
        m_ref[...] = jnp.full(m_ref.shape, NEG, F32)
        acc_ref[...] = jnp.zeros(acc_ref.shape, F32)

    def build_queries():
        ii = lax.broadcasted_iota(I32, (1, 2 * tq), 1)
        ii = jnp.where(ii >= tq, ii - tq, ii).astype(F32)
        qt = qt_ref[...]
        col = lax.broadcasted_iota(I32, (hw, 2 * tq), 1)
        row = lax.broadcasted_iota(I32, (hw, 2 * tq), 0)
        own_map = row // dq == col // tq
        for h in range(DIFF_HEADS):
            qh = qt[h * hw:(h + 1) * hw, :]
            qh2 = jnp.where(own_map, jnp.concatenate([qh, qh], axis=1), jnp.zeros((hw, 2 * tq), BF16))
            sl = jnp.full((1, 2 * tq), slope_ref[h], F32)
            s_hi, s_lo = _bf16_split(sl)
            v_hi, v_lo = _bf16_split(-sl * ii)
            rows = jnp.concatenate([s_hi, s_lo, s_hi, s_lo, v_hi, v_lo], axis=0)
            zrow = jnp.zeros((nf, 2 * tq), F32)
            zero = jnp.zeros((hw - 2 * nf, 2 * tq), F32)
            variants = ([rows, zrow], [zrow, -rows], [zrow, zrow])
            for v, pieces in enumerate(variants):
                qtb_ref[v, h, 0:hw, :] = qh2
                qtb_ref[v, h, hw:2 * hw, :] = jnp.concatenate(pieces + [zero], axis=0).astype(BF16)

    def pair_of(step):
        c = jnp.clip(step, 0, n_pairs - 1)
        return c // nkt, c % nkt

    def key_start(qi, step):
        return _diff_key_tile(qi, step, tq, tk, nkt) * tk

    def score_head(slot, h, variant, dist):
        s = jnp.dot(k_ref[:, h * LANES:(h + 1) * LANES], qtb_ref[variant, h], preferred_element_type=F32)
        if dist is not None:
            s = s - slope_ref[h] * dist
        s_ref[slot, h] = s
        mx_ref[slot, h:h + 1, :] = jnp.max(s, axis=0, keepdims=True)

    def softmax_head(slot, h, shift):
        m_old = m_ref[h:h + 1, :]
        m_new = jnp.maximum(m_old, mx_ref[slot, h:h + 1, :] + shift)
        p_ref[h] = jnp.exp2(s_ref[slot, h] - (m_new - shift)).astype(BF16)
        m_ref[h:h + 1, :] = m_new
        return jnp.exp2(m_old - m_new)

    def value_head(h, alpha):
        pv = jnp.dot(vt_ref[h * DIFF_VROWS:(h + 1) * DIFF_VROWS, :], p_ref[h], preferred_element_type=F32)
        acc_ref[h] = alpha * acc_ref[h] + pv

    def tile_shift(step):
        qi, js = pair_of(step)
        k0 = key_start(qi, js)
        q0 = qi * tq
        sign = jnp.where(js == 0, 0.0, jnp.where(k0 < q0, 1.0, -1.0))
        return sign * (k0 - q0).astype(F32)

    def diagonal_scores(slot):
        qi, _ = pair_of(t)
        kpos = key_start(qi, 0) + lax.broadcasted_iota(I32, (tk, 2 * tq), 0)
        qpos = qi * tq + lax.broadcasted_iota(I32, (tk, 2 * tq), 1) % tq
        dist = jnp.abs(qpos - kpos).astype(F32)
        for h in range(DIFF_HEADS):
            score_head(slot, h, 2, dist)

    def finish_query_tile():
        lp = lam_ref[...]
        lam = (jnp.exp(jnp.sum(lp[0:1] * lp[1:2], axis=-1, keepdims=True))
               - jnp.exp(jnp.sum(lp[2:3] * lp[3:4], axis=-1, keepdims=True)) + lambda_init)
        outs = []
        for h in range(DIFF_HEADS):
            a = acc_ref[h]
            o = a[0:DIFF_V_DIM] / a[DIFF_V_DIM:DIFF_V_DIM + 1]
            o = o[:, 0:tq] - lam * o[:, tq:2 * tq]
            ms = jnp.mean(o * o, axis=0, keepdims=True)
            outs.append(o * lax.rsqrt(ms + NORM_EPS) * sw_ref[...] * (1.0 - lambda_init))
        o_ref[...] = jnp.transpose(jnp.concatenate(outs, axis=0)).astype(o_ref.dtype)

    _, step_in_tile = pair_of(t)
    is_first = (t < n_pairs) & (step_in_tile == 0)

    @pl.when(t == 0)
    def _():
        reset_stats()
        build_queries()
        diagonal_scores(0)

    for parity in range(2):
        @pl.when((t > 0) & (t < n_pairs) & jnp.logical_not(is_first) & (t % 2 == parity))
        def _():
            qi, js = pair_of(t)
            variant = jnp.where(key_start(qi, js) < qi * tq, 0, 1)
            shift = tile_shift(t - 1)
            alphas = []
            for h in range(DIFF_HEADS):
                alphas.append(softmax_head(1 - parity, h, slope_ref[h] * shift))
                score_head(parity, h, variant, None)
                value_head(h, alphas[h])

        if nkt % 2 == 0 and parity == 1:
            continue

        @pl.when((t > 0) & is_first & (t % 2 == parity))
        def _():
            shift = tile_shift(t - 1)
            alphas = [softmax_head(1 - parity, h, slope_ref[h] * shift) for h in range(DIFF_HEADS)]
            build_queries()
            diagonal_scores(parity)
            for h in range(DIFF_HEADS):
                value_head(h, alphas[h])
            finish_query_tile()
            reset_stats()

    @pl.when(t == n_pairs)
    def _():
        shift = tile_shift(t - 1)
        alphas = [softmax_head((n_pairs - 1) % 2, h, slope_ref[h] * shift) for h in range(DIFF_HEADS)]
        for h in range(DIFF_HEADS):
            value_head(h, alphas[h])
        finish_query_tile()


def _diff(cqt, ck, cvt, slopes, lam_params, subln_w_col, lambda_init, batch, seq, tq, tk):
    assert tk % tq == 0 and seq % tk == 0
    nq, nk = seq // tq, seq // tk
    t = batch * seq
    smem = pl.BlockSpec(memory_space=pltpu.SMEM)
    n_pairs = nq * nk

    def query_tile(p):
        return jnp.clip(p, 0, n_pairs - 1) // nk

    def key_tile(p):
        c = jnp.clip(p, 0, n_pairs - 1)
        return _diff_key_tile(c // nk, c % nk, tq, tk, nk)

    return pl.pallas_call(
        functools.partial(_diff_kernel, lambda_init=lambda_init, n_query_tiles=nq, n_key_tiles=nk),
        grid=(batch, n_pairs + 1),
        in_specs=[smem,
                  pl.BlockSpec((DIFF_QK_WIDTH, tq), lambda b, p: (0, b * nq + query_tile(p))),
                  pl.BlockSpec((tk, DIFF_HEADS * LANES), lambda b, p: (b * nk + key_tile(p), 0)),
                  pl.BlockSpec((DIFF_HEADS * DIFF_VROWS, tk), lambda b, p: (0, b * nk + key_tile(p - 1))),
                  pl.BlockSpec(lam_params.shape, lambda b, p: (0, 0)),
                  pl.BlockSpec(subln_w_col.shape, lambda b, p: (0, 0))],
        out_specs=pl.BlockSpec((tq, DIFF_V_WIDTH), lambda b, p: (b * nq + query_tile(p - 1), 0)),
        out_shape=jax.ShapeDtypeStruct((t, DIFF_V_WIDTH), BF16),
        scratch_shapes=[pltpu.VMEM((3, DIFF_HEADS, LANES, 2 * tq), BF16),
                        pltpu.VMEM((DIFF_HEADS, 2 * tq), F32),
                        pltpu.VMEM((DIFF_HEADS, DIFF_VROWS, 2 * tq), F32),
                        pltpu.VMEM((2, DIFF_HEADS, tk, 2 * tq), F32),
                        pltpu.VMEM((DIFF_HEADS, tk, 2 * tq), BF16),
                        pltpu.VMEM((2, DIFF_HEADS, 2 * tq), F32)],
        compiler_params=_cparams("parallel", "arbitrary"),
        name="diffattn",
    )(slopes, cqt, ck, cvt, lam_params, subln_w_col)


def _ssd_direction(fwd, off, xs_ref, bt_ref, cm_ref, dt_ref, dtb_ref, alog_ref, dsk_ref, y_ref, state_ref):
    q = SSD_CHUNK
    rows = slice(off, off + q)
    lane0 = 0 if fwd else SSD_HEADS
    dt_all = _softplus(dt_ref[rows, :] + dtb_ref[...])
    dta_all = dt_all * -jnp.exp(alog_ref[...])
    row = lax.broadcasted_iota(I32, (q, q), 0)
    col = lax.broadcasted_iota(I32, (q, q), 1)
    keep = (row >= col) if fwd else (row <= col)
    tri = keep.astype(BF16)
    part_hi = dta_all.astype(BF16)
    rest = dta_all - part_hi.astype(F32)
    part_mid = rest.astype(BF16)
    part_lo = (rest - part_mid.astype(F32)).astype(BF16)
    da_all = (jnp.dot(tri, part_hi, preferred_element_type=F32)
              + jnp.dot(tri, part_mid, preferred_element_type=F32)
              + jnp.dot(tri, part_lo, preferred_element_type=F32))
    da_all_t = jnp.transpose(da_all)
    dt_all_t = jnp.transpose(dt_all)
    tot_all = jnp.sum(dta_all, axis=0, keepdims=True)

    xs = xs_ref[rows, :]
    bmt = bt_ref[:, rows]
    cm = cm_ref[rows, :]
    rep = SSD_HEADS // SSD_GROUPS
    ns = SSD_STATE
    lane = lax.broadcasted_iota(I32, (q, LANES), 1)
    low = lane < SSD_HEAD_DIM
    low_n = lax.broadcasted_iota(I32, (ns, LANES), 1) < SSD_HEAD_DIM
    cm_g = [jnp.where((lane // ns) == g, cm, 0.0) for g in range(SSD_GROUPS)]
    bmt_b = bmt.astype(BF16)
    g_mats = [jnp.dot(cm_g[g].astype(BF16), bmt_b, preferred_element_type=F32)
              for g in range(SSD_GROUPS)]
    zeros_n = jnp.zeros((ns, LANES), BF16)
    ys = []
    for pair in range(SSD_HEADS // 2):
        x_pair = xs[:, pair * LANES:(pair + 1) * LANES]
        x_pair_b = x_pair.astype(BF16)
        st = state_ref[pair]
        st_b = st.astype(BF16)
        y_heads, s_heads, keep_heads = [], [], []
        for h in (2 * pair, 2 * pair + 1):
            g = h // rep
            ln = lane0 + h
            dac = da_all[:, ln:ln + 1]
            dar = da_all_t[ln:ln + 1, :]
            dtr = dt_all_t[ln:ln + 1, :]
            tot = tot_all[:, ln:ln + 1]
            dac_b = jnp.broadcast_to(dac, (q, q))
            decay = jnp.exp(jnp.where(keep, dac_b - dar, NEG))
            y = jnp.dot((g_mats[g] * decay * dtr).astype(BF16), x_pair_b, preferred_element_type=F32)
            st_ext = jnp.concatenate([st_b, zeros_n] if g == 0 else [zeros_n, st_b], axis=0)
            c_in = (cm_g[g] * jnp.exp(dac_b)).astype(BF16)
            y_heads.append(y + jnp.dot(c_in, st_ext, preferred_element_type=F32))
            to_end = jnp.exp(tot - dar) * dtr
            b_out = (bmt[g * ns:(g + 1) * ns, :] * to_end).astype(BF16)
            s_heads.append(jnp.dot(b_out, x_pair_b, preferred_element_type=F32))
            keep_heads.append(jnp.exp(tot))
        state_keep = jnp.where(low_n, jnp.broadcast_to(keep_heads[0], (ns, LANES)),
                               jnp.broadcast_to(keep_heads[1], (ns, LANES)))
        state_ref[pair] = st * state_keep + jnp.where(low_n, s_heads[0], s_heads[1])
        y = jnp.where(low, y_heads[0], y_heads[1])
        if fwd:
            y = y + dsk_ref[:, pair * LANES:(pair + 1) * LANES] * x_pair
        ys.append(y)
    y_ref[rows, :] = jnp.concatenate(ys, axis=-1)


def _ssd_kernel(xsf_ref, btf_ref, cmf_ref, dtf_ref, xsb_ref, btb_ref, cmb_ref, dtb_in_ref,
                dtbias_ref, alog_ref, dsk_ref, yf_ref, yb_ref, state_ref):
    @pl.when(pl.program_id(1) == 0)
    def _():
        state_ref[...] = jnp.zeros(state_ref.shape, F32)

    n = SSD_CHUNKS_PER_STEP
    for u in range(n):
        _ssd_direction(True, u * SSD_CHUNK, xsf_ref, btf_ref, cmf_ref, dtf_ref, dtbias_ref, alog_ref, dsk_ref,
                       yf_ref, state_ref.at[0])
        _ssd_direction(False, (n - 1 - u) * SSD_CHUNK, xsb_ref, btb_ref, cmb_ref, dtb_in_ref, dtbias_ref,
                       alog_ref, dsk_ref, yb_ref, state_ref.at[1])


def _ssd(xs, bt, cm, dt_raw, dt_bias, a_log, d_skip, batch, seq):
    q = SSD_CHUNK * SSD_CHUNKS_PER_STEP
    nc = seq // q
    t = batch * seq
    full = lambda b, c: (0, 0)
    fw = lambda b, c: b * nc + c
    bw = lambda b, c: b * nc + nc - 1 - c

    def specs(idx):
        return [pl.BlockSpec((q, SSD_D_INNER), lambda b, c: (idx(b, c), 0)),
                pl.BlockSpec((SSD_BC_DIM, q), lambda b, c: (0, idx(b, c))),
                pl.BlockSpec((q, SSD_BC_DIM), lambda b, c: (idx(b, c), 0)),
                pl.BlockSpec((q, LANES), lambda b, c: (idx(b, c), 0))]

    return pl.pallas_call(
        _ssd_kernel,
        grid=(batch, nc),
        in_specs=specs(fw) + specs(bw) + [pl.BlockSpec(dt_bias.shape, full), pl.BlockSpec(a_log.shape, full),
                                          pl.BlockSpec(d_skip.shape, full)],
        out_specs=[pl.BlockSpec((q, SSD_D_INNER), lambda b, c: (fw(b, c), 0)),
                   pl.BlockSpec((q, SSD_D_INNER), lambda b, c: (bw(b, c), 0))],
        out_shape=[jax.ShapeDtypeStruct((t, SSD_D_INNER), F32), jax.ShapeDtypeStruct((t, SSD_D_INNER), F32)],
        scratch_shapes=[pltpu.VMEM((2, SSD_HEADS // 2, SSD_STATE, 2 * SSD_HEAD_DIM), F32)],
        compiler_params=_cparams("parallel", "arbitrary"),
        name="ssd",
    )(xs, bt, cm, dt_raw, xs, bt, cm, dt_raw, dt_bias, a_log, d_skip)


def _outproj_kernel(x_ref, ya_ref, yf_ref, yb_ref, z_ref, snw_ref, yc_ref, wo_ref, fnw_ref, wrt_ref, br_ref,
                    xn_ref, ri_ref, rf_ref, cnt_ref, tri_ref, carry_ref):
    step = pl.program_id(0)
    tm = x_ref.shape[0]

    @pl.when(step == 0)
    def _():
        carry_ref[...] = jnp.zeros(carry_ref.shape, F32)
        r = lax.broadcasted_iota(I32, (tm, tm), 0)
        cc = lax.broadcasted_iota(I32, (tm, tm), 1)
        tri_ref[...] = (r <= cc).astype(BF16)

    y = (yf_ref[...] + yb_ref[...]) * _silu(z_ref[...])
    yb = _rms(y, snw_ref[...]).astype(BF16)
    acc = jnp.dot(ya_ref[...], wo_ref[0:A_Q_DIM, :], preferred_element_type=F32)
    acc = acc + jnp.dot(yb, wo_ref[A_Q_DIM:A_Q_DIM + SSD_D_INNER, :], preferred_element_type=F32)
    acc = acc + jnp.dot(yc_ref[...], wo_ref[A_Q_DIM + SSD_D_INNER:, :], preferred_element_type=F32)
    xn = x_ref[...] + acc
    xn_ref[...] = xn

    h = _rms(xn, fnw_ref[...])
    h_hi = h.astype(BF16)
    h_lo = (h - h_hi.astype(F32)).astype(BF16)
    nt = (((1,), (1,)), ((), ()))
    logits = (lax.dot_general(wrt_ref[0], h_hi, nt, preferred_element_type=F32)
              + lax.dot_general(wrt_ref[0], h_lo, nt, preferred_element_type=F32)
              + lax.dot_general(wrt_ref[1], h_hi, nt, preferred_element_type=F32)) + br_ref[...]
    ne, epg, ng = N_EXPERTS, EXPERTS_PER_GROUP, N_EXPERT_GROUPS
    gl = logits[ne:ne + ng, :]
    gmax = jnp.max(gl, axis=0, keepdims=True)
    g_sel = jnp.full((1, tm), float(ng - 1), F32)
    for g in range(ng - 2, -1, -1):
        g_sel = jnp.where(gl[g:g + 1, :] == gmax, float(g), g_sel)
    g_gate = 1.0 / jnp.sum(jnp.exp(gl - gmax), axis=0, keepdims=True)
    e_in = logits[0:epg, :]
    for g in range(1, ng):
        e_in = jnp.where(g_sel == float(g), logits[g * epg:(g + 1) * epg, :], e_in)
    sub = lax.broadcasted_iota(I32, (epg, tm), 0).astype(F32)
    m1 = jnp.max(e_in, axis=0, keepdims=True)
    i1 = jnp.min(jnp.where(e_in == m1, sub, float(epg)), axis=0, keepdims=True)
    rest = jnp.where(sub == i1, NEG, e_in)
    m2 = jnp.max(rest, axis=0, keepdims=True)
    i2 = jnp.min(jnp.where(rest == m2, sub, float(epg)), axis=0, keepdims=True)
    r = jnp.exp(m2 - m1)
    c1 = g_gate / (1.0 + r)
    c2 = g_gate * r / (1.0 + r)
    e1 = (g_sel * epg + i1).astype(I32)
    e2 = (g_sel * epg + i2).astype(I32)

    erow = lax.broadcasted_iota(I32, (ne, tm), 0)
    hit1 = erow == e1
    hit2 = erow == e2
    oh = jnp.where(hit1 | hit2, 1.0, 0.0)
    incl = jnp.dot(oh.astype(BF16), tri_ref[...], preferred_element_type=F32)
    before = incl - oh + carry_ref[:, 0:1]
    rank1 = jnp.sum(jnp.where(hit1, before, 0.0), axis=0, keepdims=True)
    rank2 = jnp.sum(jnp.where(hit2, before, 0.0), axis=0, keepdims=True)
    carry_ref[...] = carry_ref[...] + jnp.sum(oh, axis=1, keepdims=True)
    cnt_ref[...] = carry_ref[...]
    zi = jnp.zeros((1, tm), I32)
    ri_ref[...] = jnp.concatenate([e1, e2, rank1.astype(I32), rank2.astype(I32), zi, zi, zi, zi], axis=0)
    zf = jnp.zeros((1, tm), F32)
    rf_ref[...] = jnp.concatenate([c1, c2, zf, zf, zf, zf, zf, zf], axis=0)


def _outproj(x2, ya, yf, yb, z, ssd_norm_w, yc, w_out, ffn_norm_w, wrt, br, tm):
    t, d = x2.shape
    row = lambda i: (i, 0)
    full = lambda i: (0, 0)
    return pl.pallas_call(
        _outproj_kernel,
        grid=(t // tm,),
        in_specs=[pl.BlockSpec((tm, d), row),
                  pl.BlockSpec((tm, A_Q_DIM), row),
                  pl.BlockSpec((tm, SSD_D_INNER), row),
                  pl.BlockSpec((tm, SSD_D_INNER), row),
                  pl.BlockSpec((tm, SSD_D_INNER), row),
                  pl.BlockSpec(ssd_norm_w.shape, full),
                  pl.BlockSpec((tm, DIFF_V_WIDTH), row),
                  pl.BlockSpec(w_out.shape, full),
                  pl.BlockSpec(ffn_norm_w.shape, full),
                  pl.BlockSpec(wrt.shape, lambda i: (0, 0, 0)),
                  pl.BlockSpec(br.shape, full)],
        out_specs=[pl.BlockSpec((tm, d), row),
                   pl.BlockSpec((SUBLANES, tm), lambda i: (0, i)),
                   pl.BlockSpec((SUBLANES, tm), lambda i: (0, i)),
                   pl.BlockSpec((N_EXPERTS, LANES), full)],
        out_shape=[jax.ShapeDtypeStruct((t, d), F32),
                   jax.ShapeDtypeStruct((SUBLANES, t), I32),
                   jax.ShapeDtypeStruct((SUBLANES, t), F32),
                   jax.ShapeDtypeStruct((N_EXPERTS, LANES), F32)],
        scratch_shapes=[pltpu.VMEM((tm, tm), BF16), pltpu.VMEM((N_EXPERTS, LANES), F32)],
        compiler_params=_cparams("arbitrary"),
        name="outproj_router",
    )(x2, ya, yf, yb, z, ssd_norm_w, yc, w_out, ffn_norm_w, wrt, br)


_PAD_PIECES = tuple(1 << b for b in reversed(range(MOE_ROW_TILE.bit_length() - 1)))


def _dispatch_kernel(slot1_ref, slot2_ref, pstart_ref, plen_ref, nused_ref, x_ref, xs_hbm, zero_ref, xbuf, sem):
    i = pl.program_id(0)
    tm = x_ref.shape[0]

    @pl.when(i == 0)
    def _():
        zero_ref[...] = jnp.zeros(zero_ref.shape, F32)

        def pieces(e, wait):
            n = plen_ref[e]
            first = pstart_ref[e]
            off = first + n
            for b in _PAD_PIECES:
                off = off - (n & b)
                dst = pl.ds(pl.multiple_of(off, b), b) if b >= SUBLANES else None
                if dst is not None:
                    @pl.when((n & b) != 0)
                    def _():
                        cp = pltpu.make_async_copy(zero_ref.at[pl.ds(0, b)], xs_hbm.at[dst], sem.at[2])
                        cp.wait() if wait else cp.start()

            for u in range(SUBLANES - 1):
                @pl.when(u < (n & (SUBLANES - 1)))
                def _():
                    cp = pltpu.make_async_copy(zero_ref.at[pl.ds(0, 1)], xs_hbm.at[pl.ds(first + u, 1)],
                                               sem.at[2])
                    cp.wait() if wait else cp.start()

        def tail(tile, wait):
            big = _PAD_PIECES[0]
            for part in range(MOE_ROW_TILE // big):
                dst = xs_hbm.at[pl.ds(pl.multiple_of(tile * MOE_ROW_TILE + part * big, big), big)]
                cp = pltpu.make_async_copy(zero_ref, dst, sem.at[2])
                cp.wait() if wait else cp.start()

        def loop(fn, lo, hi, wait):
            def body(k, carry):
                fn(k, wait)
                return carry

            lax.fori_loop(lo, hi, body, 0)

        n_tiles = xs_hbm.shape[0] // MOE_ROW_TILE
        for wait in (False, True):
            loop(pieces, 0, N_EXPERTS, wait)
            loop(tail, nused_ref[0], n_tiles, wait)

    base = i * tm
    last = pl.num_programs(0) - 1

    def drain(slot):
        for _ in range(2):
            pltpu.make_async_copy(xbuf.at[slot], xs_hbm.at[pl.ds(0, tm)], sem.at[slot]).wait()

    for parity in range(2):
        @pl.when(i % 2 == parity)
        def _():
            buf = xbuf.at[parity]
            buf[...] = x_ref[...]
            for r in range(tm):
                src = buf.at[pl.ds(r, 1)]
                pltpu.make_async_copy(src, xs_hbm.at[pl.ds(slot1_ref[base + r], 1)],
                                      sem.at[parity]).start(priority=0)
                pltpu.make_async_copy(src, xs_hbm.at[pl.ds(slot2_ref[base + r], 1)],
                                      sem.at[parity]).start(priority=1)

            @pl.when(i > 0)
            def _():
                drain(1 - parity)

            @pl.when(i == last)
            def _():
                drain(parity)


def _dispatch(xn, slot1, slot2, pad_start, pad_len, n_used, n_rows, tm):
    t, d = xn.shape
    grid_spec = pltpu.PrefetchScalarGridSpec(
        num_scalar_prefetch=5,
        grid=(t // tm,),
        in_specs=[pl.BlockSpec((tm, d), lambda i, s1, s2, ps, pn, nu: (i, 0))],
        out_specs=pl.BlockSpec(memory_space=pl.ANY),
        scratch_shapes=[pltpu.VMEM((_PAD_PIECES[0], d), F32), pltpu.VMEM((2, tm, d), F32),
                        pltpu.SemaphoreType.DMA((3,))],
    )
    return pl.pallas_call(
        _dispatch_kernel,
        grid_spec=grid_spec,
        out_shape=jax.ShapeDtypeStruct((n_rows, d), F32),
        compiler_params=_cparams("arbitrary"),
        name="moe_dispatch",
    )(slot1, slot2, pad_start, pad_len, n_used, xn)


def _moe_kernel(texp_ref, nused_ref, first_ref, wslot_ref, next_ref, x_ref, fnw_ref, wg_hbm, wu_hbm, wd_hbm,
                y_ref, wg_buf, wu_buf, wd_buf, sem, *, layer):
    i = pl.program_id(0)

    def weight_copies(expert, slot):
        return [pltpu.make_async_copy(hbm.at[layer, expert], buf.at[slot], sem.at[slot])
                for hbm, buf in ((wg_hbm, wg_buf), (wu_hbm, wu_buf), (wd_hbm, wd_buf))]

    @pl.when(i < nused_ref[0])
    def _():
        slot = wslot_ref[i]

        @pl.when(i == 0)
        def _():
            for cp in weight_copies(texp_ref[0], 0):
                cp.start()

        @pl.when(first_ref[i] == 1)
        def _():
            for cp in weight_copies(texp_ref[i], slot):
                cp.wait()

            @pl.when(next_ref[i] >= 0)
            def _():
                for cp in weight_copies(next_ref[i], 1 - slot):
                    cp.start()

        h = _rms(x_ref[...], fnw_ref[...]).astype(BF16)
        hg = jnp.dot(h, wg_buf[slot].astype(BF16), preferred_element_type=F32)
        hu = jnp.dot(h, wu_buf[slot].astype(BF16), preferred_element_type=F32)
        act = (_silu(hg) * hu).astype(BF16)
        y_ref[...] = jnp.dot(act, wd_buf[slot].astype(BF16), preferred_element_type=F32)

    @pl.when(i >= nused_ref[0])
    def _():
        y_ref[...] = jnp.zeros(y_ref.shape, F32)


def _moe(xs, ffn_norm_w, w_gate, w_up, w_down, layer, tile_expert, n_used):
    n_rows, d = xs.shape
    f = w_gate.shape[-1]
    tr = MOE_ROW_TILE

    n_tiles = n_rows // tr

    idx = jnp.arange(n_tiles, dtype=I32)
    prev_expert = jnp.concatenate([jnp.full((1,), -1, I32), tile_expert[:-1]])
    first = ((idx < n_used[0]) & (tile_expert != prev_expert)).astype(I32)
    wslot = ((jnp.cumsum(first) - 1) % 2).astype(I32)
    first_pos = jnp.where(first == 1, idx, n_tiles)
    next_first = jnp.concatenate([lax.cummin(first_pos, reverse=True)[1:], jnp.full((1,), n_tiles, I32)])
    next_expert = jnp.where(next_first < n_tiles, tile_expert[jnp.minimum(next_first, n_tiles - 1)], -1).astype(I32)

    def used(i, nu):
        return jnp.maximum(jnp.minimum(i, nu[0] - 1), 0)

    anyspace = pl.BlockSpec(memory_space=pl.ANY)
    grid_spec = pltpu.PrefetchScalarGridSpec(
        num_scalar_prefetch=5,
        grid=(n_tiles,),
        in_specs=[pl.BlockSpec((tr, d), lambda i, te, nu, fi, ws, nx: (used(i, nu), 0)),
                  pl.BlockSpec(ffn_norm_w.shape, lambda i, te, nu, fi, ws, nx: (0, 0)),
                  anyspace, anyspace, anyspace],
        out_specs=pl.BlockSpec((tr, d), lambda i, te, nu, fi, ws, nx: (i, 0)),
        scratch_shapes=[pltpu.VMEM((2, d, f), F32), pltpu.VMEM((2, d, f), F32), pltpu.VMEM((2, f, d), F32),
                        pltpu.SemaphoreType.DMA((2,))],
    )
    return pl.pallas_call(
        functools.partial(_moe_kernel, layer=layer),
        grid_spec=grid_spec,
        out_shape=jax.ShapeDtypeStruct((n_rows, d), F32),
        compiler_params=_cparams("arbitrary"),
        name="moe_experts",
    )(tile_expert, n_used, first, wslot, next_expert, xs, ffn_norm_w, w_gate, w_up, w_down)


def _combine_kernel(slot1_ref, slot2_ref, x_ref, cw_ref, nw_ref, y_hbm, o_ref, ybuf, sem, *, final_norm):
    i = pl.program_id(0)
    n = pl.num_programs(0)
    tm = x_ref.shape[0]

    def start_gather(tile, slot):
        base = tile * tm
        for r in range(tm):
            pltpu.make_async_copy(y_hbm.at[pl.ds(slot1_ref[base + r], 1)], ybuf.at[slot, 0, pl.ds(r, 1)],
                                  sem.at[slot]).start(priority=0)
            pltpu.make_async_copy(y_hbm.at[pl.ds(slot2_ref[base + r], 1)], ybuf.at[slot, 1, pl.ds(r, 1)],
                                  sem.at[slot]).start(priority=1)

    def compute(slot):
        for k in range(2):
            pltpu.make_async_copy(y_hbm.at[pl.ds(0, tm)], ybuf.at[slot, k], sem.at[slot]).wait()
        cw = cw_ref[...]
        out = x_ref[...] + cw[:, 0:1] * ybuf[slot, 0] + cw[:, 1:2] * ybuf[slot, 1]
        if final_norm:
            out = _rms(out, nw_ref[...])
        o_ref[...] = out

    @pl.when(i == 0)
    def _():
        start_gather(0, 0)

    for parity in range(2):
        @pl.when(i % 2 == parity)
        def _():
            @pl.when(i + 1 < n)
            def _():
                start_gather(i + 1, 1 - parity)

            compute(parity)


def _combine(xn, cw, norm_w, y_sorted, slot1, slot2, tm, final_norm):
    t, d = xn.shape
    grid_spec = pltpu.PrefetchScalarGridSpec(
        num_scalar_prefetch=2,
        grid=(t // tm,),
        in_specs=[pl.BlockSpec((tm, d), lambda i, s1, s2: (i, 0)),
                  pl.BlockSpec((tm, cw.shape[1]), lambda i, s1, s2: (i, 0)),
                  pl.BlockSpec(norm_w.shape, lambda i, s1, s2: (0, 0)),
                  pl.BlockSpec(memory_space=pl.ANY)],
        out_specs=pl.BlockSpec((tm, d), lambda i, s1, s2: (i, 0)),
        scratch_shapes=[pltpu.VMEM((2, 2, tm, d), F32), pltpu.SemaphoreType.DMA((2,))],
    )
    return pl.pallas_call(
        functools.partial(_combine_kernel, final_norm=final_norm),
        grid_spec=grid_spec,
        out_shape=jax.ShapeDtypeStruct((t, d), F32),
        compiler_params=_cparams("arbitrary"),
        name="moe_combine",
    )(slot1, slot2, xn, cw, norm_w, y_sorted)


def _pad_lanes(v):
    v = v.reshape(1, -1).astype(F32)
    return jnp.pad(v, ((0, 0), (0, LANES - v.shape[1])))


def kernel(x, attn_norm_w, w_in, swa_sink, ssd_conv_w, ssd_conv_b, ssd_dt_bias, ssd_a_log, ssd_d, ssd_norm_w,
           diff_lambda, diff_subln_w, w_out, ffn_norm_w, w_router_group, b_router_group, w_router_expert,
           b_router_expert, w_gate, w_up, w_down, final_norm_w):
    return _forward(x, attn_norm_w, w_in, swa_sink, ssd_conv_w, ssd_conv_b, ssd_dt_bias, ssd_a_log, ssd_d,
                    ssd_norm_w, diff_lambda, diff_subln_w, w_out, ffn_norm_w, w_router_group, b_router_group,
                    w_router_expert, b_router_expert, w_gate, w_up, w_down, final_norm_w)


def _forward(x, attn_norm_w, w_in, swa_sink, ssd_conv_w, ssd_conv_b, ssd_dt_bias, ssd_a_log, ssd_d, ssd_norm_w,
             diff_lambda, diff_subln_w, w_out, ffn_norm_w, w_router_group, b_router_group, w_router_expert,
             b_router_expert, w_gate, w_up, w_down, final_norm_w, tm=512, tq=512, tk=1024, tmc=512):
    batch, seq, d = x.shape
    depth = w_in.shape[0]
    t = batch * seq
    tr = MOE_ROW_TILE
    n_tiles = (2 * t) // tr + N_EXPERTS
    slopes = jnp.exp2(-8.0 * jnp.arange(1, N_ALIBI_HEADS + 1, dtype=F32) / N_ALIBI_HEADS)
    swa_slopes, diff_slopes = slopes[:SWA_HEADS], slopes[SWA_HEADS:]

    sizes = [A_Q_DIM, A_KV_DIM, A_KV_DIM, SSD_D_INNER, SSD_CONV_DIM, SSD_DT_DIM, DIFF_QK_WIDTH, DIFF_QK_WIDTH,
             DIFF_V_WIDTH]
    offs = [0]
    for s in sizes:
        offs.append(offs[-1] + s)
    o_aq, o_ak, o_av, o_z, o_xbc, o_dt, o_cq, o_ck, o_cv, o_end = offs

    x2 = x.reshape(t, d)
    for l in range(depth):
        w = w_in[l]
        hw = 2 * DIFF_QK_DIM
        w_ck = jnp.pad(w[:, o_ck:o_cv].reshape(d, DIFF_HEADS, hw), ((0, 0), (0, 0), (0, LANES - hw)))
        w_main = jnp.concatenate(
            [w[:, o_ak:o_av], w[:, o_z:o_dt], w_ck.reshape(d, DIFF_HEADS * LANES), w[:, o_dt:o_cq],
             jnp.zeros((d, LANES - SSD_DT_DIM), w.dtype)], axis=1).astype(BF16)
        w_t = jnp.concatenate([w[:, o_aq:o_ak], w[:, o_av:o_z], w[:, o_cq:o_ck], w[:, o_cv:o_end]],
                              axis=1).T.astype(BF16)
        ak, z, xs, cm, ck, dt_raw, aqt, avt3, cqt, cvt, bt = _inproj(
            x2, attn_norm_w[l].reshape(1, d), w_main, w_t, ssd_conv_w[l].astype(F32),
            ssd_conv_b[l].reshape(1, -1).astype(F32), tm, seq, tk)

        ya = _swa(aqt, ak, avt3, swa_sink[l].astype(F32) * LOG2E, swa_slopes * LOG2E, batch, seq)
        lambda_init = 0.8 - 0.6 * math.exp(-0.3 * l)
        yc = _diff(cqt, ck, cvt, diff_slopes * LOG2E, diff_lambda[l].astype(F32),
                   diff_subln_w[l].reshape(DIFF_V_DIM, 1).astype(F32), lambda_init, batch, seq, tq, tk)
        yf, yb = _ssd(xs, bt, cm, dt_raw, _pad_lanes(ssd_dt_bias[l]), _pad_lanes(ssd_a_log[l]),
                      jnp.repeat(ssd_d[l].astype(F32), SSD_HEAD_DIM).reshape(1, SSD_D_INNER), batch, seq)

        wr32 = jnp.concatenate([w_router_expert[l], w_router_group[l],
                                jnp.zeros((d, SUBLANES - N_EXPERT_GROUPS), F32)], axis=1).T.astype(F32)
        wr_hi = wr32.astype(BF16)
        wrt = jnp.stack([wr_hi, (wr32 - wr_hi.astype(F32)).astype(BF16)])
        br = jnp.concatenate([b_router_expert[l], b_router_group[l],
                              jnp.zeros((SUBLANES - N_EXPERT_GROUPS,), F32)]).reshape(-1, 1).astype(F32)
        xn, ri, rf, cnt = _outproj(x2, ya, yf, yb, z, ssd_norm_w[l].reshape(1, -1), yc, w_out[l].astype(BF16),
                                   ffn_norm_w[l].reshape(1, d), wrt, br, tm)

        counts = cnt[:, 0].astype(I32)
        padded = ((counts + tr - 1) // tr) * tr
        ends = jnp.cumsum(padded)
        starts = ends - padded
        experts = jnp.arange(N_EXPERTS, dtype=I32)[:, None]

        def slot_of(e, rank):
            return jnp.sum(jnp.where(e[None, :] == experts, starts[:, None], 0), axis=0) + rank

        slot1 = slot_of(ri[0], ri[2])
        slot2 = slot_of(ri[1], ri[3])
        tile_start = jnp.arange(n_tiles, dtype=I32) * tr
        tile_expert = jnp.minimum(jnp.sum(ends[None, :] <= tile_start[:, None], axis=1), N_EXPERTS - 1).astype(I32)
        n_used = (ends[-1] // tr).astype(I32).reshape(1)

        xs_sorted = _dispatch(xn, slot1, slot2, starts + counts, padded - counts, n_used, n_tiles * tr, tmc)
        y_sorted = _moe(xs_sorted, ffn_norm_w[l].reshape(1, d), w_gate, w_up, w_down, l, tile_expert, n_used)
        last = l == depth - 1
        x2 = _combine(xn, rf.T, final_norm_w.reshape(1, d), y_sorted, slot1, slot2, tmc, last)
    return x2.reshape(batch, seq, d)
```

```python
import functools
import math

import jax
import jax.numpy as jnp
from jax import lax
from jax.experimental import pallas as pl
from jax.experimental.pallas import tpu as pltpu

F32 = jnp.float32
BF16 = jnp.bfloat16
I32 = jnp.int32

HEAD_DIM = 64
SWA_HEADS = 6
SWA_KV_HEADS = 2
SWA_WINDOW = 128
SSD_HEADS = 6
SSD_HEAD_DIM = 64
SSD_GROUPS = 2
SSD_STATE = 64
SSD_CONV = 5
DIFF_HEADS = 4
DIFF_QK_DIM = 32
DIFF_V_DIM = 64
N_EXPERT_GROUPS = 4
EXPERTS_PER_GROUP = 8
N_EXPERTS = N_EXPERT_GROUPS * EXPERTS_PER_GROUP
NORM_EPS = 1e-6

A_Q_DIM = SWA_HEADS * HEAD_DIM
A_KV_DIM = SWA_KV_HEADS * HEAD_DIM
SSD_D_INNER = SSD_HEADS * SSD_HEAD_DIM
SSD_BC_DIM = SSD_GROUPS * SSD_STATE
SSD_CONV_DIM = SSD_D_INNER + 2 * SSD_BC_DIM
SSD_DT_DIM = 2 * SSD_HEADS
DIFF_QK_WIDTH = DIFF_HEADS * 2 * DIFF_QK_DIM
DIFF_V_WIDTH = DIFF_HEADS * DIFF_V_DIM
N_ALIBI_HEADS = SWA_HEADS + DIFF_HEADS

LANES = 128
SUBLANES = 8
VMEM_LIMIT = 56 * 1024 * 1024
NEG = -1e30
LOG2E = math.log2(math.e)

SSD_CHUNK = 128
SSD_CHUNKS_PER_STEP = 4
SWA_BLOCKS_PER_STEP = 8
MOE_ROW_TILE = 512
DIFF_PAIR = 2 * 2 * DIFF_QK_DIM
DIFF_VROWS = 80
DIFF_NFEAT = 6


def _cparams(*sem):
    return pltpu.CompilerParams(dimension_semantics=sem, vmem_limit_bytes=VMEM_LIMIT)


def _rms(x, w):
    return x * lax.rsqrt(jnp.mean(x * x, axis=-1, keepdims=True) + NORM_EPS) * w


def _silu(x):
    return x / (1.0 + jnp.exp(-x))


def _softplus(x):
    return jnp.maximum(x, 0.0) + jnp.log(1.0 + jnp.exp(-jnp.abs(x)))


def _bf16_split(x):
    hi = x.astype(BF16).astype(F32)
    lo = (x - hi).astype(BF16).astype(F32)
    return hi, lo


_C_AK = 0
_C_Z = _C_AK + A_KV_DIM
_C_XBC = _C_Z + SSD_D_INNER
_C_CK = _C_XBC + SSD_CONV_DIM
_C_DT = _C_CK + DIFF_HEADS * LANES
_C_END = _C_DT + LANES
_R_AQ = 0
_R_AV = _R_AQ + A_Q_DIM
_R_CQ = _R_AV + A_KV_DIM
_R_CV = _R_CQ + DIFF_QK_WIDTH
_R_END = _R_CV + DIFF_V_WIDTH


def _inproj_kernel(x_ref, xp_ref, xn_ref, nw_ref, w_ref, wt_ref, cw_ref, cb_ref,
                   ak_ref, z_ref, xs_ref, cm_ref, ck_ref, dt_ref, aqt_ref, avt_ref, cqt_ref, cvt_ref, bt_ref,
                   *, tiles_per_seq, diff_key_tile):
    i = pl.program_id(0)
    tm = x_ref.shape[0]
    nw = nw_ref[...]
    h = _rms(x_ref[...], nw).astype(BF16)

    def seg(lo, hi):
        return jnp.dot(h, w_ref[:, lo:hi], preferred_element_type=F32)

    ak_ref[...] = seg(_C_AK, _C_Z).astype(BF16)
    z_ref[...] = seg(_C_Z, _C_XBC)
    pos = (i * tm + lax.broadcasted_iota(I32, (tm, _C_DT - _C_CK), 0)) % diff_key_tile
    ck_ref[...] = (seg(_C_CK, _C_DT) + _diff_key_features(pos)).astype(BF16)
    dt_ref[...] = seg(_C_DT, _C_END)

    w_xbc = w_ref[:, _C_XBC:_C_CK]
    first = i % tiles_per_seq == 0
    last = i % tiles_per_seq == tiles_per_seq - 1
    prev = jnp.dot(_rms(xp_ref[...], nw).astype(BF16), w_xbc, preferred_element_type=F32)
    nxt = jnp.dot(_rms(xn_ref[...], nw).astype(BF16), w_xbc, preferred_element_type=F32)
    prev = jnp.where(first, 0.0, prev)
    nxt = jnp.where(last, 0.0, nxt)
    ext = jnp.concatenate([prev, seg(_C_XBC, _C_CK), nxt], axis=0)
    half = SSD_CONV // 2
    conv = cb_ref[...]
    for k in range(SSD_CONV):
        off = SUBLANES - half + k
        conv = conv + cw_ref[k:k + 1, :] * ext[off:off + tm, :]
    u = _silu(conv)
    xs_ref[...] = u[:, :SSD_D_INNER]
    bt_ref[...] = jnp.transpose(u[:, SSD_D_INNER:SSD_D_INNER + SSD_BC_DIM])
    cm_ref[...] = u[:, SSD_D_INNER + SSD_BC_DIM:]

    tr = lax.dot_general(wt_ref[...], h, (((1,), (1,)), ((), ())), preferred_element_type=F32)
    aqt_ref[...] = (tr[_R_AQ:_R_AV] * (HEAD_DIM ** -0.5 * LOG2E)).astype(BF16)
    avt = tr[_R_AV:_R_CQ].astype(BF16)
    for c in range(tm // LANES):
        avt_ref[c] = avt[:, c * LANES:(c + 1) * LANES]
    cqt_ref[...] = (tr[_R_CQ:_R_CV] * (DIFF_QK_DIM ** -0.5 * LOG2E)).astype(BF16)
    pad = DIFF_VROWS - DIFF_V_DIM
    ones_row = (lax.broadcasted_iota(I32, (pad, tm), 0) == 0).astype(BF16)
    for hh in range(DIFF_HEADS):
        cvt_ref[hh * DIFF_VROWS:hh * DIFF_VROWS + DIFF_V_DIM, :] = (
            tr[_R_CV + hh * DIFF_V_DIM:_R_CV + (hh + 1) * DIFF_V_DIM].astype(BF16))
        cvt_ref[hh * DIFF_VROWS + DIFF_V_DIM:(hh + 1) * DIFF_VROWS, :] = ones_row


def _inproj(x2, norm_w, w_main, w_t, conv_w, conv_b, tm, seq, diff_key_tile):
    t, d = x2.shape
    hb = tm // SUBLANES
    n_hblk = t // SUBLANES
    row = lambda i: (i, 0)
    col = lambda i: (0, i)
    full = lambda i: (0, 0)
    row_outs = [(A_KV_DIM, BF16), (SSD_D_INNER, F32), (SSD_D_INNER, F32), (SSD_BC_DIM, F32),
                (DIFF_HEADS * LANES, BF16), (LANES, F32)]
    out_shape = [jax.ShapeDtypeStruct((t, w), dt) for w, dt in row_outs]
    out_specs = [pl.BlockSpec((tm, w), row) for w, _ in row_outs]
    out_shape += [jax.ShapeDtypeStruct((A_Q_DIM, t), BF16),
                  jax.ShapeDtypeStruct((t // LANES, A_KV_DIM, LANES), BF16),
                  jax.ShapeDtypeStruct((DIFF_QK_WIDTH, t), BF16),
                  jax.ShapeDtypeStruct((DIFF_HEADS * DIFF_VROWS, t), BF16),
                  jax.ShapeDtypeStruct((SSD_BC_DIM, t), F32)]
    out_specs += [pl.BlockSpec((A_Q_DIM, tm), col),
                  pl.BlockSpec((tm // LANES, A_KV_DIM, LANES), lambda i: (i, 0, 0)),
                  pl.BlockSpec((DIFF_QK_WIDTH, tm), col),
                  pl.BlockSpec((DIFF_HEADS * DIFF_VROWS, tm), col),
                  pl.BlockSpec((SSD_BC_DIM, tm), col)]
    return pl.pallas_call(
        functools.partial(_inproj_kernel, tiles_per_seq=seq // tm, diff_key_tile=diff_key_tile),
        grid=(t // tm,),
        in_specs=[pl.BlockSpec((tm, d), row),
                  pl.BlockSpec((SUBLANES, d), lambda i: (jnp.maximum(i * hb - 1, 0), 0)),
                  pl.BlockSpec((SUBLANES, d), lambda i: (jnp.minimum((i + 1) * hb, n_hblk - 1), 0)),
                  pl.BlockSpec((1, d), full),
                  pl.BlockSpec(w_main.shape, full), pl.BlockSpec(w_t.shape, full),
                  pl.BlockSpec(conv_w.shape, full), pl.BlockSpec(conv_b.shape, full)],
        out_specs=out_specs,
        out_shape=out_shape,
        compiler_params=_cparams("parallel"),
        name="inproj",
    )(x2, x2, x2, norm_w, w_main, w_t, conv_w, conv_b)


def _swa_kernel(sink_ref, slope_ref, qt_ref, k_ref, vt_ref, o_ref):
    step = pl.program_id(1)
    s_len = k_ref.shape[0]
    blk = SWA_WINDOW
    band = 3 * blk
    nb = s_len // blk
    rep = SWA_HEADS // SWA_KV_HEADS
    hd = HEAD_DIM
    blocks_per_step = qt_ref.shape[1] // blk
    for u in range(blocks_per_step):
        n = step * blocks_per_step + u
        start_blk = jnp.clip(n - 1, 0, nb - 3)
        start = pl.multiple_of(start_blk * blk, blk)
        kb = k_ref[pl.ds(start, band), :]
        v3 = vt_ref[pl.ds(start_blk, 3)]
        vtb = jnp.concatenate([v3[0], v3[1], v3[2]], axis=1)
        qt = qt_ref[:, u * blk:(u + 1) * blk]
        zero = jnp.zeros((hd, rep * blk), BF16)
        grp = [jnp.concatenate([qt[(g * rep + r) * hd:(g * rep + r + 1) * hd] for r in range(rep)], axis=1)
               for g in range(SWA_KV_HEADS)]
        qbd = jnp.concatenate([jnp.concatenate([grp[0], zero], axis=1),
                               jnp.concatenate([zero, grp[1]], axis=1)], axis=0)
        st = jnp.dot(kb, qbd, preferred_element_type=F32)
        kpos = start + lax.broadcasted_iota(I32, (band, blk), 0)
        qpos = n * blk + lax.broadcasted_iota(I32, (band, blk), 1)
        dist_i = jnp.abs(qpos - kpos)
        valid = dist_i <= SWA_WINDOW
        dist = dist_i.astype(F32)
        ps, inv = [], []
        for h in range(SWA_HEADS):
            s = jnp.where(valid, st[:, h * blk:(h + 1) * blk] - slope_ref[h] * dist, NEG)
            sink = sink_ref[h]
            m = jnp.maximum(jnp.max(s, axis=0, keepdims=True), sink)
            p = jnp.exp2(s - m)
            inv.append(1.0 / (jnp.sum(p, axis=0, keepdims=True) + jnp.exp2(sink - m)))
            ps.append(p.astype(BF16))
        outs = []
        for g in range(SWA_KV_HEADS):
            pg = jnp.concatenate(ps[g * rep:(g + 1) * rep], axis=1)
            og = jnp.dot(vtb[g * hd:(g + 1) * hd, :], pg, preferred_element_type=F32)
            for r in range(rep):
                outs.append(og[:, r * blk:(r + 1) * blk] * inv[g * rep + r])
        o_ref[u * blk:(u + 1) * blk, :] = jnp.transpose(jnp.concatenate(outs, axis=0)).astype(o_ref.dtype)


def _swa(aqt, ak, avt3, sink, slopes, batch, seq):
    blk = SWA_WINDOW
    rows = min(blk * SWA_BLOCKS_PER_STEP, seq)
    steps = seq // rows
    nb = seq // blk
    t = batch * seq
    smem = pl.BlockSpec(memory_space=pltpu.SMEM)
    return pl.pallas_call(
        _swa_kernel,
        grid=(batch, steps),
        in_specs=[smem, smem,
                  pl.BlockSpec((A_Q_DIM, rows), lambda b, s: (0, b * steps + s)),
                  pl.BlockSpec((seq, A_KV_DIM), lambda b, s: (b, 0)),
                  pl.BlockSpec((nb, A_KV_DIM, blk), lambda b, s: (b, 0, 0))],
        out_specs=pl.BlockSpec((rows, A_Q_DIM), lambda b, s: (b * steps + s, 0)),
        out_shape=jax.ShapeDtypeStruct((t, A_Q_DIM), BF16),
        compiler_params=_cparams("parallel", "parallel"),
        name="swa",
    )(sink, slopes, aqt, ak, avt3)


def _diff_key_tile(i, j, tq, tk, nk):
    return ((i * tq) // tk + j) % nk


def _diff_key_features(pos_in_tile):
    lane = lax.broadcasted_iota(I32, pos_in_tile.shape, 1) % LANES - 2 * DIFF_QK_DIM
    coarse = ((pos_in_tile // 16) * 16).astype(F32)
    fine = (pos_in_tile % 16).astype(F32)
    f = lane % DIFF_NFEAT
    feat = jnp.where(f < 2, coarse, jnp.where(f < 4, fine, 1.0))
    return jnp.where((lane >= 0) & (lane < 2 * DIFF_NFEAT), feat, 0.0)


def _diff_kernel(slope_ref, qt_ref, k_ref, vt_ref, lam_ref, sw_ref, o_ref, qtb_ref, m_ref, acc_ref,
                 s_ref, p_ref, mx_ref, *, lambda_init, n_query_tiles, n_key_tiles):
    t = pl.program_id(1)
    n_pairs = n_query_tiles * n_key_tiles
    tq = qt_ref.shape[1]
    tk = k_ref.shape[0]
    dq = DIFF_QK_DIM
    hw = 2 * dq
    nf = DIFF_NFEAT
    nkt = n_key_tiles

    def reset_stats():
        m_ref[...] = jnp.full(m_ref.shape, NEG, F32)
        acc_ref[...] = jnp.zeros(acc_ref.shape, F32)

    def build_queries():
        ii = lax.broadcasted_iota(I32, (1, 2 * tq), 1)
        ii = jnp.where(ii >= tq, ii - tq, ii).astype(F32)
        qt = qt_ref[...]
        col = lax.broadcasted_iota(I32, (hw, 2 * tq), 1)
        row = lax.broadcasted_iota(I32, (hw, 2 * tq), 0)
        own_map = row // dq == col // tq
        for h in range(DIFF_HEADS):
            qh = qt[h * hw:(h + 1) * hw, :]
            qh2 = jnp.where(own_map, jnp.concatenate([qh, qh], axis=1), jnp.zeros((hw, 2 * tq), BF16))
            sl = jnp.full((1, 2 * tq), slope_ref[h], F32)
            s_hi, s_lo = _bf16_split(sl)
            v_hi, v_lo = _bf16_split(-sl * ii)
            rows = jnp.concatenate([s_hi, s_lo, s_hi, s_lo, v_hi, v_lo], axis=0)
            zrow = jnp.zeros((nf, 2 * tq), F32)
            zero = jnp.zeros((hw - 2 * nf, 2 * tq), F32)
            variants = ([rows, zrow], [zrow, -rows], [zrow, zrow])
            for v, pieces in enumerate(variants):
                qtb_ref[v, h, 0:hw, :] = qh2
                qtb_ref[v, h, hw:2 * hw, :] = jnp.concatenate(pieces + [zero], axis=0).astype(BF16)

    def pair_of(step):
        c = jnp.clip(step, 0, n_pairs - 1)
        return c // nkt, c % nkt

    def key_start(qi, step):
        return _diff_key_tile(qi, step, tq, tk, nkt) * tk

    def score_head(slot, h, variant, dist):
        s = jnp.dot(k_ref[:, h * LANES:(h + 1) * LANES], qtb_ref[variant, h], preferred_element_type=F32)
        if dist is not None:
            s = s - slope_ref[h] * dist
        s_ref[slot, h] = s
        mx_ref[slot, h:h + 1, :] = jnp.max(s, axis=0, keepdims=True)

    def softmax_head(slot, h, shift):
        m_old = m_ref[h:h + 1, :]
        m_new = jnp.maximum(m_old, mx_ref[slot, h:h + 1, :] + shift)
        p_ref[h] = jnp.exp2(s_ref[slot, h] - (m_new - shift)).astype(BF16)
        m_ref[h:h + 1, :] = m_new
        return jnp.exp2(m_old - m_new)

    def value_head(h, alpha):
        pv = jnp.dot(vt_ref[h * DIFF_VROWS:(h + 1) * DIFF_VROWS, :], p_ref[h], preferred_element_type=F32)
        acc_ref[h] = alpha * acc_ref[h] + pv

    def tile_shift(step):
        qi, js = pair_of(step)
        k0 = key_start(qi, js)
        q0 = qi * tq
        sign = jnp.where(js == 0, 0.0, jnp.where(k0 < q0, 1.0, -1.0))
        return sign * (k0 - q0).astype(F32)

    def diagonal_scores(slot):
        qi, _ = pair_of(t)
        kpos = key_start(qi, 0) + lax.broadcasted_iota(I32, (tk, 2 * tq), 0)
        qpos = qi * tq + lax.broadcasted_iota(I32, (tk, 2 * tq), 1) % tq
        dist = jnp.abs(qpos - kpos).astype(F32)
        for h in range(DIFF_HEADS):
            score_head(slot, h, 2, dist)

    def finish_query_tile():
        lp = lam_ref[...]
        lam = (jnp.exp(jnp.sum(lp[0:1] * lp[1:2], axis=-1, keepdims=True))
               - jnp.exp(jnp.sum(lp[2:3] * lp[3:4], axis=-1, keepdims=True)) + lambda_init)
        outs = []
        for h in range(DIFF_HEADS):
            a = acc_ref[h]
            o = a[0:DIFF_V_DIM] / a[DIFF_V_DIM:DIFF_V_DIM + 1]
            o = o[:, 0:tq] - lam * o[:, tq:2 * tq]
            ms = jnp.mean(o * o, axis=0, keepdims=True)
            outs.append(o * lax.rsqrt(ms + NORM_EPS) * sw_ref[...] * (1.0 - lambda_init))
        o_ref[...] = jnp.transpose(jnp.concatenate(outs, axis=0)).astype(o_ref.dtype)

    _, step_in_tile = pair_of(t)
    is_first = (t < n_pairs) & (step_in_tile == 0)

    @pl.when(t == 0)
    def _():
        reset_stats()
        build_queries()
        diagonal_scores(0)

    for parity in range(2):
        @pl.when((t > 0) & (t < n_pairs) & jnp.logical_not(is_first) & (t % 2 == parity))
        def _():
            qi, js = pair_of(t)
            variant = jnp.where(key_start(qi, js) < qi * tq, 0, 1)
            shift = tile_shift(t - 1)
            alphas = []
            for h in range(DIFF_HEADS):
                alphas.append(softmax_head(1 - parity, h, slope_ref[h] * shift))
                score_head(parity, h, variant, None)
                value_head(h, alphas[h])

        if nkt % 2 == 0 and parity == 1:
            continue

        @pl.when((t > 0) & is_first & (t % 2 == parity))
        def _():
            shift = tile_shift(t - 1)
            alphas = [softmax_head(1 - parity, h, slope_ref[h] * shift) for h in range(DIFF_HEADS)]
            build_queries()
            diagonal_scores(parity)
            for h in range(DIFF_HEADS):
                value_head(h, alphas[h])
            finish_query_tile()
            reset_stats()

    @pl.when(t == n_pairs)
    def _():
        shift = tile_shift(t - 1)
        alphas = [softmax_head((n_pairs - 1) % 2, h, slope_ref[h] * shift) for h in range(DIFF_HEADS)]
        for h in range(DIFF_HEADS):
            value_head(h, alphas[h])
        finish_query_tile()


def _diff(cqt, ck, cvt, slopes, lam_params, subln_w_col, lambda_init, batch, seq, tq, tk):
    assert tk % tq == 0 and seq % tk == 0
    nq, nk = seq // tq, seq // tk
    t = batch * seq
    smem = pl.BlockSpec(memory_space=pltpu.SMEM)
    n_pairs = nq * nk

    def query_tile(p):
        return jnp.clip(p, 0, n_pairs - 1) // nk

    def key_tile(p):
        c = jnp.clip(p, 0, n_pairs - 1)
        return _diff_key_tile(c // nk, c % nk, tq, tk, nk)

    return pl.pallas_call(
        functools.partial(_diff_kernel, lambda_init=lambda_init, n_query_tiles=nq, n_key_tiles=nk),
        grid=(batch, n_pairs + 1),
        in_specs=[smem,
                  pl.BlockSpec((DIFF_QK_WIDTH, tq), lambda b, p: (0, b * nq + query_tile(p))),
                  pl.BlockSpec((tk, DIFF_HEADS * LANES), lambda b, p: (b * nk + key_tile(p), 0)),
                  pl.BlockSpec((DIFF_HEADS * DIFF_VROWS, tk), lambda b, p: (0, b * nk + key_tile(p - 1))),
                  pl.BlockSpec(lam_params.shape, lambda b, p: (0, 0)),
                  pl.BlockSpec(subln_w_col.shape, lambda b, p: (0, 0))],
        out_specs=pl.BlockSpec((tq, DIFF_V_WIDTH), lambda b, p: (b * nq + query_tile(p - 1), 0)),
        out_shape=jax.ShapeDtypeStruct((t, DIFF_V_WIDTH), BF16),
        scratch_shapes=[pltpu.VMEM((3, DIFF_HEADS, LANES, 2 * tq), BF16),
                        pltpu.VMEM((DIFF_HEADS, 2 * tq), F32),
                        pltpu.VMEM((DIFF_HEADS, DIFF_VROWS, 2 * tq), F32),
                        pltpu.VMEM((2, DIFF_HEADS, tk, 2 * tq), F32),
                        pltpu.VMEM((DIFF_HEADS, tk, 2 * tq), BF16),
                        pltpu.VMEM((2, DIFF_HEADS, 2 * tq), F32)],
        compiler_params=_cparams("parallel", "arbitrary"),
        name="diffattn",
    )(slopes, cqt, ck, cvt, lam_params, subln_w_col)


def _ssd_direction(fwd, off, xs_ref, bt_ref, cm_ref, dt_ref, dtb_ref, alog_ref, dsk_ref, y_ref, state_ref):
    q = SSD_CHUNK
    rows = slice(off, off + q)
    lane0 = 0 if fwd else SSD_HEADS
    dt_all = _softplus(dt_ref[rows, :] + dtb_ref[...])
    dta_all = dt_all * -jnp.exp(alog_ref[...])
    row = lax.broadcasted_iota(I32, (q, q), 0)
    col = lax.broadcasted_iota(I32, (q, q), 1)
    keep = (row >= col) if fwd else (row <= col)
    tri = keep.astype(BF16)
    part_hi = dta_all.astype(BF16)
    rest = dta_all - part_hi.astype(F32)
    part_mid = rest.astype(BF16)
    part_lo = (rest - part_mid.astype(F32)).astype(BF16)
    da_all = (jnp.dot(tri, part_hi, preferred_element_type=F32)
              + jnp.dot(tri, part_mid, preferred_element_type=F32)
              + jnp.dot(tri, part_lo, preferred_element_type=F32))
    da_all_t = jnp.transpose(da_all)
    dt_all_t = jnp.transpose(dt_all)
    tot_all = jnp.sum(dta_all, axis=0, keepdims=True)

    xs = xs_ref[rows, :]
    bmt = bt_ref[:, rows]
    cm = cm_ref[rows, :]
    rep = SSD_HEADS // SSD_GROUPS
    ns = SSD_STATE
    lane = lax.broadcasted_iota(I32, (q, LANES), 1)
    low = lane < SSD_HEAD_DIM
    low_n = lax.broadcasted_iota(I32, (ns, LANES), 1) < SSD_HEAD_DIM
    cm_g = [jnp.where((lane // ns) == g, cm, 0.0) for g in range(SSD_GROUPS)]
    bmt_b = bmt.astype(BF16)
    g_mats = [jnp.dot(cm_g[g].astype(BF16), bmt_b, preferred_element_type=F32)
              for g in range(SSD_GROUPS)]
    zeros_n = jnp.zeros((ns, LANES), BF16)
    ys = []
    for pair in range(SSD_HEADS // 2):
        x_pair = xs[:, pair * LANES:(pair + 1) * LANES]
        x_pair_b = x_pair.astype(BF16)
        st = state_ref[pair]
        st_b = st.astype(BF16)
        y_heads, s_heads, keep_heads = [], [], []
        for h in (2 * pair, 2 * pair + 1):
            g = h // rep
            ln = lane0 + h
            dac = da_all[:, ln:ln + 1]
            dar = da_all_t[ln:ln + 1, :]
            dtr = dt_all_t[ln:ln + 1, :]
            tot = tot_all[:, ln:ln + 1]
            dac_b = jnp.broadcast_to(dac, (q, q))
            decay = jnp.exp(jnp.where(keep, dac_b - dar, NEG))
            y = jnp.dot((g_mats[g] * decay * dtr).astype(BF16), x_pair_b, preferred_element_type=F32)
            st_ext = jnp.concatenate([st_b, zeros_n] if g == 0 else [zeros_n, st_b], axis=0)
            c_in = (cm_g[g] * jnp.exp(dac_b)).astype(BF16)
            y_heads.append(y + jnp.dot(c_in, st_ext, preferred_element_type=F32))
            to_end = jnp.exp(tot - dar) * dtr
            b_out = (bmt[g * ns:(g + 1) * ns, :] * to_end).astype(BF16)
            s_heads.append(jnp.dot(b_out, x_pair_b, preferred_element_type=F32))
            keep_heads.append(jnp.exp(tot))
        state_keep = jnp.where(low_n, jnp.broadcast_to(keep_heads[0], (ns, LANES)),
                               jnp.broadcast_to(keep_heads[1], (ns, LANES)))
        state_ref[pair] = st * state_keep + jnp.where(low_n, s_heads[0], s_heads[1])
        y = jnp.where(low, y_heads[0], y_heads[1])
        if fwd:
            y = y + dsk_ref[:, pair * LANES:(pair + 1) * LANES] * x_pair
        ys.append(y)
    y_ref[rows, :] = jnp.concatenate(ys, axis=-1)


def _ssd_kernel(xsf_ref, btf_ref, cmf_ref, dtf_ref, xsb_ref, btb_ref, cmb_ref, dtb_in_ref,
                dtbias_ref, alog_ref, dsk_ref, yf_ref, yb_ref, state_ref):
    @pl.when(pl.program_id(1) == 0)
    def _():
        state_ref[...] = jnp.zeros(state_ref.shape, F32)

    n = SSD_CHUNKS_PER_STEP
    for u in range(n):
        _ssd_direction(True, u * SSD_CHUNK, xsf_ref, btf_ref, cmf_ref, dtf_ref, dtbias_ref, alog_ref, dsk_ref,
                       yf_ref, state_ref.at[0])
        _ssd_direction(False, (n - 1 - u) * SSD_CHUNK, xsb_ref, btb_ref, cmb_ref, dtb_in_ref, dtbias_ref,
                       alog_ref, dsk_ref, yb_ref, state_ref.at[1])


def _ssd(xs, bt, cm, dt_raw, dt_bias, a_log, d_skip, batch, seq):
    q = SSD_CHUNK * SSD_CHUNKS_PER_STEP
    nc = seq // q
    t = batch * seq
    full = lambda b, c: (0, 0)
    fw = lambda b, c: b * nc + c
    bw = lambda b, c: b * nc + nc - 1 - c

    def specs(idx):
        return [pl.BlockSpec((q, SSD_D_INNER), lambda b, c: (idx(b, c), 0)),
                pl.BlockSpec((SSD_BC_DIM, q), lambda b, c: (0, idx(b, c))),
                pl.BlockSpec((q, SSD_BC_DIM), lambda b, c: (idx(b, c), 0)),
                pl.BlockSpec((q, LANES), lambda b, c: (idx(b, c), 0))]

    return pl.pallas_call(
        _ssd_kernel,
        grid=(batch, nc),
        in_specs=specs(fw) + specs(bw) + [pl.BlockSpec(dt_bias.shape, full), pl.BlockSpec(a_log.shape, full),
                                          pl.BlockSpec(d_skip.shape, full)],
        out_specs=[pl.BlockSpec((q, SSD_D_INNER), lambda b, c: (fw(b, c), 0)),
                   pl.BlockSpec((q, SSD_D_INNER), lambda b, c: (bw(b, c), 0))],
        out_shape=[jax.ShapeDtypeStruct((t, SSD_D_INNER), F32), jax.ShapeDtypeStruct((t, SSD_D_INNER), F32)],
        scratch_shapes=[pltpu.VMEM((2, SSD_HEADS // 2, SSD_STATE, 2 * SSD_HEAD_DIM), F32)],
        compiler_params=_cparams("parallel", "arbitrary"),
        name="ssd",
    )(xs, bt, cm, dt_raw, xs, bt, cm, dt_raw, dt_bias, a_log, d_skip)


def _outproj_kernel(x_ref, ya_ref, yf_ref, yb_ref, z_ref, snw_ref, yc_ref, wo_ref, fnw_ref, wrt_ref, br_ref,
                    xn_ref, ri_ref, rf_ref, cnt_ref, tri_ref, carry_ref):
    step = pl.program_id(0)
    tm = x_ref.shape[0]

    @pl.when(step == 0)
    def _():
        carry_ref[...] = jnp.zeros(carry_ref.shape, F32)
        r = lax.broadcasted_iota(I32, (tm, tm), 0)
        cc = lax.broadcasted_iota(I32, (tm, tm), 1)
        tri_ref[...] = (r <= cc).astype(BF16)

    y = (yf_ref[...] + yb_ref[...]) * _silu(z_ref[...])
    yb = _rms(y, snw_ref[...]).astype(BF16)
    acc = jnp.dot(ya_ref[...], wo_ref[0:A_Q_DIM, :], preferred_element_type=F32)
    acc = acc + jnp.dot(yb, wo_ref[A_Q_DIM:A_Q_DIM + SSD_D_INNER, :], preferred_element_type=F32)
    acc = acc + jnp.dot(yc_ref[...], wo_ref[A_Q_DIM + SSD_D_INNER:, :], preferred_element_type=F32)
    xn = x_ref[...] + acc
    xn_ref[...] = xn

    h = _rms(xn, fnw_ref[...])
    h_hi = h.astype(BF16)
    h_lo = (h - h_hi.astype(F32)).astype(BF16)
    nt = (((1,), (1,)), ((), ()))
    logits = (lax.dot_general(wrt_ref[0], h_hi, nt, preferred_element_type=F32)
              + lax.dot_general(wrt_ref[0], h_lo, nt, preferred_element_type=F32)
              + lax.dot_general(wrt_ref[1], h_hi, nt, preferred_element_type=F32)) + br_ref[...]
    ne, epg, ng = N_EXPERTS, EXPERTS_PER_GROUP, N_EXPERT_GROUPS
    gl = logits[ne:ne + ng, :]
    gmax = jnp.max(gl, axis=0, keepdims=True)
    g_sel = jnp.full((1, tm), float(ng - 1), F32)
    for g in range(ng - 2, -1, -1):
        g_sel = jnp.where(gl[g:g + 1, :] == gmax, float(g), g_sel)
    g_gate = 1.0 / jnp.sum(jnp.exp(gl - gmax), axis=0, keepdims=True)
    e_in = logits[0:epg, :]
    for g in range(1, ng):
        e_in = jnp.where(g_sel == float(g), logits[g * epg:(g + 1) * epg, :], e_in)
    sub = lax.broadcasted_iota(I32, (epg, tm), 0).astype(F32)
    m1 = jnp.max(e_in, axis=0, keepdims=True)
    i1 = jnp.min(jnp.where(e_in == m1, sub, float(epg)), axis=0, keepdims=True)
    rest = jnp.where(sub == i1, NEG, e_in)
    m2 = jnp.max(rest, axis=0, keepdims=True)
    i2 = jnp.min(jnp.where(rest == m2, sub, float(epg)), axis=0, keepdims=True)
    r = jnp.exp(m2 - m1)
    c1 = g_gate / (1.0 + r)
    c2 = g_gate * r / (1.0 + r)
    e1 = (g_sel * epg + i1).astype(I32)
    e2 = (g_sel * epg + i2).astype(I32)

    erow = lax.broadcasted_iota(I32, (ne, tm), 0)
    hit1 = erow == e1
    hit2 = erow == e2
    oh = jnp.where(hit1 | hit2, 1.0, 0.0)
    incl = jnp.dot(oh.astype(BF16), tri_ref[...], preferred_element_type=F32)
    before = incl - oh + carry_ref[:, 0:1]
    rank1 = jnp.sum(jnp.where(hit1, before, 0.0), axis=0, keepdims=True)
    rank2 = jnp.sum(jnp.where(hit2, before, 0.0), axis=0, keepdims=True)
    carry_ref[...] = carry_ref[...] + jnp.sum(oh, axis=1, keepdims=True)
    cnt_ref[...] = carry_ref[...]
    zi = jnp.zeros((1, tm), I32)
    ri_ref[...] = jnp.concatenate([e1, e2, rank1.astype(I32), rank2.astype(I32), zi, zi, zi, zi], axis=0)
    zf = jnp.zeros((1, tm), F32)
    rf_ref[...] = jnp.concatenate([c1, c2, zf, zf, zf, zf, zf, zf], axis=0)


def _outproj(x2, ya, yf, yb, z, ssd_norm_w, yc, w_out, ffn_norm_w, wrt, br, tm):
    t, d = x2.shape
    row = lambda i: (i, 0)
    full = lambda i: (0, 0)
    return pl.pallas_call(
        _outproj_kernel,
        grid=(t // tm,),
        in_specs=[pl.BlockSpec((tm, d), row),
                  pl.BlockSpec((tm, A_Q_DIM), row),
                  pl.BlockSpec((tm, SSD_D_INNER), row),
                  pl.BlockSpec((tm, SSD_D_INNER), row),
                  pl.BlockSpec((tm, SSD_D_INNER), row),
                  pl.BlockSpec(ssd_norm_w.shape, full),
                  pl.BlockSpec((tm, DIFF_V_WIDTH), row),
                  pl.BlockSpec(w_out.shape, full),
                  pl.BlockSpec(ffn_norm_w.shape, full),
                  pl.BlockSpec(wrt.shape, lambda i: (0, 0, 0)),
                  pl.BlockSpec(br.shape, full)],
        out_specs=[pl.BlockSpec((tm, d), row),
                   pl.BlockSpec((SUBLANES, tm), lambda i: (0, i)),
                   pl.BlockSpec((SUBLANES, tm), lambda i: (0, i)),
                   pl.BlockSpec((N_EXPERTS, LANES), full)],
        out_shape=[jax.ShapeDtypeStruct((t, d), F32),
                   jax.ShapeDtypeStruct((SUBLANES, t), I32),
                   jax.ShapeDtypeStruct((SUBLANES, t), F32),
                   jax.ShapeDtypeStruct((N_EXPERTS, LANES), F32)],
        scratch_shapes=[pltpu.VMEM((tm, tm), BF16), pltpu.VMEM((N_EXPERTS, LANES), F32)],
        compiler_params=_cparams("arbitrary"),
        name="outproj_router",
    )(x2, ya, yf, yb, z, ssd_norm_w, yc, w_out, ffn_norm_w, wrt, br)


_PAD_PIECES = tuple(1 << b for b in reversed(range(MOE_ROW_TILE.bit_length() - 1)))


def _dispatch_kernel(slot1_ref, slot2_ref, pstart_ref, plen_ref, nused_ref, x_ref, xs_hbm, zero_ref, xbuf, sem):
    i = pl.program_id(0)
    tm = x_ref.shape[0]

    @pl.when(i == 0)
    def _():
        zero_ref[...] = jnp.zeros(zero_ref.shape, F32)

        def pieces(e, wait):
            n = plen_ref[e]
            first = pstart_ref[e]
            off = first + n
            for b in _PAD_PIECES:
                off = off - (n & b)
                dst = pl.ds(pl.multiple_of(off, b), b) if b >= SUBLANES else None
                if dst is not None:
                    @pl.when((n & b) != 0)
                    def _():
                        cp = pltpu.make_async_copy(zero_ref.at[pl.ds(0, b)], xs_hbm.at[dst], sem.at[2])
                        cp.wait() if wait else cp.start()

            for u in range(SUBLANES - 1):
                @pl.when(u < (n & (SUBLANES - 1)))
                def _():
                    cp = pltpu.make_async_copy(zero_ref.at[pl.ds(0, 1)], xs_hbm.at[pl.ds(first + u, 1)],
                                               sem.at[2])
                    cp.wait() if wait else cp.start()

        def tail(tile, wait):
            big = _PAD_PIECES[0]
            for part in range(MOE_ROW_TILE // big):
                dst = xs_hbm.at[pl.ds(pl.multiple_of(tile * MOE_ROW_TILE + part * big, big), big)]
                cp = pltpu.make_async_copy(zero_ref, dst, sem.at[2])
                cp.wait() if wait else cp.start()

        def loop(fn, lo, hi, wait):
            def body(k, carry):
                fn(k, wait)
                return carry

            lax.fori_loop(lo, hi, body, 0)

        n_tiles = xs_hbm.shape[0] // MOE_ROW_TILE
        for wait in (False, True):
            loop(pieces, 0, N_EXPERTS, wait)
            loop(tail, nused_ref[0], n_tiles, wait)

    base = i * tm
    last = pl.num_programs(0) - 1

    def drain(slot):
        for _ in range(2):
            pltpu.make_async_copy(xbuf.at[slot], xs_hbm.at[pl.ds(0, tm)], sem.at[slot]).wait()

    for parity in range(2):
        @pl.when(i % 2 == parity)
        def _():
            buf = xbuf.at[parity]
            buf[...] = x_ref[...]
            for r in range(tm):
                src = buf.at[pl.ds(r, 1)]
                pltpu.make_async_copy(src, xs_hbm.at[pl.ds(slot1_ref[base + r], 1)],
                                      sem.at[parity]).start(priority=0)
                pltpu.make_async_copy(src, xs_hbm.at[pl.ds(slot2_ref[base + r], 1)],
                                      sem.at[parity]).start(priority=1)

            @pl.when(i > 0)
            def _():
                drain(1 - parity)

            @pl.when(i == last)
            def _():
                drain(parity)


def _dispatch(xn, slot1, slot2, pad_start, pad_len, n_used, n_rows, tm):
    t, d = xn.shape
    grid_spec = pltpu.PrefetchScalarGridSpec(
        num_scalar_prefetch=5,
        grid=(t // tm,),
        in_specs=[pl.BlockSpec((tm, d), lambda i, s1, s2, ps, pn, nu: (i, 0))],
        out_specs=pl.BlockSpec(memory_space=pl.ANY),
        scratch_shapes=[pltpu.VMEM((_PAD_PIECES[0], d), F32), pltpu.VMEM((2, tm, d), F32),
                        pltpu.SemaphoreType.DMA((3,))],
    )
    return pl.pallas_call(
        _dispatch_kernel,
        grid_spec=grid_spec,
        out_shape=jax.ShapeDtypeStruct((n_rows, d), F32),
        compiler_params=_cparams("arbitrary"),
        name="moe_dispatch",
    )(slot1, slot2, pad_start, pad_len, n_used, xn)


def _moe_kernel(texp_ref, nused_ref, first_ref, wslot_ref, next_ref, x_ref, fnw_ref, wg_hbm, wu_hbm, wd_hbm,
                y_ref, wg_buf, wu_buf, wd_buf, sem, *, layer):
    i = pl.program_id(0)

    def weight_copies(expert, slot):
        return [pltpu.make_async_copy(hbm.at[layer, expert], buf.at[slot], sem.at[slot])
                for hbm, buf in ((wg_hbm, wg_buf), (wu_hbm, wu_buf), (wd_hbm, wd_buf))]

    @pl.when(i < nused_ref[0])
    def _():
        slot = wslot_ref[i]

        @pl.when(i == 0)
        def _():
            for cp in weight_copies(texp_ref[0], 0):
                cp.start()

        @pl.when(first_ref[i] == 1)
        def _():
            for cp in weight_copies(texp_ref[i], slot):
                cp.wait()

            @pl.when(next_ref[i] >= 0)
            def _():
                for cp in weight_copies(next_ref[i], 1 - slot):
                    cp.start()

        h = _rms(x_ref[...], fnw_ref[...]).astype(BF16)
        hg = jnp.dot(h, wg_buf[slot].astype(BF16), preferred_element_type=F32)
        hu = jnp.dot(h, wu_buf[slot].astype(BF16), preferred_element_type=F32)
        act = (_silu(hg) * hu).astype(BF16)
        y_ref[...] = jnp.dot(act, wd_buf[slot].astype(BF16), preferred_element_type=F32)

    @pl.when(i >= nused_ref[0])
    def _():
        y_ref[...] = jnp.zeros(y_ref.shape, F32)


def _moe(xs, ffn_norm_w, w_gate, w_up, w_down, layer, tile_expert, n_used):
    n_rows, d = xs.shape
    f = w_gate.shape[-1]
    tr = MOE_ROW_TILE

    n_tiles = n_rows // tr

    idx = jnp.arange(n_tiles, dtype=I32)
    prev_expert = jnp.concatenate([jnp.full((1,), -1, I32), tile_expert[:-1]])
    first = ((idx < n_used[0]) & (tile_expert != prev_expert)).astype(I32)
    wslot = ((jnp.cumsum(first) - 1) % 2).astype(I32)
    first_pos = jnp.where(first == 1, idx, n_tiles)
    next_first = jnp.concatenate([lax.cummin(first_pos, reverse=True)[1:], jnp.full((1,), n_tiles, I32)])
    next_expert = jnp.where(next_first < n_tiles, tile_expert[jnp.minimum(next_first, n_tiles - 1)], -1).astype(I32)

    def used(i, nu):
        return jnp.maximum(jnp.minimum(i, nu[0] - 1), 0)

    anyspace = pl.BlockSpec(memory_space=pl.ANY)
    grid_spec = pltpu.PrefetchScalarGridSpec(
        num_scalar_prefetch=5,
        grid=(n_tiles,),
        in_specs=[pl.BlockSpec((tr, d), lambda i, te, nu, fi, ws, nx: (used(i, nu), 0)),
                  pl.BlockSpec(ffn_norm_w.shape, lambda i, te, nu, fi, ws, nx: (0, 0)),
                  anyspace, anyspace, anyspace],
        out_specs=pl.BlockSpec((tr, d), lambda i, te, nu, fi, ws, nx: (i, 0)),
        scratch_shapes=[pltpu.VMEM((2, d, f), F32), pltpu.VMEM((2, d, f), F32), pltpu.VMEM((2, f, d), F32),
                        pltpu.SemaphoreType.DMA((2,))],
    )
    return pl.pallas_call(
        functools.partial(_moe_kernel, layer=layer),
        grid_spec=grid_spec,
        out_shape=jax.ShapeDtypeStruct((n_rows, d), F32),
        compiler_params=_cparams("arbitrary"),
        name="moe_experts",
    )(tile_expert, n_used, first, wslot, next_expert, xs, ffn_norm_w, w_gate, w_up, w_down)


def _combine_kernel(slot1_ref, slot2_ref, x_ref, cw_ref, nw_ref, y_hbm, o_ref, ybuf, sem, *, final_norm):
    i = pl.program_id(0)
    n = pl.num_programs(0)
    tm = x_ref.shape[0]

    def start_gather(tile, slot):
        base = tile * tm
        for r in range(tm):
            pltpu.make_async_copy(y_hbm.at[pl.ds(slot1_ref[base + r], 1)], ybuf.at[slot, 0, pl.ds(r, 1)],
                                  sem.at[slot]).start(priority=0)
            pltpu.make_async_copy(y_hbm.at[pl.ds(slot2_ref[base + r], 1)], ybuf.at[slot, 1, pl.ds(r, 1)],
                                  sem.at[slot]).start(priority=1)

    def compute(slot):
        for k in range(2):
            pltpu.make_async_copy(y_hbm.at[pl.ds(0, tm)], ybuf.at[slot, k], sem.at[slot]).wait()
        cw = cw_ref[...]
        out = x_ref[...] + cw[:, 0:1] * ybuf[slot, 0] + cw[:, 1:2] * ybuf[slot, 1]
        if final_norm:
            out = _rms(out, nw_ref[...])
        o_ref[...] = out

    @pl.when(i == 0)
    def _():
        start_gather(0, 0)

    for parity in range(2):
        @pl.when(i % 2 == parity)
        def _():
            @pl.when(i + 1 < n)
            def _():
                start_gather(i + 1, 1 - parity)

            compute(parity)


def _combine(xn, cw, norm_w, y_sorted, slot1, slot2, tm, final_norm):
    t, d = xn.shape
    grid_spec = pltpu.PrefetchScalarGridSpec(
        num_scalar_prefetch=2,
        grid=(t // tm,),
        in_specs=[pl.BlockSpec((tm, d), lambda i, s1, s2: (i, 0)),
                  pl.BlockSpec((tm, cw.shape[1]), lambda i, s1, s2: (i, 0)),
                  pl.BlockSpec(norm_w.shape, lambda i, s1, s2: (0, 0)),
                  pl.BlockSpec(memory_space=pl.ANY)],
        out_specs=pl.BlockSpec((tm, d), lambda i, s1, s2: (i, 0)),
        scratch_shapes=[pltpu.VMEM((2, 2, tm, d), F32), pltpu.SemaphoreType.DMA((2,))],
    )
    return pl.pallas_call(
        functools.partial(_combine_kernel, final_norm=final_norm),
        grid_spec=grid_spec,
        out_shape=jax.ShapeDtypeStruct((t, d), F32),
        compiler_params=_cparams("arbitrary"),
        name="moe_combine",
    )(slot1, slot2, xn, cw, norm_w, y_sorted)


def _pad_lanes(v):
    v = v.reshape(1, -1).astype(F32)
    return jnp.pad(v, ((0, 0), (0, LANES - v.shape[1])))


def kernel(x, attn_norm_w, w_in, swa_sink, ssd_conv_w, ssd_conv_b, ssd_dt_bias, ssd_a_log, ssd_d, ssd_norm_w,
           diff_lambda, diff_subln_w, w_out, ffn_norm_w, w_router_group, b_router_group, w_router_expert,
           b_router_expert, w_gate, w_up, w_down, final_norm_w):
    return _forward(x, attn_norm_w, w_in, swa_sink, ssd_conv_w, ssd_conv_b, ssd_dt_bias, ssd_a_log, ssd_d,
                    ssd_norm_w, diff_lambda, diff_subln_w, w_out, ffn_norm_w, w_router_group, b_router_group,
                    w_router_expert, b_router_expert, w_gate, w_up, w_down, final_norm_w)


def _forward(x, attn_norm_w, w_in, swa_sink, ssd_conv_w, ssd_conv_b, ssd_dt_bias, ssd_a_log, ssd_d, ssd_norm_w,
             diff_lambda, diff_subln_w, w_out, ffn_norm_w, w_router_group, b_router_group, w_router_expert,
             b_router_expert, w_gate, w_up, w_down, final_norm_w, tm=512, tq=512, tk=1024, tmc=512):
    batch, seq, d = x.shape
    depth = w_in.shape[0]
    t = batch * seq
    tr = MOE_ROW_TILE
    n_tiles = (2 * t) // tr + N_EXPERTS
    slopes = jnp.exp2(-8.0 * jnp.arange(1, N_ALIBI_HEADS + 1, dtype=F32) / N_ALIBI_HEADS)
    swa_slopes, diff_slopes = slopes[:SWA_HEADS], slopes[SWA_HEADS:]

    sizes = [A_Q_DIM, A_KV_DIM, A_KV_DIM, SSD_D_INNER, SSD_CONV_DIM, SSD_DT_DIM, DIFF_QK_WIDTH, DIFF_QK_WIDTH,
             DIFF_V_WIDTH]
    offs = [0]
    for s in sizes:
        offs.append(offs[-1] + s)
    o_aq, o_ak, o_av, o_z, o_xbc, o_dt, o_cq, o_ck, o_cv, o_end = offs

    x2 = x.reshape(t, d)
    for l in range(depth):
        w = w_in[l]
        hw = 2 * DIFF_QK_DIM
        w_ck = jnp.pad(w[:, o_ck:o_cv].reshape(d, DIFF_HEADS, hw), ((0, 0), (0, 0), (0, LANES - hw)))
        w_main = jnp.concatenate(
            [w[:, o_ak:o_av], w[:, o_z:o_dt], w_ck.reshape(d, DIFF_HEADS * LANES), w[:, o_dt:o_cq],
             jnp.zeros((d, LANES - SSD_DT_DIM), w.dtype)], axis=1).astype(BF16)
        w_t = jnp.concatenate([w[:, o_aq:o_ak], w[:, o_av:o_z], w[:, o_cq:o_ck], w[:, o_cv:o_end]],
                              axis=1).T.astype(BF16)
        ak, z, xs, cm, ck, dt_raw, aqt, avt3, cqt, cvt, bt = _inproj(
            x2, attn_norm_w[l].reshape(1, d), w_main, w_t, ssd_conv_w[l].astype(F32),
            ssd_conv_b[l].reshape(1, -1).astype(F32), tm, seq, tk)

        ya = _swa(aqt, ak, avt3, swa_sink[l].astype(F32) * LOG2E, swa_slopes * LOG2E, batch, seq)
        lambda_init = 0.8 - 0.6 * math.exp(-0.3 * l)
        yc = _diff(cqt, ck, cvt, diff_slopes * LOG2E, diff_lambda[l].astype(F32),
                   diff_subln_w[l].reshape(DIFF_V_DIM, 1).astype(F32), lambda_init, batch, seq, tq, tk)
        yf, yb = _ssd(xs, bt, cm, dt_raw, _pad_lanes(ssd_dt_bias[l]), _pad_lanes(ssd_a_log[l]),
                      jnp.repeat(ssd_d[l].astype(F32), SSD_HEAD_DIM).reshape(1, SSD_D_INNER), batch, seq)

        wr32 = jnp.concatenate([w_router_expert[l], w_router_group[l],
                                jnp.zeros((d, SUBLANES - N_EXPERT_GROUPS), F32)], axis=1).T.astype(F32)
        wr_hi = wr32.astype(BF16)
        wrt = jnp.stack([wr_hi, (wr32 - wr_hi.astype(F32)).astype(BF16)])
        br = jnp.concatenate([b_router_expert[l], b_router_group[l],
                              jnp.zeros((SUBLANES - N_EXPERT_GROUPS,), F32)]).reshape(-1, 1).astype(F32)
        xn, ri, rf, cnt = _outproj(x2, ya, yf, yb, z, ssd_norm_w[l].reshape(1, -1), yc, w_out[l].astype(BF16),
                                   ffn_norm_w[l].reshape(1, d), wrt, br, tm)

        counts = cnt[:, 0].astype(I32)
        padded = ((counts + tr - 1) // tr) * tr
        ends = jnp.cumsum(padded)
        starts = ends - padded
        experts = jnp.arange(N_EXPERTS, dtype=I32)[:, None]

        def slot_of(e, rank):
            return jnp.sum(jnp.where(e[None, :] == experts, starts[:, None], 0), axis=0) + rank

        slot1 = slot_of(ri[0], ri[2])
        slot2 = slot_of(ri[1], ri[3])
        tile_start = jnp.arange(n_tiles, dtype=I32) * tr
        tile_expert = jnp.minimum(jnp.sum(ends[None, :] <= tile_start[:, None], axis=1), N_EXPERTS - 1).astype(I32)
        n_used = (ends[-1] // tr).astype(I32).reshape(1)

        xs_sorted = _dispatch(xn, slot1, slot2, starts + counts, padded - counts, n_used, n_tiles * tr, tmc)
        y_sorted = _moe(xs_sorted, ffn_norm_w[l].reshape(1, d), w_gate, w_up, w_down, l, tile_expert, n_used)
        last = l == depth - 1
        x2 = _combine(xn, rf.T, final_norm_w.reshape(1, d), y_sorted, slot1, slot2, tmc, last)
    return x2.reshape(batch, seq, d)
```
